```python
import math
import jax
import jax.numpy as jnp
from jax import lax
import numpy as np

D_MODEL = 1024
BATCH = 8
SEQ = 2048
DEPTH = 1

HEAD_DIM = 64
MOBA_HEADS = 6
RWKV_HEADS = 6
MEM_HEADS = 4
MOBA_W = MOBA_HEADS * HEAD_DIM
RWKV_W = RWKV_HEADS * HEAD_DIM
MEM_W = MEM_HEADS * HEAD_DIM
MIX_W = MOBA_W + RWKV_W + MEM_W
MOBA_BLOCK = 256
MOBA_TOPK = 3
MOBA_QCHUNK = 64
N_BUCKETS = 32
MAX_DISTANCE = 128
DECAY_LORA = 64
AAA_LORA = 64
GATE_LORA = 128
RWKV_COLS = 3 * RWKV_W + DECAY_LORA + AAA_LORA + GATE_LORA
RWKV_GN_EPS = 64e-5
N_MEM = 256
IN_COLS = 3 * MOBA_W + RWKV_COLS + MEM_W
N_EXPERTS = 32
TOP_K = 4
D_EXPERT = D_MODEL
SWIGLU_ALPHA = 1.702
SWIGLU_LIMIT = 7.0
MOE_BLOCK = 128
RMS_EPS = 1e-5

kernel_name = 'hymba_moba_rwkv7_memxattn_moe'


def rmsnorm(x, g):
    xf = x.astype(jnp.float32)
    y = xf * lax.rsqrt(jnp.mean(xf * xf, axis=-1, keepdims=True) + RMS_EPS)
    return (y * g.astype(jnp.float32)).astype(x.dtype)


def t5_bucket(dist):
    n = jnp.maximum(dist, 0)
    max_exact = N_BUCKETS // 2
    nf = jnp.maximum(n, 1).astype(jnp.float32)
    large = max_exact + (jnp.log(nf / max_exact) / math.log(MAX_DISTANCE / max_exact)
                         * (N_BUCKETS - max_exact)).astype(jnp.int32)
    large = jnp.minimum(large, N_BUCKETS - 1)
    return jnp.where(n < max_exact, n, large)


def moba_attention(q, k, v, rel_bias):
    B, S, H, dh = q.shape
    nb = -(-S // MOBA_BLOCK)
    topk = min(MOBA_TOPK, nb)
    nc = S // MOBA_QCHUNK
    scale = dh ** -0.5

    def blocks(t):
        t = jnp.pad(t, ((0, 0), (0, nb * MOBA_BLOCK - S), (0, 0), (0, 0)))
        return t.transpose(0, 2, 1, 3).reshape(B, H, nb, MOBA_BLOCK, dh)

    kb, vb = blocks(k), blocks(v)
    qh = q.transpose(0, 2, 1, 3)
    k_mean = jnp.mean(kb, axis=3)
    q_blk = jnp.arange(S) // MOBA_BLOCK
    past = jnp.arange(nb)[None, :] < q_blk[:, None]
    gate = jnp.einsum('bhsd,bhnd->bhsn', qh, k_mean).astype(jnp.float32)
    gate = jnp.where(past, gate, -jnp.inf)
    _, sel = lax.top_k(gate, topk)
    valid = sel < q_blk[:, None]
    bias_h = rel_bias.astype(jnp.float32).T
    b_idx = jnp.arange(B)[:, None, None, None]
    h_idx = jnp.arange(H)[None, :, None, None]
    offs = jnp.arange(MOBA_BLOCK)

    def chunks(t):
        t = t.reshape(t.shape[:2] + (nc, MOBA_QCHUNK) + t.shape[3:])
        return jnp.moveaxis(t, 2, 0)

    def attend_chunk(args):
        qc, selc, validc, ci = args
        q_pos = ci * MOBA_QCHUNK + jnp.arange(MOBA_QCHUNK)
        own = (ci * MOBA_QCHUNK) // MOBA_BLOCK
        kg = kb[b_idx, h_idx, selc]
        vg = vb[b_idx, h_idx, selc]
        kpos_g = selc[..., None] * MOBA_BLOCK + offs
        bias_g = bias_h[h_idx[..., None], t5_bucket(q_pos[:, None, None] - kpos_g)]
        s_g = jnp.einsum('bhqd,bhqkjd->bhqkj', qc, kg).astype(jnp.float32) * scale + bias_g
        s_g = jnp.where(validc[..., None], s_g, -jnp.inf)
        k_own = lax.dynamic_index_in_dim(kb, own, axis=2, keepdims=False)
        v_own = lax.dynamic_index_in_dim(vb, own, axis=2, keepdims=False)
        kpos_own = own * MOBA_BLOCK + offs
        bias_own = bias_h[:, t5_bucket(q_pos[:, None] - kpos_own[None, :])]
        s_own = jnp.einsum('bhqd,bhjd->bhqj', qc, k_own).astype(jnp.float32) * scale + bias_own
        s_own = jnp.where(kpos_own[None, :] <= q_pos[:, None], s_own, -jnp.inf)
        logits = jnp.concatenate([s_g.reshape(B, H, MOBA_QCHUNK, topk * MOBA_BLOCK), s_own], axis=-1)
        p = jax.nn.softmax(logits, axis=-1).astype(vb.dtype)
        p_g = p[..., :topk * MOBA_BLOCK].reshape(B, H, MOBA_QCHUNK, topk, MOBA_BLOCK)
        p_own = p[..., topk * MOBA_BLOCK:]
        return (jnp.einsum('bhqkj,bhqkjd->bhqd', p_g, vg)
                + jnp.einsum('bhqj,bhjd->bhqd', p_own, v_own))

    out = lax.map(attend_chunk, (chunks(qh), chunks(sel), chunks(valid), jnp.arange(nc)))
    out = jnp.moveaxis(out, 0, 2).reshape(B, H, S, dh)
    return out.transpose(0, 2, 1, 3).reshape(B, S, H * dh)


def token_shift(p):
    return jnp.pad(p, ((0, 0), (1, 0), (0, 0)))[:, :-1]


def rwkv7_time_mix(p, mu, w0, w_up, a0, a_up, g_up, k_k, k_a, r_k, gn_g, gn_b):
    B, S, _ = p.shape
    H, N = RWKV_HEADS, HEAD_DIM
    f32 = jnp.float32
    p = p + (token_shift(p) - p) * mu
    r, k, v, xw, xa, xg = jnp.split(
        p, [RWKV_W, 2 * RWKV_W, 3 * RWKV_W, 3 * RWKV_W + DECAY_LORA,
            3 * RWKV_W + DECAY_LORA + AAA_LORA], axis=-1)
    w = -jax.nn.softplus(-(w0 + jnp.tanh(xw) @ w_up)) - 0.5
    decay = jnp.exp(-jnp.exp(w.astype(f32)))
    a = jax.nn.sigmoid(a0 + xa @ a_up)
    g = jax.nn.sigmoid(xg) @ g_up
    heads = lambda t: t.astype(f32).reshape(B, S, H, N)
    kk = heads(k * k_k)
    kk = kk / jnp.maximum(jnp.sqrt(jnp.sum(kk * kk, axis=-1, keepdims=True)), 1e-12)
    k = k * (1.0 + (a - 1.0) * k_a)
    r, k, v, decay, a = heads(r), heads(k), heads(v), heads(decay), heads(a)

    def step(state, inp):
        r_t, w_t, k_t, v_t, a_t, b_t = inp
        sa = jnp.einsum('bhij,bhj->bhi', state, a_t)
        state = (state * w_t[:, :, None, :] + sa[..., None] * b_t[:, :, None, :]
                 + v_t[..., None] * k_t[:, :, None, :])
        return state, jnp.einsum('bhij,bhj->bhi', state, r_t)

    seq_first = lambda t: jnp.moveaxis(t, 1, 0)
    xs = tuple(seq_first(t) for t in (r, decay, k, v, -kk, kk * a))
    _, o = lax.scan(step, jnp.zeros((B, H, N, N), f32), xs)
    o = seq_first(o)
    mean = jnp.mean(o, axis=-1, keepdims=True)
    var = jnp.mean(jnp.square(o - mean), axis=-1, keepdims=True)
    o = (o - mean) * lax.rsqrt(var + RWKV_GN_EPS)
    o = o.reshape(B, S, H * N) * gn_g.astype(f32) + gn_b.astype(f32)
    o = o + (jnp.sum(r * k * r_k.astype(f32), axis=-1, keepdims=True) * v).reshape(B, S, H * N)
    return (o * g.astype(f32)).astype(p.dtype)


def memory_attention(q, mk, mv):
    scale = q.shape[-1] ** -0.5
    s = jnp.einsum('bshd,bmhd->bhsm', q, mk).astype(jnp.float32) * scale
    p = jax.nn.softmax(s, axis=-1).astype(mv.dtype)
    o = jnp.einsum('bhsm,bmhd->bshd', p, mv)
    return o.reshape(q.shape[0], q.shape[1], -1)


def moe_ffn(h, w_router, b_router, w_gate_up, b_gate_up, w_down, b_down):
    B, S, D = h.shape
    T = B * S
    A = T * TOP_K
    xt = h.reshape(T, D)
    logits = (xt @ w_router + b_router).astype(jnp.float32)
    top_val, top_idx = lax.top_k(logits, TOP_K)
    top_w = jax.nn.softmax(top_val, axis=-1)
    e_flat = top_idx.reshape(A).astype(jnp.int32)
    tok_flat = jnp.arange(A, dtype=jnp.int32) // TOP_K
    w_flat = top_w.reshape(A)
    e_sorted, order = lax.sort((e_flat, jnp.arange(A, dtype=jnp.int32)), num_keys=1, is_stable=True)
    counts = jnp.bincount(e_flat, length=N_EXPERTS)
    starts = jnp.cumsum(counts) - counts
    padded = (counts + MOE_BLOCK - 1) // MOE_BLOCK * MOE_BLOCK
    pad_ends = jnp.cumsum(padded)
    pad_starts = pad_ends - padded
    dest = pad_starts[e_sorted] + (jnp.arange(A, dtype=jnp.int32) - starts[e_sorted])
    n_blocks = -(-A // MOE_BLOCK) + N_EXPERTS
    P = n_blocks * MOE_BLOCK
    row_tok = jnp.zeros((P,), jnp.int32).at[dest].set(tok_flat[order])
    row_w = jnp.zeros((P,), jnp.float32).at[dest].set(w_flat[order])
    blk_start = jnp.arange(n_blocks, dtype=jnp.int32) * MOE_BLOCK
    blk_expert = jnp.minimum(jnp.searchsorted(pad_ends, blk_start, side='right'), N_EXPERTS - 1)
    xs = xt[row_tok].reshape(n_blocks, MOE_BLOCK, D)

    def expert_block(args):
        xb, e = args
        gu = xb @ w_gate_up[e] + b_gate_up[e]
        gate, up = gu[:, ::2], gu[:, 1::2]
        gate = jnp.minimum(gate, SWIGLU_LIMIT)
        up = jnp.clip(up, -SWIGLU_LIMIT, SWIGLU_LIMIT)
        glu = gate * jax.nn.sigmoid(gate * SWIGLU_ALPHA)
        return ((up + 1.0) * glu) @ w_down[e] + b_down[e]

    ys = lax.map(expert_block, (xs, blk_expert)).reshape(P, D)
    out = jax.ops.segment_sum(ys * row_w[:, None].astype(ys.dtype), row_tok, num_segments=T)
    return out.reshape(B, S, D)


def setup_inputs(seed: int = 0) -> dict:
    key = jax.random.key(seed)
    ks = jax.random.split(key, 32)
    f32 = jnp.float32
    L = DEPTH

    def nrm(k, shape, s):
        return jax.random.normal(k, shape, f32) * s

    return {
        'x': nrm(ks[0], (BATCH, SEQ, D_MODEL), 1.0),
        'mem': nrm(ks[1], (BATCH, N_MEM, D_MODEL), 1.0),
        'w_in': nrm(ks[2], (L, D_MODEL, IN_COLS), D_MODEL ** -0.5),
        'w_out': nrm(ks[3], (L, MIX_W, D_MODEL), MIX_W ** -0.5),
        'w_mem_kv': nrm(ks[4], (L, D_MODEL, 2 * MEM_W), D_MODEL ** -0.5),
        'g_mix': 1.0 + nrm(ks[5], (L, D_MODEL), 0.02),
        'g_mem': 1.0 + nrm(ks[6], (L, D_MODEL), 0.02),
        'g_ffn': 1.0 + nrm(ks[7], (L, D_MODEL), 0.02),
        'g_final': 1.0 + nrm(ks[8], (D_MODEL,), 0.02),
        'rel_bias': nrm(ks[9], (N_BUCKETS, MOBA_HEADS), 0.3),
        'rwkv_mu': jax.random.uniform(ks[10], (L, RWKV_COLS), f32),
        'rwkv_w0': jax.random.uniform(ks[11], (L, RWKV_W), f32, -6.0, 0.0),
        'rwkv_w_up': nrm(ks[12], (L, DECAY_LORA, RWKV_W), 0.1 * DECAY_LORA ** -0.5),
        'rwkv_a0': nrm(ks[13], (L, RWKV_W), 0.1),
        'rwkv_a_up': nrm(ks[14], (L, AAA_LORA, RWKV_W), 0.5 * AAA_LORA ** -0.5),
        'rwkv_g_up': nrm(ks[15], (L, GATE_LORA, RWKV_W), GATE_LORA ** -0.5),
        'rwkv_k_k': 0.85 + nrm(ks[16], (L, RWKV_W), 0.02),
        'rwkv_k_a': 1.0 + nrm(ks[17], (L, RWKV_W), 0.02),
        'rwkv_r_k': nrm(ks[18], (L, RWKV_HEADS, HEAD_DIM), 0.1),
        'rwkv_gn_g': 1.0 + nrm(ks[19], (L, RWKV_W), 0.02),
        'rwkv_gn_b': nrm(ks[20], (L, RWKV_W), 0.02),
        'w_router': nrm(ks[21], (L, D_MODEL, N_EXPERTS), D_MODEL ** -0.5),
        'b_router': nrm(ks[22], (L, N_EXPERTS), 0.01),
        'w_gate_up': nrm(ks[23], (L, N_EXPERTS, D_MODEL, 2 * D_EXPERT), D_MODEL ** -0.5),
        'b_gate_up': nrm(ks[24], (L, N_EXPERTS, 2 * D_EXPERT), 0.01),
        'w_down': nrm(ks[25], (L, N_EXPERTS, D_EXPERT, D_MODEL), D_EXPERT ** -0.5),
        'b_down': nrm(ks[26], (L, N_EXPERTS, D_MODEL), 0.01),
    }


def reference(x, mem, w_in, w_out, w_mem_kv, g_mix, g_mem, g_ffn, g_final, rel_bias,
              rwkv_mu, rwkv_w0, rwkv_w_up, rwkv_a0, rwkv_a_up, rwkv_g_up, rwkv_k_k, rwkv_k_a,
              rwkv_r_k, rwkv_gn_g, rwkv_gn_b, w_router, b_router, w_gate_up, b_gate_up,
              w_down, b_down):
    B, S, _ = x.shape
    M = mem.shape[1]
    for l in range(DEPTH):
        h = rmsnorm(x, g_mix[l])
        p = h @ w_in[l]
        pq, pk, pv, p_rwkv, p_mq = jnp.split(
            p, [MOBA_W, 2 * MOBA_W, 3 * MOBA_W, 3 * MOBA_W + RWKV_COLS], axis=-1)
        y_moba = moba_attention(pq.reshape(B, S, MOBA_HEADS, HEAD_DIM),
                                pk.reshape(B, S, MOBA_HEADS, HEAD_DIM),
                                pv.reshape(B, S, MOBA_HEADS, HEAD_DIM), rel_bias)
        y_rwkv = rwkv7_time_mix(p_rwkv, rwkv_mu[l], rwkv_w0[l], rwkv_w_up[l], rwkv_a0[l],
                                rwkv_a_up[l], rwkv_g_up[l], rwkv_k_k[l], rwkv_k_a[l],
                                rwkv_r_k[l], rwkv_gn_g[l], rwkv_gn_b[l])
        mkv = rmsnorm(mem, g_mem[l]) @ w_mem_kv[l]
        mk, mv = jnp.split(mkv, 2, axis=-1)
        y_mem = memory_attention(p_mq.reshape(B, S, MEM_HEADS, HEAD_DIM),
                                 mk.reshape(B, M, MEM_HEADS, HEAD_DIM),
                                 mv.reshape(B, M, MEM_HEADS, HEAD_DIM))
        x = x + jnp.concatenate([y_moba, y_rwkv, y_mem], axis=-1) @ w_out[l]
        x = x + moe_ffn(rmsnorm(x, g_ffn[l]), w_router[l], b_router[l], w_gate_up[l],
                        b_gate_up[l], w_down[l], b_down[l])
    return rmsnorm(x, g_final)
```

```python
import functools
import math

import numpy as np
import jax
import jax.numpy as jnp
from jax import lax
from jax.experimental import pallas as pl
from jax.experimental.pallas import tpu as pltpu

F32 = jnp.float32
BF16 = jnp.bfloat16
HI = lax.Precision.HIGHEST

D_MODEL = 1024
HEAD_DIM = 64
MOBA_HEADS = 6
RWKV_HEADS = 6
MEM_HEADS = 4
MOBA_W = MOBA_HEADS * HEAD_DIM
RWKV_W = RWKV_HEADS * HEAD_DIM
MEM_W = MEM_HEADS * HEAD_DIM
MOBA_BLOCK = 256
MOBA_TOPK = 3
N_BUCKETS = 32
MAX_DISTANCE = 128
DECAY_LORA = 64
AAA_LORA = 64
GATE_LORA = 128
RWKV_COLS = 3 * RWKV_W + DECAY_LORA + AAA_LORA + GATE_LORA
RWKV_GN_EPS = 64e-5
IN_COLS = 3 * MOBA_W + RWKV_COLS + MEM_W
N_EXPERTS = 32
TOP_K = 4
D_EXPERT = D_MODEL
SWIGLU_ALPHA = 1.702
SWIGLU_LIMIT = 7.0
RMS_EPS = 1e-5

LANES = 128
RWKV_CHUNK = 64
EXPERT_ROWS = 256
VMEM_LIMIT = 56 * 1024 * 1024
NEG_INF = float("-inf")


def _cparams(sem):
    return pltpu.CompilerParams(dimension_semantics=sem, vmem_limit_bytes=VMEM_LIMIT)


def _rms(x, g):
    return x * lax.rsqrt(jnp.mean(x * x, axis=-1, keepdims=True) + RMS_EPS) * g


def _bdot(a, b):
    return jnp.dot(a.astype(BF16), b.astype(BF16), preferred_element_type=F32)


def _hdot(a, b):
    return jnp.dot(a, b, preferred_element_type=F32, precision=HI)


def _dot_nt(a, b, **kw):
    return lax.dot_general(a, b, (((1,), (1,)), ((), ())), preferred_element_type=F32, **kw)


def _dot_tn(a, b, **kw):
    return lax.dot_general(a, b, (((0,), (0,)), ((), ())), preferred_element_type=F32, **kw)


def _norm_matmul_kernel(x_ref, g_ref, w_ref, o_ref):
    h = _rms(x_ref[...], g_ref[...])
    o_ref[...] = jnp.dot(h.astype(BF16), w_ref[...], preferred_element_type=F32)


def _norm_matmul(x, g, w_bf16, tm, name):
    t, d = x.shape
    n = w_bf16.shape[1]
    tm = min(tm, t)
    return pl.pallas_call(
        _norm_matmul_kernel,
        grid=(t // tm,),
        in_specs=[pl.BlockSpec((tm, d), lambda i: (i, 0)),
                  pl.BlockSpec((1, d), lambda i: (0, 0)),
                  pl.BlockSpec((d, n), lambda i: (0, 0))],
        out_specs=pl.BlockSpec((tm, n), lambda i: (i, 0)),
        out_shape=jax.ShapeDtypeStruct((t, n), F32),
        compiler_params=_cparams(("parallel",)),
        name=name,
    )(x, g.reshape(1, d), w_bf16)


def _t5_bucket_np(dist):
    n = np.maximum(dist, 0)
    max_exact = N_BUCKETS // 2
    nf = np.maximum(n, 1).astype(np.float64)
    large = max_exact + (np.log(nf / max_exact) / math.log(MAX_DISTANCE / max_exact)
                         * (N_BUCKETS - max_exact)).astype(np.int64)
    large = np.minimum(large, N_BUCKETS - 1)
    return np.where(n < max_exact, n, large)


def _moba_bias_tables(rel_bias):
    kk = np.arange(MOBA_BLOCK)[:, None]
    qq = np.arange(MOBA_BLOCK)[None, :]
    d_own = qq - kk
    d_prev = MOBA_BLOCK + qq - kk
    assert np.all(_t5_bucket_np(np.arange(MOBA_BLOCK + 1, 64 * MOBA_BLOCK)) == N_BUCKETS - 1)
    bias_t = rel_bias.astype(F32).T
    t_own = bias_t[:, _t5_bucket_np(d_own)]
    t_own = jnp.where(jnp.asarray(d_own >= 0)[None], t_own, NEG_INF)
    t_prev = bias_t[:, _t5_bucket_np(d_prev)]
    far = bias_t[:, N_BUCKETS - 1]
    return t_own, t_prev, far


def _moba_kernel(far_ref, q_ref, k_ref, v_ref, t0_ref, t1_ref, o_ref,
                 qt_s, vt_s, gate_s, neg_s, *, nb):
    hp = pl.program_id(1)
    blk = MOBA_BLOCK
    scale = HEAD_DIM ** -0.5
    row_head = lax.broadcasted_iota(jnp.int32, (LANES, 1), 0) // HEAD_DIM
    lane_head = lax.broadcasted_iota(jnp.int32, (1, LANES), 1) // HEAD_DIM

    kmean_rows = []
    for j in range(nb):
        sl = pl.ds(j * blk, blk)
        qt_s[j] = (q_ref[0, sl, :] * scale).T
        vt_s[j] = v_ref[0, sl, :].T
        kmean_rows.append(jnp.mean(k_ref[0, sl, :], axis=0, keepdims=True))
    kmean = jnp.concatenate(kmean_rows, axis=0)
    for h in range(2):
        km_h = jnp.where(lane_head == h, kmean, 0.0)
        for i in range(nb):
            gate_s[h, i] = _hdot(km_h, qt_s[i])

    blk_iota = lax.broadcasted_iota(jnp.int32, (nb, blk), 0)

    def q_block(i, carry):
        outs = []
        for h in range(2):
            g = gate_s[h, i]
            cnt = jnp.zeros((nb, blk), jnp.int32)
            for m in range(nb):
                gm = g[m:m + 1, :]
                beats = (gm > g) | ((gm == g) & (m < blk_iota))
                cnt = cnt + jnp.where(beats & (m < i), 1, 0)
            sel = (blk_iota < i) & (cnt < MOBA_TOPK)
            neg_s[...] = jnp.where(sel, 0.0, NEG_INF)

            qt_h = jnp.where(row_head == h, qt_s[i], 0.0).astype(BF16)
            far_bias = far_ref[2 * hp + h]

            def scores(j):
                kj = k_ref[0, pl.ds(pl.multiple_of(j * blk, blk), blk), :].astype(BF16)
                return jnp.dot(kj, qt_h, preferred_element_type=F32)

            def pv(j, p_t):
                vt_h = vt_s[j, h * HEAD_DIM:(h + 1) * HEAD_DIM, :].astype(BF16)
                return jnp.dot(vt_h, p_t.astype(BF16), preferred_element_type=F32)

            s = scores(i) + t0_ref[h]
            m_run = jnp.max(s, axis=0, keepdims=True)
            p_t = jnp.exp(s - m_run)
            l_run = jnp.sum(p_t, axis=0, keepdims=True)
            acc = pv(i, p_t)

            def update(j, s, state):
                m_run, l_run, acc = state
                s = s + neg_s[pl.ds(j, 1), :]
                m_new = jnp.maximum(m_run, jnp.max(s, axis=0, keepdims=True))
                alpha = jnp.exp(m_run - m_new)
                p_t = jnp.exp(s - m_new)
                l_new = alpha * l_run + jnp.sum(p_t, axis=0, keepdims=True)
                return m_new, l_new, alpha * acc + pv(j, p_t)

            jp = jnp.maximum(i - 1, 0)
            state = update(jp, scores(jp) + t1_ref[h], (m_run, l_run, acc))

            def far_body(j, state):
                return update(j, scores(j) + far_bias, state)

            m_run, l_run, acc = lax.fori_loop(0, jnp.maximum(i - 1, 0), far_body, state)
            outs.append(acc / l_run)
        o_t = jnp.concatenate(outs, axis=0)
        o_ref[0, pl.ds(pl.multiple_of(i * blk, blk), blk), :] = o_t.T
        return carry

    lax.fori_loop(0, nb, q_block, 0)


def _moba_attention(p3, t_own, t_prev, far):
    b, s, _ = p3.shape
    nb = s // MOBA_BLOCK
    n_pairs = MOBA_HEADS // 2
    blk_spec = lambda off: pl.BlockSpec((1, s, LANES), lambda bi, hp: (bi, 0, off + hp))
    tab_spec = pl.BlockSpec((2, MOBA_BLOCK, MOBA_BLOCK), lambda bi, hp: (hp, 0, 0))
    return pl.pallas_call(
        functools.partial(_moba_kernel, nb=nb),
        grid=(b, n_pairs),
        in_specs=[pl.BlockSpec(memory_space=pltpu.SMEM),
                  blk_spec(0), blk_spec(n_pairs), blk_spec(2 * n_pairs), tab_spec, tab_spec],
        out_specs=pl.BlockSpec((1, s, LANES), lambda bi, hp: (bi, 0, hp)),
        out_shape=jax.ShapeDtypeStruct((b, s, MOBA_W), F32),
        scratch_shapes=[pltpu.VMEM((nb, LANES, MOBA_BLOCK), F32),
                        pltpu.VMEM((nb, LANES, MOBA_BLOCK), F32),
                        pltpu.VMEM((2, nb, nb, MOBA_BLOCK), F32),
                        pltpu.VMEM((nb, MOBA_BLOCK), F32)],
        compiler_params=_cparams(("parallel", "parallel")),
        name="moba",
    )(far, p3, p3, p3, t_own, t_prev)


def _softplus(z):
    return jnp.maximum(z, 0.0) + jnp.log(1.0 + jnp.exp(-jnp.abs(z)))


def _sigmoid(z):
    return 1.0 / (1.0 + jnp.exp(-z))


def _rwkv_kernel(r_ref, k_ref, v_ref, wa_ref, g_ref,
                 mu_r_ref, mu_k_ref, mu_v_ref, mu_wa_ref, mu_g_ref,
                 w0_ref, wup_ref, a0_ref, aup_ref, gup_ref, kk_ref, ka_ref, rk_ref,
                 gng_ref, gnb_ref, hsum_ref, hmask_ref, tri_ref,
                 o_ref, st_s, prev_r, prev_k, prev_v, prev_wa, prev_g):
    c = RWKV_CHUNK
    n_heads = RWKV_HEADS

    @pl.when(pl.program_id(1) == 0)
    def _():
        st_s[...] = jnp.zeros_like(st_s)
        prev_r[...] = jnp.zeros_like(prev_r)
        prev_k[...] = jnp.zeros_like(prev_k)
        prev_v[...] = jnp.zeros_like(prev_v)
        prev_wa[...] = jnp.zeros_like(prev_wa)
        prev_g[...] = jnp.zeros_like(prev_g)

    def mix(x_ref, prev_ref, mu_ref):
        x = x_ref[0]
        row = lax.broadcasted_iota(jnp.int32, x.shape, 0)
        shifted = jnp.where(row == 0, prev_ref[...], pltpu.roll(x, 1, axis=0))
        prev_ref[...] = x[c - 1:c, :]
        return x + (shifted - x) * mu_ref[...]

    r = mix(r_ref, prev_r, mu_r_ref)
    k = mix(k_ref, prev_k, mu_k_ref)
    v = mix(v_ref, prev_v, mu_v_ref)
    xwa = mix(wa_ref, prev_wa, mu_wa_ref)
    xg = mix(g_ref, prev_g, mu_g_ref)

    hsum = hsum_ref[...]
    w = -_softplus(-(w0_ref[...] + _bdot(jnp.tanh(xwa), wup_ref[...]))) - 0.5
    logd = -jnp.exp(w)
    a = _sigmoid(a0_ref[...] + _bdot(xwa, aup_ref[...]))
    gate = _bdot(_sigmoid(xg), gup_ref[...])
    kk = k * kk_ref[...]
    kk = kk / jnp.maximum(jnp.sqrt(_hdot(kk * kk, hsum)), 1e-12)
    k2 = k * (1.0 + (a - 1.0) * ka_ref[...])

    tri = tri_ref[...]
    cum = _hdot(tri, logd)
    cum_last = cum[c - 1:c, :]
    g_in = jnp.exp(cum)
    g_inv = jnp.exp(-cum)
    g_tail = jnp.exp(cum_last - cum)
    a_t = -kk * jnp.exp(cum - logd)
    b_t = kk * a * g_inv
    k_t = k2 * g_inv
    r_t = r * g_in
    b_hat = kk * a * g_tail
    k_hat = k2 * g_tail

    st = st_s[...]
    p0 = _bdot(a_t, st)
    o0 = _bdot(r_t, st)

    row = lax.broadcasted_iota(jnp.int32, (c, c), 0)
    col = lax.broadcasted_iota(jnp.int32, (c, c), 1)
    strict = row > col
    incl = row >= col
    eye = (row == col).astype(F32)
    bk = jnp.concatenate([b_t, k_t], axis=0).astype(BF16)
    lane_head = lax.broadcasted_iota(jnp.int32, (1, RWKV_W), 1) // HEAD_DIM

    u_all = jnp.zeros((c, RWKV_W), F32)
    o_all = jnp.zeros((c, RWKV_W), F32)
    for h in range(n_heads):
        hm = lane_head == h
        ar = jnp.concatenate([jnp.where(hm, a_t, 0.0), jnp.where(hm, r_t, 0.0)], axis=0)
        m4 = _dot_nt(ar.astype(BF16), bk)
        l_ab = jnp.where(strict, m4[:c, :c], 0.0)
        l_ak = jnp.where(strict, m4[:c, c:], 0.0)
        m_rb = jnp.where(incl, m4[c:, :c], 0.0)
        m_rk = jnp.where(incl, m4[c:, c:], 0.0)
        t_inv = eye + l_ab
        pw = l_ab
        span = 2
        while span < c:
            pw = _hdot(pw, pw)
            t_inv = t_inv + _hdot(t_inv, pw)
            span *= 2
        u_h = _hdot(t_inv, p0 + _bdot(l_ak, v))
        u_all = jnp.where(hm, u_h, u_all)
        o_h = _bdot(m_rb, u_h) + _bdot(m_rk, v)
        o_all = jnp.where(hm, o_h, o_all)
    o = o0 + o_all

    upd = _dot_tn(b_hat.astype(BF16), u_all.astype(BF16)) + _dot_tn(k_hat.astype(BF16), v.astype(BF16))
    first_row = (lax.broadcasted_iota(jnp.int32, (c, RWKV_W), 0) == 0)
    g_col = _dot_tn(jnp.where(first_row, jnp.exp(cum_last), 0.0), jnp.ones((c, RWKV_W), F32), precision=HI)
    st_s[...] = g_col * st + hmask_ref[...] * upd

    inv_n = 1.0 / HEAD_DIM
    mean = _hdot(o, hsum) * inv_n
    dev = o - mean
    var = _hdot(dev * dev, hsum) * inv_n
    y = dev * lax.rsqrt(var + RWKV_GN_EPS) * gng_ref[...] + gnb_ref[...]
    y = y + _hdot(r * k2 * rk_ref[...], hsum) * v
    o_ref[0] = y * gate


def _rwkv(p3, mu, w0, w_up, a0, a_up, g_up, k_k, k_a, r_k, gn_g, gn_b):
    b, s, _ = p3.shape
    c = RWKV_CHUNK
    w = RWKV_W
    base = 3 * MOBA_W
    assert base % w == 0 and (base + 3 * w) % LANES == 0
    lora = DECAY_LORA + AAA_LORA

    def col_spec(width, off_cols):
        assert off_cols % width == 0
        return pl.BlockSpec((1, c, width), lambda bi, ci: (bi, ci, off_cols // width))

    row = lambda x: x.reshape(1, -1).astype(F32)
    const = lambda shape: pl.BlockSpec(shape, lambda bi, ci: (0,) * len(shape))
    wup_pad = jnp.concatenate([w_up, jnp.zeros((AAA_LORA, w), F32)], axis=0).astype(BF16)
    aup_pad = jnp.concatenate([jnp.zeros((DECAY_LORA, w), F32), a_up], axis=0).astype(BF16)
    head = np.arange(w) // HEAD_DIM
    hsum = jnp.asarray((head[:, None] == head[None, :]).astype(np.float32))
    tri = jnp.asarray(np.tril(np.ones((c, c), np.float32)))
    vec_args = [mu[:w], mu[w:2 * w], mu[2 * w:3 * w], mu[3 * w:3 * w + lora], mu[3 * w + lora:],
                w0, None, a0, None, None, k_k, k_a, r_k.reshape(-1), gn_g, gn_b]
    args = [p3, p3, p3, p3, p3]
    specs = [col_spec(w, base), col_spec(w, base + w), col_spec(w, base + 2 * w),
             col_spec(lora, base + 3 * w), col_spec(GATE_LORA, base + 3 * w + lora)]
    mats = {6: wup_pad, 8: aup_pad, 9: g_up.astype(BF16)}
    for idx, a in enumerate(vec_args):
        arr = mats[idx] if a is None else row(a)
        args.append(arr)
        specs.append(const(arr.shape))
    for arr in (hsum, hsum, tri):
        args.append(arr)
        specs.append(const(arr.shape))
    return pl.pallas_call(
        _rwkv_kernel,
        grid=(b, s // c),
        in_specs=specs,
        out_specs=pl.BlockSpec((1, c, w), lambda bi, ci: (bi, ci, 0)),
        out_shape=jax.ShapeDtypeStruct((b, s, w), F32),
        scratch_shapes=[pltpu.VMEM((w, w), F32), pltpu.VMEM((1, w), F32), pltpu.VMEM((1, w), F32),
                        pltpu.VMEM((1, w), F32), pltpu.VMEM((1, lora), F32),
                        pltpu.VMEM((1, GATE_LORA), F32)],
        compiler_params=_cparams(("parallel", "arbitrary")),
        name="rwkv",
    )(*args)


def _mem_attn_kernel(q_ref, kv_ref, o_ref):
    scale = HEAD_DIM ** -0.5
    lane_head = lax.broadcasted_iota(jnp.int32, (1, MEM_W), 1) // HEAD_DIM
    q = q_ref[0] * scale
    mk = kv_ref[0, :, :MEM_W].astype(BF16)
    mv = kv_ref[0, :, MEM_W:].astype(BF16)
    out = jnp.zeros(q.shape, F32)
    for h in range(MEM_HEADS):
        hm = lane_head == h
        s = _dot_nt(jnp.where(hm, q, 0.0).astype(BF16), mk)
        s = s - jnp.max(s, axis=-1, keepdims=True)
        e = jnp.exp(s)
        p = e / jnp.sum(e, axis=-1, keepdims=True)
        out = jnp.where(hm, jnp.dot(p.astype(BF16), mv, preferred_element_type=F32), out)
    o_ref[0] = out


def _mem_attention(p3, mkv3, tq):
    b, s, _ = p3.shape
    m = mkv3.shape[1]
    off = (3 * MOBA_W + RWKV_COLS) // MEM_W
    assert off * MEM_W == 3 * MOBA_W + RWKV_COLS
    return pl.pallas_call(
        _mem_attn_kernel,
        grid=(b, s // tq),
        in_specs=[pl.BlockSpec((1, tq, MEM_W), lambda bi, i: (bi, i, off)),
                  pl.BlockSpec((1, m, 2 * MEM_W), lambda bi, i: (bi, 0, 0))],
        out_specs=pl.BlockSpec((1, tq, MEM_W), lambda bi, i: (bi, i, 0)),
        out_shape=jax.ShapeDtypeStruct((b, s, MEM_W), F32),
        compiler_params=_cparams(("parallel", "parallel")),
        name="mem_attn",
    )(p3, mkv3)


def _out_router_kernel(x_ref, ym_ref, yr_ref, ye_ref, wo1_ref, wo2_ref, wo3_ref, g_ref,
                       wr_ref, br_ref, ltri_ref,
                       x1_ref, h_ref, idx_ref, wgt_ref, rank_ref, cnt_ref, run_s):
    @pl.when(pl.program_id(0) == 0)
    def _():
        run_s[...] = jnp.zeros_like(run_s)

    x1 = (x_ref[...] + _bdot(ym_ref[...], wo1_ref[...]) + _bdot(yr_ref[...], wo2_ref[...])
          + _bdot(ye_ref[...], wo3_ref[...]))
    x1_ref[...] = x1
    h = _rms(x1, g_ref[...])
    h_ref[...] = h.astype(BF16)
    logits = _hdot(h, wr_ref[...]) + br_ref[...]
    tm = logits.shape[0]
    lane = lax.broadcasted_iota(jnp.int32, (tm, LANES), 1)

    vals, idxs = [], []
    lg = logits
    for _ in range(TOP_K):
        m = jnp.max(lg, axis=1, keepdims=True)
        idx = jnp.min(jnp.where(lg == m, lane, LANES), axis=1, keepdims=True)
        vals.append(m)
        idxs.append(idx)
        lg = jnp.where(lane == idx, NEG_INF, lg)
    exps = [jnp.exp(vk - vals[0]) for vk in vals]
    denom = exps[0] + exps[1] + exps[2] + exps[3]

    chosen = jnp.zeros((tm, LANES), F32)
    for idx in idxs:
        chosen = chosen + jnp.where(lane == idx, 1.0, 0.0)
    before = jnp.dot(ltri_ref[...], chosen.astype(BF16), preferred_element_type=F32) + run_s[...]
    run_s[...] = run_s[...] + jnp.sum(chosen, axis=0, keepdims=True)
    cnt_ref[...] = run_s[...]

    idx_out = jnp.zeros((tm, LANES), jnp.int32)
    wgt_out = jnp.zeros((tm, LANES), F32)
    rank_out = jnp.zeros((tm, LANES), jnp.int32)
    for kk in range(TOP_K):
        rank = jnp.sum(jnp.where(lane == idxs[kk], before, 0.0), axis=1, keepdims=True)
        idx_out = jnp.where(lane == kk, idxs[kk], idx_out)
        wgt_out = jnp.where(lane == kk, exps[kk] / denom, wgt_out)
        rank_out = jnp.where(lane == kk, rank.astype(jnp.int32), rank_out)
    idx_ref[...] = idx_out
    wgt_ref[...] = wgt_out
    rank_ref[...] = rank_out


def _out_router(x2, ym, yr, ye, w_out, g_ffn, w_router, b_router, tm):
    t, d = x2.shape
    wo = w_out.astype(BF16)
    wo1, wo2, wo3 = wo[:MOBA_W], wo[MOBA_W:MOBA_W + RWKV_W], wo[MOBA_W + RWKV_W:]
    wr = jnp.zeros((d, LANES), F32).at[:, :N_EXPERTS].set(w_router)
    br = jnp.full((1, LANES), NEG_INF, F32).at[0, :N_EXPERTS].set(b_router)
    ltri = jnp.asarray(np.tril(np.ones((tm, tm), np.float32), -1)).astype(BF16)
    tile = lambda n: pl.BlockSpec((tm, n), lambda i: (i, 0))
    const = lambda a: pl.BlockSpec(a.shape, lambda i: (0,) * a.ndim)
    g2 = g_ffn.reshape(1, d)
    return pl.pallas_call(
        _out_router_kernel,
        grid=(t // tm,),
        in_specs=[tile(d), tile(MOBA_W), tile(RWKV_W), tile(MEM_W), const(wo1), const(wo2), const(wo3),
                  const(g2), const(wr), const(br), const(ltri)],
        out_specs=[tile(d), tile(d), tile(LANES), tile(LANES), tile(LANES),
                   pl.BlockSpec((1, LANES), lambda i: (0, 0))],
        out_shape=[jax.ShapeDtypeStruct((t, d), F32), jax.ShapeDtypeStruct((t, d), BF16),
                   jax.ShapeDtypeStruct((t, LANES), jnp.int32), jax.ShapeDtypeStruct((t, LANES), F32),
                   jax.ShapeDtypeStruct((t, LANES), jnp.int32), jax.ShapeDtypeStruct((1, LANES), F32)],
        scratch_shapes=[pltpu.VMEM((1, LANES), F32)],
        compiler_params=_cparams(("arbitrary",)),
        name="out_router",
    )(x2, ym, yr, ye, wo1, wo2, wo3, g2, wr, br, ltri)


def _experts_kernel(be_ref, nused_ref, xs_ref, wgu_ref, bg_ref, bu_ref, wd_ref, bd_ref, perm_ref,
                    o_ref, wg_s, wu_s, wd_s):
    i = pl.program_id(0)
    prev = be_ref[jnp.maximum(i - 1, 0)]
    changed = (i == 0) | (be_ref[i] != prev)

    @pl.when(changed & (i < nused_ref[0]))
    def _():
        half = LANES
        for cblk in range(2 * D_EXPERT // (2 * half)):
            wt = wgu_ref[0, :, cblk * 2 * half:(cblk + 1) * 2 * half].astype(BF16)
            sep = jnp.dot(wt, perm_ref[...], preferred_element_type=F32).astype(BF16)
            wg_s[:, cblk * half:(cblk + 1) * half] = sep[:, :half]
            wu_s[:, cblk * half:(cblk + 1) * half] = sep[:, half:]
        wd_s[...] = wd_ref[0].astype(BF16)

    @pl.when(i < nused_ref[0])
    def _():
        xb = xs_ref[...]
        gate = jnp.dot(xb, wg_s[...], preferred_element_type=F32) + bg_ref[0]
        up = jnp.dot(xb, wu_s[...], preferred_element_type=F32) + bu_ref[0]
        gate = jnp.minimum(gate, SWIGLU_LIMIT)
        up = jnp.clip(up, -SWIGLU_LIMIT, SWIGLU_LIMIT)
        glu = gate * _sigmoid(gate * SWIGLU_ALPHA)
        act = ((up + 1.0) * glu).astype(BF16)
        o_ref[...] = jnp.dot(act, wd_s[...], preferred_element_type=F32) + bd_ref[0]

    @pl.when(i >= nused_ref[0])
    def _():
        o_ref[...] = jnp.zeros_like(o_ref)


def _experts(blk_expert, n_used, xs, w_gate_up, b_gate_up, w_down, b_down):
    p_rows, d = xs.shape
    n_blocks = p_rows // EXPERT_ROWS
    bg = b_gate_up[:, 0::2].reshape(N_EXPERTS, 1, D_EXPERT)
    bu = b_gate_up[:, 1::2].reshape(N_EXPERTS, 1, D_EXPERT)
    bd = b_down.reshape(N_EXPERTS, 1, d)
    perm_np = np.zeros((2 * LANES, 2 * LANES), np.float32)
    perm_np[2 * np.arange(LANES), np.arange(LANES)] = 1.0
    perm_np[2 * np.arange(LANES) + 1, LANES + np.arange(LANES)] = 1.0
    perm = jnp.asarray(perm_np).astype(BF16)
    by_expert = lambda shape: pl.BlockSpec((1,) + shape, lambda i, be, nu: (be[i], 0, 0))
    grid_spec = pltpu.PrefetchScalarGridSpec(
        num_scalar_prefetch=2,
        grid=(n_blocks,),
        in_specs=[pl.BlockSpec((EXPERT_ROWS, d), lambda i, be, nu: (i, 0)),
                  by_expert((d, 2 * D_EXPERT)), by_expert((1, D_EXPERT)), by_expert((1, D_EXPERT)),
                  by_expert((D_EXPERT, d)), by_expert((1, d)),
                  pl.BlockSpec(perm.shape, lambda i, be, nu: (0, 0))],
        out_specs=pl.BlockSpec((EXPERT_ROWS, d), lambda i, be, nu: (i, 0)),
        scratch_shapes=[pltpu.VMEM((d, D_EXPERT), BF16), pltpu.VMEM((d, D_EXPERT), BF16),
                        pltpu.VMEM((D_EXPERT, d), BF16)],
    )
    return pl.pallas_call(
        _experts_kernel,
        grid_spec=grid_spec,
        out_shape=jax.ShapeDtypeStruct((p_rows, d), F32),
        compiler_params=_cparams(("arbitrary",)),
        name="experts",
    )(blk_expert, n_used, xs, w_gate_up, bg, bu, w_down, bd, perm)


def _combine_kernel(x1_ref, yg_ref, w_ref, g_ref, o_ref):
    d = x1_ref.shape[1]
    acc = x1_ref[...]
    wts = w_ref[...]
    for kk in range(TOP_K):
        acc = acc + yg_ref[:, kk * d:(kk + 1) * d] * wts[:, kk:kk + 1]
    o_ref[...] = _rms(acc, g_ref[...])


def _combine(x1, yg, wts, g_final, tm):
    t, d = x1.shape
    return pl.pallas_call(
        _combine_kernel,
        grid=(t // tm,),
        in_specs=[pl.BlockSpec((tm, d), lambda i: (i, 0)),
                  pl.BlockSpec((tm, TOP_K * d), lambda i: (i, 0)),
                  pl.BlockSpec((tm, LANES), lambda i: (i, 0)),
                  pl.BlockSpec((1, d), lambda i: (0, 0))],
        out_specs=pl.BlockSpec((tm, d), lambda i: (i, 0)),
        out_shape=jax.ShapeDtypeStruct((t, d), F32),
        compiler_params=_cparams(("parallel",)),
        name="combine",
    )(x1, yg, wts, g_final.reshape(1, d))


def _layer(x, mem, w_in, w_out, w_mem_kv, g_mix, g_mem, g_ffn, bias_tables, mu, w0, w_up, a0, a_up,
           g_up, k_k, k_a, r_k, gn_g, gn_b, w_router, b_router, w_gate_up, b_gate_up, w_down, b_down,
           g_last):
    b, s, d = x.shape
    m = mem.shape[1]
    t = b * s
    x2 = x.reshape(t, d)

    p = _norm_matmul(x2, g_mix, w_in.astype(BF16), 512, "in_proj")
    p3 = p.reshape(b, s, IN_COLS)
    y_moba = _moba_attention(p3, *bias_tables)
    y_rwkv = _rwkv(p3, mu, w0, w_up, a0, a_up, g_up, k_k, k_a, r_k, gn_g, gn_b)
    mkv = _norm_matmul(mem.reshape(b * m, d), g_mem, w_mem_kv.astype(BF16), 512, "mem_kv")
    y_mem = _mem_attention(p3, mkv.reshape(b, m, 2 * MEM_W), 512)

    x1, h2, idx_p, wgt_p, rank_p, cnt = _out_router(
        x2, y_moba.reshape(t, MOBA_W), y_rwkv.reshape(t, RWKV_W), y_mem.reshape(t, MEM_W),
        w_out, g_ffn, w_router, b_router, 256)

    top_idx = idx_p[:, :TOP_K]
    counts = cnt[0, :N_EXPERTS].astype(jnp.int32)
    padded = (counts + EXPERT_ROWS - 1) // EXPERT_ROWS * EXPERT_ROWS
    pad_ends = jnp.cumsum(padded)
    pad_starts = pad_ends - padded
    dest = pad_starts[top_idx] + rank_p[:, :TOP_K]
    n_blocks = (t * TOP_K) // EXPERT_ROWS + N_EXPERTS
    blk_start = jnp.arange(n_blocks, dtype=jnp.int32) * EXPERT_ROWS
    blk_expert = jnp.minimum(jnp.searchsorted(pad_ends, blk_start, side='right'),
                             N_EXPERTS - 1).astype(jnp.int32)
    n_used = (pad_ends[-1:] // EXPERT_ROWS).astype(jnp.int32)
    tok = jnp.broadcast_to(jnp.arange(t, dtype=jnp.int32)[:, None], (t, TOP_K))
    row_tok = jnp.zeros((n_blocks * EXPERT_ROWS,), jnp.int32).at[dest.reshape(-1)].set(tok.reshape(-1))
    xs = h2[row_tok]

    ys = _experts(blk_expert, n_used, xs, w_gate_up, b_gate_up, w_down, b_down)
    yg = ys[dest.reshape(-1)].reshape(t, TOP_K * d)
    out = _combine(x1, yg, wgt_p, g_last, 256)
    return out.reshape(b, s, d)


def kernel(x, mem, w_in, w_out, w_mem_kv, g_mix, g_mem, g_ffn, g_final, rel_bias, rwkv_mu, rwkv_w0,
           rwkv_w_up, rwkv_a0, rwkv_a_up, rwkv_g_up, rwkv_k_k, rwkv_k_a, rwkv_r_k, rwkv_gn_g, rwkv_gn_b,
           w_router, b_router, w_gate_up, b_gate_up, w_down, b_down):
    depth = w_in.shape[0]
    assert depth == 1, "the final norm is fused into the last layer's combine kernel"
    bias_tables = _moba_bias_tables(rel_bias)
    l = 0
    return _layer(x, mem, w_in[l], w_out[l], w_mem_kv[l], g_mix[l], g_mem[l], g_ffn[l], bias_tables,
                  rwkv_mu[l], rwkv_w0[l], rwkv_w_up[l], rwkv_a0[l], rwkv_a_up[l], rwkv_g_up[l],
                  rwkv_k_k[l], rwkv_k_a[l], rwkv_r_k[l], rwkv_gn_g[l], rwkv_gn_b[l], w_router[l],
                  b_router[l], w_gate_up[l], b_gate_up[l], w_down[l], b_down[l], g_final)
```

```python
import functools
import math

import numpy as np
import jax
import jax.numpy as jnp
from jax import lax
from jax.experimental import pallas as pl
from jax.experimental.pallas import tpu as pltpu

F32 = jnp.float32
BF16 = jnp.bfloat16
HI = lax.Precision.HIGHEST

D_MODEL = 1024
HEAD_DIM = 64
MOBA_HEADS = 6
RWKV_HEADS = 6
MEM_HEADS = 4
MOBA_W = MOBA_HEADS * HEAD_DIM
RWKV_W = RWKV_HEADS * HEAD_DIM
MEM_W = MEM_HEADS * HEAD_DIM
MOBA_BLOCK = 256
MOBA_TOPK = 3
N_BUCKETS = 32
MAX_DISTANCE = 128
DECAY_LORA = 64
AAA_LORA = 64
GATE_LORA = 128
RWKV_COLS = 3 * RWKV_W + DECAY_LORA + AAA_LORA + GATE_LORA
RWKV_GN_EPS = 64e-5
IN_COLS = 3 * MOBA_W + RWKV_COLS + MEM_W
N_EXPERTS = 32
TOP_K = 4
D_EXPERT = D_MODEL
SWIGLU_ALPHA = 1.702
SWIGLU_LIMIT = 7.0
RMS_EPS = 1e-5

LANES = 128
RWKV_CHUNK = 128
RWKV_SEQS_PER_STEP = 2
RWKV_INV_TERMS = 1
EXPERT_ROWS = 256
VMEM_LIMIT = 56 * 1024 * 1024
NEG_INF = float("-inf")


def _cparams(sem):
    return pltpu.CompilerParams(dimension_semantics=sem, vmem_limit_bytes=VMEM_LIMIT)


def _rms(x, g):
    return x * lax.rsqrt(jnp.mean(x * x, axis=-1, keepdims=True) + RMS_EPS) * g


def _bdot(a, b):
    return jnp.dot(a.astype(BF16), b.astype(BF16), preferred_element_type=F32)


def _hdot(a, b):
    return jnp.dot(a, b, preferred_element_type=F32, precision=HI)


def _dot_nt(a, b, **kw):
    return lax.dot_general(a, b, (((1,), (1,)), ((), ())), preferred_element_type=F32, **kw)


def _dot_tn(a, b, **kw):
    return lax.dot_general(a, b, (((0,), (0,)), ((), ())), preferred_element_type=F32, **kw)


def _norm_matmul_kernel(x_ref, g_ref, w_ref, o_ref):
    h = _rms(x_ref[...], g_ref[...])
    o_ref[...] = jnp.dot(h.astype(BF16), w_ref[...], preferred_element_type=F32)


def _norm_matmul(x, g, w_bf16, tm, name):
    t, d = x.shape
    n = w_bf16.shape[1]
    tm = min(tm, t)
    return pl.pallas_call(
        _norm_matmul_kernel,
        grid=(t // tm,),
        in_specs=[pl.BlockSpec((tm, d), lambda i: (i, 0)),
                  pl.BlockSpec((1, d), lambda i: (0, 0)),
                  pl.BlockSpec((d, n), lambda i: (0, 0))],
        out_specs=pl.BlockSpec((tm, n), lambda i: (i, 0)),
        out_shape=jax.ShapeDtypeStruct((t, n), F32),
        compiler_params=_cparams(("parallel",)),
        name=name,
    )(x, g.reshape(1, d), w_bf16)


def _t5_bucket_np(dist):
    n = np.maximum(dist, 0)
    max_exact = N_BUCKETS // 2
    nf = np.maximum(n, 1).astype(np.float64)
    large = max_exact + (np.log(nf / max_exact) / math.log(MAX_DISTANCE / max_exact)
                         * (N_BUCKETS - max_exact)).astype(np.int64)
    large = np.minimum(large, N_BUCKETS - 1)
    return np.where(n < max_exact, n, large)


def _moba_bias_tables(rel_bias):
    kk = np.arange(MOBA_BLOCK)[:, None]
    qq = np.arange(MOBA_BLOCK)[None, :]
    d_own = qq - kk
    d_prev = MOBA_BLOCK + qq - kk
    assert np.all(_t5_bucket_np(np.arange(MOBA_BLOCK + 1, 64 * MOBA_BLOCK)) == N_BUCKETS - 1)
    bias_t = rel_bias.astype(F32).T
    t_own = bias_t[:, _t5_bucket_np(d_own)]
    t_own = jnp.where(jnp.asarray(d_own >= 0)[None], t_own, NEG_INF)
    t_prev = bias_t[:, _t5_bucket_np(d_prev)]
    far = bias_t[:, N_BUCKETS - 1]
    return t_own, t_prev, far


def _moba_kernel(far_ref, q_ref, k_ref, v_ref, t0_ref, t1_ref, o_ref,
                 qt_s, vt_s, gate_s, neg_s, *, nb):
    hp = pl.program_id(1)
    blk = MOBA_BLOCK
    scale = HEAD_DIM ** -0.5
    row_head = lax.broadcasted_iota(jnp.int32, (LANES, 1), 0) // HEAD_DIM
    lane_head = lax.broadcasted_iota(jnp.int32, (1, LANES), 1) // HEAD_DIM

    kmean_rows = []
    for j in range(nb):
        sl = pl.ds(j * blk, blk)
        qt_s[j] = (q_ref[0, sl, :] * scale).T
        vt_s[j] = v_ref[0, sl, :].T
        kmean_rows.append(jnp.mean(k_ref[0, sl, :], axis=0, keepdims=True))
    kmean = jnp.concatenate(kmean_rows, axis=0)
    for h in range(2):
        km_h = jnp.where(lane_head == h, kmean, 0.0)
        for i in range(nb):
            gate_s[h, i] = _hdot(km_h, qt_s[i])

    blk_iota = lax.broadcasted_iota(jnp.int32, (nb, blk), 0)

    def q_block(i, carry):
        outs = []
        for h in range(2):
            g = gate_s[h, i]
            cnt = jnp.zeros((nb, blk), jnp.int32)
            for m in range(nb):
                gm = g[m:m + 1, :]
                beats = (gm > g) | ((gm == g) & (m < blk_iota))
                cnt = cnt + jnp.where(beats & (m < i), 1, 0)
            sel = (blk_iota < i) & (cnt < MOBA_TOPK)
            neg_s[...] = jnp.where(sel, 0.0, NEG_INF)

            qt_h = jnp.where(row_head == h, qt_s[i], 0.0).astype(BF16)
            far_bias = far_ref[2 * hp + h]

            def scores(j):
                kj = k_ref[0, pl.ds(pl.multiple_of(j * blk, blk), blk), :].astype(BF16)
                return jnp.dot(kj, qt_h, preferred_element_type=F32)

            def pv(j, p_t):
                vt_h = vt_s[j, h * HEAD_DIM:(h + 1) * HEAD_DIM, :].astype(BF16)
                return jnp.dot(vt_h, p_t.astype(BF16), preferred_element_type=F32)

            s = scores(i) + t0_ref[h]
            m_run = jnp.max(s, axis=0, keepdims=True)
            p_t = jnp.exp(s - m_run)
            l_run = jnp.sum(p_t, axis=0, keepdims=True)
            acc = pv(i, p_t)

            def update(j, s, state):
                m_run, l_run, acc = state
                s = s + neg_s[pl.ds(j, 1), :]
                m_new = jnp.maximum(m_run, jnp.max(s, axis=0, keepdims=True))
                alpha = jnp.exp(m_run - m_new)
                p_t = jnp.exp(s - m_new)
                l_new = alpha * l_run + jnp.sum(p_t, axis=0, keepdims=True)
                return m_new, l_new, alpha * acc + pv(j, p_t)

            jp = jnp.maximum(i - 1, 0)
            state = update(jp, scores(jp) + t1_ref[h], (m_run, l_run, acc))

            def far_body(j, state):
                return update(j, scores(j) + far_bias, state)

            m_run, l_run, acc = lax.fori_loop(0, jnp.maximum(i - 1, 0), far_body, state)
            outs.append(acc / l_run)
        o_t = jnp.concatenate(outs, axis=0)
        o_ref[0, pl.ds(pl.multiple_of(i * blk, blk), blk), :] = o_t.T
        return carry

    lax.fori_loop(0, nb, q_block, 0)


def _moba_attention(p3, t_own, t_prev, far):
    b, s, _ = p3.shape
    nb = s // MOBA_BLOCK
    n_pairs = MOBA_HEADS // 2
    blk_spec = lambda off: pl.BlockSpec((1, s, LANES), lambda bi, hp: (bi, 0, off + hp))
    tab_spec = pl.BlockSpec((2, MOBA_BLOCK, MOBA_BLOCK), lambda bi, hp: (hp, 0, 0))
    return pl.pallas_call(
        functools.partial(_moba_kernel, nb=nb),
        grid=(b, n_pairs),
        in_specs=[pl.BlockSpec(memory_space=pltpu.SMEM),
                  blk_spec(0), blk_spec(n_pairs), blk_spec(2 * n_pairs), tab_spec, tab_spec],
        out_specs=pl.BlockSpec((1, s, LANES), lambda bi, hp: (bi, 0, hp)),
        out_shape=jax.ShapeDtypeStruct((b, s, MOBA_W), F32),
        scratch_shapes=[pltpu.VMEM((nb, LANES, MOBA_BLOCK), F32),
                        pltpu.VMEM((nb, LANES, MOBA_BLOCK), F32),
                        pltpu.VMEM((2, nb, nb, MOBA_BLOCK), F32),
                        pltpu.VMEM((nb, MOBA_BLOCK), F32)],
        compiler_params=_cparams(("parallel", "parallel")),
        name="moba",
    )(far, p3, p3, p3, t_own, t_prev)


def _softplus(z):
    return jnp.maximum(z, 0.0) + jnp.log(1.0 + jnp.exp(-jnp.abs(z)))


def _sigmoid(z):
    return 1.0 / (1.0 + jnp.exp(-z))


def _rwkv_kernel(r_ref, k_ref, v_ref, wa_ref, g_ref,
                 mu_r_ref, mu_k_ref, mu_v_ref, mu_wa_ref, mu_g_ref,
                 w0_ref, wup_ref, a0_ref, aup_ref, gup_ref, kk_ref, ka_ref, rk_ref,
                 gng_ref, gnb_ref, hsum_ref, hmask_ref, tri_ref,
                 o_ref, st_s, prev_r, prev_k, prev_v, prev_wa, prev_g):
    c = RWKV_CHUNK
    n_heads = RWKV_HEADS

    @pl.when(pl.program_id(1) == 0)
    def _():
        st_s[...] = jnp.zeros_like(st_s)
        prev_r[...] = jnp.zeros_like(prev_r)
        prev_k[...] = jnp.zeros_like(prev_k)
        prev_v[...] = jnp.zeros_like(prev_v)
        prev_wa[...] = jnp.zeros_like(prev_wa)
        prev_g[...] = jnp.zeros_like(prev_g)

    def mix(x_ref, prev_ref, mu_ref):
        x = x_ref[0]
        row = lax.broadcasted_iota(jnp.int32, x.shape, 0)
        shifted = jnp.where(row == 0, prev_ref[...], pltpu.roll(x, 1, axis=0))
        prev_ref[...] = x[c - 1:c, :]
        return x + (shifted - x) * mu_ref[...]

    r = mix(r_ref, prev_r, mu_r_ref)
    k = mix(k_ref, prev_k, mu_k_ref)
    v = mix(v_ref, prev_v, mu_v_ref)
    xwa = mix(wa_ref, prev_wa, mu_wa_ref)
    xg = mix(g_ref, prev_g, mu_g_ref)

    hsum = hsum_ref[...]
    w = -_softplus(-(w0_ref[...] + _bdot(jnp.tanh(xwa), wup_ref[...]))) - 0.5
    logd = -jnp.exp(w)
    a = _sigmoid(a0_ref[...] + _bdot(xwa, aup_ref[...]))
    gate = _bdot(_sigmoid(xg), gup_ref[...])
    kk = k * kk_ref[...]
    kk = kk / jnp.maximum(jnp.sqrt(_hdot(kk * kk, hsum)), 1e-12)
    k2 = k * (1.0 + (a - 1.0) * ka_ref[...])

    tri = tri_ref[...]
    cum = _hdot(tri, logd)
    cum_last = cum[c - 1:c, :]
    g_in = jnp.exp(cum)
    g_inv = jnp.exp(-cum)
    g_tail = jnp.exp(cum_last - cum)
    a_t = -kk * jnp.exp(cum - logd)
    b_t = kk * a * g_inv
    k_t = k2 * g_inv
    r_t = r * g_in
    b_hat = kk * a * g_tail
    k_hat = k2 * g_tail

    st = st_s[...]
    p0 = _bdot(a_t, st)
    o0 = _bdot(r_t, st)

    row = lax.broadcasted_iota(jnp.int32, (c, c), 0)
    col = lax.broadcasted_iota(jnp.int32, (c, c), 1)
    strict = row > col
    incl = row >= col
    eye = (row == col).astype(F32)
    bk = jnp.concatenate([b_t, k_t], axis=0).astype(BF16)
    lane_head = lax.broadcasted_iota(jnp.int32, (1, RWKV_W), 1) // HEAD_DIM

    u_all = jnp.zeros((c, RWKV_W), F32)
    o_all = jnp.zeros((c, RWKV_W), F32)
    for h in range(n_heads):
        hm = lane_head == h
        ar = jnp.concatenate([jnp.where(hm, a_t, 0.0), jnp.where(hm, r_t, 0.0)], axis=0)
        m4 = _dot_nt(ar.astype(BF16), bk)
        l_ab = jnp.where(strict, m4[:c, :c], 0.0)
        l_ak = jnp.where(strict, m4[:c, c:], 0.0)
        m_rb = jnp.where(incl, m4[c:, :c], 0.0)
        m_rk = jnp.where(incl, m4[c:, c:], 0.0)
        t_inv = eye + l_ab
        pw = l_ab
        span = 2
        while span < c:
            pw = _hdot(pw, pw)
            t_inv = t_inv + _hdot(t_inv, pw)
            span *= 2
        u_h = _hdot(t_inv, p0 + _bdot(l_ak, v))
        u_all = jnp.where(hm, u_h, u_all)
        o_h = _bdot(m_rb, u_h) + _bdot(m_rk, v)
        o_all = jnp.where(hm, o_h, o_all)
    o = o0 + o_all

    upd = _dot_tn(b_hat.astype(BF16), u_all.astype(BF16)) + _dot_tn(k_hat.astype(BF16), v.astype(BF16))
    first_row = (lax.broadcasted_iota(jnp.int32, (c, RWKV_W), 0) == 0)
    g_col = _dot_tn(jnp.where(first_row, jnp.exp(cum_last), 0.0), jnp.ones((c, RWKV_W), F32), precision=HI)
    st_s[...] = g_col * st + hmask_ref[...] * upd

    inv_n = 1.0 / HEAD_DIM
    mean = _hdot(o, hsum) * inv_n
    dev = o - mean
    var = _hdot(dev * dev, hsum) * inv_n
    y = dev * lax.rsqrt(var + RWKV_GN_EPS) * gng_ref[...] + gnb_ref[...]
    y = y + _hdot(r * k2 * rk_ref[...], hsum) * v
    o_ref[0] = y * gate


def _rwkv(p3, mu, w0, w_up, a0, a_up, g_up, k_k, k_a, r_k, gn_g, gn_b):
    b, s, _ = p3.shape
    c = RWKV_CHUNK
    w = RWKV_W
    base = 3 * MOBA_W
    assert base % w == 0 and (base + 3 * w) % LANES == 0
    lora = DECAY_LORA + AAA_LORA

    def col_spec(width, off_cols):
        assert off_cols % width == 0
        return pl.BlockSpec((1, c, width), lambda bi, ci: (bi, ci, off_cols // width))

    row = lambda x: x.reshape(1, -1).astype(F32)
    const = lambda shape: pl.BlockSpec(shape, lambda bi, ci: (0,) * len(shape))
    wup_pad = jnp.concatenate([w_up, jnp.zeros((AAA_LORA, w), F32)], axis=0).astype(BF16)
    aup_pad = jnp.concatenate([jnp.zeros((DECAY_LORA, w), F32), a_up], axis=0).astype(BF16)
    head = np.arange(w) // HEAD_DIM
    hsum = jnp.asarray((head[:, None] == head[None, :]).astype(np.float32))
    tri = jnp.asarray(np.tril(np.ones((c, c), np.float32)))
    vec_args = [mu[:w], mu[w:2 * w], mu[2 * w:3 * w], mu[3 * w:3 * w + lora], mu[3 * w + lora:],
                w0, None, a0, None, None, k_k, k_a, r_k.reshape(-1), gn_g, gn_b]
    args = [p3, p3, p3, p3, p3]
    specs = [col_spec(w, base), col_spec(w, base + w), col_spec(w, base + 2 * w),
             col_spec(lora, base + 3 * w), col_spec(GATE_LORA, base + 3 * w + lora)]
    mats = {6: wup_pad, 8: aup_pad, 9: g_up.astype(BF16)}
    for idx, a in enumerate(vec_args):
        arr = mats[idx] if a is None else row(a)
        args.append(arr)
        specs.append(const(arr.shape))
    for arr in (hsum, hsum, tri):
        args.append(arr)
        specs.append(const(arr.shape))
    return pl.pallas_call(
        _rwkv_kernel,
        grid=(b, s // c),
        in_specs=specs,
        out_specs=pl.BlockSpec((1, c, w), lambda bi, ci: (bi, ci, 0)),
        out_shape=jax.ShapeDtypeStruct((b, s, w), F32),
        scratch_shapes=[pltpu.VMEM((w, w), F32), pltpu.VMEM((1, w), F32), pltpu.VMEM((1, w), F32),
                        pltpu.VMEM((1, w), F32), pltpu.VMEM((1, lora), F32),
                        pltpu.VMEM((1, GATE_LORA), F32)],
        compiler_params=_cparams(("parallel", "arbitrary")),
        name="rwkv",
    )(*args)


def _split_bf16(x, terms):
    parts = []
    for _ in range(terms):
        hi = x.astype(BF16)
        parts.append(hi)
        x = x - hi.astype(F32)
    return parts


def _dot_exact_rhs(x, m_bf16, terms):
    acc = None
    for part in _split_bf16(x, terms):
        d = jnp.dot(part, m_bf16, preferred_element_type=F32)
        acc = d if acc is None else acc + d
    return acc


def _dot_parts(a_parts, b_parts):
    dot = functools.partial(jnp.dot, preferred_element_type=F32)
    acc = dot(a_parts[0], b_parts[0])
    if len(a_parts) > 1:
        acc = acc + (dot(a_parts[0], b_parts[1]) + dot(a_parts[1], b_parts[0]))
    return acc


def _rwkv2_kernel(r_ref, k_ref, v_ref, wa_ref, g_ref,
                  mu_r_ref, mu_k_ref, mu_v_ref, mu_wa_ref, mu_g_ref,
                  w0_ref, wup_ref, a0_ref, aup_ref, gup_ref, kk_ref, ka_ref, rk_ref,
                  gng_ref, gnb_ref, hsum_ref, tri_ref,
                  o_ref, st_s, prev_r, prev_k, prev_v, prev_wa, prev_g, *, n_seq):
    c = RWKV_CHUNK
    mid = c // 2

    @pl.when(pl.program_id(1) == 0)
    def _():
        st_s[...] = jnp.zeros_like(st_s)
        prev_r[...] = jnp.zeros_like(prev_r)
        prev_k[...] = jnp.zeros_like(prev_k)
        prev_v[...] = jnp.zeros_like(prev_v)
        prev_wa[...] = jnp.zeros_like(prev_wa)
        prev_g[...] = jnp.zeros_like(prev_g)

    hsum = hsum_ref[...]
    tri = tri_ref[...]
    row = lax.broadcasted_iota(jnp.int32, (c, c), 0)
    col = lax.broadcasted_iota(jnp.int32, (c, c), 1)
    strict = row > col
    incl = row >= col
    eye = (row == col).astype(F32)
    lane_head = lax.broadcasted_iota(jnp.int32, (1, RWKV_W), 1) // HEAD_DIM
    hmask = hsum.astype(F32)

    def mix(g, x_ref, prev_ref, mu_ref):
        x = x_ref[g]
        rows = lax.broadcasted_iota(jnp.int32, x.shape, 0)
        shifted = jnp.where(rows == 0, prev_ref[g], pltpu.roll(x, 1, axis=0))
        prev_ref[g] = x[c - 1:c, :]
        return x + (shifted - x) * mu_ref[...]

    def prepare(g):
        r = mix(g, r_ref, prev_r, mu_r_ref)
        k = mix(g, k_ref, prev_k, mu_k_ref)
        v = mix(g, v_ref, prev_v, mu_v_ref)
        xwa = mix(g, wa_ref, prev_wa, mu_wa_ref)
        xg = mix(g, g_ref, prev_g, mu_g_ref)

        w = -_softplus(-(w0_ref[...] + _bdot(jnp.tanh(xwa), wup_ref[...]))) - 0.5
        logd = -jnp.exp(w)
        a = _sigmoid(a0_ref[...] + _bdot(xwa, aup_ref[...]))
        gate = _bdot(_sigmoid(xg), gup_ref[...])
        kk = k * kk_ref[...]
        kk = kk / jnp.maximum(jnp.sqrt(_dot_exact_rhs(kk * kk, hsum, 2)), 1e-12)
        k2 = k * (1.0 + (a - 1.0) * ka_ref[...])

        parts = _split_bf16(logd, 3)
        cum = (jnp.dot(tri, parts[0], preferred_element_type=F32)
               + jnp.dot(tri, parts[1], preferred_element_type=F32)
               + jnp.dot(tri, parts[2], preferred_element_type=F32))
        cum_last = cum[c - 1:c, :]
        ref = cum[mid - 1:mid, :]
        rel = cum - ref
        g_in = jnp.exp(rel)
        g_inv = jnp.exp(-rel)
        g_tail = jnp.exp(cum_last - cum)
        a_t = -kk * jnp.exp(rel - logd)
        b_t = kk * a * g_inv
        k_t = k2 * g_inv
        r_t = r * g_in
        b_hat = kk * a * g_tail
        k_hat = k2 * g_tail

        st = st_s[g]
        st_ref = (st * jnp.exp(ref)).astype(BF16)
        p0 = _dot_nt(a_t.astype(BF16), st_ref)
        o0 = _dot_nt(r_t.astype(BF16), st_ref)

        bk = jnp.concatenate([b_t, k_t], axis=0).astype(BF16)
        v_bf = v.astype(BF16)
        quads = []
        for h in range(RWKV_HEADS):
            hm = lane_head == h
            ar = jnp.concatenate([jnp.where(hm, a_t, 0.0), jnp.where(hm, r_t, 0.0)], axis=0)
            m4 = _dot_nt(ar.astype(BF16), bk)
            quads.append((jnp.where(strict, m4[:c, :c], 0.0), jnp.where(strict, m4[:c, c:], 0.0),
                          jnp.where(incl, m4[c:, :c], 0.0).astype(BF16),
                          jnp.where(incl, m4[c:, c:], 0.0).astype(BF16)))
        return dict(r=r, k2=k2, v=v, v_bf=v_bf, gate=gate, p0=p0, o0=o0, b_hat=b_hat, k_hat=k_hat,
                    st=st, decay=jnp.exp(cum_last), quads=quads)

    seqs = [prepare(g) for g in range(n_seq)]
    pairs = [(g, h) for g in range(n_seq) for h in range(RWKV_HEADS)]

    pw = {gh: seqs[gh[0]]["quads"][gh[1]][0] for gh in pairs}
    t_inv = {gh: eye + pw[gh] for gh in pairs}
    span = 2
    pw_parts = {gh: _split_bf16(pw[gh], RWKV_INV_TERMS) for gh in pairs}
    while span < c:
        pw = {gh: _dot_parts(pw_parts[gh], pw_parts[gh]) for gh in pairs}
        pw_parts = {gh: _split_bf16(pw[gh], RWKV_INV_TERMS) for gh in pairs}
        t_parts = {gh: _split_bf16(t_inv[gh], RWKV_INV_TERMS) for gh in pairs}
        t_inv = {gh: t_inv[gh] + _dot_parts(t_parts[gh], pw_parts[gh]) for gh in pairs}
        span *= 2

    w_loc = {(g, h): seqs[g]["p0"] + jnp.dot(seqs[g]["quads"][h][1].astype(BF16), seqs[g]["v_bf"],
                                             preferred_element_type=F32) for g, h in pairs}
    u_h = {gh: _dot_parts(_split_bf16(t_inv[gh], RWKV_INV_TERMS), _split_bf16(w_loc[gh], RWKV_INV_TERMS))
           for gh in pairs}
    o_h = {(g, h): (jnp.dot(seqs[g]["quads"][h][2], u_h[(g, h)].astype(BF16), preferred_element_type=F32)
                    + jnp.dot(seqs[g]["quads"][h][3], seqs[g]["v_bf"], preferred_element_type=F32))
           for g, h in pairs}

    for g in range(n_seq):
        sq = seqs[g]
        u_all = jnp.zeros((c, RWKV_W), F32)
        o_all = jnp.zeros((c, RWKV_W), F32)
        for h in range(RWKV_HEADS):
            hm = lane_head == h
            u_all = jnp.where(hm, u_h[(g, h)], u_all)
            o_all = jnp.where(hm, o_h[(g, h)], o_all)
        o = sq["o0"] + o_all
        upd = (_dot_tn(u_all.astype(BF16), sq["b_hat"].astype(BF16))
               + _dot_tn(sq["v_bf"], sq["k_hat"].astype(BF16)))
        st_s[g] = sq["st"] * sq["decay"] + hmask * upd

        inv_n = 1.0 / HEAD_DIM
        mean = _dot_exact_rhs(o, hsum, 2) * inv_n
        dev = o - mean
        var = _dot_exact_rhs(dev * dev, hsum, 2) * inv_n
        y = dev * lax.rsqrt(var + RWKV_GN_EPS) * gng_ref[...] + gnb_ref[...]
        y = y + _dot_exact_rhs(sq["r"] * sq["k2"] * rk_ref[...], hsum, 2) * sq["v"]
        o_ref[g] = y * sq["gate"]


def _rwkv2(p3, mu, w0, w_up, a0, a_up, g_up, k_k, k_a, r_k, gn_g, gn_b):
    b, s, _ = p3.shape
    c = RWKV_CHUNK
    w = RWKV_W
    n_seq = RWKV_SEQS_PER_STEP if b % RWKV_SEQS_PER_STEP == 0 else 1
    base = 3 * MOBA_W
    assert base % w == 0 and (base + 3 * w) % LANES == 0
    lora = DECAY_LORA + AAA_LORA

    def col_spec(width, off_cols):
        assert off_cols % width == 0
        return pl.BlockSpec((n_seq, c, width), lambda bi, ci: (bi, ci, off_cols // width))

    row = lambda x: x.reshape(1, -1).astype(F32)
    const = lambda shape: pl.BlockSpec(shape, lambda bi, ci: (0,) * len(shape))
    wup_pad = jnp.concatenate([w_up, jnp.zeros((AAA_LORA, w), F32)], axis=0).astype(BF16)
    aup_pad = jnp.concatenate([jnp.zeros((DECAY_LORA, w), F32), a_up], axis=0).astype(BF16)
    head = np.arange(w) // HEAD_DIM
    hsum = jnp.asarray((head[:, None] == head[None, :]).astype(np.float32)).astype(BF16)
    tri = jnp.asarray(np.tril(np.ones((c, c), np.float32))).astype(BF16)
    vec_args = [mu[:w], mu[w:2 * w], mu[2 * w:3 * w], mu[3 * w:3 * w + lora], mu[3 * w + lora:],
                w0, None, a0, None, None, k_k, k_a, r_k.reshape(-1), gn_g, gn_b]
    args = [p3, p3, p3, p3, p3]
    specs = [col_spec(w, base), col_spec(w, base + w), col_spec(w, base + 2 * w),
             col_spec(lora, base + 3 * w), col_spec(GATE_LORA, base + 3 * w + lora)]
    mats = {6: wup_pad, 8: aup_pad, 9: g_up.astype(BF16)}
    for idx, a in enumerate(vec_args):
        arr = mats[idx] if a is None else row(a)
        args.append(arr)
        specs.append(const(arr.shape))
    for arr in (hsum, tri):
        args.append(arr)
        specs.append(const(arr.shape))
    return pl.pallas_call(
        functools.partial(_rwkv2_kernel, n_seq=n_seq),
        grid=(b // n_seq, s // c),
        in_specs=specs,
        out_specs=pl.BlockSpec((n_seq, c, w), lambda bi, ci: (bi, ci, 0)),
        out_shape=jax.ShapeDtypeStruct((b, s, w), F32),
        scratch_shapes=[pltpu.VMEM((n_seq, w, w), F32), pltpu.VMEM((n_seq, 1, w), F32),
                        pltpu.VMEM((n_seq, 1, w), F32), pltpu.VMEM((n_seq, 1, w), F32),
                        pltpu.VMEM((n_seq, 1, lora), F32), pltpu.VMEM((n_seq, 1, GATE_LORA), F32)],
        compiler_params=_cparams(("parallel", "arbitrary")),
        name="rwkv",
    )(*args)


def _mem_attn_kernel(q_ref, kv_ref, o_ref):
    scale = HEAD_DIM ** -0.5
    lane_head = lax.broadcasted_iota(jnp.int32, (1, MEM_W), 1) // HEAD_DIM
    q = q_ref[0] * scale
    mk = kv_ref[0, :, :MEM_W].astype(BF16)
    mv = kv_ref[0, :, MEM_W:].astype(BF16)
    out = jnp.zeros(q.shape, F32)
    for h in range(MEM_HEADS):
        hm = lane_head == h
        s = _dot_nt(jnp.where(hm, q, 0.0).astype(BF16), mk)
        s = s - jnp.max(s, axis=-1, keepdims=True)
        e = jnp.exp(s)
        p = e / jnp.sum(e, axis=-1, keepdims=True)
        out = jnp.where(hm, jnp.dot(p.astype(BF16), mv, preferred_element_type=F32), out)
    o_ref[0] = out


def _mem_attention(p3, mkv3, tq):
    b, s, _ = p3.shape
    m = mkv3.shape[1]
    off = (3 * MOBA_W + RWKV_COLS) // MEM_W
    assert off * MEM_W == 3 * MOBA_W + RWKV_COLS
    return pl.pallas_call(
        _mem_attn_kernel,
        grid=(b, s // tq),
        in_specs=[pl.BlockSpec((1, tq, MEM_W), lambda bi, i: (bi, i, off)),
                  pl.BlockSpec((1, m, 2 * MEM_W), lambda bi, i: (bi, 0, 0))],
        out_specs=pl.BlockSpec((1, tq, MEM_W), lambda bi, i: (bi, i, 0)),
        out_shape=jax.ShapeDtypeStruct((b, s, MEM_W), F32),
        compiler_params=_cparams(("parallel", "parallel")),
        name="mem_attn",
    )(p3, mkv3)


def _out_router_kernel(x_ref, ym_ref, yr_ref, ye_ref, wo1_ref, wo2_ref, wo3_ref, g_ref,
                       wr_ref, br_ref, ltri_ref,
                       x1_ref, h_ref, idx_ref, wgt_ref, rank_ref, cnt_ref, run_s):
    @pl.when(pl.program_id(0) == 0)
    def _():
        run_s[...] = jnp.zeros_like(run_s)

    x1 = (x_ref[...] + _bdot(ym_ref[...], wo1_ref[...]) + _bdot(yr_ref[...], wo2_ref[...])
          + _bdot(ye_ref[...], wo3_ref[...]))
    x1_ref[...] = x1
    h = _rms(x1, g_ref[...])
    h_ref[...] = h.astype(BF16)
    logits = _hdot(h, wr_ref[...]) + br_ref[...]
    tm = logits.shape[0]
    lane = lax.broadcasted_iota(jnp.int32, (tm, LANES), 1)

    vals, idxs = [], []
    lg = logits
    for _ in range(TOP_K):
        m = jnp.max(lg, axis=1, keepdims=True)
        idx = jnp.min(jnp.where(lg == m, lane, LANES), axis=1, keepdims=True)
        vals.append(m)
        idxs.append(idx)
        lg = jnp.where(lane == idx, NEG_INF, lg)
    exps = [jnp.exp(vk - vals[0]) for vk in vals]
    denom = exps[0] + exps[1] + exps[2] + exps[3]

    chosen = jnp.zeros((tm, LANES), F32)
    for idx in idxs:
        chosen = chosen + jnp.where(lane == idx, 1.0, 0.0)
    before = jnp.dot(ltri_ref[...], chosen.astype(BF16), preferred_element_type=F32) + run_s[...]
    run_s[...] = run_s[...] + jnp.sum(chosen, axis=0, keepdims=True)
    cnt_ref[...] = run_s[...]

    idx_out = jnp.zeros((tm, LANES), jnp.int32)
    wgt_out = jnp.zeros((tm, LANES), F32)
    rank_out = jnp.zeros((tm, LANES), jnp.int32)
    for kk in range(TOP_K):
        rank = jnp.sum(jnp.where(lane == idxs[kk], before, 0.0), axis=1, keepdims=True)
        idx_out = jnp.where(lane == kk, idxs[kk], idx_out)
        wgt_out = jnp.where(lane == kk, exps[kk] / denom, wgt_out)
        rank_out = jnp.where(lane == kk, rank.astype(jnp.int32), rank_out)
    idx_ref[...] = idx_out
    wgt_ref[...] = wgt_out
    rank_ref[...] = rank_out


def _out_router(x2, ym, yr, ye, w_out, g_ffn, w_router, b_router, tm):
    t, d = x2.shape
    wo = w_out.astype(BF16)
    wo1, wo2, wo3 = wo[:MOBA_W], wo[MOBA_W:MOBA_W + RWKV_W], wo[MOBA_W + RWKV_W:]
    wr = jnp.zeros((d, LANES), F32).at[:, :N_EXPERTS].set(w_router)
    br = jnp.full((1, LANES), NEG_INF, F32).at[0, :N_EXPERTS].set(b_router)
    ltri = jnp.asarray(np.tril(np.ones((tm, tm), np.float32), -1)).astype(BF16)
    tile = lambda n: pl.BlockSpec((tm, n), lambda i: (i, 0))
    const = lambda a: pl.BlockSpec(a.shape, lambda i: (0,) * a.ndim)
    g2 = g_ffn.reshape(1, d)
    return pl.pallas_call(
        _out_router_kernel,
        grid=(t // tm,),
        in_specs=[tile(d), tile(MOBA_W), tile(RWKV_W), tile(MEM_W), const(wo1), const(wo2), const(wo3),
                  const(g2), const(wr), const(br), const(ltri)],
        out_specs=[tile(d), tile(d), tile(LANES), tile(LANES), tile(LANES),
                   pl.BlockSpec((1, LANES), lambda i: (0, 0))],
        out_shape=[jax.ShapeDtypeStruct((t, d), F32), jax.ShapeDtypeStruct((t, d), BF16),
                   jax.ShapeDtypeStruct((t, LANES), jnp.int32), jax.ShapeDtypeStruct((t, LANES), F32),
                   jax.ShapeDtypeStruct((t, LANES), jnp.int32), jax.ShapeDtypeStruct((1, LANES), F32)],
        scratch_shapes=[pltpu.VMEM((1, LANES), F32)],
        compiler_params=_cparams(("arbitrary",)),
        name="out_router",
    )(x2, ym, yr, ye, wo1, wo2, wo3, g2, wr, br, ltri)


def _experts_kernel(be_ref, nused_ref, xs_ref, wgu_ref, bg_ref, bu_ref, wd_ref, bd_ref, perm_ref,
                    o_ref, wg_s, wu_s, wd_s):
    i = pl.program_id(0)
    prev = be_ref[jnp.maximum(i - 1, 0)]
    changed = (i == 0) | (be_ref[i] != prev)

    @pl.when(changed & (i < nused_ref[0]))
    def _():
        half = LANES
        for cblk in range(2 * D_EXPERT // (2 * half)):
            wt = wgu_ref[0, :, cblk * 2 * half:(cblk + 1) * 2 * half].astype(BF16)
            sep = jnp.dot(wt, perm_ref[...], preferred_element_type=F32).astype(BF16)
            wg_s[:, cblk * half:(cblk + 1) * half] = sep[:, :half]
            wu_s[:, cblk * half:(cblk + 1) * half] = sep[:, half:]
        wd_s[...] = wd_ref[0].astype(BF16)

    @pl.when(i < nused_ref[0])
    def _():
        xb = xs_ref[...]
        gate = jnp.dot(xb, wg_s[...], preferred_element_type=F32) + bg_ref[0]
        up = jnp.dot(xb, wu_s[...], preferred_element_type=F32) + bu_ref[0]
        gate = jnp.minimum(gate, SWIGLU_LIMIT)
        up = jnp.clip(up, -SWIGLU_LIMIT, SWIGLU_LIMIT)
        glu = gate * _sigmoid(gate * SWIGLU_ALPHA)
        act = ((up + 1.0) * glu).astype(BF16)
        o_ref[...] = jnp.dot(act, wd_s[...], preferred_element_type=F32) + bd_ref[0]

    @pl.when(i >= nused_ref[0])
    def _():
        o_ref[...] = jnp.zeros_like(o_ref)


def _experts(blk_expert, n_used, xs, w_gate_up, b_gate_up, w_down, b_down):
    p_rows, d = xs.shape
    n_blocks = p_rows // EXPERT_ROWS
    bg = b_gate_up[:, 0::2].reshape(N_EXPERTS, 1, D_EXPERT)
    bu = b_gate_up[:, 1::2].reshape(N_EXPERTS, 1, D_EXPERT)
    bd = b_down.reshape(N_EXPERTS, 1, d)
    perm_np = np.zeros((2 * LANES, 2 * LANES), np.float32)
    perm_np[2 * np.arange(LANES), np.arange(LANES)] = 1.0
    perm_np[2 * np.arange(LANES) + 1, LANES + np.arange(LANES)] = 1.0
    perm = jnp.asarray(perm_np).astype(BF16)
    by_expert = lambda shape: pl.BlockSpec((1,) + shape, lambda i, be, nu: (be[i], 0, 0))
    grid_spec = pltpu.PrefetchScalarGridSpec(
        num_scalar_prefetch=2,
        grid=(n_blocks,),
        in_specs=[pl.BlockSpec((EXPERT_ROWS, d), lambda i, be, nu: (i, 0)),
                  by_expert((d, 2 * D_EXPERT)), by_expert((1, D_EXPERT)), by_expert((1, D_EXPERT)),
                  by_expert((D_EXPERT, d)), by_expert((1, d)),
                  pl.BlockSpec(perm.shape, lambda i, be, nu: (0, 0))],
        out_specs=pl.BlockSpec((EXPERT_ROWS, d), lambda i, be, nu: (i, 0)),
        scratch_shapes=[pltpu.VMEM((d, D_EXPERT), BF16), pltpu.VMEM((d, D_EXPERT), BF16),
                        pltpu.VMEM((D_EXPERT, d), BF16)],
    )
    return pl.pallas_call(
        _experts_kernel,
        grid_spec=grid_spec,
        out_shape=jax.ShapeDtypeStruct((p_rows, d), F32),
        compiler_params=_cparams(("arbitrary",)),
        name="experts",
    )(blk_expert, n_used, xs, w_gate_up, bg, bu, w_down, bd, perm)


def _combine_kernel(x1_ref, yg_ref, w_ref, g_ref, o_ref):
    d = x1_ref.shape[1]
    acc = x1_ref[...]
    wts = w_ref[...]
    for kk in range(TOP_K):
        acc = acc + yg_ref[:, kk * d:(kk + 1) * d] * wts[:, kk:kk + 1]
    o_ref[...] = _rms(acc, g_ref[...])


def _combine(x1, yg, wts, g_final, tm):
    t, d = x1.shape
    return pl.pallas_call(
        _combine_kernel,
        grid=(t // tm,),
        in_specs=[pl.BlockSpec((tm, d), lambda i: (i, 0)),
                  pl.BlockSpec((tm, TOP_K * d), lambda i: (i, 0)),
                  pl.BlockSpec((tm, LANES), lambda i: (i, 0)),
                  pl.BlockSpec((1, d), lambda i: (0, 0))],
        out_specs=pl.BlockSpec((tm, d), lambda i: (i, 0)),
        out_shape=jax.ShapeDtypeStruct((t, d), F32),
        compiler_params=_cparams(("parallel",)),
        name="combine",
    )(x1, yg, wts, g_final.reshape(1, d))


def _layer(x, mem, w_in, w_out, w_mem_kv, g_mix, g_mem, g_ffn, bias_tables, mu, w0, w_up, a0, a_up,
           g_up, k_k, k_a, r_k, gn_g, gn_b, w_router, b_router, w_gate_up, b_gate_up, w_down, b_down,
           g_last):
    b, s, d = x.shape
    m = mem.shape[1]
    t = b * s
    x2 = x.reshape(t, d)

    p = _norm_matmul(x2, g_mix, w_in.astype(BF16), 512, "in_proj")
    p3 = p.reshape(b, s, IN_COLS)
    y_moba = _moba_attention(p3, *bias_tables)
    y_rwkv = _rwkv2(p3, mu, w0, w_up, a0, a_up, g_up, k_k, k_a, r_k, gn_g, gn_b)
    mkv = _norm_matmul(mem.reshape(b * m, d), g_mem, w_mem_kv.astype(BF16), 512, "mem_kv")
    y_mem = _mem_attention(p3, mkv.reshape(b, m, 2 * MEM_W), 512)

    x1, h2, idx_p, wgt_p, rank_p, cnt = _out_router(
        x2, y_moba.reshape(t, MOBA_W), y_rwkv.reshape(t, RWKV_W), y_mem.reshape(t, MEM_W),
        w_out, g_ffn, w_router, b_router, 256)

    top_idx = idx_p[:, :TOP_K]
    counts = cnt[0, :N_EXPERTS].astype(jnp.int32)
    padded = (counts + EXPERT_ROWS - 1) // EXPERT_ROWS * EXPERT_ROWS
    pad_ends = jnp.cumsum(padded)
    pad_starts = pad_ends - padded
    dest = pad_starts[top_idx] + rank_p[:, :TOP_K]
    n_blocks = (t * TOP_K) // EXPERT_ROWS + N_EXPERTS
    blk_start = jnp.arange(n_blocks, dtype=jnp.int32) * EXPERT_ROWS
    blk_expert = jnp.minimum(jnp.searchsorted(pad_ends, blk_start, side='right'),
                             N_EXPERTS - 1).astype(jnp.int32)
    n_used = (pad_ends[-1:] // EXPERT_ROWS).astype(jnp.int32)
    tok = jnp.broadcast_to(jnp.arange(t, dtype=jnp.int32)[:, None], (t, TOP_K))
    row_tok = jnp.zeros((n_blocks * EXPERT_ROWS,), jnp.int32).at[dest.reshape(-1)].set(tok.reshape(-1))
    xs = h2[row_tok]

    ys = _experts(blk_expert, n_used, xs, w_gate_up, b_gate_up, w_down, b_down)
    yg = ys[dest.reshape(-1)].reshape(t, TOP_K * d)
    out = _combine(x1, yg, wgt_p, g_last, 256)
    return out.reshape(b, s, d)


def kernel(x, mem, w_in, w_out, w_mem_kv, g_mix, g_mem, g_ffn, g_final, rel_bias, rwkv_mu, rwkv_w0,
           rwkv_w_up, rwkv_a0, rwkv_a_up, rwkv_g_up, rwkv_k_k, rwkv_k_a, rwkv_r_k, rwkv_gn_g, rwkv_gn_b,
           w_router, b_router, w_gate_up, b_gate_up, w_down, b_down):
    depth = w_in.shape[0]
    assert depth == 1, "the final norm is fused into the last layer's combine kernel"
    bias_tables = _moba_bias_tables(rel_bias)
    l = 0
    return _layer(x, mem, w_in[l], w_out[l], w_mem_kv[l], g_mix[l], g_mem[l], g_ffn[l], bias_tables,
                  rwkv_mu[l], rwkv_w0[l], rwkv_w_up[l], rwkv_a0[l], rwkv_a_up[l], rwkv_g_up[l],
                  rwkv_k_k[l], rwkv_k_a[l], rwkv_r_k[l], rwkv_gn_g[l], rwkv_gn_b[l], w_router[l],
                  b_router[l], w_gate_up[l], b_gate_up[l], w_down[l], b_down[l], g_final)
```

```python
import functools
import math

import numpy as np
import jax
import jax.numpy as jnp
from jax import lax
from jax.experimental import pallas as pl
from jax.experimental.pallas import tpu as pltpu

F32 = jnp.float32
BF16 = jnp.bfloat16
HI = lax.Precision.HIGHEST

D_MODEL = 1024
HEAD_DIM = 64
MOBA_HEADS = 6
RWKV_HEADS = 6
MEM_HEADS = 4
MOBA_W = MOBA_HEADS * HEAD_DIM
RWKV_W = RWKV_HEADS * HEAD_DIM
MEM_W = MEM_HEADS * HEAD_DIM
MOBA_BLOCK = 256
MOBA_TOPK = 3
N_BUCKETS = 32
MAX_DISTANCE = 128
DECAY_LORA = 64
AAA_LORA = 64
GATE_LORA = 128
RWKV_COLS = 3 * RWKV_W + DECAY_LORA + AAA_LORA + GATE_LORA
RWKV_GN_EPS = 64e-5
IN_COLS = 3 * MOBA_W + RWKV_COLS + MEM_W
N_EXPERTS = 32
TOP_K = 4
D_EXPERT = D_MODEL
SWIGLU_ALPHA = 1.702
SWIGLU_LIMIT = 7.0
RMS_EPS = 1e-5

LANES = 128
RWKV_CHUNK = 128
RWKV_SEQS_PER_STEP = 2
RWKV_INV_TERMS = 1
EXPERT_ROWS = 256
ROUTE_TOKENS = 256
VMEM_LIMIT = 56 * 1024 * 1024
NEG_INF = float("-inf")


def _cparams(sem):
    return pltpu.CompilerParams(dimension_semantics=sem, vmem_limit_bytes=VMEM_LIMIT)


def _rms(x, g):
    return x * lax.rsqrt(jnp.mean(x * x, axis=-1, keepdims=True) + RMS_EPS) * g


def _bdot(a, b):
    return jnp.dot(a.astype(BF16), b.astype(BF16), preferred_element_type=F32)


def _hdot(a, b):
    return jnp.dot(a, b, preferred_element_type=F32, precision=HI)


def _dot_nt(a, b, **kw):
    return lax.dot_general(a, b, (((1,), (1,)), ((), ())), preferred_element_type=F32, **kw)


def _dot_tn(a, b, **kw):
    return lax.dot_general(a, b, (((0,), (0,)), ((), ())), preferred_element_type=F32, **kw)


def _pack_halves(x):
    n = x.shape[1] // 2
    lo = pltpu.bitcast(x[:, :n].astype(BF16).astype(F32), jnp.uint32)
    hi = pltpu.bitcast(x[:, n:].astype(BF16).astype(F32), jnp.uint32)
    return (hi & jnp.uint32(0xFFFF0000)) | (lo >> 16)


def _unpack_halves(w):
    lo = pltpu.bitcast(w << 16, F32)
    hi = pltpu.bitcast(w & jnp.uint32(0xFFFF0000), F32)
    return jnp.concatenate([lo, hi], axis=1)


def _norm_matmul_kernel(x_ref, g_ref, w_ref, o_ref):
    h = _rms(x_ref[...], g_ref[...])
    o_ref[...] = jnp.dot(h.astype(BF16), w_ref[...], preferred_element_type=F32)


def _norm_matmul(x, g, w_bf16, tm, name):
    t, d = x.shape
    n = w_bf16.shape[1]
    tm = min(tm, t)
    return pl.pallas_call(
        _norm_matmul_kernel,
        grid=(t // tm,),
        in_specs=[pl.BlockSpec((tm, d), lambda i: (i, 0)),
                  pl.BlockSpec((1, d), lambda i: (0, 0)),
                  pl.BlockSpec((d, n), lambda i: (0, 0))],
        out_specs=pl.BlockSpec((tm, n), lambda i: (i, 0)),
        out_shape=jax.ShapeDtypeStruct((t, n), F32),
        compiler_params=_cparams(("parallel",)),
        name=name,
    )(x, g.reshape(1, d), w_bf16)


def _t5_bucket_np(dist):
    n = np.maximum(dist, 0)
    max_exact = N_BUCKETS // 2
    nf = np.maximum(n, 1).astype(np.float64)
    large = max_exact + (np.log(nf / max_exact) / math.log(MAX_DISTANCE / max_exact)
                         * (N_BUCKETS - max_exact)).astype(np.int64)
    large = np.minimum(large, N_BUCKETS - 1)
    return np.where(n < max_exact, n, large)


def _moba_bias_tables(rel_bias):
    kk = np.arange(MOBA_BLOCK)[:, None]
    qq = np.arange(MOBA_BLOCK)[None, :]
    d_own = qq - kk
    d_prev = MOBA_BLOCK + qq - kk
    assert np.all(_t5_bucket_np(np.arange(MOBA_BLOCK + 1, 64 * MOBA_BLOCK)) == N_BUCKETS - 1)
    bias_t = rel_bias.astype(F32).T
    t_own = bias_t[:, _t5_bucket_np(d_own)]
    t_own = jnp.where(jnp.asarray(d_own >= 0)[None], t_own, NEG_INF)
    t_prev = bias_t[:, _t5_bucket_np(d_prev)]
    far = bias_t[:, N_BUCKETS - 1]
    return t_own, t_prev, far


def _moba_kernel(far_ref, q_ref, k_ref, v_ref, t0_ref, t1_ref, o_ref,
                 qt_s, vt_s, gate_s, neg_s, *, nb):
    hp = pl.program_id(1)
    blk = MOBA_BLOCK
    scale = HEAD_DIM ** -0.5
    row_head = lax.broadcasted_iota(jnp.int32, (LANES, 1), 0) // HEAD_DIM
    lane_head = lax.broadcasted_iota(jnp.int32, (1, LANES), 1) // HEAD_DIM

    kmean_rows = []
    for j in range(nb):
        sl = pl.ds(j * blk, blk)
        qt_s[j] = (q_ref[0, sl, :] * scale).T
        vt_s[j] = v_ref[0, sl, :].T
        kmean_rows.append(jnp.mean(k_ref[0, sl, :], axis=0, keepdims=True))
    kmean = jnp.concatenate(kmean_rows, axis=0)
    for h in range(2):
        km_h = jnp.where(lane_head == h, kmean, 0.0)
        for i in range(nb):
            gate_s[h, i] = _hdot(km_h, qt_s[i])

    blk_iota = lax.broadcasted_iota(jnp.int32, (nb, blk), 0)

    def q_block(i, carry):
        outs = []
        for h in range(2):
            g = gate_s[h, i]
            cnt = jnp.zeros((nb, blk), jnp.int32)
            for m in range(nb):
                gm = g[m:m + 1, :]
                beats = (gm > g) | ((gm == g) & (m < blk_iota))
                cnt = cnt + jnp.where(beats & (m < i), 1, 0)
            sel = (blk_iota < i) & (cnt < MOBA_TOPK)
            neg_s[...] = jnp.where(sel, 0.0, NEG_INF)

            qt_h = jnp.where(row_head == h, qt_s[i], 0.0).astype(BF16)
            far_bias = far_ref[2 * hp + h]

            def scores(j):
                kj = k_ref[0, pl.ds(pl.multiple_of(j * blk, blk), blk), :].astype(BF16)
                return jnp.dot(kj, qt_h, preferred_element_type=F32)

            def pv(j, p_t):
                vt_h = vt_s[j, h * HEAD_DIM:(h + 1) * HEAD_DIM, :].astype(BF16)
                return jnp.dot(vt_h, p_t.astype(BF16), preferred_element_type=F32)

            s = scores(i) + t0_ref[h]
            m_run = jnp.max(s, axis=0, keepdims=True)
            p_t = jnp.exp(s - m_run)
            l_run = jnp.sum(p_t, axis=0, keepdims=True)
            acc = pv(i, p_t)

            def update(j, s, state):
                m_run, l_run, acc = state
                s = s + neg_s[pl.ds(j, 1), :]
                m_new = jnp.maximum(m_run, jnp.max(s, axis=0, keepdims=True))
                alpha = jnp.exp(m_run - m_new)
                p_t = jnp.exp(s - m_new)
                l_new = alpha * l_run + jnp.sum(p_t, axis=0, keepdims=True)
                return m_new, l_new, alpha * acc + pv(j, p_t)

            jp = jnp.maximum(i - 1, 0)
            state = update(jp, scores(jp) + t1_ref[h], (m_run, l_run, acc))

            def far_body(j, state):
                return update(j, scores(j) + far_bias, state)

            m_run, l_run, acc = lax.fori_loop(0, jnp.maximum(i - 1, 0), far_body, state)
            outs.append(acc / l_run)
        o_t = jnp.concatenate(outs, axis=0)
        o_ref[0, pl.ds(pl.multiple_of(i * blk, blk), blk), :] = o_t.T
        return carry

    lax.fori_loop(0, nb, q_block, 0)


def _moba_attention(p3, t_own, t_prev, far):
    b, s, _ = p3.shape
    nb = s // MOBA_BLOCK
    n_pairs = MOBA_HEADS // 2
    blk_spec = lambda off: pl.BlockSpec((1, s, LANES), lambda bi, hp: (bi, 0, off + hp))
    tab_spec = pl.BlockSpec((2, MOBA_BLOCK, MOBA_BLOCK), lambda bi, hp: (hp, 0, 0))
    return pl.pallas_call(
        functools.partial(_moba_kernel, nb=nb),
        grid=(b, n_pairs),
        in_specs=[pl.BlockSpec(memory_space=pltpu.SMEM),
                  blk_spec(0), blk_spec(n_pairs), blk_spec(2 * n_pairs), tab_spec, tab_spec],
        out_specs=pl.BlockSpec((1, s, LANES), lambda bi, hp: (bi, 0, hp)),
        out_shape=jax.ShapeDtypeStruct((b, s, MOBA_W), F32),
        scratch_shapes=[pltpu.VMEM((nb, LANES, MOBA_BLOCK), F32),
                        pltpu.VMEM((nb, LANES, MOBA_BLOCK), F32),
                        pltpu.VMEM((2, nb, nb, MOBA_BLOCK), F32),
                        pltpu.VMEM((nb, MOBA_BLOCK), F32)],
        compiler_params=_cparams(("parallel", "parallel")),
        name="moba",
    )(far, p3, p3, p3, t_own, t_prev)


def _softplus(z):
    return jnp.maximum(z, 0.0) + jnp.log(1.0 + jnp.exp(-jnp.abs(z)))


def _sigmoid(z):
    return 1.0 / (1.0 + jnp.exp(-z))


def _rwkv_kernel(r_ref, k_ref, v_ref, wa_ref, g_ref,
                 mu_r_ref, mu_k_ref, mu_v_ref, mu_wa_ref, mu_g_ref,
                 w0_ref, wup_ref, a0_ref, aup_ref, gup_ref, kk_ref, ka_ref, rk_ref,
                 gng_ref, gnb_ref, hsum_ref, hmask_ref, tri_ref,
                 o_ref, st_s, prev_r, prev_k, prev_v, prev_wa, prev_g):
    c = RWKV_CHUNK
    n_heads = RWKV_HEADS

    @pl.when(pl.program_id(1) == 0)
    def _():
        st_s[...] = jnp.zeros_like(st_s)
        prev_r[...] = jnp.zeros_like(prev_r)
        prev_k[...] = jnp.zeros_like(prev_k)
        prev_v[...] = jnp.zeros_like(prev_v)
        prev_wa[...] = jnp.zeros_like(prev_wa)
        prev_g[...] = jnp.zeros_like(prev_g)

    def mix(x_ref, prev_ref, mu_ref):
        x = x_ref[0]
        row = lax.broadcasted_iota(jnp.int32, x.shape, 0)
        shifted = jnp.where(row == 0, prev_ref[...], pltpu.roll(x, 1, axis=0))
        prev_ref[...] = x[c - 1:c, :]
        return x + (shifted - x) * mu_ref[...]

    r = mix(r_ref, prev_r, mu_r_ref)
    k = mix(k_ref, prev_k, mu_k_ref)
    v = mix(v_ref, prev_v, mu_v_ref)
    xwa = mix(wa_ref, prev_wa, mu_wa_ref)
    xg = mix(g_ref, prev_g, mu_g_ref)

    hsum = hsum_ref[...]
    w = -_softplus(-(w0_ref[...] + _bdot(jnp.tanh(xwa), wup_ref[...]))) - 0.5
    logd = -jnp.exp(w)
    a = _sigmoid(a0_ref[...] + _bdot(xwa, aup_ref[...]))
    gate = _bdot(_sigmoid(xg), gup_ref[...])
    kk = k * kk_ref[...]
    kk = kk / jnp.maximum(jnp.sqrt(_hdot(kk * kk, hsum)), 1e-12)
    k2 = k * (1.0 + (a - 1.0) * ka_ref[...])

    tri = tri_ref[...]
    cum = _hdot(tri, logd)
    cum_last = cum[c - 1:c, :]
    g_in = jnp.exp(cum)
    g_inv = jnp.exp(-cum)
    g_tail = jnp.exp(cum_last - cum)
    a_t = -kk * jnp.exp(cum - logd)
    b_t = kk * a * g_inv
    k_t = k2 * g_inv
    r_t = r * g_in
    b_hat = kk * a * g_tail
    k_hat = k2 * g_tail

    st = st_s[...]
    p0 = _bdot(a_t, st)
    o0 = _bdot(r_t, st)

    row = lax.broadcasted_iota(jnp.int32, (c, c), 0)
    col = lax.broadcasted_iota(jnp.int32, (c, c), 1)
    strict = row > col
    incl = row >= col
    eye = (row == col).astype(F32)
    bk = jnp.concatenate([b_t, k_t], axis=0).astype(BF16)
    lane_head = lax.broadcasted_iota(jnp.int32, (1, RWKV_W), 1) // HEAD_DIM

    u_all = jnp.zeros((c, RWKV_W), F32)
    o_all = jnp.zeros((c, RWKV_W), F32)
    for h in range(n_heads):
        hm = lane_head == h
        ar = jnp.concatenate([jnp.where(hm, a_t, 0.0), jnp.where(hm, r_t, 0.0)], axis=0)
        m4 = _dot_nt(ar.astype(BF16), bk)
        l_ab = jnp.where(strict, m4[:c, :c], 0.0)
        l_ak = jnp.where(strict, m4[:c, c:], 0.0)
        m_rb = jnp.where(incl, m4[c:, :c], 0.0)
        m_rk = jnp.where(incl, m4[c:, c:], 0.0)
        t_inv = eye + l_ab
        pw = l_ab
        span = 2
        while span < c:
            pw = _hdot(pw, pw)
            t_inv = t_inv + _hdot(t_inv, pw)
            span *= 2
        u_h = _hdot(t_inv, p0 + _bdot(l_ak, v))
        u_all = jnp.where(hm, u_h, u_all)
        o_h = _bdot(m_rb, u_h) + _bdot(m_rk, v)
        o_all = jnp.where(hm, o_h, o_all)
    o = o0 + o_all

    upd = _dot_tn(b_hat.astype(BF16), u_all.astype(BF16)) + _dot_tn(k_hat.astype(BF16), v.astype(BF16))
    first_row = (lax.broadcasted_iota(jnp.int32, (c, RWKV_W), 0) == 0)
    g_col = _dot_tn(jnp.where(first_row, jnp.exp(cum_last), 0.0), jnp.ones((c, RWKV_W), F32), precision=HI)
    st_s[...] = g_col * st + hmask_ref[...] * upd

    inv_n = 1.0 / HEAD_DIM
    mean = _hdot(o, hsum) * inv_n
    dev = o - mean
    var = _hdot(dev * dev, hsum) * inv_n
    y = dev * lax.rsqrt(var + RWKV_GN_EPS) * gng_ref[...] + gnb_ref[...]
    y = y + _hdot(r * k2 * rk_ref[...], hsum) * v
    o_ref[0] = y * gate


def _rwkv(p3, mu, w0, w_up, a0, a_up, g_up, k_k, k_a, r_k, gn_g, gn_b):
    b, s, _ = p3.shape
    c = RWKV_CHUNK
    w = RWKV_W
    base = 3 * MOBA_W
    assert base % w == 0 and (base + 3 * w) % LANES == 0
    lora = DECAY_LORA + AAA_LORA

    def col_spec(width, off_cols):
        assert off_cols % width == 0
        return pl.BlockSpec((1, c, width), lambda bi, ci: (bi, ci, off_cols // width))

    row = lambda x: x.reshape(1, -1).astype(F32)
    const = lambda shape: pl.BlockSpec(shape, lambda bi, ci: (0,) * len(shape))
    wup_pad = jnp.concatenate([w_up, jnp.zeros((AAA_LORA, w), F32)], axis=0).astype(BF16)
    aup_pad = jnp.concatenate([jnp.zeros((DECAY_LORA, w), F32), a_up], axis=0).astype(BF16)
    head = np.arange(w) // HEAD_DIM
    hsum = jnp.asarray((head[:, None] == head[None, :]).astype(np.float32))
    tri = jnp.asarray(np.tril(np.ones((c, c), np.float32)))
    vec_args = [mu[:w], mu[w:2 * w], mu[2 * w:3 * w], mu[3 * w:3 * w + lora], mu[3 * w + lora:],
                w0, None, a0, None, None, k_k, k_a, r_k.reshape(-1), gn_g, gn_b]
    args = [p3, p3, p3, p3, p3]
    specs = [col_spec(w, base), col_spec(w, base + w), col_spec(w, base + 2 * w),
             col_spec(lora, base + 3 * w), col_spec(GATE_LORA, base + 3 * w + lora)]
    mats = {6: wup_pad, 8: aup_pad, 9: g_up.astype(BF16)}
    for idx, a in enumerate(vec_args):
        arr = mats[idx] if a is None else row(a)
        args.append(arr)
        specs.append(const(arr.shape))
    for arr in (hsum, hsum, tri):
        args.append(arr)
        specs.append(const(arr.shape))
    return pl.pallas_call(
        _rwkv_kernel,
        grid=(b, s // c),
        in_specs=specs,
        out_specs=pl.BlockSpec((1, c, w), lambda bi, ci: (bi, ci, 0)),
        out_shape=jax.ShapeDtypeStruct((b, s, w), F32),
        scratch_shapes=[pltpu.VMEM((w, w), F32), pltpu.VMEM((1, w), F32), pltpu.VMEM((1, w), F32),
                        pltpu.VMEM((1, w), F32), pltpu.VMEM((1, lora), F32),
                        pltpu.VMEM((1, GATE_LORA), F32)],
        compiler_params=_cparams(("parallel", "arbitrary")),
        name="rwkv",
    )(*args)


def _split_bf16(x, terms):
    parts = []
    for _ in range(terms):
        hi = x.astype(BF16)
        parts.append(hi)
        x = x - hi.astype(F32)
    return parts


def _dot_exact_rhs(x, m_bf16, terms):
    acc = None
    for part in _split_bf16(x, terms):
        d = jnp.dot(part, m_bf16, preferred_element_type=F32)
        acc = d if acc is None else acc + d
    return acc


def _dot_parts(a_parts, b_parts):
    dot = functools.partial(jnp.dot, preferred_element_type=F32)
    acc = dot(a_parts[0], b_parts[0])
    if len(a_parts) > 1:
        acc = acc + (dot(a_parts[0], b_parts[1]) + dot(a_parts[1], b_parts[0]))
    return acc


def _rwkv2_kernel(r_ref, k_ref, v_ref, wa_ref, g_ref,
                  mu_r_ref, mu_k_ref, mu_v_ref, mu_wa_ref, mu_g_ref,
                  w0_ref, wup_ref, a0_ref, aup_ref, gup_ref, kk_ref, ka_ref, rk_ref,
                  gng_ref, gnb_ref, hsum_ref, tri_ref,
                  o_ref, st_s, prev_r, prev_k, prev_v, prev_wa, prev_g, *, n_seq):
    c = RWKV_CHUNK
    mid = c // 2

    @pl.when(pl.program_id(1) == 0)
    def _():
        st_s[...] = jnp.zeros_like(st_s)
        prev_r[...] = jnp.zeros_like(prev_r)
        prev_k[...] = jnp.zeros_like(prev_k)
        prev_v[...] = jnp.zeros_like(prev_v)
        prev_wa[...] = jnp.zeros_like(prev_wa)
        prev_g[...] = jnp.zeros_like(prev_g)

    hsum = hsum_ref[...]
    tri = tri_ref[...]
    row = lax.broadcasted_iota(jnp.int32, (c, c), 0)
    col = lax.broadcasted_iota(jnp.int32, (c, c), 1)
    strict = row > col
    incl = row >= col
    eye = (row == col).astype(F32)
    lane_head = lax.broadcasted_iota(jnp.int32, (1, RWKV_W), 1) // HEAD_DIM
    hmask = hsum.astype(F32)

    def mix(g, x_ref, prev_ref, mu_ref):
        x = x_ref[g]
        rows = lax.broadcasted_iota(jnp.int32, x.shape, 0)
        shifted = jnp.where(rows == 0, prev_ref[g], pltpu.roll(x, 1, axis=0))
        prev_ref[g] = x[c - 1:c, :]
        return x + (shifted - x) * mu_ref[...]

    def prepare(g):
        r = mix(g, r_ref, prev_r, mu_r_ref)
        k = mix(g, k_ref, prev_k, mu_k_ref)
        v = mix(g, v_ref, prev_v, mu_v_ref)
        xwa = mix(g, wa_ref, prev_wa, mu_wa_ref)
        xg = mix(g, g_ref, prev_g, mu_g_ref)

        w = -_softplus(-(w0_ref[...] + _bdot(jnp.tanh(xwa), wup_ref[...]))) - 0.5
        logd = -jnp.exp(w)
        a = _sigmoid(a0_ref[...] + _bdot(xwa, aup_ref[...]))
        gate = _bdot(_sigmoid(xg), gup_ref[...])
        kk = k * kk_ref[...]
        kk = kk / jnp.maximum(jnp.sqrt(_dot_exact_rhs(kk * kk, hsum, 2)), 1e-12)
        k2 = k * (1.0 + (a - 1.0) * ka_ref[...])

        parts = _split_bf16(logd, 3)
        cum = (jnp.dot(tri, parts[0], preferred_element_type=F32)
               + jnp.dot(tri, parts[1], preferred_element_type=F32)
               + jnp.dot(tri, parts[2], preferred_element_type=F32))
        cum_last = cum[c - 1:c, :]
        ref = cum[mid - 1:mid, :]
        rel = cum - ref
        g_in = jnp.exp(rel)
        g_inv = jnp.exp(-rel)
        g_tail = jnp.exp(cum_last - cum)
        a_t = -kk * jnp.exp(rel - logd)
        b_t = kk * a * g_inv
        k_t = k2 * g_inv
        r_t = r * g_in
        b_hat = kk * a * g_tail
        k_hat = k2 * g_tail

        st = st_s[g]
        st_ref = (st * jnp.exp(ref)).astype(BF16)
        p0 = _dot_nt(a_t.astype(BF16), st_ref)
        o0 = _dot_nt(r_t.astype(BF16), st_ref)

        bk = jnp.concatenate([b_t, k_t], axis=0).astype(BF16)
        v_bf = v.astype(BF16)
        quads = []
        for h in range(RWKV_HEADS):
            hm = lane_head == h
            ar = jnp.concatenate([jnp.where(hm, a_t, 0.0), jnp.where(hm, r_t, 0.0)], axis=0)
            m4 = _dot_nt(ar.astype(BF16), bk)
            quads.append((jnp.where(strict, m4[:c, :c], 0.0), jnp.where(strict, m4[:c, c:], 0.0),
                          jnp.where(incl, m4[c:, :c], 0.0).astype(BF16),
                          jnp.where(incl, m4[c:, c:], 0.0).astype(BF16)))
        return dict(r=r, k2=k2, v=v, v_bf=v_bf, gate=gate, p0=p0, o0=o0, b_hat=b_hat, k_hat=k_hat,
                    st=st, decay=jnp.exp(cum_last), quads=quads)

    seqs = [prepare(g) for g in range(n_seq)]
    pairs = [(g, h) for g in range(n_seq) for h in range(RWKV_HEADS)]

    pw = {gh: seqs[gh[0]]["quads"][gh[1]][0] for gh in pairs}
    t_inv = {gh: eye + pw[gh] for gh in pairs}
    span = 2
    pw_parts = {gh: _split_bf16(pw[gh], RWKV_INV_TERMS) for gh in pairs}
    while span < c:
        pw = {gh: _dot_parts(pw_parts[gh], pw_parts[gh]) for gh in pairs}
        pw_parts = {gh: _split_bf16(pw[gh], RWKV_INV_TERMS) for gh in pairs}
        t_parts = {gh: _split_bf16(t_inv[gh], RWKV_INV_TERMS) for gh in pairs}
        t_inv = {gh: t_inv[gh] + _dot_parts(t_parts[gh], pw_parts[gh]) for gh in pairs}
        span *= 2

    w_loc = {(g, h): seqs[g]["p0"] + jnp.dot(seqs[g]["quads"][h][1].astype(BF16), seqs[g]["v_bf"],
                                             preferred_element_type=F32) for g, h in pairs}
    u_h = {gh: _dot_parts(_split_bf16(t_inv[gh], RWKV_INV_TERMS), _split_bf16(w_loc[gh], RWKV_INV_TERMS))
           for gh in pairs}
    o_h = {(g, h): (jnp.dot(seqs[g]["quads"][h][2], u_h[(g, h)].astype(BF16), preferred_element_type=F32)
                    + jnp.dot(seqs[g]["quads"][h][3], seqs[g]["v_bf"], preferred_element_type=F32))
           for g, h in pairs}

    for g in range(n_seq):
        sq = seqs[g]
        u_all = jnp.zeros((c, RWKV_W), F32)
        o_all = jnp.zeros((c, RWKV_W), F32)
        for h in range(RWKV_HEADS):
            hm = lane_head == h
            u_all = jnp.where(hm, u_h[(g, h)], u_all)
            o_all = jnp.where(hm, o_h[(g, h)], o_all)
        o = sq["o0"] + o_all
        upd = (_dot_tn(u_all.astype(BF16), sq["b_hat"].astype(BF16))
               + _dot_tn(sq["v_bf"], sq["k_hat"].astype(BF16)))
        st_s[g] = sq["st"] * sq["decay"] + hmask * upd

        inv_n = 1.0 / HEAD_DIM
        mean = _dot_exact_rhs(o, hsum, 2) * inv_n
        dev = o - mean
        var = _dot_exact_rhs(dev * dev, hsum, 2) * inv_n
        y = dev * lax.rsqrt(var + RWKV_GN_EPS) * gng_ref[...] + gnb_ref[...]
        y = y + _dot_exact_rhs(sq["r"] * sq["k2"] * rk_ref[...], hsum, 2) * sq["v"]
        o_ref[g] = y * sq["gate"]


def _rwkv2(p3, mu, w0, w_up, a0, a_up, g_up, k_k, k_a, r_k, gn_g, gn_b):
    b, s, _ = p3.shape
    c = RWKV_CHUNK
    w = RWKV_W
    n_seq = RWKV_SEQS_PER_STEP if b % RWKV_SEQS_PER_STEP == 0 else 1
    base = 3 * MOBA_W
    assert base % w == 0 and (base + 3 * w) % LANES == 0
    lora = DECAY_LORA + AAA_LORA

    def col_spec(width, off_cols):
        assert off_cols % width == 0
        return pl.BlockSpec((n_seq, c, width), lambda bi, ci: (bi, ci, off_cols // width))

    row = lambda x: x.reshape(1, -1).astype(F32)
    const = lambda shape: pl.BlockSpec(shape, lambda bi, ci: (0,) * len(shape))
    wup_pad = jnp.concatenate([w_up, jnp.zeros((AAA_LORA, w), F32)], axis=0).astype(BF16)
    aup_pad = jnp.concatenate([jnp.zeros((DECAY_LORA, w), F32), a_up], axis=0).astype(BF16)
    head = np.arange(w) // HEAD_DIM
    hsum = jnp.asarray((head[:, None] == head[None, :]).astype(np.float32)).astype(BF16)
    tri = jnp.asarray(np.tril(np.ones((c, c), np.float32))).astype(BF16)
    vec_args = [mu[:w], mu[w:2 * w], mu[2 * w:3 * w], mu[3 * w:3 * w + lora], mu[3 * w + lora:],
                w0, None, a0, None, None, k_k, k_a, r_k.reshape(-1), gn_g, gn_b]
    args = [p3, p3, p3, p3, p3]
    specs = [col_spec(w, base), col_spec(w, base + w), col_spec(w, base + 2 * w),
             col_spec(lora, base + 3 * w), col_spec(GATE_LORA, base + 3 * w + lora)]
    mats = {6: wup_pad, 8: aup_pad, 9: g_up.astype(BF16)}
    for idx, a in enumerate(vec_args):
        arr = mats[idx] if a is None else row(a)
        args.append(arr)
        specs.append(const(arr.shape))
    for arr in (hsum, tri):
        args.append(arr)
        specs.append(const(arr.shape))
    return pl.pallas_call(
        functools.partial(_rwkv2_kernel, n_seq=n_seq),
        grid=(b // n_seq, s // c),
        in_specs=specs,
        out_specs=pl.BlockSpec((n_seq, c, w), lambda bi, ci: (bi, ci, 0)),
        out_shape=jax.ShapeDtypeStruct((b, s, w), F32),
        scratch_shapes=[pltpu.VMEM((n_seq, w, w), F32), pltpu.VMEM((n_seq, 1, w), F32),
                        pltpu.VMEM((n_seq, 1, w), F32), pltpu.VMEM((n_seq, 1, w), F32),
                        pltpu.VMEM((n_seq, 1, lora), F32), pltpu.VMEM((n_seq, 1, GATE_LORA), F32)],
        compiler_params=_cparams(("parallel", "arbitrary")),
        name="rwkv",
    )(*args)


def _mem_attn_kernel(q_ref, kv_ref, o_ref):
    scale = HEAD_DIM ** -0.5
    lane_head = lax.broadcasted_iota(jnp.int32, (1, MEM_W), 1) // HEAD_DIM
    q = q_ref[0] * scale
    mk = kv_ref[0, :, :MEM_W].astype(BF16)
    mv = kv_ref[0, :, MEM_W:].astype(BF16)
    out = jnp.zeros(q.shape, F32)
    for h in range(MEM_HEADS):
        hm = lane_head == h
        s = _dot_nt(jnp.where(hm, q, 0.0).astype(BF16), mk)
        s = s - jnp.max(s, axis=-1, keepdims=True)
        e = jnp.exp(s)
        p = e / jnp.sum(e, axis=-1, keepdims=True)
        out = jnp.where(hm, jnp.dot(p.astype(BF16), mv, preferred_element_type=F32), out)
    o_ref[0] = out


def _mem_attention(p3, mkv3, tq):
    b, s, _ = p3.shape
    m = mkv3.shape[1]
    off = (3 * MOBA_W + RWKV_COLS) // MEM_W
    assert off * MEM_W == 3 * MOBA_W + RWKV_COLS
    return pl.pallas_call(
        _mem_attn_kernel,
        grid=(b, s // tq),
        in_specs=[pl.BlockSpec((1, tq, MEM_W), lambda bi, i: (bi, i, off)),
                  pl.BlockSpec((1, m, 2 * MEM_W), lambda bi, i: (bi, 0, 0))],
        out_specs=pl.BlockSpec((1, tq, MEM_W), lambda bi, i: (bi, i, 0)),
        out_shape=jax.ShapeDtypeStruct((b, s, MEM_W), F32),
        compiler_params=_cparams(("parallel", "parallel")),
        name="mem_attn",
    )(p3, mkv3)


def _out_router_kernel(x_ref, ym_ref, yr_ref, ye_ref, wo1_ref, wo2_ref, wo3_ref, g_ref,
                       wr_ref, br_ref, ltri_ref,
                       x1_ref, h_ref, idx_ref, wgt_ref, rank_ref, cnt_ref, run_s):
    @pl.when(pl.program_id(0) == 0)
    def _():
        run_s[...] = jnp.zeros_like(run_s)

    x1 = (x_ref[...] + _bdot(ym_ref[...], wo1_ref[...]) + _bdot(yr_ref[...], wo2_ref[...])
          + _bdot(ye_ref[...], wo3_ref[...]))
    x1_ref[...] = x1
    h = _rms(x1, g_ref[...])
    h_ref[...] = _pack_halves(h)
    logits = _hdot(h, wr_ref[...]) + br_ref[...]
    tm = logits.shape[0]
    lane = lax.broadcasted_iota(jnp.int32, (tm, LANES), 1)

    vals, idxs = [], []
    lg = logits
    for _ in range(TOP_K):
        m = jnp.max(lg, axis=1, keepdims=True)
        idx = jnp.min(jnp.where(lg == m, lane, LANES), axis=1, keepdims=True)
        vals.append(m)
        idxs.append(idx)
        lg = jnp.where(lane == idx, NEG_INF, lg)
    exps = [jnp.exp(vk - vals[0]) for vk in vals]
    denom = exps[0] + exps[1] + exps[2] + exps[3]

    chosen = jnp.zeros((tm, LANES), F32)
    for idx in idxs:
        chosen = chosen + jnp.where(lane == idx, 1.0, 0.0)
    before = jnp.dot(ltri_ref[...], chosen.astype(BF16), preferred_element_type=F32) + run_s[...]
    run_s[...] = run_s[...] + jnp.sum(chosen, axis=0, keepdims=True)
    cnt_ref[...] = run_s[...]

    idx_out = jnp.zeros((tm, LANES), jnp.int32)
    wgt_out = jnp.zeros((tm, LANES), F32)
    rank_out = jnp.zeros((tm, LANES), jnp.int32)
    for kk in range(TOP_K):
        rank = jnp.sum(jnp.where(lane == idxs[kk], before, 0.0), axis=1, keepdims=True)
        idx_out = jnp.where(lane == kk, idxs[kk], idx_out)
        wgt_out = jnp.where(lane == kk, exps[kk] / denom, wgt_out)
        rank_out = jnp.where(lane == kk, rank.astype(jnp.int32), rank_out)
    idx_ref[...] = idx_out
    wgt_ref[...] = wgt_out
    rank_ref[...] = rank_out


def _out_router(x2, ym, yr, ye, w_out, g_ffn, w_router, b_router, tm):
    t, d = x2.shape
    wo = w_out.astype(BF16)
    wo1, wo2, wo3 = wo[:MOBA_W], wo[MOBA_W:MOBA_W + RWKV_W], wo[MOBA_W + RWKV_W:]
    wr = jnp.zeros((d, LANES), F32).at[:, :N_EXPERTS].set(w_router)
    br = jnp.full((1, LANES), NEG_INF, F32).at[0, :N_EXPERTS].set(b_router)
    ltri = jnp.asarray(np.tril(np.ones((tm, tm), np.float32), -1)).astype(BF16)
    tile = lambda n: pl.BlockSpec((tm, n), lambda i: (i, 0))
    const = lambda a: pl.BlockSpec(a.shape, lambda i: (0,) * a.ndim)
    g2 = g_ffn.reshape(1, d)
    return pl.pallas_call(
        _out_router_kernel,
        grid=(t // tm,),
        in_specs=[tile(d), tile(MOBA_W), tile(RWKV_W), tile(MEM_W), const(wo1), const(wo2), const(wo3),
                  const(g2), const(wr), const(br), const(ltri)],
        out_specs=[tile(d), tile(d // 2), tile(LANES), tile(LANES), tile(LANES),
                   pl.BlockSpec((1, LANES), lambda i: (0, 0))],
        out_shape=[jax.ShapeDtypeStruct((t, d), F32), jax.ShapeDtypeStruct((t, d // 2), jnp.uint32),
                   jax.ShapeDtypeStruct((t, LANES), jnp.int32), jax.ShapeDtypeStruct((t, LANES), F32),
                   jax.ShapeDtypeStruct((t, LANES), jnp.int32), jax.ShapeDtypeStruct((1, LANES), F32)],
        scratch_shapes=[pltpu.VMEM((1, LANES), F32)],
        compiler_params=_cparams(("arbitrary",)),
        name="out_router",
    )(x2, ym, yr, ye, wo1, wo2, wo3, g2, wr, br, ltri)


def _experts_kernel(be_ref, nused_ref, xs_ref, wgu_ref, bg_ref, bu_ref, wd_ref, bd_ref, perm_ref,
                    o_ref, wg_s, wu_s, wd_s):
    i = pl.program_id(0)
    prev = be_ref[jnp.maximum(i - 1, 0)]
    changed = (i == 0) | (be_ref[i] != prev)

    @pl.when(changed & (i < nused_ref[0]))
    def _():
        half = LANES
        for cblk in range(2 * D_EXPERT // (2 * half)):
            wt = wgu_ref[0, :, cblk * 2 * half:(cblk + 1) * 2 * half].astype(BF16)
            sep = jnp.dot(wt, perm_ref[...], preferred_element_type=F32).astype(BF16)
            wg_s[:, cblk * half:(cblk + 1) * half] = sep[:, :half]
            wu_s[:, cblk * half:(cblk + 1) * half] = sep[:, half:]
        wd_s[...] = wd_ref[0].astype(BF16)

    @pl.when(i < nused_ref[0])
    def _():
        xb = _unpack_halves(xs_ref[...]).astype(BF16)
        gate = jnp.dot(xb, wg_s[...], preferred_element_type=F32) + bg_ref[0]
        up = jnp.dot(xb, wu_s[...], preferred_element_type=F32) + bu_ref[0]
        gate = jnp.minimum(gate, SWIGLU_LIMIT)
        up = jnp.clip(up, -SWIGLU_LIMIT, SWIGLU_LIMIT)
        glu = gate * _sigmoid(gate * SWIGLU_ALPHA)
        act = ((up + 1.0) * glu).astype(BF16)
        o_ref[...] = _pack_halves(jnp.dot(act, wd_s[...], preferred_element_type=F32) + bd_ref[0])

    @pl.when(i >= nused_ref[0])
    def _():
        o_ref[...] = jnp.zeros_like(o_ref)


def _experts(blk_expert, n_used, xs, w_gate_up, b_gate_up, w_down, b_down):
    p_rows = xs.shape[0]
    d = 2 * xs.shape[1]
    n_blocks = p_rows // EXPERT_ROWS
    bg = b_gate_up[:, 0::2].reshape(N_EXPERTS, 1, D_EXPERT)
    bu = b_gate_up[:, 1::2].reshape(N_EXPERTS, 1, D_EXPERT)
    bd = b_down.reshape(N_EXPERTS, 1, d)
    perm_np = np.zeros((2 * LANES, 2 * LANES), np.float32)
    perm_np[2 * np.arange(LANES), np.arange(LANES)] = 1.0
    perm_np[2 * np.arange(LANES) + 1, LANES + np.arange(LANES)] = 1.0
    perm = jnp.asarray(perm_np).astype(BF16)
    by_expert = lambda shape: pl.BlockSpec((1,) + shape, lambda i, be, nu: (be[i], 0, 0))
    grid_spec = pltpu.PrefetchScalarGridSpec(
        num_scalar_prefetch=2,
        grid=(n_blocks,),
        in_specs=[pl.BlockSpec((EXPERT_ROWS, d // 2), lambda i, be, nu: (i, 0)),
                  by_expert((d, 2 * D_EXPERT)), by_expert((1, D_EXPERT)), by_expert((1, D_EXPERT)),
                  by_expert((D_EXPERT, d)), by_expert((1, d)),
                  pl.BlockSpec(perm.shape, lambda i, be, nu: (0, 0))],
        out_specs=pl.BlockSpec((EXPERT_ROWS, d // 2), lambda i, be, nu: (i, 0)),
        scratch_shapes=[pltpu.VMEM((d, D_EXPERT), BF16), pltpu.VMEM((d, D_EXPERT), BF16),
                        pltpu.VMEM((D_EXPERT, d), BF16)],
    )
    return pl.pallas_call(
        _experts_kernel,
        grid_spec=grid_spec,
        out_shape=jax.ShapeDtypeStruct((p_rows, d // 2), jnp.uint32),
        compiler_params=_cparams(("arbitrary",)),
        name="experts",
    )(blk_expert, n_used, xs, w_gate_up, bg, bu, w_down, bd, perm)


def _row_copies(e_ref, rank_ref, start_ref, n_tok, make_copy):
    def body(r, carry):
        for kk in range(TOP_K):
            j = r * TOP_K + kk
            row = start_ref[e_ref[0, 0, j]] + rank_ref[0, 0, j]
            make_copy(r, kk, row).start()
        return carry
    lax.fori_loop(0, n_tok, body, 0)


def _dispatch_kernel(e_ref, rank_ref, start_ref, h_ref, xs_init_ref, xs_ref, sem):
    del xs_init_ref
    tb = h_ref.shape[0]
    _row_copies(e_ref, rank_ref, start_ref, tb,
                lambda r, kk, row: pltpu.make_async_copy(h_ref.at[pl.ds(r, 1)], xs_ref.at[pl.ds(row, 1)], sem))
    for _ in range(TOP_K):
        pltpu.make_async_copy(h_ref, xs_ref.at[pl.ds(0, tb)], sem).wait()


def _dispatch(e_flat, rank_flat, pad_starts, hp, p_rows, tb):
    t, half = hp.shape
    idx_spec = pl.BlockSpec((1, 1, tb * TOP_K), lambda i: (i, 0, 0), memory_space=pltpu.SMEM)
    xs_init = jnp.zeros((p_rows, half), jnp.uint32)
    return pl.pallas_call(
        _dispatch_kernel,
        grid=(t // tb,),
        in_specs=[idx_spec, idx_spec, pl.BlockSpec(memory_space=pltpu.SMEM),
                  pl.BlockSpec((tb, half), lambda i: (i, 0)), pl.BlockSpec(memory_space=pl.ANY)],
        out_specs=pl.BlockSpec(memory_space=pl.ANY),
        out_shape=jax.ShapeDtypeStruct((p_rows, half), jnp.uint32),
        scratch_shapes=[pltpu.SemaphoreType.DMA(())],
        input_output_aliases={4: 0},
        compiler_params=_cparams(("arbitrary",)),
        name="dispatch",
    )(e_flat, rank_flat, pad_starts, hp, xs_init)


def _combine_kernel(e_ref, rank_ref, e_next_ref, rank_next_ref, start_ref, ys_ref, x1_ref, w_ref, g_ref,
                    o_ref, buf, sem):
    i = pl.program_id(0)
    n = pl.num_programs(0)
    tb = x1_ref.shape[0]

    def gather(e, rank, slot):
        _row_copies(e, rank, start_ref, tb,
                    lambda r, kk, row: pltpu.make_async_copy(
                        ys_ref.at[pl.ds(row, 1)], buf.at[slot, kk, pl.ds(r, 1)], sem.at[slot]))

    @pl.when(i == 0)
    def _():
        gather(e_ref, rank_ref, 0)

    @pl.when(i + 1 < n)
    def _():
        gather(e_next_ref, rank_next_ref, (i + 1) % 2)

    slot = i % 2
    for kk in range(TOP_K):
        pltpu.make_async_copy(ys_ref.at[pl.ds(0, tb)], buf.at[slot, kk], sem.at[slot]).wait()
    acc = x1_ref[...]
    wts = w_ref[...]
    for kk in range(TOP_K):
        acc = acc + _unpack_halves(buf[slot, kk]) * wts[:, kk:kk + 1]
    o_ref[...] = _rms(acc, g_ref[...])


def _combine(e_flat, rank_flat, pad_starts, ys, x1, wts, g_final, tb):
    t, d = x1.shape
    n = t // tb
    half = ys.shape[1]
    idx_spec = pl.BlockSpec((1, 1, tb * TOP_K), lambda i: (i, 0, 0), memory_space=pltpu.SMEM)
    next_spec = pl.BlockSpec((1, 1, tb * TOP_K), lambda i: (jnp.minimum(i + 1, n - 1), 0, 0),
                             memory_space=pltpu.SMEM)
    return pl.pallas_call(
        _combine_kernel,
        grid=(n,),
        in_specs=[idx_spec, idx_spec, next_spec, next_spec, pl.BlockSpec(memory_space=pltpu.SMEM),
                  pl.BlockSpec(memory_space=pl.ANY),
                  pl.BlockSpec((tb, d), lambda i: (i, 0)),
                  pl.BlockSpec((tb, LANES), lambda i: (i, 0)),
                  pl.BlockSpec((1, d), lambda i: (0, 0))],
        out_specs=pl.BlockSpec((tb, d), lambda i: (i, 0)),
        out_shape=jax.ShapeDtypeStruct((t, d), F32),
        scratch_shapes=[pltpu.VMEM((2, TOP_K, tb, half), jnp.uint32), pltpu.SemaphoreType.DMA((2,))],
        compiler_params=_cparams(("arbitrary",)),
        name="combine",
    )(e_flat, rank_flat, e_flat, rank_flat, pad_starts, ys, x1, wts, g_final.reshape(1, d))


def _layer(x, mem, w_in, w_out, w_mem_kv, g_mix, g_mem, g_ffn, bias_tables, mu, w0, w_up, a0, a_up,
           g_up, k_k, k_a, r_k, gn_g, gn_b, w_router, b_router, w_gate_up, b_gate_up, w_down, b_down,
           g_last):
    b, s, d = x.shape
    m = mem.shape[1]
    t = b * s
    x2 = x.reshape(t, d)

    p = _norm_matmul(x2, g_mix, w_in.astype(BF16), 512, "in_proj")
    p3 = p.reshape(b, s, IN_COLS)
    y_moba = _moba_attention(p3, *bias_tables)
    y_rwkv = _rwkv2(p3, mu, w0, w_up, a0, a_up, g_up, k_k, k_a, r_k, gn_g, gn_b)
    mkv = _norm_matmul(mem.reshape(b * m, d), g_mem, w_mem_kv.astype(BF16), 512, "mem_kv")
    y_mem = _mem_attention(p3, mkv.reshape(b, m, 2 * MEM_W), 512)

    x1, h2, idx_p, wgt_p, rank_p, cnt = _out_router(
        x2, y_moba.reshape(t, MOBA_W), y_rwkv.reshape(t, RWKV_W), y_mem.reshape(t, MEM_W),
        w_out, g_ffn, w_router, b_router, 256)

    tb = ROUTE_TOKENS
    e_flat = idx_p[:, :TOP_K].reshape(t // tb, 1, tb * TOP_K)
    rank_flat = rank_p[:, :TOP_K].reshape(t // tb, 1, tb * TOP_K)
    counts = cnt[0, :N_EXPERTS].astype(jnp.int32)
    padded = (counts + EXPERT_ROWS - 1) // EXPERT_ROWS * EXPERT_ROWS
    pad_ends = jnp.cumsum(padded)
    pad_starts = (pad_ends - padded).astype(jnp.int32)
    n_blocks = (t * TOP_K) // EXPERT_ROWS + N_EXPERTS
    blk_start = jnp.arange(n_blocks, dtype=jnp.int32) * EXPERT_ROWS
    blk_expert = jnp.minimum(jnp.sum(blk_start[:, None] >= pad_ends[None, :], axis=1),
                             N_EXPERTS - 1).astype(jnp.int32)
    n_used = (pad_ends[-1:] // EXPERT_ROWS).astype(jnp.int32)

    xs = _dispatch(e_flat, rank_flat, pad_starts, h2, n_blocks * EXPERT_ROWS, tb)
    ys = _experts(blk_expert, n_used, xs, w_gate_up, b_gate_up, w_down, b_down)
    out = _combine(e_flat, rank_flat, pad_starts, ys, x1, wgt_p, g_last, tb)
    return out.reshape(b, s, d)


def kernel(x, mem, w_in, w_out, w_mem_kv, g_mix, g_mem, g_ffn, g_final, rel_bias, rwkv_mu, rwkv_w0,
           rwkv_w_up, rwkv_a0, rwkv_a_up, rwkv_g_up, rwkv_k_k, rwkv_k_a, rwkv_r_k, rwkv_gn_g, rwkv_gn_b,
           w_router, b_router, w_gate_up, b_gate_up, w_down, b_down):
    depth = w_in.shape[0]
    assert depth == 1, "the final norm is fused into the last layer's combine kernel"
    bias_tables = _moba_bias_tables(rel_bias)
    l = 0
    return _layer(x, mem, w_in[l], w_out[l], w_mem_kv[l], g_mix[l], g_mem[l], g_ffn[l], bias_tables,
                  rwkv_mu[l], rwkv_w0[l], rwkv_w_up[l], rwkv_a0[l], rwkv_a_up[l], rwkv_g_up[l],
                  rwkv_k_k[l], rwkv_k_a[l], rwkv_r_k[l], rwkv_gn_g[l], rwkv_gn_b[l], w_router[l],
                  b_router[l], w_gate_up[l], b_gate_up[l], w_down[l], b_down[l], g_final)
```

```python
import functools
import math

import numpy as np
import jax
import jax.numpy as jnp
from jax import lax
from jax.experimental import pallas as pl
from jax.experimental.pallas import tpu as pltpu

F32 = jnp.float32
BF16 = jnp.bfloat16
HI = lax.Precision.HIGHEST

D_MODEL = 1024
HEAD_DIM = 64
MOBA_HEADS = 6
RWKV_HEADS = 6
MEM_HEADS = 4
MOBA_W = MOBA_HEADS * HEAD_DIM
RWKV_W = RWKV_HEADS * HEAD_DIM
MEM_W = MEM_HEADS * HEAD_DIM
MOBA_BLOCK = 256
MOBA_TOPK = 3
N_BUCKETS = 32
MAX_DISTANCE = 128
DECAY_LORA = 64
AAA_LORA = 64
GATE_LORA = 128
RWKV_COLS = 3 * RWKV_W + DECAY_LORA + AAA_LORA + GATE_LORA
RWKV_GN_EPS = 64e-5
IN_COLS = 3 * MOBA_W + RWKV_COLS + MEM_W
N_EXPERTS = 32
TOP_K = 4
D_EXPERT = D_MODEL
SWIGLU_ALPHA = 1.702
SWIGLU_LIMIT = 7.0
RMS_EPS = 1e-5

LANES = 128
RWKV_CHUNK = 128
RWKV_SEQS_PER_STEP = 2
RWKV_INV_TERMS = 1
EXPERT_ROWS = 256
ROUTE_TOKENS = 256
VMEM_LIMIT = 56 * 1024 * 1024
NEG_INF = float("-inf")


def _cparams(sem):
    return pltpu.CompilerParams(dimension_semantics=sem, vmem_limit_bytes=VMEM_LIMIT)


def _rms(x, g):
    return x * lax.rsqrt(jnp.mean(x * x, axis=-1, keepdims=True) + RMS_EPS) * g


def _bdot(a, b):
    return jnp.dot(a.astype(BF16), b.astype(BF16), preferred_element_type=F32)


def _hdot(a, b):
    return jnp.dot(a, b, preferred_element_type=F32, precision=HI)


def _dot_nt(a, b, **kw):
    return lax.dot_general(a, b, (((1,), (1,)), ((), ())), preferred_element_type=F32, **kw)


def _dot_tn(a, b, **kw):
    return lax.dot_general(a, b, (((0,), (0,)), ((), ())), preferred_element_type=F32, **kw)


def _pack_halves(x):
    n = x.shape[1] // 2
    lo = pltpu.bitcast(x[:, :n].astype(BF16).astype(F32), jnp.uint32)
    hi = pltpu.bitcast(x[:, n:].astype(BF16).astype(F32), jnp.uint32)
    return (hi & jnp.uint32(0xFFFF0000)) | (lo >> 16)


def _unpack_halves(w):
    lo = pltpu.bitcast(w << 16, F32)
    hi = pltpu.bitcast(w & jnp.uint32(0xFFFF0000), F32)
    return jnp.concatenate([lo, hi], axis=1)


def _norm_matmul_kernel(x_ref, g_ref, w_ref, o_ref):
    h = _rms(x_ref[...], g_ref[...])
    o_ref[...] = jnp.dot(h.astype(BF16), w_ref[...], preferred_element_type=F32)


def _norm_matmul(x, g, w_bf16, tm, name):
    t, d = x.shape
    n = w_bf16.shape[1]
    tm = min(tm, t)
    return pl.pallas_call(
        _norm_matmul_kernel,
        grid=(t // tm,),
        in_specs=[pl.BlockSpec((tm, d), lambda i: (i, 0)),
                  pl.BlockSpec((1, d), lambda i: (0, 0)),
                  pl.BlockSpec((d, n), lambda i: (0, 0))],
        out_specs=pl.BlockSpec((tm, n), lambda i: (i, 0)),
        out_shape=jax.ShapeDtypeStruct((t, n), F32),
        compiler_params=_cparams(("parallel",)),
        name=name,
    )(x, g.reshape(1, d), w_bf16)


def _t5_bucket_np(dist):
    n = np.maximum(dist, 0)
    max_exact = N_BUCKETS // 2
    nf = np.maximum(n, 1).astype(np.float64)
    large = max_exact + (np.log(nf / max_exact) / math.log(MAX_DISTANCE / max_exact)
                         * (N_BUCKETS - max_exact)).astype(np.int64)
    large = np.minimum(large, N_BUCKETS - 1)
    return np.where(n < max_exact, n, large)


def _moba_bias_tables(rel_bias):
    kk = np.arange(MOBA_BLOCK)[:, None]
    qq = np.arange(MOBA_BLOCK)[None, :]
    d_own = qq - kk
    d_prev = MOBA_BLOCK + qq - kk
    assert np.all(_t5_bucket_np(np.arange(MOBA_BLOCK + 1, 64 * MOBA_BLOCK)) == N_BUCKETS - 1)
    bias_t = rel_bias.astype(F32).T

    def lookup(bucket):
        bucket = jnp.asarray(bucket, jnp.int32)[None]
        tab = jnp.zeros((MOBA_HEADS,) + bucket.shape[1:], F32)
        for b in range(N_BUCKETS):
            tab = jnp.where(bucket == b, bias_t[:, b][:, None, None], tab)
        return tab

    t_own = jnp.where(jnp.asarray(d_own >= 0)[None], lookup(_t5_bucket_np(d_own)), NEG_INF)
    t_prev = lookup(_t5_bucket_np(d_prev))
    far = bias_t[:, N_BUCKETS - 1]
    return t_own, t_prev, far


def _moba_kernel(far_ref, q_ref, k_ref, v_ref, t0_ref, t1_ref, o_ref,
                 qt_s, vt_s, gate_s, neg_s, *, nb):
    hp = pl.program_id(1)
    blk = MOBA_BLOCK
    scale = HEAD_DIM ** -0.5
    row_head = lax.broadcasted_iota(jnp.int32, (LANES, 1), 0) // HEAD_DIM
    lane_head = lax.broadcasted_iota(jnp.int32, (1, LANES), 1) // HEAD_DIM

    kmean_rows = []
    for j in range(nb):
        sl = pl.ds(j * blk, blk)
        qt_s[j] = (q_ref[0, sl, :] * scale).T
        vt_s[j] = v_ref[0, sl, :].T
        kmean_rows.append(jnp.mean(k_ref[0, sl, :], axis=0, keepdims=True))
    kmean = jnp.concatenate(kmean_rows, axis=0)
    for h in range(2):
        km_h = jnp.where(lane_head == h, kmean, 0.0)
        for i in range(nb):
            gate_s[h, i] = _hdot(km_h, qt_s[i])

    blk_iota = lax.broadcasted_iota(jnp.int32, (nb, blk), 0)

    def q_block(i, carry):
        outs = []
        for h in range(2):
            g = gate_s[h, i]
            cnt = jnp.zeros((nb, blk), jnp.int32)
            for m in range(nb):
                gm = g[m:m + 1, :]
                beats = (gm > g) | ((gm == g) & (m < blk_iota))
                cnt = cnt + jnp.where(beats & (m < i), 1, 0)
            sel = (blk_iota < i) & (cnt < MOBA_TOPK)
            neg_s[...] = jnp.where(sel, 0.0, NEG_INF)

            qt_h = jnp.where(row_head == h, qt_s[i], 0.0).astype(BF16)
            far_bias = far_ref[2 * hp + h]

            def scores(j):
                kj = k_ref[0, pl.ds(pl.multiple_of(j * blk, blk), blk), :].astype(BF16)
                return jnp.dot(kj, qt_h, preferred_element_type=F32)

            def pv(j, p_t):
                vt_h = vt_s[j, h * HEAD_DIM:(h + 1) * HEAD_DIM, :].astype(BF16)
                return jnp.dot(vt_h, p_t.astype(BF16), preferred_element_type=F32)

            s = scores(i) + t0_ref[h]
            m_run = jnp.max(s, axis=0, keepdims=True)
            p_t = jnp.exp(s - m_run)
            l_run = jnp.sum(p_t, axis=0, keepdims=True)
            acc = pv(i, p_t)

            def update(j, s, state):
                m_run, l_run, acc = state
                s = s + neg_s[pl.ds(j, 1), :]
                m_new = jnp.maximum(m_run, jnp.max(s, axis=0, keepdims=True))
                alpha = jnp.exp(m_run - m_new)
                p_t = jnp.exp(s - m_new)
                l_new = alpha * l_run + jnp.sum(p_t, axis=0, keepdims=True)
                return m_new, l_new, alpha * acc + pv(j, p_t)

            jp = jnp.maximum(i - 1, 0)
            state = update(jp, scores(jp) + t1_ref[h], (m_run, l_run, acc))

            def far_body(j, state):
                return update(j, scores(j) + far_bias, state)

            m_run, l_run, acc = lax.fori_loop(0, jnp.maximum(i - 1, 0), far_body, state)
            outs.append(acc / l_run)
        o_t = jnp.concatenate(outs, axis=0)
        o_ref[0, pl.ds(pl.multiple_of(i * blk, blk), blk), :] = o_t.T
        return carry

    lax.fori_loop(0, nb, q_block, 0)


def _moba_attention(p3, t_own, t_prev, far):
    b, s, _ = p3.shape
    nb = s // MOBA_BLOCK
    n_pairs = MOBA_HEADS // 2
    blk_spec = lambda off: pl.BlockSpec((1, s, LANES), lambda bi, hp: (bi, 0, off + hp))
    tab_spec = pl.BlockSpec((2, MOBA_BLOCK, MOBA_BLOCK), lambda bi, hp: (hp, 0, 0))
    return pl.pallas_call(
        functools.partial(_moba_kernel, nb=nb),
        grid=(b, n_pairs),
        in_specs=[pl.BlockSpec(memory_space=pltpu.SMEM),
                  blk_spec(0), blk_spec(n_pairs), blk_spec(2 * n_pairs), tab_spec, tab_spec],
        out_specs=pl.BlockSpec((1, s, LANES), lambda bi, hp: (bi, 0, hp)),
        out_shape=jax.ShapeDtypeStruct((b, s, MOBA_W), F32),
        scratch_shapes=[pltpu.VMEM((nb, LANES, MOBA_BLOCK), F32),
                        pltpu.VMEM((nb, LANES, MOBA_BLOCK), F32),
                        pltpu.VMEM((2, nb, nb, MOBA_BLOCK), F32),
                        pltpu.VMEM((nb, MOBA_BLOCK), F32)],
        compiler_params=_cparams(("parallel", "parallel")),
        name="moba",
    )(far, p3, p3, p3, t_own, t_prev)


def _moba2_kernel(far_ref, q_ref, k_ref, v_ref, t0_ref, t1_ref, o_ref,
                  qt_s, kb_s, vt_s, sc_s, *, nb):
    hp = pl.program_id(1)
    blk = MOBA_BLOCK
    scale = HEAD_DIM ** -0.5
    row_head = lax.broadcasted_iota(jnp.int32, (LANES, 1), 0) // HEAD_DIM
    lane_head = lax.broadcasted_iota(jnp.int32, (1, LANES), 1) // HEAD_DIM

    kmean_rows = []
    for j in range(nb):
        sl = pl.ds(j * blk, blk)
        qt_s[j] = (q_ref[0, sl, :] * scale).T
        vt_s[j] = v_ref[0, sl, :].T.astype(BF16)
        kj = k_ref[0, sl, :]
        kb_s[j] = kj.astype(BF16)
        kmean_rows.append(jnp.mean(kj, axis=0, keepdims=True))
    kmean = jnp.concatenate(kmean_rows, axis=0)
    km2 = jnp.concatenate([jnp.where(lane_head == 0, kmean, 0.0),
                           jnp.where(lane_head == 1, kmean, 0.0)], axis=0)
    blk_iota = lax.broadcasted_iota(jnp.int32, (nb, blk), 0)

    for i in range(nb):
        qt = qt_s[i]
        negs = [None, None]
        if i > MOBA_TOPK:
            gate2 = _hdot(km2, qt)
            for h in range(2):
                g = gate2[h * nb:(h + 1) * nb, :]
                cnt = jnp.zeros((nb, blk), jnp.int32)
                for m in range(i):
                    gm = g[m:m + 1, :]
                    beats = (gm > g) | ((gm == g) & (m < blk_iota))
                    cnt = cnt + jnp.where(beats, 1, 0)
                negs[h] = jnp.where(cnt < MOBA_TOPK, 0.0, NEG_INF)
        outs = []
        for h in range(2):
            qt_h = jnp.where(row_head == h, qt, 0.0).astype(BF16)
            far_bias = far_ref[2 * hp + h]
            m_run = None
            for j in range(i + 1):
                s = jnp.dot(kb_s[j], qt_h, preferred_element_type=F32)
                if j == i:
                    s = s + t0_ref[h]
                else:
                    bias = t1_ref[h] if j == i - 1 else far_bias
                    if negs[h] is not None:
                        bias = bias + negs[h][j:j + 1, :]
                    s = s + bias
                sc_s[h, j] = s
                cm = jnp.max(s, axis=0, keepdims=True)
                m_run = cm if m_run is None else jnp.maximum(m_run, cm)
            l_run = jnp.zeros((1, blk), F32)
            acc = jnp.zeros((HEAD_DIM, blk), F32)
            for j in range(i + 1):
                p_t = jnp.exp(sc_s[h, j] - m_run)
                l_run = l_run + jnp.sum(p_t, axis=0, keepdims=True)
                vt_h = vt_s[j, h * HEAD_DIM:(h + 1) * HEAD_DIM, :]
                acc = acc + jnp.dot(vt_h, p_t.astype(BF16), preferred_element_type=F32)
            outs.append(acc / l_run)
        o_t = jnp.concatenate(outs, axis=0)
        o_ref[0, pl.ds(i * blk, blk), :] = o_t.T


def _moba2_attention(p3, t_own, t_prev, far):
    b, s, _ = p3.shape
    nb = s // MOBA_BLOCK
    n_pairs = MOBA_HEADS // 2
    blk_spec = lambda off: pl.BlockSpec((1, s, LANES), lambda bi, hp: (bi, 0, off + hp))
    tab_spec = pl.BlockSpec((2, MOBA_BLOCK, MOBA_BLOCK), lambda bi, hp: (hp, 0, 0))
    return pl.pallas_call(
        functools.partial(_moba2_kernel, nb=nb),
        grid=(b, n_pairs),
        in_specs=[pl.BlockSpec(memory_space=pltpu.SMEM),
                  blk_spec(0), blk_spec(n_pairs), blk_spec(2 * n_pairs), tab_spec, tab_spec],
        out_specs=pl.BlockSpec((1, s, LANES), lambda bi, hp: (bi, 0, hp)),
        out_shape=jax.ShapeDtypeStruct((b, s, MOBA_W), F32),
        scratch_shapes=[pltpu.VMEM((nb, LANES, MOBA_BLOCK), F32),
                        pltpu.VMEM((nb, MOBA_BLOCK, LANES), BF16),
                        pltpu.VMEM((nb, LANES, MOBA_BLOCK), BF16),
                        pltpu.VMEM((2, nb, MOBA_BLOCK, MOBA_BLOCK), F32)],
        compiler_params=_cparams(("parallel", "parallel")),
        name="moba",
    )(far, p3, p3, p3, t_own, t_prev)


def _softplus(z):
    return jnp.maximum(z, 0.0) + jnp.log(1.0 + jnp.exp(-jnp.abs(z)))


def _sigmoid(z):
    return 1.0 / (1.0 + jnp.exp(-z))


def _rwkv_kernel(r_ref, k_ref, v_ref, wa_ref, g_ref,
                 mu_r_ref, mu_k_ref, mu_v_ref, mu_wa_ref, mu_g_ref,
                 w0_ref, wup_ref, a0_ref, aup_ref, gup_ref, kk_ref, ka_ref, rk_ref,
                 gng_ref, gnb_ref, hsum_ref, hmask_ref, tri_ref,
                 o_ref, st_s, prev_r, prev_k, prev_v, prev_wa, prev_g):
    c = RWKV_CHUNK
    n_heads = RWKV_HEADS

    @pl.when(pl.program_id(1) == 0)
    def _():
        st_s[...] = jnp.zeros_like(st_s)
        prev_r[...] = jnp.zeros_like(prev_r)
        prev_k[...] = jnp.zeros_like(prev_k)
        prev_v[...] = jnp.zeros_like(prev_v)
        prev_wa[...] = jnp.zeros_like(prev_wa)
        prev_g[...] = jnp.zeros_like(prev_g)

    def mix(x_ref, prev_ref, mu_ref):
        x = x_ref[0]
        row = lax.broadcasted_iota(jnp.int32, x.shape, 0)
        shifted = jnp.where(row == 0, prev_ref[...], pltpu.roll(x, 1, axis=0))
        prev_ref[...] = x[c - 1:c, :]
        return x + (shifted - x) * mu_ref[...]

    r = mix(r_ref, prev_r, mu_r_ref)
    k = mix(k_ref, prev_k, mu_k_ref)
    v = mix(v_ref, prev_v, mu_v_ref)
    xwa = mix(wa_ref, prev_wa, mu_wa_ref)
    xg = mix(g_ref, prev_g, mu_g_ref)

    hsum = hsum_ref[...]
    w = -_softplus(-(w0_ref[...] + _bdot(jnp.tanh(xwa), wup_ref[...]))) - 0.5
    logd = -jnp.exp(w)
    a = _sigmoid(a0_ref[...] + _bdot(xwa, aup_ref[...]))
    gate = _bdot(_sigmoid(xg), gup_ref[...])
    kk = k * kk_ref[...]
    kk = kk / jnp.maximum(jnp.sqrt(_hdot(kk * kk, hsum)), 1e-12)
    k2 = k * (1.0 + (a - 1.0) * ka_ref[...])

    tri = tri_ref[...]
    cum = _hdot(tri, logd)
    cum_last = cum[c - 1:c, :]
    g_in = jnp.exp(cum)
    g_inv = jnp.exp(-cum)
    g_tail = jnp.exp(cum_last - cum)
    a_t = -kk * jnp.exp(cum - logd)
    b_t = kk * a * g_inv
    k_t = k2 * g_inv
    r_t = r * g_in
    b_hat = kk * a * g_tail
    k_hat = k2 * g_tail

    st = st_s[...]
    p0 = _bdot(a_t, st)
    o0 = _bdot(r_t, st)

    row = lax.broadcasted_iota(jnp.int32, (c, c), 0)
    col = lax.broadcasted_iota(jnp.int32, (c, c), 1)
    strict = row > col
    incl = row >= col
    eye = (row == col).astype(F32)
    bk = jnp.concatenate([b_t, k_t], axis=0).astype(BF16)
    lane_head = lax.broadcasted_iota(jnp.int32, (1, RWKV_W), 1) // HEAD_DIM

    u_all = jnp.zeros((c, RWKV_W), F32)
    o_all = jnp.zeros((c, RWKV_W), F32)
    for h in range(n_heads):
        hm = lane_head == h
        ar = jnp.concatenate([jnp.where(hm, a_t, 0.0), jnp.where(hm, r_t, 0.0)], axis=0)
        m4 = _dot_nt(ar.astype(BF16), bk)
        l_ab = jnp.where(strict, m4[:c, :c], 0.0)
        l_ak = jnp.where(strict, m4[:c, c:], 0.0)
        m_rb = jnp.where(incl, m4[c:, :c], 0.0)
        m_rk = jnp.where(incl, m4[c:, c:], 0.0)
        t_inv = eye + l_ab
        pw = l_ab
        span = 2
        while span < c:
            pw = _hdot(pw, pw)
            t_inv = t_inv + _hdot(t_inv, pw)
            span *= 2
        u_h = _hdot(t_inv, p0 + _bdot(l_ak, v))
        u_all = jnp.where(hm, u_h, u_all)
        o_h = _bdot(m_rb, u_h) + _bdot(m_rk, v)
        o_all = jnp.where(hm, o_h, o_all)
    o = o0 + o_all

    upd = _dot_tn(b_hat.astype(BF16), u_all.astype(BF16)) + _dot_tn(k_hat.astype(BF16), v.astype(BF16))
    first_row = (lax.broadcasted_iota(jnp.int32, (c, RWKV_W), 0) == 0)
    g_col = _dot_tn(jnp.where(first_row, jnp.exp(cum_last), 0.0), jnp.ones((c, RWKV_W), F32), precision=HI)
    st_s[...] = g_col * st + hmask_ref[...] * upd

    inv_n = 1.0 / HEAD_DIM
    mean = _hdot(o, hsum) * inv_n
    dev = o - mean
    var = _hdot(dev * dev, hsum) * inv_n
    y = dev * lax.rsqrt(var + RWKV_GN_EPS) * gng_ref[...] + gnb_ref[...]
    y = y + _hdot(r * k2 * rk_ref[...], hsum) * v
    o_ref[0] = y * gate


def _rwkv(p3, mu, w0, w_up, a0, a_up, g_up, k_k, k_a, r_k, gn_g, gn_b):
    b, s, _ = p3.shape
    c = RWKV_CHUNK
    w = RWKV_W
    base = 3 * MOBA_W
    assert base % w == 0 and (base + 3 * w) % LANES == 0
    lora = DECAY_LORA + AAA_LORA

    def col_spec(width, off_cols):
        assert off_cols % width == 0
        return pl.BlockSpec((1, c, width), lambda bi, ci: (bi, ci, off_cols // width))

    row = lambda x: x.reshape(1, -1).astype(F32)
    const = lambda shape: pl.BlockSpec(shape, lambda bi, ci: (0,) * len(shape))
    wup_pad = jnp.concatenate([w_up, jnp.zeros((AAA_LORA, w), F32)], axis=0).astype(BF16)
    aup_pad = jnp.concatenate([jnp.zeros((DECAY_LORA, w), F32), a_up], axis=0).astype(BF16)
    head = np.arange(w) // HEAD_DIM
    hsum = jnp.asarray((head[:, None] == head[None, :]).astype(np.float32))
    tri = jnp.asarray(np.tril(np.ones((c, c), np.float32)))
    vec_args = [mu[:w], mu[w:2 * w], mu[2 * w:3 * w], mu[3 * w:3 * w + lora], mu[3 * w + lora:],
                w0, None, a0, None, None, k_k, k_a, r_k.reshape(-1), gn_g, gn_b]
    args = [p3, p3, p3, p3, p3]
    specs = [col_spec(w, base), col_spec(w, base + w), col_spec(w, base + 2 * w),
             col_spec(lora, base + 3 * w), col_spec(GATE_LORA, base + 3 * w + lora)]
    mats = {6: wup_pad, 8: aup_pad, 9: g_up.astype(BF16)}
    for idx, a in enumerate(vec_args):
        arr = mats[idx] if a is None else row(a)
        args.append(arr)
        specs.append(const(arr.shape))
    for arr in (hsum, hsum, tri):
        args.append(arr)
        specs.append(const(arr.shape))
    return pl.pallas_call(
        _rwkv_kernel,
        grid=(b, s // c),
        in_specs=specs,
        out_specs=pl.BlockSpec((1, c, w), lambda bi, ci: (bi, ci, 0)),
        out_shape=jax.ShapeDtypeStruct((b, s, w), F32),
        scratch_shapes=[pltpu.VMEM((w, w), F32), pltpu.VMEM((1, w), F32), pltpu.VMEM((1, w), F32),
                        pltpu.VMEM((1, w), F32), pltpu.VMEM((1, lora), F32),
                        pltpu.VMEM((1, GATE_LORA), F32)],
        compiler_params=_cparams(("parallel", "arbitrary")),
        name="rwkv",
    )(*args)


def _split_bf16(x, terms):
    parts = []
    for _ in range(terms):
        hi = x.astype(BF16)
        parts.append(hi)
        x = x - hi.astype(F32)
    return parts


def _dot_exact_rhs(x, m_bf16, terms):
    acc = None
    for part in _split_bf16(x, terms):
        d = jnp.dot(part, m_bf16, preferred_element_type=F32)
        acc = d if acc is None else acc + d
    return acc


def _dot_parts(a_parts, b_parts):
    dot = functools.partial(jnp.dot, preferred_element_type=F32)
    acc = dot(a_parts[0], b_parts[0])
    if len(a_parts) > 1:
        acc = acc + (dot(a_parts[0], b_parts[1]) + dot(a_parts[1], b_parts[0]))
    return acc


def _rwkv2_kernel(r_ref, k_ref, v_ref, wa_ref, g_ref,
                  mu_r_ref, mu_k_ref, mu_v_ref, mu_wa_ref, mu_g_ref,
                  w0_ref, wup_ref, a0_ref, aup_ref, gup_ref, kk_ref, ka_ref, rk_ref,
                  gng_ref, gnb_ref, hsum_ref, tri_ref,
                  o_ref, st_s, prev_r, prev_k, prev_v, prev_wa, prev_g, *, n_seq):
    c = RWKV_CHUNK
    mid = c // 2

    @pl.when(pl.program_id(1) == 0)
    def _():
        st_s[...] = jnp.zeros_like(st_s)
        prev_r[...] = jnp.zeros_like(prev_r)
        prev_k[...] = jnp.zeros_like(prev_k)
        prev_v[...] = jnp.zeros_like(prev_v)
        prev_wa[...] = jnp.zeros_like(prev_wa)
        prev_g[...] = jnp.zeros_like(prev_g)

    hsum = hsum_ref[...]
    tri = tri_ref[...]
    row = lax.broadcasted_iota(jnp.int32, (c, c), 0)
    col = lax.broadcasted_iota(jnp.int32, (c, c), 1)
    strict = row > col
    incl = row >= col
    eye = (row == col).astype(F32)
    lane_head = lax.broadcasted_iota(jnp.int32, (1, RWKV_W), 1) // HEAD_DIM
    hmask = hsum.astype(F32)

    def mix(g, x_ref, prev_ref, mu_ref):
        x = x_ref[g]
        rows = lax.broadcasted_iota(jnp.int32, x.shape, 0)
        shifted = jnp.where(rows == 0, prev_ref[g], pltpu.roll(x, 1, axis=0))
        prev_ref[g] = x[c - 1:c, :]
        return x + (shifted - x) * mu_ref[...]

    def prepare(g):
        r = mix(g, r_ref, prev_r, mu_r_ref)
        k = mix(g, k_ref, prev_k, mu_k_ref)
        v = mix(g, v_ref, prev_v, mu_v_ref)
        xwa = mix(g, wa_ref, prev_wa, mu_wa_ref)
        xg = mix(g, g_ref, prev_g, mu_g_ref)

        w = -_softplus(-(w0_ref[...] + _bdot(jnp.tanh(xwa), wup_ref[...]))) - 0.5
        logd = -jnp.exp(w)
        a = _sigmoid(a0_ref[...] + _bdot(xwa, aup_ref[...]))
        gate = _bdot(_sigmoid(xg), gup_ref[...])
        kk = k * kk_ref[...]
        kk = kk / jnp.maximum(jnp.sqrt(_dot_exact_rhs(kk * kk, hsum, 2)), 1e-12)
        k2 = k * (1.0 + (a - 1.0) * ka_ref[...])

        parts = _split_bf16(logd, 3)
        cum = (jnp.dot(tri, parts[0], preferred_element_type=F32)
               + jnp.dot(tri, parts[1], preferred_element_type=F32)
               + jnp.dot(tri, parts[2], preferred_element_type=F32))
        cum_last = cum[c - 1:c, :]
        ref = cum[mid - 1:mid, :]
        rel = cum - ref
        g_in = jnp.exp(rel)
        g_inv = jnp.exp(-rel)
        g_tail = jnp.exp(cum_last - cum)
        a_t = -kk * jnp.exp(rel - logd)
        b_t = kk * a * g_inv
        k_t = k2 * g_inv
        r_t = r * g_in
        b_hat = kk * a * g_tail
        k_hat = k2 * g_tail

        st = st_s[g]
        st_ref = (st * jnp.exp(ref)).astype(BF16)
        p0 = _dot_nt(a_t.astype(BF16), st_ref)
        o0 = _dot_nt(r_t.astype(BF16), st_ref)

        bk = jnp.concatenate([b_t, k_t], axis=0).astype(BF16)
        v_bf = v.astype(BF16)
        quads = []
        for h in range(RWKV_HEADS):
            hm = lane_head == h
            ar = jnp.concatenate([jnp.where(hm, a_t, 0.0), jnp.where(hm, r_t, 0.0)], axis=0)
            m4 = _dot_nt(ar.astype(BF16), bk)
            quads.append((jnp.where(strict, m4[:c, :c], 0.0), jnp.where(strict, m4[:c, c:], 0.0),
                          jnp.where(incl, m4[c:, :c], 0.0).astype(BF16),
                          jnp.where(incl, m4[c:, c:], 0.0).astype(BF16)))
        return dict(r=r, k2=k2, v=v, v_bf=v_bf, gate=gate, p0=p0, o0=o0, b_hat=b_hat, k_hat=k_hat,
                    st=st, decay=jnp.exp(cum_last), quads=quads)

    seqs = [prepare(g) for g in range(n_seq)]
    pairs = [(g, h) for g in range(n_seq) for h in range(RWKV_HEADS)]

    pw = {gh: seqs[gh[0]]["quads"][gh[1]][0] for gh in pairs}
    t_inv = {gh: eye + pw[gh] for gh in pairs}
    span = 2
    pw_parts = {gh: _split_bf16(pw[gh], RWKV_INV_TERMS) for gh in pairs}
    while span < c:
        pw = {gh: _dot_parts(pw_parts[gh], pw_parts[gh]) for gh in pairs}
        pw_parts = {gh: _split_bf16(pw[gh], RWKV_INV_TERMS) for gh in pairs}
        t_parts = {gh: _split_bf16(t_inv[gh], RWKV_INV_TERMS) for gh in pairs}
        t_inv = {gh: t_inv[gh] + _dot_parts(t_parts[gh], pw_parts[gh]) for gh in pairs}
        span *= 2

    w_loc = {(g, h): seqs[g]["p0"] + jnp.dot(seqs[g]["quads"][h][1].astype(BF16), seqs[g]["v_bf"],
                                             preferred_element_type=F32) for g, h in pairs}
    u_h = {gh: _dot_parts(_split_bf16(t_inv[gh], RWKV_INV_TERMS), _split_bf16(w_loc[gh], RWKV_INV_TERMS))
           for gh in pairs}
    o_h = {(g, h): (jnp.dot(seqs[g]["quads"][h][2], u_h[(g, h)].astype(BF16), preferred_element_type=F32)
                    + jnp.dot(seqs[g]["quads"][h][3], seqs[g]["v_bf"], preferred_element_type=F32))
           for g, h in pairs}

    for g in range(n_seq):
        sq = seqs[g]
        u_all = jnp.zeros((c, RWKV_W), F32)
        o_all = jnp.zeros((c, RWKV_W), F32)
        for h in range(RWKV_HEADS):
            hm = lane_head == h
            u_all = jnp.where(hm, u_h[(g, h)], u_all)
            o_all = jnp.where(hm, o_h[(g, h)], o_all)
        o = sq["o0"] + o_all
        upd = (_dot_tn(u_all.astype(BF16), sq["b_hat"].astype(BF16))
               + _dot_tn(sq["v_bf"], sq["k_hat"].astype(BF16)))
        st_s[g] = sq["st"] * sq["decay"] + hmask * upd

        inv_n = 1.0 / HEAD_DIM
        mean = _dot_exact_rhs(o, hsum, 2) * inv_n
        dev = o - mean
        var = _dot_exact_rhs(dev * dev, hsum, 2) * inv_n
        y = dev * lax.rsqrt(var + RWKV_GN_EPS) * gng_ref[...] + gnb_ref[...]
        y = y + _dot_exact_rhs(sq["r"] * sq["k2"] * rk_ref[...], hsum, 2) * sq["v"]
        o_ref[g] = y * sq["gate"]


def _rwkv2(p3, mu, w0, w_up, a0, a_up, g_up, k_k, k_a, r_k, gn_g, gn_b):
    b, s, _ = p3.shape
    c = RWKV_CHUNK
    w = RWKV_W
    n_seq = RWKV_SEQS_PER_STEP if b % RWKV_SEQS_PER_STEP == 0 else 1
    base = 3 * MOBA_W
    assert base % w == 0 and (base + 3 * w) % LANES == 0
    lora = DECAY_LORA + AAA_LORA

    def col_spec(width, off_cols):
        assert off_cols % width == 0
        return pl.BlockSpec((n_seq, c, width), lambda bi, ci: (bi, ci, off_cols // width))

    row = lambda x: x.reshape(1, -1).astype(F32)
    const = lambda shape: pl.BlockSpec(shape, lambda bi, ci: (0,) * len(shape))
    wup_pad = jnp.concatenate([w_up, jnp.zeros((AAA_LORA, w), F32)], axis=0).astype(BF16)
    aup_pad = jnp.concatenate([jnp.zeros((DECAY_LORA, w), F32), a_up], axis=0).astype(BF16)
    head = np.arange(w) // HEAD_DIM
    hsum = jnp.asarray((head[:, None] == head[None, :]).astype(np.float32)).astype(BF16)
    tri = jnp.asarray(np.tril(np.ones((c, c), np.float32))).astype(BF16)
    vec_args = [mu[:w], mu[w:2 * w], mu[2 * w:3 * w], mu[3 * w:3 * w + lora], mu[3 * w + lora:],
                w0, None, a0, None, None, k_k, k_a, r_k.reshape(-1), gn_g, gn_b]
    args = [p3, p3, p3, p3, p3]
    specs = [col_spec(w, base), col_spec(w, base + w), col_spec(w, base + 2 * w),
             col_spec(lora, base + 3 * w), col_spec(GATE_LORA, base + 3 * w + lora)]
    mats = {6: wup_pad, 8: aup_pad, 9: g_up.astype(BF16)}
    for idx, a in enumerate(vec_args):
        arr = mats[idx] if a is None else row(a)
        args.append(arr)
        specs.append(const(arr.shape))
    for arr in (hsum, tri):
        args.append(arr)
        specs.append(const(arr.shape))
    return pl.pallas_call(
        functools.partial(_rwkv2_kernel, n_seq=n_seq),
        grid=(b // n_seq, s // c),
        in_specs=specs,
        out_specs=pl.BlockSpec((n_seq, c, w), lambda bi, ci: (bi, ci, 0)),
        out_shape=jax.ShapeDtypeStruct((b, s, w), F32),
        scratch_shapes=[pltpu.VMEM((n_seq, w, w), F32), pltpu.VMEM((n_seq, 1, w), F32),
                        pltpu.VMEM((n_seq, 1, w), F32), pltpu.VMEM((n_seq, 1, w), F32),
                        pltpu.VMEM((n_seq, 1, lora), F32), pltpu.VMEM((n_seq, 1, GATE_LORA), F32)],
        compiler_params=_cparams(("parallel", "arbitrary")),
        name="rwkv",
    )(*args)


def _mem_attn_kernel(q_ref, kv_ref, o_ref):
    scale = HEAD_DIM ** -0.5
    lane_head = lax.broadcasted_iota(jnp.int32, (1, MEM_W), 1) // HEAD_DIM
    q = q_ref[0] * scale
    mk = kv_ref[0, :, :MEM_W].astype(BF16)
    mv = kv_ref[0, :, MEM_W:].astype(BF16)
    out = jnp.zeros(q.shape, F32)
    for h in range(MEM_HEADS):
        hm = lane_head == h
        s = _dot_nt(jnp.where(hm, q, 0.0).astype(BF16), mk)
        s = s - jnp.max(s, axis=-1, keepdims=True)
        e = jnp.exp(s)
        p = e / jnp.sum(e, axis=-1, keepdims=True)
        out = jnp.where(hm, jnp.dot(p.astype(BF16), mv, preferred_element_type=F32), out)
    o_ref[0] = out


def _mem_attention(p3, mkv3, tq):
    b, s, _ = p3.shape
    m = mkv3.shape[1]
    off = (3 * MOBA_W + RWKV_COLS) // MEM_W
    assert off * MEM_W == 3 * MOBA_W + RWKV_COLS
    return pl.pallas_call(
        _mem_attn_kernel,
        grid=(b, s // tq),
        in_specs=[pl.BlockSpec((1, tq, MEM_W), lambda bi, i: (bi, i, off)),
                  pl.BlockSpec((1, m, 2 * MEM_W), lambda bi, i: (bi, 0, 0))],
        out_specs=pl.BlockSpec((1, tq, MEM_W), lambda bi, i: (bi, i, 0)),
        out_shape=jax.ShapeDtypeStruct((b, s, MEM_W), F32),
        compiler_params=_cparams(("parallel", "parallel")),
        name="mem_attn",
    )(p3, mkv3)


def _out_router_kernel(x_ref, ym_ref, yr_ref, ye_ref, wo1_ref, wo2_ref, wo3_ref, g_ref,
                       wr_ref, br_ref, ltri_ref,
                       x1_ref, h_ref, idx_ref, wgt_ref, rank_ref, cnt_ref, run_s):
    @pl.when(pl.program_id(0) == 0)
    def _():
        run_s[...] = jnp.zeros_like(run_s)

    x1 = (x_ref[...] + _bdot(ym_ref[...], wo1_ref[...]) + _bdot(yr_ref[...], wo2_ref[...])
          + _bdot(ye_ref[...], wo3_ref[...]))
    x1_ref[...] = x1
    h = _rms(x1, g_ref[...])
    h_ref[...] = _pack_halves(h)
    logits = _hdot(h, wr_ref[...]) + br_ref[...]
    tm = logits.shape[0]
    lane = lax.broadcasted_iota(jnp.int32, (tm, LANES), 1)

    vals, idxs = [], []
    lg = logits
    for _ in range(TOP_K):
        m = jnp.max(lg, axis=1, keepdims=True)
        idx = jnp.min(jnp.where(lg == m, lane, LANES), axis=1, keepdims=True)
        vals.append(m)
        idxs.append(idx)
        lg = jnp.where(lane == idx, NEG_INF, lg)
    exps = [jnp.exp(vk - vals[0]) for vk in vals]
    denom = exps[0] + exps[1] + exps[2] + exps[3]

    chosen = jnp.zeros((tm, LANES), F32)
    for idx in idxs:
        chosen = chosen + jnp.where(lane == idx, 1.0, 0.0)
    before = jnp.dot(ltri_ref[...], chosen.astype(BF16), preferred_element_type=F32) + run_s[...]
    run_s[...] = run_s[...] + jnp.sum(chosen, axis=0, keepdims=True)
    cnt_ref[...] = run_s[...]

    idx_out = jnp.zeros((tm, LANES), jnp.int32)
    wgt_out = jnp.zeros((tm, LANES), F32)
    rank_out = jnp.zeros((tm, LANES), jnp.int32)
    for kk in range(TOP_K):
        rank = jnp.sum(jnp.where(lane == idxs[kk], before, 0.0), axis=1, keepdims=True)
        idx_out = jnp.where(lane == kk, idxs[kk], idx_out)
        wgt_out = jnp.where(lane == kk, exps[kk] / denom, wgt_out)
        rank_out = jnp.where(lane == kk, rank.astype(jnp.int32), rank_out)
    idx_ref[...] = idx_out
    wgt_ref[...] = wgt_out
    rank_ref[...] = rank_out


def _out_router(x2, ym, yr, ye, w_out, g_ffn, w_router, b_router, tm):
    t, d = x2.shape
    wo = w_out.astype(BF16)
    wo1, wo2, wo3 = wo[:MOBA_W], wo[MOBA_W:MOBA_W + RWKV_W], wo[MOBA_W + RWKV_W:]
    wr = jnp.zeros((d, LANES), F32).at[:, :N_EXPERTS].set(w_router)
    br = jnp.full((1, LANES), NEG_INF, F32).at[0, :N_EXPERTS].set(b_router)
    ltri = jnp.asarray(np.tril(np.ones((tm, tm), np.float32), -1)).astype(BF16)
    tile = lambda n: pl.BlockSpec((tm, n), lambda i: (i, 0))
    const = lambda a: pl.BlockSpec(a.shape, lambda i: (0,) * a.ndim)
    g2 = g_ffn.reshape(1, d)
    return pl.pallas_call(
        _out_router_kernel,
        grid=(t // tm,),
        in_specs=[tile(d), tile(MOBA_W), tile(RWKV_W), tile(MEM_W), const(wo1), const(wo2), const(wo3),
                  const(g2), const(wr), const(br), const(ltri)],
        out_specs=[tile(d), tile(d // 2), tile(LANES), tile(LANES), tile(LANES),
                   pl.BlockSpec((1, LANES), lambda i: (0, 0))],
        out_shape=[jax.ShapeDtypeStruct((t, d), F32), jax.ShapeDtypeStruct((t, d // 2), jnp.uint32),
                   jax.ShapeDtypeStruct((t, LANES), jnp.int32), jax.ShapeDtypeStruct((t, LANES), F32),
                   jax.ShapeDtypeStruct((t, LANES), jnp.int32), jax.ShapeDtypeStruct((1, LANES), F32)],
        scratch_shapes=[pltpu.VMEM((1, LANES), F32)],
        compiler_params=_cparams(("arbitrary",)),
        name="out_router",
    )(x2, ym, yr, ye, wo1, wo2, wo3, g2, wr, br, ltri)


def _out_router2_kernel(x_ref, ym_ref, yr_ref, ye_ref, wo1_ref, wo2_ref, wo3_ref, g_ref,
                        wrh_ref, wrl_ref, br_ref, upper_ref, ones_ref,
                        x1_ref, h_ref, idx_ref, rank_ref, wgt_ref, cnt_ref, run_s):
    @pl.when(pl.program_id(0) == 0)
    def _():
        run_s[...] = jnp.zeros_like(run_s)

    x1 = (x_ref[...] + _bdot(ym_ref[...], wo1_ref[...]) + _bdot(yr_ref[...], wo2_ref[...])
          + _bdot(ye_ref[...], wo3_ref[...]))
    x1_ref[...] = x1
    h = _rms(x1, g_ref[...])
    h_ref[...] = _pack_halves(h)
    tm = h.shape[0]
    h_hi, h_lo = _split_bf16(h, 2)
    dot = functools.partial(jnp.dot, preferred_element_type=F32)
    logits = dot(h_hi, wrh_ref[...]) + (dot(h_hi, wrl_ref[...]) + dot(h_lo, wrh_ref[...])) + br_ref[...]
    lg = logits.T[:N_EXPERTS, :]
    e_iota = lax.broadcasted_iota(jnp.int32, (N_EXPERTS, tm), 0)

    vals, idxs = [], []
    for _ in range(TOP_K):
        m = jnp.max(lg, axis=0, keepdims=True)
        idx = jnp.min(jnp.where(lg == m, e_iota, N_EXPERTS), axis=0, keepdims=True)
        vals.append(m)
        idxs.append(idx)
        lg = jnp.where(e_iota == idx, NEG_INF, lg)
    exps = [jnp.exp(vk - vals[0]) for vk in vals]
    denom = exps[0] + exps[1] + exps[2] + exps[3]

    chosen = jnp.zeros((N_EXPERTS, tm), F32)
    for idx in idxs:
        chosen = chosen + jnp.where(e_iota == idx, 1.0, 0.0)
    chosen = chosen.astype(BF16)
    run = run_s[...]
    before = dot(chosen, upper_ref[...]) + jnp.concatenate([run] * (tm // LANES), axis=1)
    run_s[...] = run + dot(chosen, ones_ref[...])
    cnt_ref[...] = run_s[...]

    zero_i = jnp.zeros((8 - TOP_K, tm), jnp.int32)
    ranks = [jnp.sum(jnp.where(e_iota == idx, before, 0.0), axis=0, keepdims=True).astype(jnp.int32)
             for idx in idxs]
    idx_ref[0] = jnp.concatenate(idxs + [zero_i], axis=0)
    rank_ref[0] = jnp.concatenate(ranks + [zero_i], axis=0)
    wrows = jnp.concatenate([e / denom for e in exps] + [jnp.zeros((LANES - TOP_K, tm), F32)], axis=0)
    wgt_ref[...] = wrows.T


def _out_router2(x2, ym, yr, ye, w_out, g_ffn, w_router, b_router, tm):
    t, d = x2.shape
    wo = w_out.astype(BF16)
    wo1, wo2, wo3 = wo[:MOBA_W], wo[MOBA_W:MOBA_W + RWKV_W], wo[MOBA_W + RWKV_W:]
    wr = jnp.zeros((d, LANES), F32).at[:, :N_EXPERTS].set(w_router)
    wr_hi = wr.astype(BF16)
    wr_lo = (wr - wr_hi.astype(F32)).astype(BF16)
    br = jnp.full((1, LANES), NEG_INF, F32).at[0, :N_EXPERTS].set(b_router)
    upper = jnp.asarray(np.triu(np.ones((tm, tm), np.float32), 1)).astype(BF16)
    ones = jnp.ones((tm, LANES), BF16)
    tile = lambda n: pl.BlockSpec((tm, n), lambda i: (i, 0))
    slots = pl.BlockSpec((1, 8, tm), lambda i: (i, 0, 0))
    const = lambda a: pl.BlockSpec(a.shape, lambda i: (0,) * a.ndim)
    g2 = g_ffn.reshape(1, d)
    n = t // tm
    return pl.pallas_call(
        _out_router2_kernel,
        grid=(n,),
        in_specs=[tile(d), tile(MOBA_W), tile(RWKV_W), tile(MEM_W), const(wo1), const(wo2), const(wo3),
                  const(g2), const(wr_hi), const(wr_lo), const(br), const(upper), const(ones)],
        out_specs=[tile(d), tile(d // 2), slots, slots, tile(LANES),
                   pl.BlockSpec((N_EXPERTS, LANES), lambda i: (0, 0))],
        out_shape=[jax.ShapeDtypeStruct((t, d), F32), jax.ShapeDtypeStruct((t, d // 2), jnp.uint32),
                   jax.ShapeDtypeStruct((n, 8, tm), jnp.int32), jax.ShapeDtypeStruct((n, 8, tm), jnp.int32),
                   jax.ShapeDtypeStruct((t, LANES), F32), jax.ShapeDtypeStruct((N_EXPERTS, LANES), F32)],
        scratch_shapes=[pltpu.VMEM((N_EXPERTS, LANES), F32)],
        compiler_params=_cparams(("arbitrary",)),
        name="out_router",
    )(x2, ym, yr, ye, wo1, wo2, wo3, g2, wr_hi, wr_lo, br, upper, ones)


def _experts_kernel(be_ref, nused_ref, xs_ref, wgu_ref, bg_ref, bu_ref, wd_ref, bd_ref, perm_ref,
                    o_ref, wg_s, wu_s, wd_s):
    i = pl.program_id(0)
    prev = be_ref[jnp.maximum(i - 1, 0)]
    changed = (i == 0) | (be_ref[i] != prev)

    @pl.when(changed & (i < nused_ref[0]))
    def _():
        half = LANES
        for cblk in range(2 * D_EXPERT // (2 * half)):
            wt = wgu_ref[0, :, cblk * 2 * half:(cblk + 1) * 2 * half].astype(BF16)
            sep = jnp.dot(wt, perm_ref[...], preferred_element_type=F32).astype(BF16)
            wg_s[:, cblk * half:(cblk + 1) * half] = sep[:, :half]
            wu_s[:, cblk * half:(cblk + 1) * half] = sep[:, half:]
        wd_s[...] = wd_ref[0].astype(BF16)

    @pl.when(i < nused_ref[0])
    def _():
        xb = _unpack_halves(xs_ref[...]).astype(BF16)
        gate = jnp.dot(xb, wg_s[...], preferred_element_type=F32) + bg_ref[0]
        up = jnp.dot(xb, wu_s[...], preferred_element_type=F32) + bu_ref[0]
        gate = jnp.minimum(gate, SWIGLU_LIMIT)
        up = jnp.clip(up, -SWIGLU_LIMIT, SWIGLU_LIMIT)
        glu = gate * _sigmoid(gate * SWIGLU_ALPHA)
        act = ((up + 1.0) * glu).astype(BF16)
        o_ref[...] = _pack_halves(jnp.dot(act, wd_s[...], preferred_element_type=F32) + bd_ref[0])

    @pl.when(i >= nused_ref[0])
    def _():
        o_ref[...] = jnp.zeros_like(o_ref)


def _experts(blk_expert, n_used, xs, w_gate_up, b_gate_up, w_down, b_down):
    p_rows = xs.shape[0]
    d = 2 * xs.shape[1]
    n_blocks = p_rows // EXPERT_ROWS
    bg = b_gate_up[:, 0::2].reshape(N_EXPERTS, 1, D_EXPERT)
    bu = b_gate_up[:, 1::2].reshape(N_EXPERTS, 1, D_EXPERT)
    bd = b_down.reshape(N_EXPERTS, 1, d)
    perm_np = np.zeros((2 * LANES, 2 * LANES), np.float32)
    perm_np[2 * np.arange(LANES), np.arange(LANES)] = 1.0
    perm_np[2 * np.arange(LANES) + 1, LANES + np.arange(LANES)] = 1.0
    perm = jnp.asarray(perm_np).astype(BF16)
    by_expert = lambda shape: pl.BlockSpec((1,) + shape, lambda i, be, nu: (be[i], 0, 0))
    grid_spec = pltpu.PrefetchScalarGridSpec(
        num_scalar_prefetch=2,
        grid=(n_blocks,),
        in_specs=[pl.BlockSpec((EXPERT_ROWS, d // 2), lambda i, be, nu: (i, 0)),
                  by_expert((d, 2 * D_EXPERT)), by_expert((1, D_EXPERT)), by_expert((1, D_EXPERT)),
                  by_expert((D_EXPERT, d)), by_expert((1, d)),
                  pl.BlockSpec(perm.shape, lambda i, be, nu: (0, 0))],
        out_specs=pl.BlockSpec((EXPERT_ROWS, d // 2), lambda i, be, nu: (i, 0)),
        scratch_shapes=[pltpu.VMEM((d, D_EXPERT), BF16), pltpu.VMEM((d, D_EXPERT), BF16),
                        pltpu.VMEM((D_EXPERT, d), BF16)],
    )
    return pl.pallas_call(
        _experts_kernel,
        grid_spec=grid_spec,
        out_shape=jax.ShapeDtypeStruct((p_rows, d // 2), jnp.uint32),
        compiler_params=_cparams(("arbitrary",)),
        name="experts",
    )(blk_expert, n_used, xs, w_gate_up, bg, bu, w_down, bd, perm)


def _row_copies(e_ref, rank_ref, start_ref, n_tok, make_copy):
    def body(r, carry):
        for kk in range(TOP_K):
            j = kk * n_tok + r
            row = start_ref[e_ref[0, 0, j]] + rank_ref[0, 0, j]
            make_copy(r, kk, row).start()
        return carry
    lax.fori_loop(0, n_tok, body, 0)


def _dispatch_kernel(e_ref, rank_ref, start_ref, h_ref, xs_init_ref, xs_ref, sem):
    del xs_init_ref
    tb = h_ref.shape[0]
    _row_copies(e_ref, rank_ref, start_ref, tb,
                lambda r, kk, row: pltpu.make_async_copy(h_ref.at[pl.ds(r, 1)], xs_ref.at[pl.ds(row, 1)], sem))
    for _ in range(TOP_K):
        pltpu.make_async_copy(h_ref, xs_ref.at[pl.ds(0, tb)], sem).wait()


def _dispatch(e_flat, rank_flat, pad_starts, hp, p_rows, tb):
    t, half = hp.shape
    idx_spec = pl.BlockSpec((1, 1, tb * TOP_K), lambda i: (i, 0, 0), memory_space=pltpu.SMEM)
    xs_init = jnp.zeros((p_rows, half), jnp.uint32)
    return pl.pallas_call(
        _dispatch_kernel,
        grid=(t // tb,),
        in_specs=[idx_spec, idx_spec, pl.BlockSpec(memory_space=pltpu.SMEM),
                  pl.BlockSpec((tb, half), lambda i: (i, 0)), pl.BlockSpec(memory_space=pl.ANY)],
        out_specs=pl.BlockSpec(memory_space=pl.ANY),
        out_shape=jax.ShapeDtypeStruct((p_rows, half), jnp.uint32),
        scratch_shapes=[pltpu.SemaphoreType.DMA(())],
        input_output_aliases={4: 0},
        compiler_params=_cparams(("arbitrary",)),
        name="dispatch",
    )(e_flat, rank_flat, pad_starts, hp, xs_init)


def _combine_kernel(e_ref, rank_ref, e_next_ref, rank_next_ref, start_ref, ys_ref, x1_ref, w_ref, g_ref,
                    o_ref, buf, sem):
    i = pl.program_id(0)
    n = pl.num_programs(0)
    tb = x1_ref.shape[0]

    def gather(e, rank, slot):
        _row_copies(e, rank, start_ref, tb,
                    lambda r, kk, row: pltpu.make_async_copy(
                        ys_ref.at[pl.ds(row, 1)], buf.at[slot, kk, pl.ds(r, 1)], sem.at[slot]))

    @pl.when(i == 0)
    def _():
        gather(e_ref, rank_ref, 0)

    @pl.when(i + 1 < n)
    def _():
        gather(e_next_ref, rank_next_ref, (i + 1) % 2)

    slot = i % 2
    for kk in range(TOP_K):
        pltpu.make_async_copy(ys_ref.at[pl.ds(0, tb)], buf.at[slot, kk], sem.at[slot]).wait()
    acc = x1_ref[...]
    wts = w_ref[...]
    for kk in range(TOP_K):
        acc = acc + _unpack_halves(buf[slot, kk]) * wts[:, kk:kk + 1]
    o_ref[...] = _rms(acc, g_ref[...])


def _combine(e_flat, rank_flat, pad_starts, ys, x1, wts, g_final, tb):
    t, d = x1.shape
    n = t // tb
    half = ys.shape[1]
    idx_spec = pl.BlockSpec((1, 1, tb * TOP_K), lambda i: (i, 0, 0), memory_space=pltpu.SMEM)
    next_spec = pl.BlockSpec((1, 1, tb * TOP_K), lambda i: (jnp.minimum(i + 1, n - 1), 0, 0),
                             memory_space=pltpu.SMEM)
    return pl.pallas_call(
        _combine_kernel,
        grid=(n,),
        in_specs=[idx_spec, idx_spec, next_spec, next_spec, pl.BlockSpec(memory_space=pltpu.SMEM),
                  pl.BlockSpec(memory_space=pl.ANY),
                  pl.BlockSpec((tb, d), lambda i: (i, 0)),
                  pl.BlockSpec((tb, LANES), lambda i: (i, 0)),
                  pl.BlockSpec((1, d), lambda i: (0, 0))],
        out_specs=pl.BlockSpec((tb, d), lambda i: (i, 0)),
        out_shape=jax.ShapeDtypeStruct((t, d), F32),
        scratch_shapes=[pltpu.VMEM((2, TOP_K, tb, half), jnp.uint32), pltpu.SemaphoreType.DMA((2,))],
        compiler_params=_cparams(("arbitrary",)),
        name="combine",
    )(e_flat, rank_flat, e_flat, rank_flat, pad_starts, ys, x1, wts, g_final.reshape(1, d))


def _layer(x, mem, w_in, w_out, w_mem_kv, g_mix, g_mem, g_ffn, bias_tables, mu, w0, w_up, a0, a_up,
           g_up, k_k, k_a, r_k, gn_g, gn_b, w_router, b_router, w_gate_up, b_gate_up, w_down, b_down,
           g_last):
    b, s, d = x.shape
    m = mem.shape[1]
    t = b * s
    x2 = x.reshape(t, d)

    p = _norm_matmul(x2, g_mix, w_in.astype(BF16), 512, "in_proj")
    p3 = p.reshape(b, s, IN_COLS)
    y_moba = _moba2_attention(p3, *bias_tables)
    y_rwkv = _rwkv2(p3, mu, w0, w_up, a0, a_up, g_up, k_k, k_a, r_k, gn_g, gn_b)
    mkv = _norm_matmul(mem.reshape(b * m, d), g_mem, w_mem_kv.astype(BF16), 512, "mem_kv")
    y_mem = _mem_attention(p3, mkv.reshape(b, m, 2 * MEM_W), 512)

    tb = ROUTE_TOKENS
    x1, h2, idx_o, rank_o, wgt_p, cnt = _out_router2(
        x2, y_moba.reshape(t, MOBA_W), y_rwkv.reshape(t, RWKV_W), y_mem.reshape(t, MEM_W),
        w_out, g_ffn, w_router, b_router, tb)

    e_flat = idx_o[:, :TOP_K, :].reshape(t // tb, 1, tb * TOP_K)
    rank_flat = rank_o[:, :TOP_K, :].reshape(t // tb, 1, tb * TOP_K)
    counts = cnt[:, 0].astype(jnp.int32)
    padded = (counts + EXPERT_ROWS - 1) // EXPERT_ROWS * EXPERT_ROWS
    pad_ends = jnp.cumsum(padded)
    pad_starts = (pad_ends - padded).astype(jnp.int32)
    n_blocks = (t * TOP_K) // EXPERT_ROWS + N_EXPERTS
    blk_start = jnp.arange(n_blocks, dtype=jnp.int32) * EXPERT_ROWS
    blk_expert = jnp.minimum(jnp.sum(blk_start[:, None] >= pad_ends[None, :], axis=1),
                             N_EXPERTS - 1).astype(jnp.int32)
    n_used = (pad_ends[-1:] // EXPERT_ROWS).astype(jnp.int32)

    xs = _dispatch(e_flat, rank_flat, pad_starts, h2, n_blocks * EXPERT_ROWS, tb)
    ys = _experts(blk_expert, n_used, xs, w_gate_up, b_gate_up, w_down, b_down)
    out = _combine(e_flat, rank_flat, pad_starts, ys, x1, wgt_p, g_last, tb)
    return out.reshape(b, s, d)


def kernel(x, mem, w_in, w_out, w_mem_kv, g_mix, g_mem, g_ffn, g_final, rel_bias, rwkv_mu, rwkv_w0,
           rwkv_w_up, rwkv_a0, rwkv_a_up, rwkv_g_up, rwkv_k_k, rwkv_k_a, rwkv_r_k, rwkv_gn_g, rwkv_gn_b,
           w_router, b_router, w_gate_up, b_gate_up, w_down, b_down):
    depth = w_in.shape[0]
    assert depth == 1, "the final norm is fused into the last layer's combine kernel"
    bias_tables = _moba_bias_tables(rel_bias)
    l = 0
    return _layer(x, mem, w_in[l], w_out[l], w_mem_kv[l], g_mix[l], g_mem[l], g_ffn[l], bias_tables,
                  rwkv_mu[l], rwkv_w0[l], rwkv_w_up[l], rwkv_a0[l], rwkv_a_up[l], rwkv_g_up[l],
                  rwkv_k_k[l], rwkv_k_a[l], rwkv_r_k[l], rwkv_gn_g[l], rwkv_gn_b[l], w_router[l],
                  b_router[l], w_gate_up[l], b_gate_up[l], w_down[l], b_down[l], g_final)
```

```python
import functools
import math

import numpy as np
import jax
import jax.numpy as jnp
from jax import lax
from jax.experimental import pallas as pl
from jax.experimental.pallas import tpu as pltpu
from jax.experimental.pallas import tpu_sc as plsc

F32 = jnp.float32
BF16 = jnp.bfloat16
HI = lax.Precision.HIGHEST

D_MODEL = 1024
HEAD_DIM = 64
MOBA_HEADS = 6
RWKV_HEADS = 6
MEM_HEADS = 4
MOBA_W = MOBA_HEADS * HEAD_DIM
RWKV_W = RWKV_HEADS * HEAD_DIM
MEM_W = MEM_HEADS * HEAD_DIM
MOBA_BLOCK = 256
MOBA_TOPK = 3
N_BUCKETS = 32
MAX_DISTANCE = 128
DECAY_LORA = 64
AAA_LORA = 64
GATE_LORA = 128
RWKV_COLS = 3 * RWKV_W + DECAY_LORA + AAA_LORA + GATE_LORA
RWKV_GN_EPS = 64e-5
IN_COLS = 3 * MOBA_W + RWKV_COLS + MEM_W
N_EXPERTS = 32
TOP_K = 4
D_EXPERT = D_MODEL
SWIGLU_ALPHA = 1.702
SWIGLU_LIMIT = 7.0
RMS_EPS = 1e-5

LANES = 128
RWKV_CHUNK = 128
RWKV_SEQS_PER_STEP = 2
RWKV_INV_TERMS = 1
EXPERT_ROWS = 256
ROUTE_TOKENS = 256
SC_CORES = 2
SC_SUBCORES = 16
SC_GATHER_ROWS = 64
VMEM_LIMIT = 56 * 1024 * 1024
NEG_INF = float("-inf")


def _cparams(sem):
    return pltpu.CompilerParams(dimension_semantics=sem, vmem_limit_bytes=VMEM_LIMIT)


def _rms(x, g):
    return x * lax.rsqrt(jnp.mean(x * x, axis=-1, keepdims=True) + RMS_EPS) * g


def _bdot(a, b):
    return jnp.dot(a.astype(BF16), b.astype(BF16), preferred_element_type=F32)


def _hdot(a, b):
    return jnp.dot(a, b, preferred_element_type=F32, precision=HI)


def _dot_nt(a, b, **kw):
    return lax.dot_general(a, b, (((1,), (1,)), ((), ())), preferred_element_type=F32, **kw)


def _dot_tn(a, b, **kw):
    return lax.dot_general(a, b, (((0,), (0,)), ((), ())), preferred_element_type=F32, **kw)


def _pack_halves(x):
    n = x.shape[1] // 2
    lo = pltpu.bitcast(x[:, :n].astype(BF16).astype(F32), jnp.uint32)
    hi = pltpu.bitcast(x[:, n:].astype(BF16).astype(F32), jnp.uint32)
    return (hi & jnp.uint32(0xFFFF0000)) | (lo >> 16)


def _unpack_halves(w):
    lo = pltpu.bitcast(w << 16, F32)
    hi = pltpu.bitcast(w & jnp.uint32(0xFFFF0000), F32)
    return jnp.concatenate([lo, hi], axis=1)


def _norm_matmul_kernel(x_ref, g_ref, w_ref, o_ref):
    h = _rms(x_ref[...], g_ref[...])
    o_ref[...] = jnp.dot(h.astype(BF16), w_ref[...], preferred_element_type=F32)


def _norm_matmul(x, g, w_bf16, tm, name):
    t, d = x.shape
    n = w_bf16.shape[1]
    tm = min(tm, t)
    return pl.pallas_call(
        _norm_matmul_kernel,
        grid=(t // tm,),
        in_specs=[pl.BlockSpec((tm, d), lambda i: (i, 0)),
                  pl.BlockSpec((1, d), lambda i: (0, 0)),
                  pl.BlockSpec((d, n), lambda i: (0, 0))],
        out_specs=pl.BlockSpec((tm, n), lambda i: (i, 0)),
        out_shape=jax.ShapeDtypeStruct((t, n), F32),
        compiler_params=_cparams(("parallel",)),
        name=name,
    )(x, g.reshape(1, d), w_bf16)


def _t5_bucket_np(dist):
    n = np.maximum(dist, 0)
    max_exact = N_BUCKETS // 2
    nf = np.maximum(n, 1).astype(np.float64)
    large = max_exact + (np.log(nf / max_exact) / math.log(MAX_DISTANCE / max_exact)
                         * (N_BUCKETS - max_exact)).astype(np.int64)
    large = np.minimum(large, N_BUCKETS - 1)
    return np.where(n < max_exact, n, large)


def _moba_bias_tables(rel_bias):
    kk = np.arange(MOBA_BLOCK)[:, None]
    qq = np.arange(MOBA_BLOCK)[None, :]
    d_own = qq - kk
    d_prev = MOBA_BLOCK + qq - kk
    assert np.all(_t5_bucket_np(np.arange(MOBA_BLOCK + 1, 64 * MOBA_BLOCK)) == N_BUCKETS - 1)
    bias_t = rel_bias.astype(F32).T

    def lookup(bucket):
        bucket = jnp.asarray(bucket, jnp.int32)[None]
        tab = jnp.zeros((MOBA_HEADS,) + bucket.shape[1:], F32)
        for b in range(N_BUCKETS):
            tab = jnp.where(bucket == b, bias_t[:, b][:, None, None], tab)
        return tab

    t_own = jnp.where(jnp.asarray(d_own >= 0)[None], lookup(_t5_bucket_np(d_own)), NEG_INF)
    t_prev = lookup(_t5_bucket_np(d_prev))
    far = bias_t[:, N_BUCKETS - 1]
    return t_own, t_prev, far


def _moba_kernel(far_ref, q_ref, k_ref, v_ref, t0_ref, t1_ref, o_ref,
                 qt_s, vt_s, gate_s, neg_s, *, nb):
    hp = pl.program_id(1)
    blk = MOBA_BLOCK
    scale = HEAD_DIM ** -0.5
    row_head = lax.broadcasted_iota(jnp.int32, (LANES, 1), 0) // HEAD_DIM
    lane_head = lax.broadcasted_iota(jnp.int32, (1, LANES), 1) // HEAD_DIM

    kmean_rows = []
    for j in range(nb):
        sl = pl.ds(j * blk, blk)
        qt_s[j] = (q_ref[0, sl, :] * scale).T
        vt_s[j] = v_ref[0, sl, :].T
        kmean_rows.append(jnp.mean(k_ref[0, sl, :], axis=0, keepdims=True))
    kmean = jnp.concatenate(kmean_rows, axis=0)
    for h in range(2):
        km_h = jnp.where(lane_head == h, kmean, 0.0)
        for i in range(nb):
            gate_s[h, i] = _hdot(km_h, qt_s[i])

    blk_iota = lax.broadcasted_iota(jnp.int32, (nb, blk), 0)

    def q_block(i, carry):
        outs = []
        for h in range(2):
            g = gate_s[h, i]
            cnt = jnp.zeros((nb, blk), jnp.int32)
            for m in range(nb):
                gm = g[m:m + 1, :]
                beats = (gm > g) | ((gm == g) & (m < blk_iota))
                cnt = cnt + jnp.where(beats & (m < i), 1, 0)
            sel = (blk_iota < i) & (cnt < MOBA_TOPK)
            neg_s[...] = jnp.where(sel, 0.0, NEG_INF)

            qt_h = jnp.where(row_head == h, qt_s[i], 0.0).astype(BF16)
            far_bias = far_ref[2 * hp + h]

            def scores(j):
                kj = k_ref[0, pl.ds(pl.multiple_of(j * blk, blk), blk), :].astype(BF16)
                return jnp.dot(kj, qt_h, preferred_element_type=F32)

            def pv(j, p_t):
                vt_h = vt_s[j, h * HEAD_DIM:(h + 1) * HEAD_DIM, :].astype(BF16)
                return jnp.dot(vt_h, p_t.astype(BF16), preferred_element_type=F32)

            s = scores(i) + t0_ref[h]
            m_run = jnp.max(s, axis=0, keepdims=True)
            p_t = jnp.exp(s - m_run)
            l_run = jnp.sum(p_t, axis=0, keepdims=True)
            acc = pv(i, p_t)

            def update(j, s, state):
                m_run, l_run, acc = state
                s = s + neg_s[pl.ds(j, 1), :]
                m_new = jnp.maximum(m_run, jnp.max(s, axis=0, keepdims=True))
                alpha = jnp.exp(m_run - m_new)
                p_t = jnp.exp(s - m_new)
                l_new = alpha * l_run + jnp.sum(p_t, axis=0, keepdims=True)
                return m_new, l_new, alpha * acc + pv(j, p_t)

            jp = jnp.maximum(i - 1, 0)
            state = update(jp, scores(jp) + t1_ref[h], (m_run, l_run, acc))

            def far_body(j, state):
                return update(j, scores(j) + far_bias, state)

            m_run, l_run, acc = lax.fori_loop(0, jnp.maximum(i - 1, 0), far_body, state)
            outs.append(acc / l_run)
        o_t = jnp.concatenate(outs, axis=0)
        o_ref[0, pl.ds(pl.multiple_of(i * blk, blk), blk), :] = o_t.T
        return carry

    lax.fori_loop(0, nb, q_block, 0)


def _moba_attention(p3, t_own, t_prev, far):
    b, s, _ = p3.shape
    nb = s // MOBA_BLOCK
    n_pairs = MOBA_HEADS // 2
    blk_spec = lambda off: pl.BlockSpec((1, s, LANES), lambda bi, hp: (bi, 0, off + hp))
    tab_spec = pl.BlockSpec((2, MOBA_BLOCK, MOBA_BLOCK), lambda bi, hp: (hp, 0, 0))
    return pl.pallas_call(
        functools.partial(_moba_kernel, nb=nb),
        grid=(b, n_pairs),
        in_specs=[pl.BlockSpec(memory_space=pltpu.SMEM),
                  blk_spec(0), blk_spec(n_pairs), blk_spec(2 * n_pairs), tab_spec, tab_spec],
        out_specs=pl.BlockSpec((1, s, LANES), lambda bi, hp: (bi, 0, hp)),
        out_shape=jax.ShapeDtypeStruct((b, s, MOBA_W), F32),
        scratch_shapes=[pltpu.VMEM((nb, LANES, MOBA_BLOCK), F32),
                        pltpu.VMEM((nb, LANES, MOBA_BLOCK), F32),
                        pltpu.VMEM((2, nb, nb, MOBA_BLOCK), F32),
                        pltpu.VMEM((nb, MOBA_BLOCK), F32)],
        compiler_params=_cparams(("parallel", "parallel")),
        name="moba",
    )(far, p3, p3, p3, t_own, t_prev)


def _moba2_kernel(far_ref, q_ref, k_ref, v_ref, t0_ref, t1_ref, o_ref,
                  qt_s, kb_s, vt_s, sc_s, *, nb):
    hp = pl.program_id(1)
    blk = MOBA_BLOCK
    scale = HEAD_DIM ** -0.5
    row_head = lax.broadcasted_iota(jnp.int32, (LANES, 1), 0) // HEAD_DIM
    lane_head = lax.broadcasted_iota(jnp.int32, (1, LANES), 1) // HEAD_DIM

    kmean_rows = []
    for j in range(nb):
        sl = pl.ds(j * blk, blk)
        qt_s[j] = (q_ref[0, sl, :] * scale).T
        vt_s[j] = v_ref[0, sl, :].T.astype(BF16)
        kj = k_ref[0, sl, :]
        kb_s[j] = kj.astype(BF16)
        kmean_rows.append(jnp.mean(kj, axis=0, keepdims=True))
    kmean = jnp.concatenate(kmean_rows, axis=0)
    km2 = jnp.concatenate([jnp.where(lane_head == 0, kmean, 0.0),
                           jnp.where(lane_head == 1, kmean, 0.0)], axis=0)
    blk_iota = lax.broadcasted_iota(jnp.int32, (nb, blk), 0)

    for i in range(nb):
        qt = qt_s[i]
        negs = [None, None]
        if i > MOBA_TOPK:
            gate2 = _hdot(km2, qt)
            for h in range(2):
                g = gate2[h * nb:(h + 1) * nb, :]
                cnt = jnp.zeros((nb, blk), jnp.int32)
                for m in range(i):
                    gm = g[m:m + 1, :]
                    beats = (gm > g) | ((gm == g) & (m < blk_iota))
                    cnt = cnt + jnp.where(beats, 1, 0)
                negs[h] = jnp.where(cnt < MOBA_TOPK, 0.0, NEG_INF)
        outs = []
        for h in range(2):
            qt_h = jnp.where(row_head == h, qt, 0.0).astype(BF16)
            far_bias = far_ref[2 * hp + h]
            m_run = None
            for j in range(i + 1):
                s = jnp.dot(kb_s[j], qt_h, preferred_element_type=F32)
                if j == i:
                    s = s + t0_ref[h]
                else:
                    bias = t1_ref[h] if j == i - 1 else far_bias
                    if negs[h] is not None:
                        bias = bias + negs[h][j:j + 1, :]
                    s = s + bias
                sc_s[h, j] = s
                cm = jnp.max(s, axis=0, keepdims=True)
                m_run = cm if m_run is None else jnp.maximum(m_run, cm)
            l_run = jnp.zeros((1, blk), F32)
            acc = jnp.zeros((HEAD_DIM, blk), F32)
            for j in range(i + 1):
                p_t = jnp.exp(sc_s[h, j] - m_run)
                l_run = l_run + jnp.sum(p_t, axis=0, keepdims=True)
                vt_h = vt_s[j, h * HEAD_DIM:(h + 1) * HEAD_DIM, :]
                acc = acc + jnp.dot(vt_h, p_t.astype(BF16), preferred_element_type=F32)
            outs.append(acc / l_run)
        o_t = jnp.concatenate(outs, axis=0)
        o_ref[0, pl.ds(i * blk, blk), :] = o_t.T


def _moba2_attention(p3, t_own, t_prev, far):
    b, s, _ = p3.shape
    nb = s // MOBA_BLOCK
    n_pairs = MOBA_HEADS // 2
    blk_spec = lambda off: pl.BlockSpec((1, s, LANES), lambda bi, hp: (bi, 0, off + hp))
    tab_spec = pl.BlockSpec((2, MOBA_BLOCK, MOBA_BLOCK), lambda bi, hp: (hp, 0, 0))
    return pl.pallas_call(
        functools.partial(_moba2_kernel, nb=nb),
        grid=(b, n_pairs),
        in_specs=[pl.BlockSpec(memory_space=pltpu.SMEM),
                  blk_spec(0), blk_spec(n_pairs), blk_spec(2 * n_pairs), tab_spec, tab_spec],
        out_specs=pl.BlockSpec((1, s, LANES), lambda bi, hp: (bi, 0, hp)),
        out_shape=jax.ShapeDtypeStruct((b, s, MOBA_W), F32),
        scratch_shapes=[pltpu.VMEM((nb, LANES, MOBA_BLOCK), F32),
                        pltpu.VMEM((nb, MOBA_BLOCK, LANES), BF16),
                        pltpu.VMEM((nb, LANES, MOBA_BLOCK), BF16),
                        pltpu.VMEM((2, nb, MOBA_BLOCK, MOBA_BLOCK), F32)],
        compiler_params=_cparams(("parallel", "parallel")),
        name="moba",
    )(far, p3, p3, p3, t_own, t_prev)


def _softplus(z):
    return jnp.maximum(z, 0.0) + jnp.log(1.0 + jnp.exp(-jnp.abs(z)))


def _sigmoid(z):
    return 1.0 / (1.0 + jnp.exp(-z))


def _rwkv_kernel(r_ref, k_ref, v_ref, wa_ref, g_ref,
                 mu_r_ref, mu_k_ref, mu_v_ref, mu_wa_ref, mu_g_ref,
                 w0_ref, wup_ref, a0_ref, aup_ref, gup_ref, kk_ref, ka_ref, rk_ref,
                 gng_ref, gnb_ref, hsum_ref, hmask_ref, tri_ref,
                 o_ref, st_s, prev_r, prev_k, prev_v, prev_wa, prev_g):
    c = RWKV_CHUNK
    n_heads = RWKV_HEADS

    @pl.when(pl.program_id(1) == 0)
    def _():
        st_s[...] = jnp.zeros_like(st_s)
        prev_r[...] = jnp.zeros_like(prev_r)
        prev_k[...] = jnp.zeros_like(prev_k)
        prev_v[...] = jnp.zeros_like(prev_v)
        prev_wa[...] = jnp.zeros_like(prev_wa)
        prev_g[...] = jnp.zeros_like(prev_g)

    def mix(x_ref, prev_ref, mu_ref):
        x = x_ref[0]
        row = lax.broadcasted_iota(jnp.int32, x.shape, 0)
        shifted = jnp.where(row == 0, prev_ref[...], pltpu.roll(x, 1, axis=0))
        prev_ref[...] = x[c - 1:c, :]
        return x + (shifted - x) * mu_ref[...]

    r = mix(r_ref, prev_r, mu_r_ref)
    k = mix(k_ref, prev_k, mu_k_ref)
    v = mix(v_ref, prev_v, mu_v_ref)
    xwa = mix(wa_ref, prev_wa, mu_wa_ref)
    xg = mix(g_ref, prev_g, mu_g_ref)

    hsum = hsum_ref[...]
    w = -_softplus(-(w0_ref[...] + _bdot(jnp.tanh(xwa), wup_ref[...]))) - 0.5
    logd = -jnp.exp(w)
    a = _sigmoid(a0_ref[...] + _bdot(xwa, aup_ref[...]))
    gate = _bdot(_sigmoid(xg), gup_ref[...])
    kk = k * kk_ref[...]
    kk = kk / jnp.maximum(jnp.sqrt(_hdot(kk * kk, hsum)), 1e-12)
    k2 = k * (1.0 + (a - 1.0) * ka_ref[...])

    tri = tri_ref[...]
    cum = _hdot(tri, logd)
    cum_last = cum[c - 1:c, :]
    g_in = jnp.exp(cum)
    g_inv = jnp.exp(-cum)
    g_tail = jnp.exp(cum_last - cum)
    a_t = -kk * jnp.exp(cum - logd)
    b_t = kk * a * g_inv
    k_t = k2 * g_inv
    r_t = r * g_in
    b_hat = kk * a * g_tail
    k_hat = k2 * g_tail

    st = st_s[...]
    p0 = _bdot(a_t, st)
    o0 = _bdot(r_t, st)

    row = lax.broadcasted_iota(jnp.int32, (c, c), 0)
    col = lax.broadcasted_iota(jnp.int32, (c, c), 1)
    strict = row > col
    incl = row >= col
    eye = (row == col).astype(F32)
    bk = jnp.concatenate([b_t, k_t], axis=0).astype(BF16)
    lane_head = lax.broadcasted_iota(jnp.int32, (1, RWKV_W), 1) // HEAD_DIM

    u_all = jnp.zeros((c, RWKV_W), F32)
    o_all = jnp.zeros((c, RWKV_W), F32)
    for h in range(n_heads):
        hm = lane_head == h
        ar = jnp.concatenate([jnp.where(hm, a_t, 0.0), jnp.where(hm, r_t, 0.0)], axis=0)
        m4 = _dot_nt(ar.astype(BF16), bk)
        l_ab = jnp.where(strict, m4[:c, :c], 0.0)
        l_ak = jnp.where(strict, m4[:c, c:], 0.0)
        m_rb = jnp.where(incl, m4[c:, :c], 0.0)
        m_rk = jnp.where(incl, m4[c:, c:], 0.0)
        t_inv = eye + l_ab
        pw = l_ab
        span = 2
        while span < c:
            pw = _hdot(pw, pw)
            t_inv = t_inv + _hdot(t_inv, pw)
            span *= 2
        u_h = _hdot(t_inv, p0 + _bdot(l_ak, v))
        u_all = jnp.where(hm, u_h, u_all)
        o_h = _bdot(m_rb, u_h) + _bdot(m_rk, v)
        o_all = jnp.where(hm, o_h, o_all)
    o = o0 + o_all

    upd = _dot_tn(b_hat.astype(BF16), u_all.astype(BF16)) + _dot_tn(k_hat.astype(BF16), v.astype(BF16))
    first_row = (lax.broadcasted_iota(jnp.int32, (c, RWKV_W), 0) == 0)
    g_col = _dot_tn(jnp.where(first_row, jnp.exp(cum_last), 0.0), jnp.ones((c, RWKV_W), F32), precision=HI)
    st_s[...] = g_col * st + hmask_ref[...] * upd

    inv_n = 1.0 / HEAD_DIM
    mean = _hdot(o, hsum) * inv_n
    dev = o - mean
    var = _hdot(dev * dev, hsum) * inv_n
    y = dev * lax.rsqrt(var + RWKV_GN_EPS) * gng_ref[...] + gnb_ref[...]
    y = y + _hdot(r * k2 * rk_ref[...], hsum) * v
    o_ref[0] = y * gate


def _rwkv(p3, mu, w0, w_up, a0, a_up, g_up, k_k, k_a, r_k, gn_g, gn_b):
    b, s, _ = p3.shape
    c = RWKV_CHUNK
    w = RWKV_W
    base = 3 * MOBA_W
    assert base % w == 0 and (base + 3 * w) % LANES == 0
    lora = DECAY_LORA + AAA_LORA

    def col_spec(width, off_cols):
        assert off_cols % width == 0
        return pl.BlockSpec((1, c, width), lambda bi, ci: (bi, ci, off_cols // width))

    row = lambda x: x.reshape(1, -1).astype(F32)
    const = lambda shape: pl.BlockSpec(shape, lambda bi, ci: (0,) * len(shape))
    wup_pad = jnp.concatenate([w_up, jnp.zeros((AAA_LORA, w), F32)], axis=0).astype(BF16)
    aup_pad = jnp.concatenate([jnp.zeros((DECAY_LORA, w), F32), a_up], axis=0).astype(BF16)
    head = np.arange(w) // HEAD_DIM
    hsum = jnp.asarray((head[:, None] == head[None, :]).astype(np.float32))
    tri = jnp.asarray(np.tril(np.ones((c, c), np.float32)))
    vec_args = [mu[:w], mu[w:2 * w], mu[2 * w:3 * w], mu[3 * w:3 * w + lora], mu[3 * w + lora:],
                w0, None, a0, None, None, k_k, k_a, r_k.reshape(-1), gn_g, gn_b]
    args = [p3, p3, p3, p3, p3]
    specs = [col_spec(w, base), col_spec(w, base + w), col_spec(w, base + 2 * w),
             col_spec(lora, base + 3 * w), col_spec(GATE_LORA, base + 3 * w + lora)]
    mats = {6: wup_pad, 8: aup_pad, 9: g_up.astype(BF16)}
    for idx, a in enumerate(vec_args):
        arr = mats[idx] if a is None else row(a)
        args.append(arr)
        specs.append(const(arr.shape))
    for arr in (hsum, hsum, tri):
        args.append(arr)
        specs.append(const(arr.shape))
    return pl.pallas_call(
        _rwkv_kernel,
        grid=(b, s // c),
        in_specs=specs,
        out_specs=pl.BlockSpec((1, c, w), lambda bi, ci: (bi, ci, 0)),
        out_shape=jax.ShapeDtypeStruct((b, s, w), F32),
        scratch_shapes=[pltpu.VMEM((w, w), F32), pltpu.VMEM((1, w), F32), pltpu.VMEM((1, w), F32),
                        pltpu.VMEM((1, w), F32), pltpu.VMEM((1, lora), F32),
                        pltpu.VMEM((1, GATE_LORA), F32)],
        compiler_params=_cparams(("parallel", "arbitrary")),
        name="rwkv",
    )(*args)


def _split_bf16(x, terms):
    parts = []
    for _ in range(terms):
        hi = x.astype(BF16)
        parts.append(hi)
        x = x - hi.astype(F32)
    return parts


def _dot_exact_rhs(x, m_bf16, terms):
    acc = None
    for part in _split_bf16(x, terms):
        d = jnp.dot(part, m_bf16, preferred_element_type=F32)
        acc = d if acc is None else acc + d
    return acc


def _dot_parts(a_parts, b_parts):
    dot = functools.partial(jnp.dot, preferred_element_type=F32)
    acc = dot(a_parts[0], b_parts[0])
    if len(a_parts) > 1:
        acc = acc + (dot(a_parts[0], b_parts[1]) + dot(a_parts[1], b_parts[0]))
    return acc


def _rwkv2_kernel(r_ref, k_ref, v_ref, wa_ref, g_ref,
                  mu_r_ref, mu_k_ref, mu_v_ref, mu_wa_ref, mu_g_ref,
                  w0_ref, wup_ref, a0_ref, aup_ref, gup_ref, kk_ref, ka_ref, rk_ref,
                  gng_ref, gnb_ref, hsum_ref, tri_ref,
                  o_ref, st_s, prev_r, prev_k, prev_v, prev_wa, prev_g, *, n_seq):
    c = RWKV_CHUNK
    mid = c // 2

    @pl.when(pl.program_id(1) == 0)
    def _():
        st_s[...] = jnp.zeros_like(st_s)
        prev_r[...] = jnp.zeros_like(prev_r)
        prev_k[...] = jnp.zeros_like(prev_k)
        prev_v[...] = jnp.zeros_like(prev_v)
        prev_wa[...] = jnp.zeros_like(prev_wa)
        prev_g[...] = jnp.zeros_like(prev_g)

    hsum = hsum_ref[...]
    tri = tri_ref[...]
    row = lax.broadcasted_iota(jnp.int32, (c, c), 0)
    col = lax.broadcasted_iota(jnp.int32, (c, c), 1)
    strict = row > col
    incl = row >= col
    eye = (row == col).astype(F32)
    lane_head = lax.broadcasted_iota(jnp.int32, (1, RWKV_W), 1) // HEAD_DIM
    hmask = hsum.astype(F32)

    def mix(g, x_ref, prev_ref, mu_ref):
        x = x_ref[g]
        rows = lax.broadcasted_iota(jnp.int32, x.shape, 0)
        shifted = jnp.where(rows == 0, prev_ref[g], pltpu.roll(x, 1, axis=0))
        prev_ref[g] = x[c - 1:c, :]
        return x + (shifted - x) * mu_ref[...]

    def prepare(g):
        r = mix(g, r_ref, prev_r, mu_r_ref)
        k = mix(g, k_ref, prev_k, mu_k_ref)
        v = mix(g, v_ref, prev_v, mu_v_ref)
        xwa = mix(g, wa_ref, prev_wa, mu_wa_ref)
        xg = mix(g, g_ref, prev_g, mu_g_ref)

        w = -_softplus(-(w0_ref[...] + _bdot(jnp.tanh(xwa), wup_ref[...]))) - 0.5
        logd = -jnp.exp(w)
        a = _sigmoid(a0_ref[...] + _bdot(xwa, aup_ref[...]))
        gate = _bdot(_sigmoid(xg), gup_ref[...])
        kk = k * kk_ref[...]
        kk = kk / jnp.maximum(jnp.sqrt(_dot_exact_rhs(kk * kk, hsum, 2)), 1e-12)
        k2 = k * (1.0 + (a - 1.0) * ka_ref[...])

        parts = _split_bf16(logd, 3)
        cum = (jnp.dot(tri, parts[0], preferred_element_type=F32)
               + jnp.dot(tri, parts[1], preferred_element_type=F32)
               + jnp.dot(tri, parts[2], preferred_element_type=F32))
        cum_last = cum[c - 1:c, :]
        ref = cum[mid - 1:mid, :]
        rel = cum - ref
        g_in = jnp.exp(rel)
        g_inv = jnp.exp(-rel)
        g_tail = jnp.exp(cum_last - cum)
        a_t = -kk * jnp.exp(rel - logd)
        b_t = kk * a * g_inv
        k_t = k2 * g_inv
        r_t = r * g_in
        b_hat = kk * a * g_tail
        k_hat = k2 * g_tail

        st = st_s[g]
        st_ref = (st * jnp.exp(ref)).astype(BF16)
        p0 = _dot_nt(a_t.astype(BF16), st_ref)
        o0 = _dot_nt(r_t.astype(BF16), st_ref)

        bk = jnp.concatenate([b_t, k_t], axis=0).astype(BF16)
        v_bf = v.astype(BF16)
        quads = []
        for h in range(RWKV_HEADS):
            hm = lane_head == h
            ar = jnp.concatenate([jnp.where(hm, a_t, 0.0), jnp.where(hm, r_t, 0.0)], axis=0)
            m4 = _dot_nt(ar.astype(BF16), bk)
            quads.append((jnp.where(strict, m4[:c, :c], 0.0), jnp.where(strict, m4[:c, c:], 0.0),
                          jnp.where(incl, m4[c:, :c], 0.0).astype(BF16),
                          jnp.where(incl, m4[c:, c:], 0.0).astype(BF16)))
        return dict(r=r, k2=k2, v=v, v_bf=v_bf, gate=gate, p0=p0, o0=o0, b_hat=b_hat, k_hat=k_hat,
                    st=st, decay=jnp.exp(cum_last), quads=quads)

    seqs = [prepare(g) for g in range(n_seq)]
    pairs = [(g, h) for g in range(n_seq) for h in range(RWKV_HEADS)]

    pw = {gh: seqs[gh[0]]["quads"][gh[1]][0] for gh in pairs}
    t_inv = {gh: eye + pw[gh] for gh in pairs}
    span = 2
    pw_parts = {gh: _split_bf16(pw[gh], RWKV_INV_TERMS) for gh in pairs}
    while span < c:
        pw = {gh: _dot_parts(pw_parts[gh], pw_parts[gh]) for gh in pairs}
        pw_parts = {gh: _split_bf16(pw[gh], RWKV_INV_TERMS) for gh in pairs}
        t_parts = {gh: _split_bf16(t_inv[gh], RWKV_INV_TERMS) for gh in pairs}
        t_inv = {gh: t_inv[gh] + _dot_parts(t_parts[gh], pw_parts[gh]) for gh in pairs}
        span *= 2

    w_loc = {(g, h): seqs[g]["p0"] + jnp.dot(seqs[g]["quads"][h][1].astype(BF16), seqs[g]["v_bf"],
                                             preferred_element_type=F32) for g, h in pairs}
    u_h = {gh: _dot_parts(_split_bf16(t_inv[gh], RWKV_INV_TERMS), _split_bf16(w_loc[gh], RWKV_INV_TERMS))
           for gh in pairs}
    o_h = {(g, h): (jnp.dot(seqs[g]["quads"][h][2], u_h[(g, h)].astype(BF16), preferred_element_type=F32)
                    + jnp.dot(seqs[g]["quads"][h][3], seqs[g]["v_bf"], preferred_element_type=F32))
           for g, h in pairs}

    for g in range(n_seq):
        sq = seqs[g]
        u_all = jnp.zeros((c, RWKV_W), F32)
        o_all = jnp.zeros((c, RWKV_W), F32)
        for h in range(RWKV_HEADS):
            hm = lane_head == h
            u_all = jnp.where(hm, u_h[(g, h)], u_all)
            o_all = jnp.where(hm, o_h[(g, h)], o_all)
        o = sq["o0"] + o_all
        upd = (_dot_tn(u_all.astype(BF16), sq["b_hat"].astype(BF16))
               + _dot_tn(sq["v_bf"], sq["k_hat"].astype(BF16)))
        st_s[g] = sq["st"] * sq["decay"] + hmask * upd

        inv_n = 1.0 / HEAD_DIM
        mean = _dot_exact_rhs(o, hsum, 2) * inv_n
        dev = o - mean
        var = _dot_exact_rhs(dev * dev, hsum, 2) * inv_n
        y = dev * lax.rsqrt(var + RWKV_GN_EPS) * gng_ref[...] + gnb_ref[...]
        y = y + _dot_exact_rhs(sq["r"] * sq["k2"] * rk_ref[...], hsum, 2) * sq["v"]
        o_ref[g] = y * sq["gate"]


def _rwkv2(p3, mu, w0, w_up, a0, a_up, g_up, k_k, k_a, r_k, gn_g, gn_b):
    b, s, _ = p3.shape
    c = RWKV_CHUNK
    w = RWKV_W
    n_seq = RWKV_SEQS_PER_STEP if b % RWKV_SEQS_PER_STEP == 0 else 1
    base = 3 * MOBA_W
    assert base % w == 0 and (base + 3 * w) % LANES == 0
    lora = DECAY_LORA + AAA_LORA

    def col_spec(width, off_cols):
        assert off_cols % width == 0
        return pl.BlockSpec((n_seq, c, width), lambda bi, ci: (bi, ci, off_cols // width))

    row = lambda x: x.reshape(1, -1).astype(F32)
    const = lambda shape: pl.BlockSpec(shape, lambda bi, ci: (0,) * len(shape))
    wup_pad = jnp.concatenate([w_up, jnp.zeros((AAA_LORA, w), F32)], axis=0).astype(BF16)
    aup_pad = jnp.concatenate([jnp.zeros((DECAY_LORA, w), F32), a_up], axis=0).astype(BF16)
    head = np.arange(w) // HEAD_DIM
    hsum = jnp.asarray((head[:, None] == head[None, :]).astype(np.float32)).astype(BF16)
    tri = jnp.asarray(np.tril(np.ones((c, c), np.float32))).astype(BF16)
    vec_args = [mu[:w], mu[w:2 * w], mu[2 * w:3 * w], mu[3 * w:3 * w + lora], mu[3 * w + lora:],
                w0, None, a0, None, None, k_k, k_a, r_k.reshape(-1), gn_g, gn_b]
    args = [p3, p3, p3, p3, p3]
    specs = [col_spec(w, base), col_spec(w, base + w), col_spec(w, base + 2 * w),
             col_spec(lora, base + 3 * w), col_spec(GATE_LORA, base + 3 * w + lora)]
    mats = {6: wup_pad, 8: aup_pad, 9: g_up.astype(BF16)}
    for idx, a in enumerate(vec_args):
        arr = mats[idx] if a is None else row(a)
        args.append(arr)
        specs.append(const(arr.shape))
    for arr in (hsum, tri):
        args.append(arr)
        specs.append(const(arr.shape))
    return pl.pallas_call(
        functools.partial(_rwkv2_kernel, n_seq=n_seq),
        grid=(b // n_seq, s // c),
        in_specs=specs,
        out_specs=pl.BlockSpec((n_seq, c, w), lambda bi, ci: (bi, ci, 0)),
        out_shape=jax.ShapeDtypeStruct((b, s, w), F32),
        scratch_shapes=[pltpu.VMEM((n_seq, w, w), F32), pltpu.VMEM((n_seq, 1, w), F32),
                        pltpu.VMEM((n_seq, 1, w), F32), pltpu.VMEM((n_seq, 1, w), F32),
                        pltpu.VMEM((n_seq, 1, lora), F32), pltpu.VMEM((n_seq, 1, GATE_LORA), F32)],
        compiler_params=_cparams(("parallel", "arbitrary")),
        name="rwkv",
    )(*args)


def _mem_attn_kernel(q_ref, kv_ref, o_ref):
    scale = HEAD_DIM ** -0.5
    lane_head = lax.broadcasted_iota(jnp.int32, (1, MEM_W), 1) // HEAD_DIM
    q = q_ref[0] * scale
    mk = kv_ref[0, :, :MEM_W].astype(BF16)
    mv = kv_ref[0, :, MEM_W:].astype(BF16)
    out = jnp.zeros(q.shape, F32)
    for h in range(MEM_HEADS):
        hm = lane_head == h
        s = _dot_nt(jnp.where(hm, q, 0.0).astype(BF16), mk)
        s = s - jnp.max(s, axis=-1, keepdims=True)
        e = jnp.exp(s)
        p = e / jnp.sum(e, axis=-1, keepdims=True)
        out = jnp.where(hm, jnp.dot(p.astype(BF16), mv, preferred_element_type=F32), out)
    o_ref[0] = out


def _mem_attention(p3, mkv3, tq):
    b, s, _ = p3.shape
    m = mkv3.shape[1]
    off = (3 * MOBA_W + RWKV_COLS) // MEM_W
    assert off * MEM_W == 3 * MOBA_W + RWKV_COLS
    return pl.pallas_call(
        _mem_attn_kernel,
        grid=(b, s // tq),
        in_specs=[pl.BlockSpec((1, tq, MEM_W), lambda bi, i: (bi, i, off)),
                  pl.BlockSpec((1, m, 2 * MEM_W), lambda bi, i: (bi, 0, 0))],
        out_specs=pl.BlockSpec((1, tq, MEM_W), lambda bi, i: (bi, i, 0)),
        out_shape=jax.ShapeDtypeStruct((b, s, MEM_W), F32),
        compiler_params=_cparams(("parallel", "parallel")),
        name="mem_attn",
    )(p3, mkv3)


def _out_router_kernel(x_ref, ym_ref, yr_ref, ye_ref, wo1_ref, wo2_ref, wo3_ref, g_ref,
                       wr_ref, br_ref, ltri_ref,
                       x1_ref, h_ref, idx_ref, wgt_ref, rank_ref, cnt_ref, run_s):
    @pl.when(pl.program_id(0) == 0)
    def _():
        run_s[...] = jnp.zeros_like(run_s)

    x1 = (x_ref[...] + _bdot(ym_ref[...], wo1_ref[...]) + _bdot(yr_ref[...], wo2_ref[...])
          + _bdot(ye_ref[...], wo3_ref[...]))
    x1_ref[...] = x1
    h = _rms(x1, g_ref[...])
    h_ref[...] = _pack_halves(h)
    logits = _hdot(h, wr_ref[...]) + br_ref[...]
    tm = logits.shape[0]
    lane = lax.broadcasted_iota(jnp.int32, (tm, LANES), 1)

    vals, idxs = [], []
    lg = logits
    for _ in range(TOP_K):
        m = jnp.max(lg, axis=1, keepdims=True)
        idx = jnp.min(jnp.where(lg == m, lane, LANES), axis=1, keepdims=True)
        vals.append(m)
        idxs.append(idx)
        lg = jnp.where(lane == idx, NEG_INF, lg)
    exps = [jnp.exp(vk - vals[0]) for vk in vals]
    denom = exps[0] + exps[1] + exps[2] + exps[3]

    chosen = jnp.zeros((tm, LANES), F32)
    for idx in idxs:
        chosen = chosen + jnp.where(lane == idx, 1.0, 0.0)
    before = jnp.dot(ltri_ref[...], chosen.astype(BF16), preferred_element_type=F32) + run_s[...]
    run_s[...] = run_s[...] + jnp.sum(chosen, axis=0, keepdims=True)
    cnt_ref[...] = run_s[...]

    idx_out = jnp.zeros((tm, LANES), jnp.int32)
    wgt_out = jnp.zeros((tm, LANES), F32)
    rank_out = jnp.zeros((tm, LANES), jnp.int32)
    for kk in range(TOP_K):
        rank = jnp.sum(jnp.where(lane == idxs[kk], before, 0.0), axis=1, keepdims=True)
        idx_out = jnp.where(lane == kk, idxs[kk], idx_out)
        wgt_out = jnp.where(lane == kk, exps[kk] / denom, wgt_out)
        rank_out = jnp.where(lane == kk, rank.astype(jnp.int32), rank_out)
    idx_ref[...] = idx_out
    wgt_ref[...] = wgt_out
    rank_ref[...] = rank_out


def _out_router(x2, ym, yr, ye, w_out, g_ffn, w_router, b_router, tm):
    t, d = x2.shape
    wo = w_out.astype(BF16)
    wo1, wo2, wo3 = wo[:MOBA_W], wo[MOBA_W:MOBA_W + RWKV_W], wo[MOBA_W + RWKV_W:]
    wr = jnp.zeros((d, LANES), F32).at[:, :N_EXPERTS].set(w_router)
    br = jnp.full((1, LANES), NEG_INF, F32).at[0, :N_EXPERTS].set(b_router)
    ltri = jnp.asarray(np.tril(np.ones((tm, tm), np.float32), -1)).astype(BF16)
    tile = lambda n: pl.BlockSpec((tm, n), lambda i: (i, 0))
    const = lambda a: pl.BlockSpec(a.shape, lambda i: (0,) * a.ndim)
    g2 = g_ffn.reshape(1, d)
    return pl.pallas_call(
        _out_router_kernel,
        grid=(t // tm,),
        in_specs=[tile(d), tile(MOBA_W), tile(RWKV_W), tile(MEM_W), const(wo1), const(wo2), const(wo3),
                  const(g2), const(wr), const(br), const(ltri)],
        out_specs=[tile(d), tile(d // 2), tile(LANES), tile(LANES), tile(LANES),
                   pl.BlockSpec((1, LANES), lambda i: (0, 0))],
        out_shape=[jax.ShapeDtypeStruct((t, d), F32), jax.ShapeDtypeStruct((t, d // 2), jnp.uint32),
                   jax.ShapeDtypeStruct((t, LANES), jnp.int32), jax.ShapeDtypeStruct((t, LANES), F32),
                   jax.ShapeDtypeStruct((t, LANES), jnp.int32), jax.ShapeDtypeStruct((1, LANES), F32)],
        scratch_shapes=[pltpu.VMEM((1, LANES), F32)],
        compiler_params=_cparams(("arbitrary",)),
        name="out_router",
    )(x2, ym, yr, ye, wo1, wo2, wo3, g2, wr, br, ltri)


def _out_router2_kernel(x_ref, ym_ref, yr_ref, ye_ref, wo1_ref, wo2_ref, wo3_ref, g_ref,
                        wrh_ref, wrl_ref, br_ref, upper_ref, ones_ref,
                        x1_ref, h_ref, idx_ref, rank_ref, wgt_ref, cnt_ref, run_s):
    @pl.when(pl.program_id(0) == 0)
    def _():
        run_s[...] = jnp.zeros_like(run_s)

    x1 = (x_ref[...] + _bdot(ym_ref[...], wo1_ref[...]) + _bdot(yr_ref[...], wo2_ref[...])
          + _bdot(ye_ref[...], wo3_ref[...]))
    x1_ref[...] = x1
    h = _rms(x1, g_ref[...])
    h_ref[...] = _pack_halves(h)
    tm = h.shape[0]
    h_hi, h_lo = _split_bf16(h, 2)
    dot = functools.partial(jnp.dot, preferred_element_type=F32)
    logits = dot(h_hi, wrh_ref[...]) + (dot(h_hi, wrl_ref[...]) + dot(h_lo, wrh_ref[...])) + br_ref[...]
    lg = logits.T[:N_EXPERTS, :]
    e_iota = lax.broadcasted_iota(jnp.int32, (N_EXPERTS, tm), 0)

    vals, idxs = [], []
    for _ in range(TOP_K):
        m = jnp.max(lg, axis=0, keepdims=True)
        idx = jnp.min(jnp.where(lg == m, e_iota, N_EXPERTS), axis=0, keepdims=True)
        vals.append(m)
        idxs.append(idx)
        lg = jnp.where(e_iota == idx, NEG_INF, lg)
    exps = [jnp.exp(vk - vals[0]) for vk in vals]
    denom = exps[0] + exps[1] + exps[2] + exps[3]

    chosen = jnp.zeros((N_EXPERTS, tm), F32)
    for idx in idxs:
        chosen = chosen + jnp.where(e_iota == idx, 1.0, 0.0)
    chosen = chosen.astype(BF16)
    run = run_s[...]
    before = dot(chosen, upper_ref[...]) + jnp.concatenate([run] * (tm // LANES), axis=1)
    run_s[...] = run + dot(chosen, ones_ref[...])
    cnt_ref[...] = run_s[...]

    zero_i = jnp.zeros((8 - TOP_K, tm), jnp.int32)
    ranks = [jnp.sum(jnp.where(e_iota == idx, before, 0.0), axis=0, keepdims=True).astype(jnp.int32)
             for idx in idxs]
    idx_ref[0] = jnp.concatenate(idxs + [zero_i], axis=0)
    rank_ref[0] = jnp.concatenate(ranks + [zero_i], axis=0)
    wrows = jnp.concatenate([e / denom for e in exps] + [jnp.zeros((LANES - TOP_K, tm), F32)], axis=0)
    wgt_ref[...] = wrows.T


def _out_router2(x2, ym, yr, ye, w_out, g_ffn, w_router, b_router, tm):
    t, d = x2.shape
    wo = w_out.astype(BF16)
    wo1, wo2, wo3 = wo[:MOBA_W], wo[MOBA_W:MOBA_W + RWKV_W], wo[MOBA_W + RWKV_W:]
    wr = jnp.zeros((d, LANES), F32).at[:, :N_EXPERTS].set(w_router)
    wr_hi = wr.astype(BF16)
    wr_lo = (wr - wr_hi.astype(F32)).astype(BF16)
    br = jnp.full((1, LANES), NEG_INF, F32).at[0, :N_EXPERTS].set(b_router)
    upper = jnp.asarray(np.triu(np.ones((tm, tm), np.float32), 1)).astype(BF16)
    ones = jnp.ones((tm, LANES), BF16)
    tile = lambda n: pl.BlockSpec((tm, n), lambda i: (i, 0))
    slots = pl.BlockSpec((1, 8, tm), lambda i: (i, 0, 0))
    const = lambda a: pl.BlockSpec(a.shape, lambda i: (0,) * a.ndim)
    g2 = g_ffn.reshape(1, d)
    n = t // tm
    return pl.pallas_call(
        _out_router2_kernel,
        grid=(n,),
        in_specs=[tile(d), tile(MOBA_W), tile(RWKV_W), tile(MEM_W), const(wo1), const(wo2), const(wo3),
                  const(g2), const(wr_hi), const(wr_lo), const(br), const(upper), const(ones)],
        out_specs=[tile(d), tile(d // 2), slots, slots, tile(LANES),
                   pl.BlockSpec((N_EXPERTS, LANES), lambda i: (0, 0))],
        out_shape=[jax.ShapeDtypeStruct((t, d), F32), jax.ShapeDtypeStruct((t, d // 2), jnp.uint32),
                   jax.ShapeDtypeStruct((n, 8, tm), jnp.int32), jax.ShapeDtypeStruct((n, 8, tm), jnp.int32),
                   jax.ShapeDtypeStruct((t, LANES), F32), jax.ShapeDtypeStruct((N_EXPERTS, LANES), F32)],
        scratch_shapes=[pltpu.VMEM((N_EXPERTS, LANES), F32)],
        compiler_params=_cparams(("arbitrary",)),
        name="out_router",
    )(x2, ym, yr, ye, wo1, wo2, wo3, g2, wr_hi, wr_lo, br, upper, ones)


def _experts_kernel(be_ref, nused_ref, xs_ref, wgu_ref, bg_ref, bu_ref, wd_ref, bd_ref, perm_ref,
                    o_ref, wg_s, wu_s, wd_s):
    i = pl.program_id(0)
    prev = be_ref[jnp.maximum(i - 1, 0)]
    changed = (i == 0) | (be_ref[i] != prev)

    @pl.when(changed & (i < nused_ref[0]))
    def _():
        half = LANES
        for cblk in range(2 * D_EXPERT // (2 * half)):
            wt = wgu_ref[0, :, cblk * 2 * half:(cblk + 1) * 2 * half].astype(BF16)
            sep = jnp.dot(wt, perm_ref[...], preferred_element_type=F32).astype(BF16)
            wg_s[:, cblk * half:(cblk + 1) * half] = sep[:, :half]
            wu_s[:, cblk * half:(cblk + 1) * half] = sep[:, half:]
        wd_s[...] = wd_ref[0].astype(BF16)

    @pl.when(i < nused_ref[0])
    def _():
        xb = _unpack_halves(xs_ref[...]).astype(BF16)
        gate = jnp.dot(xb, wg_s[...], preferred_element_type=F32) + bg_ref[0]
        up = jnp.dot(xb, wu_s[...], preferred_element_type=F32) + bu_ref[0]
        gate = jnp.minimum(gate, SWIGLU_LIMIT)
        up = jnp.clip(up, -SWIGLU_LIMIT, SWIGLU_LIMIT)
        glu = gate * _sigmoid(gate * SWIGLU_ALPHA)
        act = ((up + 1.0) * glu).astype(BF16)
        o_ref[...] = _pack_halves(jnp.dot(act, wd_s[...], preferred_element_type=F32) + bd_ref[0])

    @pl.when(i >= nused_ref[0])
    def _():
        o_ref[...] = jnp.zeros_like(o_ref)


def _experts(blk_expert, n_used, xs, w_gate_up, b_gate_up, w_down, b_down):
    p_rows = xs.shape[0]
    d = 2 * xs.shape[1]
    n_blocks = p_rows // EXPERT_ROWS
    bg = b_gate_up[:, 0::2].reshape(N_EXPERTS, 1, D_EXPERT)
    bu = b_gate_up[:, 1::2].reshape(N_EXPERTS, 1, D_EXPERT)
    bd = b_down.reshape(N_EXPERTS, 1, d)
    perm_np = np.zeros((2 * LANES, 2 * LANES), np.float32)
    perm_np[2 * np.arange(LANES), np.arange(LANES)] = 1.0
    perm_np[2 * np.arange(LANES) + 1, LANES + np.arange(LANES)] = 1.0
    perm = jnp.asarray(perm_np).astype(BF16)
    by_expert = lambda shape: pl.BlockSpec((1,) + shape, lambda i, be, nu: (be[i], 0, 0))
    grid_spec = pltpu.PrefetchScalarGridSpec(
        num_scalar_prefetch=2,
        grid=(n_blocks,),
        in_specs=[pl.BlockSpec((EXPERT_ROWS, d // 2), lambda i, be, nu: (i, 0)),
                  by_expert((d, 2 * D_EXPERT)), by_expert((1, D_EXPERT)), by_expert((1, D_EXPERT)),
                  by_expert((D_EXPERT, d)), by_expert((1, d)),
                  pl.BlockSpec(perm.shape, lambda i, be, nu: (0, 0))],
        out_specs=pl.BlockSpec((EXPERT_ROWS, d // 2), lambda i, be, nu: (i, 0)),
        scratch_shapes=[pltpu.VMEM((d, D_EXPERT), BF16), pltpu.VMEM((d, D_EXPERT), BF16),
                        pltpu.VMEM((D_EXPERT, d), BF16)],
    )
    return pl.pallas_call(
        _experts_kernel,
        grid_spec=grid_spec,
        out_shape=jax.ShapeDtypeStruct((p_rows, d // 2), jnp.uint32),
        compiler_params=_cparams(("arbitrary",)),
        name="experts",
    )(blk_expert, n_used, xs, w_gate_up, bg, bu, w_down, bd, perm)


def _row_copies(e_ref, rank_ref, start_ref, n_tok, make_copy):
    def body(r, carry):
        for kk in range(TOP_K):
            j = kk * n_tok + r
            row = start_ref[e_ref[0, 0, j]] + rank_ref[0, 0, j]
            make_copy(r, kk, row).start()
        return carry
    lax.fori_loop(0, n_tok, body, 0)


def _dispatch_kernel(e_ref, rank_ref, start_ref, h_ref, xs_init_ref, xs_ref, sem):
    del xs_init_ref
    tb = h_ref.shape[0]
    _row_copies(e_ref, rank_ref, start_ref, tb,
                lambda r, kk, row: pltpu.make_async_copy(h_ref.at[pl.ds(r, 1)], xs_ref.at[pl.ds(row, 1)], sem))
    for _ in range(TOP_K):
        pltpu.make_async_copy(h_ref, xs_ref.at[pl.ds(0, tb)], sem).wait()


def _dispatch(e_flat, rank_flat, pad_starts, hp, p_rows, tb):
    t, half = hp.shape
    idx_spec = pl.BlockSpec((1, 1, tb * TOP_K), lambda i: (i, 0, 0), memory_space=pltpu.SMEM)
    xs_init = jnp.zeros((p_rows, half), jnp.uint32)
    return pl.pallas_call(
        _dispatch_kernel,
        grid=(t // tb,),
        in_specs=[idx_spec, idx_spec, pl.BlockSpec(memory_space=pltpu.SMEM),
                  pl.BlockSpec((tb, half), lambda i: (i, 0)), pl.BlockSpec(memory_space=pl.ANY)],
        out_specs=pl.BlockSpec(memory_space=pl.ANY),
        out_shape=jax.ShapeDtypeStruct((p_rows, half), jnp.uint32),
        scratch_shapes=[pltpu.SemaphoreType.DMA(())],
        input_output_aliases={4: 0},
        compiler_params=_cparams(("arbitrary",)),
        name="dispatch",
    )(e_flat, rank_flat, pad_starts, hp, xs_init)


def _sc_gather_rows(table, idx):
    n_rows = idx.shape[0]
    width = table.shape[1]
    n_workers = SC_CORES * SC_SUBCORES
    per_worker = n_rows // n_workers
    chunk = SC_GATHER_ROWS
    assert per_worker * n_workers == n_rows and per_worker % chunk == 0 and chunk % 8 == 0
    mesh = plsc.VectorSubcoreMesh(core_axis_name="c", subcore_axis_name="s",
                                  num_cores=SC_CORES, num_subcores=SC_SUBCORES)

    @functools.partial(
        pl.kernel, mesh=mesh,
        out_type=jax.ShapeDtypeStruct((n_rows, width), table.dtype),
        scratch_types=[pltpu.VMEM((chunk,), jnp.int32), pltpu.VMEM((chunk, width), table.dtype),
                       pltpu.SemaphoreType.DMA],
        name="sc_gather",
    )
    def gather(table_hbm, idx_hbm, out_hbm, idx_v, rows_v, sem):
        worker = lax.axis_index("s") * SC_CORES + lax.axis_index("c")
        base = worker * per_worker

        @pl.loop(0, per_worker // chunk)
        def _(step):
            off = pl.multiple_of(base + step * chunk, 8)
            pltpu.sync_copy(idx_hbm.at[pl.ds(off, chunk)], idx_v)
            pltpu.async_copy(table_hbm.at[idx_v], rows_v, sem).wait()
            pltpu.sync_copy(rows_v, out_hbm.at[pl.ds(off, chunk)])

    return gather(table, idx)


def _reduce_kernel(yg_ref, x1_ref, w_ref, g_ref, o_ref):
    acc = x1_ref[...]
    wts = w_ref[...]
    for kk in range(TOP_K):
        acc = acc + _unpack_halves(yg_ref[kk]) * wts[:, kk:kk + 1]
    o_ref[...] = _rms(acc, g_ref[...])


def _reduce(yg, x1, wts, g_final, tb):
    t, d = x1.shape
    half = yg.shape[2]
    return pl.pallas_call(
        _reduce_kernel,
        grid=(t // tb,),
        in_specs=[pl.BlockSpec((TOP_K, tb, half), lambda i: (0, i, 0)),
                  pl.BlockSpec((tb, d), lambda i: (i, 0)),
                  pl.BlockSpec((tb, LANES), lambda i: (i, 0)),
                  pl.BlockSpec((1, d), lambda i: (0, 0))],
        out_specs=pl.BlockSpec((tb, d), lambda i: (i, 0)),
        out_shape=jax.ShapeDtypeStruct((t, d), F32),
        compiler_params=_cparams(("parallel",)),
        name="combine",
    )(yg, x1, wts, g_final.reshape(1, d))


def _combine_kernel(e_ref, rank_ref, e_next_ref, rank_next_ref, start_ref, ys_ref, x1_ref, w_ref, g_ref,
                    o_ref, buf, sem):
    i = pl.program_id(0)
    n = pl.num_programs(0)
    tb = x1_ref.shape[0]

    def gather(e, rank, slot):
        _row_copies(e, rank, start_ref, tb,
                    lambda r, kk, row: pltpu.make_async_copy(
                        ys_ref.at[pl.ds(row, 1)], buf.at[slot, kk, pl.ds(r, 1)], sem.at[slot]))

    @pl.when(i == 0)
    def _():
        gather(e_ref, rank_ref, 0)

    @pl.when(i + 1 < n)
    def _():
        gather(e_next_ref, rank_next_ref, (i + 1) % 2)

    slot = i % 2
    for kk in range(TOP_K):
        pltpu.make_async_copy(ys_ref.at[pl.ds(0, tb)], buf.at[slot, kk], sem.at[slot]).wait()
    acc = x1_ref[...]
    wts = w_ref[...]
    for kk in range(TOP_K):
        acc = acc + _unpack_halves(buf[slot, kk]) * wts[:, kk:kk + 1]
    o_ref[...] = _rms(acc, g_ref[...])


def _combine(e_flat, rank_flat, pad_starts, ys, x1, wts, g_final, tb):
    t, d = x1.shape
    n = t // tb
    half = ys.shape[1]
    idx_spec = pl.BlockSpec((1, 1, tb * TOP_K), lambda i: (i, 0, 0), memory_space=pltpu.SMEM)
    next_spec = pl.BlockSpec((1, 1, tb * TOP_K), lambda i: (jnp.minimum(i + 1, n - 1), 0, 0),
                             memory_space=pltpu.SMEM)
    return pl.pallas_call(
        _combine_kernel,
        grid=(n,),
        in_specs=[idx_spec, idx_spec, next_spec, next_spec, pl.BlockSpec(memory_space=pltpu.SMEM),
                  pl.BlockSpec(memory_space=pl.ANY),
                  pl.BlockSpec((tb, d), lambda i: (i, 0)),
                  pl.BlockSpec((tb, LANES), lambda i: (i, 0)),
                  pl.BlockSpec((1, d), lambda i: (0, 0))],
        out_specs=pl.BlockSpec((tb, d), lambda i: (i, 0)),
        out_shape=jax.ShapeDtypeStruct((t, d), F32),
        scratch_shapes=[pltpu.VMEM((2, TOP_K, tb, half), jnp.uint32), pltpu.SemaphoreType.DMA((2,))],
        compiler_params=_cparams(("arbitrary",)),
        name="combine",
    )(e_flat, rank_flat, e_flat, rank_flat, pad_starts, ys, x1, wts, g_final.reshape(1, d))


def _layer(x, mem, w_in, w_out, w_mem_kv, g_mix, g_mem, g_ffn, bias_tables, mu, w0, w_up, a0, a_up,
           g_up, k_k, k_a, r_k, gn_g, gn_b, w_router, b_router, w_gate_up, b_gate_up, w_down, b_down,
           g_last):
    b, s, d = x.shape
    m = mem.shape[1]
    t = b * s
    x2 = x.reshape(t, d)

    p = _norm_matmul(x2, g_mix, w_in.astype(BF16), 512, "in_proj")
    p3 = p.reshape(b, s, IN_COLS)
    y_moba = _moba2_attention(p3, *bias_tables)
    y_rwkv = _rwkv2(p3, mu, w0, w_up, a0, a_up, g_up, k_k, k_a, r_k, gn_g, gn_b)
    mkv = _norm_matmul(mem.reshape(b * m, d), g_mem, w_mem_kv.astype(BF16), 512, "mem_kv")
    y_mem = _mem_attention(p3, mkv.reshape(b, m, 2 * MEM_W), 512)

    tb = ROUTE_TOKENS
    x1, h2, idx_o, rank_o, wgt_p, cnt = _out_router2(
        x2, y_moba.reshape(t, MOBA_W), y_rwkv.reshape(t, RWKV_W), y_mem.reshape(t, MEM_W),
        w_out, g_ffn, w_router, b_router, tb)

    e_flat = idx_o[:, :TOP_K, :].reshape(t // tb, 1, tb * TOP_K)
    rank_flat = rank_o[:, :TOP_K, :].reshape(t // tb, 1, tb * TOP_K)
    counts = cnt[:, 0].astype(jnp.int32)
    padded = (counts + EXPERT_ROWS - 1) // EXPERT_ROWS * EXPERT_ROWS
    pad_ends = jnp.cumsum(padded)
    pad_starts = (pad_ends - padded).astype(jnp.int32)
    n_blocks = (t * TOP_K) // EXPERT_ROWS + N_EXPERTS
    blk_start = jnp.arange(n_blocks, dtype=jnp.int32) * EXPERT_ROWS
    blk_expert = jnp.minimum(jnp.sum(blk_start[:, None] >= pad_ends[None, :], axis=1),
                             N_EXPERTS - 1).astype(jnp.int32)
    n_used = (pad_ends[-1:] // EXPERT_ROWS).astype(jnp.int32)

    xs = _dispatch(e_flat, rank_flat, pad_starts, h2, n_blocks * EXPERT_ROWS, tb)
    ys = _experts(blk_expert, n_used, xs, w_gate_up, b_gate_up, w_down, b_down)

    idx_kt = jnp.swapaxes(idx_o[:, :TOP_K, :], 0, 1).reshape(TOP_K, t)
    rank_kt = jnp.swapaxes(rank_o[:, :TOP_K, :], 0, 1).reshape(TOP_K, t)
    start_kt = jnp.zeros((TOP_K, t), jnp.int32)
    for e in range(N_EXPERTS):
        start_kt = jnp.where(idx_kt == e, pad_starts[e], start_kt)
    dest = (start_kt + rank_kt).reshape(TOP_K * t)
    yg = _sc_gather_rows(lax.bitcast_convert_type(ys, jnp.int32), dest)
    yg = lax.bitcast_convert_type(yg, jnp.uint32).reshape(TOP_K, t, d // 2)
    out = _reduce(yg, x1, wgt_p, g_last, tb)
    return out.reshape(b, s, d)


def kernel(x, mem, w_in, w_out, w_mem_kv, g_mix, g_mem, g_ffn, g_final, rel_bias, rwkv_mu, rwkv_w0,
           rwkv_w_up, rwkv_a0, rwkv_a_up, rwkv_g_up, rwkv_k_k, rwkv_k_a, rwkv_r_k, rwkv_gn_g, rwkv_gn_b,
           w_router, b_router, w_gate_up, b_gate_up, w_down, b_down):
    depth = w_in.shape[0]
    assert depth == 1, "the final norm is fused into the last layer's combine kernel"
    bias_tables = _moba_bias_tables(rel_bias)
    l = 0
    return _layer(x, mem, w_in[l], w_out[l], w_mem_kv[l], g_mix[l], g_mem[l], g_ffn[l], bias_tables,
                  rwkv_mu[l], rwkv_w0[l], rwkv_w_up[l], rwkv_a0[l], rwkv_a_up[l], rwkv_g_up[l],
                  rwkv_k_k[l], rwkv_k_a[l], rwkv_r_k[l], rwkv_gn_g[l], rwkv_gn_b[l], w_router[l],
                  b_router[l], w_gate_up[l], b_gate_up[l], w_down[l], b_down[l], g_final)
```

```python
import functools
import math

import numpy as np
import jax
import jax.numpy as jnp
from jax import lax
from jax.experimental import pallas as pl
from jax.experimental.pallas import tpu as pltpu
from jax.experimental.pallas import tpu_sc as plsc

F32 = jnp.float32
BF16 = jnp.bfloat16
I32 = jnp.int32
HI = lax.Precision.HIGHEST

D_MODEL = 1024
HEAD_DIM = 64
MOBA_HEADS = 6
RWKV_HEADS = 6
MEM_HEADS = 4
MOBA_W = MOBA_HEADS * HEAD_DIM
RWKV_W = RWKV_HEADS * HEAD_DIM
MEM_W = MEM_HEADS * HEAD_DIM
MOBA_BLOCK = 256
MOBA_TOPK = 3
N_BUCKETS = 32
MAX_DISTANCE = 128
DECAY_LORA = 64
AAA_LORA = 64
GATE_LORA = 128
RWKV_COLS = 3 * RWKV_W + DECAY_LORA + AAA_LORA + GATE_LORA
RWKV_GN_EPS = 64e-5
IN_COLS = 3 * MOBA_W + RWKV_COLS + MEM_W
N_EXPERTS = 32
TOP_K = 4
D_EXPERT = D_MODEL
SWIGLU_ALPHA = 1.702
SWIGLU_LIMIT = 7.0
RMS_EPS = 1e-5

LANES = 128
RWKV_CHUNK = 128
RWKV_SEQS_PER_STEP = 2
EXPERT_ROWS = 256
ROUTE_TOKENS = 256
SC_CORES = 2
SC_SUBCORES = 16
SC_ROWS = 64
VMEM_LIMIT = 56 * 1024 * 1024
NEG_INF = float("-inf")
HIGH_HALF = -65536


def _cparams(sem):
    return pltpu.CompilerParams(dimension_semantics=sem, vmem_limit_bytes=VMEM_LIMIT)


def _rms(x, g):
    return x * lax.rsqrt(jnp.mean(x * x, axis=-1, keepdims=True) + RMS_EPS) * g


def _bdot(a, b):
    return jnp.dot(a.astype(BF16), b.astype(BF16), preferred_element_type=F32)


def _hdot(a, b):
    return jnp.dot(a, b, preferred_element_type=F32, precision=HI)


def _dot_nt(a, b):
    return lax.dot_general(a, b, (((1,), (1,)), ((), ())), preferred_element_type=F32)


def _dot_tn(a, b):
    return lax.dot_general(a, b, (((0,), (0,)), ((), ())), preferred_element_type=F32)


def _split_bf16(x, terms):
    parts = []
    for _ in range(terms):
        hi = x.astype(BF16)
        parts.append(hi)
        x = x - hi.astype(F32)
    return parts


def _dot_exact_rhs(x, m_bf16, terms):
    acc = None
    for part in _split_bf16(x, terms):
        d = jnp.dot(part, m_bf16, preferred_element_type=F32)
        acc = d if acc is None else acc + d
    return acc


def _pack_halves(x):
    n = x.shape[1] // 2
    lo = pltpu.bitcast(x[:, :n].astype(BF16).astype(F32), I32)
    hi = pltpu.bitcast(x[:, n:].astype(BF16).astype(F32), I32)
    return (hi & HIGH_HALF) | lax.shift_right_logical(lo, 16)


def _unpack_halves(w):
    lo = pltpu.bitcast(w << 16, F32)
    hi = pltpu.bitcast(w & HIGH_HALF, F32)
    return jnp.concatenate([lo, hi], axis=1)


def _norm_matmul_kernel(x_ref, g_ref, w_ref, o_ref):
    h = _rms(x_ref[...], g_ref[...])
    o_ref[...] = jnp.dot(h.astype(BF16), w_ref[...], preferred_element_type=F32)


def _norm_matmul(x, g, w_bf16, tm, name):
    t, d = x.shape
    n = w_bf16.shape[1]
    tm = min(tm, t)
    return pl.pallas_call(
        _norm_matmul_kernel,
        grid=(t // tm,),
        in_specs=[pl.BlockSpec((tm, d), lambda i: (i, 0)),
                  pl.BlockSpec((1, d), lambda i: (0, 0)),
                  pl.BlockSpec((d, n), lambda i: (0, 0))],
        out_specs=pl.BlockSpec((tm, n), lambda i: (i, 0)),
        out_shape=jax.ShapeDtypeStruct((t, n), F32),
        compiler_params=_cparams(("parallel",)),
        name=name,
    )(x, g.reshape(1, d), w_bf16)


def _t5_bucket_np(dist):
    n = np.maximum(dist, 0)
    max_exact = N_BUCKETS // 2
    nf = np.maximum(n, 1).astype(np.float64)
    large = max_exact + (np.log(nf / max_exact) / math.log(MAX_DISTANCE / max_exact)
                         * (N_BUCKETS - max_exact)).astype(np.int64)
    large = np.minimum(large, N_BUCKETS - 1)
    return np.where(n < max_exact, n, large)


def _moba_bias_tables(rel_bias):
    kk = np.arange(MOBA_BLOCK)[:, None]
    qq = np.arange(MOBA_BLOCK)[None, :]
    d_own = qq - kk
    d_prev = MOBA_BLOCK + qq - kk
    assert np.all(_t5_bucket_np(np.arange(MOBA_BLOCK + 1, 64 * MOBA_BLOCK)) == N_BUCKETS - 1)
    bias_t = rel_bias.astype(F32).T

    def lookup(bucket):
        bucket = jnp.asarray(bucket, I32)[None]
        tab = jnp.zeros((MOBA_HEADS,) + bucket.shape[1:], F32)
        for b in range(N_BUCKETS):
            tab = jnp.where(bucket == b, bias_t[:, b][:, None, None], tab)
        return tab

    t_own = jnp.where(jnp.asarray(d_own >= 0)[None], lookup(_t5_bucket_np(d_own)), NEG_INF)
    t_prev = lookup(_t5_bucket_np(d_prev))
    far = bias_t[:, N_BUCKETS - 1]
    return t_own, t_prev, far


def _moba_kernel(far_ref, q_ref, k_ref, v_ref, t0_ref, t1_ref, o_ref,
                 qt_s, kb_s, vt_s, sc_s, *, nb):
    hp = pl.program_id(1)
    blk = MOBA_BLOCK
    scale = HEAD_DIM ** -0.5
    row_head = lax.broadcasted_iota(I32, (LANES, 1), 0) // HEAD_DIM
    lane_head = lax.broadcasted_iota(I32, (1, LANES), 1) // HEAD_DIM

    kmean_rows = []
    for j in range(nb):
        sl = pl.ds(j * blk, blk)
        qt_s[j] = (q_ref[0, sl, :] * scale).T
        vt_s[j] = v_ref[0, sl, :].T.astype(BF16)
        kj = k_ref[0, sl, :]
        kb_s[j] = kj.astype(BF16)
        kmean_rows.append(jnp.mean(kj, axis=0, keepdims=True))
    kmean = jnp.concatenate(kmean_rows, axis=0)
    km2 = jnp.concatenate([jnp.where(lane_head == 0, kmean, 0.0),
                           jnp.where(lane_head == 1, kmean, 0.0)], axis=0)
    blk_iota = lax.broadcasted_iota(I32, (nb, blk), 0)

    for i in range(nb):
        qt = qt_s[i]
        negs = [None, None]
        if i > MOBA_TOPK:
            gate2 = _hdot(km2, qt)
            for h in range(2):
                g = gate2[h * nb:(h + 1) * nb, :]
                cnt = jnp.zeros((nb, blk), I32)
                for m in range(i):
                    gm = g[m:m + 1, :]
                    beats = (gm > g) | ((gm == g) & (m < blk_iota))
                    cnt = cnt + jnp.where(beats, 1, 0)
                negs[h] = jnp.where(cnt < MOBA_TOPK, 0.0, NEG_INF)
        outs = []
        for h in range(2):
            qt_h = jnp.where(row_head == h, qt, 0.0).astype(BF16)
            far_bias = far_ref[2 * hp + h]
            m_run = None
            for j in range(i + 1):
                s = jnp.dot(kb_s[j], qt_h, preferred_element_type=F32)
                if j == i:
                    s = s + t0_ref[h]
                else:
                    bias = t1_ref[h] if j == i - 1 else far_bias
                    if negs[h] is not None:
                        bias = bias + negs[h][j:j + 1, :]
                    s = s + bias
                sc_s[h, j] = s
                cm = jnp.max(s, axis=0, keepdims=True)
                m_run = cm if m_run is None else jnp.maximum(m_run, cm)
            l_run = jnp.zeros((1, blk), F32)
            acc = jnp.zeros((HEAD_DIM, blk), F32)
            for j in range(i + 1):
                p_t = jnp.exp(sc_s[h, j] - m_run)
                l_run = l_run + jnp.sum(p_t, axis=0, keepdims=True)
                vt_h = vt_s[j, h * HEAD_DIM:(h + 1) * HEAD_DIM, :]
                acc = acc + jnp.dot(vt_h, p_t.astype(BF16), preferred_element_type=F32)
            outs.append(acc / l_run)
        o_t = jnp.concatenate(outs, axis=0)
        o_ref[0, pl.ds(i * blk, blk), :] = o_t.T


def _moba_attention(p3, t_own, t_prev, far):
    b, s, _ = p3.shape
    nb = s // MOBA_BLOCK
    n_pairs = MOBA_HEADS // 2
    blk_spec = lambda off: pl.BlockSpec((1, s, LANES), lambda bi, hp: (bi, 0, off + hp))
    tab_spec = pl.BlockSpec((2, MOBA_BLOCK, MOBA_BLOCK), lambda bi, hp: (hp, 0, 0))
    return pl.pallas_call(
        functools.partial(_moba_kernel, nb=nb),
        grid=(b, n_pairs),
        in_specs=[pl.BlockSpec(memory_space=pltpu.SMEM),
                  blk_spec(0), blk_spec(n_pairs), blk_spec(2 * n_pairs), tab_spec, tab_spec],
        out_specs=pl.BlockSpec((1, s, LANES), lambda bi, hp: (bi, 0, hp)),
        out_shape=jax.ShapeDtypeStruct((b, s, MOBA_W), F32),
        scratch_shapes=[pltpu.VMEM((nb, LANES, MOBA_BLOCK), F32),
                        pltpu.VMEM((nb, MOBA_BLOCK, LANES), BF16),
                        pltpu.VMEM((nb, LANES, MOBA_BLOCK), BF16),
                        pltpu.VMEM((2, nb, MOBA_BLOCK, MOBA_BLOCK), F32)],
        compiler_params=_cparams(("parallel", "parallel")),
        name="moba",
    )(far, p3, p3, p3, t_own, t_prev)


def _softplus(z):
    return jnp.maximum(z, 0.0) + jnp.log(1.0 + jnp.exp(-jnp.abs(z)))


def _sigmoid(z):
    return 1.0 / (1.0 + jnp.exp(-z))


def _rwkv_kernel(r_ref, k_ref, v_ref, wa_ref, g_ref,
                 mu_r_ref, mu_k_ref, mu_v_ref, mu_wa_ref, mu_g_ref,
                 w0_ref, wup_ref, a0_ref, aup_ref, gup_ref, kk_ref, ka_ref, rk_ref,
                 gng_ref, gnb_ref, hsum_ref, tri_ref,
                 o_ref, st_s, prev_r, prev_k, prev_v, prev_wa, prev_g, *, n_seq):
    c = RWKV_CHUNK
    mid = c // 2

    @pl.when(pl.program_id(1) == 0)
    def _():
        st_s[...] = jnp.zeros_like(st_s)
        prev_r[...] = jnp.zeros_like(prev_r)
        prev_k[...] = jnp.zeros_like(prev_k)
        prev_v[...] = jnp.zeros_like(prev_v)
        prev_wa[...] = jnp.zeros_like(prev_wa)
        prev_g[...] = jnp.zeros_like(prev_g)

    hsum = hsum_ref[...]
    tri = tri_ref[...]
    row = lax.broadcasted_iota(I32, (c, c), 0)
    col = lax.broadcasted_iota(I32, (c, c), 1)
    strict = row > col
    incl = row >= col
    eye = (row == col).astype(F32)
    lane_head = lax.broadcasted_iota(I32, (1, RWKV_W), 1) // HEAD_DIM
    hmask = hsum.astype(F32)
    dot = functools.partial(jnp.dot, preferred_element_type=F32)

    def mix(g, x_ref, prev_ref, mu_ref):
        x = x_ref[g]
        rows = lax.broadcasted_iota(I32, x.shape, 0)
        shifted = jnp.where(rows == 0, prev_ref[g], pltpu.roll(x, 1, axis=0))
        prev_ref[g] = x[c - 1:c, :]
        return x + (shifted - x) * mu_ref[...]

    def prepare(g):
        r = mix(g, r_ref, prev_r, mu_r_ref)
        k = mix(g, k_ref, prev_k, mu_k_ref)
        v = mix(g, v_ref, prev_v, mu_v_ref)
        xwa = mix(g, wa_ref, prev_wa, mu_wa_ref)
        xg = mix(g, g_ref, prev_g, mu_g_ref)

        w = -_softplus(-(w0_ref[...] + _bdot(jnp.tanh(xwa), wup_ref[...]))) - 0.5
        logd = -jnp.exp(w)
        a = _sigmoid(a0_ref[...] + _bdot(xwa, aup_ref[...]))
        gate = _bdot(_sigmoid(xg), gup_ref[...])
        kk = k * kk_ref[...]
        kk = kk / jnp.maximum(jnp.sqrt(_dot_exact_rhs(kk * kk, hsum, 2)), 1e-12)
        k2 = k * (1.0 + (a - 1.0) * ka_ref[...])

        parts = _split_bf16(logd, 3)
        cum = dot(tri, parts[0]) + dot(tri, parts[1]) + dot(tri, parts[2])
        cum_last = cum[c - 1:c, :]
        ref = cum[mid - 1:mid, :]
        rel = cum - ref
        g_in = jnp.exp(rel)
        g_inv = jnp.exp(-rel)
        g_tail = jnp.exp(cum_last - cum)
        a_t = -kk * jnp.exp(rel - logd)
        b_t = kk * a * g_inv
        k_t = k2 * g_inv
        r_t = r * g_in
        b_hat = kk * a * g_tail
        k_hat = k2 * g_tail

        st = st_s[g]
        st_ref = (st * jnp.exp(ref)).astype(BF16)
        p0 = _dot_nt(a_t.astype(BF16), st_ref)
        o0 = _dot_nt(r_t.astype(BF16), st_ref)

        bk = jnp.concatenate([b_t, k_t], axis=0).astype(BF16)
        v_bf = v.astype(BF16)
        quads = []
        for h in range(RWKV_HEADS):
            hm = lane_head == h
            ar = jnp.concatenate([jnp.where(hm, a_t, 0.0), jnp.where(hm, r_t, 0.0)], axis=0)
            m4 = _dot_nt(ar.astype(BF16), bk)
            quads.append((jnp.where(strict, m4[:c, :c], 0.0),
                          jnp.where(strict, m4[:c, c:], 0.0).astype(BF16),
                          jnp.where(incl, m4[c:, :c], 0.0).astype(BF16),
                          jnp.where(incl, m4[c:, c:], 0.0).astype(BF16)))
        return dict(r=r, k2=k2, v=v, v_bf=v_bf, gate=gate, p0=p0, o0=o0, b_hat=b_hat, k_hat=k_hat,
                    st=st, decay=jnp.exp(cum_last), quads=quads)

    seqs = [prepare(g) for g in range(n_seq)]
    pairs = [(g, h) for g in range(n_seq) for h in range(RWKV_HEADS)]

    pw = {gh: seqs[gh[0]]["quads"][gh[1]][0] for gh in pairs}
    t_inv = {gh: eye + pw[gh] for gh in pairs}
    pw = {gh: pw[gh].astype(BF16) for gh in pairs}
    span = 2
    while span < c:
        pw_f = {gh: dot(pw[gh], pw[gh]) for gh in pairs}
        pw = {gh: pw_f[gh].astype(BF16) for gh in pairs}
        t_inv = {gh: t_inv[gh] + dot(t_inv[gh].astype(BF16), pw[gh]) for gh in pairs}
        span *= 2

    w_loc = {(g, h): seqs[g]["p0"] + dot(seqs[g]["quads"][h][1], seqs[g]["v_bf"]) for g, h in pairs}
    u_h = {gh: dot(t_inv[gh].astype(BF16), w_loc[gh].astype(BF16)) for gh in pairs}
    o_h = {(g, h): (dot(seqs[g]["quads"][h][2], u_h[(g, h)].astype(BF16))
                    + dot(seqs[g]["quads"][h][3], seqs[g]["v_bf"])) for g, h in pairs}

    for g in range(n_seq):
        sq = seqs[g]
        u_all = jnp.zeros((c, RWKV_W), F32)
        o_all = jnp.zeros((c, RWKV_W), F32)
        for h in range(RWKV_HEADS):
            hm = lane_head == h
            u_all = jnp.where(hm, u_h[(g, h)], u_all)
            o_all = jnp.where(hm, o_h[(g, h)], o_all)
        o = sq["o0"] + o_all
        upd = (_dot_tn(u_all.astype(BF16), sq["b_hat"].astype(BF16))
               + _dot_tn(sq["v_bf"], sq["k_hat"].astype(BF16)))
        st_s[g] = sq["st"] * sq["decay"] + hmask * upd

        inv_n = 1.0 / HEAD_DIM
        mean = _dot_exact_rhs(o, hsum, 2) * inv_n
        dev = o - mean
        var = _dot_exact_rhs(dev * dev, hsum, 2) * inv_n
        y = dev * lax.rsqrt(var + RWKV_GN_EPS) * gng_ref[...] + gnb_ref[...]
        y = y + _dot_exact_rhs(sq["r"] * sq["k2"] * rk_ref[...], hsum, 2) * sq["v"]
        o_ref[g] = y * sq["gate"]


def _rwkv(p3, mu, w0, w_up, a0, a_up, g_up, k_k, k_a, r_k, gn_g, gn_b):
    b, s, _ = p3.shape
    c = RWKV_CHUNK
    w = RWKV_W
    n_seq = RWKV_SEQS_PER_STEP if b % RWKV_SEQS_PER_STEP == 0 else 1
    base = 3 * MOBA_W
    assert base % w == 0 and (base + 3 * w) % LANES == 0
    lora = DECAY_LORA + AAA_LORA

    def col_spec(width, off_cols):
        assert off_cols % width == 0
        return pl.BlockSpec((n_seq, c, width), lambda bi, ci: (bi, ci, off_cols // width))

    row = lambda x: x.reshape(1, -1).astype(F32)
    const = lambda shape: pl.BlockSpec(shape, lambda bi, ci: (0,) * len(shape))
    wup_pad = jnp.concatenate([w_up, jnp.zeros((AAA_LORA, w), F32)], axis=0).astype(BF16)
    aup_pad = jnp.concatenate([jnp.zeros((DECAY_LORA, w), F32), a_up], axis=0).astype(BF16)
    head = np.arange(w) // HEAD_DIM
    hsum = jnp.asarray((head[:, None] == head[None, :]).astype(np.float32)).astype(BF16)
    tri = jnp.asarray(np.tril(np.ones((c, c), np.float32))).astype(BF16)
    vec_args = [mu[:w], mu[w:2 * w], mu[2 * w:3 * w], mu[3 * w:3 * w + lora], mu[3 * w + lora:],
                w0, None, a0, None, None, k_k, k_a, r_k.reshape(-1), gn_g, gn_b]
    args = [p3, p3, p3, p3, p3]
    specs = [col_spec(w, base), col_spec(w, base + w), col_spec(w, base + 2 * w),
             col_spec(lora, base + 3 * w), col_spec(GATE_LORA, base + 3 * w + lora)]
    mats = {6: wup_pad, 8: aup_pad, 9: g_up.astype(BF16)}
    for idx, a in enumerate(vec_args):
        arr = mats[idx] if a is None else row(a)
        args.append(arr)
        specs.append(const(arr.shape))
    for arr in (hsum, tri):
        args.append(arr)
        specs.append(const(arr.shape))
    return pl.pallas_call(
        functools.partial(_rwkv_kernel, n_seq=n_seq),
        grid=(b // n_seq, s // c),
        in_specs=specs,
        out_specs=pl.BlockSpec((n_seq, c, w), lambda bi, ci: (bi, ci, 0)),
        out_shape=jax.ShapeDtypeStruct((b, s, w), F32),
        scratch_shapes=[pltpu.VMEM((n_seq, w, w), F32), pltpu.VMEM((n_seq, 1, w), F32),
                        pltpu.VMEM((n_seq, 1, w), F32), pltpu.VMEM((n_seq, 1, w), F32),
                        pltpu.VMEM((n_seq, 1, lora), F32), pltpu.VMEM((n_seq, 1, GATE_LORA), F32)],
        compiler_params=_cparams(("parallel", "arbitrary")),
        name="rwkv",
    )(*args)


def _mem_attn_kernel(q_ref, kv_ref, o_ref):
    scale = HEAD_DIM ** -0.5
    lane_head = lax.broadcasted_iota(I32, (1, MEM_W), 1) // HEAD_DIM
    q = q_ref[0] * scale
    mk = kv_ref[0, :, :MEM_W].astype(BF16)
    mv = kv_ref[0, :, MEM_W:].astype(BF16)
    out = jnp.zeros(q.shape, F32)
    for h in range(MEM_HEADS):
        hm = lane_head == h
        s = _dot_nt(jnp.where(hm, q, 0.0).astype(BF16), mk)
        s = s - jnp.max(s, axis=-1, keepdims=True)
        e = jnp.exp(s)
        p = e / jnp.sum(e, axis=-1, keepdims=True)
        out = jnp.where(hm, jnp.dot(p.astype(BF16), mv, preferred_element_type=F32), out)
    o_ref[0] = out


def _mem_attention(p3, mkv3, tq):
    b, s, _ = p3.shape
    m = mkv3.shape[1]
    off = (3 * MOBA_W + RWKV_COLS) // MEM_W
    assert off * MEM_W == 3 * MOBA_W + RWKV_COLS
    return pl.pallas_call(
        _mem_attn_kernel,
        grid=(b, s // tq),
        in_specs=[pl.BlockSpec((1, tq, MEM_W), lambda bi, i: (bi, i, off)),
                  pl.BlockSpec((1, m, 2 * MEM_W), lambda bi, i: (bi, 0, 0))],
        out_specs=pl.BlockSpec((1, tq, MEM_W), lambda bi, i: (bi, i, 0)),
        out_shape=jax.ShapeDtypeStruct((b, s, MEM_W), F32),
        compiler_params=_cparams(("parallel", "parallel")),
        name="mem_attn",
    )(p3, mkv3)


def _out_router_kernel(x_ref, ym_ref, yr_ref, ye_ref, wo1_ref, wo2_ref, wo3_ref, g_ref,
                       wrh_ref, wrl_ref, br_ref, upper_ref, ones_ref,
                       x1_ref, h_ref, idx_ref, rank_ref, wgt_ref, cnt_ref, run_s):
    @pl.when(pl.program_id(0) == 0)
    def _():
        run_s[...] = jnp.zeros_like(run_s)

    x1 = (x_ref[...] + _bdot(ym_ref[...], wo1_ref[...]) + _bdot(yr_ref[...], wo2_ref[...])
          + _bdot(ye_ref[...], wo3_ref[...]))
    x1_ref[...] = x1
    h = _rms(x1, g_ref[...])
    h_ref[...] = _pack_halves(h)
    tm = h.shape[0]
    h_hi, h_lo = _split_bf16(h, 2)
    dot = functools.partial(jnp.dot, preferred_element_type=F32)
    logits = dot(h_hi, wrh_ref[...]) + (dot(h_hi, wrl_ref[...]) + dot(h_lo, wrh_ref[...])) + br_ref[...]
    lg = logits.T[:N_EXPERTS, :]
    e_iota = lax.broadcasted_iota(I32, (N_EXPERTS, tm), 0)

    vals, idxs = [], []
    for _ in range(TOP_K):
        m = jnp.max(lg, axis=0, keepdims=True)
        idx = jnp.min(jnp.where(lg == m, e_iota, N_EXPERTS), axis=0, keepdims=True)
        vals.append(m)
        idxs.append(idx)
        lg = jnp.where(e_iota == idx, NEG_INF, lg)
    exps = [jnp.exp(vk - vals[0]) for vk in vals]
    denom = exps[0] + exps[1] + exps[2] + exps[3]

    chosen = jnp.zeros((N_EXPERTS, tm), F32)
    for idx in idxs:
        chosen = chosen + jnp.where(e_iota == idx, 1.0, 0.0)
    chosen = chosen.astype(BF16)
    run = run_s[...]
    before = dot(chosen, upper_ref[...]) + jnp.concatenate([run] * (tm // LANES), axis=1)
    run_s[...] = run + dot(chosen, ones_ref[...])
    cnt_ref[...] = run_s[...]

    zero_i = jnp.zeros((8 - TOP_K, tm), I32)
    ranks = [jnp.sum(jnp.where(e_iota == idx, before, 0.0), axis=0, keepdims=True).astype(I32)
             for idx in idxs]
    idx_ref[0] = jnp.concatenate(idxs + [zero_i], axis=0)
    rank_ref[0] = jnp.concatenate(ranks + [zero_i], axis=0)
    wrows = jnp.concatenate([e / denom for e in exps] + [jnp.zeros((LANES - TOP_K, tm), F32)], axis=0)
    wgt_ref[...] = wrows.T


def _out_router(x2, ym, yr, ye, w_out, g_ffn, w_router, b_router, tm):
    t, d = x2.shape
    wo = w_out.astype(BF16)
    wo1, wo2, wo3 = wo[:MOBA_W], wo[MOBA_W:MOBA_W + RWKV_W], wo[MOBA_W + RWKV_W:]
    wr = jnp.zeros((d, LANES), F32).at[:, :N_EXPERTS].set(w_router)
    wr_hi = wr.astype(BF16)
    wr_lo = (wr - wr_hi.astype(F32)).astype(BF16)
    br = jnp.full((1, LANES), NEG_INF, F32).at[0, :N_EXPERTS].set(b_router)
    upper = jnp.asarray(np.triu(np.ones((tm, tm), np.float32), 1)).astype(BF16)
    ones = jnp.ones((tm, LANES), BF16)
    tile = lambda n: pl.BlockSpec((tm, n), lambda i: (i, 0))
    slots = pl.BlockSpec((1, 8, tm), lambda i: (i, 0, 0))
    const = lambda a: pl.BlockSpec(a.shape, lambda i: (0,) * a.ndim)
    g2 = g_ffn.reshape(1, d)
    n = t // tm
    return pl.pallas_call(
        _out_router_kernel,
        grid=(n,),
        in_specs=[tile(d), tile(MOBA_W), tile(RWKV_W), tile(MEM_W), const(wo1), const(wo2), const(wo3),
                  const(g2), const(wr_hi), const(wr_lo), const(br), const(upper), const(ones)],
        out_specs=[tile(d), tile(d // 2), slots, slots, tile(LANES),
                   pl.BlockSpec((N_EXPERTS, LANES), lambda i: (0, 0))],
        out_shape=[jax.ShapeDtypeStruct((t, d), F32), jax.ShapeDtypeStruct((t, d // 2), I32),
                   jax.ShapeDtypeStruct((n, 8, tm), I32), jax.ShapeDtypeStruct((n, 8, tm), I32),
                   jax.ShapeDtypeStruct((t, LANES), F32), jax.ShapeDtypeStruct((N_EXPERTS, LANES), F32)],
        scratch_shapes=[pltpu.VMEM((N_EXPERTS, LANES), F32)],
        compiler_params=_cparams(("arbitrary",)),
        name="out_router",
    )(x2, ym, yr, ye, wo1, wo2, wo3, g2, wr_hi, wr_lo, br, upper, ones)


def _sc_mesh():
    return plsc.VectorSubcoreMesh(core_axis_name="c", subcore_axis_name="s",
                                  num_cores=SC_CORES, num_subcores=SC_SUBCORES)


def _sc_split(n_rows):
    n_workers = SC_CORES * SC_SUBCORES
    per_worker = n_rows // n_workers
    assert per_worker * n_workers == n_rows and per_worker % SC_ROWS == 0 and SC_ROWS % 8 == 0
    return per_worker


def _sc_gather_rows(table, idx):
    n_rows = idx.shape[0]
    width = table.shape[1]
    per_worker = _sc_split(n_rows)

    @functools.partial(
        pl.kernel, mesh=_sc_mesh(),
        out_type=jax.ShapeDtypeStruct((n_rows, width), table.dtype),
        scratch_types=[pltpu.VMEM((SC_ROWS,), I32), pltpu.VMEM((SC_ROWS, width), table.dtype),
                       pltpu.SemaphoreType.DMA],
        name="sc_gather",
    )
    def gather(table_hbm, idx_hbm, out_hbm, idx_v, rows_v, sem):
        worker = lax.axis_index("s") * SC_CORES + lax.axis_index("c")
        base = worker * per_worker

        @pl.loop(0, per_worker // SC_ROWS)
        def _(step):
            off = pl.multiple_of(base + step * SC_ROWS, 8)
            pltpu.sync_copy(idx_hbm.at[pl.ds(off, SC_ROWS)], idx_v)
            pltpu.async_copy(table_hbm.at[idx_v], rows_v, sem).wait()
            pltpu.sync_copy(rows_v, out_hbm.at[pl.ds(off, SC_ROWS)])

    return gather(table, idx)


def _sc_scatter_rows(rows, idx, n_out):
    n_idx = idx.shape[0]
    n_src, width = rows.shape
    per_worker = _sc_split(n_idx)
    assert n_src % SC_ROWS == 0

    @functools.partial(
        pl.kernel, mesh=_sc_mesh(),
        out_type=jax.ShapeDtypeStruct((n_out, width), rows.dtype),
        scratch_types=[pltpu.VMEM((SC_ROWS,), I32), pltpu.VMEM((SC_ROWS, width), rows.dtype),
                       pltpu.SemaphoreType.DMA],
        name="sc_scatter",
    )
    def scatter(rows_hbm, idx_hbm, out_hbm, idx_v, rows_v, sem):
        worker = lax.axis_index("s") * SC_CORES + lax.axis_index("c")
        base = worker * per_worker

        @pl.loop(0, per_worker // SC_ROWS)
        def _(step):
            off = pl.multiple_of(base + step * SC_ROWS, 8)
            src = pl.multiple_of(lax.rem(off, n_src), 8)
            pltpu.sync_copy(idx_hbm.at[pl.ds(off, SC_ROWS)], idx_v)
            pltpu.sync_copy(rows_hbm.at[pl.ds(src, SC_ROWS)], rows_v)
            pltpu.async_copy(rows_v, out_hbm.at[idx_v], sem).wait()

    return scatter(rows, idx)


def _experts_kernel(be_ref, bv_ref, nused_ref, xs_ref, wgu_ref, bg_ref, bu_ref, wd_ref, bd_ref, perm_ref,
                    o_ref, wg_s, wu_s, wd_s):
    i = pl.program_id(0)
    prev = be_ref[jnp.maximum(i - 1, 0)]
    changed = (i == 0) | (be_ref[i] != prev)

    @pl.when(changed & (i < nused_ref[0]))
    def _():
        half = LANES
        for cblk in range(2 * D_EXPERT // (2 * half)):
            wt = wgu_ref[0, :, cblk * 2 * half:(cblk + 1) * 2 * half].astype(BF16)
            sep = jnp.dot(wt, perm_ref[...], preferred_element_type=F32).astype(BF16)
            wg_s[:, cblk * half:(cblk + 1) * half] = sep[:, :half]
            wu_s[:, cblk * half:(cblk + 1) * half] = sep[:, half:]
        wd_s[...] = wd_ref[0].astype(BF16)

    @pl.when(i < nused_ref[0])
    def _():
        rows = lax.broadcasted_iota(I32, xs_ref.shape, 0)
        xb = _unpack_halves(jnp.where(rows < bv_ref[i], xs_ref[...], 0)).astype(BF16)
        gate = jnp.dot(xb, wg_s[...], preferred_element_type=F32) + bg_ref[0]
        up = jnp.dot(xb, wu_s[...], preferred_element_type=F32) + bu_ref[0]
        gate = jnp.minimum(gate, SWIGLU_LIMIT)
        up = jnp.clip(up, -SWIGLU_LIMIT, SWIGLU_LIMIT)
        glu = gate * _sigmoid(gate * SWIGLU_ALPHA)
        act = ((up + 1.0) * glu).astype(BF16)
        o_ref[...] = _pack_halves(jnp.dot(act, wd_s[...], preferred_element_type=F32) + bd_ref[0])

    @pl.when(i >= nused_ref[0])
    def _():
        o_ref[...] = jnp.zeros_like(o_ref)


def _experts(blk_expert, blk_valid, n_used, xs, w_gate_up, b_gate_up, w_down, b_down):
    p_rows = xs.shape[0]
    d = 2 * xs.shape[1]
    n_blocks = p_rows // EXPERT_ROWS
    bg = b_gate_up[:, 0::2].reshape(N_EXPERTS, 1, D_EXPERT)
    bu = b_gate_up[:, 1::2].reshape(N_EXPERTS, 1, D_EXPERT)
    bd = b_down.reshape(N_EXPERTS, 1, d)
    perm_np = np.zeros((2 * LANES, 2 * LANES), np.float32)
    perm_np[2 * np.arange(LANES), np.arange(LANES)] = 1.0
    perm_np[2 * np.arange(LANES) + 1, LANES + np.arange(LANES)] = 1.0
    perm = jnp.asarray(perm_np).astype(BF16)
    by_expert = lambda shape: pl.BlockSpec((1,) + shape, lambda i, be, bv, nu: (be[i], 0, 0))
    grid_spec = pltpu.PrefetchScalarGridSpec(
        num_scalar_prefetch=3,
        grid=(n_blocks,),
        in_specs=[pl.BlockSpec((EXPERT_ROWS, d // 2), lambda i, be, bv, nu: (i, 0)),
                  by_expert((d, 2 * D_EXPERT)), by_expert((1, D_EXPERT)), by_expert((1, D_EXPERT)),
                  by_expert((D_EXPERT, d)), by_expert((1, d)),
                  pl.BlockSpec(perm.shape, lambda i, be, bv, nu: (0, 0))],
        out_specs=pl.BlockSpec((EXPERT_ROWS, d // 2), lambda i, be, bv, nu: (i, 0)),
        scratch_shapes=[pltpu.VMEM((d, D_EXPERT), BF16), pltpu.VMEM((d, D_EXPERT), BF16),
                        pltpu.VMEM((D_EXPERT, d), BF16)],
    )
    return pl.pallas_call(
        _experts_kernel,
        grid_spec=grid_spec,
        out_shape=jax.ShapeDtypeStruct((p_rows, d // 2), I32),
        compiler_params=_cparams(("arbitrary",)),
        name="experts",
    )(blk_expert, blk_valid, n_used, xs, w_gate_up, bg, bu, w_down, bd, perm)


def _combine_kernel(yg_ref, x1_ref, w_ref, g_ref, o_ref):
    acc = x1_ref[...]
    wts = w_ref[...]
    for kk in range(TOP_K):
        acc = acc + _unpack_halves(yg_ref[kk]) * wts[:, kk:kk + 1]
    o_ref[...] = _rms(acc, g_ref[...])


def _combine(yg, x1, wts, g_final, tb):
    t, d = x1.shape
    half = yg.shape[2]
    return pl.pallas_call(
        _combine_kernel,
        grid=(t // tb,),
        in_specs=[pl.BlockSpec((TOP_K, tb, half), lambda i: (0, i, 0)),
                  pl.BlockSpec((tb, d), lambda i: (i, 0)),
                  pl.BlockSpec((tb, LANES), lambda i: (i, 0)),
                  pl.BlockSpec((1, d), lambda i: (0, 0))],
        out_specs=pl.BlockSpec((tb, d), lambda i: (i, 0)),
        out_shape=jax.ShapeDtypeStruct((t, d), F32),
        compiler_params=_cparams(("parallel",)),
        name="combine",
    )(yg, x1, wts, g_final.reshape(1, d))


def _layer(x, mem, w_in, w_out, w_mem_kv, g_mix, g_mem, g_ffn, bias_tables, mu, w0, w_up, a0, a_up,
           g_up, k_k, k_a, r_k, gn_g, gn_b, w_router, b_router, w_gate_up, b_gate_up, w_down, b_down,
           g_last):
    b, s, d = x.shape
    m = mem.shape[1]
    t = b * s
    x2 = x.reshape(t, d)

    p = _norm_matmul(x2, g_mix, w_in.astype(BF16), 512, "in_proj")
    p3 = p.reshape(b, s, IN_COLS)
    y_moba = _moba_attention(p3, *bias_tables)
    y_rwkv = _rwkv(p3, mu, w0, w_up, a0, a_up, g_up, k_k, k_a, r_k, gn_g, gn_b)
    mkv = _norm_matmul(mem.reshape(b * m, d), g_mem, w_mem_kv.astype(BF16), 512, "mem_kv")
    y_mem = _mem_attention(p3, mkv.reshape(b, m, 2 * MEM_W), 512)

    tb = ROUTE_TOKENS
    x1, h2, idx_o, rank_o, wgt_p, cnt = _out_router(
        x2, y_moba.reshape(t, MOBA_W), y_rwkv.reshape(t, RWKV_W), y_mem.reshape(t, MEM_W),
        w_out, g_ffn, w_router, b_router, tb)

    counts = cnt[:, 0].astype(I32)
    padded = (counts + EXPERT_ROWS - 1) // EXPERT_ROWS * EXPERT_ROWS
    pad_ends = jnp.cumsum(padded)
    pad_starts = (pad_ends - padded).astype(I32)
    n_blocks = (t * TOP_K) // EXPERT_ROWS + N_EXPERTS
    blk_start = jnp.arange(n_blocks, dtype=I32) * EXPERT_ROWS
    blk_expert = jnp.minimum(jnp.sum(blk_start[:, None] >= pad_ends[None, :], axis=1),
                             N_EXPERTS - 1).astype(I32)
    blk_valid = jnp.clip(counts[blk_expert] - (blk_start - pad_starts[blk_expert]), 0, EXPERT_ROWS)
    n_used = (pad_ends[-1:] // EXPERT_ROWS).astype(I32)
    idx_kt = jnp.swapaxes(idx_o[:, :TOP_K, :], 0, 1).reshape(TOP_K, t)
    rank_kt = jnp.swapaxes(rank_o[:, :TOP_K, :], 0, 1).reshape(TOP_K, t)
    start_kt = jnp.zeros((TOP_K, t), I32)
    for e in range(N_EXPERTS):
        start_kt = jnp.where(idx_kt == e, pad_starts[e], start_kt)
    dest = (start_kt + rank_kt).reshape(TOP_K * t)

    xs = _sc_scatter_rows(h2, dest, n_blocks * EXPERT_ROWS)
    ys = _experts(blk_expert, blk_valid.astype(I32), n_used, xs, w_gate_up, b_gate_up, w_down, b_down)
    yg = _sc_gather_rows(ys, dest).reshape(TOP_K, t, d // 2)
    out = _combine(yg, x1, wgt_p, g_last, tb)
    return out.reshape(b, s, d)


def kernel(x, mem, w_in, w_out, w_mem_kv, g_mix, g_mem, g_ffn, g_final, rel_bias, rwkv_mu, rwkv_w0,
           rwkv_w_up, rwkv_a0, rwkv_a_up, rwkv_g_up, rwkv_k_k, rwkv_k_a, rwkv_r_k, rwkv_gn_g, rwkv_gn_b,
           w_router, b_router, w_gate_up, b_gate_up, w_down, b_down):
    depth = w_in.shape[0]
    assert depth == 1, "the final norm is fused into the last layer's combine kernel"
    bias_tables = _moba_bias_tables(rel_bias)
    l = 0
    return _layer(x, mem, w_in[l], w_out[l], w_mem_kv[l], g_mix[l], g_mem[l], g_ffn[l], bias_tables,
                  rwkv_mu[l], rwkv_w0[l], rwkv_w_up[l], rwkv_a0[l], rwkv_a_up[l], rwkv_g_up[l],
                  rwkv_k_k[l], rwkv_k_a[l], rwkv_r_k[l], rwkv_gn_g[l], rwkv_gn_b[l], w_router[l],
                  b_router[l], w_gate_up[l], b_gate_up[l], w_down[l], b_down[l], g_final)
```

```python
import functools
import math

import numpy as np
import jax
import jax.numpy as jnp
from jax import lax
from jax.experimental import pallas as pl
from jax.experimental.pallas import tpu as pltpu
from jax.experimental.pallas import tpu_sc as plsc

F32 = jnp.float32
BF16 = jnp.bfloat16
I32 = jnp.int32
HI = lax.Precision.HIGHEST

D_MODEL = 1024
HEAD_DIM = 64
MOBA_HEADS = 6
RWKV_HEADS = 6
MEM_HEADS = 4
MOBA_W = MOBA_HEADS * HEAD_DIM
RWKV_W = RWKV_HEADS * HEAD_DIM
MEM_W = MEM_HEADS * HEAD_DIM
MOBA_BLOCK = 256
MOBA_TOPK = 3
N_BUCKETS = 32
MAX_DISTANCE = 128
DECAY_LORA = 64
AAA_LORA = 64
GATE_LORA = 128
RWKV_COLS = 3 * RWKV_W + DECAY_LORA + AAA_LORA + GATE_LORA
RWKV_GN_EPS = 64e-5
IN_COLS = 3 * MOBA_W + RWKV_COLS + MEM_W
N_EXPERTS = 32
TOP_K = 4
D_EXPERT = D_MODEL
SWIGLU_ALPHA = 1.702
SWIGLU_LIMIT = 7.0
RMS_EPS = 1e-5

LANES = 128
RWKV_CHUNK = 128
RWKV_SEQS_PER_STEP = 2
EXPERT_ROWS = 256
ROUTE_TOKENS = 256
SC_CORES = 2
SC_SUBCORES = 16
SC_ROWS = 64
VMEM_LIMIT = 56 * 1024 * 1024
NEG_INF = float("-inf")
HIGH_HALF = -65536


def _cparams(sem):
    return pltpu.CompilerParams(dimension_semantics=sem, vmem_limit_bytes=VMEM_LIMIT)


def _rms(x, g):
    return x * lax.rsqrt(jnp.mean(x * x, axis=-1, keepdims=True) + RMS_EPS) * g


def _bdot(a, b):
    return jnp.dot(a.astype(BF16), b.astype(BF16), preferred_element_type=F32)


def _hdot(a, b):
    return jnp.dot(a, b, preferred_element_type=F32, precision=HI)


def _dot_nt(a, b):
    return lax.dot_general(a, b, (((1,), (1,)), ((), ())), preferred_element_type=F32)


def _dot_tn(a, b):
    return lax.dot_general(a, b, (((0,), (0,)), ((), ())), preferred_element_type=F32)


def _split_bf16(x, terms):
    parts = []
    for _ in range(terms):
        hi = x.astype(BF16)
        parts.append(hi)
        x = x - hi.astype(F32)
    return parts


def _dot_exact_rhs(x, m_bf16, terms):
    acc = None
    for part in _split_bf16(x, terms):
        d = jnp.dot(part, m_bf16, preferred_element_type=F32)
        acc = d if acc is None else acc + d
    return acc


def _pack_halves(x):
    n = x.shape[1] // 2
    lo = pltpu.bitcast(x[:, :n].astype(BF16).astype(F32), I32)
    hi = pltpu.bitcast(x[:, n:].astype(BF16).astype(F32), I32)
    return (hi & HIGH_HALF) | lax.shift_right_logical(lo, 16)


def _unpack_halves(w):
    lo = pltpu.bitcast(w << 16, F32)
    hi = pltpu.bitcast(w & HIGH_HALF, F32)
    return jnp.concatenate([lo, hi], axis=1)


def _norm_matmul_kernel(x_ref, g_ref, w_ref, o_ref):
    h = _rms(x_ref[...], g_ref[...])
    o_ref[...] = jnp.dot(h.astype(BF16), w_ref[...], preferred_element_type=F32)


def _norm_matmul(x, g, w_bf16, tm, name):
    t, d = x.shape
    n = w_bf16.shape[1]
    tm = min(tm, t)
    return pl.pallas_call(
        _norm_matmul_kernel,
        grid=(t // tm,),
        in_specs=[pl.BlockSpec((tm, d), lambda i: (i, 0)),
                  pl.BlockSpec((1, d), lambda i: (0, 0)),
                  pl.BlockSpec((d, n), lambda i: (0, 0))],
        out_specs=pl.BlockSpec((tm, n), lambda i: (i, 0)),
        out_shape=jax.ShapeDtypeStruct((t, n), F32),
        compiler_params=_cparams(("parallel",)),
        name=name,
    )(x, g.reshape(1, d), w_bf16)


def _t5_bucket_np(dist):
    n = np.maximum(dist, 0)
    max_exact = N_BUCKETS // 2
    nf = np.maximum(n, 1).astype(np.float64)
    large = max_exact + (np.log(nf / max_exact) / math.log(MAX_DISTANCE / max_exact)
                         * (N_BUCKETS - max_exact)).astype(np.int64)
    large = np.minimum(large, N_BUCKETS - 1)
    return np.where(n < max_exact, n, large)


def _moba_bias_tables(rel_bias):
    kk = np.arange(MOBA_BLOCK)[:, None]
    qq = np.arange(MOBA_BLOCK)[None, :]
    d_own = qq - kk
    d_prev = MOBA_BLOCK + qq - kk
    assert np.all(_t5_bucket_np(np.arange(MOBA_BLOCK + 1, 64 * MOBA_BLOCK)) == N_BUCKETS - 1)
    bias_t = rel_bias.astype(F32).T

    def lookup(bucket):
        bucket = jnp.asarray(bucket, I32)[None]
        tab = jnp.zeros((MOBA_HEADS,) + bucket.shape[1:], F32)
        for b in range(N_BUCKETS):
            tab = jnp.where(bucket == b, bias_t[:, b][:, None, None], tab)
        return tab

    t_own = jnp.where(jnp.asarray(d_own >= 0)[None], lookup(_t5_bucket_np(d_own)), NEG_INF)
    t_prev = lookup(_t5_bucket_np(d_prev))
    far = bias_t[:, N_BUCKETS - 1]
    return t_own, t_prev, far


def _moba_kernel(far_ref, q_ref, k_ref, v_ref, t0_ref, t1_ref, o_ref,
                 qt_s, kb_s, vt_s, sc_s, *, nb):
    hp = pl.program_id(1)
    blk = MOBA_BLOCK
    scale = HEAD_DIM ** -0.5
    row_head = lax.broadcasted_iota(I32, (LANES, 1), 0) // HEAD_DIM
    lane_head = lax.broadcasted_iota(I32, (1, LANES), 1) // HEAD_DIM

    kmean_rows = []
    for j in range(nb):
        sl = pl.ds(j * blk, blk)
        qt_s[j] = (q_ref[0, sl, :] * scale).T
        vt_s[j] = v_ref[0, sl, :].T.astype(BF16)
        kj = k_ref[0, sl, :]
        kb_s[j] = kj.astype(BF16)
        kmean_rows.append(jnp.mean(kj, axis=0, keepdims=True))
    kmean = jnp.concatenate(kmean_rows, axis=0)
    km2 = jnp.concatenate([jnp.where(lane_head == 0, kmean, 0.0),
                           jnp.where(lane_head == 1, kmean, 0.0)], axis=0)
    blk_iota = lax.broadcasted_iota(I32, (nb, blk), 0)

    for i in range(nb):
        qt = qt_s[i]
        negs = [None, None]
        if i > MOBA_TOPK:
            gate2 = _hdot(km2, qt)
            for h in range(2):
                g = gate2[h * nb:(h + 1) * nb, :]
                cnt = jnp.zeros((nb, blk), I32)
                for m in range(i):
                    gm = g[m:m + 1, :]
                    beats = (gm > g) | ((gm == g) & (m < blk_iota))
                    cnt = cnt + jnp.where(beats, 1, 0)
                negs[h] = jnp.where(cnt < MOBA_TOPK, 0.0, NEG_INF)
        outs = []
        for h in range(2):
            qt_h = jnp.where(row_head == h, qt, 0.0).astype(BF16)
            far_bias = far_ref[2 * hp + h]
            m_run = None
            for j in range(i + 1):
                s = jnp.dot(kb_s[j], qt_h, preferred_element_type=F32)
                if j == i:
                    s = s + t0_ref[h]
                else:
                    bias = t1_ref[h] if j == i - 1 else far_bias
                    if negs[h] is not None:
                        bias = bias + negs[h][j:j + 1, :]
                    s = s + bias
                sc_s[h, j] = s
                cm = jnp.max(s, axis=0, keepdims=True)
                m_run = cm if m_run is None else jnp.maximum(m_run, cm)
            l_run = jnp.zeros((1, blk), F32)
            acc = jnp.zeros((HEAD_DIM, blk), F32)
            for j in range(i + 1):
                p_t = jnp.exp(sc_s[h, j] - m_run)
                l_run = l_run + jnp.sum(p_t, axis=0, keepdims=True)
                vt_h = vt_s[j, h * HEAD_DIM:(h + 1) * HEAD_DIM, :]
                acc = acc + jnp.dot(vt_h, p_t.astype(BF16), preferred_element_type=F32)
            outs.append(acc / l_run)
        o_t = jnp.concatenate(outs, axis=0)
        o_ref[0, pl.ds(i * blk, blk), :] = o_t.T


def _moba_attention(p3, t_own, t_prev, far):
    b, s, _ = p3.shape
    nb = s // MOBA_BLOCK
    n_pairs = MOBA_HEADS // 2
    blk_spec = lambda off: pl.BlockSpec((1, s, LANES), lambda bi, hp: (bi, 0, off + hp))
    tab_spec = pl.BlockSpec((2, MOBA_BLOCK, MOBA_BLOCK), lambda bi, hp: (hp, 0, 0))
    return pl.pallas_call(
        functools.partial(_moba_kernel, nb=nb),
        grid=(b, n_pairs),
        in_specs=[pl.BlockSpec(memory_space=pltpu.SMEM),
                  blk_spec(0), blk_spec(n_pairs), blk_spec(2 * n_pairs), tab_spec, tab_spec],
        out_specs=pl.BlockSpec((1, s, LANES), lambda bi, hp: (bi, 0, hp)),
        out_shape=jax.ShapeDtypeStruct((b, s, MOBA_W), F32),
        scratch_shapes=[pltpu.VMEM((nb, LANES, MOBA_BLOCK), F32),
                        pltpu.VMEM((nb, MOBA_BLOCK, LANES), BF16),
                        pltpu.VMEM((nb, LANES, MOBA_BLOCK), BF16),
                        pltpu.VMEM((2, nb, MOBA_BLOCK, MOBA_BLOCK), F32)],
        compiler_params=_cparams(("parallel", "parallel")),
        name="moba",
    )(far, p3, p3, p3, t_own, t_prev)


def _softplus(z):
    return jnp.maximum(z, 0.0) + jnp.log(1.0 + jnp.exp(-jnp.abs(z)))


def _sigmoid(z):
    return 1.0 / (1.0 + jnp.exp(-z))


def _rwkv_kernel(r_ref, k_ref, v_ref, wa_ref, g_ref,
                 mu_r_ref, mu_k_ref, mu_v_ref, mu_wa_ref, mu_g_ref,
                 w0_ref, wup_ref, a0_ref, aup_ref, gup_ref, kk_ref, ka_ref, rk_ref,
                 gng_ref, gnb_ref, hsum_ref, tri_ref,
                 o_ref, st_s, prev_r, prev_k, prev_v, prev_wa, prev_g, *, n_seq):
    c = RWKV_CHUNK
    mid = c // 2

    @pl.when(pl.program_id(1) == 0)
    def _():
        st_s[...] = jnp.zeros_like(st_s)
        prev_r[...] = jnp.zeros_like(prev_r)
        prev_k[...] = jnp.zeros_like(prev_k)
        prev_v[...] = jnp.zeros_like(prev_v)
        prev_wa[...] = jnp.zeros_like(prev_wa)
        prev_g[...] = jnp.zeros_like(prev_g)

    hsum = hsum_ref[...]
    tri = tri_ref[...]
    row = lax.broadcasted_iota(I32, (c, c), 0)
    col = lax.broadcasted_iota(I32, (c, c), 1)
    strict = row > col
    incl = row >= col
    eye = (row == col).astype(F32)
    lane_head = lax.broadcasted_iota(I32, (1, RWKV_W), 1) // HEAD_DIM
    hmask = hsum.astype(F32)
    dot = functools.partial(jnp.dot, preferred_element_type=F32)

    def mix(g, x_ref, prev_ref, mu_ref):
        x = x_ref[g]
        rows = lax.broadcasted_iota(I32, x.shape, 0)
        shifted = jnp.where(rows == 0, prev_ref[g], pltpu.roll(x, 1, axis=0))
        prev_ref[g] = x[c - 1:c, :]
        return x + (shifted - x) * mu_ref[...]

    def prepare(g):
        r = mix(g, r_ref, prev_r, mu_r_ref)
        k = mix(g, k_ref, prev_k, mu_k_ref)
        v = mix(g, v_ref, prev_v, mu_v_ref)
        xwa = mix(g, wa_ref, prev_wa, mu_wa_ref)
        xg = mix(g, g_ref, prev_g, mu_g_ref)

        w = -_softplus(-(w0_ref[...] + _bdot(jnp.tanh(xwa), wup_ref[...]))) - 0.5
        logd = -jnp.exp(w)
        a = _sigmoid(a0_ref[...] + _bdot(xwa, aup_ref[...]))
        gate = _bdot(_sigmoid(xg), gup_ref[...])
        kk = k * kk_ref[...]
        kk = kk / jnp.maximum(jnp.sqrt(_dot_exact_rhs(kk * kk, hsum, 2)), 1e-12)
        k2 = k * (1.0 + (a - 1.0) * ka_ref[...])

        parts = _split_bf16(logd, 3)
        cum = dot(tri, parts[0]) + dot(tri, parts[1]) + dot(tri, parts[2])
        cum_last = cum[c - 1:c, :]
        ref = cum[mid - 1:mid, :]
        rel = cum - ref
        g_in = jnp.exp(rel)
        g_inv = jnp.exp(-rel)
        g_tail = jnp.exp(cum_last - cum)
        a_t = -kk * jnp.exp(rel - logd)
        b_t = kk * a * g_inv
        k_t = k2 * g_inv
        r_t = r * g_in
        b_hat = kk * a * g_tail
        k_hat = k2 * g_tail

        st = st_s[g]
        st_ref = (st * jnp.exp(ref)).astype(BF16)
        p0 = _dot_nt(a_t.astype(BF16), st_ref)
        o0 = _dot_nt(r_t.astype(BF16), st_ref)

        bk = jnp.concatenate([b_t, k_t], axis=0).astype(BF16)
        v_bf = v.astype(BF16)
        quads = []
        for h in range(RWKV_HEADS):
            hm = lane_head == h
            ar = jnp.concatenate([jnp.where(hm, a_t, 0.0), jnp.where(hm, r_t, 0.0)], axis=0)
            m4 = _dot_nt(ar.astype(BF16), bk)
            quads.append((jnp.where(strict, m4[:c, :c], 0.0),
                          jnp.where(strict, m4[:c, c:], 0.0).astype(BF16),
                          jnp.where(incl, m4[c:, :c], 0.0).astype(BF16),
                          jnp.where(incl, m4[c:, c:], 0.0).astype(BF16)))
        return dict(r=r, k2=k2, v=v, v_bf=v_bf, gate=gate, p0=p0, o0=o0, b_hat=b_hat, k_hat=k_hat,
                    st=st, decay=jnp.exp(cum_last), quads=quads)

    seqs = [prepare(g) for g in range(n_seq)]
    pairs = [(g, h) for g in range(n_seq) for h in range(RWKV_HEADS)]

    pw = {gh: seqs[gh[0]]["quads"][gh[1]][0] for gh in pairs}
    t_inv = {gh: eye + pw[gh] for gh in pairs}
    pw = {gh: pw[gh].astype(BF16) for gh in pairs}
    span = 2
    while span < c:
        pw_f = {gh: dot(pw[gh], pw[gh]) for gh in pairs}
        pw = {gh: pw_f[gh].astype(BF16) for gh in pairs}
        t_inv = {gh: t_inv[gh] + dot(t_inv[gh].astype(BF16), pw[gh]) for gh in pairs}
        span *= 2

    w_loc = {(g, h): seqs[g]["p0"] + dot(seqs[g]["quads"][h][1], seqs[g]["v_bf"]) for g, h in pairs}
    u_h = {gh: dot(t_inv[gh].astype(BF16), w_loc[gh].astype(BF16)) for gh in pairs}
    o_h = {(g, h): (dot(seqs[g]["quads"][h][2], u_h[(g, h)].astype(BF16))
                    + dot(seqs[g]["quads"][h][3], seqs[g]["v_bf"])) for g, h in pairs}

    for g in range(n_seq):
        sq = seqs[g]
        u_all = jnp.zeros((c, RWKV_W), F32)
        o_all = jnp.zeros((c, RWKV_W), F32)
        for h in range(RWKV_HEADS):
            hm = lane_head == h
            u_all = jnp.where(hm, u_h[(g, h)], u_all)
            o_all = jnp.where(hm, o_h[(g, h)], o_all)
        o = sq["o0"] + o_all
        upd = (_dot_tn(u_all.astype(BF16), sq["b_hat"].astype(BF16))
               + _dot_tn(sq["v_bf"], sq["k_hat"].astype(BF16)))
        st_s[g] = sq["st"] * sq["decay"] + hmask * upd

        inv_n = 1.0 / HEAD_DIM
        mean = _dot_exact_rhs(o, hsum, 2) * inv_n
        dev = o - mean
        var = _dot_exact_rhs(dev * dev, hsum, 2) * inv_n
        y = dev * lax.rsqrt(var + RWKV_GN_EPS) * gng_ref[...] + gnb_ref[...]
        y = y + _dot_exact_rhs(sq["r"] * sq["k2"] * rk_ref[...], hsum, 2) * sq["v"]
        o_ref[g] = y * sq["gate"]


def _rwkv(p3, mu, w0, w_up, a0, a_up, g_up, k_k, k_a, r_k, gn_g, gn_b):
    b, s, _ = p3.shape
    c = RWKV_CHUNK
    w = RWKV_W
    n_seq = RWKV_SEQS_PER_STEP if b % RWKV_SEQS_PER_STEP == 0 else 1
    base = 3 * MOBA_W
    assert base % w == 0 and (base + 3 * w) % LANES == 0
    lora = DECAY_LORA + AAA_LORA

    def col_spec(width, off_cols):
        assert off_cols % width == 0
        return pl.BlockSpec((n_seq, c, width), lambda bi, ci: (bi, ci, off_cols // width))

    row = lambda x: x.reshape(1, -1).astype(F32)
    const = lambda shape: pl.BlockSpec(shape, lambda bi, ci: (0,) * len(shape))
    wup_pad = jnp.concatenate([w_up, jnp.zeros((AAA_LORA, w), F32)], axis=0).astype(BF16)
    aup_pad = jnp.concatenate([jnp.zeros((DECAY_LORA, w), F32), a_up], axis=0).astype(BF16)
    head = np.arange(w) // HEAD_DIM
    hsum = jnp.asarray((head[:, None] == head[None, :]).astype(np.float32)).astype(BF16)
    tri = jnp.asarray(np.tril(np.ones((c, c), np.float32))).astype(BF16)
    vec_args = [mu[:w], mu[w:2 * w], mu[2 * w:3 * w], mu[3 * w:3 * w + lora], mu[3 * w + lora:],
                w0, None, a0, None, None, k_k, k_a, r_k.reshape(-1), gn_g, gn_b]
    args = [p3, p3, p3, p3, p3]
    specs = [col_spec(w, base), col_spec(w, base + w), col_spec(w, base + 2 * w),
             col_spec(lora, base + 3 * w), col_spec(GATE_LORA, base + 3 * w + lora)]
    mats = {6: wup_pad, 8: aup_pad, 9: g_up.astype(BF16)}
    for idx, a in enumerate(vec_args):
        arr = mats[idx] if a is None else row(a)
        args.append(arr)
        specs.append(const(arr.shape))
    for arr in (hsum, tri):
        args.append(arr)
        specs.append(const(arr.shape))
    return pl.pallas_call(
        functools.partial(_rwkv_kernel, n_seq=n_seq),
        grid=(b // n_seq, s // c),
        in_specs=specs,
        out_specs=pl.BlockSpec((n_seq, c, w), lambda bi, ci: (bi, ci, 0)),
        out_shape=jax.ShapeDtypeStruct((b, s, w), F32),
        scratch_shapes=[pltpu.VMEM((n_seq, w, w), F32), pltpu.VMEM((n_seq, 1, w), F32),
                        pltpu.VMEM((n_seq, 1, w), F32), pltpu.VMEM((n_seq, 1, w), F32),
                        pltpu.VMEM((n_seq, 1, lora), F32), pltpu.VMEM((n_seq, 1, GATE_LORA), F32)],
        compiler_params=_cparams(("parallel", "arbitrary")),
        name="rwkv",
    )(*args)


def _mem_attn_kernel(q_ref, kv_ref, o_ref):
    scale = HEAD_DIM ** -0.5
    lane_head = lax.broadcasted_iota(I32, (1, MEM_W), 1) // HEAD_DIM
    q = q_ref[0] * scale
    mk = kv_ref[0, :, :MEM_W].astype(BF16)
    mv = kv_ref[0, :, MEM_W:].astype(BF16)
    out = jnp.zeros(q.shape, F32)
    for h in range(MEM_HEADS):
        hm = lane_head == h
        s = _dot_nt(jnp.where(hm, q, 0.0).astype(BF16), mk)
        s = s - jnp.max(s, axis=-1, keepdims=True)
        e = jnp.exp(s)
        p = e / jnp.sum(e, axis=-1, keepdims=True)
        out = jnp.where(hm, jnp.dot(p.astype(BF16), mv, preferred_element_type=F32), out)
    o_ref[0] = out


def _mem_attention(p3, mkv3, tq):
    b, s, _ = p3.shape
    m = mkv3.shape[1]
    off = (3 * MOBA_W + RWKV_COLS) // MEM_W
    assert off * MEM_W == 3 * MOBA_W + RWKV_COLS
    return pl.pallas_call(
        _mem_attn_kernel,
        grid=(b, s // tq),
        in_specs=[pl.BlockSpec((1, tq, MEM_W), lambda bi, i: (bi, i, off)),
                  pl.BlockSpec((1, m, 2 * MEM_W), lambda bi, i: (bi, 0, 0))],
        out_specs=pl.BlockSpec((1, tq, MEM_W), lambda bi, i: (bi, i, 0)),
        out_shape=jax.ShapeDtypeStruct((b, s, MEM_W), F32),
        compiler_params=_cparams(("parallel", "parallel")),
        name="mem_attn",
    )(p3, mkv3)


def _out_router_kernel(x_ref, ym_ref, yr_ref, ye_ref, wo1_ref, wo2_ref, wo3_ref, g_ref,
                       wrh_ref, wrl_ref, br_ref, upper_ref, ones_ref,
                       x1_ref, h_ref, idx_ref, rank_ref, wgt_ref, cnt_ref, run_s):
    @pl.when(pl.program_id(0) == 0)
    def _():
        run_s[...] = jnp.zeros_like(run_s)

    x1 = (x_ref[...] + _bdot(ym_ref[...], wo1_ref[...]) + _bdot(yr_ref[...], wo2_ref[...])
          + _bdot(ye_ref[...], wo3_ref[...]))
    x1_ref[...] = x1
    h = _rms(x1, g_ref[...])
    h_ref[...] = _pack_halves(h)
    tm = h.shape[0]
    h_hi, h_lo = _split_bf16(h, 2)
    dot = functools.partial(jnp.dot, preferred_element_type=F32)
    logits = dot(h_hi, wrh_ref[...]) + (dot(h_hi, wrl_ref[...]) + dot(h_lo, wrh_ref[...])) + br_ref[...]
    lg = logits.T[:N_EXPERTS, :]
    e_iota = lax.broadcasted_iota(I32, (N_EXPERTS, tm), 0)

    vals, idxs = [], []
    for _ in range(TOP_K):
        m = jnp.max(lg, axis=0, keepdims=True)
        idx = jnp.min(jnp.where(lg == m, e_iota, N_EXPERTS), axis=0, keepdims=True)
        vals.append(m)
        idxs.append(idx)
        lg = jnp.where(e_iota == idx, NEG_INF, lg)
    exps = [jnp.exp(vk - vals[0]) for vk in vals]
    denom = exps[0] + exps[1] + exps[2] + exps[3]

    chosen = jnp.zeros((N_EXPERTS, tm), F32)
    for idx in idxs:
        chosen = chosen + jnp.where(e_iota == idx, 1.0, 0.0)
    chosen = chosen.astype(BF16)
    run = run_s[...]
    before = dot(chosen, upper_ref[...]) + jnp.concatenate([run] * (tm // LANES), axis=1)
    run_s[...] = run + dot(chosen, ones_ref[...])
    cnt_ref[...] = run_s[...]

    zero_i = jnp.zeros((8 - TOP_K, tm), I32)
    ranks = [jnp.sum(jnp.where(e_iota == idx, before, 0.0), axis=0, keepdims=True).astype(I32)
             for idx in idxs]
    idx_ref[0] = jnp.concatenate(idxs + [zero_i], axis=0)
    rank_ref[0] = jnp.concatenate(ranks + [zero_i], axis=0)
    wrows = jnp.concatenate([e / denom for e in exps] + [jnp.zeros((LANES - TOP_K, tm), F32)], axis=0)
    wgt_ref[...] = wrows.T


def _out_router(x2, ym, yr, ye, w_out, g_ffn, w_router, b_router, tm):
    t, d = x2.shape
    wo = w_out.astype(BF16)
    wo1, wo2, wo3 = wo[:MOBA_W], wo[MOBA_W:MOBA_W + RWKV_W], wo[MOBA_W + RWKV_W:]
    wr = jnp.zeros((d, LANES), F32).at[:, :N_EXPERTS].set(w_router)
    wr_hi = wr.astype(BF16)
    wr_lo = (wr - wr_hi.astype(F32)).astype(BF16)
    br = jnp.full((1, LANES), NEG_INF, F32).at[0, :N_EXPERTS].set(b_router)
    upper = jnp.asarray(np.triu(np.ones((tm, tm), np.float32), 1)).astype(BF16)
    ones = jnp.ones((tm, LANES), BF16)
    tile = lambda n: pl.BlockSpec((tm, n), lambda i: (i, 0))
    slots = pl.BlockSpec((1, 8, tm), lambda i: (i, 0, 0))
    const = lambda a: pl.BlockSpec(a.shape, lambda i: (0,) * a.ndim)
    g2 = g_ffn.reshape(1, d)
    n = t // tm
    return pl.pallas_call(
        _out_router_kernel,
        grid=(n,),
        in_specs=[tile(d), tile(MOBA_W), tile(RWKV_W), tile(MEM_W), const(wo1), const(wo2), const(wo3),
                  const(g2), const(wr_hi), const(wr_lo), const(br), const(upper), const(ones)],
        out_specs=[tile(d), tile(d // 2), slots, slots, tile(LANES),
                   pl.BlockSpec((N_EXPERTS, LANES), lambda i: (0, 0))],
        out_shape=[jax.ShapeDtypeStruct((t, d), F32), jax.ShapeDtypeStruct((t, d // 2), I32),
                   jax.ShapeDtypeStruct((n, 8, tm), I32), jax.ShapeDtypeStruct((n, 8, tm), I32),
                   jax.ShapeDtypeStruct((t, LANES), F32), jax.ShapeDtypeStruct((N_EXPERTS, LANES), F32)],
        scratch_shapes=[pltpu.VMEM((N_EXPERTS, LANES), F32)],
        compiler_params=_cparams(("arbitrary",)),
        name="out_router",
    )(x2, ym, yr, ye, wo1, wo2, wo3, g2, wr_hi, wr_lo, br, upper, ones)


def _sc_mesh():
    return plsc.VectorSubcoreMesh(core_axis_name="c", subcore_axis_name="s",
                                  num_cores=SC_CORES, num_subcores=SC_SUBCORES)


def _sc_split(n_rows):
    n_workers = SC_CORES * SC_SUBCORES
    per_worker = n_rows // n_workers
    assert per_worker * n_workers == n_rows and per_worker % SC_ROWS == 0 and SC_ROWS % 8 == 0
    return per_worker


def _sc_gather_rows(table, idx):
    n_rows = idx.shape[0]
    width = table.shape[1]
    per_worker = _sc_split(n_rows)

    @functools.partial(
        pl.kernel, mesh=_sc_mesh(),
        out_type=jax.ShapeDtypeStruct((n_rows, width), table.dtype),
        scratch_types=[pltpu.VMEM((SC_ROWS,), I32), pltpu.VMEM((SC_ROWS, width), table.dtype),
                       pltpu.SemaphoreType.DMA],
        name="sc_gather",
    )
    def gather(table_hbm, idx_hbm, out_hbm, idx_v, rows_v, sem):
        worker = lax.axis_index("s") * SC_CORES + lax.axis_index("c")
        base = worker * per_worker

        @pl.loop(0, per_worker // SC_ROWS)
        def _(step):
            off = pl.multiple_of(base + step * SC_ROWS, 8)
            pltpu.sync_copy(idx_hbm.at[pl.ds(off, SC_ROWS)], idx_v)
            pltpu.async_copy(table_hbm.at[idx_v], rows_v, sem).wait()
            pltpu.sync_copy(rows_v, out_hbm.at[pl.ds(off, SC_ROWS)])

    return gather(table, idx)


def _sc_scatter_rows(rows, idx, n_out):
    n_idx = idx.shape[0]
    n_src, width = rows.shape
    per_worker = _sc_split(n_idx)
    assert n_src % SC_ROWS == 0

    @functools.partial(
        pl.kernel, mesh=_sc_mesh(),
        out_type=jax.ShapeDtypeStruct((n_out, width), rows.dtype),
        scratch_types=[pltpu.VMEM((SC_ROWS,), I32), pltpu.VMEM((SC_ROWS, width), rows.dtype),
                       pltpu.SemaphoreType.DMA],
        name="sc_scatter",
    )
    def scatter(rows_hbm, idx_hbm, out_hbm, idx_v, rows_v, sem):
        worker = lax.axis_index("s") * SC_CORES + lax.axis_index("c")
        base = worker * per_worker

        @pl.loop(0, per_worker // SC_ROWS)
        def _(step):
            off = pl.multiple_of(base + step * SC_ROWS, 8)
            src = pl.multiple_of(lax.rem(off, n_src), 8)
            pltpu.sync_copy(idx_hbm.at[pl.ds(off, SC_ROWS)], idx_v)
            pltpu.sync_copy(rows_hbm.at[pl.ds(src, SC_ROWS)], rows_v)
            pltpu.async_copy(rows_v, out_hbm.at[idx_v], sem).wait()

    return scatter(rows, idx)


def _experts_kernel(be_ref, bv_ref, nused_ref, next_ref, slot_ref,
                    xs_ref, wgu_hbm, bg_ref, bu_ref, wd_hbm, bd_ref, perm_ref,
                    o_ref, wgu_buf, wd_buf, wg_s, wu_s, wd_s, sem):
    i = pl.program_id(0)
    e = be_ref[i]
    used = i < nused_ref[0]
    changed = ((i == 0) | (e != be_ref[jnp.maximum(i - 1, 0)])) & used
    slot = slot_ref[e]

    def weight_copies(expert, buf_slot):
        return (pltpu.make_async_copy(wgu_hbm.at[expert], wgu_buf.at[buf_slot], sem.at[buf_slot, 0]),
                pltpu.make_async_copy(wd_hbm.at[expert], wd_buf.at[buf_slot], sem.at[buf_slot, 1]))

    @pl.when((i == 0) & used)
    def _():
        for cp in weight_copies(e, slot):
            cp.start()

    @pl.when(changed)
    def _():
        for cp in weight_copies(e, slot):
            cp.wait()
        nxt = next_ref[e]

        @pl.when(nxt >= 0)
        def _():
            for cp in weight_copies(nxt, 1 - slot):
                cp.start()

        half = LANES
        for cblk in range(2 * D_EXPERT // (2 * half)):
            wt = wgu_buf[slot, :, cblk * 2 * half:(cblk + 1) * 2 * half].astype(BF16)
            sep = jnp.dot(wt, perm_ref[...], preferred_element_type=F32).astype(BF16)
            wg_s[:, cblk * half:(cblk + 1) * half] = sep[:, :half]
            wu_s[:, cblk * half:(cblk + 1) * half] = sep[:, half:]
        wd_s[...] = wd_buf[slot].astype(BF16)

    @pl.when(used)
    def _():
        rows = lax.broadcasted_iota(I32, xs_ref.shape, 0)
        xb = _unpack_halves(jnp.where(rows < bv_ref[i], xs_ref[...], 0)).astype(BF16)
        gate = jnp.dot(xb, wg_s[...], preferred_element_type=F32) + bg_ref[0]
        up = jnp.dot(xb, wu_s[...], preferred_element_type=F32) + bu_ref[0]
        gate = jnp.minimum(gate, SWIGLU_LIMIT)
        up = jnp.clip(up, -SWIGLU_LIMIT, SWIGLU_LIMIT)
        glu = gate * _sigmoid(gate * SWIGLU_ALPHA)
        act = ((up + 1.0) * glu).astype(BF16)
        o_ref[...] = _pack_halves(jnp.dot(act, wd_s[...], preferred_element_type=F32) + bd_ref[0])

    @pl.when(jnp.logical_not(used))
    def _():
        o_ref[...] = jnp.zeros_like(o_ref)


def _experts(blk_expert, blk_valid, n_used, next_expert, buf_slot, xs, w_gate_up, b_gate_up, w_down,
             b_down):
    p_rows = xs.shape[0]
    d = 2 * xs.shape[1]
    n_blocks = p_rows // EXPERT_ROWS
    bg = b_gate_up[:, 0::2].reshape(N_EXPERTS, 1, D_EXPERT)
    bu = b_gate_up[:, 1::2].reshape(N_EXPERTS, 1, D_EXPERT)
    bd = b_down.reshape(N_EXPERTS, 1, d)
    perm_np = np.zeros((2 * LANES, 2 * LANES), np.float32)
    perm_np[2 * np.arange(LANES), np.arange(LANES)] = 1.0
    perm_np[2 * np.arange(LANES) + 1, LANES + np.arange(LANES)] = 1.0
    perm = jnp.asarray(perm_np).astype(BF16)
    by_expert = lambda shape: pl.BlockSpec((1,) + shape, lambda i, be, *_: (be[i], 0, 0))
    grid_spec = pltpu.PrefetchScalarGridSpec(
        num_scalar_prefetch=5,
        grid=(n_blocks,),
        in_specs=[pl.BlockSpec((EXPERT_ROWS, d // 2), lambda i, *_: (i, 0)),
                  pl.BlockSpec(memory_space=pl.ANY), by_expert((1, D_EXPERT)), by_expert((1, D_EXPERT)),
                  pl.BlockSpec(memory_space=pl.ANY), by_expert((1, d)),
                  pl.BlockSpec(perm.shape, lambda i, *_: (0, 0))],
        out_specs=pl.BlockSpec((EXPERT_ROWS, d // 2), lambda i, *_: (i, 0)),
        scratch_shapes=[pltpu.VMEM((2, d, 2 * D_EXPERT), F32), pltpu.VMEM((2, D_EXPERT, d), F32),
                        pltpu.VMEM((d, D_EXPERT), BF16), pltpu.VMEM((d, D_EXPERT), BF16),
                        pltpu.VMEM((D_EXPERT, d), BF16), pltpu.SemaphoreType.DMA((2, 2))],
    )
    return pl.pallas_call(
        _experts_kernel,
        grid_spec=grid_spec,
        out_shape=jax.ShapeDtypeStruct((p_rows, d // 2), I32),
        compiler_params=_cparams(("arbitrary",)),
        name="experts",
    )(blk_expert, blk_valid, n_used, next_expert, buf_slot, xs, w_gate_up, bg, bu, w_down, bd, perm)


def _combine_kernel(yg_ref, x1_ref, w_ref, g_ref, o_ref):
    acc = x1_ref[...]
    wts = w_ref[...]
    for kk in range(TOP_K):
        acc = acc + _unpack_halves(yg_ref[kk]) * wts[:, kk:kk + 1]
    o_ref[...] = _rms(acc, g_ref[...])


def _combine(yg, x1, wts, g_final, tb):
    t, d = x1.shape
    half = yg.shape[2]
    return pl.pallas_call(
        _combine_kernel,
        grid=(t // tb,),
        in_specs=[pl.BlockSpec((TOP_K, tb, half), lambda i: (0, i, 0)),
                  pl.BlockSpec((tb, d), lambda i: (i, 0)),
                  pl.BlockSpec((tb, LANES), lambda i: (i, 0)),
                  pl.BlockSpec((1, d), lambda i: (0, 0))],
        out_specs=pl.BlockSpec((tb, d), lambda i: (i, 0)),
        out_shape=jax.ShapeDtypeStruct((t, d), F32),
        compiler_params=_cparams(("parallel",)),
        name="combine",
    )(yg, x1, wts, g_final.reshape(1, d))


def _layer(x, mem, w_in, w_out, w_mem_kv, g_mix, g_mem, g_ffn, bias_tables, mu, w0, w_up, a0, a_up,
           g_up, k_k, k_a, r_k, gn_g, gn_b, w_router, b_router, w_gate_up, b_gate_up, w_down, b_down,
           g_last):
    b, s, d = x.shape
    m = mem.shape[1]
    t = b * s
    x2 = x.reshape(t, d)

    p = _norm_matmul(x2, g_mix, w_in.astype(BF16), 512, "in_proj")
    p3 = p.reshape(b, s, IN_COLS)
    y_moba = _moba_attention(p3, *bias_tables)
    y_rwkv = _rwkv(p3, mu, w0, w_up, a0, a_up, g_up, k_k, k_a, r_k, gn_g, gn_b)
    mkv = _norm_matmul(mem.reshape(b * m, d), g_mem, w_mem_kv.astype(BF16), 512, "mem_kv")
    y_mem = _mem_attention(p3, mkv.reshape(b, m, 2 * MEM_W), 512)

    tb = ROUTE_TOKENS
    x1, h2, idx_o, rank_o, wgt_p, cnt = _out_router(
        x2, y_moba.reshape(t, MOBA_W), y_rwkv.reshape(t, RWKV_W), y_mem.reshape(t, MEM_W),
        w_out, g_ffn, w_router, b_router, tb)

    counts = cnt[:, 0].astype(I32)
    padded = (counts + EXPERT_ROWS - 1) // EXPERT_ROWS * EXPERT_ROWS
    pad_ends = jnp.cumsum(padded)
    pad_starts = (pad_ends - padded).astype(I32)
    n_blocks = (t * TOP_K) // EXPERT_ROWS + N_EXPERTS
    blk_start = jnp.arange(n_blocks, dtype=I32) * EXPERT_ROWS
    blk_expert = jnp.minimum(jnp.sum(blk_start[:, None] >= pad_ends[None, :], axis=1),
                             N_EXPERTS - 1).astype(I32)
    blk_valid = jnp.clip(counts[blk_expert] - (blk_start - pad_starts[blk_expert]), 0, EXPERT_ROWS)
    n_used = (pad_ends[-1:] // EXPERT_ROWS).astype(I32)
    has_rows = counts > 0
    first_from = lax.cummin(jnp.where(has_rows, jnp.arange(N_EXPERTS, dtype=I32), N_EXPERTS), reverse=True)
    next_expert = jnp.concatenate([first_from[1:], jnp.full((1,), N_EXPERTS, I32)])
    next_expert = jnp.where(next_expert < N_EXPERTS, next_expert, -1).astype(I32)
    buf_slot = ((jnp.cumsum(has_rows.astype(I32)) - has_rows.astype(I32)) % 2).astype(I32)
    idx_kt = jnp.swapaxes(idx_o[:, :TOP_K, :], 0, 1).reshape(TOP_K, t)
    rank_kt = jnp.swapaxes(rank_o[:, :TOP_K, :], 0, 1).reshape(TOP_K, t)
    start_kt = jnp.zeros((TOP_K, t), I32)
    for e in range(N_EXPERTS):
        start_kt = jnp.where(idx_kt == e, pad_starts[e], start_kt)
    dest = (start_kt + rank_kt).reshape(TOP_K * t)

    xs = _sc_scatter_rows(h2, dest, n_blocks * EXPERT_ROWS)
    ys = _experts(blk_expert, blk_valid.astype(I32), n_used, next_expert, buf_slot, xs,
                  w_gate_up, b_gate_up, w_down, b_down)
    yg = _sc_gather_rows(ys, dest).reshape(TOP_K, t, d // 2)
    out = _combine(yg, x1, wgt_p, g_last, tb)
    return out.reshape(b, s, d)


def kernel(x, mem, w_in, w_out, w_mem_kv, g_mix, g_mem, g_ffn, g_final, rel_bias, rwkv_mu, rwkv_w0,
           rwkv_w_up, rwkv_a0, rwkv_a_up, rwkv_g_up, rwkv_k_k, rwkv_k_a, rwkv_r_k, rwkv_gn_g, rwkv_gn_b,
           w_router, b_router, w_gate_up, b_gate_up, w_down, b_down):
    depth = w_in.shape[0]
    assert depth == 1, "the final norm is fused into the last layer's combine kernel"
    bias_tables = _moba_bias_tables(rel_bias)
    l = 0
    return _layer(x, mem, w_in[l], w_out[l], w_mem_kv[l], g_mix[l], g_mem[l], g_ffn[l], bias_tables,
                  rwkv_mu[l], rwkv_w0[l], rwkv_w_up[l], rwkv_a0[l], rwkv_a_up[l], rwkv_g_up[l],
                  rwkv_k_k[l], rwkv_k_a[l], rwkv_r_k[l], rwkv_gn_g[l], rwkv_gn_b[l], w_router[l],
                  b_router[l], w_gate_up[l], b_gate_up[l], w_down[l], b_down[l], g_final)
```

```python
import functools
import math

import numpy as np
import jax
import jax.numpy as jnp
from jax import lax
from jax.experimental import pallas as pl
from jax.experimental.pallas import tpu as pltpu
from jax.experimental.pallas import tpu_sc as plsc

F32 = jnp.float32
BF16 = jnp.bfloat16
I32 = jnp.int32
HI = lax.Precision.HIGHEST

D_MODEL = 1024
HEAD_DIM = 64
MOBA_HEADS = 6
RWKV_HEADS = 6
MEM_HEADS = 4
MOBA_W = MOBA_HEADS * HEAD_DIM
RWKV_W = RWKV_HEADS * HEAD_DIM
MEM_W = MEM_HEADS * HEAD_DIM
MOBA_BLOCK = 256
MOBA_TOPK = 3
N_BUCKETS = 32
MAX_DISTANCE = 128
DECAY_LORA = 64
AAA_LORA = 64
GATE_LORA = 128
RWKV_COLS = 3 * RWKV_W + DECAY_LORA + AAA_LORA + GATE_LORA
RWKV_GN_EPS = 64e-5
IN_COLS = 3 * MOBA_W + RWKV_COLS + MEM_W
N_EXPERTS = 32
TOP_K = 4
D_EXPERT = D_MODEL
SWIGLU_ALPHA = 1.702
SWIGLU_LIMIT = 7.0
RMS_EPS = 1e-5

LANES = 128
RWKV_CHUNK = 128
RWKV_SEQS_PER_STEP = 2
EXPERT_ROWS = 256
ROUTE_TOKENS = 256
SC_CORES = 2
SC_SUBCORES = 16
SC_ROWS = 64
SC_INFLIGHT = 2
VMEM_LIMIT = 56 * 1024 * 1024
NEG_INF = float("-inf")
HIGH_HALF = -65536
LOG2_E = 1.4426950408889634
MOBA_VROWS = HEAD_DIM + 16


def _cparams(sem):
    return pltpu.CompilerParams(dimension_semantics=sem, vmem_limit_bytes=VMEM_LIMIT)


def _rms(x, g):
    return x * lax.rsqrt(jnp.mean(x * x, axis=-1, keepdims=True) + RMS_EPS) * g


def _bdot(a, b):
    return jnp.dot(a.astype(BF16), b.astype(BF16), preferred_element_type=F32)


def _hdot(a, b):
    return jnp.dot(a, b, preferred_element_type=F32, precision=HI)


def _dot_nt(a, b):
    return lax.dot_general(a, b, (((1,), (1,)), ((), ())), preferred_element_type=F32)


def _dot_tn(a, b):
    return lax.dot_general(a, b, (((0,), (0,)), ((), ())), preferred_element_type=F32)


def _split_bf16(x, terms):
    parts = []
    for _ in range(terms):
        hi = x.astype(BF16)
        parts.append(hi)
        x = x - hi.astype(F32)
    return parts


def _dot_exact_rhs(x, m_bf16, terms):
    acc = None
    for part in _split_bf16(x, terms):
        d = jnp.dot(part, m_bf16, preferred_element_type=F32)
        acc = d if acc is None else acc + d
    return acc


def _pack_halves(x):
    n = x.shape[1] // 2
    lo = pltpu.bitcast(x[:, :n].astype(BF16).astype(F32), I32)
    hi = pltpu.bitcast(x[:, n:].astype(BF16).astype(F32), I32)
    return (hi & HIGH_HALF) | lax.shift_right_logical(lo, 16)


def _unpack_halves(w):
    lo = pltpu.bitcast(w << 16, F32)
    hi = pltpu.bitcast(w & HIGH_HALF, F32)
    return jnp.concatenate([lo, hi], axis=1)


def _norm_matmul_kernel(x_ref, g_ref, w_ref, o_ref):
    h = _rms(x_ref[...], g_ref[...])
    o_ref[...] = jnp.dot(h.astype(BF16), w_ref[...], preferred_element_type=F32)


def _norm_matmul(x, g, w_bf16, tm, name):
    t, d = x.shape
    n = w_bf16.shape[1]
    tm = min(tm, t)
    return pl.pallas_call(
        _norm_matmul_kernel,
        grid=(t // tm,),
        in_specs=[pl.BlockSpec((tm, d), lambda i: (i, 0)),
                  pl.BlockSpec((1, d), lambda i: (0, 0)),
                  pl.BlockSpec((d, n), lambda i: (0, 0))],
        out_specs=pl.BlockSpec((tm, n), lambda i: (i, 0)),
        out_shape=jax.ShapeDtypeStruct((t, n), F32),
        compiler_params=_cparams(("parallel",)),
        name=name,
    )(x, g.reshape(1, d), w_bf16)


def _t5_bucket_np(dist):
    n = np.maximum(dist, 0)
    max_exact = N_BUCKETS // 2
    nf = np.maximum(n, 1).astype(np.float64)
    large = max_exact + (np.log(nf / max_exact) / math.log(MAX_DISTANCE / max_exact)
                         * (N_BUCKETS - max_exact)).astype(np.int64)
    large = np.minimum(large, N_BUCKETS - 1)
    return np.where(n < max_exact, n, large)


def _moba_bias_tables(rel_bias):
    n = MOBA_BLOCK
    assert np.all(_t5_bucket_np(np.arange(n + 1, 64 * n)) == N_BUCKETS - 1)
    bias_t = rel_bias.astype(F32).T * LOG2_E

    def by_distance(dist):
        bucket = jnp.asarray(_t5_bucket_np(dist), I32)[None]
        tab = jnp.zeros((MOBA_HEADS, dist.shape[0]), F32)
        for b in range(N_BUCKETS):
            tab = jnp.where(bucket == b, bias_t[:, b][:, None], tab)
        return tab

    def toeplitz(g):
        flat = jnp.tile(g, (1, n))[:, :n * (2 * n - 1)]
        return flat.reshape(MOBA_HEADS, n, 2 * n - 1)[:, :, :n]

    d = np.arange(2 * n)
    g_own = jnp.where(jnp.asarray(d < n)[None], by_distance(np.where(d < n, d, 0)), NEG_INF)
    g_prev = by_distance(np.where(d < n, n + d, d - n))
    far = bias_t[:, N_BUCKETS - 1]
    return toeplitz(g_own), toeplitz(g_prev), far


def _moba_kernel(far_ref, q_ref, k_ref, v_ref, t0_ref, t1_ref, o_ref,
                 qt_s, kb_s, vt_s, sc_s, *, nb):
    hp = pl.program_id(1)
    blk = MOBA_BLOCK
    scale = HEAD_DIM ** -0.5 * LOG2_E
    row_head = lax.broadcasted_iota(I32, (LANES, 1), 0) // HEAD_DIM
    lane_head = lax.broadcasted_iota(I32, (1, LANES), 1) // HEAD_DIM
    ones_rows = jnp.ones((MOBA_VROWS - HEAD_DIM, blk), BF16)

    kmean_rows = []
    for j in range(nb):
        sl = pl.ds(j * blk, blk)
        qt_s[j] = (q_ref[0, sl, :] * scale).T
        vt = v_ref[0, sl, :].T.astype(BF16)
        for h in range(2):
            vt_s[j, h * MOBA_VROWS:h * MOBA_VROWS + HEAD_DIM, :] = vt[h * HEAD_DIM:(h + 1) * HEAD_DIM, :]
            vt_s[j, h * MOBA_VROWS + HEAD_DIM:(h + 1) * MOBA_VROWS, :] = ones_rows
        kj = k_ref[0, sl, :]
        kb_s[j] = kj.astype(BF16)
        kmean_rows.append(jnp.mean(kj, axis=0, keepdims=True))
    kmean = jnp.concatenate(kmean_rows, axis=0)
    km2 = jnp.concatenate([jnp.where(lane_head == 0, kmean, 0.0),
                           jnp.where(lane_head == 1, kmean, 0.0)], axis=0)
    blk_iota = lax.broadcasted_iota(I32, (nb, blk), 0)

    for i in range(nb):
        qt = qt_s[i]
        negs = [None, None]
        if i > MOBA_TOPK:
            gate2 = _hdot(km2, qt)
            for h in range(2):
                g = gate2[h * nb:(h + 1) * nb, :]
                cnt = jnp.zeros((nb, blk), I32)
                for m in range(i):
                    gm = g[m:m + 1, :]
                    beats = (gm > g) | ((gm == g) & (m < blk_iota))
                    cnt = cnt + jnp.where(beats, 1, 0)
                negs[h] = jnp.where(cnt < MOBA_TOPK, 0.0, NEG_INF)
        outs = []
        for h in range(2):
            qt_h = jnp.where(row_head == h, qt, 0.0).astype(BF16)
            far_bias = far_ref[2 * hp + h]
            m_run = None
            for j in range(i + 1):
                s = jnp.dot(kb_s[j], qt_h, preferred_element_type=F32)
                if j == i:
                    s = s + t0_ref[h]
                else:
                    bias = t1_ref[h] if j == i - 1 else far_bias
                    if negs[h] is not None:
                        bias = bias + negs[h][j:j + 1, :]
                    s = s + bias
                sc_s[h, j] = s
                cm = jnp.max(s, axis=0, keepdims=True)
                m_run = cm if m_run is None else jnp.maximum(m_run, cm)
            acc = jnp.zeros((MOBA_VROWS, blk), F32)
            for j in range(i + 1):
                p_t = jnp.exp2(sc_s[h, j] - m_run)
                vt_h = vt_s[j, h * MOBA_VROWS:(h + 1) * MOBA_VROWS, :]
                acc = acc + jnp.dot(vt_h, p_t.astype(BF16), preferred_element_type=F32)
            outs.append(acc[:HEAD_DIM] / acc[HEAD_DIM:HEAD_DIM + 1])
        o_t = jnp.concatenate(outs, axis=0)
        o_ref[0, pl.ds(i * blk, blk), :] = o_t.T


def _moba_attention(p3, t_own, t_prev, far):
    b, s, _ = p3.shape
    nb = s // MOBA_BLOCK
    n_pairs = MOBA_HEADS // 2
    blk_spec = lambda off: pl.BlockSpec((1, s, LANES), lambda bi, hp: (bi, 0, off + hp))
    tab_spec = pl.BlockSpec((2, MOBA_BLOCK, MOBA_BLOCK), lambda bi, hp: (hp, 0, 0))
    return pl.pallas_call(
        functools.partial(_moba_kernel, nb=nb),
        grid=(b, n_pairs),
        in_specs=[pl.BlockSpec(memory_space=pltpu.SMEM),
                  blk_spec(0), blk_spec(n_pairs), blk_spec(2 * n_pairs), tab_spec, tab_spec],
        out_specs=pl.BlockSpec((1, s, LANES), lambda bi, hp: (bi, 0, hp)),
        out_shape=jax.ShapeDtypeStruct((b, s, MOBA_W), F32),
        scratch_shapes=[pltpu.VMEM((nb, LANES, MOBA_BLOCK), F32),
                        pltpu.VMEM((nb, MOBA_BLOCK, LANES), BF16),
                        pltpu.VMEM((nb, 2 * MOBA_VROWS, MOBA_BLOCK), BF16),
                        pltpu.VMEM((2, nb, MOBA_BLOCK, MOBA_BLOCK), F32)],
        compiler_params=_cparams(("parallel", "parallel")),
        name="moba",
    )(far, p3, p3, p3, t_own, t_prev)


def _softplus(z):
    return jnp.maximum(z, 0.0) + jnp.log(1.0 + jnp.exp(-jnp.abs(z)))


def _sigmoid(z):
    return 1.0 / (1.0 + jnp.exp(-z))


def _rwkv_kernel(r_ref, k_ref, v_ref, wa_ref, g_ref,
                 mu_r_ref, mu_k_ref, mu_v_ref, mu_wa_ref, mu_g_ref,
                 w0_ref, wup_ref, a0_ref, aup_ref, gup_ref, kk_ref, ka_ref, rk_ref,
                 gng_ref, gnb_ref, hsum_ref, tri_ref,
                 o_ref, st_s, prev_r, prev_k, prev_v, prev_wa, prev_g, *, n_seq):
    c = RWKV_CHUNK
    mid = c // 2

    @pl.when(pl.program_id(1) == 0)
    def _():
        st_s[...] = jnp.zeros_like(st_s)
        prev_r[...] = jnp.zeros_like(prev_r)
        prev_k[...] = jnp.zeros_like(prev_k)
        prev_v[...] = jnp.zeros_like(prev_v)
        prev_wa[...] = jnp.zeros_like(prev_wa)
        prev_g[...] = jnp.zeros_like(prev_g)

    hsum = hsum_ref[...]
    tri = tri_ref[...]
    row = lax.broadcasted_iota(I32, (c, c), 0)
    col = lax.broadcasted_iota(I32, (c, c), 1)
    strict = row > col
    incl = row >= col
    eye = (row == col).astype(F32)
    lane_head = lax.broadcasted_iota(I32, (1, RWKV_W), 1) // HEAD_DIM
    hmask = hsum.astype(F32)
    dot = functools.partial(jnp.dot, preferred_element_type=F32)

    def mix(g, x_ref, prev_ref, mu_ref):
        x = x_ref[g]
        rows = lax.broadcasted_iota(I32, x.shape, 0)
        shifted = jnp.where(rows == 0, prev_ref[g], pltpu.roll(x, 1, axis=0))
        prev_ref[g] = x[c - 1:c, :]
        return x + (shifted - x) * mu_ref[...]

    def prepare(g):
        r = mix(g, r_ref, prev_r, mu_r_ref)
        k = mix(g, k_ref, prev_k, mu_k_ref)
        v = mix(g, v_ref, prev_v, mu_v_ref)
        xwa = mix(g, wa_ref, prev_wa, mu_wa_ref)
        xg = mix(g, g_ref, prev_g, mu_g_ref)

        w = -_softplus(-(w0_ref[...] + _bdot(jnp.tanh(xwa), wup_ref[...]))) - 0.5
        logd = -jnp.exp(w)
        a = _sigmoid(a0_ref[...] + _bdot(xwa, aup_ref[...]))
        gate = _bdot(_sigmoid(xg), gup_ref[...])
        kk = k * kk_ref[...]
        kk = kk / jnp.maximum(jnp.sqrt(_dot_exact_rhs(kk * kk, hsum, 2)), 1e-12)
        k2 = k * (1.0 + (a - 1.0) * ka_ref[...])

        parts = _split_bf16(logd, 3)
        cum = dot(tri, parts[0]) + dot(tri, parts[1]) + dot(tri, parts[2])
        cum_last = cum[c - 1:c, :]
        ref = cum[mid - 1:mid, :]
        rel = cum - ref
        g_in = jnp.exp(rel)
        g_inv = jnp.exp(-rel)
        g_tail = jnp.exp(cum_last - cum)
        a_t = -kk * jnp.exp(rel - logd)
        b_t = kk * a * g_inv
        k_t = k2 * g_inv
        r_t = r * g_in
        b_hat = kk * a * g_tail
        k_hat = k2 * g_tail

        st = st_s[g]
        st_ref = (st * jnp.exp(ref)).astype(BF16)
        p0 = _dot_nt(a_t.astype(BF16), st_ref)
        o0 = _dot_nt(r_t.astype(BF16), st_ref)

        bk = jnp.concatenate([b_t, k_t], axis=0).astype(BF16)
        ar_all = jnp.concatenate([a_t, r_t], axis=0)
        v_bf = v.astype(BF16)
        quads = []
        for h in range(RWKV_HEADS):
            lo = (h // 2) * LANES
            hm = lane_head[:, lo:lo + LANES] == h
            ar = jnp.where(hm, ar_all[:, lo:lo + LANES], 0.0)
            m4 = _dot_nt(ar.astype(BF16), bk[:, lo:lo + LANES])
            quads.append((jnp.where(strict, m4[:c, :c], 0.0),
                          jnp.where(strict, m4[:c, c:], 0.0).astype(BF16),
                          jnp.where(incl, m4[c:, :c], 0.0).astype(BF16),
                          jnp.where(incl, m4[c:, c:], 0.0).astype(BF16)))
        return dict(r=r, k2=k2, v=v, v_bf=v_bf, gate=gate, p0=p0, o0=o0, b_hat=b_hat, k_hat=k_hat,
                    st=st, decay=jnp.exp(cum_last), quads=quads)

    seqs = [prepare(g) for g in range(n_seq)]
    pairs = [(g, h) for g in range(n_seq) for h in range(RWKV_HEADS)]

    pw = {gh: seqs[gh[0]]["quads"][gh[1]][0] for gh in pairs}
    t_inv = {gh: eye + pw[gh] for gh in pairs}
    pw = {gh: pw[gh].astype(BF16) for gh in pairs}
    span = 2
    while span < c:
        pw_f = {gh: dot(pw[gh], pw[gh]) for gh in pairs}
        pw = {gh: pw_f[gh].astype(BF16) for gh in pairs}
        t_inv = {gh: t_inv[gh] + dot(t_inv[gh].astype(BF16), pw[gh]) for gh in pairs}
        span *= 2

    def per_head_rows(x):
        return jnp.concatenate([jnp.where(lane_head == h, x, 0.0) for h in range(RWKV_HEADS)],
                               axis=0).astype(BF16)

    def heads_side_by_side(mats):
        return jnp.concatenate(mats, axis=1)

    for g in range(n_seq):
        sq = seqs[g]
        heads = range(RWKV_HEADS)
        v_rows = per_head_rows(sq["v"])
        l_ak = heads_side_by_side([sq["quads"][h][1] for h in heads])
        m_rb = heads_side_by_side([sq["quads"][h][2] for h in heads])
        m_rk = heads_side_by_side([sq["quads"][h][3] for h in heads])
        t_all = heads_side_by_side([t_inv[(g, h)].astype(BF16) for h in heads])
        w_loc = sq["p0"] + dot(l_ak, v_rows)
        u_all = dot(t_all, per_head_rows(w_loc))
        o = sq["o0"] + dot(m_rb, per_head_rows(u_all)) + dot(m_rk, v_rows)
        upd = (_dot_tn(u_all.astype(BF16), sq["b_hat"].astype(BF16))
               + _dot_tn(sq["v_bf"], sq["k_hat"].astype(BF16)))
        st_s[g] = sq["st"] * sq["decay"] + hmask * upd

        inv_n = 1.0 / HEAD_DIM
        mean = _dot_exact_rhs(o, hsum, 2) * inv_n
        dev = o - mean
        var = _dot_exact_rhs(dev * dev, hsum, 2) * inv_n
        y = dev * lax.rsqrt(var + RWKV_GN_EPS) * gng_ref[...] + gnb_ref[...]
        y = y + _dot_exact_rhs(sq["r"] * sq["k2"] * rk_ref[...], hsum, 2) * sq["v"]
        o_ref[g] = y * sq["gate"]


def _rwkv(p3, mu, w0, w_up, a0, a_up, g_up, k_k, k_a, r_k, gn_g, gn_b):
    b, s, _ = p3.shape
    c = RWKV_CHUNK
    w = RWKV_W
    n_seq = RWKV_SEQS_PER_STEP if b % RWKV_SEQS_PER_STEP == 0 else 1
    base = 3 * MOBA_W
    assert base % w == 0 and (base + 3 * w) % LANES == 0
    lora = DECAY_LORA + AAA_LORA

    def col_spec(width, off_cols):
        assert off_cols % width == 0
        return pl.BlockSpec((n_seq, c, width), lambda bi, ci: (bi, ci, off_cols // width))

    row = lambda x: x.reshape(1, -1).astype(F32)
    const = lambda shape: pl.BlockSpec(shape, lambda bi, ci: (0,) * len(shape))
    wup_pad = jnp.concatenate([w_up, jnp.zeros((AAA_LORA, w), F32)], axis=0).astype(BF16)
    aup_pad = jnp.concatenate([jnp.zeros((DECAY_LORA, w), F32), a_up], axis=0).astype(BF16)
    head = np.arange(w) // HEAD_DIM
    hsum = jnp.asarray((head[:, None] == head[None, :]).astype(np.float32)).astype(BF16)
    tri = jnp.asarray(np.tril(np.ones((c, c), np.float32))).astype(BF16)
    vec_args = [mu[:w], mu[w:2 * w], mu[2 * w:3 * w], mu[3 * w:3 * w + lora], mu[3 * w + lora:],
                w0, None, a0, None, None, k_k, k_a, r_k.reshape(-1), gn_g, gn_b]
    args = [p3, p3, p3, p3, p3]
    specs = [col_spec(w, base), col_spec(w, base + w), col_spec(w, base + 2 * w),
             col_spec(lora, base + 3 * w), col_spec(GATE_LORA, base + 3 * w + lora)]
    mats = {6: wup_pad, 8: aup_pad, 9: g_up.astype(BF16)}
    for idx, a in enumerate(vec_args):
        arr = mats[idx] if a is None else row(a)
        args.append(arr)
        specs.append(const(arr.shape))
    for arr in (hsum, tri):
        args.append(arr)
        specs.append(const(arr.shape))
    return pl.pallas_call(
        functools.partial(_rwkv_kernel, n_seq=n_seq),
        grid=(b // n_seq, s // c),
        in_specs=specs,
        out_specs=pl.BlockSpec((n_seq, c, w), lambda bi, ci: (bi, ci, 0)),
        out_shape=jax.ShapeDtypeStruct((b, s, w), F32),
        scratch_shapes=[pltpu.VMEM((n_seq, w, w), F32), pltpu.VMEM((n_seq, 1, w), F32),
                        pltpu.VMEM((n_seq, 1, w), F32), pltpu.VMEM((n_seq, 1, w), F32),
                        pltpu.VMEM((n_seq, 1, lora), F32), pltpu.VMEM((n_seq, 1, GATE_LORA), F32)],
        compiler_params=_cparams(("parallel", "arbitrary")),
        name="rwkv",
    )(*args)


def _mem_attn_kernel(q_ref, kv_ref, o_ref):
    scale = HEAD_DIM ** -0.5
    lane_head = lax.broadcasted_iota(I32, (1, MEM_W), 1) // HEAD_DIM
    q = q_ref[0] * scale
    mk = kv_ref[0, :, :MEM_W].astype(BF16)
    mv = kv_ref[0, :, MEM_W:].astype(BF16)
    out = jnp.zeros(q.shape, F32)
    for h in range(MEM_HEADS):
        hm = lane_head == h
        s = _dot_nt(jnp.where(hm, q, 0.0).astype(BF16), mk)
        s = s - jnp.max(s, axis=-1, keepdims=True)
        e = jnp.exp(s)
        p = e / jnp.sum(e, axis=-1, keepdims=True)
        out = jnp.where(hm, jnp.dot(p.astype(BF16), mv, preferred_element_type=F32), out)
    o_ref[0] = out


def _mem_attention(p3, mkv3, tq):
    b, s, _ = p3.shape
    m = mkv3.shape[1]
    off = (3 * MOBA_W + RWKV_COLS) // MEM_W
    assert off * MEM_W == 3 * MOBA_W + RWKV_COLS
    return pl.pallas_call(
        _mem_attn_kernel,
        grid=(b, s // tq),
        in_specs=[pl.BlockSpec((1, tq, MEM_W), lambda bi, i: (bi, i, off)),
                  pl.BlockSpec((1, m, 2 * MEM_W), lambda bi, i: (bi, 0, 0))],
        out_specs=pl.BlockSpec((1, tq, MEM_W), lambda bi, i: (bi, i, 0)),
        out_shape=jax.ShapeDtypeStruct((b, s, MEM_W), F32),
        compiler_params=_cparams(("parallel", "parallel")),
        name="mem_attn",
    )(p3, mkv3)


def _out_router_kernel(x_ref, ym_ref, yr_ref, ye_ref, wo1_ref, wo2_ref, wo3_ref, g_ref,
                       wrh_ref, wrl_ref, br_ref, upper_ref, ones_ref,
                       x1_ref, h_ref, idx_ref, rank_ref, wgt_ref, cnt_ref, run_s):
    @pl.when(pl.program_id(0) == 0)
    def _():
        run_s[...] = jnp.zeros_like(run_s)

    x1 = (x_ref[...] + _bdot(ym_ref[...], wo1_ref[...]) + _bdot(yr_ref[...], wo2_ref[...])
          + _bdot(ye_ref[...], wo3_ref[...]))
    x1_ref[...] = x1
    h = _rms(x1, g_ref[...])
    h_ref[...] = _pack_halves(h)
    tm = h.shape[0]
    h_hi, h_lo = _split_bf16(h, 2)
    dot = functools.partial(jnp.dot, preferred_element_type=F32)
    logits = dot(h_hi, wrh_ref[...]) + (dot(h_hi, wrl_ref[...]) + dot(h_lo, wrh_ref[...])) + br_ref[...]
    lg = logits.T[:N_EXPERTS, :]
    e_iota = lax.broadcasted_iota(I32, (N_EXPERTS, tm), 0)

    vals, idxs = [], []
    for _ in range(TOP_K):
        m = jnp.max(lg, axis=0, keepdims=True)
        idx = jnp.min(jnp.where(lg == m, e_iota, N_EXPERTS), axis=0, keepdims=True)
        vals.append(m)
        idxs.append(idx)
        lg = jnp.where(e_iota == idx, NEG_INF, lg)
    exps = [jnp.exp(vk - vals[0]) for vk in vals]
    denom = exps[0] + exps[1] + exps[2] + exps[3]

    chosen = jnp.zeros((N_EXPERTS, tm), F32)
    for idx in idxs:
        chosen = chosen + jnp.where(e_iota == idx, 1.0, 0.0)
    chosen = chosen.astype(BF16)
    run = run_s[...]
    before = dot(chosen, upper_ref[...]) + jnp.concatenate([run] * (tm // LANES), axis=1)
    run_s[...] = run + dot(chosen, ones_ref[...])
    cnt_ref[...] = run_s[...]

    zero_i = jnp.zeros((8 - TOP_K, tm), I32)
    ranks = [jnp.sum(jnp.where(e_iota == idx, before, 0.0), axis=0, keepdims=True).astype(I32)
             for idx in idxs]
    idx_ref[0] = jnp.concatenate(idxs + [zero_i], axis=0)
    rank_ref[0] = jnp.concatenate(ranks + [zero_i], axis=0)
    wrows = jnp.concatenate([e / denom for e in exps] + [jnp.zeros((LANES - TOP_K, tm), F32)], axis=0)
    wgt_ref[...] = wrows.T


def _out_router(x2, ym, yr, ye, w_out, g_ffn, w_router, b_router, tm):
    t, d = x2.shape
    wo = w_out.astype(BF16)
    wo1, wo2, wo3 = wo[:MOBA_W], wo[MOBA_W:MOBA_W + RWKV_W], wo[MOBA_W + RWKV_W:]
    wr = jnp.zeros((d, LANES), F32).at[:, :N_EXPERTS].set(w_router)
    wr_hi = wr.astype(BF16)
    wr_lo = (wr - wr_hi.astype(F32)).astype(BF16)
    br = jnp.full((1, LANES), NEG_INF, F32).at[0, :N_EXPERTS].set(b_router)
    upper = jnp.asarray(np.triu(np.ones((tm, tm), np.float32), 1)).astype(BF16)
    ones = jnp.ones((tm, LANES), BF16)
    tile = lambda n: pl.BlockSpec((tm, n), lambda i: (i, 0))
    slots = pl.BlockSpec((1, 8, tm), lambda i: (i, 0, 0))
    const = lambda a: pl.BlockSpec(a.shape, lambda i: (0,) * a.ndim)
    g2 = g_ffn.reshape(1, d)
    n = t // tm
    return pl.pallas_call(
        _out_router_kernel,
        grid=(n,),
        in_specs=[tile(d), tile(MOBA_W), tile(RWKV_W), tile(MEM_W), const(wo1), const(wo2), const(wo3),
                  const(g2), const(wr_hi), const(wr_lo), const(br), const(upper), const(ones)],
        out_specs=[tile(d), tile(d // 2), slots, slots, tile(LANES),
                   pl.BlockSpec((N_EXPERTS, LANES), lambda i: (0, 0))],
        out_shape=[jax.ShapeDtypeStruct((t, d), F32), jax.ShapeDtypeStruct((t, d // 2), I32),
                   jax.ShapeDtypeStruct((n, 8, tm), I32), jax.ShapeDtypeStruct((n, 8, tm), I32),
                   jax.ShapeDtypeStruct((t, LANES), F32), jax.ShapeDtypeStruct((N_EXPERTS, LANES), F32)],
        scratch_shapes=[pltpu.VMEM((N_EXPERTS, LANES), F32)],
        compiler_params=_cparams(("arbitrary",)),
        name="out_router",
    )(x2, ym, yr, ye, wo1, wo2, wo3, g2, wr_hi, wr_lo, br, upper, ones)


def _sc_mesh():
    return plsc.VectorSubcoreMesh(core_axis_name="c", subcore_axis_name="s",
                                  num_cores=SC_CORES, num_subcores=SC_SUBCORES)


def _sc_split(n_rows):
    n_workers = SC_CORES * SC_SUBCORES
    per_worker = n_rows // n_workers
    assert per_worker * n_workers == n_rows and per_worker % (SC_ROWS * SC_INFLIGHT) == 0
    assert SC_ROWS % 8 == 0
    return per_worker


def _sc_gather_rows(table, idx):
    n_rows = idx.shape[0]
    width = table.shape[1]
    per_worker = _sc_split(n_rows)

    buf = lambda shape, dtype: [pltpu.VMEM(shape, dtype) for _ in range(SC_INFLIGHT)]

    @functools.partial(
        pl.kernel, mesh=_sc_mesh(),
        out_type=jax.ShapeDtypeStruct((n_rows, width), table.dtype),
        scratch_types=[buf((SC_ROWS,), I32), buf((SC_ROWS, width), table.dtype),
                       [pltpu.SemaphoreType.DMA for _ in range(SC_INFLIGHT)]],
        name="sc_gather",
    )
    def gather(table_hbm, idx_hbm, out_hbm, idx_v, rows_v, sems):
        worker = lax.axis_index("s") * SC_CORES + lax.axis_index("c")
        base = worker * per_worker

        @pl.loop(0, per_worker // (SC_ROWS * SC_INFLIGHT))
        def _(step):
            offs = [pl.multiple_of(base + (step * SC_INFLIGHT + b) * SC_ROWS, 8) for b in range(SC_INFLIGHT)]
            copies = []
            for b in range(SC_INFLIGHT):
                pltpu.sync_copy(idx_hbm.at[pl.ds(offs[b], SC_ROWS)], idx_v[b])
                copies.append(pltpu.async_copy(table_hbm.at[idx_v[b]], rows_v[b], sems[b]))
            for b in range(SC_INFLIGHT):
                copies[b].wait()
                pltpu.sync_copy(rows_v[b], out_hbm.at[pl.ds(offs[b], SC_ROWS)])

    return gather(table, idx)


def _sc_scatter_rows(rows, idx, n_out):
    n_idx = idx.shape[0]
    n_src, width = rows.shape
    per_worker = _sc_split(n_idx)
    assert n_src % SC_ROWS == 0

    buf = lambda shape, dtype: [pltpu.VMEM(shape, dtype) for _ in range(SC_INFLIGHT)]

    @functools.partial(
        pl.kernel, mesh=_sc_mesh(),
        out_type=jax.ShapeDtypeStruct((n_out, width), rows.dtype),
        scratch_types=[buf((SC_ROWS,), I32), buf((SC_ROWS, width), rows.dtype),
                       [pltpu.SemaphoreType.DMA for _ in range(SC_INFLIGHT)]],
        name="sc_scatter",
    )
    def scatter(rows_hbm, idx_hbm, out_hbm, idx_v, rows_v, sems):
        worker = lax.axis_index("s") * SC_CORES + lax.axis_index("c")
        base = worker * per_worker

        @pl.loop(0, per_worker // (SC_ROWS * SC_INFLIGHT))
        def _(step):
            copies = []
            for b in range(SC_INFLIGHT):
                off = pl.multiple_of(base + (step * SC_INFLIGHT + b) * SC_ROWS, 8)
                src = pl.multiple_of(lax.rem(off, n_src), 8)
                pltpu.sync_copy(idx_hbm.at[pl.ds(off, SC_ROWS)], idx_v[b])
                pltpu.sync_copy(rows_hbm.at[pl.ds(src, SC_ROWS)], rows_v[b])
                copies.append(pltpu.async_copy(rows_v[b], out_hbm.at[idx_v[b]], sems[b]))
            for cp in copies:
                cp.wait()

    return scatter(rows, idx)


def _experts_kernel(be_ref, bv_ref, nused_ref, next_ref, slot_ref,
                    xs_ref, wgu_hbm, bg_ref, bu_ref, wd_hbm, bd_ref, perm_ref,
                    o_ref, wgu_buf, wd_buf, wg_s, wu_s, wd_s, sem):
    i = pl.program_id(0)
    e = be_ref[i]
    used = i < nused_ref[0]
    changed = ((i == 0) | (e != be_ref[jnp.maximum(i - 1, 0)])) & used
    slot = slot_ref[e]

    def weight_copies(expert, buf_slot):
        return (pltpu.make_async_copy(wgu_hbm.at[expert], wgu_buf.at[buf_slot], sem.at[buf_slot, 0]),
                pltpu.make_async_copy(wd_hbm.at[expert], wd_buf.at[buf_slot], sem.at[buf_slot, 1]))

    @pl.when((i == 0) & used)
    def _():
        for cp in weight_copies(e, slot):
            cp.start()

    @pl.when(changed)
    def _():
        for cp in weight_copies(e, slot):
            cp.wait()
        nxt = next_ref[e]

        @pl.when(nxt >= 0)
        def _():
            for cp in weight_copies(nxt, 1 - slot):
                cp.start()

        half = LANES
        for cblk in range(2 * D_EXPERT // (2 * half)):
            wt = wgu_buf[slot, :, cblk * 2 * half:(cblk + 1) * 2 * half].astype(BF16)
            sep = jnp.dot(wt, perm_ref[...], preferred_element_type=F32).astype(BF16)
            wg_s[:, cblk * half:(cblk + 1) * half] = sep[:, :half]
            wu_s[:, cblk * half:(cblk + 1) * half] = sep[:, half:]
        wd_s[...] = wd_buf[slot].astype(BF16)

    @pl.when(used)
    def _():
        rows = lax.broadcasted_iota(I32, xs_ref.shape, 0)
        xb = _unpack_halves(jnp.where(rows < bv_ref[i], xs_ref[...], 0)).astype(BF16)
        gate = jnp.dot(xb, wg_s[...], preferred_element_type=F32) + bg_ref[0]
        up = jnp.dot(xb, wu_s[...], preferred_element_type=F32) + bu_ref[0]
        gate = jnp.minimum(gate, SWIGLU_LIMIT)
        up = jnp.clip(up, -SWIGLU_LIMIT, SWIGLU_LIMIT)
        glu = gate * _sigmoid(gate * SWIGLU_ALPHA)
        act = ((up + 1.0) * glu).astype(BF16)
        o_ref[...] = _pack_halves(jnp.dot(act, wd_s[...], preferred_element_type=F32) + bd_ref[0])

    @pl.when(jnp.logical_not(used))
    def _():
        o_ref[...] = jnp.zeros_like(o_ref)


def _experts(blk_expert, blk_valid, n_used, next_expert, buf_slot, xs, w_gate_up, b_gate_up, w_down,
             b_down):
    p_rows = xs.shape[0]
    d = 2 * xs.shape[1]
    n_blocks = p_rows // EXPERT_ROWS
    bg = b_gate_up[:, 0::2].reshape(N_EXPERTS, 1, D_EXPERT)
    bu = b_gate_up[:, 1::2].reshape(N_EXPERTS, 1, D_EXPERT)
    bd = b_down.reshape(N_EXPERTS, 1, d)
    perm_np = np.zeros((2 * LANES, 2 * LANES), np.float32)
    perm_np[2 * np.arange(LANES), np.arange(LANES)] = 1.0
    perm_np[2 * np.arange(LANES) + 1, LANES + np.arange(LANES)] = 1.0
    perm = jnp.asarray(perm_np).astype(BF16)
    by_expert = lambda shape: pl.BlockSpec((1,) + shape, lambda i, be, *_: (be[i], 0, 0))
    grid_spec = pltpu.PrefetchScalarGridSpec(
        num_scalar_prefetch=5,
        grid=(n_blocks,),
        in_specs=[pl.BlockSpec((EXPERT_ROWS, d // 2), lambda i, *_: (i, 0)),
                  pl.BlockSpec(memory_space=pl.ANY), by_expert((1, D_EXPERT)), by_expert((1, D_EXPERT)),
                  pl.BlockSpec(memory_space=pl.ANY), by_expert((1, d)),
                  pl.BlockSpec(perm.shape, lambda i, *_: (0, 0))],
        out_specs=pl.BlockSpec((EXPERT_ROWS, d // 2), lambda i, *_: (i, 0)),
        scratch_shapes=[pltpu.VMEM((2, d, 2 * D_EXPERT), F32), pltpu.VMEM((2, D_EXPERT, d), F32),
                        pltpu.VMEM((d, D_EXPERT), BF16), pltpu.VMEM((d, D_EXPERT), BF16),
                        pltpu.VMEM((D_EXPERT, d), BF16), pltpu.SemaphoreType.DMA((2, 2))],
    )
    return pl.pallas_call(
        _experts_kernel,
        grid_spec=grid_spec,
        out_shape=jax.ShapeDtypeStruct((p_rows, d // 2), I32),
        compiler_params=_cparams(("arbitrary",)),
        name="experts",
    )(blk_expert, blk_valid, n_used, next_expert, buf_slot, xs, w_gate_up, bg, bu, w_down, bd, perm)


def _combine_kernel(yg_ref, x1_ref, w_ref, g_ref, o_ref):
    acc = x1_ref[...]
    wts = w_ref[...]
    for kk in range(TOP_K):
        acc = acc + _unpack_halves(yg_ref[kk]) * wts[:, kk:kk + 1]
    o_ref[...] = _rms(acc, g_ref[...])


def _combine(yg, x1, wts, g_final, tb):
    t, d = x1.shape
    half = yg.shape[2]
    return pl.pallas_call(
        _combine_kernel,
        grid=(t // tb,),
        in_specs=[pl.BlockSpec((TOP_K, tb, half), lambda i: (0, i, 0)),
                  pl.BlockSpec((tb, d), lambda i: (i, 0)),
                  pl.BlockSpec((tb, LANES), lambda i: (i, 0)),
                  pl.BlockSpec((1, d), lambda i: (0, 0))],
        out_specs=pl.BlockSpec((tb, d), lambda i: (i, 0)),
        out_shape=jax.ShapeDtypeStruct((t, d), F32),
        compiler_params=_cparams(("parallel",)),
        name="combine",
    )(yg, x1, wts, g_final.reshape(1, d))


def _layer(x, mem, w_in, w_out, w_mem_kv, g_mix, g_mem, g_ffn, bias_tables, mu, w0, w_up, a0, a_up,
           g_up, k_k, k_a, r_k, gn_g, gn_b, w_router, b_router, w_gate_up, b_gate_up, w_down, b_down,
           g_last):
    b, s, d = x.shape
    m = mem.shape[1]
    t = b * s
    x2 = x.reshape(t, d)

    p = _norm_matmul(x2, g_mix, w_in.astype(BF16), 512, "in_proj")
    p3 = p.reshape(b, s, IN_COLS)
    y_moba = _moba_attention(p3, *bias_tables)
    y_rwkv = _rwkv(p3, mu, w0, w_up, a0, a_up, g_up, k_k, k_a, r_k, gn_g, gn_b)
    mkv = _norm_matmul(mem.reshape(b * m, d), g_mem, w_mem_kv.astype(BF16), 512, "mem_kv")
    y_mem = _mem_attention(p3, mkv.reshape(b, m, 2 * MEM_W), 512)

    tb = ROUTE_TOKENS
    x1, h2, idx_o, rank_o, wgt_p, cnt = _out_router(
        x2, y_moba.reshape(t, MOBA_W), y_rwkv.reshape(t, RWKV_W), y_mem.reshape(t, MEM_W),
        w_out, g_ffn, w_router, b_router, tb)

    counts = cnt[:, 0].astype(I32)
    padded = (counts + EXPERT_ROWS - 1) // EXPERT_ROWS * EXPERT_ROWS
    pad_ends = jnp.cumsum(padded)
    pad_starts = (pad_ends - padded).astype(I32)
    n_blocks = (t * TOP_K) // EXPERT_ROWS + N_EXPERTS
    blk_start = jnp.arange(n_blocks, dtype=I32) * EXPERT_ROWS
    blk_expert = jnp.minimum(jnp.sum(blk_start[:, None] >= pad_ends[None, :], axis=1),
                             N_EXPERTS - 1).astype(I32)
    blk_valid = jnp.clip(counts[blk_expert] - (blk_start - pad_starts[blk_expert]), 0, EXPERT_ROWS)
    n_used = (pad_ends[-1:] // EXPERT_ROWS).astype(I32)
    has_rows = counts > 0
    first_from = lax.cummin(jnp.where(has_rows, jnp.arange(N_EXPERTS, dtype=I32), N_EXPERTS), reverse=True)
    next_expert = jnp.concatenate([first_from[1:], jnp.full((1,), N_EXPERTS, I32)])
    next_expert = jnp.where(next_expert < N_EXPERTS, next_expert, -1).astype(I32)
    buf_slot = ((jnp.cumsum(has_rows.astype(I32)) - has_rows.astype(I32)) % 2).astype(I32)
    idx_kt = jnp.swapaxes(idx_o[:, :TOP_K, :], 0, 1).reshape(TOP_K, t)
    rank_kt = jnp.swapaxes(rank_o[:, :TOP_K, :], 0, 1).reshape(TOP_K, t)
    start_kt = jnp.zeros((TOP_K, t), I32)
    for e in range(N_EXPERTS):
        start_kt = jnp.where(idx_kt == e, pad_starts[e], start_kt)
    dest = (start_kt + rank_kt).reshape(TOP_K * t)

    xs = _sc_scatter_rows(h2, dest, n_blocks * EXPERT_ROWS)
    ys = _experts(blk_expert, blk_valid.astype(I32), n_used, next_expert, buf_slot, xs,
                  w_gate_up, b_gate_up, w_down, b_down)
    yg = _sc_gather_rows(ys, dest).reshape(TOP_K, t, d // 2)
    out = _combine(yg, x1, wgt_p, g_last, tb)
    return out.reshape(b, s, d)


def kernel(x, mem, w_in, w_out, w_mem_kv, g_mix, g_mem, g_ffn, g_final, rel_bias, rwkv_mu, rwkv_w0,
           rwkv_w_up, rwkv_a0, rwkv_a_up, rwkv_g_up, rwkv_k_k, rwkv_k_a, rwkv_r_k, rwkv_gn_g, rwkv_gn_b,
           w_router, b_router, w_gate_up, b_gate_up, w_down, b_down):
    depth = w_in.shape[0]
    assert depth == 1, "the final norm is fused into the last layer's combine kernel"
    bias_tables = _moba_bias_tables(rel_bias)
    l = 0
    return _layer(x, mem, w_in[l], w_out[l], w_mem_kv[l], g_mix[l], g_mem[l], g_ffn[l], bias_tables,
                  rwkv_mu[l], rwkv_w0[l], rwkv_w_up[l], rwkv_a0[l], rwkv_a_up[l], rwkv_g_up[l],
                  rwkv_k_k[l], rwkv_k_a[l], rwkv_r_k[l], rwkv_gn_g[l], rwkv_gn_b[l], w_router[l],
                  b_router[l], w_gate_up[l], b_gate_up[l], w_down[l], b_down[l], g_final)
```

```python
import functools
import math

import numpy as np
import jax
import jax.numpy as jnp
from jax import lax
from jax.experimental import pallas as pl
from jax.experimental.pallas import tpu as pltpu
from jax.experimental.pallas import tpu_sc as plsc

F32 = jnp.float32
BF16 = jnp.bfloat16
I32 = jnp.int32
HI = lax.Precision.HIGHEST

D_MODEL = 1024
HEAD_DIM = 64
MOBA_HEADS = 6
RWKV_HEADS = 6
MEM_HEADS = 4
MOBA_W = MOBA_HEADS * HEAD_DIM
RWKV_W = RWKV_HEADS * HEAD_DIM
MEM_W = MEM_HEADS * HEAD_DIM
MOBA_BLOCK = 256
MOBA_TOPK = 3
N_BUCKETS = 32
MAX_DISTANCE = 128
DECAY_LORA = 64
AAA_LORA = 64
GATE_LORA = 128
RWKV_COLS = 3 * RWKV_W + DECAY_LORA + AAA_LORA + GATE_LORA
RWKV_GN_EPS = 64e-5
IN_COLS = 3 * MOBA_W + RWKV_COLS + MEM_W
N_EXPERTS = 32
TOP_K = 4
D_EXPERT = D_MODEL
SWIGLU_ALPHA = 1.702
SWIGLU_LIMIT = 7.0
RMS_EPS = 1e-5

LANES = 128
RWKV_CHUNK = 128
RWKV_SEQS_PER_STEP = 4
EXPERT_ROWS = 256
ROUTE_TOKENS = 256
SC_CORES = 2
SC_SUBCORES = 16
SC_ROWS = 64
SC_INFLIGHT = 2
VMEM_LIMIT = 56 * 1024 * 1024
NEG_INF = float("-inf")
HIGH_HALF = -65536
LOG2_E = 1.4426950408889634
MOBA_VROWS = HEAD_DIM + 16


def _cparams(sem):
    return pltpu.CompilerParams(dimension_semantics=sem, vmem_limit_bytes=VMEM_LIMIT)


def _rms(x, g):
    return x * lax.rsqrt(jnp.mean(x * x, axis=-1, keepdims=True) + RMS_EPS) * g


def _bdot(a, b):
    return jnp.dot(a.astype(BF16), b.astype(BF16), preferred_element_type=F32)


def _hdot(a, b):
    return jnp.dot(a, b, preferred_element_type=F32, precision=HI)


def _dot_nt(a, b):
    return lax.dot_general(a, b, (((1,), (1,)), ((), ())), preferred_element_type=F32)


def _dot_tn(a, b):
    return lax.dot_general(a, b, (((0,), (0,)), ((), ())), preferred_element_type=F32)


def _split_bf16(x, terms):
    parts = []
    for _ in range(terms):
        hi = x.astype(BF16)
        parts.append(hi)
        x = x - hi.astype(F32)
    return parts


def _dot_exact_rhs(x, m_bf16, terms):
    acc = None
    for part in _split_bf16(x, terms):
        d = jnp.dot(part, m_bf16, preferred_element_type=F32)
        acc = d if acc is None else acc + d
    return acc


def _pack_halves(x):
    n = x.shape[1] // 2
    lo = pltpu.bitcast(x[:, :n].astype(BF16).astype(F32), I32)
    hi = pltpu.bitcast(x[:, n:].astype(BF16).astype(F32), I32)
    return (hi & HIGH_HALF) | lax.shift_right_logical(lo, 16)


def _unpack_halves(w):
    lo = pltpu.bitcast(w << 16, F32)
    hi = pltpu.bitcast(w & HIGH_HALF, F32)
    return jnp.concatenate([lo, hi], axis=1)


def _norm_matmul_kernel(x_ref, g_ref, w_ref, o_ref):
    h = _rms(x_ref[...], g_ref[...])
    o_ref[...] = jnp.dot(h.astype(BF16), w_ref[...], preferred_element_type=F32)


def _norm_matmul(x, g, w_bf16, tm, name):
    t, d = x.shape
    n = w_bf16.shape[1]
    tm = min(tm, t)
    return pl.pallas_call(
        _norm_matmul_kernel,
        grid=(t // tm,),
        in_specs=[pl.BlockSpec((tm, d), lambda i: (i, 0)),
                  pl.BlockSpec((1, d), lambda i: (0, 0)),
                  pl.BlockSpec((d, n), lambda i: (0, 0))],
        out_specs=pl.BlockSpec((tm, n), lambda i: (i, 0)),
        out_shape=jax.ShapeDtypeStruct((t, n), F32),
        compiler_params=_cparams(("parallel",)),
        name=name,
    )(x, g.reshape(1, d), w_bf16)


def _t5_bucket_np(dist):
    n = np.maximum(dist, 0)
    max_exact = N_BUCKETS // 2
    nf = np.maximum(n, 1).astype(np.float64)
    large = max_exact + (np.log(nf / max_exact) / math.log(MAX_DISTANCE / max_exact)
                         * (N_BUCKETS - max_exact)).astype(np.int64)
    large = np.minimum(large, N_BUCKETS - 1)
    return np.where(n < max_exact, n, large)


def _moba_bias_tables(rel_bias):
    n = MOBA_BLOCK
    assert np.all(_t5_bucket_np(np.arange(n + 1, 64 * n)) == N_BUCKETS - 1)
    bias_t = rel_bias.astype(F32).T * LOG2_E

    def by_distance(dist):
        bucket = jnp.asarray(_t5_bucket_np(dist), I32)[None]
        tab = jnp.zeros((MOBA_HEADS, dist.shape[0]), F32)
        for b in range(N_BUCKETS):
            tab = jnp.where(bucket == b, bias_t[:, b][:, None], tab)
        return tab

    def toeplitz(g):
        flat = jnp.tile(g, (1, n))[:, :n * (2 * n - 1)]
        return flat.reshape(MOBA_HEADS, n, 2 * n - 1)[:, :, :n]

    d = np.arange(2 * n)
    g_own = jnp.where(jnp.asarray(d < n)[None], by_distance(np.where(d < n, d, 0)), NEG_INF)
    g_prev = by_distance(np.where(d < n, n + d, d - n))
    far = bias_t[:, N_BUCKETS - 1]
    return toeplitz(g_own), toeplitz(g_prev), far


def _moba_kernel(far_ref, q_ref, k_ref, v_ref, t0_ref, t1_ref, o_ref,
                 qt_s, kb_s, vt_s, sc_s, *, nb):
    hp = pl.program_id(1)
    blk = MOBA_BLOCK
    scale = HEAD_DIM ** -0.5 * LOG2_E
    row_head = lax.broadcasted_iota(I32, (LANES, 1), 0) // HEAD_DIM
    lane_head = lax.broadcasted_iota(I32, (1, LANES), 1) // HEAD_DIM
    ones_rows = jnp.ones((MOBA_VROWS - HEAD_DIM, blk), BF16)

    kmean_rows = []
    for j in range(nb):
        sl = pl.ds(j * blk, blk)
        qt_s[j] = (q_ref[0, sl, :] * scale).T
        vt = v_ref[0, sl, :].T.astype(BF16)
        for h in range(2):
            vt_s[j, h * MOBA_VROWS:h * MOBA_VROWS + HEAD_DIM, :] = vt[h * HEAD_DIM:(h + 1) * HEAD_DIM, :]
            vt_s[j, h * MOBA_VROWS + HEAD_DIM:(h + 1) * MOBA_VROWS, :] = ones_rows
        kj = k_ref[0, sl, :]
        kb_s[j] = kj.astype(BF16)
        kmean_rows.append(jnp.mean(kj, axis=0, keepdims=True))
    kmean = jnp.concatenate(kmean_rows, axis=0)
    km2 = jnp.concatenate([jnp.where(lane_head == 0, kmean, 0.0),
                           jnp.where(lane_head == 1, kmean, 0.0)], axis=0)
    blk_iota = lax.broadcasted_iota(I32, (nb, blk), 0)

    for i in range(nb):
        qt = qt_s[i]
        negs = [None, None]
        if i > MOBA_TOPK:
            gate2 = _hdot(km2, qt)
            for h in range(2):
                g = gate2[h * nb:(h + 1) * nb, :]
                cnt = jnp.zeros((nb, blk), I32)
                for m in range(i):
                    gm = g[m:m + 1, :]
                    beats = (gm > g) | ((gm == g) & (m < blk_iota))
                    cnt = cnt + jnp.where(beats, 1, 0)
                negs[h] = jnp.where(cnt < MOBA_TOPK, 0.0, NEG_INF)
        outs = []
        for h in range(2):
            qt_h = jnp.where(row_head == h, qt, 0.0).astype(BF16)
            far_bias = far_ref[2 * hp + h]
            m_run = None
            for j in range(i + 1):
                s = jnp.dot(kb_s[j], qt_h, preferred_element_type=F32)
                if j == i:
                    s = s + t0_ref[h]
                else:
                    bias = t1_ref[h] if j == i - 1 else far_bias
                    if negs[h] is not None:
                        bias = bias + negs[h][j:j + 1, :]
                    s = s + bias
                sc_s[h, j] = s
                cm = jnp.max(s, axis=0, keepdims=True)
                m_run = cm if m_run is None else jnp.maximum(m_run, cm)
            acc = jnp.zeros((MOBA_VROWS, blk), F32)
            for j in range(i + 1):
                p_t = jnp.exp2(sc_s[h, j] - m_run)
                vt_h = vt_s[j, h * MOBA_VROWS:(h + 1) * MOBA_VROWS, :]
                acc = acc + jnp.dot(vt_h, p_t.astype(BF16), preferred_element_type=F32)
            outs.append(acc[:HEAD_DIM] / acc[HEAD_DIM:HEAD_DIM + 1])
        o_t = jnp.concatenate(outs, axis=0)
        o_ref[0, pl.ds(i * blk, blk), :] = o_t.T


def _moba_attention(p3, t_own, t_prev, far):
    b, s, _ = p3.shape
    nb = s // MOBA_BLOCK
    n_pairs = MOBA_HEADS // 2
    blk_spec = lambda off: pl.BlockSpec((1, s, LANES), lambda bi, hp: (bi, 0, off + hp))
    tab_spec = pl.BlockSpec((2, MOBA_BLOCK, MOBA_BLOCK), lambda bi, hp: (hp, 0, 0))
    return pl.pallas_call(
        functools.partial(_moba_kernel, nb=nb),
        grid=(b, n_pairs),
        in_specs=[pl.BlockSpec(memory_space=pltpu.SMEM),
                  blk_spec(0), blk_spec(n_pairs), blk_spec(2 * n_pairs), tab_spec, tab_spec],
        out_specs=pl.BlockSpec((1, s, LANES), lambda bi, hp: (bi, 0, hp)),
        out_shape=jax.ShapeDtypeStruct((b, s, MOBA_W), F32),
        scratch_shapes=[pltpu.VMEM((nb, LANES, MOBA_BLOCK), F32),
                        pltpu.VMEM((nb, MOBA_BLOCK, LANES), BF16),
                        pltpu.VMEM((nb, 2 * MOBA_VROWS, MOBA_BLOCK), BF16),
                        pltpu.VMEM((2, nb, MOBA_BLOCK, MOBA_BLOCK), F32)],
        compiler_params=_cparams(("parallel", "parallel")),
        name="moba",
    )(far, p3, p3, p3, t_own, t_prev)


def _softplus(z):
    return jnp.maximum(z, 0.0) + jnp.log(1.0 + jnp.exp(-jnp.abs(z)))


def _sigmoid(z):
    return 1.0 / (1.0 + jnp.exp(-z))


def _rwkv_kernel(r_ref, k_ref, v_ref, wa_ref, g_ref,
                 mu_r_ref, mu_k_ref, mu_v_ref, mu_wa_ref, mu_g_ref,
                 w0_ref, wup_ref, a0_ref, aup_ref, gup_ref, kk_ref, ka_ref, rk_ref,
                 gng_ref, gnb_ref, hsum_ref, tri_ref,
                 o_ref, st_s, prev_r, prev_k, prev_v, prev_wa, prev_g, *, n_seq):
    c = RWKV_CHUNK
    mid = c // 2

    @pl.when(pl.program_id(1) == 0)
    def _():
        st_s[...] = jnp.zeros_like(st_s)
        prev_r[...] = jnp.zeros_like(prev_r)
        prev_k[...] = jnp.zeros_like(prev_k)
        prev_v[...] = jnp.zeros_like(prev_v)
        prev_wa[...] = jnp.zeros_like(prev_wa)
        prev_g[...] = jnp.zeros_like(prev_g)

    n_pairs = RWKV_HEADS // 2
    hsum = hsum_ref[...]
    tri = tri_ref[...]
    row = lax.broadcasted_iota(I32, (c, c), 0)
    col = lax.broadcasted_iota(I32, (c, c), 1)
    strict = row > col
    incl = row >= col
    eye = (row == col).astype(F32)
    lane_half = lax.broadcasted_iota(I32, (1, LANES), 1) // HEAD_DIM
    hmask = hsum.astype(F32)
    dot = functools.partial(jnp.dot, preferred_element_type=F32)
    lanes = lambda x, jp: x[:, jp * LANES:(jp + 1) * LANES]

    def group_sum(x):
        return jnp.concatenate([_dot_exact_rhs(lanes(x, jp), hsum, 2) for jp in range(n_pairs)], axis=1)

    def two_heads(x):
        return jnp.concatenate([jnp.where(lane_half == 0, x, 0.0), jnp.where(lane_half == 1, x, 0.0)],
                               axis=0).astype(BF16)

    def side(a, b):
        return jnp.concatenate([a, b], axis=1)

    def mix(g, x_ref, prev_ref, mu_ref):
        x = x_ref[g]
        rows = lax.broadcasted_iota(I32, x.shape, 0)
        shifted = jnp.where(rows == 0, prev_ref[g], pltpu.roll(x, 1, axis=0))
        prev_ref[g] = x[c - 1:c, :]
        return x + (shifted - x) * mu_ref[...]

    def prepare(g):
        r = mix(g, r_ref, prev_r, mu_r_ref)
        k = mix(g, k_ref, prev_k, mu_k_ref)
        v = mix(g, v_ref, prev_v, mu_v_ref)
        xwa = mix(g, wa_ref, prev_wa, mu_wa_ref)
        xg = mix(g, g_ref, prev_g, mu_g_ref)

        w = -_softplus(-(w0_ref[...] + _bdot(jnp.tanh(xwa), wup_ref[...]))) - 0.5
        logd = -jnp.exp(w)
        a = _sigmoid(a0_ref[...] + _bdot(xwa, aup_ref[...]))
        gate = _bdot(_sigmoid(xg), gup_ref[...])
        kk = k * kk_ref[...]
        kk = kk / jnp.maximum(jnp.sqrt(group_sum(kk * kk)), 1e-12)
        k2 = k * (1.0 + (a - 1.0) * ka_ref[...])

        parts = _split_bf16(logd, 3)
        cum = dot(tri, parts[0]) + dot(tri, parts[1]) + dot(tri, parts[2])
        cum_last = cum[c - 1:c, :]
        ref = cum[mid - 1:mid, :]
        rel = cum - ref
        g_in = jnp.exp(rel)
        g_inv = jnp.exp(-rel)
        g_tail = jnp.exp(cum_last - cum)
        a_t = -kk * jnp.exp(rel - logd)
        b_t = kk * a * g_inv
        k_t = k2 * g_inv
        r_t = r * g_in
        b_hat = kk * a * g_tail
        k_hat = k2 * g_tail

        ref_scale = jnp.exp(ref)
        decay = jnp.exp(cum_last)
        bk = jnp.concatenate([b_t, k_t], axis=0).astype(BF16)
        ar_all = jnp.concatenate([a_t, r_t], axis=0)
        per_pair = []
        for jp in range(n_pairs):
            st = st_s[g, jp]
            st_ref = (st * lanes(ref_scale, jp)).astype(BF16)
            both = _dot_nt(lanes(ar_all, jp).astype(BF16), st_ref)
            quads = []
            for half in range(2):
                ar = jnp.where(lane_half == half, lanes(ar_all, jp), 0.0)
                m4 = _dot_nt(ar.astype(BF16), lanes(bk, jp))
                quads.append((jnp.where(strict, m4[:c, :c], 0.0),
                              jnp.where(strict, m4[:c, c:], 0.0).astype(BF16),
                              jnp.where(incl, m4[c:, :c], 0.0).astype(BF16),
                              jnp.where(incl, m4[c:, c:], 0.0).astype(BF16)))
            per_pair.append(dict(
                st=st, p0=both[:c], o0=both[c:], v_rows=two_heads(lanes(v, jp)),
                l_ab=[q[0] for q in quads], l_ak=side(quads[0][1], quads[1][1]),
                m_rb=side(quads[0][2], quads[1][2]), m_rk=side(quads[0][3], quads[1][3]),
                v=lanes(v, jp).astype(BF16), b_hat=lanes(b_hat, jp).astype(BF16),
                k_hat=lanes(k_hat, jp).astype(BF16), decay=lanes(decay, jp)))
        return dict(r=r, k2=k2, v=v, gate=gate, pairs=per_pair)

    seqs = [prepare(g) for g in range(n_seq)]
    heads = [(g, jp, half) for g in range(n_seq) for jp in range(n_pairs) for half in range(2)]

    pw = {gh: seqs[gh[0]]["pairs"][gh[1]]["l_ab"][gh[2]] for gh in heads}
    t_inv = {gh: eye + pw[gh] for gh in heads}
    pw = {gh: pw[gh].astype(BF16) for gh in heads}
    span = 2
    while span < c:
        pw = {gh: dot(pw[gh], pw[gh]).astype(BF16) for gh in heads}
        t_inv = {gh: t_inv[gh] + dot(t_inv[gh].astype(BF16), pw[gh]) for gh in heads}
        span *= 2

    for g in range(n_seq):
        sq = seqs[g]
        outs = []
        for jp in range(n_pairs):
            pr = sq["pairs"][jp]
            t_pair = side(t_inv[(g, jp, 0)], t_inv[(g, jp, 1)]).astype(BF16)
            w_loc = pr["p0"] + dot(pr["l_ak"], pr["v_rows"])
            u = dot(t_pair, two_heads(w_loc))
            outs.append(pr["o0"] + dot(pr["m_rb"], two_heads(u)) + dot(pr["m_rk"], pr["v_rows"]))
            upd = _dot_tn(u.astype(BF16), pr["b_hat"]) + _dot_tn(pr["v"], pr["k_hat"])
            st_s[g, jp] = pr["st"] * pr["decay"] + hmask * upd
        o = jnp.concatenate(outs, axis=1)

        inv_n = 1.0 / HEAD_DIM
        mean = group_sum(o) * inv_n
        dev = o - mean
        var = group_sum(dev * dev) * inv_n
        y = dev * lax.rsqrt(var + RWKV_GN_EPS) * gng_ref[...] + gnb_ref[...]
        y = y + group_sum(sq["r"] * sq["k2"] * rk_ref[...]) * sq["v"]
        o_ref[g] = y * sq["gate"]


def _rwkv(p3, mu, w0, w_up, a0, a_up, g_up, k_k, k_a, r_k, gn_g, gn_b):
    b, s, _ = p3.shape
    c = RWKV_CHUNK
    w = RWKV_W
    n_seq = RWKV_SEQS_PER_STEP if b % RWKV_SEQS_PER_STEP == 0 else 1
    base = 3 * MOBA_W
    assert base % w == 0 and (base + 3 * w) % LANES == 0
    lora = DECAY_LORA + AAA_LORA

    def col_spec(width, off_cols):
        assert off_cols % width == 0
        return pl.BlockSpec((n_seq, c, width), lambda bi, ci: (bi, ci, off_cols // width))

    row = lambda x: x.reshape(1, -1).astype(F32)
    const = lambda shape: pl.BlockSpec(shape, lambda bi, ci: (0,) * len(shape))
    wup_pad = jnp.concatenate([w_up, jnp.zeros((AAA_LORA, w), F32)], axis=0).astype(BF16)
    aup_pad = jnp.concatenate([jnp.zeros((DECAY_LORA, w), F32), a_up], axis=0).astype(BF16)
    head = np.arange(LANES) // HEAD_DIM
    hsum = jnp.asarray((head[:, None] == head[None, :]).astype(np.float32)).astype(BF16)
    tri = jnp.asarray(np.tril(np.ones((c, c), np.float32))).astype(BF16)
    vec_args = [mu[:w], mu[w:2 * w], mu[2 * w:3 * w], mu[3 * w:3 * w + lora], mu[3 * w + lora:],
                w0, None, a0, None, None, k_k, k_a, r_k.reshape(-1), gn_g, gn_b]
    args = [p3, p3, p3, p3, p3]
    specs = [col_spec(w, base), col_spec(w, base + w), col_spec(w, base + 2 * w),
             col_spec(lora, base + 3 * w), col_spec(GATE_LORA, base + 3 * w + lora)]
    mats = {6: wup_pad, 8: aup_pad, 9: g_up.astype(BF16)}
    for idx, a in enumerate(vec_args):
        arr = mats[idx] if a is None else row(a)
        args.append(arr)
        specs.append(const(arr.shape))
    for arr in (hsum, tri):
        args.append(arr)
        specs.append(const(arr.shape))
    return pl.pallas_call(
        functools.partial(_rwkv_kernel, n_seq=n_seq),
        grid=(b // n_seq, s // c),
        in_specs=specs,
        out_specs=pl.BlockSpec((n_seq, c, w), lambda bi, ci: (bi, ci, 0)),
        out_shape=jax.ShapeDtypeStruct((b, s, w), F32),
        scratch_shapes=[pltpu.VMEM((n_seq, RWKV_HEADS // 2, LANES, LANES), F32),
                        pltpu.VMEM((n_seq, 1, w), F32),
                        pltpu.VMEM((n_seq, 1, w), F32), pltpu.VMEM((n_seq, 1, w), F32),
                        pltpu.VMEM((n_seq, 1, lora), F32), pltpu.VMEM((n_seq, 1, GATE_LORA), F32)],
        compiler_params=_cparams(("parallel", "arbitrary")),
        name="rwkv",
    )(*args)


def _mem_attn_kernel(q_ref, kv_ref, o_ref):
    scale = HEAD_DIM ** -0.5
    lane_head = lax.broadcasted_iota(I32, (1, MEM_W), 1) // HEAD_DIM
    q = q_ref[0] * scale
    mk = kv_ref[0, :, :MEM_W].astype(BF16)
    mv = kv_ref[0, :, MEM_W:].astype(BF16)
    out = jnp.zeros(q.shape, F32)
    for h in range(MEM_HEADS):
        hm = lane_head == h
        s = _dot_nt(jnp.where(hm, q, 0.0).astype(BF16), mk)
        s = s - jnp.max(s, axis=-1, keepdims=True)
        e = jnp.exp(s)
        p = e / jnp.sum(e, axis=-1, keepdims=True)
        out = jnp.where(hm, jnp.dot(p.astype(BF16), mv, preferred_element_type=F32), out)
    o_ref[0] = out


def _mem_attention(p3, mkv3, tq):
    b, s, _ = p3.shape
    m = mkv3.shape[1]
    off = (3 * MOBA_W + RWKV_COLS) // MEM_W
    assert off * MEM_W == 3 * MOBA_W + RWKV_COLS
    return pl.pallas_call(
        _mem_attn_kernel,
        grid=(b, s // tq),
        in_specs=[pl.BlockSpec((1, tq, MEM_W), lambda bi, i: (bi, i, off)),
                  pl.BlockSpec((1, m, 2 * MEM_W), lambda bi, i: (bi, 0, 0))],
        out_specs=pl.BlockSpec((1, tq, MEM_W), lambda bi, i: (bi, i, 0)),
        out_shape=jax.ShapeDtypeStruct((b, s, MEM_W), F32),
        compiler_params=_cparams(("parallel", "parallel")),
        name="mem_attn",
    )(p3, mkv3)


def _out_router_kernel(x_ref, ym_ref, yr_ref, ye_ref, wo1_ref, wo2_ref, wo3_ref, g_ref,
                       wrh_ref, wrl_ref, br_ref, upper_ref, ones_ref,
                       x1_ref, h_ref, idx_ref, rank_ref, wgt_ref, cnt_ref, run_s):
    @pl.when(pl.program_id(0) == 0)
    def _():
        run_s[...] = jnp.zeros_like(run_s)

    x1 = (x_ref[...] + _bdot(ym_ref[...], wo1_ref[...]) + _bdot(yr_ref[...], wo2_ref[...])
          + _bdot(ye_ref[...], wo3_ref[...]))
    x1_ref[...] = x1
    h = _rms(x1, g_ref[...])
    h_ref[...] = _pack_halves(h)
    tm = h.shape[0]
    h_hi, h_lo = _split_bf16(h, 2)
    dot = functools.partial(jnp.dot, preferred_element_type=F32)
    logits = dot(h_hi, wrh_ref[...]) + (dot(h_hi, wrl_ref[...]) + dot(h_lo, wrh_ref[...])) + br_ref[...]
    lg = logits.T[:N_EXPERTS, :]
    e_iota = lax.broadcasted_iota(I32, (N_EXPERTS, tm), 0)

    vals, idxs = [], []
    for _ in range(TOP_K):
        m = jnp.max(lg, axis=0, keepdims=True)
        idx = jnp.min(jnp.where(lg == m, e_iota, N_EXPERTS), axis=0, keepdims=True)
        vals.append(m)
        idxs.append(idx)
        lg = jnp.where(e_iota == idx, NEG_INF, lg)
    exps = [jnp.exp(vk - vals[0]) for vk in vals]
    denom = exps[0] + exps[1] + exps[2] + exps[3]

    chosen = jnp.zeros((N_EXPERTS, tm), F32)
    for idx in idxs:
        chosen = chosen + jnp.where(e_iota == idx, 1.0, 0.0)
    chosen = chosen.astype(BF16)
    run = run_s[...]
    before = dot(chosen, upper_ref[...]) + jnp.concatenate([run] * (tm // LANES), axis=1)
    run_s[...] = run + dot(chosen, ones_ref[...])
    cnt_ref[...] = run_s[...]

    zero_i = jnp.zeros((8 - TOP_K, tm), I32)
    ranks = [jnp.sum(jnp.where(e_iota == idx, before, 0.0), axis=0, keepdims=True).astype(I32)
             for idx in idxs]
    idx_ref[0] = jnp.concatenate(idxs + [zero_i], axis=0)
    rank_ref[0] = jnp.concatenate(ranks + [zero_i], axis=0)
    wrows = jnp.concatenate([e / denom for e in exps] + [jnp.zeros((LANES - TOP_K, tm), F32)], axis=0)
    wgt_ref[...] = wrows.T


def _out_router(x2, ym, yr, ye, w_out, g_ffn, w_router, b_router, tm):
    t, d = x2.shape
    wo = w_out.astype(BF16)
    wo1, wo2, wo3 = wo[:MOBA_W], wo[MOBA_W:MOBA_W + RWKV_W], wo[MOBA_W + RWKV_W:]
    wr = jnp.zeros((d, LANES), F32).at[:, :N_EXPERTS].set(w_router)
    wr_hi = wr.astype(BF16)
    wr_lo = (wr - wr_hi.astype(F32)).astype(BF16)
    br = jnp.full((1, LANES), NEG_INF, F32).at[0, :N_EXPERTS].set(b_router)
    upper = jnp.asarray(np.triu(np.ones((tm, tm), np.float32), 1)).astype(BF16)
    ones = jnp.ones((tm, LANES), BF16)
    tile = lambda n: pl.BlockSpec((tm, n), lambda i: (i, 0))
    slots = pl.BlockSpec((1, 8, tm), lambda i: (i, 0, 0))
    const = lambda a: pl.BlockSpec(a.shape, lambda i: (0,) * a.ndim)
    g2 = g_ffn.reshape(1, d)
    n = t // tm
    return pl.pallas_call(
        _out_router_kernel,
        grid=(n,),
        in_specs=[tile(d), tile(MOBA_W), tile(RWKV_W), tile(MEM_W), const(wo1), const(wo2), const(wo3),
                  const(g2), const(wr_hi), const(wr_lo), const(br), const(upper), const(ones)],
        out_specs=[tile(d), tile(d // 2), slots, slots, tile(LANES),
                   pl.BlockSpec((N_EXPERTS, LANES), lambda i: (0, 0))],
        out_shape=[jax.ShapeDtypeStruct((t, d), F32), jax.ShapeDtypeStruct((t, d // 2), I32),
                   jax.ShapeDtypeStruct((n, 8, tm), I32), jax.ShapeDtypeStruct((n, 8, tm), I32),
                   jax.ShapeDtypeStruct((t, LANES), F32), jax.ShapeDtypeStruct((N_EXPERTS, LANES), F32)],
        scratch_shapes=[pltpu.VMEM((N_EXPERTS, LANES), F32)],
        compiler_params=_cparams(("arbitrary",)),
        name="out_router",
    )(x2, ym, yr, ye, wo1, wo2, wo3, g2, wr_hi, wr_lo, br, upper, ones)


def _sc_mesh():
    return plsc.VectorSubcoreMesh(core_axis_name="c", subcore_axis_name="s",
                                  num_cores=SC_CORES, num_subcores=SC_SUBCORES)


def _sc_split(n_rows):
    n_workers = SC_CORES * SC_SUBCORES
    per_worker = n_rows // n_workers
    assert per_worker * n_workers == n_rows and per_worker % (SC_ROWS * SC_INFLIGHT) == 0
    assert SC_ROWS % 8 == 0
    return per_worker


def _sc_gather_rows(table, idx):
    n_rows = idx.shape[0]
    width = table.shape[1]
    per_worker = _sc_split(n_rows)

    buf = lambda shape, dtype: [pltpu.VMEM(shape, dtype) for _ in range(SC_INFLIGHT)]

    @functools.partial(
        pl.kernel, mesh=_sc_mesh(),
        out_type=jax.ShapeDtypeStruct((n_rows, width), table.dtype),
        scratch_types=[buf((SC_ROWS,), I32), buf((SC_ROWS, width), table.dtype),
                       [pltpu.SemaphoreType.DMA for _ in range(SC_INFLIGHT)]],
        name="sc_gather",
    )
    def gather(table_hbm, idx_hbm, out_hbm, idx_v, rows_v, sems):
        worker = lax.axis_index("s") * SC_CORES + lax.axis_index("c")
        base = worker * per_worker

        @pl.loop(0, per_worker // (SC_ROWS * SC_INFLIGHT))
        def _(step):
            offs = [pl.multiple_of(base + (step * SC_INFLIGHT + b) * SC_ROWS, 8) for b in range(SC_INFLIGHT)]
            copies = []
            for b in range(SC_INFLIGHT):
                pltpu.sync_copy(idx_hbm.at[pl.ds(offs[b], SC_ROWS)], idx_v[b])
                copies.append(pltpu.async_copy(table_hbm.at[idx_v[b]], rows_v[b], sems[b]))
            for b in range(SC_INFLIGHT):
                copies[b].wait()
                pltpu.sync_copy(rows_v[b], out_hbm.at[pl.ds(offs[b], SC_ROWS)])

    return gather(table, idx)


def _sc_scatter_rows(rows, idx, n_out):
    n_idx = idx.shape[0]
    n_src, width = rows.shape
    per_worker = _sc_split(n_idx)
    assert n_src % SC_ROWS == 0

    buf = lambda shape, dtype: [pltpu.VMEM(shape, dtype) for _ in range(SC_INFLIGHT)]

    @functools.partial(
        pl.kernel, mesh=_sc_mesh(),
        out_type=jax.ShapeDtypeStruct((n_out, width), rows.dtype),
        scratch_types=[buf((SC_ROWS,), I32), buf((SC_ROWS, width), rows.dtype),
                       [pltpu.SemaphoreType.DMA for _ in range(SC_INFLIGHT)]],
        name="sc_scatter",
    )
    def scatter(rows_hbm, idx_hbm, out_hbm, idx_v, rows_v, sems):
        worker = lax.axis_index("s") * SC_CORES + lax.axis_index("c")
        base = worker * per_worker

        @pl.loop(0, per_worker // (SC_ROWS * SC_INFLIGHT))
        def _(step):
            copies = []
            for b in range(SC_INFLIGHT):
                off = pl.multiple_of(base + (step * SC_INFLIGHT + b) * SC_ROWS, 8)
                src = pl.multiple_of(lax.rem(off, n_src), 8)
                pltpu.sync_copy(idx_hbm.at[pl.ds(off, SC_ROWS)], idx_v[b])
                pltpu.sync_copy(rows_hbm.at[pl.ds(src, SC_ROWS)], rows_v[b])
                copies.append(pltpu.async_copy(rows_v[b], out_hbm.at[idx_v[b]], sems[b]))
            for cp in copies:
                cp.wait()

    return scatter(rows, idx)


def _experts_kernel(be_ref, bv_ref, nused_ref, next_ref, slot_ref,
                    xs_ref, wgu_hbm, bg_ref, bu_ref, wd_hbm, bd_ref, perm_ref,
                    o_ref, wgu_buf, wd_buf, wg_s, wu_s, wd_s, sem):
    i = pl.program_id(0)
    e = be_ref[i]
    used = i < nused_ref[0]
    changed = ((i == 0) | (e != be_ref[jnp.maximum(i - 1, 0)])) & used
    slot = slot_ref[e]

    def weight_copies(expert, buf_slot):
        return (pltpu.make_async_copy(wgu_hbm.at[expert], wgu_buf.at[buf_slot], sem.at[buf_slot, 0]),
                pltpu.make_async_copy(wd_hbm.at[expert], wd_buf.at[buf_slot], sem.at[buf_slot, 1]))

    @pl.when((i == 0) & used)
    def _():
        for cp in weight_copies(e, slot):
            cp.start()

    @pl.when(changed)
    def _():
        for cp in weight_copies(e, slot):
            cp.wait()
        nxt = next_ref[e]

        @pl.when(nxt >= 0)
        def _():
            for cp in weight_copies(nxt, 1 - slot):
                cp.start()

        half = LANES
        for cblk in range(2 * D_EXPERT // (2 * half)):
            wt = wgu_buf[slot, :, cblk * 2 * half:(cblk + 1) * 2 * half].astype(BF16)
            sep = jnp.dot(wt, perm_ref[...], preferred_element_type=F32).astype(BF16)
            wg_s[:, cblk * half:(cblk + 1) * half] = sep[:, :half]
            wu_s[:, cblk * half:(cblk + 1) * half] = sep[:, half:]
        wd_s[...] = wd_buf[slot].astype(BF16)

    @pl.when(used)
    def _():
        rows = lax.broadcasted_iota(I32, xs_ref.shape, 0)
        xb = _unpack_halves(jnp.where(rows < bv_ref[i], xs_ref[...], 0)).astype(BF16)
        gate = jnp.dot(xb, wg_s[...], preferred_element_type=F32) + bg_ref[0]
        up = jnp.dot(xb, wu_s[...], preferred_element_type=F32) + bu_ref[0]
        gate = jnp.minimum(gate, SWIGLU_LIMIT)
        up = jnp.clip(up, -SWIGLU_LIMIT, SWIGLU_LIMIT)
        glu = gate * _sigmoid(gate * SWIGLU_ALPHA)
        act = ((up + 1.0) * glu).astype(BF16)
        o_ref[...] = _pack_halves(jnp.dot(act, wd_s[...], preferred_element_type=F32) + bd_ref[0])

    @pl.when(jnp.logical_not(used))
    def _():
        o_ref[...] = jnp.zeros_like(o_ref)


def _experts(blk_expert, blk_valid, n_used, next_expert, buf_slot, xs, w_gate_up, b_gate_up, w_down,
             b_down):
    p_rows = xs.shape[0]
    d = 2 * xs.shape[1]
    n_blocks = p_rows // EXPERT_ROWS
    bg = b_gate_up[:, 0::2].reshape(N_EXPERTS, 1, D_EXPERT)
    bu = b_gate_up[:, 1::2].reshape(N_EXPERTS, 1, D_EXPERT)
    bd = b_down.reshape(N_EXPERTS, 1, d)
    perm_np = np.zeros((2 * LANES, 2 * LANES), np.float32)
    perm_np[2 * np.arange(LANES), np.arange(LANES)] = 1.0
    perm_np[2 * np.arange(LANES) + 1, LANES + np.arange(LANES)] = 1.0
    perm = jnp.asarray(perm_np).astype(BF16)
    by_expert = lambda shape: pl.BlockSpec((1,) + shape, lambda i, be, *_: (be[i], 0, 0))
    grid_spec = pltpu.PrefetchScalarGridSpec(
        num_scalar_prefetch=5,
        grid=(n_blocks,),
        in_specs=[pl.BlockSpec((EXPERT_ROWS, d // 2), lambda i, *_: (i, 0)),
                  pl.BlockSpec(memory_space=pl.ANY), by_expert((1, D_EXPERT)), by_expert((1, D_EXPERT)),
                  pl.BlockSpec(memory_space=pl.ANY), by_expert((1, d)),
                  pl.BlockSpec(perm.shape, lambda i, *_: (0, 0))],
        out_specs=pl.BlockSpec((EXPERT_ROWS, d // 2), lambda i, *_: (i, 0)),
        scratch_shapes=[pltpu.VMEM((2, d, 2 * D_EXPERT), F32), pltpu.VMEM((2, D_EXPERT, d), F32),
                        pltpu.VMEM((d, D_EXPERT), BF16), pltpu.VMEM((d, D_EXPERT), BF16),
                        pltpu.VMEM((D_EXPERT, d), BF16), pltpu.SemaphoreType.DMA((2, 2))],
    )
    return pl.pallas_call(
        _experts_kernel,
        grid_spec=grid_spec,
        out_shape=jax.ShapeDtypeStruct((p_rows, d // 2), I32),
        compiler_params=_cparams(("arbitrary",)),
        name="experts",
    )(blk_expert, blk_valid, n_used, next_expert, buf_slot, xs, w_gate_up, bg, bu, w_down, bd, perm)


def _combine_kernel(yg_ref, x1_ref, w_ref, g_ref, o_ref):
    acc = x1_ref[...]
    wts = w_ref[...]
    for kk in range(TOP_K):
        acc = acc + _unpack_halves(yg_ref[kk]) * wts[:, kk:kk + 1]
    o_ref[...] = _rms(acc, g_ref[...])


def _combine(yg, x1, wts, g_final, tb):
    t, d = x1.shape
    half = yg.shape[2]
    return pl.pallas_call(
        _combine_kernel,
        grid=(t // tb,),
        in_specs=[pl.BlockSpec((TOP_K, tb, half), lambda i: (0, i, 0)),
                  pl.BlockSpec((tb, d), lambda i: (i, 0)),
                  pl.BlockSpec((tb, LANES), lambda i: (i, 0)),
                  pl.BlockSpec((1, d), lambda i: (0, 0))],
        out_specs=pl.BlockSpec((tb, d), lambda i: (i, 0)),
        out_shape=jax.ShapeDtypeStruct((t, d), F32),
        compiler_params=_cparams(("parallel",)),
        name="combine",
    )(yg, x1, wts, g_final.reshape(1, d))


def _layer(x, mem, w_in, w_out, w_mem_kv, g_mix, g_mem, g_ffn, bias_tables, mu, w0, w_up, a0, a_up,
           g_up, k_k, k_a, r_k, gn_g, gn_b, w_router, b_router, w_gate_up, b_gate_up, w_down, b_down,
           g_last):
    b, s, d = x.shape
    m = mem.shape[1]
    t = b * s
    x2 = x.reshape(t, d)

    p = _norm_matmul(x2, g_mix, w_in.astype(BF16), 512, "in_proj")
    p3 = p.reshape(b, s, IN_COLS)
    y_moba = _moba_attention(p3, *bias_tables)
    y_rwkv = _rwkv(p3, mu, w0, w_up, a0, a_up, g_up, k_k, k_a, r_k, gn_g, gn_b)
    mkv = _norm_matmul(mem.reshape(b * m, d), g_mem, w_mem_kv.astype(BF16), 512, "mem_kv")
    y_mem = _mem_attention(p3, mkv.reshape(b, m, 2 * MEM_W), 512)

    tb = ROUTE_TOKENS
    x1, h2, idx_o, rank_o, wgt_p, cnt = _out_router(
        x2, y_moba.reshape(t, MOBA_W), y_rwkv.reshape(t, RWKV_W), y_mem.reshape(t, MEM_W),
        w_out, g_ffn, w_router, b_router, tb)

    counts = cnt[:, 0].astype(I32)
    padded = (counts + EXPERT_ROWS - 1) // EXPERT_ROWS * EXPERT_ROWS
    pad_ends = jnp.cumsum(padded)
    pad_starts = (pad_ends - padded).astype(I32)
    n_blocks = (t * TOP_K) // EXPERT_ROWS + N_EXPERTS
    blk_start = jnp.arange(n_blocks, dtype=I32) * EXPERT_ROWS
    blk_expert = jnp.minimum(jnp.sum(blk_start[:, None] >= pad_ends[None, :], axis=1),
                             N_EXPERTS - 1).astype(I32)
    blk_valid = jnp.clip(counts[blk_expert] - (blk_start - pad_starts[blk_expert]), 0, EXPERT_ROWS)
    n_used = (pad_ends[-1:] // EXPERT_ROWS).astype(I32)
    has_rows = counts > 0
    first_from = lax.cummin(jnp.where(has_rows, jnp.arange(N_EXPERTS, dtype=I32), N_EXPERTS), reverse=True)
    next_expert = jnp.concatenate([first_from[1:], jnp.full((1,), N_EXPERTS, I32)])
    next_expert = jnp.where(next_expert < N_EXPERTS, next_expert, -1).astype(I32)
    buf_slot = ((jnp.cumsum(has_rows.astype(I32)) - has_rows.astype(I32)) % 2).astype(I32)
    idx_kt = jnp.swapaxes(idx_o[:, :TOP_K, :], 0, 1).reshape(TOP_K, t)
    rank_kt = jnp.swapaxes(rank_o[:, :TOP_K, :], 0, 1).reshape(TOP_K, t)
    start_kt = jnp.zeros((TOP_K, t), I32)
    for e in range(N_EXPERTS):
        start_kt = jnp.where(idx_kt == e, pad_starts[e], start_kt)
    dest = (start_kt + rank_kt).reshape(TOP_K * t)

    xs = _sc_scatter_rows(h2, dest, n_blocks * EXPERT_ROWS)
    ys = _experts(blk_expert, blk_valid.astype(I32), n_used, next_expert, buf_slot, xs,
                  w_gate_up, b_gate_up, w_down, b_down)
    yg = _sc_gather_rows(ys, dest).reshape(TOP_K, t, d // 2)
    out = _combine(yg, x1, wgt_p, g_last, tb)
    return out.reshape(b, s, d)


def kernel(x, mem, w_in, w_out, w_mem_kv, g_mix, g_mem, g_ffn, g_final, rel_bias, rwkv_mu, rwkv_w0,
           rwkv_w_up, rwkv_a0, rwkv_a_up, rwkv_g_up, rwkv_k_k, rwkv_k_a, rwkv_r_k, rwkv_gn_g, rwkv_gn_b,
           w_router, b_router, w_gate_up, b_gate_up, w_down, b_down):
    depth = w_in.shape[0]
    assert depth == 1, "the final norm is fused into the last layer's combine kernel"
    bias_tables = _moba_bias_tables(rel_bias)
    l = 0
    return _layer(x, mem, w_in[l], w_out[l], w_mem_kv[l], g_mix[l], g_mem[l], g_ffn[l], bias_tables,
                  rwkv_mu[l], rwkv_w0[l], rwkv_w_up[l], rwkv_a0[l], rwkv_a_up[l], rwkv_g_up[l],
                  rwkv_k_k[l], rwkv_k_a[l], rwkv_r_k[l], rwkv_gn_g[l], rwkv_gn_b[l], w_router[l],
                  b_router[l], w_gate_up[l], b_gate_up[l], w_down[l], b_down[l], g_final)
```

```python
import functools
import math

import numpy as np
import jax
import jax.numpy as jnp
from jax import lax
from jax.experimental import pallas as pl
from jax.experimental.pallas import tpu as pltpu
from jax.experimental.pallas import tpu_sc as plsc

F32 = jnp.float32
BF16 = jnp.bfloat16
I32 = jnp.int32
HI = lax.Precision.HIGHEST

D_MODEL = 1024
HEAD_DIM = 64
MOBA_HEADS = 6
RWKV_HEADS = 6
MEM_HEADS = 4
MOBA_W = MOBA_HEADS * HEAD_DIM
RWKV_W = RWKV_HEADS * HEAD_DIM
MEM_W = MEM_HEADS * HEAD_DIM
MOBA_BLOCK = 256
MOBA_TOPK = 3
N_BUCKETS = 32
MAX_DISTANCE = 128
DECAY_LORA = 64
AAA_LORA = 64
GATE_LORA = 128
RWKV_COLS = 3 * RWKV_W + DECAY_LORA + AAA_LORA + GATE_LORA
RWKV_GN_EPS = 64e-5
IN_COLS = 3 * MOBA_W + RWKV_COLS + MEM_W
N_EXPERTS = 32
TOP_K = 4
D_EXPERT = D_MODEL
SWIGLU_ALPHA = 1.702
SWIGLU_LIMIT = 7.0
RMS_EPS = 1e-5

LANES = 128
RWKV_CHUNK = 128
RWKV_SEQS_PER_STEP = 4
EXPERT_ROWS = 256
ROUTE_TOKENS = 512
SC_CORES = 2
SC_SUBCORES = 16
SC_ROWS = 64
SC_INFLIGHT = 2
COMBINE_PARTS = 2
VMEM_LIMIT = 56 * 1024 * 1024
NEG_INF = float("-inf")
HIGH_HALF = -65536
LOG2_E = 1.4426950408889634
MOBA_VROWS = HEAD_DIM + 16


def _cparams(sem):
    return pltpu.CompilerParams(dimension_semantics=sem, vmem_limit_bytes=VMEM_LIMIT)


def _rms(x, g):
    return x * lax.rsqrt(jnp.mean(x * x, axis=-1, keepdims=True) + RMS_EPS) * g


def _bdot(a, b):
    return jnp.dot(a.astype(BF16), b.astype(BF16), preferred_element_type=F32)


def _hdot(a, b):
    return jnp.dot(a, b, preferred_element_type=F32, precision=HI)


def _dot_nt(a, b):
    return lax.dot_general(a, b, (((1,), (1,)), ((), ())), preferred_element_type=F32)


def _dot_tn(a, b):
    return lax.dot_general(a, b, (((0,), (0,)), ((), ())), preferred_element_type=F32)


def _split_bf16(x, terms):
    parts = []
    for _ in range(terms):
        hi = x.astype(BF16)
        parts.append(hi)
        x = x - hi.astype(F32)
    return parts


def _dot_exact_rhs(x, m_bf16, terms):
    acc = None
    for part in _split_bf16(x, terms):
        d = jnp.dot(part, m_bf16, preferred_element_type=F32)
        acc = d if acc is None else acc + d
    return acc


def _pack_halves(x):
    n = x.shape[1] // 2
    lo = pltpu.bitcast(x[:, :n].astype(BF16).astype(F32), I32)
    hi = pltpu.bitcast(x[:, n:].astype(BF16).astype(F32), I32)
    return (hi & HIGH_HALF) | lax.shift_right_logical(lo, 16)


def _unpack_halves(w):
    lo = pltpu.bitcast(w << 16, F32)
    hi = pltpu.bitcast(w & HIGH_HALF, F32)
    return jnp.concatenate([lo, hi], axis=1)


def _norm_matmul_kernel(x_ref, g_ref, w_ref, o_ref):
    h = _rms(x_ref[...], g_ref[...])
    o_ref[...] = jnp.dot(h.astype(BF16), w_ref[...], preferred_element_type=F32)


def _norm_matmul(x, g, w_bf16, tm, name):
    t, d = x.shape
    n = w_bf16.shape[1]
    tm = min(tm, t)
    return pl.pallas_call(
        _norm_matmul_kernel,
        grid=(t // tm,),
        in_specs=[pl.BlockSpec((tm, d), lambda i: (i, 0)),
                  pl.BlockSpec((1, d), lambda i: (0, 0)),
                  pl.BlockSpec((d, n), lambda i: (0, 0))],
        out_specs=pl.BlockSpec((tm, n), lambda i: (i, 0)),
        out_shape=jax.ShapeDtypeStruct((t, n), F32),
        compiler_params=_cparams(("parallel",)),
        name=name,
    )(x, g.reshape(1, d), w_bf16)


def _t5_bucket_np(dist):
    n = np.maximum(dist, 0)
    max_exact = N_BUCKETS // 2
    nf = np.maximum(n, 1).astype(np.float64)
    large = max_exact + (np.log(nf / max_exact) / math.log(MAX_DISTANCE / max_exact)
                         * (N_BUCKETS - max_exact)).astype(np.int64)
    large = np.minimum(large, N_BUCKETS - 1)
    return np.where(n < max_exact, n, large)


def _moba_bias_tables(rel_bias):
    n = MOBA_BLOCK
    assert np.all(_t5_bucket_np(np.arange(n + 1, 64 * n)) == N_BUCKETS - 1)
    bias_t = rel_bias.astype(F32).T * LOG2_E

    def by_distance(dist):
        bucket = jnp.asarray(_t5_bucket_np(dist), I32)[None]
        tab = jnp.zeros((MOBA_HEADS, dist.shape[0]), F32)
        for b in range(N_BUCKETS):
            tab = jnp.where(bucket == b, bias_t[:, b][:, None], tab)
        return tab

    d = np.arange(2 * n)
    g_own = jnp.where(jnp.asarray(d < n)[None], by_distance(np.where(d < n, d, 0)), NEG_INF)
    g_prev = by_distance(np.where(d < n, n + d, d - n))
    far = bias_t[:, N_BUCKETS - 1]
    return g_own[:, None, :], g_prev[:, None, :], far


def _moba_kernel(far_ref, q_ref, k_ref, v_ref, g0_ref, g1_ref, o_ref,
                 qt_s, kb_s, vt_s, sc_s, t0_ref, t1_ref, *, nb):
    hp = pl.program_id(1)
    blk = MOBA_BLOCK
    scale = HEAD_DIM ** -0.5 * LOG2_E
    row_head = lax.broadcasted_iota(I32, (LANES, 1), 0) // HEAD_DIM
    lane_head = lax.broadcasted_iota(I32, (1, LANES), 1) // HEAD_DIM
    ones_rows = jnp.ones((MOBA_VROWS - HEAD_DIM, blk), BF16)

    for h in range(2):
        for g_ref, t_ref in ((g0_ref, t0_ref), (g1_ref, t1_ref)):
            rows = jnp.broadcast_to(g_ref[h], (blk, 2 * blk))
            t_ref[h] = pltpu.roll(rows, 0, 1, stride=1, stride_axis=0)[:, :blk]

    kmean_rows = []
    for j in range(nb):
        sl = pl.ds(j * blk, blk)
        qt_s[j] = (q_ref[0, sl, :] * scale).T
        vt = v_ref[0, sl, :].T.astype(BF16)
        for h in range(2):
            vt_s[j, h * MOBA_VROWS:h * MOBA_VROWS + HEAD_DIM, :] = vt[h * HEAD_DIM:(h + 1) * HEAD_DIM, :]
            vt_s[j, h * MOBA_VROWS + HEAD_DIM:(h + 1) * MOBA_VROWS, :] = ones_rows
        kj = k_ref[0, sl, :]
        kb_s[j] = kj.astype(BF16)
        kmean_rows.append(jnp.mean(kj, axis=0, keepdims=True))
    kmean = jnp.concatenate(kmean_rows, axis=0)
    km2 = jnp.concatenate([jnp.where(lane_head == 0, kmean, 0.0),
                           jnp.where(lane_head == 1, kmean, 0.0)], axis=0)
    blk_iota = lax.broadcasted_iota(I32, (nb, blk), 0)

    for i in range(nb):
        qt = qt_s[i]
        negs = [None, None]
        if i > MOBA_TOPK:
            gate2 = _hdot(km2, qt)
            for h in range(2):
                g = gate2[h * nb:(h + 1) * nb, :]
                cnt = jnp.zeros((nb, blk), I32)
                for m in range(i):
                    gm = g[m:m + 1, :]
                    beats = (gm > g) | ((gm == g) & (m < blk_iota))
                    cnt = cnt + jnp.where(beats, 1, 0)
                negs[h] = jnp.where(cnt < MOBA_TOPK, 0.0, NEG_INF)
        outs = []
        for h in range(2):
            qt_h = jnp.where(row_head == h, qt, 0.0).astype(BF16)
            far_bias = far_ref[2 * hp + h]
            m_run = None
            for j in range(i + 1):
                s = jnp.dot(kb_s[j], qt_h, preferred_element_type=F32)
                if j == i:
                    s = s + t0_ref[h]
                else:
                    bias = t1_ref[h] if j == i - 1 else far_bias
                    if negs[h] is not None:
                        bias = bias + negs[h][j:j + 1, :]
                    s = s + bias
                sc_s[h, j] = s
                cm = jnp.max(s, axis=0, keepdims=True)
                m_run = cm if m_run is None else jnp.maximum(m_run, cm)
            acc = jnp.zeros((MOBA_VROWS, blk), F32)
            for j in range(i + 1):
                p_t = jnp.exp2(sc_s[h, j] - m_run)
                vt_h = vt_s[j, h * MOBA_VROWS:(h + 1) * MOBA_VROWS, :]
                acc = acc + jnp.dot(vt_h, p_t.astype(BF16), preferred_element_type=F32)
            outs.append(acc[:HEAD_DIM] / acc[HEAD_DIM:HEAD_DIM + 1])
        o_t = jnp.concatenate(outs, axis=0)
        o_ref[0, pl.ds(i * blk, blk), :] = o_t.T


def _moba_attention(p3, g_own, g_prev, far):
    b, s, _ = p3.shape
    nb = s // MOBA_BLOCK
    n_pairs = MOBA_HEADS // 2
    blk_spec = lambda off: pl.BlockSpec((1, s, LANES), lambda bi, hp: (bi, 0, off + hp))
    tab_spec = pl.BlockSpec((2, 1, 2 * MOBA_BLOCK), lambda bi, hp: (hp, 0, 0))
    return pl.pallas_call(
        functools.partial(_moba_kernel, nb=nb),
        grid=(b, n_pairs),
        in_specs=[pl.BlockSpec(memory_space=pltpu.SMEM),
                  blk_spec(0), blk_spec(n_pairs), blk_spec(2 * n_pairs), tab_spec, tab_spec],
        out_specs=pl.BlockSpec((1, s, LANES), lambda bi, hp: (bi, 0, hp)),
        out_shape=jax.ShapeDtypeStruct((b, s, MOBA_W), F32),
        scratch_shapes=[pltpu.VMEM((nb, LANES, MOBA_BLOCK), F32),
                        pltpu.VMEM((nb, MOBA_BLOCK, LANES), BF16),
                        pltpu.VMEM((nb, 2 * MOBA_VROWS, MOBA_BLOCK), BF16),
                        pltpu.VMEM((2, nb, MOBA_BLOCK, MOBA_BLOCK), F32),
                        pltpu.VMEM((2, MOBA_BLOCK, MOBA_BLOCK), F32),
                        pltpu.VMEM((2, MOBA_BLOCK, MOBA_BLOCK), F32)],
        compiler_params=_cparams(("parallel", "parallel")),
        name="moba",
    )(far, p3, p3, p3, g_own, g_prev)


def _softplus(z):
    return jnp.maximum(z, 0.0) + jnp.log(1.0 + jnp.exp(-jnp.abs(z)))


def _sigmoid(z):
    return 1.0 / (1.0 + jnp.exp(-z))


def _rwkv_kernel(r_ref, k_ref, v_ref, wa_ref, g_ref,
                 mu_r_ref, mu_k_ref, mu_v_ref, mu_wa_ref, mu_g_ref,
                 w0_ref, wup_ref, a0_ref, aup_ref, gup_ref, kk_ref, ka_ref, rk_ref,
                 gng_ref, gnb_ref, hsum_ref, tri_ref,
                 o_ref, st_s, prev_r, prev_k, prev_v, prev_wa, prev_g, *, n_seq):
    c = RWKV_CHUNK
    mid = c // 2

    @pl.when(pl.program_id(1) == 0)
    def _():
        st_s[...] = jnp.zeros_like(st_s)
        prev_r[...] = jnp.zeros_like(prev_r)
        prev_k[...] = jnp.zeros_like(prev_k)
        prev_v[...] = jnp.zeros_like(prev_v)
        prev_wa[...] = jnp.zeros_like(prev_wa)
        prev_g[...] = jnp.zeros_like(prev_g)

    n_pairs = RWKV_HEADS // 2
    hsum = hsum_ref[...]
    tri = tri_ref[...]
    row = lax.broadcasted_iota(I32, (c, c), 0)
    col = lax.broadcasted_iota(I32, (c, c), 1)
    strict = row > col
    incl = row >= col
    eye = (row == col).astype(F32)
    lane_half = lax.broadcasted_iota(I32, (1, LANES), 1) // HEAD_DIM
    hmask = hsum.astype(F32)
    dot = functools.partial(jnp.dot, preferred_element_type=F32)
    lanes = lambda x, jp: x[:, jp * LANES:(jp + 1) * LANES]

    def group_sum(x):
        return jnp.concatenate([_dot_exact_rhs(lanes(x, jp), hsum, 2) for jp in range(n_pairs)], axis=1)

    def two_heads(x):
        return jnp.concatenate([jnp.where(lane_half == 0, x, 0.0), jnp.where(lane_half == 1, x, 0.0)],
                               axis=0).astype(BF16)

    def side(a, b):
        return jnp.concatenate([a, b], axis=1)

    def mix(g, x_ref, prev_ref, mu_ref):
        x = x_ref[g]
        rows = lax.broadcasted_iota(I32, x.shape, 0)
        shifted = jnp.where(rows == 0, prev_ref[g], pltpu.roll(x, 1, axis=0))
        prev_ref[g] = x[c - 1:c, :]
        return x + (shifted - x) * mu_ref[...]

    def prepare(g):
        r = mix(g, r_ref, prev_r, mu_r_ref)
        k = mix(g, k_ref, prev_k, mu_k_ref)
        v = mix(g, v_ref, prev_v, mu_v_ref)
        xwa = mix(g, wa_ref, prev_wa, mu_wa_ref)
        xg = mix(g, g_ref, prev_g, mu_g_ref)

        w = -_softplus(-(w0_ref[...] + _bdot(jnp.tanh(xwa), wup_ref[...]))) - 0.5
        logd = -jnp.exp(w)
        a = _sigmoid(a0_ref[...] + _bdot(xwa, aup_ref[...]))
        gate = _bdot(_sigmoid(xg), gup_ref[...])
        kk = k * kk_ref[...]
        kk = kk / jnp.maximum(jnp.sqrt(group_sum(kk * kk)), 1e-12)
        k2 = k * (1.0 + (a - 1.0) * ka_ref[...])

        parts = _split_bf16(logd, 3)
        cum = dot(tri, parts[0]) + dot(tri, parts[1]) + dot(tri, parts[2])
        cum_last = cum[c - 1:c, :]
        ref = cum[mid - 1:mid, :]
        rel = cum - ref
        g_in = jnp.exp(rel)
        g_inv = jnp.exp(-rel)
        g_tail = jnp.exp(cum_last - cum)
        a_t = -kk * jnp.exp(rel - logd)
        b_t = kk * a * g_inv
        k_t = k2 * g_inv
        r_t = r * g_in
        b_hat = kk * a * g_tail
        k_hat = k2 * g_tail

        ref_scale = jnp.exp(ref)
        decay = jnp.exp(cum_last)
        bk = jnp.concatenate([b_t, k_t], axis=0).astype(BF16)
        ar_all = jnp.concatenate([a_t, r_t], axis=0)
        per_pair = []
        for jp in range(n_pairs):
            st = st_s[g, jp]
            st_ref = (st * lanes(ref_scale, jp)).astype(BF16)
            both = _dot_nt(lanes(ar_all, jp).astype(BF16), st_ref)
            quads = []
            for half in range(2):
                ar = jnp.where(lane_half == half, lanes(ar_all, jp), 0.0)
                m4 = _dot_nt(ar.astype(BF16), lanes(bk, jp))
                quads.append((jnp.where(strict, m4[:c, :c], 0.0),
                              jnp.where(strict, m4[:c, c:], 0.0).astype(BF16),
                              jnp.where(incl, m4[c:, :c], 0.0).astype(BF16),
                              jnp.where(incl, m4[c:, c:], 0.0).astype(BF16)))
            per_pair.append(dict(
                st=st, p0=both[:c], o0=both[c:], v_rows=two_heads(lanes(v, jp)),
                l_ab=[q[0] for q in quads], l_ak=side(quads[0][1], quads[1][1]),
                m_rb=side(quads[0][2], quads[1][2]), m_rk=side(quads[0][3], quads[1][3]),
                v=lanes(v, jp).astype(BF16), b_hat=lanes(b_hat, jp).astype(BF16),
                k_hat=lanes(k_hat, jp).astype(BF16), decay=lanes(decay, jp)))
        return dict(r=r, k2=k2, v=v, gate=gate, pairs=per_pair)

    seqs = [prepare(g) for g in range(n_seq)]
    heads = [(g, jp, half) for g in range(n_seq) for jp in range(n_pairs) for half in range(2)]

    pw = {gh: seqs[gh[0]]["pairs"][gh[1]]["l_ab"][gh[2]] for gh in heads}
    t_inv = {gh: eye + pw[gh] for gh in heads}
    pw = {gh: pw[gh].astype(BF16) for gh in heads}
    span = 2
    while span < c:
        pw = {gh: dot(pw[gh], pw[gh]).astype(BF16) for gh in heads}
        t_inv = {gh: t_inv[gh] + dot(t_inv[gh].astype(BF16), pw[gh]) for gh in heads}
        span *= 2

    for g in range(n_seq):
        sq = seqs[g]
        outs = []
        for jp in range(n_pairs):
            pr = sq["pairs"][jp]
            t_pair = side(t_inv[(g, jp, 0)], t_inv[(g, jp, 1)]).astype(BF16)
            w_loc = pr["p0"] + dot(pr["l_ak"], pr["v_rows"])
            u = dot(t_pair, two_heads(w_loc))
            outs.append(pr["o0"] + dot(pr["m_rb"], two_heads(u)) + dot(pr["m_rk"], pr["v_rows"]))
            upd = _dot_tn(u.astype(BF16), pr["b_hat"]) + _dot_tn(pr["v"], pr["k_hat"])
            st_s[g, jp] = pr["st"] * pr["decay"] + hmask * upd
        o = jnp.concatenate(outs, axis=1)

        inv_n = 1.0 / HEAD_DIM
        mean = group_sum(o) * inv_n
        dev = o - mean
        var = group_sum(dev * dev) * inv_n
        y = dev * lax.rsqrt(var + RWKV_GN_EPS) * gng_ref[...] + gnb_ref[...]
        y = y + group_sum(sq["r"] * sq["k2"] * rk_ref[...]) * sq["v"]
        o_ref[g] = y * sq["gate"]


def _rwkv(p3, mu, w0, w_up, a0, a_up, g_up, k_k, k_a, r_k, gn_g, gn_b):
    b, s, _ = p3.shape
    c = RWKV_CHUNK
    w = RWKV_W
    n_seq = RWKV_SEQS_PER_STEP if b % RWKV_SEQS_PER_STEP == 0 else 1
    base = 3 * MOBA_W
    assert base % w == 0 and (base + 3 * w) % LANES == 0
    lora = DECAY_LORA + AAA_LORA

    def col_spec(width, off_cols):
        assert off_cols % width == 0
        return pl.BlockSpec((n_seq, c, width), lambda bi, ci: (bi, ci, off_cols // width))

    row = lambda x: x.reshape(1, -1).astype(F32)
    const = lambda shape: pl.BlockSpec(shape, lambda bi, ci: (0,) * len(shape))
    wup_pad = jnp.concatenate([w_up, jnp.zeros((AAA_LORA, w), F32)], axis=0).astype(BF16)
    aup_pad = jnp.concatenate([jnp.zeros((DECAY_LORA, w), F32), a_up], axis=0).astype(BF16)
    head = np.arange(LANES) // HEAD_DIM
    hsum = jnp.asarray((head[:, None] == head[None, :]).astype(np.float32)).astype(BF16)
    tri = jnp.asarray(np.tril(np.ones((c, c), np.float32))).astype(BF16)
    vec_args = [mu[:w], mu[w:2 * w], mu[2 * w:3 * w], mu[3 * w:3 * w + lora], mu[3 * w + lora:],
                w0, None, a0, None, None, k_k, k_a, r_k.reshape(-1), gn_g, gn_b]
    args = [p3, p3, p3, p3, p3]
    specs = [col_spec(w, base), col_spec(w, base + w), col_spec(w, base + 2 * w),
             col_spec(lora, base + 3 * w), col_spec(GATE_LORA, base + 3 * w + lora)]
    mats = {6: wup_pad, 8: aup_pad, 9: g_up.astype(BF16)}
    for idx, a in enumerate(vec_args):
        arr = mats[idx] if a is None else row(a)
        args.append(arr)
        specs.append(const(arr.shape))
    for arr in (hsum, tri):
        args.append(arr)
        specs.append(const(arr.shape))
    return pl.pallas_call(
        functools.partial(_rwkv_kernel, n_seq=n_seq),
        grid=(b // n_seq, s // c),
        in_specs=specs,
        out_specs=pl.BlockSpec((n_seq, c, w), lambda bi, ci: (bi, ci, 0)),
        out_shape=jax.ShapeDtypeStruct((b, s, w), F32),
        scratch_shapes=[pltpu.VMEM((n_seq, RWKV_HEADS // 2, LANES, LANES), F32),
                        pltpu.VMEM((n_seq, 1, w), F32),
                        pltpu.VMEM((n_seq, 1, w), F32), pltpu.VMEM((n_seq, 1, w), F32),
                        pltpu.VMEM((n_seq, 1, lora), F32), pltpu.VMEM((n_seq, 1, GATE_LORA), F32)],
        compiler_params=_cparams(("parallel", "arbitrary")),
        name="rwkv",
    )(*args)


def _mem_attn_kernel(q_ref, kv_ref, o_ref):
    scale = HEAD_DIM ** -0.5
    lane_head = lax.broadcasted_iota(I32, (1, MEM_W), 1) // HEAD_DIM
    q = q_ref[0] * scale
    mk = kv_ref[0, :, :MEM_W].astype(BF16)
    mv = kv_ref[0, :, MEM_W:].astype(BF16)
    out = jnp.zeros(q.shape, F32)
    for h in range(MEM_HEADS):
        hm = lane_head == h
        s = _dot_nt(jnp.where(hm, q, 0.0).astype(BF16), mk)
        s = s - jnp.max(s, axis=-1, keepdims=True)
        e = jnp.exp(s)
        p = e / jnp.sum(e, axis=-1, keepdims=True)
        out = jnp.where(hm, jnp.dot(p.astype(BF16), mv, preferred_element_type=F32), out)
    o_ref[0] = out


def _mem_attention(p3, mkv3, tq):
    b, s, _ = p3.shape
    m = mkv3.shape[1]
    off = (3 * MOBA_W + RWKV_COLS) // MEM_W
    assert off * MEM_W == 3 * MOBA_W + RWKV_COLS
    return pl.pallas_call(
        _mem_attn_kernel,
        grid=(b, s // tq),
        in_specs=[pl.BlockSpec((1, tq, MEM_W), lambda bi, i: (bi, i, off)),
                  pl.BlockSpec((1, m, 2 * MEM_W), lambda bi, i: (bi, 0, 0))],
        out_specs=pl.BlockSpec((1, tq, MEM_W), lambda bi, i: (bi, i, 0)),
        out_shape=jax.ShapeDtypeStruct((b, s, MEM_W), F32),
        compiler_params=_cparams(("parallel", "parallel")),
        name="mem_attn",
    )(p3, mkv3)


def _out_router_kernel(x_ref, ym_ref, yr_ref, ye_ref, wo1_ref, wo2_ref, wo3_ref, g_ref,
                       wrh_ref, wrl_ref, br_ref, upper_ref, ones_ref,
                       x1_ref, h_ref, idx_ref, rank_ref, wgt_ref, cnt_ref, run_s):
    @pl.when(pl.program_id(0) == 0)
    def _():
        run_s[...] = jnp.zeros_like(run_s)

    x1 = (x_ref[...] + _bdot(ym_ref[...], wo1_ref[...]) + _bdot(yr_ref[...], wo2_ref[...])
          + _bdot(ye_ref[...], wo3_ref[...]))
    x1_ref[...] = x1
    h = _rms(x1, g_ref[...])
    h_ref[...] = _pack_halves(h)
    tm = h.shape[0]
    h_hi, h_lo = _split_bf16(h, 2)
    dot = functools.partial(jnp.dot, preferred_element_type=F32)
    logits = dot(h_hi, wrh_ref[...]) + (dot(h_hi, wrl_ref[...]) + dot(h_lo, wrh_ref[...])) + br_ref[...]
    lg = logits.T[:N_EXPERTS, :]
    e_iota = lax.broadcasted_iota(I32, (N_EXPERTS, tm), 0)

    vals, idxs = [], []
    for _ in range(TOP_K):
        m = jnp.max(lg, axis=0, keepdims=True)
        idx = jnp.min(jnp.where(lg == m, e_iota, N_EXPERTS), axis=0, keepdims=True)
        vals.append(m)
        idxs.append(idx)
        lg = jnp.where(e_iota == idx, NEG_INF, lg)
    exps = [jnp.exp(vk - vals[0]) for vk in vals]
    denom = exps[0] + exps[1] + exps[2] + exps[3]

    chosen = jnp.zeros((N_EXPERTS, tm), F32)
    for idx in idxs:
        chosen = chosen + jnp.where(e_iota == idx, 1.0, 0.0)
    chosen = chosen.astype(BF16)
    run = run_s[...]
    before = dot(chosen, upper_ref[...]) + jnp.concatenate([run] * (tm // LANES), axis=1)
    run_s[...] = run + dot(chosen, ones_ref[...])
    cnt_ref[...] = run_s[...]

    zero_i = jnp.zeros((8 - TOP_K, tm), I32)
    ranks = [jnp.sum(jnp.where(e_iota == idx, before, 0.0), axis=0, keepdims=True).astype(I32)
             for idx in idxs]
    idx_ref[0] = jnp.concatenate(idxs + [zero_i], axis=0)
    rank_ref[0] = jnp.concatenate(ranks + [zero_i], axis=0)
    wrows = jnp.concatenate([e / denom for e in exps] + [jnp.zeros((LANES - TOP_K, tm), F32)], axis=0)
    wgt_ref[...] = wrows.T


def _out_router(x2, ym, yr, ye, w_out, g_ffn, w_router, b_router, tm):
    t, d = x2.shape
    wo = w_out.astype(BF16)
    wo1, wo2, wo3 = wo[:MOBA_W], wo[MOBA_W:MOBA_W + RWKV_W], wo[MOBA_W + RWKV_W:]
    wr = jnp.zeros((d, LANES), F32).at[:, :N_EXPERTS].set(w_router)
    wr_hi = wr.astype(BF16)
    wr_lo = (wr - wr_hi.astype(F32)).astype(BF16)
    br = jnp.full((1, LANES), NEG_INF, F32).at[0, :N_EXPERTS].set(b_router)
    upper = jnp.asarray(np.triu(np.ones((tm, tm), np.float32), 1)).astype(BF16)
    ones = jnp.ones((tm, LANES), BF16)
    tile = lambda n: pl.BlockSpec((tm, n), lambda i: (i, 0))
    slots = pl.BlockSpec((1, 8, tm), lambda i: (i, 0, 0))
    const = lambda a: pl.BlockSpec(a.shape, lambda i: (0,) * a.ndim)
    g2 = g_ffn.reshape(1, d)
    n = t // tm
    return pl.pallas_call(
        _out_router_kernel,
        grid=(n,),
        in_specs=[tile(d), tile(MOBA_W), tile(RWKV_W), tile(MEM_W), const(wo1), const(wo2), const(wo3),
                  const(g2), const(wr_hi), const(wr_lo), const(br), const(upper), const(ones)],
        out_specs=[tile(d), tile(d // 2), slots, slots, tile(LANES),
                   pl.BlockSpec((N_EXPERTS, LANES), lambda i: (0, 0))],
        out_shape=[jax.ShapeDtypeStruct((t, d), F32), jax.ShapeDtypeStruct((t, d // 2), I32),
                   jax.ShapeDtypeStruct((n, 8, tm), I32), jax.ShapeDtypeStruct((n, 8, tm), I32),
                   jax.ShapeDtypeStruct((t, LANES), F32), jax.ShapeDtypeStruct((N_EXPERTS, LANES), F32)],
        scratch_shapes=[pltpu.VMEM((N_EXPERTS, LANES), F32)],
        compiler_params=_cparams(("arbitrary",)),
        name="out_router",
    )(x2, ym, yr, ye, wo1, wo2, wo3, g2, wr_hi, wr_lo, br, upper, ones)


def _sc_mesh():
    return plsc.VectorSubcoreMesh(core_axis_name="c", subcore_axis_name="s",
                                  num_cores=SC_CORES, num_subcores=SC_SUBCORES)


def _sc_split(n_rows):
    n_workers = SC_CORES * SC_SUBCORES
    per_worker = n_rows // n_workers
    assert per_worker * n_workers == n_rows and per_worker % (SC_ROWS * SC_INFLIGHT) == 0
    assert SC_ROWS % 8 == 0
    return per_worker


def _sc_gather_rows(table, idx):
    n_rows = idx.shape[0]
    width = table.shape[1]
    per_worker = _sc_split(n_rows)

    buf = lambda shape, dtype: [pltpu.VMEM(shape, dtype) for _ in range(SC_INFLIGHT)]

    @functools.partial(
        pl.kernel, mesh=_sc_mesh(),
        out_type=jax.ShapeDtypeStruct((n_rows, width), table.dtype),
        scratch_types=[buf((SC_ROWS,), I32), buf((SC_ROWS, width), table.dtype),
                       [pltpu.SemaphoreType.DMA for _ in range(SC_INFLIGHT)]],
        name="sc_gather",
    )
    def gather(table_hbm, idx_hbm, out_hbm, idx_v, rows_v, sems):
        worker = lax.axis_index("s") * SC_CORES + lax.axis_index("c")
        base = worker * per_worker

        @pl.loop(0, per_worker // (SC_ROWS * SC_INFLIGHT))
        def _(step):
            offs = [pl.multiple_of(base + (step * SC_INFLIGHT + b) * SC_ROWS, 8) for b in range(SC_INFLIGHT)]
            copies = []
            for b in range(SC_INFLIGHT):
                pltpu.sync_copy(idx_hbm.at[pl.ds(offs[b], SC_ROWS)], idx_v[b])
                copies.append(pltpu.async_copy(table_hbm.at[idx_v[b]], rows_v[b], sems[b]))
            for b in range(SC_INFLIGHT):
                copies[b].wait()
                pltpu.sync_copy(rows_v[b], out_hbm.at[pl.ds(offs[b], SC_ROWS)])

    return gather(table, idx)


def _sc_scatter_rows(rows, idx, n_out):
    n_idx = idx.shape[0]
    n_src, width = rows.shape
    per_worker = _sc_split(n_idx)
    assert n_src % SC_ROWS == 0

    buf = lambda shape, dtype: [pltpu.VMEM(shape, dtype) for _ in range(SC_INFLIGHT)]

    @functools.partial(
        pl.kernel, mesh=_sc_mesh(),
        out_type=jax.ShapeDtypeStruct((n_out, width), rows.dtype),
        scratch_types=[buf((SC_ROWS,), I32), buf((SC_ROWS, width), rows.dtype),
                       [pltpu.SemaphoreType.DMA for _ in range(SC_INFLIGHT)]],
        name="sc_scatter",
    )
    def scatter(rows_hbm, idx_hbm, out_hbm, idx_v, rows_v, sems):
        worker = lax.axis_index("s") * SC_CORES + lax.axis_index("c")
        base = worker * per_worker

        @pl.loop(0, per_worker // (SC_ROWS * SC_INFLIGHT))
        def _(step):
            copies = []
            for b in range(SC_INFLIGHT):
                off = pl.multiple_of(base + (step * SC_INFLIGHT + b) * SC_ROWS, 8)
                src = pl.multiple_of(lax.rem(off, n_src), 8)
                pltpu.sync_copy(idx_hbm.at[pl.ds(off, SC_ROWS)], idx_v[b])
                pltpu.sync_copy(rows_hbm.at[pl.ds(src, SC_ROWS)], rows_v[b])
                copies.append(pltpu.async_copy(rows_v[b], out_hbm.at[idx_v[b]], sems[b]))
            for cp in copies:
                cp.wait()

    return scatter(rows, idx)


def _experts_kernel(be_ref, bv_ref, nused_ref, next_ref, slot_ref,
                    xs_ref, wgu_hbm, bg_ref, bu_ref, wd_hbm, bd_ref, perm_ref,
                    o_ref, wgu_buf, wd_buf, wg_s, wu_s, wd_s, sem):
    i = pl.program_id(0)
    e = be_ref[i]
    used = i < nused_ref[0]
    changed = ((i == 0) | (e != be_ref[jnp.maximum(i - 1, 0)])) & used
    slot = slot_ref[e]

    def weight_copies(expert, buf_slot):
        return (pltpu.make_async_copy(wgu_hbm.at[expert], wgu_buf.at[buf_slot], sem.at[buf_slot, 0]),
                pltpu.make_async_copy(wd_hbm.at[expert], wd_buf.at[buf_slot], sem.at[buf_slot, 1]))

    @pl.when((i == 0) & used)
    def _():
        for cp in weight_copies(e, slot):
            cp.start()

    @pl.when(changed)
    def _():
        for cp in weight_copies(e, slot):
            cp.wait()
        nxt = next_ref[e]

        @pl.when(nxt >= 0)
        def _():
            for cp in weight_copies(nxt, 1 - slot):
                cp.start()

        half = LANES
        for cblk in range(2 * D_EXPERT // (2 * half)):
            wt = wgu_buf[slot, :, cblk * 2 * half:(cblk + 1) * 2 * half].astype(BF16)
            sep = jnp.dot(wt, perm_ref[...], preferred_element_type=F32).astype(BF16)
            wg_s[:, cblk * half:(cblk + 1) * half] = sep[:, :half]
            wu_s[:, cblk * half:(cblk + 1) * half] = sep[:, half:]
        wd_s[...] = wd_buf[slot].astype(BF16)

    @pl.when(used)
    def _():
        rows = lax.broadcasted_iota(I32, xs_ref.shape, 0)
        xb = _unpack_halves(jnp.where(rows < bv_ref[i], xs_ref[...], 0)).astype(BF16)
        gate = jnp.dot(xb, wg_s[...], preferred_element_type=F32) + bg_ref[0]
        up = jnp.dot(xb, wu_s[...], preferred_element_type=F32) + bu_ref[0]
        gate = jnp.minimum(gate, SWIGLU_LIMIT)
        up = jnp.clip(up, -SWIGLU_LIMIT, SWIGLU_LIMIT)
        glu = gate * _sigmoid(gate * SWIGLU_ALPHA)
        act = ((up + 1.0) * glu).astype(BF16)
        o_ref[...] = _pack_halves(jnp.dot(act, wd_s[...], preferred_element_type=F32) + bd_ref[0])

    @pl.when(jnp.logical_not(used))
    def _():
        o_ref[...] = jnp.zeros_like(o_ref)


def _experts(blk_expert, blk_valid, n_used, next_expert, buf_slot, xs, w_gate_up, b_gate_up, w_down,
             b_down):
    p_rows = xs.shape[0]
    d = 2 * xs.shape[1]
    n_blocks = p_rows // EXPERT_ROWS
    bg = b_gate_up[:, 0::2].reshape(N_EXPERTS, 1, D_EXPERT)
    bu = b_gate_up[:, 1::2].reshape(N_EXPERTS, 1, D_EXPERT)
    bd = b_down.reshape(N_EXPERTS, 1, d)
    perm_np = np.zeros((2 * LANES, 2 * LANES), np.float32)
    perm_np[2 * np.arange(LANES), np.arange(LANES)] = 1.0
    perm_np[2 * np.arange(LANES) + 1, LANES + np.arange(LANES)] = 1.0
    perm = jnp.asarray(perm_np).astype(BF16)
    by_expert = lambda shape: pl.BlockSpec((1,) + shape, lambda i, be, *_: (be[i], 0, 0))
    grid_spec = pltpu.PrefetchScalarGridSpec(
        num_scalar_prefetch=5,
        grid=(n_blocks,),
        in_specs=[pl.BlockSpec((EXPERT_ROWS, d // 2), lambda i, *_: (i, 0)),
                  pl.BlockSpec(memory_space=pl.ANY), by_expert((1, D_EXPERT)), by_expert((1, D_EXPERT)),
                  pl.BlockSpec(memory_space=pl.ANY), by_expert((1, d)),
                  pl.BlockSpec(perm.shape, lambda i, *_: (0, 0))],
        out_specs=pl.BlockSpec((EXPERT_ROWS, d // 2), lambda i, *_: (i, 0)),
        scratch_shapes=[pltpu.VMEM((2, d, 2 * D_EXPERT), F32), pltpu.VMEM((2, D_EXPERT, d), F32),
                        pltpu.VMEM((d, D_EXPERT), BF16), pltpu.VMEM((d, D_EXPERT), BF16),
                        pltpu.VMEM((D_EXPERT, d), BF16), pltpu.SemaphoreType.DMA((2, 2))],
    )
    return pl.pallas_call(
        _experts_kernel,
        grid_spec=grid_spec,
        out_shape=jax.ShapeDtypeStruct((p_rows, d // 2), I32),
        compiler_params=_cparams(("arbitrary",)),
        name="experts",
    )(blk_expert, blk_valid, n_used, next_expert, buf_slot, xs, w_gate_up, bg, bu, w_down, bd, perm)


def _combine_kernel(yg_ref, x1_ref, w_ref, g_ref, *rest):
    o_ref = rest[-1]
    acc = x1_ref[...]
    wts = w_ref[...]
    for kk in range(TOP_K):
        acc = acc + _unpack_halves(yg_ref[kk]) * wts[:, kk:kk + 1]
    o_ref[...] = _rms(acc, g_ref[...])


def _combine(yg, x1, wts, g_final, tb, first_tile, partial_out):
    t, d = x1.shape
    half = yg.shape[2]
    tile = lambda n: pl.BlockSpec((tb, n), lambda i: (i + first_tile, 0))
    in_specs = [pl.BlockSpec((TOP_K, tb, half), lambda i: (0, i, 0)), tile(d), tile(LANES),
                pl.BlockSpec((1, d), lambda i: (0, 0))]
    args = [yg, x1, wts, g_final.reshape(1, d)]
    aliases = {}
    if partial_out is not None:
        in_specs.append(pl.BlockSpec(memory_space=pl.ANY))
        args.append(partial_out)
        aliases = {len(args) - 1: 0}
    return pl.pallas_call(
        _combine_kernel,
        grid=(yg.shape[1] // tb,),
        in_specs=in_specs,
        out_specs=tile(d),
        out_shape=jax.ShapeDtypeStruct((t, d), F32),
        input_output_aliases=aliases,
        compiler_params=_cparams(("parallel",)),
        name="combine",
    )(*args)


def _layer(x, mem, w_in, w_out, w_mem_kv, g_mix, g_mem, g_ffn, bias_tables, mu, w0, w_up, a0, a_up,
           g_up, k_k, k_a, r_k, gn_g, gn_b, w_router, b_router, w_gate_up, b_gate_up, w_down, b_down,
           g_last):
    b, s, d = x.shape
    m = mem.shape[1]
    t = b * s
    x2 = x.reshape(t, d)

    p = _norm_matmul(x2, g_mix, w_in.astype(BF16), 512, "in_proj")
    p3 = p.reshape(b, s, IN_COLS)
    y_moba = _moba_attention(p3, *bias_tables)
    y_rwkv = _rwkv(p3, mu, w0, w_up, a0, a_up, g_up, k_k, k_a, r_k, gn_g, gn_b)
    mkv = _norm_matmul(mem.reshape(b * m, d), g_mem, w_mem_kv.astype(BF16), 512, "mem_kv")
    y_mem = _mem_attention(p3, mkv.reshape(b, m, 2 * MEM_W), 512)

    tb = ROUTE_TOKENS
    x1, h2, idx_o, rank_o, wgt_p, cnt = _out_router(
        x2, y_moba.reshape(t, MOBA_W), y_rwkv.reshape(t, RWKV_W), y_mem.reshape(t, MEM_W),
        w_out, g_ffn, w_router, b_router, tb)

    counts = cnt[:, 0].astype(I32)
    padded = (counts + EXPERT_ROWS - 1) // EXPERT_ROWS * EXPERT_ROWS
    pad_ends = jnp.cumsum(padded)
    pad_starts = (pad_ends - padded).astype(I32)
    n_blocks = (t * TOP_K) // EXPERT_ROWS + N_EXPERTS
    blk_start = jnp.arange(n_blocks, dtype=I32) * EXPERT_ROWS
    blk_expert = jnp.minimum(jnp.sum(blk_start[:, None] >= pad_ends[None, :], axis=1),
                             N_EXPERTS - 1).astype(I32)
    blk_valid = jnp.clip(counts[blk_expert] - (blk_start - pad_starts[blk_expert]), 0, EXPERT_ROWS)
    n_used = (pad_ends[-1:] // EXPERT_ROWS).astype(I32)
    has_rows = counts > 0
    first_from = lax.cummin(jnp.where(has_rows, jnp.arange(N_EXPERTS, dtype=I32), N_EXPERTS), reverse=True)
    next_expert = jnp.concatenate([first_from[1:], jnp.full((1,), N_EXPERTS, I32)])
    next_expert = jnp.where(next_expert < N_EXPERTS, next_expert, -1).astype(I32)
    buf_slot = ((jnp.cumsum(has_rows.astype(I32)) - has_rows.astype(I32)) % 2).astype(I32)
    idx_kt = jnp.swapaxes(idx_o[:, :TOP_K, :], 0, 1).reshape(TOP_K, t)
    rank_kt = jnp.swapaxes(rank_o[:, :TOP_K, :], 0, 1).reshape(TOP_K, t)
    start_kt = jnp.zeros((TOP_K, t), I32)
    for e in range(N_EXPERTS):
        start_kt = jnp.where(idx_kt == e, pad_starts[e], start_kt)
    dest = (start_kt + rank_kt).reshape(TOP_K * t)

    xs = _sc_scatter_rows(h2, dest, n_blocks * EXPERT_ROWS)
    ys = _experts(blk_expert, blk_valid.astype(I32), n_used, next_expert, buf_slot, xs,
                  w_gate_up, b_gate_up, w_down, b_down)
    dest_kt = dest.reshape(TOP_K, t)
    out = None
    for part in range(COMBINE_PARTS):
        t_part = t // COMBINE_PARTS
        rows = dest_kt[:, part * t_part:(part + 1) * t_part].reshape(TOP_K * t_part)
        yg = _sc_gather_rows(ys, rows).reshape(TOP_K, t_part, d // 2)
        out = _combine(yg, x1, wgt_p, g_last, tb, part * (t_part // tb), out)
    return out.reshape(b, s, d)


def kernel(x, mem, w_in, w_out, w_mem_kv, g_mix, g_mem, g_ffn, g_final, rel_bias, rwkv_mu, rwkv_w0,
           rwkv_w_up, rwkv_a0, rwkv_a_up, rwkv_g_up, rwkv_k_k, rwkv_k_a, rwkv_r_k, rwkv_gn_g, rwkv_gn_b,
           w_router, b_router, w_gate_up, b_gate_up, w_down, b_down):
    depth = w_in.shape[0]
    assert depth == 1, "the final norm is fused into the last layer's combine kernel"
    bias_tables = _moba_bias_tables(rel_bias)
    l = 0
    return _layer(x, mem, w_in[l], w_out[l], w_mem_kv[l], g_mix[l], g_mem[l], g_ffn[l], bias_tables,
                  rwkv_mu[l], rwkv_w0[l], rwkv_w_up[l], rwkv_a0[l], rwkv_a_up[l], rwkv_g_up[l],
                  rwkv_k_k[l], rwkv_k_a[l], rwkv_r_k[l], rwkv_gn_g[l], rwkv_gn_b[l], w_router[l],
                  b_router[l], w_gate_up[l], b_gate_up[l], w_down[l], b_down[l], g_final)
```

```python
import functools
import math

import numpy as np
import jax
import jax.numpy as jnp
from jax import lax
from jax.experimental import pallas as pl
from jax.experimental.pallas import tpu as pltpu
from jax.experimental.pallas import tpu_sc as plsc

F32 = jnp.float32
BF16 = jnp.bfloat16
I32 = jnp.int32
HI = lax.Precision.HIGHEST

D_MODEL = 1024
HEAD_DIM = 64
MOBA_HEADS = 6
RWKV_HEADS = 6
MEM_HEADS = 4
MOBA_W = MOBA_HEADS * HEAD_DIM
RWKV_W = RWKV_HEADS * HEAD_DIM
MEM_W = MEM_HEADS * HEAD_DIM
MOBA_BLOCK = 256
MOBA_TOPK = 3
N_BUCKETS = 32
MAX_DISTANCE = 128
DECAY_LORA = 64
AAA_LORA = 64
GATE_LORA = 128
RWKV_COLS = 3 * RWKV_W + DECAY_LORA + AAA_LORA + GATE_LORA
RWKV_GN_EPS = 64e-5
IN_COLS = 3 * MOBA_W + RWKV_COLS + MEM_W
N_EXPERTS = 32
TOP_K = 4
D_EXPERT = D_MODEL
SWIGLU_ALPHA = 1.702
SWIGLU_LIMIT = 7.0
RMS_EPS = 1e-5

LANES = 128
RWKV_CHUNK = 128
RWKV_SEQS_PER_STEP = 4
EXPERT_ROWS = 256
ROUTE_TOKENS = 512
SC_CORES = 2
SC_SUBCORES = 16
SC_ROWS = 64
SC_INFLIGHT = 2
COMBINE_PARTS = 2
VMEM_LIMIT = 56 * 1024 * 1024
NEG_INF = float("-inf")
HIGH_HALF = -65536
LOG2_E = 1.4426950408889634
MOBA_VROWS = HEAD_DIM + 16


def _cparams(sem):
    return pltpu.CompilerParams(dimension_semantics=sem, vmem_limit_bytes=VMEM_LIMIT)


def _rms(x, g):
    return x * lax.rsqrt(jnp.mean(x * x, axis=-1, keepdims=True) + RMS_EPS) * g


def _bdot(a, b):
    return jnp.dot(a.astype(BF16), b.astype(BF16), preferred_element_type=F32)


def _hdot(a, b):
    return jnp.dot(a, b, preferred_element_type=F32, precision=HI)


def _dot_nt(a, b):
    return lax.dot_general(a, b, (((1,), (1,)), ((), ())), preferred_element_type=F32)


def _dot_tn(a, b):
    return lax.dot_general(a, b, (((0,), (0,)), ((), ())), preferred_element_type=F32)


def _split_bf16(x, terms):
    parts = []
    for _ in range(terms):
        hi = x.astype(BF16)
        parts.append(hi)
        x = x - hi.astype(F32)
    return parts


def _dot_exact_rhs(x, m_bf16, terms):
    acc = None
    for part in _split_bf16(x, terms):
        d = jnp.dot(part, m_bf16, preferred_element_type=F32)
        acc = d if acc is None else acc + d
    return acc


def _pack_halves(x):
    n = x.shape[1] // 2
    lo = pltpu.bitcast(x[:, :n].astype(BF16).astype(F32), I32)
    hi = pltpu.bitcast(x[:, n:].astype(BF16).astype(F32), I32)
    return (hi & HIGH_HALF) | lax.shift_right_logical(lo, 16)


def _unpack_halves(w):
    lo = pltpu.bitcast(w << 16, F32)
    hi = pltpu.bitcast(w & HIGH_HALF, F32)
    return jnp.concatenate([lo, hi], axis=1)


def _norm_matmul_kernel(x_ref, g_ref, w_ref, o_ref):
    h = _rms(x_ref[...], g_ref[...])
    o_ref[...] = jnp.dot(h.astype(BF16), w_ref[...], preferred_element_type=F32)


def _norm_matmul(x, g, w_bf16, tm, name):
    t, d = x.shape
    n = w_bf16.shape[1]
    tm = min(tm, t)
    return pl.pallas_call(
        _norm_matmul_kernel,
        grid=(t // tm,),
        in_specs=[pl.BlockSpec((tm, d), lambda i: (i, 0)),
                  pl.BlockSpec((1, d), lambda i: (0, 0)),
                  pl.BlockSpec((d, n), lambda i: (0, 0))],
        out_specs=pl.BlockSpec((tm, n), lambda i: (i, 0)),
        out_shape=jax.ShapeDtypeStruct((t, n), F32),
        compiler_params=_cparams(("parallel",)),
        name=name,
    )(x, g.reshape(1, d), w_bf16)


def _t5_bucket_np(dist):
    n = np.maximum(dist, 0)
    max_exact = N_BUCKETS // 2
    nf = np.maximum(n, 1).astype(np.float64)
    large = max_exact + (np.log(nf / max_exact) / math.log(MAX_DISTANCE / max_exact)
                         * (N_BUCKETS - max_exact)).astype(np.int64)
    large = np.minimum(large, N_BUCKETS - 1)
    return np.where(n < max_exact, n, large)


def _moba_bias_tables(rel_bias):
    n = MOBA_BLOCK
    assert np.all(_t5_bucket_np(np.arange(n + 1, 64 * n)) == N_BUCKETS - 1)
    bias_t = rel_bias.astype(F32).T * LOG2_E

    def by_distance(dist):
        bucket = jnp.asarray(_t5_bucket_np(dist), I32)[None]
        tab = jnp.zeros((MOBA_HEADS, dist.shape[0]), F32)
        for b in range(N_BUCKETS):
            tab = jnp.where(bucket == b, bias_t[:, b][:, None], tab)
        return tab

    d = np.arange(2 * n)
    g_own = jnp.where(jnp.asarray(d < n)[None], by_distance(np.where(d < n, d, 0)), NEG_INF)
    g_prev = by_distance(np.where(d < n, n + d, d - n))
    far = bias_t[:, N_BUCKETS - 1]
    return g_own[:, None, :], g_prev[:, None, :], far


def _moba_kernel(far_ref, q_ref, k_ref, v_ref, g0_ref, g1_ref, o_ref,
                 qt_s, kb_s, vt_s, sc_s, t0_ref, t1_ref, *, nb):
    hp = pl.program_id(1)
    blk = MOBA_BLOCK
    scale = HEAD_DIM ** -0.5 * LOG2_E
    row_head = lax.broadcasted_iota(I32, (LANES, 1), 0) // HEAD_DIM
    lane_head = lax.broadcasted_iota(I32, (1, LANES), 1) // HEAD_DIM
    ones_rows = jnp.ones((MOBA_VROWS - HEAD_DIM, blk), BF16)

    for h in range(2):
        for g_ref, t_ref in ((g0_ref, t0_ref), (g1_ref, t1_ref)):
            rows = jnp.broadcast_to(g_ref[h], (blk, 2 * blk))
            t_ref[h] = pltpu.roll(rows, 0, 1, stride=1, stride_axis=0)[:, :blk]

    kmean_rows = []
    for j in range(nb):
        sl = pl.ds(j * blk, blk)
        qt_s[j] = (q_ref[0, sl, :] * scale).T
        vt = v_ref[0, sl, :].T.astype(BF16)
        for h in range(2):
            vt_s[j, h * MOBA_VROWS:h * MOBA_VROWS + HEAD_DIM, :] = vt[h * HEAD_DIM:(h + 1) * HEAD_DIM, :]
            vt_s[j, h * MOBA_VROWS + HEAD_DIM:(h + 1) * MOBA_VROWS, :] = ones_rows
        kj = k_ref[0, sl, :]
        kb_s[j] = kj.astype(BF16)
        kmean_rows.append(jnp.mean(kj, axis=0, keepdims=True))
    kmean = jnp.concatenate(kmean_rows, axis=0)
    km2 = jnp.concatenate([jnp.where(lane_head == 0, kmean, 0.0),
                           jnp.where(lane_head == 1, kmean, 0.0)], axis=0)
    blk_iota = lax.broadcasted_iota(I32, (nb, blk), 0)

    for i in range(nb):
        qt = qt_s[i]
        negs = [None, None]
        if i > MOBA_TOPK:
            gate2 = _hdot(km2, qt)
            for h in range(2):
                g = gate2[h * nb:(h + 1) * nb, :]
                cnt = jnp.zeros((nb, blk), I32)
                for m in range(i):
                    gm = g[m:m + 1, :]
                    beats = (gm > g) | ((gm == g) & (m < blk_iota))
                    cnt = cnt + jnp.where(beats, 1, 0)
                negs[h] = jnp.where(cnt < MOBA_TOPK, 0.0, NEG_INF)
        outs = []
        for h in range(2):
            qt_h = jnp.where(row_head == h, qt, 0.0).astype(BF16)
            far_bias = far_ref[2 * hp + h]
            m_run = None
            for j in range(i + 1):
                s = jnp.dot(kb_s[j], qt_h, preferred_element_type=F32)
                if j == i:
                    s = s + t0_ref[h]
                else:
                    bias = t1_ref[h] if j == i - 1 else far_bias
                    if negs[h] is not None:
                        bias = bias + negs[h][j:j + 1, :]
                    s = s + bias
                sc_s[h, j] = s
                cm = jnp.max(s, axis=0, keepdims=True)
                m_run = cm if m_run is None else jnp.maximum(m_run, cm)
            acc = jnp.zeros((MOBA_VROWS, blk), F32)
            for j in range(i + 1):
                p_t = jnp.exp2(sc_s[h, j] - m_run)
                vt_h = vt_s[j, h * MOBA_VROWS:(h + 1) * MOBA_VROWS, :]
                acc = acc + jnp.dot(vt_h, p_t.astype(BF16), preferred_element_type=F32)
            outs.append(acc[:HEAD_DIM] / acc[HEAD_DIM:HEAD_DIM + 1])
        o_t = jnp.concatenate(outs, axis=0)
        o_ref[0, pl.ds(i * blk, blk), :] = o_t.T


def _moba_attention(p3, g_own, g_prev, far):
    b, s, _ = p3.shape
    nb = s // MOBA_BLOCK
    n_pairs = MOBA_HEADS // 2
    blk_spec = lambda off: pl.BlockSpec((1, s, LANES), lambda bi, hp: (bi, 0, off + hp))
    tab_spec = pl.BlockSpec((2, 1, 2 * MOBA_BLOCK), lambda bi, hp: (hp, 0, 0))
    return pl.pallas_call(
        functools.partial(_moba_kernel, nb=nb),
        grid=(b, n_pairs),
        in_specs=[pl.BlockSpec(memory_space=pltpu.SMEM),
                  blk_spec(0), blk_spec(n_pairs), blk_spec(2 * n_pairs), tab_spec, tab_spec],
        out_specs=pl.BlockSpec((1, s, LANES), lambda bi, hp: (bi, 0, hp)),
        out_shape=jax.ShapeDtypeStruct((b, s, MOBA_W), F32),
        scratch_shapes=[pltpu.VMEM((nb, LANES, MOBA_BLOCK), F32),
                        pltpu.VMEM((nb, MOBA_BLOCK, LANES), BF16),
                        pltpu.VMEM((nb, 2 * MOBA_VROWS, MOBA_BLOCK), BF16),
                        pltpu.VMEM((2, nb, MOBA_BLOCK, MOBA_BLOCK), F32),
                        pltpu.VMEM((2, MOBA_BLOCK, MOBA_BLOCK), F32),
                        pltpu.VMEM((2, MOBA_BLOCK, MOBA_BLOCK), F32)],
        compiler_params=_cparams(("parallel", "parallel")),
        name="moba",
    )(far, p3, p3, p3, g_own, g_prev)


def _softplus(z):
    return jnp.maximum(z, 0.0) + jnp.log(1.0 + jnp.exp(-jnp.abs(z)))


def _sigmoid(z):
    return 1.0 / (1.0 + jnp.exp(-z))


def _rwkv_kernel(r_ref, k_ref, v_ref, wa_ref, g_ref,
                 mu_r_ref, mu_k_ref, mu_v_ref, mu_wa_ref, mu_g_ref,
                 w0_ref, wup_ref, a0_ref, aup_ref, gup_ref, kk_ref, ka_ref, rk_ref,
                 gng_ref, gnb_ref, hsum_ref, tri_ref,
                 o_ref, st_s, prev_r, prev_k, prev_v, prev_wa, prev_g, *, n_seq):
    c = RWKV_CHUNK
    mid = c // 2

    @pl.when(pl.program_id(1) == 0)
    def _():
        st_s[...] = jnp.zeros_like(st_s)
        prev_r[...] = jnp.zeros_like(prev_r)
        prev_k[...] = jnp.zeros_like(prev_k)
        prev_v[...] = jnp.zeros_like(prev_v)
        prev_wa[...] = jnp.zeros_like(prev_wa)
        prev_g[...] = jnp.zeros_like(prev_g)

    n_pairs = RWKV_HEADS // 2
    hsum = hsum_ref[...]
    tri = tri_ref[...]
    row = lax.broadcasted_iota(I32, (c, c), 0)
    col = lax.broadcasted_iota(I32, (c, c), 1)
    strict = row > col
    incl = row >= col
    eye = (row == col).astype(F32)
    lane_half = lax.broadcasted_iota(I32, (1, LANES), 1) // HEAD_DIM
    hmask = hsum.astype(F32)
    dot = functools.partial(jnp.dot, preferred_element_type=F32)
    lanes = lambda x, jp: x[:, jp * LANES:(jp + 1) * LANES]

    def group_sum(x):
        return jnp.concatenate([_dot_exact_rhs(lanes(x, jp), hsum, 2) for jp in range(n_pairs)], axis=1)

    def two_heads(x):
        return jnp.concatenate([jnp.where(lane_half == 0, x, 0.0), jnp.where(lane_half == 1, x, 0.0)],
                               axis=0).astype(BF16)

    def side(a, b):
        return jnp.concatenate([a, b], axis=1)

    def mix(g, x_ref, prev_ref, mu_ref):
        x = x_ref[g]
        rows = lax.broadcasted_iota(I32, x.shape, 0)
        shifted = jnp.where(rows == 0, prev_ref[g], pltpu.roll(x, 1, axis=0))
        prev_ref[g] = x[c - 1:c, :]
        return x + (shifted - x) * mu_ref[...]

    def prepare(g):
        r = mix(g, r_ref, prev_r, mu_r_ref)
        k = mix(g, k_ref, prev_k, mu_k_ref)
        v = mix(g, v_ref, prev_v, mu_v_ref)
        xwa = mix(g, wa_ref, prev_wa, mu_wa_ref)
        xg = mix(g, g_ref, prev_g, mu_g_ref)

        w = -_softplus(-(w0_ref[...] + _bdot(jnp.tanh(xwa), wup_ref[...]))) - 0.5
        logd = -jnp.exp(w)
        a = _sigmoid(a0_ref[...] + _bdot(xwa, aup_ref[...]))
        gate = _bdot(_sigmoid(xg), gup_ref[...])
        kk = k * kk_ref[...]
        kk = kk / jnp.maximum(jnp.sqrt(group_sum(kk * kk)), 1e-12)
        k2 = k * (1.0 + (a - 1.0) * ka_ref[...])

        parts = _split_bf16(logd, 3)
        cum = dot(tri, parts[0]) + dot(tri, parts[1]) + dot(tri, parts[2])
        cum_last = cum[c - 1:c, :]
        ref = cum[mid - 1:mid, :]
        rel = cum - ref
        g_in = jnp.exp(rel)
        g_inv = jnp.exp(-rel)
        g_tail = jnp.exp(cum_last - cum)
        a_t = -kk * jnp.exp(rel - logd)
        b_t = kk * a * g_inv
        k_t = k2 * g_inv
        r_t = r * g_in
        b_hat = kk * a * g_tail
        k_hat = k2 * g_tail

        ref_scale = jnp.exp(ref)
        decay = jnp.exp(cum_last)
        bk = jnp.concatenate([b_t, k_t], axis=0).astype(BF16)
        ar_all = jnp.concatenate([a_t, r_t], axis=0)
        per_pair = []
        for jp in range(n_pairs):
            st = st_s[g, jp]
            st_ref = (st * lanes(ref_scale, jp)).astype(BF16)
            both = _dot_nt(lanes(ar_all, jp).astype(BF16), st_ref)
            quads = []
            for half in range(2):
                ar = jnp.where(lane_half == half, lanes(ar_all, jp), 0.0)
                m4 = _dot_nt(ar.astype(BF16), lanes(bk, jp))
                quads.append((jnp.where(strict, m4[:c, :c], 0.0),
                              jnp.where(strict, m4[:c, c:], 0.0).astype(BF16),
                              jnp.where(incl, m4[c:, :c], 0.0).astype(BF16),
                              jnp.where(incl, m4[c:, c:], 0.0).astype(BF16)))
            per_pair.append(dict(
                st=st, p0=both[:c], o0=both[c:], v_rows=two_heads(lanes(v, jp)),
                l_ab=[q[0] for q in quads], l_ak=side(quads[0][1], quads[1][1]),
                m_rb=side(quads[0][2], quads[1][2]), m_rk=side(quads[0][3], quads[1][3]),
                v=lanes(v, jp).astype(BF16), b_hat=lanes(b_hat, jp).astype(BF16),
                k_hat=lanes(k_hat, jp).astype(BF16), decay=lanes(decay, jp)))
        return dict(r=r, k2=k2, v=v, gate=gate, pairs=per_pair)

    seqs = [prepare(g) for g in range(n_seq)]
    heads = [(g, jp, half) for g in range(n_seq) for jp in range(n_pairs) for half in range(2)]

    pw = {gh: seqs[gh[0]]["pairs"][gh[1]]["l_ab"][gh[2]] for gh in heads}
    t_inv = {gh: eye + pw[gh] for gh in heads}
    pw = {gh: pw[gh].astype(BF16) for gh in heads}
    span = 2
    while span < c:
        pw = {gh: dot(pw[gh], pw[gh]).astype(BF16) for gh in heads}
        t_inv = {gh: t_inv[gh] + dot(t_inv[gh].astype(BF16), pw[gh]) for gh in heads}
        span *= 2

    for g in range(n_seq):
        sq = seqs[g]
        outs = []
        for jp in range(n_pairs):
            pr = sq["pairs"][jp]
            t_pair = side(t_inv[(g, jp, 0)], t_inv[(g, jp, 1)]).astype(BF16)
            w_loc = pr["p0"] + dot(pr["l_ak"], pr["v_rows"])
            u = dot(t_pair, two_heads(w_loc))
            outs.append(pr["o0"] + dot(pr["m_rb"], two_heads(u)) + dot(pr["m_rk"], pr["v_rows"]))
            upd = _dot_tn(u.astype(BF16), pr["b_hat"]) + _dot_tn(pr["v"], pr["k_hat"])
            st_s[g, jp] = pr["st"] * pr["decay"] + hmask * upd
        o = jnp.concatenate(outs, axis=1)

        inv_n = 1.0 / HEAD_DIM
        mean = group_sum(o) * inv_n
        dev = o - mean
        var = group_sum(dev * dev) * inv_n
        y = dev * lax.rsqrt(var + RWKV_GN_EPS) * gng_ref[...] + gnb_ref[...]
        y = y + group_sum(sq["r"] * sq["k2"] * rk_ref[...]) * sq["v"]
        o_ref[g] = y * sq["gate"]


def _rwkv(p3, mu, w0, w_up, a0, a_up, g_up, k_k, k_a, r_k, gn_g, gn_b):
    b, s, _ = p3.shape
    c = RWKV_CHUNK
    w = RWKV_W
    n_seq = RWKV_SEQS_PER_STEP if b % RWKV_SEQS_PER_STEP == 0 else 1
    base = 3 * MOBA_W
    assert base % w == 0 and (base + 3 * w) % LANES == 0
    lora = DECAY_LORA + AAA_LORA

    def col_spec(width, off_cols):
        assert off_cols % width == 0
        return pl.BlockSpec((n_seq, c, width), lambda bi, ci: (bi, ci, off_cols // width))

    row = lambda x: x.reshape(1, -1).astype(F32)
    const = lambda shape: pl.BlockSpec(shape, lambda bi, ci: (0,) * len(shape))
    wup_pad = jnp.concatenate([w_up, jnp.zeros((AAA_LORA, w), F32)], axis=0).astype(BF16)
    aup_pad = jnp.concatenate([jnp.zeros((DECAY_LORA, w), F32), a_up], axis=0).astype(BF16)
    head = np.arange(LANES) // HEAD_DIM
    hsum = jnp.asarray((head[:, None] == head[None, :]).astype(np.float32)).astype(BF16)
    tri = jnp.asarray(np.tril(np.ones((c, c), np.float32))).astype(BF16)
    vec_args = [mu[:w], mu[w:2 * w], mu[2 * w:3 * w], mu[3 * w:3 * w + lora], mu[3 * w + lora:],
                w0, None, a0, None, None, k_k, k_a, r_k.reshape(-1), gn_g, gn_b]
    args = [p3, p3, p3, p3, p3]
    specs = [col_spec(w, base), col_spec(w, base + w), col_spec(w, base + 2 * w),
             col_spec(lora, base + 3 * w), col_spec(GATE_LORA, base + 3 * w + lora)]
    mats = {6: wup_pad, 8: aup_pad, 9: g_up.astype(BF16)}
    for idx, a in enumerate(vec_args):
        arr = mats[idx] if a is None else row(a)
        args.append(arr)
        specs.append(const(arr.shape))
    for arr in (hsum, tri):
        args.append(arr)
        specs.append(const(arr.shape))
    return pl.pallas_call(
        functools.partial(_rwkv_kernel, n_seq=n_seq),
        grid=(b // n_seq, s // c),
        in_specs=specs,
        out_specs=pl.BlockSpec((n_seq, c, w), lambda bi, ci: (bi, ci, 0)),
        out_shape=jax.ShapeDtypeStruct((b, s, w), F32),
        scratch_shapes=[pltpu.VMEM((n_seq, RWKV_HEADS // 2, LANES, LANES), F32),
                        pltpu.VMEM((n_seq, 1, w), F32),
                        pltpu.VMEM((n_seq, 1, w), F32), pltpu.VMEM((n_seq, 1, w), F32),
                        pltpu.VMEM((n_seq, 1, lora), F32), pltpu.VMEM((n_seq, 1, GATE_LORA), F32)],
        compiler_params=_cparams(("parallel", "arbitrary")),
        name="rwkv",
    )(*args)


def _mem_attn_kernel(q_ref, kv_ref, o_ref):
    scale = HEAD_DIM ** -0.5
    lane_head = lax.broadcasted_iota(I32, (1, MEM_W), 1) // HEAD_DIM
    q = q_ref[0] * scale
    mk = kv_ref[0, :, :MEM_W].astype(BF16)
    mv = kv_ref[0, :, MEM_W:].astype(BF16)
    out = jnp.zeros(q.shape, F32)
    for h in range(MEM_HEADS):
        hm = lane_head == h
        s = _dot_nt(jnp.where(hm, q, 0.0).astype(BF16), mk)
        s = s - jnp.max(s, axis=-1, keepdims=True)
        e = jnp.exp(s)
        p = e / jnp.sum(e, axis=-1, keepdims=True)
        out = jnp.where(hm, jnp.dot(p.astype(BF16), mv, preferred_element_type=F32), out)
    o_ref[0] = out


def _mem_attention(p3, mkv3, tq):
    b, s, _ = p3.shape
    m = mkv3.shape[1]
    off = (3 * MOBA_W + RWKV_COLS) // MEM_W
    assert off * MEM_W == 3 * MOBA_W + RWKV_COLS
    return pl.pallas_call(
        _mem_attn_kernel,
        grid=(b, s // tq),
        in_specs=[pl.BlockSpec((1, tq, MEM_W), lambda bi, i: (bi, i, off)),
                  pl.BlockSpec((1, m, 2 * MEM_W), lambda bi, i: (bi, 0, 0))],
        out_specs=pl.BlockSpec((1, tq, MEM_W), lambda bi, i: (bi, i, 0)),
        out_shape=jax.ShapeDtypeStruct((b, s, MEM_W), F32),
        compiler_params=_cparams(("parallel", "parallel")),
        name="mem_attn",
    )(p3, mkv3)


def _out_router_kernel(x_ref, ym_ref, yr_ref, ye_ref, wo1_ref, wo2_ref, wo3_ref, g_ref,
                       wrh_ref, wrl_ref, br_ref, upper_ref, ones_ref,
                       x1_ref, h_ref, idx_ref, rank_ref, wgt_ref, cnt_ref, run_s):
    @pl.when(pl.program_id(0) == 0)
    def _():
        run_s[...] = jnp.zeros_like(run_s)

    x1 = (x_ref[...] + _bdot(ym_ref[...], wo1_ref[...]) + _bdot(yr_ref[...], wo2_ref[...])
          + _bdot(ye_ref[...], wo3_ref[...]))
    x1_ref[...] = x1
    h = _rms(x1, g_ref[...])
    h_ref[...] = _pack_halves(h)
    tm = h.shape[0]
    h_hi, h_lo = _split_bf16(h, 2)
    dot = functools.partial(jnp.dot, preferred_element_type=F32)
    logits = dot(h_hi, wrh_ref[...]) + (dot(h_hi, wrl_ref[...]) + dot(h_lo, wrh_ref[...])) + br_ref[...]
    lg = logits.T[:N_EXPERTS, :]
    e_iota = lax.broadcasted_iota(I32, (N_EXPERTS, tm), 0)

    vals, idxs = [], []
    for _ in range(TOP_K):
        m = jnp.max(lg, axis=0, keepdims=True)
        idx = jnp.min(jnp.where(lg == m, e_iota, N_EXPERTS), axis=0, keepdims=True)
        vals.append(m)
        idxs.append(idx)
        lg = jnp.where(e_iota == idx, NEG_INF, lg)
    exps = [jnp.exp(vk - vals[0]) for vk in vals]
    denom = exps[0] + exps[1] + exps[2] + exps[3]

    chosen = jnp.zeros((N_EXPERTS, tm), F32)
    for idx in idxs:
        chosen = chosen + jnp.where(e_iota == idx, 1.0, 0.0)
    chosen = chosen.astype(BF16)
    run = run_s[...]
    before = dot(chosen, upper_ref[...]) + jnp.concatenate([run] * (tm // LANES), axis=1)
    run_s[...] = run + dot(chosen, ones_ref[...])
    cnt_ref[...] = run_s[...]

    zero_i = jnp.zeros((8 - TOP_K, tm), I32)
    ranks = [jnp.sum(jnp.where(e_iota == idx, before, 0.0), axis=0, keepdims=True).astype(I32)
             for idx in idxs]
    idx_ref[0] = jnp.concatenate(idxs + [zero_i], axis=0)
    rank_ref[0] = jnp.concatenate(ranks + [zero_i], axis=0)
    wrows = jnp.concatenate([e / denom for e in exps] + [jnp.zeros((LANES - TOP_K, tm), F32)], axis=0)
    wgt_ref[...] = wrows.T


def _out_router(x2, ym, yr, ye, w_out, g_ffn, w_router, b_router, tm):
    t, d = x2.shape
    wo = w_out.astype(BF16)
    wo1, wo2, wo3 = wo[:MOBA_W], wo[MOBA_W:MOBA_W + RWKV_W], wo[MOBA_W + RWKV_W:]
    wr = jnp.zeros((d, LANES), F32).at[:, :N_EXPERTS].set(w_router)
    wr_hi = wr.astype(BF16)
    wr_lo = (wr - wr_hi.astype(F32)).astype(BF16)
    br = jnp.full((1, LANES), NEG_INF, F32).at[0, :N_EXPERTS].set(b_router)
    upper = jnp.asarray(np.triu(np.ones((tm, tm), np.float32), 1)).astype(BF16)
    ones = jnp.ones((tm, LANES), BF16)
    tile = lambda n: pl.BlockSpec((tm, n), lambda i: (i, 0))
    slots = pl.BlockSpec((1, 8, tm), lambda i: (i, 0, 0))
    const = lambda a: pl.BlockSpec(a.shape, lambda i: (0,) * a.ndim)
    g2 = g_ffn.reshape(1, d)
    n = t // tm
    return pl.pallas_call(
        _out_router_kernel,
        grid=(n,),
        in_specs=[tile(d), tile(MOBA_W), tile(RWKV_W), tile(MEM_W), const(wo1), const(wo2), const(wo3),
                  const(g2), const(wr_hi), const(wr_lo), const(br), const(upper), const(ones)],
        out_specs=[tile(d), tile(d // 2), slots, slots, tile(LANES),
                   pl.BlockSpec((N_EXPERTS, LANES), lambda i: (0, 0))],
        out_shape=[jax.ShapeDtypeStruct((t, d), F32), jax.ShapeDtypeStruct((t, d // 2), I32),
                   jax.ShapeDtypeStruct((n, 8, tm), I32), jax.ShapeDtypeStruct((n, 8, tm), I32),
                   jax.ShapeDtypeStruct((t, LANES), F32), jax.ShapeDtypeStruct((N_EXPERTS, LANES), F32)],
        scratch_shapes=[pltpu.VMEM((N_EXPERTS, LANES), F32)],
        compiler_params=_cparams(("arbitrary",)),
        name="out_router",
    )(x2, ym, yr, ye, wo1, wo2, wo3, g2, wr_hi, wr_lo, br, upper, ones)


def _sc_mesh():
    return plsc.VectorSubcoreMesh(core_axis_name="c", subcore_axis_name="s",
                                  num_cores=SC_CORES, num_subcores=SC_SUBCORES)


def _sc_split(n_rows):
    n_workers = SC_CORES * SC_SUBCORES
    per_worker = n_rows // n_workers
    assert per_worker * n_workers == n_rows and per_worker % (SC_ROWS * SC_INFLIGHT) == 0
    assert SC_ROWS % 8 == 0
    return per_worker


def _sc_gather_rows(table, idx):
    n_rows = idx.shape[0]
    width = table.shape[1]
    per_worker = _sc_split(n_rows)

    buf = lambda shape, dtype: [pltpu.VMEM(shape, dtype) for _ in range(SC_INFLIGHT)]

    @functools.partial(
        pl.kernel, mesh=_sc_mesh(),
        out_type=jax.ShapeDtypeStruct((n_rows, width), table.dtype),
        scratch_types=[buf((SC_ROWS,), I32), buf((SC_ROWS, width), table.dtype),
                       [pltpu.SemaphoreType.DMA for _ in range(SC_INFLIGHT)]],
        name="sc_gather",
    )
    def gather(table_hbm, idx_hbm, out_hbm, idx_v, rows_v, sems):
        worker = lax.axis_index("s") * SC_CORES + lax.axis_index("c")
        base = worker * per_worker

        @pl.loop(0, per_worker // (SC_ROWS * SC_INFLIGHT))
        def _(step):
            offs = [pl.multiple_of(base + (step * SC_INFLIGHT + b) * SC_ROWS, 8) for b in range(SC_INFLIGHT)]
            copies = []
            for b in range(SC_INFLIGHT):
                pltpu.sync_copy(idx_hbm.at[pl.ds(offs[b], SC_ROWS)], idx_v[b])
                copies.append(pltpu.async_copy(table_hbm.at[idx_v[b]], rows_v[b], sems[b]))
            for b in range(SC_INFLIGHT):
                copies[b].wait()
                pltpu.sync_copy(rows_v[b], out_hbm.at[pl.ds(offs[b], SC_ROWS)])

    return gather(table, idx)


def _sc_scatter_rows(rows, idx, n_out):
    n_idx = idx.shape[0]
    n_src, width = rows.shape
    per_worker = _sc_split(n_idx)
    assert n_src % SC_ROWS == 0

    buf = lambda shape, dtype: [pltpu.VMEM(shape, dtype) for _ in range(SC_INFLIGHT)]

    @functools.partial(
        pl.kernel, mesh=_sc_mesh(),
        out_type=jax.ShapeDtypeStruct((n_out, width), rows.dtype),
        scratch_types=[buf((SC_ROWS,), I32), buf((SC_ROWS, width), rows.dtype),
                       [pltpu.SemaphoreType.DMA for _ in range(SC_INFLIGHT)]],
        name="sc_scatter",
    )
    def scatter(rows_hbm, idx_hbm, out_hbm, idx_v, rows_v, sems):
        worker = lax.axis_index("s") * SC_CORES + lax.axis_index("c")
        base = worker * per_worker

        @pl.loop(0, per_worker // (SC_ROWS * SC_INFLIGHT))
        def _(step):
            copies = []
            for b in range(SC_INFLIGHT):
                off = pl.multiple_of(base + (step * SC_INFLIGHT + b) * SC_ROWS, 8)
                src = pl.multiple_of(lax.rem(off, n_src), 8)
                pltpu.sync_copy(idx_hbm.at[pl.ds(off, SC_ROWS)], idx_v[b])
                pltpu.sync_copy(rows_hbm.at[pl.ds(src, SC_ROWS)], rows_v[b])
                copies.append(pltpu.async_copy(rows_v[b], out_hbm.at[idx_v[b]], sems[b]))
            for cp in copies:
                cp.wait()

    return scatter(rows, idx)


def _experts_kernel(be_ref, bv_ref, nused_ref, next_ref, slot_ref,
                    xs_ref, wgu_hbm, bg_ref, bu_ref, wd_hbm, bd_ref, perm_ref,
                    o_ref, wgu_buf, wd_buf, wg_s, wu_s, wd_s, sem):
    i = pl.program_id(0)
    e = be_ref[i]
    used = i < nused_ref[0]
    changed = ((i == 0) | (e != be_ref[jnp.maximum(i - 1, 0)])) & used
    slot = slot_ref[e]

    def weight_copies(expert, buf_slot):
        return (pltpu.make_async_copy(wgu_hbm.at[expert], wgu_buf.at[buf_slot], sem.at[buf_slot, 0]),
                pltpu.make_async_copy(wd_hbm.at[expert], wd_buf.at[buf_slot], sem.at[buf_slot, 1]))

    @pl.when((i == 0) & used)
    def _():
        for cp in weight_copies(e, slot):
            cp.start()

    @pl.when(changed)
    def _():
        for cp in weight_copies(e, slot):
            cp.wait()
        nxt = next_ref[e]

        @pl.when(nxt >= 0)
        def _():
            for cp in weight_copies(nxt, 1 - slot):
                cp.start()

        half = LANES
        for cblk in range(2 * D_EXPERT // (2 * half)):
            wt = wgu_buf[slot, :, cblk * 2 * half:(cblk + 1) * 2 * half].astype(BF16)
            sep = jnp.dot(wt, perm_ref[...], preferred_element_type=F32).astype(BF16)
            wg_s[:, cblk * half:(cblk + 1) * half] = sep[:, :half]
            wu_s[:, cblk * half:(cblk + 1) * half] = sep[:, half:]
        wd_s[...] = wd_buf[slot].astype(BF16)

    @pl.when(used)
    def _():
        rows = lax.broadcasted_iota(I32, xs_ref.shape, 0)
        xb = _unpack_halves(jnp.where(rows < bv_ref[i], xs_ref[...], 0)).astype(BF16)
        gate = jnp.dot(xb, wg_s[...], preferred_element_type=F32) + bg_ref[0]
        up = jnp.dot(xb, wu_s[...], preferred_element_type=F32) + bu_ref[0]
        gate = jnp.minimum(gate, SWIGLU_LIMIT)
        up = jnp.clip(up, -SWIGLU_LIMIT, SWIGLU_LIMIT)
        glu = gate * _sigmoid(gate * SWIGLU_ALPHA)
        act = ((up + 1.0) * glu).astype(BF16)
        o_ref[...] = _pack_halves(jnp.dot(act, wd_s[...], preferred_element_type=F32) + bd_ref[0])

    @pl.when(jnp.logical_not(used))
    def _():
        o_ref[...] = jnp.zeros_like(o_ref)


def _experts(blk_expert, blk_valid, n_used, next_expert, buf_slot, xs, w_gate_up, b_gate_up, w_down,
             b_down):
    p_rows = xs.shape[0]
    d = 2 * xs.shape[1]
    n_blocks = p_rows // EXPERT_ROWS
    bg = b_gate_up[:, 0::2].reshape(N_EXPERTS, 1, D_EXPERT)
    bu = b_gate_up[:, 1::2].reshape(N_EXPERTS, 1, D_EXPERT)
    bd = b_down.reshape(N_EXPERTS, 1, d)
    perm_np = np.zeros((2 * LANES, 2 * LANES), np.float32)
    perm_np[2 * np.arange(LANES), np.arange(LANES)] = 1.0
    perm_np[2 * np.arange(LANES) + 1, LANES + np.arange(LANES)] = 1.0
    perm = jnp.asarray(perm_np).astype(BF16)
    by_expert = lambda shape: pl.BlockSpec((1,) + shape, lambda i, be, *_: (be[i], 0, 0))
    grid_spec = pltpu.PrefetchScalarGridSpec(
        num_scalar_prefetch=5,
        grid=(n_blocks,),
        in_specs=[pl.BlockSpec((EXPERT_ROWS, d // 2), lambda i, *_: (i, 0)),
                  pl.BlockSpec(memory_space=pl.ANY), by_expert((1, D_EXPERT)), by_expert((1, D_EXPERT)),
                  pl.BlockSpec(memory_space=pl.ANY), by_expert((1, d)),
                  pl.BlockSpec(perm.shape, lambda i, *_: (0, 0))],
        out_specs=pl.BlockSpec((EXPERT_ROWS, d // 2), lambda i, *_: (i, 0)),
        scratch_shapes=[pltpu.VMEM((2, d, 2 * D_EXPERT), F32), pltpu.VMEM((2, D_EXPERT, d), F32),
                        pltpu.VMEM((d, D_EXPERT), BF16), pltpu.VMEM((d, D_EXPERT), BF16),
                        pltpu.VMEM((D_EXPERT, d), BF16), pltpu.SemaphoreType.DMA((2, 2))],
    )
    return pl.pallas_call(
        _experts_kernel,
        grid_spec=grid_spec,
        out_shape=jax.ShapeDtypeStruct((p_rows, d // 2), I32),
        compiler_params=_cparams(("arbitrary",)),
        name="experts",
    )(blk_expert, blk_valid, n_used, next_expert, buf_slot, xs, w_gate_up, bg, bu, w_down, bd, perm)


def _combine_kernel(yg_ref, x1_ref, w_ref, g_ref, *rest):
    o_ref = rest[-1]
    acc = x1_ref[...]
    wts = w_ref[...]
    for kk in range(TOP_K):
        acc = acc + _unpack_halves(yg_ref[kk]) * wts[:, kk:kk + 1]
    o_ref[...] = _rms(acc, g_ref[...])


def _combine(yg, x1, wts, g_final, tb, first_tile, partial_out):
    t, d = x1.shape
    half = yg.shape[2]
    tile = lambda n: pl.BlockSpec((tb, n), lambda i: (i + first_tile, 0))
    in_specs = [pl.BlockSpec((TOP_K, tb, half), lambda i: (0, i, 0)), tile(d), tile(LANES),
                pl.BlockSpec((1, d), lambda i: (0, 0))]
    args = [yg, x1, wts, g_final.reshape(1, d)]
    aliases = {}
    if partial_out is not None:
        in_specs.append(pl.BlockSpec(memory_space=pl.ANY))
        args.append(partial_out)
        aliases = {len(args) - 1: 0}
    return pl.pallas_call(
        _combine_kernel,
        grid=(yg.shape[1] // tb,),
        in_specs=in_specs,
        out_specs=tile(d),
        out_shape=jax.ShapeDtypeStruct((t, d), F32),
        input_output_aliases=aliases,
        compiler_params=_cparams(("parallel",)),
        name="combine",
    )(*args)


def _layer(x, mem, w_in, w_out, w_mem_kv, g_mix, g_mem, g_ffn, bias_tables, mu, w0, w_up, a0, a_up,
           g_up, k_k, k_a, r_k, gn_g, gn_b, w_router, b_router, w_gate_up, b_gate_up, w_down, b_down,
           g_last):
    b, s, d = x.shape
    m = mem.shape[1]
    t = b * s
    x2 = x.reshape(t, d)

    p = _norm_matmul(x2, g_mix, w_in.astype(BF16), 512, "in_proj")
    p3 = p.reshape(b, s, IN_COLS)
    y_moba = _moba_attention(p3, *bias_tables)
    y_rwkv = _rwkv(p3, mu, w0, w_up, a0, a_up, g_up, k_k, k_a, r_k, gn_g, gn_b)
    mkv = _norm_matmul(mem.reshape(b * m, d), g_mem, w_mem_kv.astype(BF16), 512, "mem_kv")
    y_mem = _mem_attention(p3, mkv.reshape(b, m, 2 * MEM_W), 512)

    tb = ROUTE_TOKENS
    x1, h2, idx_o, rank_o, wgt_p, cnt = _out_router(
        x2, y_moba.reshape(t, MOBA_W), y_rwkv.reshape(t, RWKV_W), y_mem.reshape(t, MEM_W),
        w_out, g_ffn, w_router, b_router, tb)

    counts = cnt[:, 0].astype(I32)
    padded = (counts + EXPERT_ROWS - 1) // EXPERT_ROWS * EXPERT_ROWS
    pad_ends = jnp.cumsum(padded)
    pad_starts = (pad_ends - padded).astype(I32)
    n_blocks = (t * TOP_K) // EXPERT_ROWS + N_EXPERTS
    blk_start = jnp.arange(n_blocks, dtype=I32) * EXPERT_ROWS
    blk_expert = jnp.minimum(jnp.sum(blk_start[:, None] >= pad_ends[None, :], axis=1),
                             N_EXPERTS - 1).astype(I32)
    of_block = blk_expert[:, None] == jnp.arange(N_EXPERTS, dtype=I32)[None, :]
    rows_left = jnp.sum(jnp.where(of_block, (counts + pad_starts)[None, :], 0), axis=1) - blk_start
    blk_valid = jnp.clip(rows_left, 0, EXPERT_ROWS)
    n_used = (pad_ends[-1:] // EXPERT_ROWS).astype(I32)
    has_rows = counts > 0
    first_from = lax.cummin(jnp.where(has_rows, jnp.arange(N_EXPERTS, dtype=I32), N_EXPERTS), reverse=True)
    next_expert = jnp.concatenate([first_from[1:], jnp.full((1,), N_EXPERTS, I32)])
    next_expert = jnp.where(next_expert < N_EXPERTS, next_expert, -1).astype(I32)
    buf_slot = ((jnp.cumsum(has_rows.astype(I32)) - has_rows.astype(I32)) % 2).astype(I32)
    idx_kt = jnp.swapaxes(idx_o[:, :TOP_K, :], 0, 1).reshape(TOP_K, t)
    rank_kt = jnp.swapaxes(rank_o[:, :TOP_K, :], 0, 1).reshape(TOP_K, t)
    experts = jnp.arange(N_EXPERTS, dtype=I32)[:, None, None]
    start_kt = jnp.sum(jnp.where(idx_kt[None] == experts, pad_starts[:, None, None], 0), axis=0)
    dest = (start_kt + rank_kt).reshape(TOP_K * t)

    xs = _sc_scatter_rows(h2, dest, n_blocks * EXPERT_ROWS)
    ys = _experts(blk_expert, blk_valid.astype(I32), n_used, next_expert, buf_slot, xs,
                  w_gate_up, b_gate_up, w_down, b_down)
    dest_kt = dest.reshape(TOP_K, t)
    out = None
    for part in range(COMBINE_PARTS):
        t_part = t // COMBINE_PARTS
        rows = dest_kt[:, part * t_part:(part + 1) * t_part].reshape(TOP_K * t_part)
        yg = _sc_gather_rows(ys, rows).reshape(TOP_K, t_part, d // 2)
        out = _combine(yg, x1, wgt_p, g_last, tb, part * (t_part // tb), out)
    return out.reshape(b, s, d)


def kernel(x, mem, w_in, w_out, w_mem_kv, g_mix, g_mem, g_ffn, g_final, rel_bias, rwkv_mu, rwkv_w0,
           rwkv_w_up, rwkv_a0, rwkv_a_up, rwkv_g_up, rwkv_k_k, rwkv_k_a, rwkv_r_k, rwkv_gn_g, rwkv_gn_b,
           w_router, b_router, w_gate_up, b_gate_up, w_down, b_down):
    depth = w_in.shape[0]
    assert depth == 1, "the final norm is fused into the last layer's combine kernel"
    bias_tables = _moba_bias_tables(rel_bias)
    l = 0
    return _layer(x, mem, w_in[l], w_out[l], w_mem_kv[l], g_mix[l], g_mem[l], g_ffn[l], bias_tables,
                  rwkv_mu[l], rwkv_w0[l], rwkv_w_up[l], rwkv_a0[l], rwkv_a_up[l], rwkv_g_up[l],
                  rwkv_k_k[l], rwkv_k_a[l], rwkv_r_k[l], rwkv_gn_g[l], rwkv_gn_b[l], w_router[l],
                  b_router[l], w_gate_up[l], b_gate_up[l], w_down[l], b_down[l], g_final)
```

```python
import functools
import math

import numpy as np
import jax
import jax.numpy as jnp
from jax import lax
from jax.experimental import pallas as pl
from jax.experimental.pallas import tpu as pltpu
from jax.experimental.pallas import tpu_sc as plsc

F32 = jnp.float32
BF16 = jnp.bfloat16
I32 = jnp.int32
HI = lax.Precision.HIGHEST

D_MODEL = 1024
HEAD_DIM = 64
MOBA_HEADS = 6
RWKV_HEADS = 6
MEM_HEADS = 4
MOBA_W = MOBA_HEADS * HEAD_DIM
RWKV_W = RWKV_HEADS * HEAD_DIM
MEM_W = MEM_HEADS * HEAD_DIM
MOBA_BLOCK = 256
MOBA_TOPK = 3
N_BUCKETS = 32
MAX_DISTANCE = 128
DECAY_LORA = 64
AAA_LORA = 64
GATE_LORA = 128
RWKV_COLS = 3 * RWKV_W + DECAY_LORA + AAA_LORA + GATE_LORA
RWKV_GN_EPS = 64e-5
IN_COLS = 3 * MOBA_W + RWKV_COLS + MEM_W
N_EXPERTS = 32
TOP_K = 4
D_EXPERT = D_MODEL
SWIGLU_ALPHA = 1.702
SWIGLU_LIMIT = 7.0
RMS_EPS = 1e-5

LANES = 128
RWKV_CHUNK = 128
RWKV_SEQS_PER_STEP = 4
EXPERT_ROWS = 256
ROUTE_TOKENS = 512
SC_CORES = 2
SC_SUBCORES = 16
SC_ROWS = 64
SC_INFLIGHT = 2
COMBINE_PARTS = 2
VMEM_LIMIT = 56 * 1024 * 1024
NEG_INF = float("-inf")
HIGH_HALF = -65536
LOG2_E = 1.4426950408889634
MOBA_VROWS = HEAD_DIM + 16


def _cparams(sem):
    return pltpu.CompilerParams(dimension_semantics=sem, vmem_limit_bytes=VMEM_LIMIT)


def _rms(x, g):
    return x * lax.rsqrt(jnp.mean(x * x, axis=-1, keepdims=True) + RMS_EPS) * g


def _bdot(a, b):
    return jnp.dot(a.astype(BF16), b.astype(BF16), preferred_element_type=F32)


def _hdot(a, b):
    return jnp.dot(a, b, preferred_element_type=F32, precision=HI)


def _dot_nt(a, b):
    return lax.dot_general(a, b, (((1,), (1,)), ((), ())), preferred_element_type=F32)


def _dot_tn(a, b):
    return lax.dot_general(a, b, (((0,), (0,)), ((), ())), preferred_element_type=F32)


def _split_bf16(x, terms):
    parts = []
    for _ in range(terms):
        hi = x.astype(BF16)
        parts.append(hi)
        x = x - hi.astype(F32)
    return parts


def _dot_exact_rhs(x, m_bf16, terms):
    acc = None
    for part in _split_bf16(x, terms):
        d = jnp.dot(part, m_bf16, preferred_element_type=F32)
        acc = d if acc is None else acc + d
    return acc


def _pack_halves(x):
    n = x.shape[1] // 2
    lo = pltpu.bitcast(x[:, :n].astype(BF16).astype(F32), I32)
    hi = pltpu.bitcast(x[:, n:].astype(BF16).astype(F32), I32)
    return (hi & HIGH_HALF) | lax.shift_right_logical(lo, 16)


def _unpack_halves(w):
    lo = pltpu.bitcast(w << 16, F32)
    hi = pltpu.bitcast(w & HIGH_HALF, F32)
    return jnp.concatenate([lo, hi], axis=1)


def _norm_matmul_kernel(x_ref, g_ref, w_ref, o_ref):
    h = _rms(x_ref[...], g_ref[...])
    o_ref[...] = jnp.dot(h.astype(BF16), w_ref[...], preferred_element_type=F32)


def _norm_matmul(x, g, w_bf16, tm, name):
    t, d = x.shape
    n = w_bf16.shape[1]
    tm = min(tm, t)
    return pl.pallas_call(
        _norm_matmul_kernel,
        grid=(t // tm,),
        in_specs=[pl.BlockSpec((tm, d), lambda i: (i, 0)),
                  pl.BlockSpec((1, d), lambda i: (0, 0)),
                  pl.BlockSpec((d, n), lambda i: (0, 0))],
        out_specs=pl.BlockSpec((tm, n), lambda i: (i, 0)),
        out_shape=jax.ShapeDtypeStruct((t, n), F32),
        compiler_params=_cparams(("parallel",)),
        name=name,
    )(x, g.reshape(1, d), w_bf16)


def _t5_bucket_np(dist):
    n = np.maximum(dist, 0)
    max_exact = N_BUCKETS // 2
    nf = np.maximum(n, 1).astype(np.float64)
    large = max_exact + (np.log(nf / max_exact) / math.log(MAX_DISTANCE / max_exact)
                         * (N_BUCKETS - max_exact)).astype(np.int64)
    large = np.minimum(large, N_BUCKETS - 1)
    return np.where(n < max_exact, n, large)


def _moba_bias_tables(rel_bias):
    n = MOBA_BLOCK
    assert np.all(_t5_bucket_np(np.arange(n + 1, 64 * n)) == N_BUCKETS - 1)
    bias_t = rel_bias.astype(F32).T * LOG2_E

    def by_distance(dist):
        bucket = jnp.asarray(_t5_bucket_np(dist), I32)[None]
        tab = jnp.zeros((MOBA_HEADS, dist.shape[0]), F32)
        for b in range(N_BUCKETS):
            tab = jnp.where(bucket == b, bias_t[:, b][:, None], tab)
        return tab

    d = np.arange(2 * n)
    g_own = jnp.where(jnp.asarray(d < n)[None], by_distance(np.where(d < n, d, 0)), NEG_INF)
    g_prev = by_distance(np.where(d < n, n + d, d - n))
    far = bias_t[:, N_BUCKETS - 1]
    return g_own[:, None, :], g_prev[:, None, :], far


def _moba_kernel(far_ref, q_ref, k_ref, v_ref, g0_ref, g1_ref, o_ref,
                 qt_s, kb_s, vt_s, sc_s, t0_ref, t1_ref, *, nb):
    hp = pl.program_id(1)
    blk = MOBA_BLOCK
    scale = HEAD_DIM ** -0.5 * LOG2_E
    row_head = lax.broadcasted_iota(I32, (LANES, 1), 0) // HEAD_DIM
    lane_head = lax.broadcasted_iota(I32, (1, LANES), 1) // HEAD_DIM
    ones_rows = jnp.ones((MOBA_VROWS - HEAD_DIM, blk), BF16)

    for h in range(2):
        for g_ref, t_ref in ((g0_ref, t0_ref), (g1_ref, t1_ref)):
            rows = jnp.broadcast_to(g_ref[h], (blk, 2 * blk))
            t_ref[h] = pltpu.roll(rows, 0, 1, stride=1, stride_axis=0)[:, :blk]

    kmean_rows = []
    for j in range(nb):
        sl = pl.ds(j * blk, blk)
        qt_s[j] = (q_ref[0, sl, :] * scale).T
        vt = v_ref[0, sl, :].T.astype(BF16)
        for h in range(2):
            vt_s[j, h * MOBA_VROWS:h * MOBA_VROWS + HEAD_DIM, :] = vt[h * HEAD_DIM:(h + 1) * HEAD_DIM, :]
            vt_s[j, h * MOBA_VROWS + HEAD_DIM:(h + 1) * MOBA_VROWS, :] = ones_rows
        kj = k_ref[0, sl, :]
        kb_s[j] = kj.astype(BF16)
        kmean_rows.append(jnp.mean(kj, axis=0, keepdims=True))
    kmean = jnp.concatenate(kmean_rows, axis=0)
    km2 = jnp.concatenate([jnp.where(lane_head == 0, kmean, 0.0),
                           jnp.where(lane_head == 1, kmean, 0.0)], axis=0)
    blk_iota = lax.broadcasted_iota(I32, (nb, blk), 0)

    for i in range(nb):
        qt = qt_s[i]
        negs = [None, None]
        if i > MOBA_TOPK:
            gate2 = _hdot(km2, qt)
            for h in range(2):
                g = gate2[h * nb:(h + 1) * nb, :]
                cnt = jnp.zeros((nb, blk), I32)
                for m in range(i):
                    gm = g[m:m + 1, :]
                    beats = (gm > g) | ((gm == g) & (m < blk_iota))
                    cnt = cnt + jnp.where(beats, 1, 0)
                negs[h] = jnp.where(cnt < MOBA_TOPK, 0.0, NEG_INF)
        outs = []
        for h in range(2):
            qt_h = jnp.where(row_head == h, qt, 0.0).astype(BF16)
            far_bias = far_ref[2 * hp + h]
            m_run = None
            for j in range(i + 1):
                s = jnp.dot(kb_s[j], qt_h, preferred_element_type=F32)
                if j == i:
                    s = s + t0_ref[h]
                else:
                    bias = t1_ref[h] if j == i - 1 else far_bias
                    if negs[h] is not None:
                        bias = bias + negs[h][j:j + 1, :]
                    s = s + bias
                sc_s[h, j] = s
                cm = jnp.max(s, axis=0, keepdims=True)
                m_run = cm if m_run is None else jnp.maximum(m_run, cm)
            probs = jnp.concatenate([jnp.exp2(sc_s[h, j] - m_run).astype(BF16) for j in range(i + 1)],
                                    axis=0)
            vals = jnp.concatenate([vt_s[j, h * MOBA_VROWS:(h + 1) * MOBA_VROWS, :]
                                    for j in range(i + 1)], axis=1)
            acc = jnp.dot(vals, probs, preferred_element_type=F32)
            outs.append(acc[:HEAD_DIM] / acc[HEAD_DIM:HEAD_DIM + 1])
        o_t = jnp.concatenate(outs, axis=0)
        o_ref[0, pl.ds(i * blk, blk), :] = o_t.T


def _moba_attention(p3, g_own, g_prev, far):
    b, s, _ = p3.shape
    nb = s // MOBA_BLOCK
    n_pairs = MOBA_HEADS // 2
    blk_spec = lambda off: pl.BlockSpec((1, s, LANES), lambda bi, hp: (bi, 0, off + hp))
    tab_spec = pl.BlockSpec((2, 1, 2 * MOBA_BLOCK), lambda bi, hp: (hp, 0, 0))
    return pl.pallas_call(
        functools.partial(_moba_kernel, nb=nb),
        grid=(b, n_pairs),
        in_specs=[pl.BlockSpec(memory_space=pltpu.SMEM),
                  blk_spec(0), blk_spec(n_pairs), blk_spec(2 * n_pairs), tab_spec, tab_spec],
        out_specs=pl.BlockSpec((1, s, LANES), lambda bi, hp: (bi, 0, hp)),
        out_shape=jax.ShapeDtypeStruct((b, s, MOBA_W), F32),
        scratch_shapes=[pltpu.VMEM((nb, LANES, MOBA_BLOCK), F32),
                        pltpu.VMEM((nb, MOBA_BLOCK, LANES), BF16),
                        pltpu.VMEM((nb, 2 * MOBA_VROWS, MOBA_BLOCK), BF16),
                        pltpu.VMEM((2, nb, MOBA_BLOCK, MOBA_BLOCK), F32),
                        pltpu.VMEM((2, MOBA_BLOCK, MOBA_BLOCK), F32),
                        pltpu.VMEM((2, MOBA_BLOCK, MOBA_BLOCK), F32)],
        compiler_params=_cparams(("parallel", "parallel")),
        name="moba",
    )(far, p3, p3, p3, g_own, g_prev)


def _softplus(z):
    return jnp.maximum(z, 0.0) + jnp.log(1.0 + jnp.exp(-jnp.abs(z)))


def _sigmoid(z):
    return 1.0 / (1.0 + jnp.exp(-z))


def _rwkv_kernel(r_ref, k_ref, v_ref, wa_ref, g_ref,
                 mu_r_ref, mu_k_ref, mu_v_ref, mu_wa_ref, mu_g_ref,
                 w0_ref, wup_ref, a0_ref, aup_ref, gup_ref, kk_ref, ka_ref, rk_ref,
                 gng_ref, gnb_ref, hsum_ref, tri_ref,
                 o_ref, st_s, prev_r, prev_k, prev_v, prev_wa, prev_g, *, n_seq):
    c = RWKV_CHUNK
    mid = c // 2

    @pl.when(pl.program_id(1) == 0)
    def _():
        st_s[...] = jnp.zeros_like(st_s)
        prev_r[...] = jnp.zeros_like(prev_r)
        prev_k[...] = jnp.zeros_like(prev_k)
        prev_v[...] = jnp.zeros_like(prev_v)
        prev_wa[...] = jnp.zeros_like(prev_wa)
        prev_g[...] = jnp.zeros_like(prev_g)

    n_pairs = RWKV_HEADS // 2
    hsum = hsum_ref[...]
    tri = tri_ref[...]
    row = lax.broadcasted_iota(I32, (c, c), 0)
    col = lax.broadcasted_iota(I32, (c, c), 1)
    strict = row > col
    incl = row >= col
    eye = (row == col).astype(F32)
    lane_half = lax.broadcasted_iota(I32, (1, LANES), 1) // HEAD_DIM
    hmask = hsum.astype(F32)
    dot = functools.partial(jnp.dot, preferred_element_type=F32)
    lanes = lambda x, jp: x[:, jp * LANES:(jp + 1) * LANES]

    def group_sum(x):
        return jnp.concatenate([_dot_exact_rhs(lanes(x, jp), hsum, 2) for jp in range(n_pairs)], axis=1)

    def two_heads(x):
        return jnp.concatenate([jnp.where(lane_half == 0, x, 0.0), jnp.where(lane_half == 1, x, 0.0)],
                               axis=0).astype(BF16)

    def side(a, b):
        return jnp.concatenate([a, b], axis=1)

    def mix(g, x_ref, prev_ref, mu_ref):
        x = x_ref[g]
        rows = lax.broadcasted_iota(I32, x.shape, 0)
        shifted = jnp.where(rows == 0, prev_ref[g], pltpu.roll(x, 1, axis=0))
        prev_ref[g] = x[c - 1:c, :]
        return x + (shifted - x) * mu_ref[...]

    def prepare(g):
        r = mix(g, r_ref, prev_r, mu_r_ref)
        k = mix(g, k_ref, prev_k, mu_k_ref)
        v = mix(g, v_ref, prev_v, mu_v_ref)
        xwa = mix(g, wa_ref, prev_wa, mu_wa_ref)
        xg = mix(g, g_ref, prev_g, mu_g_ref)

        w = -_softplus(-(w0_ref[...] + _bdot(jnp.tanh(xwa), wup_ref[...]))) - 0.5
        logd = -jnp.exp(w)
        a = _sigmoid(a0_ref[...] + _bdot(xwa, aup_ref[...]))
        gate = _bdot(_sigmoid(xg), gup_ref[...])
        kk = k * kk_ref[...]
        kk = kk / jnp.maximum(jnp.sqrt(group_sum(kk * kk)), 1e-12)
        k2 = k * (1.0 + (a - 1.0) * ka_ref[...])

        parts = _split_bf16(logd, 3)
        cum = dot(tri, parts[0]) + dot(tri, parts[1]) + dot(tri, parts[2])
        cum_last = cum[c - 1:c, :]
        ref = cum[mid - 1:mid, :]
        rel = cum - ref
        g_in = jnp.exp(rel)
        g_inv = jnp.exp(-rel)
        g_tail = jnp.exp(cum_last - cum)
        a_t = -kk * jnp.exp(rel - logd)
        b_t = kk * a * g_inv
        k_t = k2 * g_inv
        r_t = r * g_in
        b_hat = kk * a * g_tail
        k_hat = k2 * g_tail

        ref_scale = jnp.exp(ref)
        decay = jnp.exp(cum_last)
        bk = jnp.concatenate([b_t, k_t], axis=0).astype(BF16)
        ar_all = jnp.concatenate([a_t, r_t], axis=0)
        per_pair = []
        for jp in range(n_pairs):
            st = st_s[g, jp]
            st_ref = (st * lanes(ref_scale, jp)).astype(BF16)
            both = _dot_nt(lanes(ar_all, jp).astype(BF16), st_ref)
            quads = []
            for half in range(2):
                ar = jnp.where(lane_half == half, lanes(ar_all, jp), 0.0)
                m4 = _dot_nt(ar.astype(BF16), lanes(bk, jp))
                quads.append((jnp.where(strict, m4[:c, :c], 0.0),
                              jnp.where(strict, m4[:c, c:], 0.0).astype(BF16),
                              jnp.where(incl, m4[c:, :c], 0.0).astype(BF16),
                              jnp.where(incl, m4[c:, c:], 0.0).astype(BF16)))
            per_pair.append(dict(
                st=st, p0=both[:c], o0=both[c:], v_rows=two_heads(lanes(v, jp)),
                l_ab=[q[0] for q in quads], l_ak=side(quads[0][1], quads[1][1]),
                m_rb=side(quads[0][2], quads[1][2]), m_rk=side(quads[0][3], quads[1][3]),
                v=lanes(v, jp).astype(BF16), b_hat=lanes(b_hat, jp).astype(BF16),
                k_hat=lanes(k_hat, jp).astype(BF16), decay=lanes(decay, jp)))
        return dict(r=r, k2=k2, v=v, gate=gate, pairs=per_pair)

    seqs = [prepare(g) for g in range(n_seq)]
    heads = [(g, jp, half) for g in range(n_seq) for jp in range(n_pairs) for half in range(2)]

    pw = {gh: seqs[gh[0]]["pairs"][gh[1]]["l_ab"][gh[2]] for gh in heads}
    t_inv = {gh: eye + pw[gh] for gh in heads}
    pw = {gh: pw[gh].astype(BF16) for gh in heads}
    span = 2
    while span < c:
        pw = {gh: dot(pw[gh], pw[gh]).astype(BF16) for gh in heads}
        t_inv = {gh: t_inv[gh] + dot(t_inv[gh].astype(BF16), pw[gh]) for gh in heads}
        span *= 2

    for g in range(n_seq):
        sq = seqs[g]
        outs = []
        for jp in range(n_pairs):
            pr = sq["pairs"][jp]
            t_pair = side(t_inv[(g, jp, 0)], t_inv[(g, jp, 1)]).astype(BF16)
            w_loc = pr["p0"] + dot(pr["l_ak"], pr["v_rows"])
            u = dot(t_pair, two_heads(w_loc))
            outs.append(pr["o0"] + dot(pr["m_rb"], two_heads(u)) + dot(pr["m_rk"], pr["v_rows"]))
            upd = _dot_tn(u.astype(BF16), pr["b_hat"]) + _dot_tn(pr["v"], pr["k_hat"])
            st_s[g, jp] = pr["st"] * pr["decay"] + hmask * upd
        o = jnp.concatenate(outs, axis=1)

        inv_n = 1.0 / HEAD_DIM
        mean = group_sum(o) * inv_n
        dev = o - mean
        var = group_sum(dev * dev) * inv_n
        y = dev * lax.rsqrt(var + RWKV_GN_EPS) * gng_ref[...] + gnb_ref[...]
        y = y + group_sum(sq["r"] * sq["k2"] * rk_ref[...]) * sq["v"]
        o_ref[g] = y * sq["gate"]


def _rwkv(p3, mu, w0, w_up, a0, a_up, g_up, k_k, k_a, r_k, gn_g, gn_b):
    b, s, _ = p3.shape
    c = RWKV_CHUNK
    w = RWKV_W
    n_seq = RWKV_SEQS_PER_STEP if b % RWKV_SEQS_PER_STEP == 0 else 1
    base = 3 * MOBA_W
    assert base % w == 0 and (base + 3 * w) % LANES == 0
    lora = DECAY_LORA + AAA_LORA

    def col_spec(width, off_cols):
        assert off_cols % width == 0
        return pl.BlockSpec((n_seq, c, width), lambda bi, ci: (bi, ci, off_cols // width))

    row = lambda x: x.reshape(1, -1).astype(F32)
    const = lambda shape: pl.BlockSpec(shape, lambda bi, ci: (0,) * len(shape))
    wup_pad = jnp.concatenate([w_up, jnp.zeros((AAA_LORA, w), F32)], axis=0).astype(BF16)
    aup_pad = jnp.concatenate([jnp.zeros((DECAY_LORA, w), F32), a_up], axis=0).astype(BF16)
    head = np.arange(LANES) // HEAD_DIM
    hsum = jnp.asarray((head[:, None] == head[None, :]).astype(np.float32)).astype(BF16)
    tri = jnp.asarray(np.tril(np.ones((c, c), np.float32))).astype(BF16)
    vec_args = [mu[:w], mu[w:2 * w], mu[2 * w:3 * w], mu[3 * w:3 * w + lora], mu[3 * w + lora:],
                w0, None, a0, None, None, k_k, k_a, r_k.reshape(-1), gn_g, gn_b]
    args = [p3, p3, p3, p3, p3]
    specs = [col_spec(w, base), col_spec(w, base + w), col_spec(w, base + 2 * w),
             col_spec(lora, base + 3 * w), col_spec(GATE_LORA, base + 3 * w + lora)]
    mats = {6: wup_pad, 8: aup_pad, 9: g_up.astype(BF16)}
    for idx, a in enumerate(vec_args):
        arr = mats[idx] if a is None else row(a)
        args.append(arr)
        specs.append(const(arr.shape))
    for arr in (hsum, tri):
        args.append(arr)
        specs.append(const(arr.shape))
    return pl.pallas_call(
        functools.partial(_rwkv_kernel, n_seq=n_seq),
        grid=(b // n_seq, s // c),
        in_specs=specs,
        out_specs=pl.BlockSpec((n_seq, c, w), lambda bi, ci: (bi, ci, 0)),
        out_shape=jax.ShapeDtypeStruct((b, s, w), F32),
        scratch_shapes=[pltpu.VMEM((n_seq, RWKV_HEADS // 2, LANES, LANES), F32),
                        pltpu.VMEM((n_seq, 1, w), F32),
                        pltpu.VMEM((n_seq, 1, w), F32), pltpu.VMEM((n_seq, 1, w), F32),
                        pltpu.VMEM((n_seq, 1, lora), F32), pltpu.VMEM((n_seq, 1, GATE_LORA), F32)],
        compiler_params=_cparams(("parallel", "arbitrary")),
        name="rwkv",
    )(*args)


def _mem_attn_kernel(q_ref, kv_ref, o_ref):
    scale = HEAD_DIM ** -0.5
    lane_head = lax.broadcasted_iota(I32, (1, MEM_W), 1) // HEAD_DIM
    q = q_ref[0] * scale
    mk = kv_ref[0, :, :MEM_W].astype(BF16)
    mv = kv_ref[0, :, MEM_W:].astype(BF16)
    out = jnp.zeros(q.shape, F32)
    for h in range(MEM_HEADS):
        hm = lane_head == h
        s = _dot_nt(jnp.where(hm, q, 0.0).astype(BF16), mk)
        s = s - jnp.max(s, axis=-1, keepdims=True)
        e = jnp.exp(s)
        p = e / jnp.sum(e, axis=-1, keepdims=True)
        out = jnp.where(hm, jnp.dot(p.astype(BF16), mv, preferred_element_type=F32), out)
    o_ref[0] = out


def _mem_attention(p3, mkv3, tq):
    b, s, _ = p3.shape
    m = mkv3.shape[1]
    off = (3 * MOBA_W + RWKV_COLS) // MEM_W
    assert off * MEM_W == 3 * MOBA_W + RWKV_COLS
    return pl.pallas_call(
        _mem_attn_kernel,
        grid=(b, s // tq),
        in_specs=[pl.BlockSpec((1, tq, MEM_W), lambda bi, i: (bi, i, off)),
                  pl.BlockSpec((1, m, 2 * MEM_W), lambda bi, i: (bi, 0, 0))],
        out_specs=pl.BlockSpec((1, tq, MEM_W), lambda bi, i: (bi, i, 0)),
        out_shape=jax.ShapeDtypeStruct((b, s, MEM_W), F32),
        compiler_params=_cparams(("parallel", "parallel")),
        name="mem_attn",
    )(p3, mkv3)


def _out_router_kernel(x_ref, ym_ref, yr_ref, ye_ref, wo1_ref, wo2_ref, wo3_ref, g_ref,
                       wrh_ref, wrl_ref, br_ref, upper_ref, ones_ref,
                       x1_ref, h_ref, idx_ref, rank_ref, wgt_ref, cnt_ref, run_s):
    @pl.when(pl.program_id(0) == 0)
    def _():
        run_s[...] = jnp.zeros_like(run_s)

    x1 = (x_ref[...] + _bdot(ym_ref[...], wo1_ref[...]) + _bdot(yr_ref[...], wo2_ref[...])
          + _bdot(ye_ref[...], wo3_ref[...]))
    x1_ref[...] = x1
    h = _rms(x1, g_ref[...])
    h_ref[...] = _pack_halves(h)
    tm = h.shape[0]
    h_hi, h_lo = _split_bf16(h, 2)
    dot = functools.partial(jnp.dot, preferred_element_type=F32)
    logits = dot(h_hi, wrh_ref[...]) + (dot(h_hi, wrl_ref[...]) + dot(h_lo, wrh_ref[...])) + br_ref[...]
    lg = logits.T[:N_EXPERTS, :]
    e_iota = lax.broadcasted_iota(I32, (N_EXPERTS, tm), 0)

    vals, idxs = [], []
    for _ in range(TOP_K):
        m = jnp.max(lg, axis=0, keepdims=True)
        idx = jnp.min(jnp.where(lg == m, e_iota, N_EXPERTS), axis=0, keepdims=True)
        vals.append(m)
        idxs.append(idx)
        lg = jnp.where(e_iota == idx, NEG_INF, lg)
    exps = [jnp.exp(vk - vals[0]) for vk in vals]
    denom = exps[0] + exps[1] + exps[2] + exps[3]

    chosen = jnp.zeros((N_EXPERTS, tm), F32)
    for idx in idxs:
        chosen = chosen + jnp.where(e_iota == idx, 1.0, 0.0)
    chosen = chosen.astype(BF16)
    run = run_s[...]
    before = dot(chosen, upper_ref[...]) + jnp.concatenate([run] * (tm // LANES), axis=1)
    run_s[...] = run + dot(chosen, ones_ref[...])
    cnt_ref[...] = run_s[...]

    zero_i = jnp.zeros((8 - TOP_K, tm), I32)
    ranks = [jnp.sum(jnp.where(e_iota == idx, before, 0.0), axis=0, keepdims=True).astype(I32)
             for idx in idxs]
    idx_ref[0] = jnp.concatenate(idxs + [zero_i], axis=0)
    rank_ref[0] = jnp.concatenate(ranks + [zero_i], axis=0)
    wrows = jnp.concatenate([e / denom for e in exps] + [jnp.zeros((LANES - TOP_K, tm), F32)], axis=0)
    wgt_ref[...] = wrows.T


def _out_router(x2, ym, yr, ye, w_out, g_ffn, w_router, b_router, tm):
    t, d = x2.shape
    wo = w_out.astype(BF16)
    wo1, wo2, wo3 = wo[:MOBA_W], wo[MOBA_W:MOBA_W + RWKV_W], wo[MOBA_W + RWKV_W:]
    wr = jnp.zeros((d, LANES), F32).at[:, :N_EXPERTS].set(w_router)
    wr_hi = wr.astype(BF16)
    wr_lo = (wr - wr_hi.astype(F32)).astype(BF16)
    br = jnp.full((1, LANES), NEG_INF, F32).at[0, :N_EXPERTS].set(b_router)
    upper = jnp.asarray(np.triu(np.ones((tm, tm), np.float32), 1)).astype(BF16)
    ones = jnp.ones((tm, LANES), BF16)
    tile = lambda n: pl.BlockSpec((tm, n), lambda i: (i, 0))
    slots = pl.BlockSpec((1, 8, tm), lambda i: (i, 0, 0))
    const = lambda a: pl.BlockSpec(a.shape, lambda i: (0,) * a.ndim)
    g2 = g_ffn.reshape(1, d)
    n = t // tm
    return pl.pallas_call(
        _out_router_kernel,
        grid=(n,),
        in_specs=[tile(d), tile(MOBA_W), tile(RWKV_W), tile(MEM_W), const(wo1), const(wo2), const(wo3),
                  const(g2), const(wr_hi), const(wr_lo), const(br), const(upper), const(ones)],
        out_specs=[tile(d), tile(d // 2), slots, slots, tile(LANES),
                   pl.BlockSpec((N_EXPERTS, LANES), lambda i: (0, 0))],
        out_shape=[jax.ShapeDtypeStruct((t, d), F32), jax.ShapeDtypeStruct((t, d // 2), I32),
                   jax.ShapeDtypeStruct((n, 8, tm), I32), jax.ShapeDtypeStruct((n, 8, tm), I32),
                   jax.ShapeDtypeStruct((t, LANES), F32), jax.ShapeDtypeStruct((N_EXPERTS, LANES), F32)],
        scratch_shapes=[pltpu.VMEM((N_EXPERTS, LANES), F32)],
        compiler_params=_cparams(("arbitrary",)),
        name="out_router",
    )(x2, ym, yr, ye, wo1, wo2, wo3, g2, wr_hi, wr_lo, br, upper, ones)


def _sc_mesh():
    return plsc.VectorSubcoreMesh(core_axis_name="c", subcore_axis_name="s",
                                  num_cores=SC_CORES, num_subcores=SC_SUBCORES)


def _sc_split(n_rows):
    n_workers = SC_CORES * SC_SUBCORES
    per_worker = n_rows // n_workers
    assert per_worker * n_workers == n_rows and per_worker % (SC_ROWS * SC_INFLIGHT) == 0
    assert SC_ROWS % 8 == 0
    return per_worker


def _sc_gather_rows(table, idx):
    n_rows = idx.shape[0]
    width = table.shape[1]
    per_worker = _sc_split(n_rows)

    buf = lambda shape, dtype: [pltpu.VMEM(shape, dtype) for _ in range(SC_INFLIGHT)]

    @functools.partial(
        pl.kernel, mesh=_sc_mesh(),
        out_type=jax.ShapeDtypeStruct((n_rows, width), table.dtype),
        scratch_types=[buf((SC_ROWS,), I32), buf((SC_ROWS, width), table.dtype),
                       [pltpu.SemaphoreType.DMA for _ in range(SC_INFLIGHT)]],
        name="sc_gather",
    )
    def gather(table_hbm, idx_hbm, out_hbm, idx_v, rows_v, sems):
        worker = lax.axis_index("s") * SC_CORES + lax.axis_index("c")
        base = worker * per_worker

        @pl.loop(0, per_worker // (SC_ROWS * SC_INFLIGHT))
        def _(step):
            offs = [pl.multiple_of(base + (step * SC_INFLIGHT + b) * SC_ROWS, 8) for b in range(SC_INFLIGHT)]
            copies = []
            for b in range(SC_INFLIGHT):
                pltpu.sync_copy(idx_hbm.at[pl.ds(offs[b], SC_ROWS)], idx_v[b])
                copies.append(pltpu.async_copy(table_hbm.at[idx_v[b]], rows_v[b], sems[b]))
            for b in range(SC_INFLIGHT):
                copies[b].wait()
                pltpu.sync_copy(rows_v[b], out_hbm.at[pl.ds(offs[b], SC_ROWS)])

    return gather(table, idx)


def _sc_scatter_rows(rows, idx, n_out):
    n_src, width = rows.shape
    n_copies = idx.shape[0] // n_src
    assert n_copies * n_src == idx.shape[0]
    per_worker = _sc_split(n_src)

    @functools.partial(
        pl.kernel, mesh=_sc_mesh(),
        out_type=jax.ShapeDtypeStruct((n_out, width), rows.dtype),
        scratch_types=[[pltpu.VMEM((SC_ROWS,), I32) for _ in range(n_copies)],
                       pltpu.VMEM((SC_ROWS, width), rows.dtype),
                       [pltpu.SemaphoreType.DMA for _ in range(n_copies)]],
        name="sc_scatter",
    )
    def scatter(rows_hbm, idx_hbm, out_hbm, idx_v, rows_v, sems):
        worker = lax.axis_index("s") * SC_CORES + lax.axis_index("c")
        base = worker * per_worker

        @pl.loop(0, per_worker // SC_ROWS)
        def _(step):
            src = pl.multiple_of(base + step * SC_ROWS, 8)
            pltpu.sync_copy(rows_hbm.at[pl.ds(src, SC_ROWS)], rows_v)
            copies = []
            for cpy in range(n_copies):
                off = pl.multiple_of(cpy * n_src + src, 8)
                pltpu.sync_copy(idx_hbm.at[pl.ds(off, SC_ROWS)], idx_v[cpy])
                copies.append(pltpu.async_copy(rows_v, out_hbm.at[idx_v[cpy]], sems[cpy]))
            for cp in copies:
                cp.wait()

    return scatter(rows, idx)


def _experts_kernel(be_ref, bv_ref, nused_ref, next_ref, slot_ref,
                    xs_ref, wgu_hbm, bg_ref, bu_ref, wd_hbm, bd_ref, perm_ref,
                    o_ref, wgu_buf, wd_buf, wg_s, wu_s, wd_s, sem):
    i = pl.program_id(0)
    e = be_ref[i]
    used = i < nused_ref[0]
    changed = ((i == 0) | (e != be_ref[jnp.maximum(i - 1, 0)])) & used
    slot = slot_ref[e]

    def weight_copies(expert, buf_slot):
        return (pltpu.make_async_copy(wgu_hbm.at[expert], wgu_buf.at[buf_slot], sem.at[buf_slot, 0]),
                pltpu.make_async_copy(wd_hbm.at[expert], wd_buf.at[buf_slot], sem.at[buf_slot, 1]))

    @pl.when((i == 0) & used)
    def _():
        for cp in weight_copies(e, slot):
            cp.start()

    @pl.when(changed)
    def _():
        for cp in weight_copies(e, slot):
            cp.wait()
        nxt = next_ref[e]

        @pl.when(nxt >= 0)
        def _():
            for cp in weight_copies(nxt, 1 - slot):
                cp.start()

        half = LANES
        for cblk in range(2 * D_EXPERT // (2 * half)):
            wt = wgu_buf[slot, :, cblk * 2 * half:(cblk + 1) * 2 * half].astype(BF16)
            sep = jnp.dot(wt, perm_ref[...], preferred_element_type=F32).astype(BF16)
            wg_s[:, cblk * half:(cblk + 1) * half] = sep[:, :half]
            wu_s[:, cblk * half:(cblk + 1) * half] = sep[:, half:]
        wd_s[...] = wd_buf[slot].astype(BF16)

    @pl.when(used)
    def _():
        rows = lax.broadcasted_iota(I32, xs_ref.shape, 0)
        xb = _unpack_halves(jnp.where(rows < bv_ref[i], xs_ref[...], 0)).astype(BF16)
        gate = jnp.dot(xb, wg_s[...], preferred_element_type=F32) + bg_ref[0]
        up = jnp.dot(xb, wu_s[...], preferred_element_type=F32) + bu_ref[0]
        gate = jnp.minimum(gate, SWIGLU_LIMIT)
        up = jnp.clip(up, -SWIGLU_LIMIT, SWIGLU_LIMIT)
        glu = gate * _sigmoid(gate * SWIGLU_ALPHA)
        act = ((up + 1.0) * glu).astype(BF16)
        o_ref[...] = _pack_halves(jnp.dot(act, wd_s[...], preferred_element_type=F32) + bd_ref[0])

    @pl.when(jnp.logical_not(used))
    def _():
        o_ref[...] = jnp.zeros_like(o_ref)


def _experts(blk_expert, blk_valid, n_used, next_expert, buf_slot, xs, w_gate_up, b_gate_up, w_down,
             b_down):
    p_rows = xs.shape[0]
    d = 2 * xs.shape[1]
    n_blocks = p_rows // EXPERT_ROWS
    bg = b_gate_up[:, 0::2].reshape(N_EXPERTS, 1, D_EXPERT)
    bu = b_gate_up[:, 1::2].reshape(N_EXPERTS, 1, D_EXPERT)
    bd = b_down.reshape(N_EXPERTS, 1, d)
    perm_np = np.zeros((2 * LANES, 2 * LANES), np.float32)
    perm_np[2 * np.arange(LANES), np.arange(LANES)] = 1.0
    perm_np[2 * np.arange(LANES) + 1, LANES + np.arange(LANES)] = 1.0
    perm = jnp.asarray(perm_np).astype(BF16)
    by_expert = lambda shape: pl.BlockSpec((1,) + shape, lambda i, be, *_: (be[i], 0, 0))
    grid_spec = pltpu.PrefetchScalarGridSpec(
        num_scalar_prefetch=5,
        grid=(n_blocks,),
        in_specs=[pl.BlockSpec((EXPERT_ROWS, d // 2), lambda i, *_: (i, 0)),
                  pl.BlockSpec(memory_space=pl.ANY), by_expert((1, D_EXPERT)), by_expert((1, D_EXPERT)),
                  pl.BlockSpec(memory_space=pl.ANY), by_expert((1, d)),
                  pl.BlockSpec(perm.shape, lambda i, *_: (0, 0))],
        out_specs=pl.BlockSpec((EXPERT_ROWS, d // 2), lambda i, *_: (i, 0)),
        scratch_shapes=[pltpu.VMEM((2, d, 2 * D_EXPERT), F32), pltpu.VMEM((2, D_EXPERT, d), F32),
                        pltpu.VMEM((d, D_EXPERT), BF16), pltpu.VMEM((d, D_EXPERT), BF16),
                        pltpu.VMEM((D_EXPERT, d), BF16), pltpu.SemaphoreType.DMA((2, 2))],
    )
    return pl.pallas_call(
        _experts_kernel,
        grid_spec=grid_spec,
        out_shape=jax.ShapeDtypeStruct((p_rows, d // 2), I32),
        compiler_params=_cparams(("arbitrary",)),
        name="experts",
    )(blk_expert, blk_valid, n_used, next_expert, buf_slot, xs, w_gate_up, bg, bu, w_down, bd, perm)


def _combine_kernel(yg_ref, x1_ref, w_ref, g_ref, *rest):
    o_ref = rest[-1]
    acc = x1_ref[...]
    wts = w_ref[...]
    for kk in range(TOP_K):
        acc = acc + _unpack_halves(yg_ref[kk]) * wts[:, kk:kk + 1]
    o_ref[...] = _rms(acc, g_ref[...])


def _combine(yg, x1, wts, g_final, tb, first_tile, partial_out):
    t, d = x1.shape
    half = yg.shape[2]
    tile = lambda n: pl.BlockSpec((tb, n), lambda i: (i + first_tile, 0))
    in_specs = [pl.BlockSpec((TOP_K, tb, half), lambda i: (0, i, 0)), tile(d), tile(LANES),
                pl.BlockSpec((1, d), lambda i: (0, 0))]
    args = [yg, x1, wts, g_final.reshape(1, d)]
    aliases = {}
    if partial_out is not None:
        in_specs.append(pl.BlockSpec(memory_space=pl.ANY))
        args.append(partial_out)
        aliases = {len(args) - 1: 0}
    return pl.pallas_call(
        _combine_kernel,
        grid=(yg.shape[1] // tb,),
        in_specs=in_specs,
        out_specs=tile(d),
        out_shape=jax.ShapeDtypeStruct((t, d), F32),
        input_output_aliases=aliases,
        compiler_params=_cparams(("parallel",)),
        name="combine",
    )(*args)


def _layer(x, mem, w_in, w_out, w_mem_kv, g_mix, g_mem, g_ffn, bias_tables, mu, w0, w_up, a0, a_up,
           g_up, k_k, k_a, r_k, gn_g, gn_b, w_router, b_router, w_gate_up, b_gate_up, w_down, b_down,
           g_last):
    b, s, d = x.shape
    m = mem.shape[1]
    t = b * s
    x2 = x.reshape(t, d)

    p = _norm_matmul(x2, g_mix, w_in.astype(BF16), 512, "in_proj")
    p3 = p.reshape(b, s, IN_COLS)
    y_moba = _moba_attention(p3, *bias_tables)
    y_rwkv = _rwkv(p3, mu, w0, w_up, a0, a_up, g_up, k_k, k_a, r_k, gn_g, gn_b)
    mkv = _norm_matmul(mem.reshape(b * m, d), g_mem, w_mem_kv.astype(BF16), 512, "mem_kv")
    y_mem = _mem_attention(p3, mkv.reshape(b, m, 2 * MEM_W), 512)

    tb = ROUTE_TOKENS
    x1, h2, idx_o, rank_o, wgt_p, cnt = _out_router(
        x2, y_moba.reshape(t, MOBA_W), y_rwkv.reshape(t, RWKV_W), y_mem.reshape(t, MEM_W),
        w_out, g_ffn, w_router, b_router, tb)

    counts = cnt[:, 0].astype(I32)
    padded = (counts + EXPERT_ROWS - 1) // EXPERT_ROWS * EXPERT_ROWS
    pad_ends = jnp.cumsum(padded)
    pad_starts = (pad_ends - padded).astype(I32)
    n_blocks = (t * TOP_K) // EXPERT_ROWS + N_EXPERTS
    blk_start = jnp.arange(n_blocks, dtype=I32) * EXPERT_ROWS
    blk_expert = jnp.minimum(jnp.sum(blk_start[:, None] >= pad_ends[None, :], axis=1),
                             N_EXPERTS - 1).astype(I32)
    of_block = blk_expert[:, None] == jnp.arange(N_EXPERTS, dtype=I32)[None, :]
    rows_left = jnp.sum(jnp.where(of_block, (counts + pad_starts)[None, :], 0), axis=1) - blk_start
    blk_valid = jnp.clip(rows_left, 0, EXPERT_ROWS)
    n_used = (pad_ends[-1:] // EXPERT_ROWS).astype(I32)
    has_rows = counts > 0
    first_from = lax.cummin(jnp.where(has_rows, jnp.arange(N_EXPERTS, dtype=I32), N_EXPERTS), reverse=True)
    next_expert = jnp.concatenate([first_from[1:], jnp.full((1,), N_EXPERTS, I32)])
    next_expert = jnp.where(next_expert < N_EXPERTS, next_expert, -1).astype(I32)
    buf_slot = ((jnp.cumsum(has_rows.astype(I32)) - has_rows.astype(I32)) % 2).astype(I32)
    idx_kt = jnp.swapaxes(idx_o[:, :TOP_K, :], 0, 1).reshape(TOP_K, t)
    rank_kt = jnp.swapaxes(rank_o[:, :TOP_K, :], 0, 1).reshape(TOP_K, t)
    experts = jnp.arange(N_EXPERTS, dtype=I32)[:, None, None]
    start_kt = jnp.sum(jnp.where(idx_kt[None] == experts, pad_starts[:, None, None], 0), axis=0)
    dest = (start_kt + rank_kt).reshape(TOP_K * t)

    xs = _sc_scatter_rows(h2, dest, n_blocks * EXPERT_ROWS)
    ys = _experts(blk_expert, blk_valid.astype(I32), n_used, next_expert, buf_slot, xs,
                  w_gate_up, b_gate_up, w_down, b_down)
    dest_kt = dest.reshape(TOP_K, t)
    out = None
    for part in range(COMBINE_PARTS):
        t_part = t // COMBINE_PARTS
        rows = dest_kt[:, part * t_part:(part + 1) * t_part].reshape(TOP_K * t_part)
        yg = _sc_gather_rows(ys, rows).reshape(TOP_K, t_part, d // 2)
        out = _combine(yg, x1, wgt_p, g_last, tb, part * (t_part // tb), out)
    return out.reshape(b, s, d)


def kernel(x, mem, w_in, w_out, w_mem_kv, g_mix, g_mem, g_ffn, g_final, rel_bias, rwkv_mu, rwkv_w0,
           rwkv_w_up, rwkv_a0, rwkv_a_up, rwkv_g_up, rwkv_k_k, rwkv_k_a, rwkv_r_k, rwkv_gn_g, rwkv_gn_b,
           w_router, b_router, w_gate_up, b_gate_up, w_down, b_down):
    depth = w_in.shape[0]
    assert depth == 1, "the final norm is fused into the last layer's combine kernel"
    bias_tables = _moba_bias_tables(rel_bias)
    l = 0
    return _layer(x, mem, w_in[l], w_out[l], w_mem_kv[l], g_mix[l], g_mem[l], g_ffn[l], bias_tables,
                  rwkv_mu[l], rwkv_w0[l], rwkv_w_up[l], rwkv_a0[l], rwkv_a_up[l], rwkv_g_up[l],
                  rwkv_k_k[l], rwkv_k_a[l], rwkv_r_k[l], rwkv_gn_g[l], rwkv_gn_b[l], w_router[l],
                  b_router[l], w_gate_up[l], b_gate_up[l], w_down[l], b_down[l], g_final)
```

```python
import functools
import math

import numpy as np
import jax
import jax.numpy as jnp
from jax import lax
from jax.experimental import pallas as pl
from jax.experimental.pallas import tpu as pltpu
from jax.experimental.pallas import tpu_sc as plsc

F32 = jnp.float32
BF16 = jnp.bfloat16
I32 = jnp.int32
HI = lax.Precision.HIGHEST

D_MODEL = 1024
HEAD_DIM = 64
MOBA_HEADS = 6
RWKV_HEADS = 6
MEM_HEADS = 4
MOBA_W = MOBA_HEADS * HEAD_DIM
RWKV_W = RWKV_HEADS * HEAD_DIM
MEM_W = MEM_HEADS * HEAD_DIM
MOBA_BLOCK = 256
MOBA_TOPK = 3
N_BUCKETS = 32
MAX_DISTANCE = 128
DECAY_LORA = 64
AAA_LORA = 64
GATE_LORA = 128
RWKV_COLS = 3 * RWKV_W + DECAY_LORA + AAA_LORA + GATE_LORA
RWKV_GN_EPS = 64e-5
IN_COLS = 3 * MOBA_W + RWKV_COLS + MEM_W
N_EXPERTS = 32
TOP_K = 4
D_EXPERT = D_MODEL
SWIGLU_ALPHA = 1.702
SWIGLU_LIMIT = 7.0
RMS_EPS = 1e-5

LANES = 128
RWKV_CHUNK = 128
RWKV_SEQS_PER_STEP = 4
EXPERT_ROWS = 256
ROUTE_TOKENS = 512
SC_CORES = 2
SC_SUBCORES = 16
SC_ROWS = 64
SC_INFLIGHT = 2
COMBINE_PARTS = 2
VMEM_LIMIT = 56 * 1024 * 1024
NEG_INF = float("-inf")
HIGH_HALF = -65536
LOG2_E = 1.4426950408889634
MOBA_VROWS = HEAD_DIM + 16


def _cparams(sem):
    return pltpu.CompilerParams(dimension_semantics=sem, vmem_limit_bytes=VMEM_LIMIT)


def _rms(x, g):
    return x * lax.rsqrt(jnp.mean(x * x, axis=-1, keepdims=True) + RMS_EPS) * g


def _bdot(a, b):
    return jnp.dot(a.astype(BF16), b.astype(BF16), preferred_element_type=F32)


def _hdot(a, b):
    return jnp.dot(a, b, preferred_element_type=F32, precision=HI)


def _dot_nt(a, b):
    return lax.dot_general(a, b, (((1,), (1,)), ((), ())), preferred_element_type=F32)


def _dot_tn(a, b):
    return lax.dot_general(a, b, (((0,), (0,)), ((), ())), preferred_element_type=F32)


def _split_bf16(x, terms):
    parts = []
    for _ in range(terms):
        hi = x.astype(BF16)
        parts.append(hi)
        x = x - hi.astype(F32)
    return parts


def _dot_exact_rhs(x, m_bf16, terms):
    acc = None
    for part in _split_bf16(x, terms):
        d = jnp.dot(part, m_bf16, preferred_element_type=F32)
        acc = d if acc is None else acc + d
    return acc


def _pack_halves(x):
    n = x.shape[1] // 2
    lo = pltpu.bitcast(x[:, :n].astype(BF16).astype(F32), I32)
    hi = pltpu.bitcast(x[:, n:].astype(BF16).astype(F32), I32)
    return (hi & HIGH_HALF) | lax.shift_right_logical(lo, 16)


def _unpack_halves(w):
    lo = pltpu.bitcast(w << 16, F32)
    hi = pltpu.bitcast(w & HIGH_HALF, F32)
    return jnp.concatenate([lo, hi], axis=1)


def _norm_matmul_kernel(x_ref, g_ref, w_ref, o_ref):
    h = _rms(x_ref[...], g_ref[...])
    o_ref[...] = jnp.dot(h.astype(BF16), w_ref[...], preferred_element_type=F32)


def _norm_matmul(x, g, w_bf16, tm, name):
    t, d = x.shape
    n = w_bf16.shape[1]
    tm = min(tm, t)
    return pl.pallas_call(
        _norm_matmul_kernel,
        grid=(t // tm,),
        in_specs=[pl.BlockSpec((tm, d), lambda i: (i, 0)),
                  pl.BlockSpec((1, d), lambda i: (0, 0)),
                  pl.BlockSpec((d, n), lambda i: (0, 0))],
        out_specs=pl.BlockSpec((tm, n), lambda i: (i, 0)),
        out_shape=jax.ShapeDtypeStruct((t, n), F32),
        compiler_params=_cparams(("parallel",)),
        name=name,
    )(x, g.reshape(1, d), w_bf16)


def _t5_bucket_np(dist):
    n = np.maximum(dist, 0)
    max_exact = N_BUCKETS // 2
    nf = np.maximum(n, 1).astype(np.float64)
    large = max_exact + (np.log(nf / max_exact) / math.log(MAX_DISTANCE / max_exact)
                         * (N_BUCKETS - max_exact)).astype(np.int64)
    large = np.minimum(large, N_BUCKETS - 1)
    return np.where(n < max_exact, n, large)


def _moba_bias_tables(rel_bias):
    n = MOBA_BLOCK
    assert np.all(_t5_bucket_np(np.arange(n + 1, 64 * n)) == N_BUCKETS - 1)
    bias_t = rel_bias.astype(F32).T * LOG2_E

    def by_distance(dist):
        bucket = jnp.asarray(_t5_bucket_np(dist), I32)[None]
        tab = jnp.zeros((MOBA_HEADS, dist.shape[0]), F32)
        for b in range(N_BUCKETS):
            tab = jnp.where(bucket == b, bias_t[:, b][:, None], tab)
        return tab

    d = np.arange(2 * n)
    g_own = jnp.where(jnp.asarray(d < n)[None], by_distance(np.where(d < n, d, 0)), NEG_INF)
    g_prev = by_distance(np.where(d < n, n + d, d - n))
    far = bias_t[:, N_BUCKETS - 1]
    return g_own[:, None, :], g_prev[:, None, :], far


def _moba_kernel(far_ref, q_ref, k_ref, v_ref, g0_ref, g1_ref, o_ref,
                 qt_s, kb_s, vt_s, sc_s, t0_ref, t1_ref, *, nb):
    hp = pl.program_id(1)
    blk = MOBA_BLOCK
    scale = HEAD_DIM ** -0.5 * LOG2_E
    row_head = lax.broadcasted_iota(I32, (LANES, 1), 0) // HEAD_DIM
    lane_head = lax.broadcasted_iota(I32, (1, LANES), 1) // HEAD_DIM
    ones_rows = jnp.ones((MOBA_VROWS - HEAD_DIM, blk), BF16)

    for h in range(2):
        for g_ref, t_ref in ((g0_ref, t0_ref), (g1_ref, t1_ref)):
            rows = jnp.broadcast_to(g_ref[h], (blk, 2 * blk))
            t_ref[h] = pltpu.roll(rows, 0, 1, stride=1, stride_axis=0)[:, :blk]

    kmean_rows = []
    for j in range(nb):
        sl = pl.ds(j * blk, blk)
        qt_s[j] = (q_ref[0, sl, :] * scale).T
        vt = v_ref[0, sl, :].T.astype(BF16)
        for h in range(2):
            vt_s[j, h * MOBA_VROWS:h * MOBA_VROWS + HEAD_DIM, :] = vt[h * HEAD_DIM:(h + 1) * HEAD_DIM, :]
            vt_s[j, h * MOBA_VROWS + HEAD_DIM:(h + 1) * MOBA_VROWS, :] = ones_rows
        kj = k_ref[0, sl, :]
        kb_s[j] = kj.astype(BF16)
        kmean_rows.append(jnp.mean(kj, axis=0, keepdims=True))
    kmean = jnp.concatenate(kmean_rows, axis=0)
    km2 = jnp.concatenate([jnp.where(lane_head == 0, kmean, 0.0),
                           jnp.where(lane_head == 1, kmean, 0.0)], axis=0)
    blk_iota = lax.broadcasted_iota(I32, (nb, blk), 0)

    def score_pass(i):
        qt = qt_s[i]
        negs = [None, None]
        if i > MOBA_TOPK:
            gate2 = _hdot(km2, qt)
            for h in range(2):
                g = gate2[h * nb:(h + 1) * nb, :]
                cnt = jnp.zeros((nb, blk), I32)
                for m in range(i):
                    gm = g[m:m + 1, :]
                    beats = (gm > g) | ((gm == g) & (m < blk_iota))
                    cnt = cnt + jnp.where(beats, 1, 0)
                negs[h] = jnp.where(cnt < MOBA_TOPK, 0.0, NEG_INF)
        col_max = []
        for h in range(2):
            qt_h = jnp.where(row_head == h, qt, 0.0).astype(BF16)
            far_bias = far_ref[2 * hp + h]
            m_run = None
            for j in range(i + 1):
                s = jnp.dot(kb_s[j], qt_h, preferred_element_type=F32)
                if j == i:
                    s = s + t0_ref[h]
                else:
                    bias = t1_ref[h] if j == i - 1 else far_bias
                    if negs[h] is not None:
                        bias = bias + negs[h][j:j + 1, :]
                    s = s + bias
                sc_s[i % 2, h, j] = s
                cm = jnp.max(s, axis=0, keepdims=True)
                m_run = cm if m_run is None else jnp.maximum(m_run, cm)
            col_max.append(m_run)
        return col_max

    def prob_pass(i, col_max):
        outs = []
        for h in range(2):
            m_run = col_max[h]
            probs = jnp.concatenate([jnp.exp2(sc_s[i % 2, h, j] - m_run).astype(BF16)
                                     for j in range(i + 1)], axis=0)
            vals = jnp.concatenate([vt_s[j, h * MOBA_VROWS:(h + 1) * MOBA_VROWS, :]
                                    for j in range(i + 1)], axis=1)
            acc = jnp.dot(vals, probs, preferred_element_type=F32)
            outs.append(acc[:HEAD_DIM] / acc[HEAD_DIM:HEAD_DIM + 1])
        o_t = jnp.concatenate(outs, axis=0)
        o_ref[0, pl.ds(i * blk, blk), :] = o_t.T

    pending = None
    for i in range(nb):
        col_max = score_pass(i)
        if pending is not None:
            prob_pass(*pending)
        pending = (i, col_max)
    prob_pass(*pending)


def _moba_attention(p3, g_own, g_prev, far):
    b, s, _ = p3.shape
    nb = s // MOBA_BLOCK
    n_pairs = MOBA_HEADS // 2
    blk_spec = lambda off: pl.BlockSpec((1, s, LANES), lambda bi, hp: (bi, 0, off + hp))
    tab_spec = pl.BlockSpec((2, 1, 2 * MOBA_BLOCK), lambda bi, hp: (hp, 0, 0))
    return pl.pallas_call(
        functools.partial(_moba_kernel, nb=nb),
        grid=(b, n_pairs),
        in_specs=[pl.BlockSpec(memory_space=pltpu.SMEM),
                  blk_spec(0), blk_spec(n_pairs), blk_spec(2 * n_pairs), tab_spec, tab_spec],
        out_specs=pl.BlockSpec((1, s, LANES), lambda bi, hp: (bi, 0, hp)),
        out_shape=jax.ShapeDtypeStruct((b, s, MOBA_W), F32),
        scratch_shapes=[pltpu.VMEM((nb, LANES, MOBA_BLOCK), F32),
                        pltpu.VMEM((nb, MOBA_BLOCK, LANES), BF16),
                        pltpu.VMEM((nb, 2 * MOBA_VROWS, MOBA_BLOCK), BF16),
                        pltpu.VMEM((2, 2, nb, MOBA_BLOCK, MOBA_BLOCK), F32),
                        pltpu.VMEM((2, MOBA_BLOCK, MOBA_BLOCK), F32),
                        pltpu.VMEM((2, MOBA_BLOCK, MOBA_BLOCK), F32)],
        compiler_params=_cparams(("parallel", "parallel")),
        name="moba",
    )(far, p3, p3, p3, g_own, g_prev)


def _softplus(z):
    return jnp.maximum(z, 0.0) + jnp.log(1.0 + jnp.exp(-jnp.abs(z)))


def _sigmoid(z):
    return 1.0 / (1.0 + jnp.exp(-z))


def _rwkv_kernel(r_ref, k_ref, v_ref, wa_ref, g_ref,
                 mu_r_ref, mu_k_ref, mu_v_ref, mu_wa_ref, mu_g_ref,
                 w0_ref, wup_ref, a0_ref, aup_ref, gup_ref, kk_ref, ka_ref, rk_ref,
                 gng_ref, gnb_ref, hsum_ref, tri_ref,
                 o_ref, st_s, prev_r, prev_k, prev_v, prev_wa, prev_g, *, n_seq):
    c = RWKV_CHUNK
    mid = c // 2

    @pl.when(pl.program_id(1) == 0)
    def _():
        st_s[...] = jnp.zeros_like(st_s)
        prev_r[...] = jnp.zeros_like(prev_r)
        prev_k[...] = jnp.zeros_like(prev_k)
        prev_v[...] = jnp.zeros_like(prev_v)
        prev_wa[...] = jnp.zeros_like(prev_wa)
        prev_g[...] = jnp.zeros_like(prev_g)

    n_pairs = RWKV_HEADS // 2
    hsum = hsum_ref[...]
    tri = tri_ref[...]
    row = lax.broadcasted_iota(I32, (c, c), 0)
    col = lax.broadcasted_iota(I32, (c, c), 1)
    strict = row > col
    incl = row >= col
    eye = (row == col).astype(F32)
    lane_half = lax.broadcasted_iota(I32, (1, LANES), 1) // HEAD_DIM
    hmask = hsum.astype(F32)
    dot = functools.partial(jnp.dot, preferred_element_type=F32)
    lanes = lambda x, jp: x[:, jp * LANES:(jp + 1) * LANES]

    def group_sum(x):
        return jnp.concatenate([_dot_exact_rhs(lanes(x, jp), hsum, 2) for jp in range(n_pairs)], axis=1)

    def two_heads(x):
        return jnp.concatenate([jnp.where(lane_half == 0, x, 0.0), jnp.where(lane_half == 1, x, 0.0)],
                               axis=0).astype(BF16)

    def side(a, b):
        return jnp.concatenate([a, b], axis=1)

    def mix(g, x_ref, prev_ref, mu_ref):
        x = x_ref[g]
        rows = lax.broadcasted_iota(I32, x.shape, 0)
        shifted = jnp.where(rows == 0, prev_ref[g], pltpu.roll(x, 1, axis=0))
        prev_ref[g] = x[c - 1:c, :]
        return x + (shifted - x) * mu_ref[...]

    def prepare(g):
        r = mix(g, r_ref, prev_r, mu_r_ref)
        k = mix(g, k_ref, prev_k, mu_k_ref)
        v = mix(g, v_ref, prev_v, mu_v_ref)
        xwa = mix(g, wa_ref, prev_wa, mu_wa_ref)
        xg = mix(g, g_ref, prev_g, mu_g_ref)

        w = -_softplus(-(w0_ref[...] + _bdot(jnp.tanh(xwa), wup_ref[...]))) - 0.5
        logd = -jnp.exp(w)
        a = _sigmoid(a0_ref[...] + _bdot(xwa, aup_ref[...]))
        gate = _bdot(_sigmoid(xg), gup_ref[...])
        kk = k * kk_ref[...]
        kk = kk / jnp.maximum(jnp.sqrt(group_sum(kk * kk)), 1e-12)
        k2 = k * (1.0 + (a - 1.0) * ka_ref[...])

        parts = _split_bf16(logd, 3)
        cum = dot(tri, parts[0]) + dot(tri, parts[1]) + dot(tri, parts[2])
        cum_last = cum[c - 1:c, :]
        ref = cum[mid - 1:mid, :]
        rel = cum - ref
        g_in = jnp.exp(rel)
        g_inv = jnp.exp(-rel)
        g_tail = jnp.exp(cum_last - cum)
        a_t = -kk * jnp.exp(rel - logd)
        b_t = kk * a * g_inv
        k_t = k2 * g_inv
        r_t = r * g_in
        b_hat = kk * a * g_tail
        k_hat = k2 * g_tail

        ref_scale = jnp.exp(ref)
        decay = jnp.exp(cum_last)
        bk = jnp.concatenate([b_t, k_t], axis=0).astype(BF16)
        ar_all = jnp.concatenate([a_t, r_t], axis=0)
        per_pair = []
        for jp in range(n_pairs):
            st = st_s[g, jp]
            st_ref = (st * lanes(ref_scale, jp)).astype(BF16)
            both = _dot_nt(lanes(ar_all, jp).astype(BF16), st_ref)
            quads = []
            for half in range(2):
                ar = jnp.where(lane_half == half, lanes(ar_all, jp), 0.0)
                m4 = _dot_nt(ar.astype(BF16), lanes(bk, jp))
                quads.append((jnp.where(strict, m4[:c, :c], 0.0),
                              jnp.where(strict, m4[:c, c:], 0.0).astype(BF16),
                              jnp.where(incl, m4[c:, :c], 0.0).astype(BF16),
                              jnp.where(incl, m4[c:, c:], 0.0).astype(BF16)))
            per_pair.append(dict(
                st=st, p0=both[:c], o0=both[c:], v_rows=two_heads(lanes(v, jp)),
                l_ab=[q[0] for q in quads], l_ak=side(quads[0][1], quads[1][1]),
                m_rb=side(quads[0][2], quads[1][2]), m_rk=side(quads[0][3], quads[1][3]),
                v=lanes(v, jp).astype(BF16), b_hat=lanes(b_hat, jp).astype(BF16),
                k_hat=lanes(k_hat, jp).astype(BF16), decay=lanes(decay, jp)))
        return dict(r=r, k2=k2, v=v, gate=gate, pairs=per_pair)

    seqs = [prepare(g) for g in range(n_seq)]
    heads = [(g, jp, half) for g in range(n_seq) for jp in range(n_pairs) for half in range(2)]

    pw = {gh: seqs[gh[0]]["pairs"][gh[1]]["l_ab"][gh[2]] for gh in heads}
    t_inv = {gh: eye + pw[gh] for gh in heads}
    pw = {gh: pw[gh].astype(BF16) for gh in heads}
    span = 2
    while span < c:
        pw = {gh: dot(pw[gh], pw[gh]).astype(BF16) for gh in heads}
        t_inv = {gh: t_inv[gh] + dot(t_inv[gh].astype(BF16), pw[gh]) for gh in heads}
        span *= 2

    for g in range(n_seq):
        sq = seqs[g]
        outs = []
        for jp in range(n_pairs):
            pr = sq["pairs"][jp]
            t_pair = side(t_inv[(g, jp, 0)], t_inv[(g, jp, 1)]).astype(BF16)
            w_loc = pr["p0"] + dot(pr["l_ak"], pr["v_rows"])
            u = dot(t_pair, two_heads(w_loc))
            outs.append(pr["o0"] + dot(pr["m_rb"], two_heads(u)) + dot(pr["m_rk"], pr["v_rows"]))
            upd = _dot_tn(u.astype(BF16), pr["b_hat"]) + _dot_tn(pr["v"], pr["k_hat"])
            st_s[g, jp] = pr["st"] * pr["decay"] + hmask * upd
        o = jnp.concatenate(outs, axis=1)

        inv_n = 1.0 / HEAD_DIM
        mean = group_sum(o) * inv_n
        dev = o - mean
        var = group_sum(dev * dev) * inv_n
        y = dev * lax.rsqrt(var + RWKV_GN_EPS) * gng_ref[...] + gnb_ref[...]
        y = y + group_sum(sq["r"] * sq["k2"] * rk_ref[...]) * sq["v"]
        o_ref[g] = y * sq["gate"]


def _rwkv(p3, mu, w0, w_up, a0, a_up, g_up, k_k, k_a, r_k, gn_g, gn_b):
    b, s, _ = p3.shape
    c = RWKV_CHUNK
    w = RWKV_W
    n_seq = RWKV_SEQS_PER_STEP if b % RWKV_SEQS_PER_STEP == 0 else 1
    base = 3 * MOBA_W
    assert base % w == 0 and (base + 3 * w) % LANES == 0
    lora = DECAY_LORA + AAA_LORA

    def col_spec(width, off_cols):
        assert off_cols % width == 0
        return pl.BlockSpec((n_seq, c, width), lambda bi, ci: (bi, ci, off_cols // width))

    row = lambda x: x.reshape(1, -1).astype(F32)
    const = lambda shape: pl.BlockSpec(shape, lambda bi, ci: (0,) * len(shape))
    wup_pad = jnp.concatenate([w_up, jnp.zeros((AAA_LORA, w), F32)], axis=0).astype(BF16)
    aup_pad = jnp.concatenate([jnp.zeros((DECAY_LORA, w), F32), a_up], axis=0).astype(BF16)
    head = np.arange(LANES) // HEAD_DIM
    hsum = jnp.asarray((head[:, None] == head[None, :]).astype(np.float32)).astype(BF16)
    tri = jnp.asarray(np.tril(np.ones((c, c), np.float32))).astype(BF16)
    vec_args = [mu[:w], mu[w:2 * w], mu[2 * w:3 * w], mu[3 * w:3 * w + lora], mu[3 * w + lora:],
                w0, None, a0, None, None, k_k, k_a, r_k.reshape(-1), gn_g, gn_b]
    args = [p3, p3, p3, p3, p3]
    specs = [col_spec(w, base), col_spec(w, base + w), col_spec(w, base + 2 * w),
             col_spec(lora, base + 3 * w), col_spec(GATE_LORA, base + 3 * w + lora)]
    mats = {6: wup_pad, 8: aup_pad, 9: g_up.astype(BF16)}
    for idx, a in enumerate(vec_args):
        arr = mats[idx] if a is None else row(a)
        args.append(arr)
        specs.append(const(arr.shape))
    for arr in (hsum, tri):
        args.append(arr)
        specs.append(const(arr.shape))
    return pl.pallas_call(
        functools.partial(_rwkv_kernel, n_seq=n_seq),
        grid=(b // n_seq, s // c),
        in_specs=specs,
        out_specs=pl.BlockSpec((n_seq, c, w), lambda bi, ci: (bi, ci, 0)),
        out_shape=jax.ShapeDtypeStruct((b, s, w), F32),
        scratch_shapes=[pltpu.VMEM((n_seq, RWKV_HEADS // 2, LANES, LANES), F32),
                        pltpu.VMEM((n_seq, 1, w), F32),
                        pltpu.VMEM((n_seq, 1, w), F32), pltpu.VMEM((n_seq, 1, w), F32),
                        pltpu.VMEM((n_seq, 1, lora), F32), pltpu.VMEM((n_seq, 1, GATE_LORA), F32)],
        compiler_params=_cparams(("parallel", "arbitrary")),
        name="rwkv",
    )(*args)


def _mem_attn_kernel(q_ref, kv_ref, o_ref):
    scale = HEAD_DIM ** -0.5
    lane_head = lax.broadcasted_iota(I32, (1, MEM_W), 1) // HEAD_DIM
    q = q_ref[0] * scale
    mk = kv_ref[0, :, :MEM_W].astype(BF16)
    mv = kv_ref[0, :, MEM_W:].astype(BF16)
    out = jnp.zeros(q.shape, F32)
    for h in range(MEM_HEADS):
        hm = lane_head == h
        s = _dot_nt(jnp.where(hm, q, 0.0).astype(BF16), mk)
        s = s - jnp.max(s, axis=-1, keepdims=True)
        e = jnp.exp(s)
        p = e / jnp.sum(e, axis=-1, keepdims=True)
        out = jnp.where(hm, jnp.dot(p.astype(BF16), mv, preferred_element_type=F32), out)
    o_ref[0] = out


def _mem_attention(p3, mkv3, tq):
    b, s, _ = p3.shape
    m = mkv3.shape[1]
    off = (3 * MOBA_W + RWKV_COLS) // MEM_W
    assert off * MEM_W == 3 * MOBA_W + RWKV_COLS
    return pl.pallas_call(
        _mem_attn_kernel,
        grid=(b, s // tq),
        in_specs=[pl.BlockSpec((1, tq, MEM_W), lambda bi, i: (bi, i, off)),
                  pl.BlockSpec((1, m, 2 * MEM_W), lambda bi, i: (bi, 0, 0))],
        out_specs=pl.BlockSpec((1, tq, MEM_W), lambda bi, i: (bi, i, 0)),
        out_shape=jax.ShapeDtypeStruct((b, s, MEM_W), F32),
        compiler_params=_cparams(("parallel", "parallel")),
        name="mem_attn",
    )(p3, mkv3)


def _out_router_kernel(x_ref, ym_ref, yr_ref, ye_ref, wo1_ref, wo2_ref, wo3_ref, g_ref,
                       wrh_ref, wrl_ref, br_ref, upper_ref, ones_ref,
                       x1_ref, h_ref, idx_ref, rank_ref, wgt_ref, cnt_ref, run_s):
    @pl.when(pl.program_id(0) == 0)
    def _():
        run_s[...] = jnp.zeros_like(run_s)

    x1 = (x_ref[...] + _bdot(ym_ref[...], wo1_ref[...]) + _bdot(yr_ref[...], wo2_ref[...])
          + _bdot(ye_ref[...], wo3_ref[...]))
    x1_ref[...] = x1
    h = _rms(x1, g_ref[...])
    h_ref[...] = _pack_halves(h)
    tm = h.shape[0]
    h_hi, h_lo = _split_bf16(h, 2)
    dot = functools.partial(jnp.dot, preferred_element_type=F32)
    logits = dot(h_hi, wrh_ref[...]) + (dot(h_hi, wrl_ref[...]) + dot(h_lo, wrh_ref[...])) + br_ref[...]
    lg = logits.T[:N_EXPERTS, :]
    e_iota = lax.broadcasted_iota(I32, (N_EXPERTS, tm), 0)

    vals, idxs = [], []
    for _ in range(TOP_K):
        m = jnp.max(lg, axis=0, keepdims=True)
        idx = jnp.min(jnp.where(lg == m, e_iota, N_EXPERTS), axis=0, keepdims=True)
        vals.append(m)
        idxs.append(idx)
        lg = jnp.where(e_iota == idx, NEG_INF, lg)
    exps = [jnp.exp(vk - vals[0]) for vk in vals]
    denom = exps[0] + exps[1] + exps[2] + exps[3]

    chosen = jnp.zeros((N_EXPERTS, tm), F32)
    for idx in idxs:
        chosen = chosen + jnp.where(e_iota == idx, 1.0, 0.0)
    chosen = chosen.astype(BF16)
    run = run_s[...]
    before = dot(chosen, upper_ref[...]) + jnp.concatenate([run] * (tm // LANES), axis=1)
    run_s[...] = run + dot(chosen, ones_ref[...])
    cnt_ref[...] = run_s[...]

    zero_i = jnp.zeros((8 - TOP_K, tm), I32)
    ranks = [jnp.sum(jnp.where(e_iota == idx, before, 0.0), axis=0, keepdims=True).astype(I32)
             for idx in idxs]
    idx_ref[0] = jnp.concatenate(idxs + [zero_i], axis=0)
    rank_ref[0] = jnp.concatenate(ranks + [zero_i], axis=0)
    wrows = jnp.concatenate([e / denom for e in exps] + [jnp.zeros((LANES - TOP_K, tm), F32)], axis=0)
    wgt_ref[...] = wrows.T


def _out_router(x2, ym, yr, ye, w_out, g_ffn, w_router, b_router, tm):
    t, d = x2.shape
    wo = w_out.astype(BF16)
    wo1, wo2, wo3 = wo[:MOBA_W], wo[MOBA_W:MOBA_W + RWKV_W], wo[MOBA_W + RWKV_W:]
    wr = jnp.zeros((d, LANES), F32).at[:, :N_EXPERTS].set(w_router)
    wr_hi = wr.astype(BF16)
    wr_lo = (wr - wr_hi.astype(F32)).astype(BF16)
    br = jnp.full((1, LANES), NEG_INF, F32).at[0, :N_EXPERTS].set(b_router)
    upper = jnp.asarray(np.triu(np.ones((tm, tm), np.float32), 1)).astype(BF16)
    ones = jnp.ones((tm, LANES), BF16)
    tile = lambda n: pl.BlockSpec((tm, n), lambda i: (i, 0))
    slots = pl.BlockSpec((1, 8, tm), lambda i: (i, 0, 0))
    const = lambda a: pl.BlockSpec(a.shape, lambda i: (0,) * a.ndim)
    g2 = g_ffn.reshape(1, d)
    n = t // tm
    return pl.pallas_call(
        _out_router_kernel,
        grid=(n,),
        in_specs=[tile(d), tile(MOBA_W), tile(RWKV_W), tile(MEM_W), const(wo1), const(wo2), const(wo3),
                  const(g2), const(wr_hi), const(wr_lo), const(br), const(upper), const(ones)],
        out_specs=[tile(d), tile(d // 2), slots, slots, tile(LANES),
                   pl.BlockSpec((N_EXPERTS, LANES), lambda i: (0, 0))],
        out_shape=[jax.ShapeDtypeStruct((t, d), F32), jax.ShapeDtypeStruct((t, d // 2), I32),
                   jax.ShapeDtypeStruct((n, 8, tm), I32), jax.ShapeDtypeStruct((n, 8, tm), I32),
                   jax.ShapeDtypeStruct((t, LANES), F32), jax.ShapeDtypeStruct((N_EXPERTS, LANES), F32)],
        scratch_shapes=[pltpu.VMEM((N_EXPERTS, LANES), F32)],
        compiler_params=_cparams(("arbitrary",)),
        name="out_router",
    )(x2, ym, yr, ye, wo1, wo2, wo3, g2, wr_hi, wr_lo, br, upper, ones)


def _sc_mesh():
    return plsc.VectorSubcoreMesh(core_axis_name="c", subcore_axis_name="s",
                                  num_cores=SC_CORES, num_subcores=SC_SUBCORES)


def _sc_split(n_rows):
    n_workers = SC_CORES * SC_SUBCORES
    per_worker = n_rows // n_workers
    assert per_worker * n_workers == n_rows and per_worker % (SC_ROWS * SC_INFLIGHT) == 0
    assert SC_ROWS % 8 == 0
    return per_worker


def _sc_gather_rows(table, idx):
    n_rows = idx.shape[0]
    width = table.shape[1]
    per_worker = _sc_split(n_rows)

    buf = lambda shape, dtype: [pltpu.VMEM(shape, dtype) for _ in range(SC_INFLIGHT)]

    @functools.partial(
        pl.kernel, mesh=_sc_mesh(),
        out_type=jax.ShapeDtypeStruct((n_rows, width), table.dtype),
        scratch_types=[buf((SC_ROWS,), I32), buf((SC_ROWS, width), table.dtype),
                       [pltpu.SemaphoreType.DMA for _ in range(SC_INFLIGHT)]],
        name="sc_gather",
    )
    def gather(table_hbm, idx_hbm, out_hbm, idx_v, rows_v, sems):
        worker = lax.axis_index("s") * SC_CORES + lax.axis_index("c")
        base = worker * per_worker

        @pl.loop(0, per_worker // (SC_ROWS * SC_INFLIGHT))
        def _(step):
            offs = [pl.multiple_of(base + (step * SC_INFLIGHT + b) * SC_ROWS, 8) for b in range(SC_INFLIGHT)]
            copies = []
            for b in range(SC_INFLIGHT):
                pltpu.sync_copy(idx_hbm.at[pl.ds(offs[b], SC_ROWS)], idx_v[b])
                copies.append(pltpu.async_copy(table_hbm.at[idx_v[b]], rows_v[b], sems[b]))
            for b in range(SC_INFLIGHT):
                copies[b].wait()
                pltpu.sync_copy(rows_v[b], out_hbm.at[pl.ds(offs[b], SC_ROWS)])

    return gather(table, idx)


def _sc_scatter_rows(rows, idx, n_out):
    n_src, width = rows.shape
    n_copies = idx.shape[0] // n_src
    assert n_copies * n_src == idx.shape[0]
    per_worker = _sc_split(n_src)

    @functools.partial(
        pl.kernel, mesh=_sc_mesh(),
        out_type=jax.ShapeDtypeStruct((n_out, width), rows.dtype),
        scratch_types=[[pltpu.VMEM((SC_ROWS,), I32) for _ in range(n_copies)],
                       pltpu.VMEM((SC_ROWS, width), rows.dtype),
                       [pltpu.SemaphoreType.DMA for _ in range(n_copies)]],
        name="sc_scatter",
    )
    def scatter(rows_hbm, idx_hbm, out_hbm, idx_v, rows_v, sems):
        worker = lax.axis_index("s") * SC_CORES + lax.axis_index("c")
        base = worker * per_worker

        @pl.loop(0, per_worker // SC_ROWS)
        def _(step):
            src = pl.multiple_of(base + step * SC_ROWS, 8)
            pltpu.sync_copy(rows_hbm.at[pl.ds(src, SC_ROWS)], rows_v)
            copies = []
            for cpy in range(n_copies):
                off = pl.multiple_of(cpy * n_src + src, 8)
                pltpu.sync_copy(idx_hbm.at[pl.ds(off, SC_ROWS)], idx_v[cpy])
                copies.append(pltpu.async_copy(rows_v, out_hbm.at[idx_v[cpy]], sems[cpy]))
            for cp in copies:
                cp.wait()

    return scatter(rows, idx)


def _experts_kernel(be_ref, bv_ref, nused_ref, next_ref, slot_ref,
                    xs_ref, wgu_hbm, bg_ref, bu_ref, wd_hbm, bd_ref, perm_ref,
                    o_ref, wgu_buf, wd_buf, wg_s, wu_s, wd_s, sem):
    i = pl.program_id(0)
    e = be_ref[i]
    used = i < nused_ref[0]
    changed = ((i == 0) | (e != be_ref[jnp.maximum(i - 1, 0)])) & used
    slot = slot_ref[e]

    def weight_copies(expert, buf_slot):
        return (pltpu.make_async_copy(wgu_hbm.at[expert], wgu_buf.at[buf_slot], sem.at[buf_slot, 0]),
                pltpu.make_async_copy(wd_hbm.at[expert], wd_buf.at[buf_slot], sem.at[buf_slot, 1]))

    @pl.when((i == 0) & used)
    def _():
        for cp in weight_copies(e, slot):
            cp.start()

    @pl.when(changed)
    def _():
        for cp in weight_copies(e, slot):
            cp.wait()
        nxt = next_ref[e]

        @pl.when(nxt >= 0)
        def _():
            for cp in weight_copies(nxt, 1 - slot):
                cp.start()

        half = LANES
        for cblk in range(2 * D_EXPERT // (2 * half)):
            wt = wgu_buf[slot, :, cblk * 2 * half:(cblk + 1) * 2 * half].astype(BF16)
            sep = jnp.dot(wt, perm_ref[...], preferred_element_type=F32).astype(BF16)
            wg_s[:, cblk * half:(cblk + 1) * half] = sep[:, :half]
            wu_s[:, cblk * half:(cblk + 1) * half] = sep[:, half:]
        wd_s[...] = wd_buf[slot].astype(BF16)

    @pl.when(used)
    def _():
        rows = lax.broadcasted_iota(I32, xs_ref.shape, 0)
        xb = _unpack_halves(jnp.where(rows < bv_ref[i], xs_ref[...], 0)).astype(BF16)
        gate = jnp.dot(xb, wg_s[...], preferred_element_type=F32) + bg_ref[0]
        up = jnp.dot(xb, wu_s[...], preferred_element_type=F32) + bu_ref[0]
        gate = jnp.minimum(gate, SWIGLU_LIMIT)
        up = jnp.clip(up, -SWIGLU_LIMIT, SWIGLU_LIMIT)
        glu = gate * _sigmoid(gate * SWIGLU_ALPHA)
        act = ((up + 1.0) * glu).astype(BF16)
        o_ref[...] = _pack_halves(jnp.dot(act, wd_s[...], preferred_element_type=F32) + bd_ref[0])

    @pl.when(jnp.logical_not(used))
    def _():
        o_ref[...] = jnp.zeros_like(o_ref)


def _experts(blk_expert, blk_valid, n_used, next_expert, buf_slot, xs, w_gate_up, b_gate_up, w_down,
             b_down):
    p_rows = xs.shape[0]
    d = 2 * xs.shape[1]
    n_blocks = p_rows // EXPERT_ROWS
    bg = b_gate_up[:, 0::2].reshape(N_EXPERTS, 1, D_EXPERT)
    bu = b_gate_up[:, 1::2].reshape(N_EXPERTS, 1, D_EXPERT)
    bd = b_down.reshape(N_EXPERTS, 1, d)
    perm_np = np.zeros((2 * LANES, 2 * LANES), np.float32)
    perm_np[2 * np.arange(LANES), np.arange(LANES)] = 1.0
    perm_np[2 * np.arange(LANES) + 1, LANES + np.arange(LANES)] = 1.0
    perm = jnp.asarray(perm_np).astype(BF16)
    by_expert = lambda shape: pl.BlockSpec((1,) + shape, lambda i, be, *_: (be[i], 0, 0))
    grid_spec = pltpu.PrefetchScalarGridSpec(
        num_scalar_prefetch=5,
        grid=(n_blocks,),
        in_specs=[pl.BlockSpec((EXPERT_ROWS, d // 2), lambda i, *_: (i, 0)),
                  pl.BlockSpec(memory_space=pl.ANY), by_expert((1, D_EXPERT)), by_expert((1, D_EXPERT)),
                  pl.BlockSpec(memory_space=pl.ANY), by_expert((1, d)),
                  pl.BlockSpec(perm.shape, lambda i, *_: (0, 0))],
        out_specs=pl.BlockSpec((EXPERT_ROWS, d // 2), lambda i, *_: (i, 0)),
        scratch_shapes=[pltpu.VMEM((2, d, 2 * D_EXPERT), F32), pltpu.VMEM((2, D_EXPERT, d), F32),
                        pltpu.VMEM((d, D_EXPERT), BF16), pltpu.VMEM((d, D_EXPERT), BF16),
                        pltpu.VMEM((D_EXPERT, d), BF16), pltpu.SemaphoreType.DMA((2, 2))],
    )
    return pl.pallas_call(
        _experts_kernel,
        grid_spec=grid_spec,
        out_shape=jax.ShapeDtypeStruct((p_rows, d // 2), I32),
        compiler_params=_cparams(("arbitrary",)),
        name="experts",
    )(blk_expert, blk_valid, n_used, next_expert, buf_slot, xs, w_gate_up, bg, bu, w_down, bd, perm)


def _combine_kernel(yg_ref, x1_ref, w_ref, g_ref, *rest):
    o_ref = rest[-1]
    acc = x1_ref[...]
    wts = w_ref[...]
    for kk in range(TOP_K):
        acc = acc + _unpack_halves(yg_ref[kk]) * wts[:, kk:kk + 1]
    o_ref[...] = _rms(acc, g_ref[...])


def _combine(yg, x1, wts, g_final, tb, first_tile, partial_out):
    t, d = x1.shape
    half = yg.shape[2]
    tile = lambda n: pl.BlockSpec((tb, n), lambda i: (i + first_tile, 0))
    in_specs = [pl.BlockSpec((TOP_K, tb, half), lambda i: (0, i, 0)), tile(d), tile(LANES),
                pl.BlockSpec((1, d), lambda i: (0, 0))]
    args = [yg, x1, wts, g_final.reshape(1, d)]
    aliases = {}
    if partial_out is not None:
        in_specs.append(pl.BlockSpec(memory_space=pl.ANY))
        args.append(partial_out)
        aliases = {len(args) - 1: 0}
    return pl.pallas_call(
        _combine_kernel,
        grid=(yg.shape[1] // tb,),
        in_specs=in_specs,
        out_specs=tile(d),
        out_shape=jax.ShapeDtypeStruct((t, d), F32),
        input_output_aliases=aliases,
        compiler_params=_cparams(("parallel",)),
        name="combine",
    )(*args)


def _layer(x, mem, w_in, w_out, w_mem_kv, g_mix, g_mem, g_ffn, bias_tables, mu, w0, w_up, a0, a_up,
           g_up, k_k, k_a, r_k, gn_g, gn_b, w_router, b_router, w_gate_up, b_gate_up, w_down, b_down,
           g_last):
    b, s, d = x.shape
    m = mem.shape[1]
    t = b * s
    x2 = x.reshape(t, d)

    p = _norm_matmul(x2, g_mix, w_in.astype(BF16), 512, "in_proj")
    p3 = p.reshape(b, s, IN_COLS)
    y_moba = _moba_attention(p3, *bias_tables)
    y_rwkv = _rwkv(p3, mu, w0, w_up, a0, a_up, g_up, k_k, k_a, r_k, gn_g, gn_b)
    mkv = _norm_matmul(mem.reshape(b * m, d), g_mem, w_mem_kv.astype(BF16), 512, "mem_kv")
    y_mem = _mem_attention(p3, mkv.reshape(b, m, 2 * MEM_W), 512)

    tb = ROUTE_TOKENS
    x1, h2, idx_o, rank_o, wgt_p, cnt = _out_router(
        x2, y_moba.reshape(t, MOBA_W), y_rwkv.reshape(t, RWKV_W), y_mem.reshape(t, MEM_W),
        w_out, g_ffn, w_router, b_router, tb)

    counts = cnt[:, 0].astype(I32)
    padded = (counts + EXPERT_ROWS - 1) // EXPERT_ROWS * EXPERT_ROWS
    pad_ends = jnp.cumsum(padded)
    pad_starts = (pad_ends - padded).astype(I32)
    n_blocks = (t * TOP_K) // EXPERT_ROWS + N_EXPERTS
    blk_start = jnp.arange(n_blocks, dtype=I32) * EXPERT_ROWS
    blk_expert = jnp.minimum(jnp.sum(blk_start[:, None] >= pad_ends[None, :], axis=1),
                             N_EXPERTS - 1).astype(I32)
    of_block = blk_expert[:, None] == jnp.arange(N_EXPERTS, dtype=I32)[None, :]
    rows_left = jnp.sum(jnp.where(of_block, (counts + pad_starts)[None, :], 0), axis=1) - blk_start
    blk_valid = jnp.clip(rows_left, 0, EXPERT_ROWS)
    n_used = (pad_ends[-1:] // EXPERT_ROWS).astype(I32)
    has_rows = counts > 0
    first_from = lax.cummin(jnp.where(has_rows, jnp.arange(N_EXPERTS, dtype=I32), N_EXPERTS), reverse=True)
    next_expert = jnp.concatenate([first_from[1:], jnp.full((1,), N_EXPERTS, I32)])
    next_expert = jnp.where(next_expert < N_EXPERTS, next_expert, -1).astype(I32)
    buf_slot = ((jnp.cumsum(has_rows.astype(I32)) - has_rows.astype(I32)) % 2).astype(I32)
    idx_kt = jnp.swapaxes(idx_o[:, :TOP_K, :], 0, 1).reshape(TOP_K, t)
    rank_kt = jnp.swapaxes(rank_o[:, :TOP_K, :], 0, 1).reshape(TOP_K, t)
    experts = jnp.arange(N_EXPERTS, dtype=I32)[:, None, None]
    start_kt = jnp.sum(jnp.where(idx_kt[None] == experts, pad_starts[:, None, None], 0), axis=0)
    dest = (start_kt + rank_kt).reshape(TOP_K * t)

    xs = _sc_scatter_rows(h2, dest, n_blocks * EXPERT_ROWS)
    ys = _experts(blk_expert, blk_valid.astype(I32), n_used, next_expert, buf_slot, xs,
                  w_gate_up, b_gate_up, w_down, b_down)
    dest_kt = dest.reshape(TOP_K, t)
    out = None
    for part in range(COMBINE_PARTS):
        t_part = t // COMBINE_PARTS
        rows = dest_kt[:, part * t_part:(part + 1) * t_part].reshape(TOP_K * t_part)
        yg = _sc_gather_rows(ys, rows).reshape(TOP_K, t_part, d // 2)
        out = _combine(yg, x1, wgt_p, g_last, tb, part * (t_part // tb), out)
    return out.reshape(b, s, d)


def kernel(x, mem, w_in, w_out, w_mem_kv, g_mix, g_mem, g_ffn, g_final, rel_bias, rwkv_mu, rwkv_w0,
           rwkv_w_up, rwkv_a0, rwkv_a_up, rwkv_g_up, rwkv_k_k, rwkv_k_a, rwkv_r_k, rwkv_gn_g, rwkv_gn_b,
           w_router, b_router, w_gate_up, b_gate_up, w_down, b_down):
    depth = w_in.shape[0]
    assert depth == 1, "the final norm is fused into the last layer's combine kernel"
    bias_tables = _moba_bias_tables(rel_bias)
    l = 0
    return _layer(x, mem, w_in[l], w_out[l], w_mem_kv[l], g_mix[l], g_mem[l], g_ffn[l], bias_tables,
                  rwkv_mu[l], rwkv_w0[l], rwkv_w_up[l], rwkv_a0[l], rwkv_a_up[l], rwkv_g_up[l],
                  rwkv_k_k[l], rwkv_k_a[l], rwkv_r_k[l], rwkv_gn_g[l], rwkv_gn_b[l], w_router[l],
                  b_router[l], w_gate_up[l], b_gate_up[l], w_down[l], b_down[l], g_final)
```

```python
import functools
import math

import numpy as np
import jax
import jax.numpy as jnp
from jax import lax
from jax.experimental import pallas as pl
from jax.experimental.pallas import tpu as pltpu
from jax.experimental.pallas import tpu_sc as plsc

F32 = jnp.float32
BF16 = jnp.bfloat16
I32 = jnp.int32
HI = lax.Precision.HIGHEST

D_MODEL = 1024
HEAD_DIM = 64
MOBA_HEADS = 6
RWKV_HEADS = 6
MEM_HEADS = 4
MOBA_W = MOBA_HEADS * HEAD_DIM
RWKV_W = RWKV_HEADS * HEAD_DIM
MEM_W = MEM_HEADS * HEAD_DIM
MOBA_BLOCK = 256
MOBA_TOPK = 3
N_BUCKETS = 32
MAX_DISTANCE = 128
DECAY_LORA = 64
AAA_LORA = 64
GATE_LORA = 128
RWKV_COLS = 3 * RWKV_W + DECAY_LORA + AAA_LORA + GATE_LORA
RWKV_GN_EPS = 64e-5
IN_COLS = 3 * MOBA_W + RWKV_COLS + MEM_W
N_EXPERTS = 32
TOP_K = 4
D_EXPERT = D_MODEL
SWIGLU_ALPHA = 1.702
SWIGLU_LIMIT = 7.0
RMS_EPS = 1e-5

LANES = 128
RWKV_CHUNK = 128
RWKV_SEQS_PER_STEP = 4
EXPERT_ROWS = 256
ROUTE_TOKENS = 512
SC_CORES = 2
SC_SUBCORES = 16
SC_ROWS = 64
SC_INFLIGHT = 2
COMBINE_PARTS = 2
VMEM_LIMIT = 56 * 1024 * 1024
NEG_INF = float("-inf")
HIGH_HALF = -65536
LOG2_E = 1.4426950408889634
MOBA_VROWS = HEAD_DIM + 16


def _cparams(sem):
    return pltpu.CompilerParams(dimension_semantics=sem, vmem_limit_bytes=VMEM_LIMIT)


def _rms(x, g):
    return x * lax.rsqrt(jnp.mean(x * x, axis=-1, keepdims=True) + RMS_EPS) * g


def _bdot(a, b):
    return jnp.dot(a.astype(BF16), b.astype(BF16), preferred_element_type=F32)


def _hdot(a, b):
    return jnp.dot(a, b, preferred_element_type=F32, precision=HI)


def _dot_nt(a, b):
    return lax.dot_general(a, b, (((1,), (1,)), ((), ())), preferred_element_type=F32)


def _dot_tn(a, b):
    return lax.dot_general(a, b, (((0,), (0,)), ((), ())), preferred_element_type=F32)


def _split_bf16(x, terms):
    parts = []
    for _ in range(terms):
        hi = x.astype(BF16)
        parts.append(hi)
        x = x - hi.astype(F32)
    return parts


def _dot_exact_rhs(x, m_bf16, terms):
    acc = None
    for part in _split_bf16(x, terms):
        d = jnp.dot(part, m_bf16, preferred_element_type=F32)
        acc = d if acc is None else acc + d
    return acc


def _pack_halves(x):
    n = x.shape[1] // 2
    lo = pltpu.bitcast(x[:, :n].astype(BF16).astype(F32), I32)
    hi = pltpu.bitcast(x[:, n:].astype(BF16).astype(F32), I32)
    return (hi & HIGH_HALF) | lax.shift_right_logical(lo, 16)


def _unpack_halves(w):
    lo = pltpu.bitcast(w << 16, F32)
    hi = pltpu.bitcast(w & HIGH_HALF, F32)
    return jnp.concatenate([lo, hi], axis=1)


def _norm_matmul_kernel(x_ref, g_ref, w_ref, o_ref):
    h = _rms(x_ref[...], g_ref[...])
    o_ref[...] = jnp.dot(h.astype(BF16), w_ref[...], preferred_element_type=F32)


def _norm_matmul(x, g, w_bf16, tm, name):
    t, d = x.shape
    n = w_bf16.shape[1]
    tm = min(tm, t)
    return pl.pallas_call(
        _norm_matmul_kernel,
        grid=(t // tm,),
        in_specs=[pl.BlockSpec((tm, d), lambda i: (i, 0)),
                  pl.BlockSpec((1, d), lambda i: (0, 0)),
                  pl.BlockSpec((d, n), lambda i: (0, 0))],
        out_specs=pl.BlockSpec((tm, n), lambda i: (i, 0)),
        out_shape=jax.ShapeDtypeStruct((t, n), F32),
        compiler_params=_cparams(("parallel",)),
        name=name,
    )(x, g.reshape(1, d), w_bf16)


def _t5_bucket_np(dist):
    n = np.maximum(dist, 0)
    max_exact = N_BUCKETS // 2
    nf = np.maximum(n, 1).astype(np.float64)
    large = max_exact + (np.log(nf / max_exact) / math.log(MAX_DISTANCE / max_exact)
                         * (N_BUCKETS - max_exact)).astype(np.int64)
    large = np.minimum(large, N_BUCKETS - 1)
    return np.where(n < max_exact, n, large)


def _moba_bias_tables(rel_bias):
    n = MOBA_BLOCK
    assert np.all(_t5_bucket_np(np.arange(n + 1, 64 * n)) == N_BUCKETS - 1)
    bias_t = rel_bias.astype(F32).T * LOG2_E

    def by_distance(dist):
        bucket = jnp.asarray(_t5_bucket_np(dist), I32)[None]
        tab = jnp.zeros((MOBA_HEADS, dist.shape[0]), F32)
        for b in range(N_BUCKETS):
            tab = jnp.where(bucket == b, bias_t[:, b][:, None], tab)
        return tab

    d = np.arange(2 * n)
    g_own = jnp.where(jnp.asarray(d < n)[None], by_distance(np.where(d < n, d, 0)), NEG_INF)
    g_prev = by_distance(np.where(d < n, n + d, d - n))
    far = bias_t[:, N_BUCKETS - 1]
    return g_own[:, None, :], g_prev[:, None, :], far


def _moba_kernel(far_ref, q_ref, k_ref, v_ref, g0_ref, g1_ref, o_ref,
                 qt_s, kb_s, vt_s, sc_s, t0_ref, t1_ref, *, nb):
    hp = pl.program_id(1)
    blk = MOBA_BLOCK
    scale = HEAD_DIM ** -0.5 * LOG2_E
    row_head = lax.broadcasted_iota(I32, (LANES, 1), 0) // HEAD_DIM
    lane_head = lax.broadcasted_iota(I32, (1, LANES), 1) // HEAD_DIM
    ones_rows = jnp.ones((MOBA_VROWS - HEAD_DIM, blk), BF16)

    for h in range(2):
        for g_ref, t_ref in ((g0_ref, t0_ref), (g1_ref, t1_ref)):
            rows = jnp.broadcast_to(g_ref[h], (blk, 2 * blk))
            t_ref[h] = pltpu.roll(rows, 0, 1, stride=1, stride_axis=0)[:, :blk]

    kmean_rows = []
    for j in range(nb):
        sl = pl.ds(j * blk, blk)
        qt_s[j] = (q_ref[0, sl, :] * scale).T
        vt = v_ref[0, sl, :].T.astype(BF16)
        for h in range(2):
            vt_s[j, h * MOBA_VROWS:h * MOBA_VROWS + HEAD_DIM, :] = vt[h * HEAD_DIM:(h + 1) * HEAD_DIM, :]
            vt_s[j, h * MOBA_VROWS + HEAD_DIM:(h + 1) * MOBA_VROWS, :] = ones_rows
        kj = k_ref[0, sl, :]
        kb_s[j] = kj.astype(BF16)
        kmean_rows.append(jnp.mean(kj, axis=0, keepdims=True))
    kmean = jnp.concatenate(kmean_rows, axis=0)
    km2 = jnp.concatenate([jnp.where(lane_head == 0, kmean, 0.0),
                           jnp.where(lane_head == 1, kmean, 0.0)], axis=0)
    blk_iota = lax.broadcasted_iota(I32, (nb, blk), 0)

    def score_pass(i):
        qt = qt_s[i]
        negs = [None, None]
        if i > MOBA_TOPK:
            gate2 = _hdot(km2, qt)
            for h in range(2):
                g = gate2[h * nb:(h + 1) * nb, :]
                cnt = jnp.zeros((nb, blk), I32)
                for m in range(i):
                    gm = g[m:m + 1, :]
                    beats = (gm > g) | ((gm == g) & (m < blk_iota))
                    cnt = cnt + jnp.where(beats, 1, 0)
                negs[h] = jnp.where(cnt < MOBA_TOPK, 0.0, NEG_INF)
        col_max = []
        for h in range(2):
            qt_h = jnp.where(row_head == h, qt, 0.0).astype(BF16)
            far_bias = far_ref[2 * hp + h]
            m_run = None
            for j in range(i + 1):
                s = jnp.dot(kb_s[j], qt_h, preferred_element_type=F32)
                if j == i:
                    s = s + t0_ref[h]
                else:
                    bias = t1_ref[h] if j == i - 1 else far_bias
                    if negs[h] is not None:
                        bias = bias + negs[h][j:j + 1, :]
                    s = s + bias
                sc_s[i % 2, h, j] = s
                cm = jnp.max(s, axis=0, keepdims=True)
                m_run = cm if m_run is None else jnp.maximum(m_run, cm)
            col_max.append(m_run)
        return col_max

    def prob_pass(i, col_max):
        outs = []
        for h in range(2):
            m_run = col_max[h]
            probs = jnp.concatenate([jnp.exp2(sc_s[i % 2, h, j] - m_run).astype(BF16)
                                     for j in range(i + 1)], axis=0)
            vals = jnp.concatenate([vt_s[j, h * MOBA_VROWS:(h + 1) * MOBA_VROWS, :]
                                    for j in range(i + 1)], axis=1)
            acc = jnp.dot(vals, probs, preferred_element_type=F32)
            outs.append(acc[:HEAD_DIM] / acc[HEAD_DIM:HEAD_DIM + 1])
        o_t = jnp.concatenate(outs, axis=0)
        o_ref[0, pl.ds(i * blk, blk), :] = o_t.T

    pending = None
    for i in range(nb):
        col_max = score_pass(i)
        if pending is not None:
            prob_pass(*pending)
        pending = (i, col_max)
    prob_pass(*pending)


def _moba_attention(p3, g_own, g_prev, far):
    b, s, _ = p3.shape
    nb = s // MOBA_BLOCK
    n_pairs = MOBA_HEADS // 2
    blk_spec = lambda off: pl.BlockSpec((1, s, LANES), lambda bi, hp: (bi, 0, off + hp))
    tab_spec = pl.BlockSpec((2, 1, 2 * MOBA_BLOCK), lambda bi, hp: (hp, 0, 0))
    return pl.pallas_call(
        functools.partial(_moba_kernel, nb=nb),
        grid=(b, n_pairs),
        in_specs=[pl.BlockSpec(memory_space=pltpu.SMEM),
                  blk_spec(0), blk_spec(n_pairs), blk_spec(2 * n_pairs), tab_spec, tab_spec],
        out_specs=pl.BlockSpec((1, s, LANES), lambda bi, hp: (bi, 0, hp)),
        out_shape=jax.ShapeDtypeStruct((b, s, MOBA_W), F32),
        scratch_shapes=[pltpu.VMEM((nb, LANES, MOBA_BLOCK), F32),
                        pltpu.VMEM((nb, MOBA_BLOCK, LANES), BF16),
                        pltpu.VMEM((nb, 2 * MOBA_VROWS, MOBA_BLOCK), BF16),
                        pltpu.VMEM((2, 2, nb, MOBA_BLOCK, MOBA_BLOCK), F32),
                        pltpu.VMEM((2, MOBA_BLOCK, MOBA_BLOCK), F32),
                        pltpu.VMEM((2, MOBA_BLOCK, MOBA_BLOCK), F32)],
        compiler_params=_cparams(("parallel", "parallel")),
        name="moba",
    )(far, p3, p3, p3, g_own, g_prev)


def _softplus(z):
    return jnp.maximum(z, 0.0) + jnp.log(1.0 + jnp.exp(-jnp.abs(z)))


def _sigmoid(z):
    return 1.0 / (1.0 + jnp.exp(-z))


def _rwkv_kernel(r_ref, k_ref, v_ref, wa_ref, g_ref,
                 mu_r_ref, mu_k_ref, mu_v_ref, mu_wa_ref, mu_g_ref,
                 w0_ref, wup_ref, a0_ref, aup_ref, gup_ref, kk_ref, ka_ref, rk_ref,
                 gng_ref, gnb_ref, hsum_ref, tri_ref,
                 o_ref, st_s, prev_r, prev_k, prev_v, prev_wa, prev_g, *, n_seq):
    c = RWKV_CHUNK
    mid = c // 2

    @pl.when(pl.program_id(1) == 0)
    def _():
        st_s[...] = jnp.zeros_like(st_s)
        prev_r[...] = jnp.zeros_like(prev_r)
        prev_k[...] = jnp.zeros_like(prev_k)
        prev_v[...] = jnp.zeros_like(prev_v)
        prev_wa[...] = jnp.zeros_like(prev_wa)
        prev_g[...] = jnp.zeros_like(prev_g)

    n_pairs = RWKV_HEADS // 2
    hsum = hsum_ref[...]
    tri = tri_ref[...]
    row = lax.broadcasted_iota(I32, (c, c), 0)
    col = lax.broadcasted_iota(I32, (c, c), 1)
    strict = row > col
    incl = row >= col
    lane_half = lax.broadcasted_iota(I32, (1, LANES), 1) // HEAD_DIM
    hmask = hsum.astype(F32)
    dot = functools.partial(jnp.dot, preferred_element_type=F32)
    lanes = lambda x, jp: x[:, jp * LANES:(jp + 1) * LANES]

    def group_sum(x):
        return jnp.concatenate([_dot_exact_rhs(lanes(x, jp), hsum, 2) for jp in range(n_pairs)], axis=1)

    def two_heads(x):
        return jnp.concatenate([jnp.where(lane_half == 0, x, 0.0), jnp.where(lane_half == 1, x, 0.0)],
                               axis=0).astype(BF16)

    def side(a, b):
        return jnp.concatenate([a, b], axis=1)

    def mix(g, x_ref, prev_ref, mu_ref):
        x = x_ref[g]
        rows = lax.broadcasted_iota(I32, x.shape, 0)
        shifted = jnp.where(rows == 0, prev_ref[g], pltpu.roll(x, 1, axis=0))
        prev_ref[g] = x[c - 1:c, :]
        return x + (shifted - x) * mu_ref[...]

    def prepare(g):
        r = mix(g, r_ref, prev_r, mu_r_ref)
        k = mix(g, k_ref, prev_k, mu_k_ref)
        v = mix(g, v_ref, prev_v, mu_v_ref)
        xwa = mix(g, wa_ref, prev_wa, mu_wa_ref)
        xg = mix(g, g_ref, prev_g, mu_g_ref)

        w = -_softplus(-(w0_ref[...] + _bdot(jnp.tanh(xwa), wup_ref[...]))) - 0.5
        logd = -jnp.exp(w)
        a = _sigmoid(a0_ref[...] + _bdot(xwa, aup_ref[...]))
        gate = _bdot(_sigmoid(xg), gup_ref[...])
        kk = k * kk_ref[...]
        kk = kk / jnp.maximum(jnp.sqrt(group_sum(kk * kk)), 1e-12)
        k2 = k * (1.0 + (a - 1.0) * ka_ref[...])

        parts = _split_bf16(logd, 3)
        cum = dot(tri, parts[0]) + dot(tri, parts[1]) + dot(tri, parts[2])
        cum_last = cum[c - 1:c, :]
        ref = cum[mid - 1:mid, :]
        rel = cum - ref
        g_in = jnp.exp(rel)
        g_inv = jnp.exp(-rel)
        g_tail = jnp.exp(cum_last - cum)
        a_t = -kk * jnp.exp(rel - logd)
        b_t = kk * a * g_inv
        k_t = k2 * g_inv
        r_t = r * g_in
        b_hat = kk * a * g_tail
        k_hat = k2 * g_tail

        ref_scale = jnp.exp(ref)
        decay = jnp.exp(cum_last)
        bk = jnp.concatenate([b_t, k_t], axis=0).astype(BF16)
        ar_all = jnp.concatenate([a_t, r_t], axis=0)
        per_pair = []
        for jp in range(n_pairs):
            st = st_s[g, jp]
            st_ref = (st * lanes(ref_scale, jp)).astype(BF16)
            both = _dot_nt(lanes(ar_all, jp).astype(BF16), st_ref)
            quads = []
            for half in range(2):
                ar = jnp.where(lane_half == half, lanes(ar_all, jp), 0.0)
                m4 = _dot_nt(ar.astype(BF16), lanes(bk, jp))
                quads.append((jnp.where(strict, m4[:c, :c], 0.0),
                              jnp.where(strict, m4[:c, c:], 0.0).astype(BF16),
                              jnp.where(incl, m4[c:, :c], 0.0).astype(BF16),
                              jnp.where(incl, m4[c:, c:], 0.0).astype(BF16)))
            per_pair.append(dict(
                st=st, p0=both[:c], o0=both[c:], v_rows=two_heads(lanes(v, jp)),
                l_ab=[q[0] for q in quads], l_ak=side(quads[0][1], quads[1][1]),
                m_rb=side(quads[0][2], quads[1][2]), m_rk=side(quads[0][3], quads[1][3]),
                v=lanes(v, jp).astype(BF16), b_hat=lanes(b_hat, jp).astype(BF16),
                k_hat=lanes(k_hat, jp).astype(BF16), decay=lanes(decay, jp)))
        return dict(r=r, k2=k2, v=v, gate=gate, pairs=per_pair)

    seqs = [prepare(g) for g in range(n_seq)]
    heads = [(g, jp, half) for g in range(n_seq) for jp in range(n_pairs) for half in range(2)]

    pair_ids = [(g, jp) for g in range(n_seq) for jp in range(n_pairs)]
    pw = {gh: seqs[gh[0]]["pairs"][gh[1]]["l_ab"][gh[2]].astype(BF16) for gh in heads}
    u_loc = {(g, jp): seqs[g]["pairs"][jp]["p0"]
             + dot(seqs[g]["pairs"][jp]["l_ak"], seqs[g]["pairs"][jp]["v_rows"]) for g, jp in pair_ids}
    span = 1
    while span < c:
        u_loc = {(g, jp): u_loc[(g, jp)] + dot(side(pw[(g, jp, 0)], pw[(g, jp, 1)]), two_heads(u_loc[(g, jp)]))
                 for g, jp in pair_ids}
        span *= 2
        if span < c:
            pw = {gh: dot(pw[gh], pw[gh]).astype(BF16) for gh in heads}

    for g in range(n_seq):
        sq = seqs[g]
        outs = []
        for jp in range(n_pairs):
            pr = sq["pairs"][jp]
            u = u_loc[(g, jp)]
            outs.append(pr["o0"] + dot(pr["m_rb"], two_heads(u)) + dot(pr["m_rk"], pr["v_rows"]))
            upd = _dot_tn(u.astype(BF16), pr["b_hat"]) + _dot_tn(pr["v"], pr["k_hat"])
            st_s[g, jp] = pr["st"] * pr["decay"] + hmask * upd
        o = jnp.concatenate(outs, axis=1)

        inv_n = 1.0 / HEAD_DIM
        mean = group_sum(o) * inv_n
        dev = o - mean
        var = group_sum(dev * dev) * inv_n
        y = dev * lax.rsqrt(var + RWKV_GN_EPS) * gng_ref[...] + gnb_ref[...]
        y = y + group_sum(sq["r"] * sq["k2"] * rk_ref[...]) * sq["v"]
        o_ref[g] = y * sq["gate"]


def _rwkv(p3, mu, w0, w_up, a0, a_up, g_up, k_k, k_a, r_k, gn_g, gn_b):
    b, s, _ = p3.shape
    c = RWKV_CHUNK
    w = RWKV_W
    n_seq = RWKV_SEQS_PER_STEP if b % RWKV_SEQS_PER_STEP == 0 else 1
    base = 3 * MOBA_W
    assert base % w == 0 and (base + 3 * w) % LANES == 0
    lora = DECAY_LORA + AAA_LORA

    def col_spec(width, off_cols):
        assert off_cols % width == 0
        return pl.BlockSpec((n_seq, c, width), lambda bi, ci: (bi, ci, off_cols // width))

    row = lambda x: x.reshape(1, -1).astype(F32)
    const = lambda shape: pl.BlockSpec(shape, lambda bi, ci: (0,) * len(shape))
    wup_pad = jnp.concatenate([w_up, jnp.zeros((AAA_LORA, w), F32)], axis=0).astype(BF16)
    aup_pad = jnp.concatenate([jnp.zeros((DECAY_LORA, w), F32), a_up], axis=0).astype(BF16)
    head = np.arange(LANES) // HEAD_DIM
    hsum = jnp.asarray((head[:, None] == head[None, :]).astype(np.float32)).astype(BF16)
    tri = jnp.asarray(np.tril(np.ones((c, c), np.float32))).astype(BF16)
    vec_args = [mu[:w], mu[w:2 * w], mu[2 * w:3 * w], mu[3 * w:3 * w + lora], mu[3 * w + lora:],
                w0, None, a0, None, None, k_k, k_a, r_k.reshape(-1), gn_g, gn_b]
    args = [p3, p3, p3, p3, p3]
    specs = [col_spec(w, base), col_spec(w, base + w), col_spec(w, base + 2 * w),
             col_spec(lora, base + 3 * w), col_spec(GATE_LORA, base + 3 * w + lora)]
    mats = {6: wup_pad, 8: aup_pad, 9: g_up.astype(BF16)}
    for idx, a in enumerate(vec_args):
        arr = mats[idx] if a is None else row(a)
        args.append(arr)
        specs.append(const(arr.shape))
    for arr in (hsum, tri):
        args.append(arr)
        specs.append(const(arr.shape))
    return pl.pallas_call(
        functools.partial(_rwkv_kernel, n_seq=n_seq),
        grid=(b // n_seq, s // c),
        in_specs=specs,
        out_specs=pl.BlockSpec((n_seq, c, w), lambda bi, ci: (bi, ci, 0)),
        out_shape=jax.ShapeDtypeStruct((b, s, w), F32),
        scratch_shapes=[pltpu.VMEM((n_seq, RWKV_HEADS // 2, LANES, LANES), F32),
                        pltpu.VMEM((n_seq, 1, w), F32),
                        pltpu.VMEM((n_seq, 1, w), F32), pltpu.VMEM((n_seq, 1, w), F32),
                        pltpu.VMEM((n_seq, 1, lora), F32), pltpu.VMEM((n_seq, 1, GATE_LORA), F32)],
        compiler_params=_cparams(("parallel", "arbitrary")),
        name="rwkv",
    )(*args)


def _mem_attn_kernel(q_ref, kv_ref, o_ref):
    scale = HEAD_DIM ** -0.5
    lane_head = lax.broadcasted_iota(I32, (1, MEM_W), 1) // HEAD_DIM
    q = q_ref[0] * scale
    mk = kv_ref[0, :, :MEM_W].astype(BF16)
    mv = kv_ref[0, :, MEM_W:].astype(BF16)
    out = jnp.zeros(q.shape, F32)
    for h in range(MEM_HEADS):
        hm = lane_head == h
        s = _dot_nt(jnp.where(hm, q, 0.0).astype(BF16), mk)
        s = s - jnp.max(s, axis=-1, keepdims=True)
        e = jnp.exp(s)
        p = e / jnp.sum(e, axis=-1, keepdims=True)
        out = jnp.where(hm, jnp.dot(p.astype(BF16), mv, preferred_element_type=F32), out)
    o_ref[0] = out


def _mem_attention(p3, mkv3, tq):
    b, s, _ = p3.shape
    m = mkv3.shape[1]
    off = (3 * MOBA_W + RWKV_COLS) // MEM_W
    assert off * MEM_W == 3 * MOBA_W + RWKV_COLS
    return pl.pallas_call(
        _mem_attn_kernel,
        grid=(b, s // tq),
        in_specs=[pl.BlockSpec((1, tq, MEM_W), lambda bi, i: (bi, i, off)),
                  pl.BlockSpec((1, m, 2 * MEM_W), lambda bi, i: (bi, 0, 0))],
        out_specs=pl.BlockSpec((1, tq, MEM_W), lambda bi, i: (bi, i, 0)),
        out_shape=jax.ShapeDtypeStruct((b, s, MEM_W), F32),
        compiler_params=_cparams(("parallel", "parallel")),
        name="mem_attn",
    )(p3, mkv3)


def _out_router_kernel(x_ref, ym_ref, yr_ref, ye_ref, wo1_ref, wo2_ref, wo3_ref, g_ref,
                       wrh_ref, wrl_ref, br_ref, upper_ref, ones_ref,
                       x1_ref, h_ref, idx_ref, rank_ref, wgt_ref, cnt_ref, run_s):
    @pl.when(pl.program_id(0) == 0)
    def _():
        run_s[...] = jnp.zeros_like(run_s)

    x1 = (x_ref[...] + _bdot(ym_ref[...], wo1_ref[...]) + _bdot(yr_ref[...], wo2_ref[...])
          + _bdot(ye_ref[...], wo3_ref[...]))
    x1_ref[...] = x1
    h = _rms(x1, g_ref[...])
    h_ref[...] = _pack_halves(h)
    tm = h.shape[0]
    h_hi, h_lo = _split_bf16(h, 2)
    dot = functools.partial(jnp.dot, preferred_element_type=F32)
    logits = dot(h_hi, wrh_ref[...]) + (dot(h_hi, wrl_ref[...]) + dot(h_lo, wrh_ref[...])) + br_ref[...]
    lg = logits.T[:N_EXPERTS, :]
    e_iota = lax.broadcasted_iota(I32, (N_EXPERTS, tm), 0)

    vals, idxs = [], []
    for _ in range(TOP_K):
        m = jnp.max(lg, axis=0, keepdims=True)
        idx = jnp.min(jnp.where(lg == m, e_iota, N_EXPERTS), axis=0, keepdims=True)
        vals.append(m)
        idxs.append(idx)
        lg = jnp.where(e_iota == idx, NEG_INF, lg)
    exps = [jnp.exp(vk - vals[0]) for vk in vals]
    denom = exps[0] + exps[1] + exps[2] + exps[3]

    chosen = jnp.zeros((N_EXPERTS, tm), F32)
    for idx in idxs:
        chosen = chosen + jnp.where(e_iota == idx, 1.0, 0.0)
    chosen = chosen.astype(BF16)
    run = run_s[...]
    before = dot(chosen, upper_ref[...]) + jnp.concatenate([run] * (tm // LANES), axis=1)
    run_s[...] = run + dot(chosen, ones_ref[...])
    cnt_ref[...] = run_s[...]

    zero_i = jnp.zeros((8 - TOP_K, tm), I32)
    ranks = [jnp.sum(jnp.where(e_iota == idx, before, 0.0), axis=0, keepdims=True).astype(I32)
             for idx in idxs]
    idx_ref[0] = jnp.concatenate(idxs + [zero_i], axis=0)
    rank_ref[0] = jnp.concatenate(ranks + [zero_i], axis=0)
    wrows = jnp.concatenate([e / denom for e in exps] + [jnp.zeros((LANES - TOP_K, tm), F32)], axis=0)
    wgt_ref[...] = wrows.T


def _out_router(x2, ym, yr, ye, w_out, g_ffn, w_router, b_router, tm):
    t, d = x2.shape
    wo = w_out.astype(BF16)
    wo1, wo2, wo3 = wo[:MOBA_W], wo[MOBA_W:MOBA_W + RWKV_W], wo[MOBA_W + RWKV_W:]
    wr = jnp.zeros((d, LANES), F32).at[:, :N_EXPERTS].set(w_router)
    wr_hi = wr.astype(BF16)
    wr_lo = (wr - wr_hi.astype(F32)).astype(BF16)
    br = jnp.full((1, LANES), NEG_INF, F32).at[0, :N_EXPERTS].set(b_router)
    upper = jnp.asarray(np.triu(np.ones((tm, tm), np.float32), 1)).astype(BF16)
    ones = jnp.ones((tm, LANES), BF16)
    tile = lambda n: pl.BlockSpec((tm, n), lambda i: (i, 0))
    slots = pl.BlockSpec((1, 8, tm), lambda i: (i, 0, 0))
    const = lambda a: pl.BlockSpec(a.shape, lambda i: (0,) * a.ndim)
    g2 = g_ffn.reshape(1, d)
    n = t // tm
    return pl.pallas_call(
        _out_router_kernel,
        grid=(n,),
        in_specs=[tile(d), tile(MOBA_W), tile(RWKV_W), tile(MEM_W), const(wo1), const(wo2), const(wo3),
                  const(g2), const(wr_hi), const(wr_lo), const(br), const(upper), const(ones)],
        out_specs=[tile(d), tile(d // 2), slots, slots, tile(LANES),
                   pl.BlockSpec((N_EXPERTS, LANES), lambda i: (0, 0))],
        out_shape=[jax.ShapeDtypeStruct((t, d), F32), jax.ShapeDtypeStruct((t, d // 2), I32),
                   jax.ShapeDtypeStruct((n, 8, tm), I32), jax.ShapeDtypeStruct((n, 8, tm), I32),
                   jax.ShapeDtypeStruct((t, LANES), F32), jax.ShapeDtypeStruct((N_EXPERTS, LANES), F32)],
        scratch_shapes=[pltpu.VMEM((N_EXPERTS, LANES), F32)],
        compiler_params=_cparams(("arbitrary",)),
        name="out_router",
    )(x2, ym, yr, ye, wo1, wo2, wo3, g2, wr_hi, wr_lo, br, upper, ones)


def _sc_mesh():
    return plsc.VectorSubcoreMesh(core_axis_name="c", subcore_axis_name="s",
                                  num_cores=SC_CORES, num_subcores=SC_SUBCORES)


def _sc_split(n_rows):
    n_workers = SC_CORES * SC_SUBCORES
    per_worker = n_rows // n_workers
    assert per_worker * n_workers == n_rows and per_worker % (SC_ROWS * SC_INFLIGHT) == 0
    assert SC_ROWS % 8 == 0
    return per_worker


def _sc_gather_rows(table, idx):
    n_rows = idx.shape[0]
    width = table.shape[1]
    per_worker = _sc_split(n_rows)

    buf = lambda shape, dtype: [pltpu.VMEM(shape, dtype) for _ in range(SC_INFLIGHT)]

    @functools.partial(
        pl.kernel, mesh=_sc_mesh(),
        out_type=jax.ShapeDtypeStruct((n_rows, width), table.dtype),
        scratch_types=[buf((SC_ROWS,), I32), buf((SC_ROWS, width), table.dtype),
                       [pltpu.SemaphoreType.DMA for _ in range(SC_INFLIGHT)]],
        name="sc_gather",
    )
    def gather(table_hbm, idx_hbm, out_hbm, idx_v, rows_v, sems):
        worker = lax.axis_index("s") * SC_CORES + lax.axis_index("c")
        base = worker * per_worker

        @pl.loop(0, per_worker // (SC_ROWS * SC_INFLIGHT))
        def _(step):
            offs = [pl.multiple_of(base + (step * SC_INFLIGHT + b) * SC_ROWS, 8) for b in range(SC_INFLIGHT)]
            copies = []
            for b in range(SC_INFLIGHT):
                pltpu.sync_copy(idx_hbm.at[pl.ds(offs[b], SC_ROWS)], idx_v[b])
                copies.append(pltpu.async_copy(table_hbm.at[idx_v[b]], rows_v[b], sems[b]))
            for b in range(SC_INFLIGHT):
                copies[b].wait()
                pltpu.sync_copy(rows_v[b], out_hbm.at[pl.ds(offs[b], SC_ROWS)])

    return gather(table, idx)


def _sc_scatter_rows(rows, idx, n_out):
    n_src, width = rows.shape
    n_copies = idx.shape[0] // n_src
    assert n_copies * n_src == idx.shape[0]
    per_worker = _sc_split(n_src)

    @functools.partial(
        pl.kernel, mesh=_sc_mesh(),
        out_type=jax.ShapeDtypeStruct((n_out, width), rows.dtype),
        scratch_types=[[pltpu.VMEM((SC_ROWS,), I32) for _ in range(n_copies)],
                       pltpu.VMEM((SC_ROWS, width), rows.dtype),
                       [pltpu.SemaphoreType.DMA for _ in range(n_copies)]],
        name="sc_scatter",
    )
    def scatter(rows_hbm, idx_hbm, out_hbm, idx_v, rows_v, sems):
        worker = lax.axis_index("s") * SC_CORES + lax.axis_index("c")
        base = worker * per_worker

        @pl.loop(0, per_worker // SC_ROWS)
        def _(step):
            src = pl.multiple_of(base + step * SC_ROWS, 8)
            pltpu.sync_copy(rows_hbm.at[pl.ds(src, SC_ROWS)], rows_v)
            copies = []
            for cpy in range(n_copies):
                off = pl.multiple_of(cpy * n_src + src, 8)
                pltpu.sync_copy(idx_hbm.at[pl.ds(off, SC_ROWS)], idx_v[cpy])
                copies.append(pltpu.async_copy(rows_v, out_hbm.at[idx_v[cpy]], sems[cpy]))
            for cp in copies:
                cp.wait()

    return scatter(rows, idx)


def _experts_kernel(be_ref, bv_ref, nused_ref, next_ref, slot_ref,
                    xs_ref, wgu_hbm, bg_ref, bu_ref, wd_hbm, bd_ref, perm_ref,
                    o_ref, wgu_buf, wd_buf, wg_s, wu_s, wd_s, sem):
    i = pl.program_id(0)
    e = be_ref[i]
    used = i < nused_ref[0]
    changed = ((i == 0) | (e != be_ref[jnp.maximum(i - 1, 0)])) & used
    slot = slot_ref[e]

    def weight_copies(expert, buf_slot):
        return (pltpu.make_async_copy(wgu_hbm.at[expert], wgu_buf.at[buf_slot], sem.at[buf_slot, 0]),
                pltpu.make_async_copy(wd_hbm.at[expert], wd_buf.at[buf_slot], sem.at[buf_slot, 1]))

    @pl.when((i == 0) & used)
    def _():
        for cp in weight_copies(e, slot):
            cp.start()

    @pl.when(changed)
    def _():
        for cp in weight_copies(e, slot):
            cp.wait()
        nxt = next_ref[e]

        @pl.when(nxt >= 0)
        def _():
            for cp in weight_copies(nxt, 1 - slot):
                cp.start()

        half = LANES
        for cblk in range(2 * D_EXPERT // (2 * half)):
            wt = wgu_buf[slot, :, cblk * 2 * half:(cblk + 1) * 2 * half].astype(BF16)
            sep = jnp.dot(wt, perm_ref[...], preferred_element_type=F32).astype(BF16)
            wg_s[:, cblk * half:(cblk + 1) * half] = sep[:, :half]
            wu_s[:, cblk * half:(cblk + 1) * half] = sep[:, half:]
        wd_s[...] = wd_buf[slot].astype(BF16)

    @pl.when(used)
    def _():
        rows = lax.broadcasted_iota(I32, xs_ref.shape, 0)
        xb = _unpack_halves(jnp.where(rows < bv_ref[i], xs_ref[...], 0)).astype(BF16)
        gate = jnp.dot(xb, wg_s[...], preferred_element_type=F32) + bg_ref[0]
        up = jnp.dot(xb, wu_s[...], preferred_element_type=F32) + bu_ref[0]
        gate = jnp.minimum(gate, SWIGLU_LIMIT)
        up = jnp.clip(up, -SWIGLU_LIMIT, SWIGLU_LIMIT)
        glu = gate * _sigmoid(gate * SWIGLU_ALPHA)
        act = ((up + 1.0) * glu).astype(BF16)
        o_ref[...] = _pack_halves(jnp.dot(act, wd_s[...], preferred_element_type=F32) + bd_ref[0])

    @pl.when(jnp.logical_not(used))
    def _():
        o_ref[...] = jnp.zeros_like(o_ref)


def _experts(blk_expert, blk_valid, n_used, next_expert, buf_slot, xs, w_gate_up, b_gate_up, w_down,
             b_down):
    p_rows = xs.shape[0]
    d = 2 * xs.shape[1]
    n_blocks = p_rows // EXPERT_ROWS
    bg = b_gate_up[:, 0::2].reshape(N_EXPERTS, 1, D_EXPERT)
    bu = b_gate_up[:, 1::2].reshape(N_EXPERTS, 1, D_EXPERT)
    bd = b_down.reshape(N_EXPERTS, 1, d)
    perm_np = np.zeros((2 * LANES, 2 * LANES), np.float32)
    perm_np[2 * np.arange(LANES), np.arange(LANES)] = 1.0
    perm_np[2 * np.arange(LANES) + 1, LANES + np.arange(LANES)] = 1.0
    perm = jnp.asarray(perm_np).astype(BF16)
    by_expert = lambda shape: pl.BlockSpec((1,) + shape, lambda i, be, *_: (be[i], 0, 0))
    grid_spec = pltpu.PrefetchScalarGridSpec(
        num_scalar_prefetch=5,
        grid=(n_blocks,),
        in_specs=[pl.BlockSpec((EXPERT_ROWS, d // 2), lambda i, *_: (i, 0)),
                  pl.BlockSpec(memory_space=pl.ANY), by_expert((1, D_EXPERT)), by_expert((1, D_EXPERT)),
                  pl.BlockSpec(memory_space=pl.ANY), by_expert((1, d)),
                  pl.BlockSpec(perm.shape, lambda i, *_: (0, 0))],
        out_specs=pl.BlockSpec((EXPERT_ROWS, d // 2), lambda i, *_: (i, 0)),
        scratch_shapes=[pltpu.VMEM((2, d, 2 * D_EXPERT), F32), pltpu.VMEM((2, D_EXPERT, d), F32),
                        pltpu.VMEM((d, D_EXPERT), BF16), pltpu.VMEM((d, D_EXPERT), BF16),
                        pltpu.VMEM((D_EXPERT, d), BF16), pltpu.SemaphoreType.DMA((2, 2))],
    )
    return pl.pallas_call(
        _experts_kernel,
        grid_spec=grid_spec,
        out_shape=jax.ShapeDtypeStruct((p_rows, d // 2), I32),
        compiler_params=_cparams(("arbitrary",)),
        name="experts",
    )(blk_expert, blk_valid, n_used, next_expert, buf_slot, xs, w_gate_up, bg, bu, w_down, bd, perm)


def _combine_kernel(yg_ref, x1_ref, w_ref, g_ref, *rest):
    o_ref = rest[-1]
    acc = x1_ref[...]
    wts = w_ref[...]
    for kk in range(TOP_K):
        acc = acc + _unpack_halves(yg_ref[kk]) * wts[:, kk:kk + 1]
    o_ref[...] = _rms(acc, g_ref[...])


def _combine(yg, x1, wts, g_final, tb, first_tile, partial_out):
    t, d = x1.shape
    half = yg.shape[2]
    tile = lambda n: pl.BlockSpec((tb, n), lambda i: (i + first_tile, 0))
    in_specs = [pl.BlockSpec((TOP_K, tb, half), lambda i: (0, i, 0)), tile(d), tile(LANES),
                pl.BlockSpec((1, d), lambda i: (0, 0))]
    args = [yg, x1, wts, g_final.reshape(1, d)]
    aliases = {}
    if partial_out is not None:
        in_specs.append(pl.BlockSpec(memory_space=pl.ANY))
        args.append(partial_out)
        aliases = {len(args) - 1: 0}
    return pl.pallas_call(
        _combine_kernel,
        grid=(yg.shape[1] // tb,),
        in_specs=in_specs,
        out_specs=tile(d),
        out_shape=jax.ShapeDtypeStruct((t, d), F32),
        input_output_aliases=aliases,
        compiler_params=_cparams(("parallel",)),
        name="combine",
    )(*args)


def _layer(x, mem, w_in, w_out, w_mem_kv, g_mix, g_mem, g_ffn, bias_tables, mu, w0, w_up, a0, a_up,
           g_up, k_k, k_a, r_k, gn_g, gn_b, w_router, b_router, w_gate_up, b_gate_up, w_down, b_down,
           g_last):
    b, s, d = x.shape
    m = mem.shape[1]
    t = b * s
    x2 = x.reshape(t, d)

    p = _norm_matmul(x2, g_mix, w_in.astype(BF16), 512, "in_proj")
    p3 = p.reshape(b, s, IN_COLS)
    y_moba = _moba_attention(p3, *bias_tables)
    y_rwkv = _rwkv(p3, mu, w0, w_up, a0, a_up, g_up, k_k, k_a, r_k, gn_g, gn_b)
    mkv = _norm_matmul(mem.reshape(b * m, d), g_mem, w_mem_kv.astype(BF16), 512, "mem_kv")
    y_mem = _mem_attention(p3, mkv.reshape(b, m, 2 * MEM_W), 512)

    tb = ROUTE_TOKENS
    x1, h2, idx_o, rank_o, wgt_p, cnt = _out_router(
        x2, y_moba.reshape(t, MOBA_W), y_rwkv.reshape(t, RWKV_W), y_mem.reshape(t, MEM_W),
        w_out, g_ffn, w_router, b_router, tb)

    counts = cnt[:, 0].astype(I32)
    padded = (counts + EXPERT_ROWS - 1) // EXPERT_ROWS * EXPERT_ROWS
    pad_ends = jnp.cumsum(padded)
    pad_starts = (pad_ends - padded).astype(I32)
    n_blocks = (t * TOP_K) // EXPERT_ROWS + N_EXPERTS
    blk_start = jnp.arange(n_blocks, dtype=I32) * EXPERT_ROWS
    blk_expert = jnp.minimum(jnp.sum(blk_start[:, None] >= pad_ends[None, :], axis=1),
                             N_EXPERTS - 1).astype(I32)
    of_block = blk_expert[:, None] == jnp.arange(N_EXPERTS, dtype=I32)[None, :]
    rows_left = jnp.sum(jnp.where(of_block, (counts + pad_starts)[None, :], 0), axis=1) - blk_start
    blk_valid = jnp.clip(rows_left, 0, EXPERT_ROWS)
    n_used = (pad_ends[-1:] // EXPERT_ROWS).astype(I32)
    has_rows = counts > 0
    first_from = lax.cummin(jnp.where(has_rows, jnp.arange(N_EXPERTS, dtype=I32), N_EXPERTS), reverse=True)
    next_expert = jnp.concatenate([first_from[1:], jnp.full((1,), N_EXPERTS, I32)])
    next_expert = jnp.where(next_expert < N_EXPERTS, next_expert, -1).astype(I32)
    buf_slot = ((jnp.cumsum(has_rows.astype(I32)) - has_rows.astype(I32)) % 2).astype(I32)
    idx_kt = jnp.swapaxes(idx_o[:, :TOP_K, :], 0, 1).reshape(TOP_K, t)
    rank_kt = jnp.swapaxes(rank_o[:, :TOP_K, :], 0, 1).reshape(TOP_K, t)
    experts = jnp.arange(N_EXPERTS, dtype=I32)[:, None, None]
    start_kt = jnp.sum(jnp.where(idx_kt[None] == experts, pad_starts[:, None, None], 0), axis=0)
    dest = (start_kt + rank_kt).reshape(TOP_K * t)

    xs = _sc_scatter_rows(h2, dest, n_blocks * EXPERT_ROWS)
    ys = _experts(blk_expert, blk_valid.astype(I32), n_used, next_expert, buf_slot, xs,
                  w_gate_up, b_gate_up, w_down, b_down)
    dest_kt = dest.reshape(TOP_K, t)
    out = None
    for part in range(COMBINE_PARTS):
        t_part = t // COMBINE_PARTS
        rows = dest_kt[:, part * t_part:(part + 1) * t_part].reshape(TOP_K * t_part)
        yg = _sc_gather_rows(ys, rows).reshape(TOP_K, t_part, d // 2)
        out = _combine(yg, x1, wgt_p, g_last, tb, part * (t_part // tb), out)
    return out.reshape(b, s, d)


def kernel(x, mem, w_in, w_out, w_mem_kv, g_mix, g_mem, g_ffn, g_final, rel_bias, rwkv_mu, rwkv_w0,
           rwkv_w_up, rwkv_a0, rwkv_a_up, rwkv_g_up, rwkv_k_k, rwkv_k_a, rwkv_r_k, rwkv_gn_g, rwkv_gn_b,
           w_router, b_router, w_gate_up, b_gate_up, w_down, b_down):
    depth = w_in.shape[0]
    assert depth == 1, "the final norm is fused into the last layer's combine kernel"
    bias_tables = _moba_bias_tables(rel_bias)
    l = 0
    return _layer(x, mem, w_in[l], w_out[l], w_mem_kv[l], g_mix[l], g_mem[l], g_ffn[l], bias_tables,
                  rwkv_mu[l], rwkv_w0[l], rwkv_w_up[l], rwkv_a0[l], rwkv_a_up[l], rwkv_g_up[l],
                  rwkv_k_k[l], rwkv_k_a[l], rwkv_r_k[l], rwkv_gn_g[l], rwkv_gn_b[l], w_router[l],
                  b_router[l], w_gate_up[l], b_gate_up[l], w_down[l], b_down[l], g_final)
```

```python
import functools
import math

import numpy as np
import jax
import jax.numpy as jnp
from jax import lax
from jax.experimental import pallas as pl
from jax.experimental.pallas import tpu as pltpu
from jax.experimental.pallas import tpu_sc as plsc

F32 = jnp.float32
BF16 = jnp.bfloat16
I32 = jnp.int32
HI = lax.Precision.HIGHEST

D_MODEL = 1024
HEAD_DIM = 64
MOBA_HEADS = 6
RWKV_HEADS = 6
MEM_HEADS = 4
MOBA_W = MOBA_HEADS * HEAD_DIM
RWKV_W = RWKV_HEADS * HEAD_DIM
MEM_W = MEM_HEADS * HEAD_DIM
MOBA_BLOCK = 256
MOBA_TOPK = 3
N_BUCKETS = 32
MAX_DISTANCE = 128
DECAY_LORA = 64
AAA_LORA = 64
GATE_LORA = 128
RWKV_COLS = 3 * RWKV_W + DECAY_LORA + AAA_LORA + GATE_LORA
RWKV_GN_EPS = 64e-5
IN_COLS = 3 * MOBA_W + RWKV_COLS + MEM_W
N_EXPERTS = 32
TOP_K = 4
D_EXPERT = D_MODEL
SWIGLU_ALPHA = 1.702
SWIGLU_LIMIT = 7.0
RMS_EPS = 1e-5

LANES = 128
RWKV_CHUNK = 128
RWKV_SEQS_PER_STEP = 4
EXPERT_ROWS = 256
ROUTE_TOKENS = 512
SC_CORES = 2
SC_SUBCORES = 16
SC_ROWS = 64
SC_INFLIGHT = 2
COMBINE_PARTS = 2
VMEM_LIMIT = 56 * 1024 * 1024
NEG_INF = float("-inf")
HIGH_HALF = -65536
LOG2_E = 1.4426950408889634
MOBA_VROWS = HEAD_DIM + 16


def _cparams(sem):
    return pltpu.CompilerParams(dimension_semantics=sem, vmem_limit_bytes=VMEM_LIMIT)


def _rms(x, g):
    return x * lax.rsqrt(jnp.mean(x * x, axis=-1, keepdims=True) + RMS_EPS) * g


def _bdot(a, b):
    return jnp.dot(a.astype(BF16), b.astype(BF16), preferred_element_type=F32)


def _hdot(a, b):
    return jnp.dot(a, b, preferred_element_type=F32, precision=HI)


def _dot_nt(a, b):
    return lax.dot_general(a, b, (((1,), (1,)), ((), ())), preferred_element_type=F32)


def _dot_tn(a, b):
    return lax.dot_general(a, b, (((0,), (0,)), ((), ())), preferred_element_type=F32)


def _split_bf16(x, terms):
    parts = []
    for _ in range(terms):
        hi = x.astype(BF16)
        parts.append(hi)
        x = x - hi.astype(F32)
    return parts


def _dot_exact_rhs(x, m_bf16, terms):
    acc = None
    for part in _split_bf16(x, terms):
        d = jnp.dot(part, m_bf16, preferred_element_type=F32)
        acc = d if acc is None else acc + d
    return acc


def _pack_halves(x):
    n = x.shape[1] // 2
    lo = pltpu.bitcast(x[:, :n].astype(BF16).astype(F32), I32)
    hi = pltpu.bitcast(x[:, n:].astype(BF16).astype(F32), I32)
    return (hi & HIGH_HALF) | lax.shift_right_logical(lo, 16)


def _unpack_halves(w):
    lo = pltpu.bitcast(w << 16, F32)
    hi = pltpu.bitcast(w & HIGH_HALF, F32)
    return jnp.concatenate([lo, hi], axis=1)


def _norm_matmul_kernel(x_ref, g_ref, w_ref, o_ref):
    h = _rms(x_ref[...], g_ref[...])
    o_ref[...] = jnp.dot(h.astype(BF16), w_ref[...], preferred_element_type=F32)


def _norm_matmul(x, g, w_bf16, tm, name):
    t, d = x.shape
    n = w_bf16.shape[1]
    tm = min(tm, t)
    return pl.pallas_call(
        _norm_matmul_kernel,
        grid=(t // tm,),
        in_specs=[pl.BlockSpec((tm, d), lambda i: (i, 0)),
                  pl.BlockSpec((1, d), lambda i: (0, 0)),
                  pl.BlockSpec((d, n), lambda i: (0, 0))],
        out_specs=pl.BlockSpec((tm, n), lambda i: (i, 0)),
        out_shape=jax.ShapeDtypeStruct((t, n), F32),
        compiler_params=_cparams(("parallel",)),
        name=name,
    )(x, g.reshape(1, d), w_bf16)


def _t5_bucket_np(dist):
    n = np.maximum(dist, 0)
    max_exact = N_BUCKETS // 2
    nf = np.maximum(n, 1).astype(np.float64)
    large = max_exact + (np.log(nf / max_exact) / math.log(MAX_DISTANCE / max_exact)
                         * (N_BUCKETS - max_exact)).astype(np.int64)
    large = np.minimum(large, N_BUCKETS - 1)
    return np.where(n < max_exact, n, large)


def _moba_bias_tables(rel_bias):
    n = MOBA_BLOCK
    assert np.all(_t5_bucket_np(np.arange(n + 1, 64 * n)) == N_BUCKETS - 1)
    bias_t = rel_bias.astype(F32).T * LOG2_E

    def by_distance(dist):
        bucket = jnp.asarray(_t5_bucket_np(dist), I32)[None]
        tab = jnp.zeros((MOBA_HEADS, dist.shape[0]), F32)
        for b in range(N_BUCKETS):
            tab = jnp.where(bucket == b, bias_t[:, b][:, None], tab)
        return tab

    d = np.arange(2 * n)
    g_own = jnp.where(jnp.asarray(d < n)[None], by_distance(np.where(d < n, d, 0)), NEG_INF)
    g_prev = by_distance(np.where(d < n, n + d, d - n))
    far = bias_t[:, N_BUCKETS - 1]
    return g_own[:, None, :], g_prev[:, None, :], far


def _moba_kernel(far_ref, q_ref, k_ref, v_ref, g0_ref, g1_ref, o_ref,
                 qt_s, kb_s, vt_s, sc_s, t0_ref, t1_ref, *, nb):
    hp = pl.program_id(1)
    blk = MOBA_BLOCK
    scale = HEAD_DIM ** -0.5 * LOG2_E
    row_head = lax.broadcasted_iota(I32, (LANES, 1), 0) // HEAD_DIM
    lane_head = lax.broadcasted_iota(I32, (1, LANES), 1) // HEAD_DIM
    ones_rows = jnp.ones((MOBA_VROWS - HEAD_DIM, blk), BF16)

    for h in range(2):
        for g_ref, t_ref in ((g0_ref, t0_ref), (g1_ref, t1_ref)):
            rows = jnp.broadcast_to(g_ref[h], (blk, 2 * blk))
            t_ref[h] = pltpu.roll(rows, 0, 1, stride=1, stride_axis=0)[:, :blk]

    kmean_rows = []
    for j in range(nb):
        sl = pl.ds(j * blk, blk)
        qt_s[j] = (q_ref[0, sl, :] * scale).T
        vt = v_ref[0, sl, :].T.astype(BF16)
        for h in range(2):
            vt_s[j, h * MOBA_VROWS:h * MOBA_VROWS + HEAD_DIM, :] = vt[h * HEAD_DIM:(h + 1) * HEAD_DIM, :]
            vt_s[j, h * MOBA_VROWS + HEAD_DIM:(h + 1) * MOBA_VROWS, :] = ones_rows
        kj = k_ref[0, sl, :]
        kb_s[j] = kj.astype(BF16)
        kmean_rows.append(jnp.mean(kj, axis=0, keepdims=True))
    kmean = jnp.concatenate(kmean_rows, axis=0)
    km2 = jnp.concatenate([jnp.where(lane_head == 0, kmean, 0.0),
                           jnp.where(lane_head == 1, kmean, 0.0)], axis=0)
    blk_iota = lax.broadcasted_iota(I32, (nb, blk), 0)

    def score_pass(i):
        qt = qt_s[i]
        negs = [None, None]
        if i > MOBA_TOPK:
            gate2 = _hdot(km2, qt)
            for h in range(2):
                g = gate2[h * nb:(h + 1) * nb, :]
                cnt = jnp.zeros((nb, blk), I32)
                for m in range(i):
                    gm = g[m:m + 1, :]
                    beats = (gm > g) | ((gm == g) & (m < blk_iota))
                    cnt = cnt + jnp.where(beats, 1, 0)
                negs[h] = jnp.where(cnt < MOBA_TOPK, 0.0, NEG_INF)
        col_max = []
        for h in range(2):
            qt_h = jnp.where(row_head == h, qt, 0.0).astype(BF16)
            far_bias = far_ref[2 * hp + h]
            m_run = None
            for j in range(i + 1):
                s = jnp.dot(kb_s[j], qt_h, preferred_element_type=F32)
                if j == i:
                    s = s + t0_ref[h]
                else:
                    bias = t1_ref[h] if j == i - 1 else far_bias
                    if negs[h] is not None:
                        bias = bias + negs[h][j:j + 1, :]
                    s = s + bias
                sc_s[i % 2, h, j] = s
                cm = jnp.max(s, axis=0, keepdims=True)
                m_run = cm if m_run is None else jnp.maximum(m_run, cm)
            col_max.append(m_run)
        return col_max

    def prob_pass(i, col_max):
        outs = []
        for h in range(2):
            m_run = col_max[h]
            probs = jnp.concatenate([jnp.exp2(sc_s[i % 2, h, j] - m_run).astype(BF16)
                                     for j in range(i + 1)], axis=0)
            vals = jnp.concatenate([vt_s[j, h * MOBA_VROWS:(h + 1) * MOBA_VROWS, :]
                                    for j in range(i + 1)], axis=1)
            acc = jnp.dot(vals, probs, preferred_element_type=F32)
            outs.append(acc[:HEAD_DIM] / acc[HEAD_DIM:HEAD_DIM + 1])
        o_t = jnp.concatenate(outs, axis=0)
        o_ref[0, pl.ds(i * blk, blk), :] = o_t.T

    pending = None
    for i in range(nb):
        col_max = score_pass(i)
        if pending is not None:
            prob_pass(*pending)
        pending = (i, col_max)
    prob_pass(*pending)


def _moba_attention(p3, g_own, g_prev, far):
    b, s, _ = p3.shape
    nb = s // MOBA_BLOCK
    n_pairs = MOBA_HEADS // 2
    blk_spec = lambda off: pl.BlockSpec((1, s, LANES), lambda bi, hp: (bi, 0, off + hp))
    tab_spec = pl.BlockSpec((2, 1, 2 * MOBA_BLOCK), lambda bi, hp: (hp, 0, 0))
    return pl.pallas_call(
        functools.partial(_moba_kernel, nb=nb),
        grid=(b, n_pairs),
        in_specs=[pl.BlockSpec(memory_space=pltpu.SMEM),
                  blk_spec(0), blk_spec(n_pairs), blk_spec(2 * n_pairs), tab_spec, tab_spec],
        out_specs=pl.BlockSpec((1, s, LANES), lambda bi, hp: (bi, 0, hp)),
        out_shape=jax.ShapeDtypeStruct((b, s, MOBA_W), F32),
        scratch_shapes=[pltpu.VMEM((nb, LANES, MOBA_BLOCK), F32),
                        pltpu.VMEM((nb, MOBA_BLOCK, LANES), BF16),
                        pltpu.VMEM((nb, 2 * MOBA_VROWS, MOBA_BLOCK), BF16),
                        pltpu.VMEM((2, 2, nb, MOBA_BLOCK, MOBA_BLOCK), F32),
                        pltpu.VMEM((2, MOBA_BLOCK, MOBA_BLOCK), F32),
                        pltpu.VMEM((2, MOBA_BLOCK, MOBA_BLOCK), F32)],
        compiler_params=_cparams(("parallel", "parallel")),
        name="moba",
    )(far, p3, p3, p3, g_own, g_prev)


def _softplus(z):
    return jnp.maximum(z, 0.0) + jnp.log(1.0 + jnp.exp(-jnp.abs(z)))


def _sigmoid(z):
    return 1.0 / (1.0 + jnp.exp(-z))


def _rwkv_kernel(r_ref, k_ref, v_ref, wa_ref, g_ref,
                 mu_r_ref, mu_k_ref, mu_v_ref, mu_wa_ref, mu_g_ref,
                 w0_ref, wup_ref, a0_ref, aup_ref, gup_ref, kk_ref, ka_ref, rk_ref,
                 gng_ref, gnb_ref, hsum_ref, tri_ref,
                 o_ref, st_s, prev_r, prev_k, prev_v, prev_wa, prev_g, *, n_seq):
    c = RWKV_CHUNK
    mid = c // 2

    @pl.when(pl.program_id(1) == 0)
    def _():
        st_s[...] = jnp.zeros_like(st_s)
        prev_r[...] = jnp.zeros_like(prev_r)
        prev_k[...] = jnp.zeros_like(prev_k)
        prev_v[...] = jnp.zeros_like(prev_v)
        prev_wa[...] = jnp.zeros_like(prev_wa)
        prev_g[...] = jnp.zeros_like(prev_g)

    n_pairs = RWKV_HEADS // 2
    hsum = hsum_ref[...]
    tri = tri_ref[...]
    row = lax.broadcasted_iota(I32, (c, c), 0)
    col = lax.broadcasted_iota(I32, (c, c), 1)
    strict = row > col
    incl = row >= col
    lane_half = lax.broadcasted_iota(I32, (1, LANES), 1) // HEAD_DIM
    hmask = hsum.astype(F32)
    dot = functools.partial(jnp.dot, preferred_element_type=F32)
    lanes = lambda x, jp: x[:, jp * LANES:(jp + 1) * LANES]

    def group_sum(x):
        return jnp.concatenate([_dot_exact_rhs(lanes(x, jp), hsum, 2) for jp in range(n_pairs)], axis=1)

    def two_heads(x):
        return jnp.concatenate([jnp.where(lane_half == 0, x, 0.0), jnp.where(lane_half == 1, x, 0.0)],
                               axis=0).astype(BF16)

    def side(a, b):
        return jnp.concatenate([a, b], axis=1)

    def mix(g, x_ref, prev_ref, mu_ref):
        x = x_ref[g]
        rows = lax.broadcasted_iota(I32, x.shape, 0)
        shifted = jnp.where(rows == 0, prev_ref[g], pltpu.roll(x, 1, axis=0))
        prev_ref[g] = x[c - 1:c, :]
        return x + (shifted - x) * mu_ref[...]

    def prepare(g):
        r = mix(g, r_ref, prev_r, mu_r_ref)
        k = mix(g, k_ref, prev_k, mu_k_ref)
        v = mix(g, v_ref, prev_v, mu_v_ref)
        xwa = mix(g, wa_ref, prev_wa, mu_wa_ref)
        xg = mix(g, g_ref, prev_g, mu_g_ref)

        w = -_softplus(-(w0_ref[...] + _bdot(jnp.tanh(xwa), wup_ref[...]))) - 0.5
        logd = -jnp.exp(w)
        a = _sigmoid(a0_ref[...] + _bdot(xwa, aup_ref[...]))
        gate = _bdot(_sigmoid(xg), gup_ref[...])
        kk = k * kk_ref[...]
        kk = kk / jnp.maximum(jnp.sqrt(group_sum(kk * kk)), 1e-12)
        k2 = k * (1.0 + (a - 1.0) * ka_ref[...])

        parts = _split_bf16(logd, 3)
        cum = dot(tri, parts[0]) + dot(tri, parts[1]) + dot(tri, parts[2])
        cum_last = cum[c - 1:c, :]
        ref = cum[mid - 1:mid, :]
        rel = cum - ref
        g_in = jnp.exp(rel)
        g_inv = jnp.exp(-rel)
        g_tail = jnp.exp(cum_last - cum)
        a_t = -kk * jnp.exp(rel - logd)
        b_t = kk * a * g_inv
        k_t = k2 * g_inv
        r_t = r * g_in
        b_hat = kk * a * g_tail
        k_hat = k2 * g_tail

        ref_scale = jnp.exp(ref)
        decay = jnp.exp(cum_last)
        bk = jnp.concatenate([b_t, k_t], axis=0).astype(BF16)
        ar_all = jnp.concatenate([a_t, r_t], axis=0)
        per_pair = []
        for jp in range(n_pairs):
            st = st_s[g, jp]
            st_ref = (st * lanes(ref_scale, jp)).astype(BF16)
            both = _dot_nt(lanes(ar_all, jp).astype(BF16), st_ref)
            quads = []
            for half in range(2):
                ar = jnp.where(lane_half == half, lanes(ar_all, jp), 0.0)
                m4 = _dot_nt(ar.astype(BF16), lanes(bk, jp))
                quads.append((jnp.where(strict, m4[:c, :c], 0.0),
                              jnp.where(strict, m4[:c, c:], 0.0).astype(BF16),
                              jnp.where(incl, m4[c:, :c], 0.0).astype(BF16),
                              jnp.where(incl, m4[c:, c:], 0.0).astype(BF16)))
            per_pair.append(dict(
                st=st, p0=both[:c], o0=both[c:], v_rows=two_heads(lanes(v, jp)),
                l_ab=[q[0] for q in quads], l_ak=side(quads[0][1], quads[1][1]),
                m_rb=side(quads[0][2], quads[1][2]), m_rk=side(quads[0][3], quads[1][3]),
                v=lanes(v, jp).astype(BF16), b_hat=lanes(b_hat, jp).astype(BF16),
                k_hat=lanes(k_hat, jp).astype(BF16), decay=lanes(decay, jp)))
        return dict(r=r, k2=k2, v=v, gate=gate, pairs=per_pair)

    seqs = [prepare(g) for g in range(n_seq)]
    heads = [(g, jp, half) for g in range(n_seq) for jp in range(n_pairs) for half in range(2)]

    pair_ids = [(g, jp) for g in range(n_seq) for jp in range(n_pairs)]
    pw = {gh: seqs[gh[0]]["pairs"][gh[1]]["l_ab"][gh[2]].astype(BF16) for gh in heads}
    u_loc = {(g, jp): seqs[g]["pairs"][jp]["p0"]
             + dot(seqs[g]["pairs"][jp]["l_ak"], seqs[g]["pairs"][jp]["v_rows"]) for g, jp in pair_ids}
    span = 1
    while span < c:
        u_loc = {(g, jp): u_loc[(g, jp)] + dot(side(pw[(g, jp, 0)], pw[(g, jp, 1)]), two_heads(u_loc[(g, jp)]))
                 for g, jp in pair_ids}
        span *= 2
        if span < c:
            pw = {gh: dot(pw[gh], pw[gh]).astype(BF16) for gh in heads}

    for g in range(n_seq):
        sq = seqs[g]
        outs = []
        for jp in range(n_pairs):
            pr = sq["pairs"][jp]
            u = u_loc[(g, jp)]
            outs.append(pr["o0"] + dot(pr["m_rb"], two_heads(u)) + dot(pr["m_rk"], pr["v_rows"]))
            upd = _dot_tn(u.astype(BF16), pr["b_hat"]) + _dot_tn(pr["v"], pr["k_hat"])
            st_s[g, jp] = pr["st"] * pr["decay"] + hmask * upd
        o = jnp.concatenate(outs, axis=1)

        inv_n = 1.0 / HEAD_DIM
        mean = group_sum(o) * inv_n
        dev = o - mean
        var = group_sum(dev * dev) * inv_n
        y = dev * lax.rsqrt(var + RWKV_GN_EPS) * gng_ref[...] + gnb_ref[...]
        y = y + group_sum(sq["r"] * sq["k2"] * rk_ref[...]) * sq["v"]
        o_ref[g] = y * sq["gate"]


def _rwkv(p3, mu, w0, w_up, a0, a_up, g_up, k_k, k_a, r_k, gn_g, gn_b):
    b, s, _ = p3.shape
    c = RWKV_CHUNK
    w = RWKV_W
    n_seq = RWKV_SEQS_PER_STEP if b % RWKV_SEQS_PER_STEP == 0 else 1
    base = 3 * MOBA_W
    assert base % w == 0 and (base + 3 * w) % LANES == 0
    lora = DECAY_LORA + AAA_LORA

    def col_spec(width, off_cols):
        assert off_cols % width == 0
        return pl.BlockSpec((n_seq, c, width), lambda bi, ci: (bi, ci, off_cols // width))

    row = lambda x: x.reshape(1, -1).astype(F32)
    const = lambda shape: pl.BlockSpec(shape, lambda bi, ci: (0,) * len(shape))
    wup_pad = jnp.concatenate([w_up, jnp.zeros((AAA_LORA, w), F32)], axis=0).astype(BF16)
    aup_pad = jnp.concatenate([jnp.zeros((DECAY_LORA, w), F32), a_up], axis=0).astype(BF16)
    head = np.arange(LANES) // HEAD_DIM
    hsum = jnp.asarray((head[:, None] == head[None, :]).astype(np.float32)).astype(BF16)
    tri = jnp.asarray(np.tril(np.ones((c, c), np.float32))).astype(BF16)
    vec_args = [mu[:w], mu[w:2 * w], mu[2 * w:3 * w], mu[3 * w:3 * w + lora], mu[3 * w + lora:],
                w0, None, a0, None, None, k_k, k_a, r_k.reshape(-1), gn_g, gn_b]
    args = [p3, p3, p3, p3, p3]
    specs = [col_spec(w, base), col_spec(w, base + w), col_spec(w, base + 2 * w),
             col_spec(lora, base + 3 * w), col_spec(GATE_LORA, base + 3 * w + lora)]
    mats = {6: wup_pad, 8: aup_pad, 9: g_up.astype(BF16)}
    for idx, a in enumerate(vec_args):
        arr = mats[idx] if a is None else row(a)
        args.append(arr)
        specs.append(const(arr.shape))
    for arr in (hsum, tri):
        args.append(arr)
        specs.append(const(arr.shape))
    return pl.pallas_call(
        functools.partial(_rwkv_kernel, n_seq=n_seq),
        grid=(b // n_seq, s // c),
        in_specs=specs,
        out_specs=pl.BlockSpec((n_seq, c, w), lambda bi, ci: (bi, ci, 0)),
        out_shape=jax.ShapeDtypeStruct((b, s, w), F32),
        scratch_shapes=[pltpu.VMEM((n_seq, RWKV_HEADS // 2, LANES, LANES), F32),
                        pltpu.VMEM((n_seq, 1, w), F32),
                        pltpu.VMEM((n_seq, 1, w), F32), pltpu.VMEM((n_seq, 1, w), F32),
                        pltpu.VMEM((n_seq, 1, lora), F32), pltpu.VMEM((n_seq, 1, GATE_LORA), F32)],
        compiler_params=_cparams(("parallel", "arbitrary")),
        name="rwkv",
    )(*args)


def _mem_attn_kernel(q_ref, kv_ref, o_ref):
    scale = HEAD_DIM ** -0.5
    lane_head = lax.broadcasted_iota(I32, (1, MEM_W), 1) // HEAD_DIM
    q = q_ref[0] * scale
    mk = kv_ref[0, :, :MEM_W].astype(BF16)
    mv = kv_ref[0, :, MEM_W:].astype(BF16)
    out = jnp.zeros(q.shape, F32)
    for h in range(MEM_HEADS):
        hm = lane_head == h
        s = _dot_nt(jnp.where(hm, q, 0.0).astype(BF16), mk)
        s = s - jnp.max(s, axis=-1, keepdims=True)
        e = jnp.exp(s)
        p = e / jnp.sum(e, axis=-1, keepdims=True)
        out = jnp.where(hm, jnp.dot(p.astype(BF16), mv, preferred_element_type=F32), out)
    o_ref[0] = out


def _mem_attention(p3, mkv3, tq):
    b, s, _ = p3.shape
    m = mkv3.shape[1]
    off = (3 * MOBA_W + RWKV_COLS) // MEM_W
    assert off * MEM_W == 3 * MOBA_W + RWKV_COLS
    return pl.pallas_call(
        _mem_attn_kernel,
        grid=(b, s // tq),
        in_specs=[pl.BlockSpec((1, tq, MEM_W), lambda bi, i: (bi, i, off)),
                  pl.BlockSpec((1, m, 2 * MEM_W), lambda bi, i: (bi, 0, 0))],
        out_specs=pl.BlockSpec((1, tq, MEM_W), lambda bi, i: (bi, i, 0)),
        out_shape=jax.ShapeDtypeStruct((b, s, MEM_W), F32),
        compiler_params=_cparams(("parallel", "parallel")),
        name="mem_attn",
    )(p3, mkv3)


def _out_router_kernel(x_ref, ym_ref, yr_ref, ye_ref, wo1_ref, wo2_ref, wo3_ref, g_ref,
                       wrh_ref, wrl_ref, br_ref, upper_ref, ones_ref,
                       x1_ref, h_ref, idx_ref, rank_ref, wgt_ref, cnt_ref, run_s):
    @pl.when(pl.program_id(0) == 0)
    def _():
        run_s[...] = jnp.zeros_like(run_s)

    x1 = (x_ref[...] + _bdot(ym_ref[...], wo1_ref[...]) + _bdot(yr_ref[...], wo2_ref[...])
          + _bdot(ye_ref[...], wo3_ref[...]))
    x1_ref[...] = x1
    h = _rms(x1, g_ref[...])
    h_ref[...] = _pack_halves(h)
    tm = h.shape[0]
    h_hi, h_lo = _split_bf16(h, 2)
    dot = functools.partial(jnp.dot, preferred_element_type=F32)
    logits = dot(h_hi, wrh_ref[...]) + (dot(h_hi, wrl_ref[...]) + dot(h_lo, wrh_ref[...])) + br_ref[...]
    lg = logits.T[:N_EXPERTS, :]
    e_iota = lax.broadcasted_iota(I32, (N_EXPERTS, tm), 0)

    vals, idxs = [], []
    for _ in range(TOP_K):
        m = jnp.max(lg, axis=0, keepdims=True)
        idx = jnp.min(jnp.where(lg == m, e_iota, N_EXPERTS), axis=0, keepdims=True)
        vals.append(m)
        idxs.append(idx)
        lg = jnp.where(e_iota == idx, NEG_INF, lg)
    exps = [jnp.exp(vk - vals[0]) for vk in vals]
    denom = exps[0] + exps[1] + exps[2] + exps[3]

    chosen = jnp.zeros((N_EXPERTS, tm), F32)
    for idx in idxs:
        chosen = chosen + jnp.where(e_iota == idx, 1.0, 0.0)
    chosen = chosen.astype(BF16)
    run = run_s[...]
    before = dot(chosen, upper_ref[...]) + jnp.concatenate([run] * (tm // LANES), axis=1)
    run_s[...] = run + dot(chosen, ones_ref[...])
    cnt_ref[...] = run_s[...]

    zero_i = jnp.zeros((8 - TOP_K, tm), I32)
    ranks = [jnp.sum(jnp.where(e_iota == idx, before, 0.0), axis=0, keepdims=True).astype(I32)
             for idx in idxs]
    idx_ref[0] = jnp.concatenate(idxs + [zero_i], axis=0)
    rank_ref[0] = jnp.concatenate(ranks + [zero_i], axis=0)
    wrows = jnp.concatenate([e / denom for e in exps] + [jnp.zeros((LANES - TOP_K, tm), F32)], axis=0)
    wgt_ref[...] = wrows.T


def _out_router(x2, ym, yr, ye, w_out, g_ffn, w_router, b_router, tm):
    t, d = x2.shape
    wo = w_out.astype(BF16)
    wo1, wo2, wo3 = wo[:MOBA_W], wo[MOBA_W:MOBA_W + RWKV_W], wo[MOBA_W + RWKV_W:]
    wr = jnp.zeros((d, LANES), F32).at[:, :N_EXPERTS].set(w_router)
    wr_hi = wr.astype(BF16)
    wr_lo = (wr - wr_hi.astype(F32)).astype(BF16)
    br = jnp.full((1, LANES), NEG_INF, F32).at[0, :N_EXPERTS].set(b_router)
    upper = jnp.asarray(np.triu(np.ones((tm, tm), np.float32), 1)).astype(BF16)
    ones = jnp.ones((tm, LANES), BF16)
    tile = lambda n: pl.BlockSpec((tm, n), lambda i: (i, 0))
    slots = pl.BlockSpec((1, 8, tm), lambda i: (i, 0, 0))
    const = lambda a: pl.BlockSpec(a.shape, lambda i: (0,) * a.ndim)
    g2 = g_ffn.reshape(1, d)
    n = t // tm
    return pl.pallas_call(
        _out_router_kernel,
        grid=(n,),
        in_specs=[tile(d), tile(MOBA_W), tile(RWKV_W), tile(MEM_W), const(wo1), const(wo2), const(wo3),
                  const(g2), const(wr_hi), const(wr_lo), const(br), const(upper), const(ones)],
        out_specs=[tile(d), tile(d // 2), slots, slots, tile(LANES),
                   pl.BlockSpec((N_EXPERTS, LANES), lambda i: (0, 0))],
        out_shape=[jax.ShapeDtypeStruct((t, d), F32), jax.ShapeDtypeStruct((t, d // 2), I32),
                   jax.ShapeDtypeStruct((n, 8, tm), I32), jax.ShapeDtypeStruct((n, 8, tm), I32),
                   jax.ShapeDtypeStruct((t, LANES), F32), jax.ShapeDtypeStruct((N_EXPERTS, LANES), F32)],
        scratch_shapes=[pltpu.VMEM((N_EXPERTS, LANES), F32)],
        compiler_params=_cparams(("arbitrary",)),
        name="out_router",
    )(x2, ym, yr, ye, wo1, wo2, wo3, g2, wr_hi, wr_lo, br, upper, ones)


def _sc_mesh():
    return plsc.VectorSubcoreMesh(core_axis_name="c", subcore_axis_name="s",
                                  num_cores=SC_CORES, num_subcores=SC_SUBCORES)


def _sc_split(n_rows):
    n_workers = SC_CORES * SC_SUBCORES
    per_worker = n_rows // n_workers
    assert per_worker * n_workers == n_rows and per_worker % (SC_ROWS * SC_INFLIGHT) == 0
    assert SC_ROWS % 8 == 0
    return per_worker


def _sc_gather_rows(table, idx):
    n_rows = idx.shape[0]
    width = table.shape[1]
    per_worker = _sc_split(n_rows)

    buf = lambda shape, dtype: [pltpu.VMEM(shape, dtype) for _ in range(SC_INFLIGHT)]

    @functools.partial(
        pl.kernel, mesh=_sc_mesh(),
        out_type=jax.ShapeDtypeStruct((n_rows, width), table.dtype),
        scratch_types=[buf((SC_ROWS,), I32), buf((SC_ROWS, width), table.dtype),
                       [pltpu.SemaphoreType.DMA for _ in range(SC_INFLIGHT)]],
        name="sc_gather",
    )
    def gather(table_hbm, idx_hbm, out_hbm, idx_v, rows_v, sems):
        worker = lax.axis_index("s") * SC_CORES + lax.axis_index("c")
        base = worker * per_worker

        @pl.loop(0, per_worker // (SC_ROWS * SC_INFLIGHT))
        def _(step):
            offs = [pl.multiple_of(base + (step * SC_INFLIGHT + b) * SC_ROWS, 8) for b in range(SC_INFLIGHT)]
            copies = []
            for b in range(SC_INFLIGHT):
                pltpu.sync_copy(idx_hbm.at[pl.ds(offs[b], SC_ROWS)], idx_v[b])
                copies.append(pltpu.async_copy(table_hbm.at[idx_v[b]], rows_v[b], sems[b]))
            for b in range(SC_INFLIGHT):
                copies[b].wait()
                pltpu.sync_copy(rows_v[b], out_hbm.at[pl.ds(offs[b], SC_ROWS)])

    return gather(table, idx)


def _sc_scatter_rows(rows, idx, n_out):
    n_src, width = rows.shape
    n_copies = idx.shape[0] // n_src
    assert n_copies * n_src == idx.shape[0]
    per_worker = _sc_split(n_src)

    @functools.partial(
        pl.kernel, mesh=_sc_mesh(),
        out_type=jax.ShapeDtypeStruct((n_out, width), rows.dtype),
        scratch_types=[[pltpu.VMEM((SC_ROWS,), I32) for _ in range(n_copies)],
                       pltpu.VMEM((SC_ROWS, width), rows.dtype),
                       [pltpu.SemaphoreType.DMA for _ in range(n_copies)]],
        name="sc_scatter",
    )
    def scatter(rows_hbm, idx_hbm, out_hbm, idx_v, rows_v, sems):
        worker = lax.axis_index("s") * SC_CORES + lax.axis_index("c")
        base = worker * per_worker

        @pl.loop(0, per_worker // SC_ROWS)
        def _(step):
            src = pl.multiple_of(base + step * SC_ROWS, 8)
            pltpu.sync_copy(rows_hbm.at[pl.ds(src, SC_ROWS)], rows_v)
            copies = []
            for cpy in range(n_copies):
                off = pl.multiple_of(cpy * n_src + src, 8)
                pltpu.sync_copy(idx_hbm.at[pl.ds(off, SC_ROWS)], idx_v[cpy])
                copies.append(pltpu.async_copy(rows_v, out_hbm.at[idx_v[cpy]], sems[cpy]))
            for cp in copies:
                cp.wait()

    return scatter(rows, idx)


def _experts_kernel(be_ref, bv_ref, nused_ref, next_ref, slot_ref,
                    xs_ref, wgu_hbm, bg_ref, bu_ref, wd_hbm, bd_ref, perm_ref,
                    o_ref, wgu_buf, wd_buf, wg_s, wu_s, wd_s, sem):
    i = pl.program_id(0)
    e = be_ref[i]
    used = i < nused_ref[0]
    changed = ((i == 0) | (e != be_ref[jnp.maximum(i - 1, 0)])) & used
    slot = slot_ref[e]

    def weight_copies(expert, buf_slot):
        return (pltpu.make_async_copy(wgu_hbm.at[expert], wgu_buf.at[buf_slot], sem.at[buf_slot, 0]),
                pltpu.make_async_copy(wd_hbm.at[expert], wd_buf.at[buf_slot], sem.at[buf_slot, 1]))

    @pl.when((i == 0) & used)
    def _():
        for cp in weight_copies(e, slot):
            cp.start()

    @pl.when(changed)
    def _():
        for cp in weight_copies(e, slot):
            cp.wait()
        nxt = next_ref[e]

        @pl.when(nxt >= 0)
        def _():
            for cp in weight_copies(nxt, 1 - slot):
                cp.start(priority=1)

        half = LANES
        for cblk in range(2 * D_EXPERT // (2 * half)):
            wt = wgu_buf[slot, :, cblk * 2 * half:(cblk + 1) * 2 * half].astype(BF16)
            sep = jnp.dot(wt, perm_ref[...], preferred_element_type=F32).astype(BF16)
            wg_s[:, cblk * half:(cblk + 1) * half] = sep[:, :half]
            wu_s[:, cblk * half:(cblk + 1) * half] = sep[:, half:]
        wd_s[...] = wd_buf[slot].astype(BF16)

    @pl.when(used)
    def _():
        rows = lax.broadcasted_iota(I32, xs_ref.shape, 0)
        xb = _unpack_halves(jnp.where(rows < bv_ref[i], xs_ref[...], 0)).astype(BF16)
        gate = jnp.dot(xb, wg_s[...], preferred_element_type=F32) + bg_ref[0]
        up = jnp.dot(xb, wu_s[...], preferred_element_type=F32) + bu_ref[0]
        gate = jnp.minimum(gate, SWIGLU_LIMIT)
        up = jnp.clip(up, -SWIGLU_LIMIT, SWIGLU_LIMIT)
        glu = gate * _sigmoid(gate * SWIGLU_ALPHA)
        act = ((up + 1.0) * glu).astype(BF16)
        o_ref[...] = _pack_halves(jnp.dot(act, wd_s[...], preferred_element_type=F32) + bd_ref[0])

    @pl.when(jnp.logical_not(used))
    def _():
        o_ref[...] = jnp.zeros_like(o_ref)


def _experts(blk_expert, blk_valid, n_used, next_expert, buf_slot, xs, w_gate_up, b_gate_up, w_down,
             b_down):
    p_rows = xs.shape[0]
    d = 2 * xs.shape[1]
    n_blocks = p_rows // EXPERT_ROWS
    bg = b_gate_up[:, 0::2].reshape(N_EXPERTS, 1, D_EXPERT)
    bu = b_gate_up[:, 1::2].reshape(N_EXPERTS, 1, D_EXPERT)
    bd = b_down.reshape(N_EXPERTS, 1, d)
    perm_np = np.zeros((2 * LANES, 2 * LANES), np.float32)
    perm_np[2 * np.arange(LANES), np.arange(LANES)] = 1.0
    perm_np[2 * np.arange(LANES) + 1, LANES + np.arange(LANES)] = 1.0
    perm = jnp.asarray(perm_np).astype(BF16)
    by_expert = lambda shape: pl.BlockSpec((1,) + shape, lambda i, be, *_: (be[i], 0, 0))
    grid_spec = pltpu.PrefetchScalarGridSpec(
        num_scalar_prefetch=5,
        grid=(n_blocks,),
        in_specs=[pl.BlockSpec((EXPERT_ROWS, d // 2), lambda i, *_: (i, 0)),
                  pl.BlockSpec(memory_space=pl.ANY), by_expert((1, D_EXPERT)), by_expert((1, D_EXPERT)),
                  pl.BlockSpec(memory_space=pl.ANY), by_expert((1, d)),
                  pl.BlockSpec(perm.shape, lambda i, *_: (0, 0))],
        out_specs=pl.BlockSpec((EXPERT_ROWS, d // 2), lambda i, *_: (i, 0)),
        scratch_shapes=[pltpu.VMEM((2, d, 2 * D_EXPERT), F32), pltpu.VMEM((2, D_EXPERT, d), F32),
                        pltpu.VMEM((d, D_EXPERT), BF16), pltpu.VMEM((d, D_EXPERT), BF16),
                        pltpu.VMEM((D_EXPERT, d), BF16), pltpu.SemaphoreType.DMA((2, 2))],
    )
    return pl.pallas_call(
        _experts_kernel,
        grid_spec=grid_spec,
        out_shape=jax.ShapeDtypeStruct((p_rows, d // 2), I32),
        compiler_params=_cparams(("arbitrary",)),
        name="experts",
    )(blk_expert, blk_valid, n_used, next_expert, buf_slot, xs, w_gate_up, bg, bu, w_down, bd, perm)


def _combine_kernel(yg_ref, x1_ref, w_ref, g_ref, *rest):
    o_ref = rest[-1]
    acc = x1_ref[...]
    wts = w_ref[...]
    for kk in range(TOP_K):
        acc = acc + _unpack_halves(yg_ref[kk]) * wts[:, kk:kk + 1]
    o_ref[...] = _rms(acc, g_ref[...])


def _combine(yg, x1, wts, g_final, tb, first_tile, partial_out):
    t, d = x1.shape
    half = yg.shape[2]
    tile = lambda n: pl.BlockSpec((tb, n), lambda i: (i + first_tile, 0))
    in_specs = [pl.BlockSpec((TOP_K, tb, half), lambda i: (0, i, 0)), tile(d), tile(LANES),
                pl.BlockSpec((1, d), lambda i: (0, 0))]
    args = [yg, x1, wts, g_final.reshape(1, d)]
    aliases = {}
    if partial_out is not None:
        in_specs.append(pl.BlockSpec(memory_space=pl.ANY))
        args.append(partial_out)
        aliases = {len(args) - 1: 0}
    return pl.pallas_call(
        _combine_kernel,
        grid=(yg.shape[1] // tb,),
        in_specs=in_specs,
        out_specs=tile(d),
        out_shape=jax.ShapeDtypeStruct((t, d), F32),
        input_output_aliases=aliases,
        compiler_params=_cparams(("parallel",)),
        name="combine",
    )(*args)


def _layer(x, mem, w_in, w_out, w_mem_kv, g_mix, g_mem, g_ffn, bias_tables, mu, w0, w_up, a0, a_up,
           g_up, k_k, k_a, r_k, gn_g, gn_b, w_router, b_router, w_gate_up, b_gate_up, w_down, b_down,
           g_last):
    b, s, d = x.shape
    m = mem.shape[1]
    t = b * s
    x2 = x.reshape(t, d)

    p = _norm_matmul(x2, g_mix, w_in.astype(BF16), 512, "in_proj")
    p3 = p.reshape(b, s, IN_COLS)
    y_moba = _moba_attention(p3, *bias_tables)
    y_rwkv = _rwkv(p3, mu, w0, w_up, a0, a_up, g_up, k_k, k_a, r_k, gn_g, gn_b)
    mkv = _norm_matmul(mem.reshape(b * m, d), g_mem, w_mem_kv.astype(BF16), 512, "mem_kv")
    y_mem = _mem_attention(p3, mkv.reshape(b, m, 2 * MEM_W), 512)

    tb = ROUTE_TOKENS
    x1, h2, idx_o, rank_o, wgt_p, cnt = _out_router(
        x2, y_moba.reshape(t, MOBA_W), y_rwkv.reshape(t, RWKV_W), y_mem.reshape(t, MEM_W),
        w_out, g_ffn, w_router, b_router, tb)

    counts = cnt[:, 0].astype(I32)
    padded = (counts + EXPERT_ROWS - 1) // EXPERT_ROWS * EXPERT_ROWS
    pad_ends = jnp.cumsum(padded)
    pad_starts = (pad_ends - padded).astype(I32)
    n_blocks = (t * TOP_K) // EXPERT_ROWS + N_EXPERTS
    blk_start = jnp.arange(n_blocks, dtype=I32) * EXPERT_ROWS
    blk_expert = jnp.minimum(jnp.sum(blk_start[:, None] >= pad_ends[None, :], axis=1),
                             N_EXPERTS - 1).astype(I32)
    of_block = blk_expert[:, None] == jnp.arange(N_EXPERTS, dtype=I32)[None, :]
    rows_left = jnp.sum(jnp.where(of_block, (counts + pad_starts)[None, :], 0), axis=1) - blk_start
    blk_valid = jnp.clip(rows_left, 0, EXPERT_ROWS)
    n_used = (pad_ends[-1:] // EXPERT_ROWS).astype(I32)
    has_rows = counts > 0
    first_from = lax.cummin(jnp.where(has_rows, jnp.arange(N_EXPERTS, dtype=I32), N_EXPERTS), reverse=True)
    next_expert = jnp.concatenate([first_from[1:], jnp.full((1,), N_EXPERTS, I32)])
    next_expert = jnp.where(next_expert < N_EXPERTS, next_expert, -1).astype(I32)
    buf_slot = ((jnp.cumsum(has_rows.astype(I32)) - has_rows.astype(I32)) % 2).astype(I32)
    idx_kt = jnp.swapaxes(idx_o[:, :TOP_K, :], 0, 1).reshape(TOP_K, t)
    rank_kt = jnp.swapaxes(rank_o[:, :TOP_K, :], 0, 1).reshape(TOP_K, t)
    experts = jnp.arange(N_EXPERTS, dtype=I32)[:, None, None]
    start_kt = jnp.sum(jnp.where(idx_kt[None] == experts, pad_starts[:, None, None], 0), axis=0)
    dest = (start_kt + rank_kt).reshape(TOP_K * t)

    xs = _sc_scatter_rows(h2, dest, n_blocks * EXPERT_ROWS)
    ys = _experts(blk_expert, blk_valid.astype(I32), n_used, next_expert, buf_slot, xs,
                  w_gate_up, b_gate_up, w_down, b_down)
    dest_kt = dest.reshape(TOP_K, t)
    out = None
    for part in range(COMBINE_PARTS):
        t_part = t // COMBINE_PARTS
        rows = dest_kt[:, part * t_part:(part + 1) * t_part].reshape(TOP_K * t_part)
        yg = _sc_gather_rows(ys, rows).reshape(TOP_K, t_part, d // 2)
        out = _combine(yg, x1, wgt_p, g_last, tb, part * (t_part // tb), out)
    return out.reshape(b, s, d)


def kernel(x, mem, w_in, w_out, w_mem_kv, g_mix, g_mem, g_ffn, g_final, rel_bias, rwkv_mu, rwkv_w0,
           rwkv_w_up, rwkv_a0, rwkv_a_up, rwkv_g_up, rwkv_k_k, rwkv_k_a, rwkv_r_k, rwkv_gn_g, rwkv_gn_b,
           w_router, b_router, w_gate_up, b_gate_up, w_down, b_down):
    depth = w_in.shape[0]
    assert depth == 1, "the final norm is fused into the last layer's combine kernel"
    bias_tables = _moba_bias_tables(rel_bias)
    l = 0
    return _layer(x, mem, w_in[l], w_out[l], w_mem_kv[l], g_mix[l], g_mem[l], g_ffn[l], bias_tables,
                  rwkv_mu[l], rwkv_w0[l], rwkv_w_up[l], rwkv_a0[l], rwkv_a_up[l], rwkv_g_up[l],
                  rwkv_k_k[l], rwkv_k_a[l], rwkv_r_k[l], rwkv_gn_g[l], rwkv_gn_b[l], w_router[l],
                  b_router[l], w_gate_up[l], b_gate_up[l], w_down[l], b_down[l], g_final)
```

```python
import functools
import math

import numpy as np
import jax
import jax.numpy as jnp
from jax import lax
from jax.experimental import pallas as pl
from jax.experimental.pallas import tpu as pltpu
from jax.experimental.pallas import tpu_sc as plsc

F32 = jnp.float32
BF16 = jnp.bfloat16
I32 = jnp.int32
HI = lax.Precision.HIGHEST

D_MODEL = 1024
HEAD_DIM = 64
MOBA_HEADS = 6
RWKV_HEADS = 6
MEM_HEADS = 4
MOBA_W = MOBA_HEADS * HEAD_DIM
RWKV_W = RWKV_HEADS * HEAD_DIM
MEM_W = MEM_HEADS * HEAD_DIM
MOBA_BLOCK = 256
MOBA_TOPK = 3
N_BUCKETS = 32
MAX_DISTANCE = 128
DECAY_LORA = 64
AAA_LORA = 64
GATE_LORA = 128
RWKV_COLS = 3 * RWKV_W + DECAY_LORA + AAA_LORA + GATE_LORA
RWKV_GN_EPS = 64e-5
IN_COLS = 3 * MOBA_W + RWKV_COLS + MEM_W
N_EXPERTS = 32
TOP_K = 4
D_EXPERT = D_MODEL
SWIGLU_ALPHA = 1.702
SWIGLU_LIMIT = 7.0
RMS_EPS = 1e-5

LANES = 128
RWKV_CHUNK = 128
RWKV_SEQS_PER_STEP = 4
EXPERT_ROWS = 256
ROUTE_TOKENS = 512
SC_CORES = 2
SC_SUBCORES = 16
SC_ROWS = 64
SC_INFLIGHT = 2
COMBINE_PARTS = 4
VMEM_LIMIT = 56 * 1024 * 1024
NEG_INF = float("-inf")
HIGH_HALF = -65536
LOG2_E = 1.4426950408889634
MOBA_VROWS = HEAD_DIM + 16


def _cparams(sem):
    return pltpu.CompilerParams(dimension_semantics=sem, vmem_limit_bytes=VMEM_LIMIT)


def _rms(x, g):
    return x * lax.rsqrt(jnp.mean(x * x, axis=-1, keepdims=True) + RMS_EPS) * g


def _bdot(a, b):
    return jnp.dot(a.astype(BF16), b.astype(BF16), preferred_element_type=F32)


def _hdot(a, b):
    return jnp.dot(a, b, preferred_element_type=F32, precision=HI)


def _dot_nt(a, b):
    return lax.dot_general(a, b, (((1,), (1,)), ((), ())), preferred_element_type=F32)


def _dot_tn(a, b):
    return lax.dot_general(a, b, (((0,), (0,)), ((), ())), preferred_element_type=F32)


def _split_bf16(x, terms):
    parts = []
    for _ in range(terms):
        hi = x.astype(BF16)
        parts.append(hi)
        x = x - hi.astype(F32)
    return parts


def _dot_exact_rhs(x, m_bf16, terms):
    acc = None
    for part in _split_bf16(x, terms):
        d = jnp.dot(part, m_bf16, preferred_element_type=F32)
        acc = d if acc is None else acc + d
    return acc


def _pack_halves(x):
    n = x.shape[1] // 2
    lo = pltpu.bitcast(x[:, :n].astype(BF16).astype(F32), I32)
    hi = pltpu.bitcast(x[:, n:].astype(BF16).astype(F32), I32)
    return (hi & HIGH_HALF) | lax.shift_right_logical(lo, 16)


def _unpack_halves(w):
    lo = pltpu.bitcast(w << 16, F32)
    hi = pltpu.bitcast(w & HIGH_HALF, F32)
    return jnp.concatenate([lo, hi], axis=1)


def _norm_matmul_kernel(x_ref, g_ref, w_ref, o_ref):
    h = _rms(x_ref[...], g_ref[...])
    o_ref[...] = jnp.dot(h.astype(BF16), w_ref[...], preferred_element_type=F32)


def _norm_matmul(x, g, w_bf16, tm, name):
    t, d = x.shape
    n = w_bf16.shape[1]
    tm = min(tm, t)
    return pl.pallas_call(
        _norm_matmul_kernel,
        grid=(t // tm,),
        in_specs=[pl.BlockSpec((tm, d), lambda i: (i, 0)),
                  pl.BlockSpec((1, d), lambda i: (0, 0)),
                  pl.BlockSpec((d, n), lambda i: (0, 0))],
        out_specs=pl.BlockSpec((tm, n), lambda i: (i, 0)),
        out_shape=jax.ShapeDtypeStruct((t, n), F32),
        compiler_params=_cparams(("parallel",)),
        name=name,
    )(x, g.reshape(1, d), w_bf16)


def _t5_bucket_np(dist):
    n = np.maximum(dist, 0)
    max_exact = N_BUCKETS // 2
    nf = np.maximum(n, 1).astype(np.float64)
    large = max_exact + (np.log(nf / max_exact) / math.log(MAX_DISTANCE / max_exact)
                         * (N_BUCKETS - max_exact)).astype(np.int64)
    large = np.minimum(large, N_BUCKETS - 1)
    return np.where(n < max_exact, n, large)


def _moba_bias_tables(rel_bias):
    n = MOBA_BLOCK
    assert np.all(_t5_bucket_np(np.arange(n + 1, 64 * n)) == N_BUCKETS - 1)
    bias_t = rel_bias.astype(F32).T * LOG2_E

    def by_distance(dist):
        bucket = jnp.asarray(_t5_bucket_np(dist), I32)[None]
        tab = jnp.zeros((MOBA_HEADS, dist.shape[0]), F32)
        for b in range(N_BUCKETS):
            tab = jnp.where(bucket == b, bias_t[:, b][:, None], tab)
        return tab

    d = np.arange(2 * n)
    g_own = jnp.where(jnp.asarray(d < n)[None], by_distance(np.where(d < n, d, 0)), NEG_INF)
    g_prev = by_distance(np.where(d < n, n + d, d - n))
    far = bias_t[:, N_BUCKETS - 1]
    return g_own[:, None, :], g_prev[:, None, :], far


def _moba_kernel(far_ref, q_ref, k_ref, v_ref, g0_ref, g1_ref, o_ref,
                 qt_s, kb_s, vt_s, sc_s, t0_ref, t1_ref, *, nb):
    hp = pl.program_id(1)
    blk = MOBA_BLOCK
    scale = HEAD_DIM ** -0.5 * LOG2_E
    row_head = lax.broadcasted_iota(I32, (LANES, 1), 0) // HEAD_DIM
    lane_head = lax.broadcasted_iota(I32, (1, LANES), 1) // HEAD_DIM
    ones_rows = jnp.ones((MOBA_VROWS - HEAD_DIM, blk), BF16)

    for h in range(2):
        for g_ref, t_ref in ((g0_ref, t0_ref), (g1_ref, t1_ref)):
            rows = jnp.broadcast_to(g_ref[h], (blk, 2 * blk))
            t_ref[h] = pltpu.roll(rows, 0, 1, stride=1, stride_axis=0)[:, :blk]

    kmean_rows = []
    for j in range(nb):
        sl = pl.ds(j * blk, blk)
        qt_s[j] = (q_ref[0, sl, :] * scale).T
        vt = v_ref[0, sl, :].T.astype(BF16)
        for h in range(2):
            vt_s[j, h * MOBA_VROWS:h * MOBA_VROWS + HEAD_DIM, :] = vt[h * HEAD_DIM:(h + 1) * HEAD_DIM, :]
            vt_s[j, h * MOBA_VROWS + HEAD_DIM:(h + 1) * MOBA_VROWS, :] = ones_rows
        kj = k_ref[0, sl, :]
        kb_s[j] = kj.astype(BF16)
        kmean_rows.append(jnp.mean(kj, axis=0, keepdims=True))
    kmean = jnp.concatenate(kmean_rows, axis=0)
    km2 = jnp.concatenate([jnp.where(lane_head == 0, kmean, 0.0),
                           jnp.where(lane_head == 1, kmean, 0.0)], axis=0)
    blk_iota = lax.broadcasted_iota(I32, (nb, blk), 0)

    def score_pass(i):
        qt = qt_s[i]
        negs = [None, None]
        if i > MOBA_TOPK:
            gate2 = _hdot(km2, qt)
            for h in range(2):
                g = gate2[h * nb:(h + 1) * nb, :]
                cnt = jnp.zeros((nb, blk), I32)
                for m in range(i):
                    gm = g[m:m + 1, :]
                    beats = (gm > g) | ((gm == g) & (m < blk_iota))
                    cnt = cnt + jnp.where(beats, 1, 0)
                negs[h] = jnp.where(cnt < MOBA_TOPK, 0.0, NEG_INF)
        col_max = []
        for h in range(2):
            qt_h = jnp.where(row_head == h, qt, 0.0).astype(BF16)
            far_bias = far_ref[2 * hp + h]
            m_run = None
            for j in range(i + 1):
                s = jnp.dot(kb_s[j], qt_h, preferred_element_type=F32)
                if j == i:
                    s = s + t0_ref[h]
                else:
                    bias = t1_ref[h] if j == i - 1 else far_bias
                    if negs[h] is not None:
                        bias = bias + negs[h][j:j + 1, :]
                    s = s + bias
                sc_s[i % 2, h, j] = s
                cm = jnp.max(s, axis=0, keepdims=True)
                m_run = cm if m_run is None else jnp.maximum(m_run, cm)
            col_max.append(m_run)
        return col_max

    def prob_pass(i, col_max):
        outs = []
        for h in range(2):
            m_run = col_max[h]
            probs = jnp.concatenate([jnp.exp2(sc_s[i % 2, h, j] - m_run).astype(BF16)
                                     for j in range(i + 1)], axis=0)
            vals = jnp.concatenate([vt_s[j, h * MOBA_VROWS:(h + 1) * MOBA_VROWS, :]
                                    for j in range(i + 1)], axis=1)
            acc = jnp.dot(vals, probs, preferred_element_type=F32)
            outs.append(acc[:HEAD_DIM] / acc[HEAD_DIM:HEAD_DIM + 1])
        o_t = jnp.concatenate(outs, axis=0)
        o_ref[0, pl.ds(i * blk, blk), :] = o_t.T

    pending = None
    for i in range(nb):
        col_max = score_pass(i)
        if pending is not None:
            prob_pass(*pending)
        pending = (i, col_max)
    prob_pass(*pending)


def _moba_attention(p3, g_own, g_prev, far):
    b, s, _ = p3.shape
    nb = s // MOBA_BLOCK
    n_pairs = MOBA_HEADS // 2
    blk_spec = lambda off: pl.BlockSpec((1, s, LANES), lambda bi, hp: (bi, 0, off + hp))
    tab_spec = pl.BlockSpec((2, 1, 2 * MOBA_BLOCK), lambda bi, hp: (hp, 0, 0))
    return pl.pallas_call(
        functools.partial(_moba_kernel, nb=nb),
        grid=(b, n_pairs),
        in_specs=[pl.BlockSpec(memory_space=pltpu.SMEM),
                  blk_spec(0), blk_spec(n_pairs), blk_spec(2 * n_pairs), tab_spec, tab_spec],
        out_specs=pl.BlockSpec((1, s, LANES), lambda bi, hp: (bi, 0, hp)),
        out_shape=jax.ShapeDtypeStruct((b, s, MOBA_W), F32),
        scratch_shapes=[pltpu.VMEM((nb, LANES, MOBA_BLOCK), F32),
                        pltpu.VMEM((nb, MOBA_BLOCK, LANES), BF16),
                        pltpu.VMEM((nb, 2 * MOBA_VROWS, MOBA_BLOCK), BF16),
                        pltpu.VMEM((2, 2, nb, MOBA_BLOCK, MOBA_BLOCK), F32),
                        pltpu.VMEM((2, MOBA_BLOCK, MOBA_BLOCK), F32),
                        pltpu.VMEM((2, MOBA_BLOCK, MOBA_BLOCK), F32)],
        compiler_params=_cparams(("parallel", "parallel")),
        name="moba",
    )(far, p3, p3, p3, g_own, g_prev)


def _softplus(z):
    return jnp.maximum(z, 0.0) + jnp.log(1.0 + jnp.exp(-jnp.abs(z)))


def _sigmoid(z):
    return 1.0 / (1.0 + jnp.exp(-z))


def _rwkv_kernel(r_ref, k_ref, v_ref, wa_ref, g_ref,
                 mu_r_ref, mu_k_ref, mu_v_ref, mu_wa_ref, mu_g_ref,
                 w0_ref, wup_ref, a0_ref, aup_ref, gup_ref, kk_ref, ka_ref, rk_ref,
                 gng_ref, gnb_ref, hsum_ref, tri_ref,
                 o_ref, st_s, prev_r, prev_k, prev_v, prev_wa, prev_g, *, n_seq):
    c = RWKV_CHUNK
    mid = c // 2

    @pl.when(pl.program_id(1) == 0)
    def _():
        st_s[...] = jnp.zeros_like(st_s)
        prev_r[...] = jnp.zeros_like(prev_r)
        prev_k[...] = jnp.zeros_like(prev_k)
        prev_v[...] = jnp.zeros_like(prev_v)
        prev_wa[...] = jnp.zeros_like(prev_wa)
        prev_g[...] = jnp.zeros_like(prev_g)

    n_pairs = RWKV_HEADS // 2
    hsum = hsum_ref[...]
    tri = tri_ref[...]
    row = lax.broadcasted_iota(I32, (c, c), 0)
    col = lax.broadcasted_iota(I32, (c, c), 1)
    strict = row > col
    incl = row >= col
    lane_half = lax.broadcasted_iota(I32, (1, LANES), 1) // HEAD_DIM
    hmask = hsum.astype(F32)
    dot = functools.partial(jnp.dot, preferred_element_type=F32)
    lanes = lambda x, jp: x[:, jp * LANES:(jp + 1) * LANES]

    def group_sum(x):
        return jnp.concatenate([_dot_exact_rhs(lanes(x, jp), hsum, 2) for jp in range(n_pairs)], axis=1)

    def two_heads(x):
        return jnp.concatenate([jnp.where(lane_half == 0, x, 0.0), jnp.where(lane_half == 1, x, 0.0)],
                               axis=0).astype(BF16)

    def side(a, b):
        return jnp.concatenate([a, b], axis=1)

    def mix(g, x_ref, prev_ref, mu_ref):
        x = x_ref[g]
        rows = lax.broadcasted_iota(I32, x.shape, 0)
        shifted = jnp.where(rows == 0, prev_ref[g], pltpu.roll(x, 1, axis=0))
        prev_ref[g] = x[c - 1:c, :]
        return x + (shifted - x) * mu_ref[...]

    def prepare(g):
        r = mix(g, r_ref, prev_r, mu_r_ref)
        k = mix(g, k_ref, prev_k, mu_k_ref)
        v = mix(g, v_ref, prev_v, mu_v_ref)
        xwa = mix(g, wa_ref, prev_wa, mu_wa_ref)
        xg = mix(g, g_ref, prev_g, mu_g_ref)

        w = -_softplus(-(w0_ref[...] + _bdot(jnp.tanh(xwa), wup_ref[...]))) - 0.5
        logd = -jnp.exp(w)
        a = _sigmoid(a0_ref[...] + _bdot(xwa, aup_ref[...]))
        gate = _bdot(_sigmoid(xg), gup_ref[...])
        kk = k * kk_ref[...]
        kk = kk / jnp.maximum(jnp.sqrt(group_sum(kk * kk)), 1e-12)
        k2 = k * (1.0 + (a - 1.0) * ka_ref[...])

        parts = _split_bf16(logd, 3)
        cum = dot(tri, parts[0]) + dot(tri, parts[1]) + dot(tri, parts[2])
        cum_last = cum[c - 1:c, :]
        ref = cum[mid - 1:mid, :]
        rel = cum - ref
        g_in = jnp.exp(rel)
        g_inv = jnp.exp(-rel)
        g_tail = jnp.exp(cum_last - cum)
        a_t = -kk * jnp.exp(rel - logd)
        b_t = kk * a * g_inv
        k_t = k2 * g_inv
        r_t = r * g_in
        b_hat = kk * a * g_tail
        k_hat = k2 * g_tail

        return dict(r=r, k2=k2, v=v, gate=gate, b_hat=b_hat, k_hat=k_hat, ref_scale=jnp.exp(ref),
                    decay=jnp.exp(cum_last), bk=jnp.concatenate([b_t, k_t], axis=0).astype(BF16),
                    ar_all=jnp.concatenate([a_t, r_t], axis=0))

    def chunk_products(g, el):
        v, b_hat, k_hat, ref_scale, decay = el["v"], el["b_hat"], el["k_hat"], el["ref_scale"], el["decay"]
        bk, ar_all = el["bk"], el["ar_all"]
        per_pair = []
        for jp in range(n_pairs):
            st = st_s[g, jp]
            st_ref = (st * lanes(ref_scale, jp)).astype(BF16)
            both = _dot_nt(lanes(ar_all, jp).astype(BF16), st_ref)
            quads = []
            for half in range(2):
                ar = jnp.where(lane_half == half, lanes(ar_all, jp), 0.0)
                m4 = _dot_nt(ar.astype(BF16), lanes(bk, jp))
                quads.append((jnp.where(strict, m4[:c, :c], 0.0),
                              jnp.where(strict, m4[:c, c:], 0.0).astype(BF16),
                              jnp.where(incl, m4[c:, :c], 0.0).astype(BF16),
                              jnp.where(incl, m4[c:, c:], 0.0).astype(BF16)))
            per_pair.append(dict(
                st=st, p0=both[:c], o0=both[c:], v_rows=two_heads(lanes(v, jp)),
                l_ab=[q[0] for q in quads], l_ak=side(quads[0][1], quads[1][1]),
                m_rb=side(quads[0][2], quads[1][2]), m_rk=side(quads[0][3], quads[1][3]),
                v=lanes(v, jp).astype(BF16), b_hat=lanes(b_hat, jp).astype(BF16),
                k_hat=lanes(k_hat, jp).astype(BF16), decay=lanes(decay, jp)))
        return dict(r=el["r"], k2=el["k2"], v=v, gate=el["gate"], pairs=per_pair)

    seqs = {}

    def solve(group):
        heads = [(g, jp, half) for g in group for jp in range(n_pairs) for half in range(2)]
        pair_ids = [(g, jp) for g in group for jp in range(n_pairs)]

        pw = {gh: seqs[gh[0]]["pairs"][gh[1]]["l_ab"][gh[2]].astype(BF16) for gh in heads}
        u_loc = {(g, jp): seqs[g]["pairs"][jp]["p0"]
                 + dot(seqs[g]["pairs"][jp]["l_ak"], seqs[g]["pairs"][jp]["v_rows"]) for g, jp in pair_ids}
        span = 1
        while span < c:
            u_loc = {(g, jp): u_loc[(g, jp)]
                     + dot(side(pw[(g, jp, 0)], pw[(g, jp, 1)]), two_heads(u_loc[(g, jp)]))
                     for g, jp in pair_ids}
            span *= 2
            if span < c:
                pw = {gh: dot(pw[gh], pw[gh]).astype(BF16) for gh in heads}

        for g in group:
            sq = seqs[g]
            outs = []
            for jp in range(n_pairs):
                pr = sq["pairs"][jp]
                u = u_loc[(g, jp)]
                outs.append(pr["o0"] + dot(pr["m_rb"], two_heads(u)) + dot(pr["m_rk"], pr["v_rows"]))
                upd = _dot_tn(u.astype(BF16), pr["b_hat"]) + _dot_tn(pr["v"], pr["k_hat"])
                st_s[g, jp] = pr["st"] * pr["decay"] + hmask * upd
            o = jnp.concatenate(outs, axis=1)

            inv_n = 1.0 / HEAD_DIM
            mean = group_sum(o) * inv_n
            dev = o - mean
            var = group_sum(dev * dev) * inv_n
            y = dev * lax.rsqrt(var + RWKV_GN_EPS) * gng_ref[...] + gnb_ref[...]
            y = y + group_sum(sq["r"] * sq["k2"] * rk_ref[...]) * sq["v"]
            o_ref[g] = y * sq["gate"]

    elementwise = None
    for g in range(n_seq):
        nxt = prepare(g)
        if elementwise is not None:
            seqs[g - 1] = chunk_products(g - 1, elementwise)
        elementwise = nxt
    seqs[n_seq - 1] = chunk_products(n_seq - 1, elementwise)
    solve(list(range(n_seq)))


def _rwkv(p3, mu, w0, w_up, a0, a_up, g_up, k_k, k_a, r_k, gn_g, gn_b):
    b, s, _ = p3.shape
    c = RWKV_CHUNK
    w = RWKV_W
    n_seq = RWKV_SEQS_PER_STEP if b % RWKV_SEQS_PER_STEP == 0 else 1
    base = 3 * MOBA_W
    assert base % w == 0 and (base + 3 * w) % LANES == 0
    lora = DECAY_LORA + AAA_LORA

    def col_spec(width, off_cols):
        assert off_cols % width == 0
        return pl.BlockSpec((n_seq, c, width), lambda bi, ci: (bi, ci, off_cols // width))

    row = lambda x: x.reshape(1, -1).astype(F32)
    const = lambda shape: pl.BlockSpec(shape, lambda bi, ci: (0,) * len(shape))
    wup_pad = jnp.concatenate([w_up, jnp.zeros((AAA_LORA, w), F32)], axis=0).astype(BF16)
    aup_pad = jnp.concatenate([jnp.zeros((DECAY_LORA, w), F32), a_up], axis=0).astype(BF16)
    head = np.arange(LANES) // HEAD_DIM
    hsum = jnp.asarray((head[:, None] == head[None, :]).astype(np.float32)).astype(BF16)
    tri = jnp.asarray(np.tril(np.ones((c, c), np.float32))).astype(BF16)
    vec_args = [mu[:w], mu[w:2 * w], mu[2 * w:3 * w], mu[3 * w:3 * w + lora], mu[3 * w + lora:],
                w0, None, a0, None, None, k_k, k_a, r_k.reshape(-1), gn_g, gn_b]
    args = [p3, p3, p3, p3, p3]
    specs = [col_spec(w, base), col_spec(w, base + w), col_spec(w, base + 2 * w),
             col_spec(lora, base + 3 * w), col_spec(GATE_LORA, base + 3 * w + lora)]
    mats = {6: wup_pad, 8: aup_pad, 9: g_up.astype(BF16)}
    for idx, a in enumerate(vec_args):
        arr = mats[idx] if a is None else row(a)
        args.append(arr)
        specs.append(const(arr.shape))
    for arr in (hsum, tri):
        args.append(arr)
        specs.append(const(arr.shape))
    return pl.pallas_call(
        functools.partial(_rwkv_kernel, n_seq=n_seq),
        grid=(b // n_seq, s // c),
        in_specs=specs,
        out_specs=pl.BlockSpec((n_seq, c, w), lambda bi, ci: (bi, ci, 0)),
        out_shape=jax.ShapeDtypeStruct((b, s, w), F32),
        scratch_shapes=[pltpu.VMEM((n_seq, RWKV_HEADS // 2, LANES, LANES), F32),
                        pltpu.VMEM((n_seq, 1, w), F32),
                        pltpu.VMEM((n_seq, 1, w), F32), pltpu.VMEM((n_seq, 1, w), F32),
                        pltpu.VMEM((n_seq, 1, lora), F32), pltpu.VMEM((n_seq, 1, GATE_LORA), F32)],
        compiler_params=_cparams(("parallel", "arbitrary")),
        name="rwkv",
    )(*args)


def _mem_attn_kernel(q_ref, kv_ref, o_ref):
    scale = HEAD_DIM ** -0.5
    lane_head = lax.broadcasted_iota(I32, (1, MEM_W), 1) // HEAD_DIM
    q = q_ref[0] * scale
    mk = kv_ref[0, :, :MEM_W].astype(BF16)
    mv = kv_ref[0, :, MEM_W:].astype(BF16)
    out = jnp.zeros(q.shape, F32)
    for h in range(MEM_HEADS):
        hm = lane_head == h
        s = _dot_nt(jnp.where(hm, q, 0.0).astype(BF16), mk)
        s = s - jnp.max(s, axis=-1, keepdims=True)
        e = jnp.exp(s)
        p = e / jnp.sum(e, axis=-1, keepdims=True)
        out = jnp.where(hm, jnp.dot(p.astype(BF16), mv, preferred_element_type=F32), out)
    o_ref[0] = out


def _mem_attention(p3, mkv3, tq):
    b, s, _ = p3.shape
    m = mkv3.shape[1]
    off = (3 * MOBA_W + RWKV_COLS) // MEM_W
    assert off * MEM_W == 3 * MOBA_W + RWKV_COLS
    return pl.pallas_call(
        _mem_attn_kernel,
        grid=(b, s // tq),
        in_specs=[pl.BlockSpec((1, tq, MEM_W), lambda bi, i: (bi, i, off)),
                  pl.BlockSpec((1, m, 2 * MEM_W), lambda bi, i: (bi, 0, 0))],
        out_specs=pl.BlockSpec((1, tq, MEM_W), lambda bi, i: (bi, i, 0)),
        out_shape=jax.ShapeDtypeStruct((b, s, MEM_W), F32),
        compiler_params=_cparams(("parallel", "parallel")),
        name="mem_attn",
    )(p3, mkv3)


def _out_router_kernel(x_ref, ym_ref, yr_ref, ye_ref, wo1_ref, wo2_ref, wo3_ref, g_ref,
                       wrh_ref, wrl_ref, br_ref, upper_ref, ones_ref,
                       x1_ref, h_ref, idx_ref, rank_ref, wgt_ref, cnt_ref, run_s):
    @pl.when(pl.program_id(0) == 0)
    def _():
        run_s[...] = jnp.zeros_like(run_s)

    x1 = (x_ref[...] + _bdot(ym_ref[...], wo1_ref[...]) + _bdot(yr_ref[...], wo2_ref[...])
          + _bdot(ye_ref[...], wo3_ref[...]))
    x1_ref[...] = x1
    h = _rms(x1, g_ref[...])
    h_ref[...] = _pack_halves(h)
    tm = h.shape[0]
    h_hi, h_lo = _split_bf16(h, 2)
    dot = functools.partial(jnp.dot, preferred_element_type=F32)
    logits = dot(h_hi, wrh_ref[...]) + (dot(h_hi, wrl_ref[...]) + dot(h_lo, wrh_ref[...])) + br_ref[...]
    lg = logits.T[:N_EXPERTS, :]
    e_iota = lax.broadcasted_iota(I32, (N_EXPERTS, tm), 0)

    vals, idxs = [], []
    for _ in range(TOP_K):
        m = jnp.max(lg, axis=0, keepdims=True)
        idx = jnp.min(jnp.where(lg == m, e_iota, N_EXPERTS), axis=0, keepdims=True)
        vals.append(m)
        idxs.append(idx)
        lg = jnp.where(e_iota == idx, NEG_INF, lg)
    exps = [jnp.exp(vk - vals[0]) for vk in vals]
    denom = exps[0] + exps[1] + exps[2] + exps[3]

    chosen = jnp.zeros((N_EXPERTS, tm), F32)
    for idx in idxs:
        chosen = chosen + jnp.where(e_iota == idx, 1.0, 0.0)
    chosen = chosen.astype(BF16)
    run = run_s[...]
    before = dot(chosen, upper_ref[...]) + jnp.concatenate([run] * (tm // LANES), axis=1)
    run_s[...] = run + dot(chosen, ones_ref[...])
    cnt_ref[...] = run_s[...]

    zero_i = jnp.zeros((8 - TOP_K, tm), I32)
    ranks = [jnp.sum(jnp.where(e_iota == idx, before, 0.0), axis=0, keepdims=True).astype(I32)
             for idx in idxs]
    idx_ref[0] = jnp.concatenate(idxs + [zero_i], axis=0)
    rank_ref[0] = jnp.concatenate(ranks + [zero_i], axis=0)
    wrows = jnp.concatenate([e / denom for e in exps] + [jnp.zeros((LANES - TOP_K, tm), F32)], axis=0)
    wgt_ref[...] = wrows.T


def _out_router(x2, ym, yr, ye, w_out, g_ffn, w_router, b_router, tm):
    t, d = x2.shape
    wo = w_out.astype(BF16)
    wo1, wo2, wo3 = wo[:MOBA_W], wo[MOBA_W:MOBA_W + RWKV_W], wo[MOBA_W + RWKV_W:]
    wr = jnp.zeros((d, LANES), F32).at[:, :N_EXPERTS].set(w_router)
    wr_hi = wr.astype(BF16)
    wr_lo = (wr - wr_hi.astype(F32)).astype(BF16)
    br = jnp.full((1, LANES), NEG_INF, F32).at[0, :N_EXPERTS].set(b_router)
    upper = jnp.asarray(np.triu(np.ones((tm, tm), np.float32), 1)).astype(BF16)
    ones = jnp.ones((tm, LANES), BF16)
    tile = lambda n: pl.BlockSpec((tm, n), lambda i: (i, 0))
    slots = pl.BlockSpec((1, 8, tm), lambda i: (i, 0, 0))
    const = lambda a: pl.BlockSpec(a.shape, lambda i: (0,) * a.ndim)
    g2 = g_ffn.reshape(1, d)
    n = t // tm
    return pl.pallas_call(
        _out_router_kernel,
        grid=(n,),
        in_specs=[tile(d), tile(MOBA_W), tile(RWKV_W), tile(MEM_W), const(wo1), const(wo2), const(wo3),
                  const(g2), const(wr_hi), const(wr_lo), const(br), const(upper), const(ones)],
        out_specs=[tile(d), tile(d // 2), slots, slots, tile(LANES),
                   pl.BlockSpec((N_EXPERTS, LANES), lambda i: (0, 0))],
        out_shape=[jax.ShapeDtypeStruct((t, d), F32), jax.ShapeDtypeStruct((t, d // 2), I32),
                   jax.ShapeDtypeStruct((n, 8, tm), I32), jax.ShapeDtypeStruct((n, 8, tm), I32),
                   jax.ShapeDtypeStruct((t, LANES), F32), jax.ShapeDtypeStruct((N_EXPERTS, LANES), F32)],
        scratch_shapes=[pltpu.VMEM((N_EXPERTS, LANES), F32)],
        compiler_params=_cparams(("arbitrary",)),
        name="out_router",
    )(x2, ym, yr, ye, wo1, wo2, wo3, g2, wr_hi, wr_lo, br, upper, ones)


def _sc_mesh():
    return plsc.VectorSubcoreMesh(core_axis_name="c", subcore_axis_name="s",
                                  num_cores=SC_CORES, num_subcores=SC_SUBCORES)


def _sc_split(n_rows):
    n_workers = SC_CORES * SC_SUBCORES
    per_worker = n_rows // n_workers
    assert per_worker * n_workers == n_rows and per_worker % (SC_ROWS * SC_INFLIGHT) == 0
    assert SC_ROWS % 8 == 0
    return per_worker


def _sc_gather_rows(table, idx):
    n_rows = idx.shape[0]
    width = table.shape[1]
    per_worker = _sc_split(n_rows)

    buf = lambda shape, dtype: [pltpu.VMEM(shape, dtype) for _ in range(SC_INFLIGHT)]

    @functools.partial(
        pl.kernel, mesh=_sc_mesh(),
        out_type=jax.ShapeDtypeStruct((n_rows, width), table.dtype),
        scratch_types=[buf((SC_ROWS,), I32), buf((SC_ROWS, width), table.dtype),
                       [pltpu.SemaphoreType.DMA for _ in range(SC_INFLIGHT)]],
        name="sc_gather",
    )
    def gather(table_hbm, idx_hbm, out_hbm, idx_v, rows_v, sems):
        worker = lax.axis_index("s") * SC_CORES + lax.axis_index("c")
        base = worker * per_worker

        @pl.loop(0, per_worker // (SC_ROWS * SC_INFLIGHT))
        def _(step):
            offs = [pl.multiple_of(base + (step * SC_INFLIGHT + b) * SC_ROWS, 8) for b in range(SC_INFLIGHT)]
            copies = []
            for b in range(SC_INFLIGHT):
                pltpu.sync_copy(idx_hbm.at[pl.ds(offs[b], SC_ROWS)], idx_v[b])
                copies.append(pltpu.async_copy(table_hbm.at[idx_v[b]], rows_v[b], sems[b]))
            for b in range(SC_INFLIGHT):
                copies[b].wait()
                pltpu.sync_copy(rows_v[b], out_hbm.at[pl.ds(offs[b], SC_ROWS)])

    return gather(table, idx)


def _sc_scatter_rows(rows, idx, n_out):
    n_src, width = rows.shape
    n_copies = idx.shape[0] // n_src
    assert n_copies * n_src == idx.shape[0]
    per_worker = _sc_split(n_src)

    @functools.partial(
        pl.kernel, mesh=_sc_mesh(),
        out_type=jax.ShapeDtypeStruct((n_out, width), rows.dtype),
        scratch_types=[[pltpu.VMEM((SC_ROWS,), I32) for _ in range(n_copies)],
                       pltpu.VMEM((SC_ROWS, width), rows.dtype),
                       [pltpu.SemaphoreType.DMA for _ in range(n_copies)]],
        name="sc_scatter",
    )
    def scatter(rows_hbm, idx_hbm, out_hbm, idx_v, rows_v, sems):
        worker = lax.axis_index("s") * SC_CORES + lax.axis_index("c")
        base = worker * per_worker

        @pl.loop(0, per_worker // SC_ROWS)
        def _(step):
            src = pl.multiple_of(base + step * SC_ROWS, 8)
            pltpu.sync_copy(rows_hbm.at[pl.ds(src, SC_ROWS)], rows_v)
            copies = []
            for cpy in range(n_copies):
                off = pl.multiple_of(cpy * n_src + src, 8)
                pltpu.sync_copy(idx_hbm.at[pl.ds(off, SC_ROWS)], idx_v[cpy])
                copies.append(pltpu.async_copy(rows_v, out_hbm.at[idx_v[cpy]], sems[cpy]))
            for cp in copies:
                cp.wait()

    return scatter(rows, idx)


def _experts_kernel(be_ref, bv_ref, nused_ref, next_ref, slot_ref,
                    xs_ref, wgu_hbm, bg_ref, bu_ref, wd_hbm, bd_ref, perm_ref,
                    o_ref, wgu_buf, wd_buf, wg_s, wu_s, wd_s, sem):
    i = pl.program_id(0)
    e = be_ref[i]
    used = i < nused_ref[0]
    changed = ((i == 0) | (e != be_ref[jnp.maximum(i - 1, 0)])) & used
    slot = slot_ref[e]

    def weight_copies(expert, buf_slot):
        return (pltpu.make_async_copy(wgu_hbm.at[expert], wgu_buf.at[buf_slot], sem.at[buf_slot, 0]),
                pltpu.make_async_copy(wd_hbm.at[expert], wd_buf.at[buf_slot], sem.at[buf_slot, 1]))

    @pl.when((i == 0) & used)
    def _():
        for cp in weight_copies(e, slot):
            cp.start()

    @pl.when(changed)
    def _():
        for cp in weight_copies(e, slot):
            cp.wait()
        nxt = next_ref[e]

        @pl.when(nxt >= 0)
        def _():
            for cp in weight_copies(nxt, 1 - slot):
                cp.start()

        half = LANES
        for cblk in range(2 * D_EXPERT // (2 * half)):
            wt = wgu_buf[slot, :, cblk * 2 * half:(cblk + 1) * 2 * half].astype(BF16)
            sep = jnp.dot(wt, perm_ref[...], preferred_element_type=F32).astype(BF16)
            wg_s[:, cblk * half:(cblk + 1) * half] = sep[:, :half]
            wu_s[:, cblk * half:(cblk + 1) * half] = sep[:, half:]
        wd_s[...] = wd_buf[slot].astype(BF16)

    @pl.when(used)
    def _():
        rows = lax.broadcasted_iota(I32, xs_ref.shape, 0)
        xb = _unpack_halves(jnp.where(rows < bv_ref[i], xs_ref[...], 0)).astype(BF16)
        gate = jnp.dot(xb, wg_s[...], preferred_element_type=F32) + bg_ref[0]
        up = jnp.dot(xb, wu_s[...], preferred_element_type=F32) + bu_ref[0]
        gate = jnp.minimum(gate, SWIGLU_LIMIT)
        up = jnp.clip(up, -SWIGLU_LIMIT, SWIGLU_LIMIT)
        glu = gate * _sigmoid(gate * SWIGLU_ALPHA)
        act = ((up + 1.0) * glu).astype(BF16)
        o_ref[...] = _pack_halves(jnp.dot(act, wd_s[...], preferred_element_type=F32) + bd_ref[0])

    @pl.when(jnp.logical_not(used))
    def _():
        o_ref[...] = jnp.zeros_like(o_ref)


def _experts(blk_expert, blk_valid, n_used, next_expert, buf_slot, xs, w_gate_up, b_gate_up, w_down,
             b_down):
    p_rows = xs.shape[0]
    d = 2 * xs.shape[1]
    n_blocks = p_rows // EXPERT_ROWS
    bg = b_gate_up[:, 0::2].reshape(N_EXPERTS, 1, D_EXPERT)
    bu = b_gate_up[:, 1::2].reshape(N_EXPERTS, 1, D_EXPERT)
    bd = b_down.reshape(N_EXPERTS, 1, d)
    perm_np = np.zeros((2 * LANES, 2 * LANES), np.float32)
    perm_np[2 * np.arange(LANES), np.arange(LANES)] = 1.0
    perm_np[2 * np.arange(LANES) + 1, LANES + np.arange(LANES)] = 1.0
    perm = jnp.asarray(perm_np).astype(BF16)
    by_expert = lambda shape: pl.BlockSpec((1,) + shape, lambda i, be, *_: (be[i], 0, 0))
    grid_spec = pltpu.PrefetchScalarGridSpec(
        num_scalar_prefetch=5,
        grid=(n_blocks,),
        in_specs=[pl.BlockSpec((EXPERT_ROWS, d // 2), lambda i, *_: (i, 0)),
                  pl.BlockSpec(memory_space=pl.ANY), by_expert((1, D_EXPERT)), by_expert((1, D_EXPERT)),
                  pl.BlockSpec(memory_space=pl.ANY), by_expert((1, d)),
                  pl.BlockSpec(perm.shape, lambda i, *_: (0, 0))],
        out_specs=pl.BlockSpec((EXPERT_ROWS, d // 2), lambda i, *_: (i, 0)),
        scratch_shapes=[pltpu.VMEM((2, d, 2 * D_EXPERT), F32), pltpu.VMEM((2, D_EXPERT, d), F32),
                        pltpu.VMEM((d, D_EXPERT), BF16), pltpu.VMEM((d, D_EXPERT), BF16),
                        pltpu.VMEM((D_EXPERT, d), BF16), pltpu.SemaphoreType.DMA((2, 2))],
    )
    return pl.pallas_call(
        _experts_kernel,
        grid_spec=grid_spec,
        out_shape=jax.ShapeDtypeStruct((p_rows, d // 2), I32),
        compiler_params=_cparams(("arbitrary",)),
        name="experts",
    )(blk_expert, blk_valid, n_used, next_expert, buf_slot, xs, w_gate_up, bg, bu, w_down, bd, perm)


def _combine_kernel(yg_ref, x1_ref, w_ref, g_ref, *rest):
    o_ref = rest[-1]
    acc = x1_ref[...]
    wts = w_ref[...]
    for kk in range(TOP_K):
        acc = acc + _unpack_halves(yg_ref[kk]) * wts[:, kk:kk + 1]
    o_ref[...] = _rms(acc, g_ref[...])


def _combine(yg, x1, wts, g_final, tb, first_tile, partial_out):
    t, d = x1.shape
    half = yg.shape[2]
    tile = lambda n: pl.BlockSpec((tb, n), lambda i: (i + first_tile, 0))
    in_specs = [pl.BlockSpec((TOP_K, tb, half), lambda i: (0, i, 0)), tile(d), tile(LANES),
                pl.BlockSpec((1, d), lambda i: (0, 0))]
    args = [yg, x1, wts, g_final.reshape(1, d)]
    aliases = {}
    if partial_out is not None:
        in_specs.append(pl.BlockSpec(memory_space=pl.ANY))
        args.append(partial_out)
        aliases = {len(args) - 1: 0}
    return pl.pallas_call(
        _combine_kernel,
        grid=(yg.shape[1] // tb,),
        in_specs=in_specs,
        out_specs=tile(d),
        out_shape=jax.ShapeDtypeStruct((t, d), F32),
        input_output_aliases=aliases,
        compiler_params=_cparams(("parallel",)),
        name="combine",
    )(*args)


def _layer(x, mem, w_in, w_out, w_mem_kv, g_mix, g_mem, g_ffn, bias_tables, mu, w0, w_up, a0, a_up,
           g_up, k_k, k_a, r_k, gn_g, gn_b, w_router, b_router, w_gate_up, b_gate_up, w_down, b_down,
           g_last):
    b, s, d = x.shape
    m = mem.shape[1]
    t = b * s
    x2 = x.reshape(t, d)

    p = _norm_matmul(x2, g_mix, w_in.astype(BF16), 512, "in_proj")
    p3 = p.reshape(b, s, IN_COLS)
    y_moba = _moba_attention(p3, *bias_tables)
    y_rwkv = _rwkv(p3, mu, w0, w_up, a0, a_up, g_up, k_k, k_a, r_k, gn_g, gn_b)
    mkv = _norm_matmul(mem.reshape(b * m, d), g_mem, w_mem_kv.astype(BF16), 512, "mem_kv")
    y_mem = _mem_attention(p3, mkv.reshape(b, m, 2 * MEM_W), 512)

    tb = ROUTE_TOKENS
    x1, h2, idx_o, rank_o, wgt_p, cnt = _out_router(
        x2, y_moba.reshape(t, MOBA_W), y_rwkv.reshape(t, RWKV_W), y_mem.reshape(t, MEM_W),
        w_out, g_ffn, w_router, b_router, tb)

    counts = cnt[:, 0].astype(I32)
    padded = (counts + EXPERT_ROWS - 1) // EXPERT_ROWS * EXPERT_ROWS
    pad_ends = jnp.cumsum(padded)
    pad_starts = (pad_ends - padded).astype(I32)
    n_blocks = (t * TOP_K) // EXPERT_ROWS + N_EXPERTS
    blk_start = jnp.arange(n_blocks, dtype=I32) * EXPERT_ROWS
    blk_expert = jnp.minimum(jnp.sum(blk_start[:, None] >= pad_ends[None, :], axis=1),
                             N_EXPERTS - 1).astype(I32)
    of_block = blk_expert[:, None] == jnp.arange(N_EXPERTS, dtype=I32)[None, :]
    rows_left = jnp.sum(jnp.where(of_block, (counts + pad_starts)[None, :], 0), axis=1) - blk_start
    blk_valid = jnp.clip(rows_left, 0, EXPERT_ROWS)
    n_used = (pad_ends[-1:] // EXPERT_ROWS).astype(I32)
    has_rows = counts > 0
    first_from = lax.cummin(jnp.where(has_rows, jnp.arange(N_EXPERTS, dtype=I32), N_EXPERTS), reverse=True)
    next_expert = jnp.concatenate([first_from[1:], jnp.full((1,), N_EXPERTS, I32)])
    next_expert = jnp.where(next_expert < N_EXPERTS, next_expert, -1).astype(I32)
    buf_slot = ((jnp.cumsum(has_rows.astype(I32)) - has_rows.astype(I32)) % 2).astype(I32)
    idx_kt = jnp.swapaxes(idx_o[:, :TOP_K, :], 0, 1).reshape(TOP_K, t)
    rank_kt = jnp.swapaxes(rank_o[:, :TOP_K, :], 0, 1).reshape(TOP_K, t)
    experts = jnp.arange(N_EXPERTS, dtype=I32)[:, None, None]
    start_kt = jnp.sum(jnp.where(idx_kt[None] == experts, pad_starts[:, None, None], 0), axis=0)
    dest = (start_kt + rank_kt).reshape(TOP_K * t)

    xs = _sc_scatter_rows(h2, dest, n_blocks * EXPERT_ROWS)
    ys = _experts(blk_expert, blk_valid.astype(I32), n_used, next_expert, buf_slot, xs,
                  w_gate_up, b_gate_up, w_down, b_down)
    dest_kt = dest.reshape(TOP_K, t)
    out = None
    for part in range(COMBINE_PARTS):
        t_part = t // COMBINE_PARTS
        rows = dest_kt[:, part * t_part:(part + 1) * t_part].reshape(TOP_K * t_part)
        yg = _sc_gather_rows(ys, rows).reshape(TOP_K, t_part, d // 2)
        out = _combine(yg, x1, wgt_p, g_last, tb, part * (t_part // tb), out)
    return out.reshape(b, s, d)


def kernel(x, mem, w_in, w_out, w_mem_kv, g_mix, g_mem, g_ffn, g_final, rel_bias, rwkv_mu, rwkv_w0,
           rwkv_w_up, rwkv_a0, rwkv_a_up, rwkv_g_up, rwkv_k_k, rwkv_k_a, rwkv_r_k, rwkv_gn_g, rwkv_gn_b,
           w_router, b_router, w_gate_up, b_gate_up, w_down, b_down):
    depth = w_in.shape[0]
    assert depth == 1, "the final norm is fused into the last layer's combine kernel"
    bias_tables = _moba_bias_tables(rel_bias)
    l = 0
    return _layer(x, mem, w_in[l], w_out[l], w_mem_kv[l], g_mix[l], g_mem[l], g_ffn[l], bias_tables,
                  rwkv_mu[l], rwkv_w0[l], rwkv_w_up[l], rwkv_a0[l], rwkv_a_up[l], rwkv_g_up[l],
                  rwkv_k_k[l], rwkv_k_a[l], rwkv_r_k[l], rwkv_gn_g[l], rwkv_gn_b[l], w_router[l],
                  b_router[l], w_gate_up[l], b_gate_up[l], w_down[l], b_down[l], g_final)
```

```python
import functools
import math

import numpy as np
import jax
import jax.numpy as jnp
from jax import lax
from jax.experimental import pallas as pl
from jax.experimental.pallas import tpu as pltpu
from jax.experimental.pallas import tpu_sc as plsc

F32 = jnp.float32
BF16 = jnp.bfloat16
I32 = jnp.int32
HI = lax.Precision.HIGHEST

D_MODEL = 1024
HEAD_DIM = 64
MOBA_HEADS = 6
RWKV_HEADS = 6
MEM_HEADS = 4
MOBA_W = MOBA_HEADS * HEAD_DIM
RWKV_W = RWKV_HEADS * HEAD_DIM
MEM_W = MEM_HEADS * HEAD_DIM
MOBA_BLOCK = 256
MOBA_TOPK = 3
N_BUCKETS = 32
MAX_DISTANCE = 128
DECAY_LORA = 64
AAA_LORA = 64
GATE_LORA = 128
RWKV_COLS = 3 * RWKV_W + DECAY_LORA + AAA_LORA + GATE_LORA
RWKV_GN_EPS = 64e-5
IN_COLS = 3 * MOBA_W + RWKV_COLS + MEM_W
N_EXPERTS = 32
TOP_K = 4
D_EXPERT = D_MODEL
SWIGLU_ALPHA = 1.702
SWIGLU_LIMIT = 7.0
RMS_EPS = 1e-5

LANES = 128
SUBLANES = 8
BF16_SUBLANES = 16
PROJ_TOKENS = 512
MEM_Q_TOKENS = 512
RWKV_CHUNK = 128
RWKV_SEQS_PER_STEP = 4
EXPERT_ROWS = 256
ROUTE_TOKENS = 512
SC_CORES = 2
SC_SUBCORES = 16
SC_ROWS = 64
SC_ALIGN = 8
SC_INFLIGHT = 2
COMBINE_PARTS = 4
VMEM_LIMIT = 56 * 1024 * 1024
NEG_INF = float("-inf")
HIGH_HALF = -65536
LOG2_E = 1.4426950408889634
MOBA_VROWS = HEAD_DIM + BF16_SUBLANES


def _cparams(sem):
    return pltpu.CompilerParams(dimension_semantics=sem, vmem_limit_bytes=VMEM_LIMIT)


def _rms(x, g):
    return x * lax.rsqrt(jnp.mean(x * x, axis=-1, keepdims=True) + RMS_EPS) * g


def _bdot(a, b):
    return jnp.dot(a.astype(BF16), b.astype(BF16), preferred_element_type=F32)


def _hdot(a, b):
    return jnp.dot(a, b, preferred_element_type=F32, precision=HI)


def _dot_nt(a, b):
    return lax.dot_general(a, b, (((1,), (1,)), ((), ())), preferred_element_type=F32)


def _dot_tn(a, b):
    return lax.dot_general(a, b, (((0,), (0,)), ((), ())), preferred_element_type=F32)


def _split_bf16(x, terms):
    parts = []
    for _ in range(terms):
        hi = x.astype(BF16)
        parts.append(hi)
        x = x - hi.astype(F32)
    return parts


def _dot_exact_rhs(x, m_bf16, terms):
    acc = None
    for part in _split_bf16(x, terms):
        d = jnp.dot(part, m_bf16, preferred_element_type=F32)
        acc = d if acc is None else acc + d
    return acc


def _pack_halves(x):
    n = x.shape[1] // 2
    lo = pltpu.bitcast(x[:, :n].astype(BF16).astype(F32), I32)
    hi = pltpu.bitcast(x[:, n:].astype(BF16).astype(F32), I32)
    return (hi & HIGH_HALF) | lax.shift_right_logical(lo, 16)


def _unpack_halves(w):
    lo = pltpu.bitcast(w << 16, F32)
    hi = pltpu.bitcast(w & HIGH_HALF, F32)
    return jnp.concatenate([lo, hi], axis=1)


def _norm_matmul_kernel(x_ref, g_ref, w_ref, o_ref):
    h = _rms(x_ref[...], g_ref[...])
    o_ref[...] = jnp.dot(h.astype(BF16), w_ref[...], preferred_element_type=F32)


def _norm_matmul(x, g, w_bf16, tm, name):
    t, d = x.shape
    n = w_bf16.shape[1]
    tm = min(tm, t)
    return pl.pallas_call(
        _norm_matmul_kernel,
        grid=(t // tm,),
        in_specs=[pl.BlockSpec((tm, d), lambda i: (i, 0)),
                  pl.BlockSpec((1, d), lambda i: (0, 0)),
                  pl.BlockSpec((d, n), lambda i: (0, 0))],
        out_specs=pl.BlockSpec((tm, n), lambda i: (i, 0)),
        out_shape=jax.ShapeDtypeStruct((t, n), F32),
        compiler_params=_cparams(("parallel",)),
        name=name,
    )(x, g.reshape(1, d), w_bf16)


def _t5_bucket_np(dist):
    n = np.maximum(dist, 0)
    max_exact = N_BUCKETS // 2
    nf = np.maximum(n, 1).astype(np.float64)
    large = max_exact + (np.log(nf / max_exact) / math.log(MAX_DISTANCE / max_exact)
                         * (N_BUCKETS - max_exact)).astype(np.int64)
    large = np.minimum(large, N_BUCKETS - 1)
    return np.where(n < max_exact, n, large)


def _moba_bias_tables(rel_bias):
    n = MOBA_BLOCK
    assert np.all(_t5_bucket_np(np.arange(n + 1, 64 * n)) == N_BUCKETS - 1)
    bias_t = rel_bias.astype(F32).T * LOG2_E

    def by_distance(dist):
        bucket = jnp.asarray(_t5_bucket_np(dist), I32)[None]
        tab = jnp.zeros((MOBA_HEADS, dist.shape[0]), F32)
        for b in range(N_BUCKETS):
            tab = jnp.where(bucket == b, bias_t[:, b][:, None], tab)
        return tab

    d = np.arange(2 * n)
    g_own = jnp.where(jnp.asarray(d < n)[None], by_distance(np.where(d < n, d, 0)), NEG_INF)
    g_prev = by_distance(np.where(d < n, n + d, d - n))
    far = bias_t[:, N_BUCKETS - 1]
    return g_own[:, None, :], g_prev[:, None, :], far


def _moba_kernel(far_ref, q_ref, k_ref, v_ref, g0_ref, g1_ref, o_ref,
                 qt_s, kb_s, vt_s, sc_s, t0_ref, t1_ref, *, nb):
    hp = pl.program_id(1)
    blk = MOBA_BLOCK
    scale = HEAD_DIM ** -0.5 * LOG2_E
    row_head = lax.broadcasted_iota(I32, (LANES, 1), 0) // HEAD_DIM
    lane_head = lax.broadcasted_iota(I32, (1, LANES), 1) // HEAD_DIM
    ones_rows = jnp.ones((MOBA_VROWS - HEAD_DIM, blk), BF16)

    for h in range(2):
        for g_ref, t_ref in ((g0_ref, t0_ref), (g1_ref, t1_ref)):
            rows = jnp.broadcast_to(g_ref[h], (blk, 2 * blk))
            t_ref[h] = pltpu.roll(rows, 0, 1, stride=1, stride_axis=0)[:, :blk]

    kmean_rows = []
    for j in range(nb):
        sl = pl.ds(j * blk, blk)
        qt_s[j] = (q_ref[0, sl, :] * scale).T
        vt = v_ref[0, sl, :].T.astype(BF16)
        for h in range(2):
            vt_s[j, h * MOBA_VROWS:h * MOBA_VROWS + HEAD_DIM, :] = vt[h * HEAD_DIM:(h + 1) * HEAD_DIM, :]
            vt_s[j, h * MOBA_VROWS + HEAD_DIM:(h + 1) * MOBA_VROWS, :] = ones_rows
        kj = k_ref[0, sl, :]
        kb_s[j] = kj.astype(BF16)
        kmean_rows.append(jnp.mean(kj, axis=0, keepdims=True))
    kmean = jnp.concatenate(kmean_rows, axis=0)
    km2 = jnp.concatenate([jnp.where(lane_head == 0, kmean, 0.0),
                           jnp.where(lane_head == 1, kmean, 0.0)], axis=0)
    blk_iota = lax.broadcasted_iota(I32, (nb, blk), 0)

    def score_pass(i):
        qt = qt_s[i]
        negs = [None, None]
        if i > MOBA_TOPK:
            gate2 = _hdot(km2, qt)
            for h in range(2):
                g = gate2[h * nb:(h + 1) * nb, :]
                cnt = jnp.zeros((nb, blk), I32)
                for m in range(i):
                    gm = g[m:m + 1, :]
                    beats = (gm > g) | ((gm == g) & (m < blk_iota))
                    cnt = cnt + jnp.where(beats, 1, 0)
                negs[h] = jnp.where(cnt < MOBA_TOPK, 0.0, NEG_INF)
        col_max = []
        for h in range(2):
            qt_h = jnp.where(row_head == h, qt, 0.0).astype(BF16)
            far_bias = far_ref[2 * hp + h]
            m_run = None
            for j in range(i + 1):
                s = jnp.dot(kb_s[j], qt_h, preferred_element_type=F32)
                if j == i:
                    s = s + t0_ref[h]
                else:
                    bias = t1_ref[h] if j == i - 1 else far_bias
                    if negs[h] is not None:
                        bias = bias + negs[h][j:j + 1, :]
                    s = s + bias
                sc_s[i % 2, h, j] = s
                cm = jnp.max(s, axis=0, keepdims=True)
                m_run = cm if m_run is None else jnp.maximum(m_run, cm)
            col_max.append(m_run)
        return col_max

    def prob_pass(i, col_max):
        outs = []
        for h in range(2):
            m_run = col_max[h]
            probs = jnp.concatenate([jnp.exp2(sc_s[i % 2, h, j] - m_run).astype(BF16)
                                     for j in range(i + 1)], axis=0)
            vals = jnp.concatenate([vt_s[j, h * MOBA_VROWS:(h + 1) * MOBA_VROWS, :]
                                    for j in range(i + 1)], axis=1)
            acc = jnp.dot(vals, probs, preferred_element_type=F32)
            outs.append(acc[:HEAD_DIM] / acc[HEAD_DIM:HEAD_DIM + 1])
        o_t = jnp.concatenate(outs, axis=0)
        o_ref[0, pl.ds(i * blk, blk), :] = o_t.T

    pending = None
    for i in range(nb):
        col_max = score_pass(i)
        if pending is not None:
            prob_pass(*pending)
        pending = (i, col_max)
    prob_pass(*pending)


def _moba_attention(p3, g_own, g_prev, far):
    b, s, _ = p3.shape
    nb = s // MOBA_BLOCK
    n_pairs = MOBA_HEADS // 2
    blk_spec = lambda off: pl.BlockSpec((1, s, LANES), lambda bi, hp: (bi, 0, off + hp))
    tab_spec = pl.BlockSpec((2, 1, 2 * MOBA_BLOCK), lambda bi, hp: (hp, 0, 0))
    return pl.pallas_call(
        functools.partial(_moba_kernel, nb=nb),
        grid=(b, n_pairs),
        in_specs=[pl.BlockSpec(memory_space=pltpu.SMEM),
                  blk_spec(0), blk_spec(n_pairs), blk_spec(2 * n_pairs), tab_spec, tab_spec],
        out_specs=pl.BlockSpec((1, s, LANES), lambda bi, hp: (bi, 0, hp)),
        out_shape=jax.ShapeDtypeStruct((b, s, MOBA_W), F32),
        scratch_shapes=[pltpu.VMEM((nb, LANES, MOBA_BLOCK), F32),
                        pltpu.VMEM((nb, MOBA_BLOCK, LANES), BF16),
                        pltpu.VMEM((nb, 2 * MOBA_VROWS, MOBA_BLOCK), BF16),
                        pltpu.VMEM((2, 2, nb, MOBA_BLOCK, MOBA_BLOCK), F32),
                        pltpu.VMEM((2, MOBA_BLOCK, MOBA_BLOCK), F32),
                        pltpu.VMEM((2, MOBA_BLOCK, MOBA_BLOCK), F32)],
        compiler_params=_cparams(("parallel", "parallel")),
        name="moba",
    )(far, p3, p3, p3, g_own, g_prev)


def _softplus(z):
    return jnp.maximum(z, 0.0) + jnp.log(1.0 + jnp.exp(-jnp.abs(z)))


def _sigmoid(z):
    return 1.0 / (1.0 + jnp.exp(-z))


def _rwkv_kernel(r_ref, k_ref, v_ref, wa_ref, g_ref,
                 mu_r_ref, mu_k_ref, mu_v_ref, mu_wa_ref, mu_g_ref,
                 w0_ref, wup_ref, a0_ref, aup_ref, gup_ref, kk_ref, ka_ref, rk_ref,
                 gng_ref, gnb_ref, hsum_ref, tri_ref,
                 o_ref, st_s, prev_r, prev_k, prev_v, prev_wa, prev_g, *, n_seq):
    c = RWKV_CHUNK
    mid = c // 2

    @pl.when(pl.program_id(1) == 0)
    def _():
        st_s[...] = jnp.zeros_like(st_s)
        prev_r[...] = jnp.zeros_like(prev_r)
        prev_k[...] = jnp.zeros_like(prev_k)
        prev_v[...] = jnp.zeros_like(prev_v)
        prev_wa[...] = jnp.zeros_like(prev_wa)
        prev_g[...] = jnp.zeros_like(prev_g)

    n_pairs = RWKV_HEADS // 2
    hsum = hsum_ref[...]
    tri = tri_ref[...]
    row = lax.broadcasted_iota(I32, (c, c), 0)
    col = lax.broadcasted_iota(I32, (c, c), 1)
    strict = row > col
    incl = row >= col
    lane_half = lax.broadcasted_iota(I32, (1, LANES), 1) // HEAD_DIM
    hmask = hsum.astype(F32)
    dot = functools.partial(jnp.dot, preferred_element_type=F32)
    lanes = lambda x, jp: x[:, jp * LANES:(jp + 1) * LANES]

    def group_sum(x):
        return jnp.concatenate([_dot_exact_rhs(lanes(x, jp), hsum, 2) for jp in range(n_pairs)], axis=1)

    def two_heads(x):
        return jnp.concatenate([jnp.where(lane_half == 0, x, 0.0), jnp.where(lane_half == 1, x, 0.0)],
                               axis=0).astype(BF16)

    def side(a, b):
        return jnp.concatenate([a, b], axis=1)

    def mix(g, x_ref, prev_ref, mu_ref):
        x = x_ref[g]
        rows = lax.broadcasted_iota(I32, x.shape, 0)
        shifted = jnp.where(rows == 0, prev_ref[g], pltpu.roll(x, 1, axis=0))
        prev_ref[g] = x[c - 1:c, :]
        return x + (shifted - x) * mu_ref[...]

    def prepare(g):
        r = mix(g, r_ref, prev_r, mu_r_ref)
        k = mix(g, k_ref, prev_k, mu_k_ref)
        v = mix(g, v_ref, prev_v, mu_v_ref)
        xwa = mix(g, wa_ref, prev_wa, mu_wa_ref)
        xg = mix(g, g_ref, prev_g, mu_g_ref)

        w = -_softplus(-(w0_ref[...] + _bdot(jnp.tanh(xwa), wup_ref[...]))) - 0.5
        logd = -jnp.exp(w)
        a = _sigmoid(a0_ref[...] + _bdot(xwa, aup_ref[...]))
        gate = _bdot(_sigmoid(xg), gup_ref[...])
        kk = k * kk_ref[...]
        kk = kk / jnp.maximum(jnp.sqrt(group_sum(kk * kk)), 1e-12)
        k2 = k * (1.0 + (a - 1.0) * ka_ref[...])

        parts = _split_bf16(logd, 3)
        cum = dot(tri, parts[0]) + dot(tri, parts[1]) + dot(tri, parts[2])
        cum_last = cum[c - 1:c, :]
        ref = cum[mid - 1:mid, :]
        rel = cum - ref
        g_in = jnp.exp(rel)
        g_inv = jnp.exp(-rel)
        g_tail = jnp.exp(cum_last - cum)
        a_t = -kk * jnp.exp(rel - logd)
        b_t = kk * a * g_inv
        k_t = k2 * g_inv
        r_t = r * g_in
        b_hat = kk * a * g_tail
        k_hat = k2 * g_tail

        return dict(r=r, k2=k2, v=v, gate=gate, b_hat=b_hat, k_hat=k_hat, ref_scale=jnp.exp(ref),
                    decay=jnp.exp(cum_last), bk=jnp.concatenate([b_t, k_t], axis=0).astype(BF16),
                    ar_all=jnp.concatenate([a_t, r_t], axis=0))

    def chunk_products(g, el):
        v, b_hat, k_hat, ref_scale, decay = el["v"], el["b_hat"], el["k_hat"], el["ref_scale"], el["decay"]
        bk, ar_all = el["bk"], el["ar_all"]
        per_pair = []
        for jp in range(n_pairs):
            st = st_s[g, jp]
            st_ref = (st * lanes(ref_scale, jp)).astype(BF16)
            both = _dot_nt(lanes(ar_all, jp).astype(BF16), st_ref)
            quads = []
            for half in range(2):
                ar = jnp.where(lane_half == half, lanes(ar_all, jp), 0.0)
                m4 = _dot_nt(ar.astype(BF16), lanes(bk, jp))
                quads.append((jnp.where(strict, m4[:c, :c], 0.0),
                              jnp.where(strict, m4[:c, c:], 0.0).astype(BF16),
                              jnp.where(incl, m4[c:, :c], 0.0).astype(BF16),
                              jnp.where(incl, m4[c:, c:], 0.0).astype(BF16)))
            per_pair.append(dict(
                st=st, p0=both[:c], o0=both[c:], v_rows=two_heads(lanes(v, jp)),
                l_ab=[q[0] for q in quads], l_ak=side(quads[0][1], quads[1][1]),
                m_rb=side(quads[0][2], quads[1][2]), m_rk=side(quads[0][3], quads[1][3]),
                v=lanes(v, jp).astype(BF16), b_hat=lanes(b_hat, jp).astype(BF16),
                k_hat=lanes(k_hat, jp).astype(BF16), decay=lanes(decay, jp)))
        return dict(r=el["r"], k2=el["k2"], v=v, gate=el["gate"], pairs=per_pair)

    seqs = {}

    def solve(group):
        heads = [(g, jp, half) for g in group for jp in range(n_pairs) for half in range(2)]
        pair_ids = [(g, jp) for g in group for jp in range(n_pairs)]

        pw = {gh: seqs[gh[0]]["pairs"][gh[1]]["l_ab"][gh[2]].astype(BF16) for gh in heads}
        u_loc = {(g, jp): seqs[g]["pairs"][jp]["p0"]
                 + dot(seqs[g]["pairs"][jp]["l_ak"], seqs[g]["pairs"][jp]["v_rows"]) for g, jp in pair_ids}
        span = 1
        while span < c:
            u_loc = {(g, jp): u_loc[(g, jp)]
                     + dot(side(pw[(g, jp, 0)], pw[(g, jp, 1)]), two_heads(u_loc[(g, jp)]))
                     for g, jp in pair_ids}
            span *= 2
            if span < c:
                pw = {gh: dot(pw[gh], pw[gh]).astype(BF16) for gh in heads}

        for g in group:
            sq = seqs[g]
            outs = []
            for jp in range(n_pairs):
                pr = sq["pairs"][jp]
                u = u_loc[(g, jp)]
                outs.append(pr["o0"] + dot(pr["m_rb"], two_heads(u)) + dot(pr["m_rk"], pr["v_rows"]))
                upd = _dot_tn(u.astype(BF16), pr["b_hat"]) + _dot_tn(pr["v"], pr["k_hat"])
                st_s[g, jp] = pr["st"] * pr["decay"] + hmask * upd
            o = jnp.concatenate(outs, axis=1)

            inv_n = 1.0 / HEAD_DIM
            mean = group_sum(o) * inv_n
            dev = o - mean
            var = group_sum(dev * dev) * inv_n
            y = dev * lax.rsqrt(var + RWKV_GN_EPS) * gng_ref[...] + gnb_ref[...]
            y = y + group_sum(sq["r"] * sq["k2"] * rk_ref[...]) * sq["v"]
            o_ref[g] = y * sq["gate"]

    elementwise = None
    for g in range(n_seq):
        nxt = prepare(g)
        if elementwise is not None:
            seqs[g - 1] = chunk_products(g - 1, elementwise)
        elementwise = nxt
    seqs[n_seq - 1] = chunk_products(n_seq - 1, elementwise)
    solve(list(range(n_seq)))


def _rwkv(p3, mu, w0, w_up, a0, a_up, g_up, k_k, k_a, r_k, gn_g, gn_b):
    b, s, _ = p3.shape
    c = RWKV_CHUNK
    w = RWKV_W
    n_seq = RWKV_SEQS_PER_STEP if b % RWKV_SEQS_PER_STEP == 0 else 1
    base = 3 * MOBA_W
    assert base % w == 0 and (base + 3 * w) % LANES == 0
    lora = DECAY_LORA + AAA_LORA

    def col_spec(width, off_cols):
        assert off_cols % width == 0
        return pl.BlockSpec((n_seq, c, width), lambda bi, ci: (bi, ci, off_cols // width))

    row = lambda x: x.reshape(1, -1).astype(F32)
    const = lambda shape: pl.BlockSpec(shape, lambda bi, ci: (0,) * len(shape))
    wup_pad = jnp.concatenate([w_up, jnp.zeros((AAA_LORA, w), F32)], axis=0).astype(BF16)
    aup_pad = jnp.concatenate([jnp.zeros((DECAY_LORA, w), F32), a_up], axis=0).astype(BF16)
    head = np.arange(LANES) // HEAD_DIM
    hsum = jnp.asarray((head[:, None] == head[None, :]).astype(np.float32)).astype(BF16)
    tri = jnp.asarray(np.tril(np.ones((c, c), np.float32))).astype(BF16)
    vec_args = [mu[:w], mu[w:2 * w], mu[2 * w:3 * w], mu[3 * w:3 * w + lora], mu[3 * w + lora:],
                w0, None, a0, None, None, k_k, k_a, r_k.reshape(-1), gn_g, gn_b]
    args = [p3, p3, p3, p3, p3]
    specs = [col_spec(w, base), col_spec(w, base + w), col_spec(w, base + 2 * w),
             col_spec(lora, base + 3 * w), col_spec(GATE_LORA, base + 3 * w + lora)]
    mats = {6: wup_pad, 8: aup_pad, 9: g_up.astype(BF16)}
    for idx, a in enumerate(vec_args):
        arr = mats[idx] if a is None else row(a)
        args.append(arr)
        specs.append(const(arr.shape))
    for arr in (hsum, tri):
        args.append(arr)
        specs.append(const(arr.shape))
    return pl.pallas_call(
        functools.partial(_rwkv_kernel, n_seq=n_seq),
        grid=(b // n_seq, s // c),
        in_specs=specs,
        out_specs=pl.BlockSpec((n_seq, c, w), lambda bi, ci: (bi, ci, 0)),
        out_shape=jax.ShapeDtypeStruct((b, s, w), F32),
        scratch_shapes=[pltpu.VMEM((n_seq, RWKV_HEADS // 2, LANES, LANES), F32),
                        pltpu.VMEM((n_seq, 1, w), F32),
                        pltpu.VMEM((n_seq, 1, w), F32), pltpu.VMEM((n_seq, 1, w), F32),
                        pltpu.VMEM((n_seq, 1, lora), F32), pltpu.VMEM((n_seq, 1, GATE_LORA), F32)],
        compiler_params=_cparams(("parallel", "arbitrary")),
        name="rwkv",
    )(*args)


def _mem_attn_kernel(q_ref, kv_ref, o_ref):
    scale = HEAD_DIM ** -0.5
    lane_head = lax.broadcasted_iota(I32, (1, MEM_W), 1) // HEAD_DIM
    q = q_ref[0] * scale
    mk = kv_ref[0, :, :MEM_W].astype(BF16)
    mv = kv_ref[0, :, MEM_W:].astype(BF16)
    out = jnp.zeros(q.shape, F32)
    for h in range(MEM_HEADS):
        hm = lane_head == h
        s = _dot_nt(jnp.where(hm, q, 0.0).astype(BF16), mk)
        s = s - jnp.max(s, axis=-1, keepdims=True)
        e = jnp.exp(s)
        p = e / jnp.sum(e, axis=-1, keepdims=True)
        out = jnp.where(hm, jnp.dot(p.astype(BF16), mv, preferred_element_type=F32), out)
    o_ref[0] = out


def _mem_attention(p3, mkv3, tq):
    b, s, _ = p3.shape
    m = mkv3.shape[1]
    off = (3 * MOBA_W + RWKV_COLS) // MEM_W
    assert off * MEM_W == 3 * MOBA_W + RWKV_COLS
    return pl.pallas_call(
        _mem_attn_kernel,
        grid=(b, s // tq),
        in_specs=[pl.BlockSpec((1, tq, MEM_W), lambda bi, i: (bi, i, off)),
                  pl.BlockSpec((1, m, 2 * MEM_W), lambda bi, i: (bi, 0, 0))],
        out_specs=pl.BlockSpec((1, tq, MEM_W), lambda bi, i: (bi, i, 0)),
        out_shape=jax.ShapeDtypeStruct((b, s, MEM_W), F32),
        compiler_params=_cparams(("parallel", "parallel")),
        name="mem_attn",
    )(p3, mkv3)


def _out_router_kernel(x_ref, ym_ref, yr_ref, ye_ref, wo1_ref, wo2_ref, wo3_ref, g_ref,
                       wrh_ref, wrl_ref, br_ref, upper_ref, ones_ref,
                       x1_ref, h_ref, idx_ref, rank_ref, wgt_ref, cnt_ref, run_s):
    @pl.when(pl.program_id(0) == 0)
    def _():
        run_s[...] = jnp.zeros_like(run_s)

    x1 = (x_ref[...] + _bdot(ym_ref[...], wo1_ref[...]) + _bdot(yr_ref[...], wo2_ref[...])
          + _bdot(ye_ref[...], wo3_ref[...]))
    x1_ref[...] = x1
    h = _rms(x1, g_ref[...])
    h_ref[...] = _pack_halves(h)
    tm = h.shape[0]
    h_hi, h_lo = _split_bf16(h, 2)
    dot = functools.partial(jnp.dot, preferred_element_type=F32)
    logits = dot(h_hi, wrh_ref[...]) + (dot(h_hi, wrl_ref[...]) + dot(h_lo, wrh_ref[...])) + br_ref[...]
    lg = logits.T[:N_EXPERTS, :]
    e_iota = lax.broadcasted_iota(I32, (N_EXPERTS, tm), 0)

    vals, idxs = [], []
    for _ in range(TOP_K):
        m = jnp.max(lg, axis=0, keepdims=True)
        idx = jnp.min(jnp.where(lg == m, e_iota, N_EXPERTS), axis=0, keepdims=True)
        vals.append(m)
        idxs.append(idx)
        lg = jnp.where(e_iota == idx, NEG_INF, lg)
    exps = [jnp.exp(vk - vals[0]) for vk in vals]
    denom = exps[0] + exps[1] + exps[2] + exps[3]

    chosen = jnp.zeros((N_EXPERTS, tm), F32)
    for idx in idxs:
        chosen = chosen + jnp.where(e_iota == idx, 1.0, 0.0)
    chosen = chosen.astype(BF16)
    run = run_s[...]
    before = dot(chosen, upper_ref[...]) + jnp.concatenate([run] * (tm // LANES), axis=1)
    run_s[...] = run + dot(chosen, ones_ref[...])
    cnt_ref[...] = run_s[...]

    zero_i = jnp.zeros((SUBLANES - TOP_K, tm), I32)
    ranks = [jnp.sum(jnp.where(e_iota == idx, before, 0.0), axis=0, keepdims=True).astype(I32)
             for idx in idxs]
    idx_ref[0] = jnp.concatenate(idxs + [zero_i], axis=0)
    rank_ref[0] = jnp.concatenate(ranks + [zero_i], axis=0)
    wrows = jnp.concatenate([e / denom for e in exps] + [jnp.zeros((LANES - TOP_K, tm), F32)], axis=0)
    wgt_ref[...] = wrows.T


def _out_router(x2, ym, yr, ye, w_out, g_ffn, w_router, b_router, tm):
    t, d = x2.shape
    wo = w_out.astype(BF16)
    wo1, wo2, wo3 = wo[:MOBA_W], wo[MOBA_W:MOBA_W + RWKV_W], wo[MOBA_W + RWKV_W:]
    wr = jnp.zeros((d, LANES), F32).at[:, :N_EXPERTS].set(w_router)
    wr_hi = wr.astype(BF16)
    wr_lo = (wr - wr_hi.astype(F32)).astype(BF16)
    br = jnp.full((1, LANES), NEG_INF, F32).at[0, :N_EXPERTS].set(b_router)
    upper = jnp.asarray(np.triu(np.ones((tm, tm), np.float32), 1)).astype(BF16)
    ones = jnp.ones((tm, LANES), BF16)
    tile = lambda n: pl.BlockSpec((tm, n), lambda i: (i, 0))
    slots = pl.BlockSpec((1, SUBLANES, tm), lambda i: (i, 0, 0))
    const = lambda a: pl.BlockSpec(a.shape, lambda i: (0,) * a.ndim)
    g2 = g_ffn.reshape(1, d)
    n = t // tm
    return pl.pallas_call(
        _out_router_kernel,
        grid=(n,),
        in_specs=[tile(d), tile(MOBA_W), tile(RWKV_W), tile(MEM_W), const(wo1), const(wo2), const(wo3),
                  const(g2), const(wr_hi), const(wr_lo), const(br), const(upper), const(ones)],
        out_specs=[tile(d), tile(d // 2), slots, slots, tile(LANES),
                   pl.BlockSpec((N_EXPERTS, LANES), lambda i: (0, 0))],
        out_shape=[jax.ShapeDtypeStruct((t, d), F32), jax.ShapeDtypeStruct((t, d // 2), I32),
                   jax.ShapeDtypeStruct((n, SUBLANES, tm), I32), jax.ShapeDtypeStruct((n, SUBLANES, tm), I32),
                   jax.ShapeDtypeStruct((t, LANES), F32), jax.ShapeDtypeStruct((N_EXPERTS, LANES), F32)],
        scratch_shapes=[pltpu.VMEM((N_EXPERTS, LANES), F32)],
        compiler_params=_cparams(("arbitrary",)),
        name="out_router",
    )(x2, ym, yr, ye, wo1, wo2, wo3, g2, wr_hi, wr_lo, br, upper, ones)


def _sc_mesh():
    return plsc.VectorSubcoreMesh(core_axis_name="c", subcore_axis_name="s",
                                  num_cores=SC_CORES, num_subcores=SC_SUBCORES)


def _sc_split(n_rows):
    n_workers = SC_CORES * SC_SUBCORES
    per_worker = n_rows // n_workers
    assert per_worker * n_workers == n_rows and per_worker % (SC_ROWS * SC_INFLIGHT) == 0
    assert SC_ROWS % SC_ALIGN == 0
    return per_worker


def _sc_gather_rows(table, idx):
    n_rows = idx.shape[0]
    width = table.shape[1]
    per_worker = _sc_split(n_rows)

    buf = lambda shape, dtype: [pltpu.VMEM(shape, dtype) for _ in range(SC_INFLIGHT)]

    @functools.partial(
        pl.kernel, mesh=_sc_mesh(),
        out_type=jax.ShapeDtypeStruct((n_rows, width), table.dtype),
        scratch_types=[buf((SC_ROWS,), I32), buf((SC_ROWS, width), table.dtype),
                       [pltpu.SemaphoreType.DMA for _ in range(SC_INFLIGHT)]],
        name="sc_gather",
    )
    def gather(table_hbm, idx_hbm, out_hbm, idx_v, rows_v, sems):
        worker = lax.axis_index("s") * SC_CORES + lax.axis_index("c")
        base = worker * per_worker

        @pl.loop(0, per_worker // (SC_ROWS * SC_INFLIGHT))
        def _(step):
            offs = [pl.multiple_of(base + (step * SC_INFLIGHT + b) * SC_ROWS, SC_ALIGN) for b in range(SC_INFLIGHT)]
            copies = []
            for b in range(SC_INFLIGHT):
                pltpu.sync_copy(idx_hbm.at[pl.ds(offs[b], SC_ROWS)], idx_v[b])
                copies.append(pltpu.async_copy(table_hbm.at[idx_v[b]], rows_v[b], sems[b]))
            for b in range(SC_INFLIGHT):
                copies[b].wait()
                pltpu.sync_copy(rows_v[b], out_hbm.at[pl.ds(offs[b], SC_ROWS)])

    return gather(table, idx)


def _sc_scatter_rows(rows, idx, n_out):
    n_src, width = rows.shape
    n_copies = idx.shape[0] // n_src
    assert n_copies * n_src == idx.shape[0]
    per_worker = _sc_split(n_src)
    chunks = per_worker // SC_ROWS
    assert chunks % SC_ALIGN == 0

    @functools.partial(
        pl.kernel, mesh=_sc_mesh(),
        out_type=jax.ShapeDtypeStruct((n_out, width), rows.dtype),
        scratch_types=[pltpu.VMEM((n_copies, chunks, SC_ROWS), I32),
                       pltpu.VMEM((SC_ROWS, width), rows.dtype),
                       [pltpu.SemaphoreType.DMA for _ in range(n_copies)]],
        name="sc_scatter",
    )
    def scatter(rows_hbm, idx_hbm, out_hbm, idx_v, rows_v, sems):
        worker = lax.axis_index("s") * SC_CORES + lax.axis_index("c")
        first = pl.multiple_of(worker * chunks, SC_ALIGN)
        for cpy in range(n_copies):
            pltpu.sync_copy(idx_hbm.at[cpy, pl.ds(first, chunks)], idx_v.at[cpy])

        @pl.loop(0, chunks)
        def _(step):
            src = pl.multiple_of((first + step) * SC_ROWS, SC_ALIGN)
            pltpu.sync_copy(rows_hbm.at[pl.ds(src, SC_ROWS)], rows_v)
            copies = [pltpu.async_copy(rows_v, out_hbm.at[idx_v.at[cpy, step]], sems[cpy])
                      for cpy in range(n_copies)]
            for cp in copies:
                cp.wait()

    return scatter(rows, idx.reshape(n_copies, n_src // SC_ROWS, SC_ROWS))


def _experts_kernel(be_ref, bv_ref, nused_ref, next_ref, slot_ref,
                    xs_ref, wgu_hbm, bg_ref, bu_ref, wd_hbm, bd_ref, perm_ref,
                    o_ref, wgu_buf, wd_buf, wg_s, wu_s, wd_s, sem):
    i = pl.program_id(0)
    e = be_ref[i]
    used = i < nused_ref[0]
    changed = ((i == 0) | (e != be_ref[jnp.maximum(i - 1, 0)])) & used
    slot = slot_ref[e]

    def weight_copies(expert, buf_slot):
        return (pltpu.make_async_copy(wgu_hbm.at[expert], wgu_buf.at[buf_slot], sem.at[buf_slot, 0]),
                pltpu.make_async_copy(wd_hbm.at[expert], wd_buf.at[buf_slot], sem.at[buf_slot, 1]))

    @pl.when((i == 0) & used)
    def _():
        for cp in weight_copies(e, slot):
            cp.start()

    @pl.when(changed)
    def _():
        for cp in weight_copies(e, slot):
            cp.wait()
        nxt = next_ref[e]

        @pl.when(nxt >= 0)
        def _():
            for cp in weight_copies(nxt, 1 - slot):
                cp.start()

        half = LANES
        for cblk in range(2 * D_EXPERT // (2 * half)):
            wt = wgu_buf[slot, :, cblk * 2 * half:(cblk + 1) * 2 * half].astype(BF16)
            sep = jnp.dot(wt, perm_ref[...], preferred_element_type=F32).astype(BF16)
            wg_s[:, cblk * half:(cblk + 1) * half] = sep[:, :half]
            wu_s[:, cblk * half:(cblk + 1) * half] = sep[:, half:]
        wd_s[...] = wd_buf[slot].astype(BF16)

    @pl.when(used)
    def _():
        rows = lax.broadcasted_iota(I32, xs_ref.shape, 0)
        xb = _unpack_halves(jnp.where(rows < bv_ref[i], xs_ref[...], 0)).astype(BF16)
        gate = jnp.dot(xb, wg_s[...], preferred_element_type=F32) + bg_ref[0]
        up = jnp.dot(xb, wu_s[...], preferred_element_type=F32) + bu_ref[0]
        gate = jnp.minimum(gate, SWIGLU_LIMIT)
        up = jnp.clip(up, -SWIGLU_LIMIT, SWIGLU_LIMIT)
        glu = gate * _sigmoid(gate * SWIGLU_ALPHA)
        act = ((up + 1.0) * glu).astype(BF16)
        o_ref[...] = _pack_halves(jnp.dot(act, wd_s[...], preferred_element_type=F32) + bd_ref[0])

    @pl.when(jnp.logical_not(used))
    def _():
        o_ref[...] = jnp.zeros_like(o_ref)


def _experts(blk_expert, blk_valid, n_used, next_expert, buf_slot, xs, w_gate_up, b_gate_up, w_down,
             b_down):
    p_rows = xs.shape[0]
    d = 2 * xs.shape[1]
    n_blocks = p_rows // EXPERT_ROWS
    bg = b_gate_up[:, 0::2].reshape(N_EXPERTS, 1, D_EXPERT)
    bu = b_gate_up[:, 1::2].reshape(N_EXPERTS, 1, D_EXPERT)
    bd = b_down.reshape(N_EXPERTS, 1, d)
    perm_np = np.zeros((2 * LANES, 2 * LANES), np.float32)
    perm_np[2 * np.arange(LANES), np.arange(LANES)] = 1.0
    perm_np[2 * np.arange(LANES) + 1, LANES + np.arange(LANES)] = 1.0
    perm = jnp.asarray(perm_np).astype(BF16)
    by_expert = lambda shape: pl.BlockSpec((1,) + shape, lambda i, be, *_: (be[i], 0, 0))
    grid_spec = pltpu.PrefetchScalarGridSpec(
        num_scalar_prefetch=5,
        grid=(n_blocks,),
        in_specs=[pl.BlockSpec((EXPERT_ROWS, d // 2), lambda i, *_: (i, 0)),
                  pl.BlockSpec(memory_space=pl.ANY), by_expert((1, D_EXPERT)), by_expert((1, D_EXPERT)),
                  pl.BlockSpec(memory_space=pl.ANY), by_expert((1, d)),
                  pl.BlockSpec(perm.shape, lambda i, *_: (0, 0))],
        out_specs=pl.BlockSpec((EXPERT_ROWS, d // 2), lambda i, *_: (i, 0)),
        scratch_shapes=[pltpu.VMEM((2, d, 2 * D_EXPERT), F32), pltpu.VMEM((2, D_EXPERT, d), F32),
                        pltpu.VMEM((d, D_EXPERT), BF16), pltpu.VMEM((d, D_EXPERT), BF16),
                        pltpu.VMEM((D_EXPERT, d), BF16), pltpu.SemaphoreType.DMA((2, 2))],
    )
    return pl.pallas_call(
        _experts_kernel,
        grid_spec=grid_spec,
        out_shape=jax.ShapeDtypeStruct((p_rows, d // 2), I32),
        compiler_params=_cparams(("arbitrary",)),
        name="experts",
    )(blk_expert, blk_valid, n_used, next_expert, buf_slot, xs, w_gate_up, bg, bu, w_down, bd, perm)


def _combine_kernel(yg_ref, x1_ref, w_ref, g_ref, *rest):
    o_ref = rest[-1]
    acc = x1_ref[...]
    wts = w_ref[...]
    for kk in range(TOP_K):
        acc = acc + _unpack_halves(yg_ref[kk]) * wts[:, kk:kk + 1]
    o_ref[...] = _rms(acc, g_ref[...])


def _combine(yg, x1, wts, g_final, tb, first_tile, partial_out):
    t, d = x1.shape
    half = yg.shape[2]
    tile = lambda n: pl.BlockSpec((tb, n), lambda i: (i + first_tile, 0))
    in_specs = [pl.BlockSpec((TOP_K, tb, half), lambda i: (0, i, 0)), tile(d), tile(LANES),
                pl.BlockSpec((1, d), lambda i: (0, 0))]
    args = [yg, x1, wts, g_final.reshape(1, d)]
    aliases = {}
    if partial_out is not None:
        in_specs.append(pl.BlockSpec(memory_space=pl.ANY))
        args.append(partial_out)
        aliases = {len(args) - 1: 0}
    return pl.pallas_call(
        _combine_kernel,
        grid=(yg.shape[1] // tb,),
        in_specs=in_specs,
        out_specs=tile(d),
        out_shape=jax.ShapeDtypeStruct((t, d), F32),
        input_output_aliases=aliases,
        compiler_params=_cparams(("parallel",)),
        name="combine",
    )(*args)


def _layer(x, mem, w_in, w_out, w_mem_kv, g_mix, g_mem, g_ffn, bias_tables, mu, w0, w_up, a0, a_up,
           g_up, k_k, k_a, r_k, gn_g, gn_b, w_router, b_router, w_gate_up, b_gate_up, w_down, b_down,
           g_last):
    b, s, d = x.shape
    m = mem.shape[1]
    t = b * s
    x2 = x.reshape(t, d)

    p = _norm_matmul(x2, g_mix, w_in.astype(BF16), PROJ_TOKENS, "in_proj")
    p3 = p.reshape(b, s, IN_COLS)
    y_moba = _moba_attention(p3, *bias_tables)
    y_rwkv = _rwkv(p3, mu, w0, w_up, a0, a_up, g_up, k_k, k_a, r_k, gn_g, gn_b)
    mkv = _norm_matmul(mem.reshape(b * m, d), g_mem, w_mem_kv.astype(BF16), PROJ_TOKENS, "mem_kv")
    y_mem = _mem_attention(p3, mkv.reshape(b, m, 2 * MEM_W), MEM_Q_TOKENS)

    tb = ROUTE_TOKENS
    x1, h2, idx_o, rank_o, wgt_p, cnt = _out_router(
        x2, y_moba.reshape(t, MOBA_W), y_rwkv.reshape(t, RWKV_W), y_mem.reshape(t, MEM_W),
        w_out, g_ffn, w_router, b_router, tb)

    counts = cnt[:, 0].astype(I32)
    padded = (counts + EXPERT_ROWS - 1) // EXPERT_ROWS * EXPERT_ROWS
    pad_ends = jnp.cumsum(padded)
    pad_starts = (pad_ends - padded).astype(I32)
    n_blocks = (t * TOP_K) // EXPERT_ROWS + N_EXPERTS
    blk_start = jnp.arange(n_blocks, dtype=I32) * EXPERT_ROWS
    blk_expert = jnp.minimum(jnp.sum(blk_start[:, None] >= pad_ends[None, :], axis=1),
                             N_EXPERTS - 1).astype(I32)
    of_block = blk_expert[:, None] == jnp.arange(N_EXPERTS, dtype=I32)[None, :]
    rows_left = jnp.sum(jnp.where(of_block, (counts + pad_starts)[None, :], 0), axis=1) - blk_start
    blk_valid = jnp.clip(rows_left, 0, EXPERT_ROWS)
    n_used = (pad_ends[-1:] // EXPERT_ROWS).astype(I32)
    has_rows = counts > 0
    first_from = lax.cummin(jnp.where(has_rows, jnp.arange(N_EXPERTS, dtype=I32), N_EXPERTS), reverse=True)
    next_expert = jnp.concatenate([first_from[1:], jnp.full((1,), N_EXPERTS, I32)])
    next_expert = jnp.where(next_expert < N_EXPERTS, next_expert, -1).astype(I32)
    buf_slot = ((jnp.cumsum(has_rows.astype(I32)) - has_rows.astype(I32)) % 2).astype(I32)
    idx_kt = jnp.swapaxes(idx_o[:, :TOP_K, :], 0, 1).reshape(TOP_K, t)
    rank_kt = jnp.swapaxes(rank_o[:, :TOP_K, :], 0, 1).reshape(TOP_K, t)
    experts = jnp.arange(N_EXPERTS, dtype=I32)[:, None, None]
    start_kt = jnp.sum(jnp.where(idx_kt[None] == experts, pad_starts[:, None, None], 0), axis=0)
    dest = (start_kt + rank_kt).reshape(TOP_K * t)

    xs = _sc_scatter_rows(h2, dest, n_blocks * EXPERT_ROWS)
    ys = _experts(blk_expert, blk_valid.astype(I32), n_used, next_expert, buf_slot, xs,
                  w_gate_up, b_gate_up, w_down, b_down)
    dest_kt = dest.reshape(TOP_K, t)
    out = None
    for part in range(COMBINE_PARTS):
        t_part = t // COMBINE_PARTS
        rows = dest_kt[:, part * t_part:(part + 1) * t_part].reshape(TOP_K * t_part)
        yg = _sc_gather_rows(ys, rows).reshape(TOP_K, t_part, d // 2)
        out = _combine(yg, x1, wgt_p, g_last, tb, part * (t_part // tb), out)
    return out.reshape(b, s, d)


def kernel(x, mem, w_in, w_out, w_mem_kv, g_mix, g_mem, g_ffn, g_final, rel_bias, rwkv_mu, rwkv_w0,
           rwkv_w_up, rwkv_a0, rwkv_a_up, rwkv_g_up, rwkv_k_k, rwkv_k_a, rwkv_r_k, rwkv_gn_g, rwkv_gn_b,
           w_router, b_router, w_gate_up, b_gate_up, w_down, b_down):
    depth = w_in.shape[0]
    assert depth == 1, "the final norm is fused into the last layer's combine kernel"
    bias_tables = _moba_bias_tables(rel_bias)
    l = 0
    return _layer(x, mem, w_in[l], w_out[l], w_mem_kv[l], g_mix[l], g_mem[l], g_ffn[l], bias_tables,
                  rwkv_mu[l], rwkv_w0[l], rwkv_w_up[l], rwkv_a0[l], rwkv_a_up[l], rwkv_g_up[l],
                  rwkv_k_k[l], rwkv_k_a[l], rwkv_r_k[l], rwkv_gn_g[l], rwkv_gn_b[l], w_router[l],
                  b_router[l], w_gate_up[l], b_gate_up[l], w_down[l], b_down[l], g_final)
```

```python
import functools
import math

import numpy as np
import jax
import jax.numpy as jnp
from jax import lax
from jax.experimental import pallas as pl
from jax.experimental.pallas import tpu as pltpu
from jax.experimental.pallas import tpu_sc as plsc

F32 = jnp.float32
BF16 = jnp.bfloat16
I32 = jnp.int32
HI = lax.Precision.HIGHEST

D_MODEL = 1024
HEAD_DIM = 64
MOBA_HEADS = 6
RWKV_HEADS = 6
MEM_HEADS = 4
MOBA_W = MOBA_HEADS * HEAD_DIM
RWKV_W = RWKV_HEADS * HEAD_DIM
MEM_W = MEM_HEADS * HEAD_DIM
MOBA_BLOCK = 256
MOBA_TOPK = 3
N_BUCKETS = 32
MAX_DISTANCE = 128
DECAY_LORA = 64
AAA_LORA = 64
GATE_LORA = 128
RWKV_COLS = 3 * RWKV_W + DECAY_LORA + AAA_LORA + GATE_LORA
RWKV_GN_EPS = 64e-5
IN_COLS = 3 * MOBA_W + RWKV_COLS + MEM_W
N_EXPERTS = 32
TOP_K = 4
D_EXPERT = D_MODEL
SWIGLU_ALPHA = 1.702
SWIGLU_LIMIT = 7.0
RMS_EPS = 1e-5

LANES = 128
SUBLANES = 8
BF16_SUBLANES = 16
PROJ_TOKENS = 512
MEM_Q_TOKENS = 512
RWKV_CHUNK = 128
RWKV_SEQS_PER_STEP = 4
EXPERT_ROWS = 256
ROUTE_TOKENS = 512
SC_CORES = 2
SC_SUBCORES = 16
SC_ROWS = 64
SC_ALIGN = 8
SC_INFLIGHT = 2
COMBINE_PARTS = 8
VMEM_LIMIT = 56 * 1024 * 1024
NEG_INF = float("-inf")
HIGH_HALF = -65536
LOG2_E = 1.4426950408889634
MOBA_VROWS = HEAD_DIM + BF16_SUBLANES


def _cparams(sem):
    return pltpu.CompilerParams(dimension_semantics=sem, vmem_limit_bytes=VMEM_LIMIT)


def _rms(x, g):
    return x * lax.rsqrt(jnp.mean(x * x, axis=-1, keepdims=True) + RMS_EPS) * g


def _bdot(a, b):
    return jnp.dot(a.astype(BF16), b.astype(BF16), preferred_element_type=F32)


def _hdot(a, b):
    return jnp.dot(a, b, preferred_element_type=F32, precision=HI)


def _dot_nt(a, b):
    return lax.dot_general(a, b, (((1,), (1,)), ((), ())), preferred_element_type=F32)


def _dot_tn(a, b):
    return lax.dot_general(a, b, (((0,), (0,)), ((), ())), preferred_element_type=F32)


def _split_bf16(x, terms):
    parts = []
    for _ in range(terms):
        hi = x.astype(BF16)
        parts.append(hi)
        x = x - hi.astype(F32)
    return parts


def _dot_exact_rhs(x, m_bf16, terms):
    acc = None
    for part in _split_bf16(x, terms):
        d = jnp.dot(part, m_bf16, preferred_element_type=F32)
        acc = d if acc is None else acc + d
    return acc


def _pack_halves(x):
    n = x.shape[1] // 2
    lo = pltpu.bitcast(x[:, :n].astype(BF16).astype(F32), I32)
    hi = pltpu.bitcast(x[:, n:].astype(BF16).astype(F32), I32)
    return (hi & HIGH_HALF) | lax.shift_right_logical(lo, 16)


def _unpack_halves(w):
    lo = pltpu.bitcast(w << 16, F32)
    hi = pltpu.bitcast(w & HIGH_HALF, F32)
    return jnp.concatenate([lo, hi], axis=1)


def _norm_matmul_kernel(x_ref, g_ref, w_ref, o_ref):
    h = _rms(x_ref[...], g_ref[...])
    o_ref[...] = jnp.dot(h.astype(BF16), w_ref[...], preferred_element_type=F32)


def _norm_matmul(x, g, w_bf16, tm, name):
    t, d = x.shape
    n = w_bf16.shape[1]
    tm = min(tm, t)
    return pl.pallas_call(
        _norm_matmul_kernel,
        grid=(t // tm,),
        in_specs=[pl.BlockSpec((tm, d), lambda i: (i, 0)),
                  pl.BlockSpec((1, d), lambda i: (0, 0)),
                  pl.BlockSpec((d, n), lambda i: (0, 0))],
        out_specs=pl.BlockSpec((tm, n), lambda i: (i, 0)),
        out_shape=jax.ShapeDtypeStruct((t, n), F32),
        compiler_params=_cparams(("parallel",)),
        name=name,
    )(x, g.reshape(1, d), w_bf16)


def _t5_bucket_np(dist):
    n = np.maximum(dist, 0)
    max_exact = N_BUCKETS // 2
    nf = np.maximum(n, 1).astype(np.float64)
    large = max_exact + (np.log(nf / max_exact) / math.log(MAX_DISTANCE / max_exact)
                         * (N_BUCKETS - max_exact)).astype(np.int64)
    large = np.minimum(large, N_BUCKETS - 1)
    return np.where(n < max_exact, n, large)


def _moba_bias_tables(rel_bias):
    n = MOBA_BLOCK
    assert np.all(_t5_bucket_np(np.arange(n + 1, 64 * n)) == N_BUCKETS - 1)
    bias_t = rel_bias.astype(F32).T * LOG2_E

    def by_distance(dist):
        bucket = jnp.asarray(_t5_bucket_np(dist), I32)[None]
        tab = jnp.zeros((MOBA_HEADS, dist.shape[0]), F32)
        for b in range(N_BUCKETS):
            tab = jnp.where(bucket == b, bias_t[:, b][:, None], tab)
        return tab

    d = np.arange(2 * n)
    g_own = jnp.where(jnp.asarray(d < n)[None], by_distance(np.where(d < n, d, 0)), NEG_INF)
    g_prev = by_distance(np.where(d < n, n + d, d - n))
    far = bias_t[:, N_BUCKETS - 1]
    return g_own[:, None, :], g_prev[:, None, :], far


def _moba_kernel(far_ref, q_ref, k_ref, v_ref, g0_ref, g1_ref, o_ref,
                 qt_s, kb_s, vt_s, sc_s, t0_ref, t1_ref, *, nb):
    hp = pl.program_id(1)
    blk = MOBA_BLOCK
    scale = HEAD_DIM ** -0.5 * LOG2_E
    row_head = lax.broadcasted_iota(I32, (LANES, 1), 0) // HEAD_DIM
    lane_head = lax.broadcasted_iota(I32, (1, LANES), 1) // HEAD_DIM
    ones_rows = jnp.ones((MOBA_VROWS - HEAD_DIM, blk), BF16)

    for h in range(2):
        for g_ref, t_ref in ((g0_ref, t0_ref), (g1_ref, t1_ref)):
            rows = jnp.broadcast_to(g_ref[h], (blk, 2 * blk))
            t_ref[h] = pltpu.roll(rows, 0, 1, stride=1, stride_axis=0)[:, :blk]

    kmean_rows = []
    for j in range(nb):
        sl = pl.ds(j * blk, blk)
        qt_s[j] = (q_ref[0, sl, :] * scale).T
        vt = v_ref[0, sl, :].T.astype(BF16)
        for h in range(2):
            vt_s[j, h * MOBA_VROWS:h * MOBA_VROWS + HEAD_DIM, :] = vt[h * HEAD_DIM:(h + 1) * HEAD_DIM, :]
            vt_s[j, h * MOBA_VROWS + HEAD_DIM:(h + 1) * MOBA_VROWS, :] = ones_rows
        kj = k_ref[0, sl, :]
        kb_s[j] = kj.astype(BF16)
        kmean_rows.append(jnp.mean(kj, axis=0, keepdims=True))
    kmean = jnp.concatenate(kmean_rows, axis=0)
    km2 = jnp.concatenate([jnp.where(lane_head == 0, kmean, 0.0),
                           jnp.where(lane_head == 1, kmean, 0.0)], axis=0)
    blk_iota = lax.broadcasted_iota(I32, (nb, blk), 0)

    def score_pass(i):
        qt = qt_s[i]
        negs = [None, None]
        if i > MOBA_TOPK:
            gate2 = _hdot(km2, qt)
            for h in range(2):
                g = gate2[h * nb:(h + 1) * nb, :]
                cnt = jnp.zeros((nb, blk), I32)
                for m in range(i):
                    gm = g[m:m + 1, :]
                    beats = (gm > g) | ((gm == g) & (m < blk_iota))
                    cnt = cnt + jnp.where(beats, 1, 0)
                negs[h] = jnp.where(cnt < MOBA_TOPK, 0.0, NEG_INF)
        col_max = []
        for h in range(2):
            qt_h = jnp.where(row_head == h, qt, 0.0).astype(BF16)
            far_bias = far_ref[2 * hp + h]
            m_run = None
            for j in range(i + 1):
                s = jnp.dot(kb_s[j], qt_h, preferred_element_type=F32)
                if j == i:
                    s = s + t0_ref[h]
                else:
                    bias = t1_ref[h] if j == i - 1 else far_bias
                    if negs[h] is not None:
                        bias = bias + negs[h][j:j + 1, :]
                    s = s + bias
                sc_s[i % 2, h, j] = s
                cm = jnp.max(s, axis=0, keepdims=True)
                m_run = cm if m_run is None else jnp.maximum(m_run, cm)
            col_max.append(m_run)
        return col_max

    def prob_pass(i, col_max):
        outs = []
        for h in range(2):
            m_run = col_max[h]
            probs = jnp.concatenate([jnp.exp2(sc_s[i % 2, h, j] - m_run).astype(BF16)
                                     for j in range(i + 1)], axis=0)
            vals = jnp.concatenate([vt_s[j, h * MOBA_VROWS:(h + 1) * MOBA_VROWS, :]
                                    for j in range(i + 1)], axis=1)
            acc = jnp.dot(vals, probs, preferred_element_type=F32)
            outs.append(acc[:HEAD_DIM] / acc[HEAD_DIM:HEAD_DIM + 1])
        o_t = jnp.concatenate(outs, axis=0)
        o_ref[0, pl.ds(i * blk, blk), :] = o_t.T

    pending = None
    for i in range(nb):
        col_max = score_pass(i)
        if pending is not None:
            prob_pass(*pending)
        pending = (i, col_max)
    prob_pass(*pending)


def _moba_attention(p3, g_own, g_prev, far):
    b, s, _ = p3.shape
    nb = s // MOBA_BLOCK
    n_pairs = MOBA_HEADS // 2
    blk_spec = lambda off: pl.BlockSpec((1, s, LANES), lambda bi, hp: (bi, 0, off + hp))
    tab_spec = pl.BlockSpec((2, 1, 2 * MOBA_BLOCK), lambda bi, hp: (hp, 0, 0))
    return pl.pallas_call(
        functools.partial(_moba_kernel, nb=nb),
        grid=(b, n_pairs),
        in_specs=[pl.BlockSpec(memory_space=pltpu.SMEM),
                  blk_spec(0), blk_spec(n_pairs), blk_spec(2 * n_pairs), tab_spec, tab_spec],
        out_specs=pl.BlockSpec((1, s, LANES), lambda bi, hp: (bi, 0, hp)),
        out_shape=jax.ShapeDtypeStruct((b, s, MOBA_W), F32),
        scratch_shapes=[pltpu.VMEM((nb, LANES, MOBA_BLOCK), F32),
                        pltpu.VMEM((nb, MOBA_BLOCK, LANES), BF16),
                        pltpu.VMEM((nb, 2 * MOBA_VROWS, MOBA_BLOCK), BF16),
                        pltpu.VMEM((2, 2, nb, MOBA_BLOCK, MOBA_BLOCK), F32),
                        pltpu.VMEM((2, MOBA_BLOCK, MOBA_BLOCK), F32),
                        pltpu.VMEM((2, MOBA_BLOCK, MOBA_BLOCK), F32)],
        compiler_params=_cparams(("parallel", "parallel")),
        name="moba",
    )(far, p3, p3, p3, g_own, g_prev)


def _softplus(z):
    return jnp.maximum(z, 0.0) + jnp.log(1.0 + jnp.exp(-jnp.abs(z)))


def _sigmoid(z):
    return 1.0 / (1.0 + jnp.exp(-z))


def _rwkv_kernel(r_ref, k_ref, v_ref, wa_ref, g_ref,
                 mu_r_ref, mu_k_ref, mu_v_ref, mu_wa_ref, mu_g_ref,
                 w0_ref, wup_ref, a0_ref, aup_ref, gup_ref, kk_ref, ka_ref, rk_ref,
                 gng_ref, gnb_ref, hsum_ref, tri_ref,
                 o_ref, st_s, prev_r, prev_k, prev_v, prev_wa, prev_g, *, n_seq):
    c = RWKV_CHUNK
    mid = c // 2

    @pl.when(pl.program_id(1) == 0)
    def _():
        st_s[...] = jnp.zeros_like(st_s)
        prev_r[...] = jnp.zeros_like(prev_r)
        prev_k[...] = jnp.zeros_like(prev_k)
        prev_v[...] = jnp.zeros_like(prev_v)
        prev_wa[...] = jnp.zeros_like(prev_wa)
        prev_g[...] = jnp.zeros_like(prev_g)

    n_pairs = RWKV_HEADS // 2
    hsum = hsum_ref[...]
    tri = tri_ref[...]
    row = lax.broadcasted_iota(I32, (c, c), 0)
    col = lax.broadcasted_iota(I32, (c, c), 1)
    strict = row > col
    incl = row >= col
    lane_half = lax.broadcasted_iota(I32, (1, LANES), 1) // HEAD_DIM
    hmask = hsum.astype(F32)
    dot = functools.partial(jnp.dot, preferred_element_type=F32)
    lanes = lambda x, jp: x[:, jp * LANES:(jp + 1) * LANES]

    def group_sum(x):
        return jnp.concatenate([_dot_exact_rhs(lanes(x, jp), hsum, 2) for jp in range(n_pairs)], axis=1)

    def two_heads(x):
        return jnp.concatenate([jnp.where(lane_half == 0, x, 0.0), jnp.where(lane_half == 1, x, 0.0)],
                               axis=0).astype(BF16)

    def side(a, b):
        return jnp.concatenate([a, b], axis=1)

    def mix(g, x_ref, prev_ref, mu_ref):
        x = x_ref[g]
        rows = lax.broadcasted_iota(I32, x.shape, 0)
        shifted = jnp.where(rows == 0, prev_ref[g], pltpu.roll(x, 1, axis=0))
        prev_ref[g] = x[c - 1:c, :]
        return x + (shifted - x) * mu_ref[...]

    def prepare(g):
        r = mix(g, r_ref, prev_r, mu_r_ref)
        k = mix(g, k_ref, prev_k, mu_k_ref)
        v = mix(g, v_ref, prev_v, mu_v_ref)
        xwa = mix(g, wa_ref, prev_wa, mu_wa_ref)
        xg = mix(g, g_ref, prev_g, mu_g_ref)

        w = -_softplus(-(w0_ref[...] + _bdot(jnp.tanh(xwa), wup_ref[...]))) - 0.5
        logd = -jnp.exp(w)
        a = _sigmoid(a0_ref[...] + _bdot(xwa, aup_ref[...]))
        gate = _bdot(_sigmoid(xg), gup_ref[...])
        kk = k * kk_ref[...]
        kk = kk / jnp.maximum(jnp.sqrt(group_sum(kk * kk)), 1e-12)
        k2 = k * (1.0 + (a - 1.0) * ka_ref[...])

        parts = _split_bf16(logd, 3)
        cum = dot(tri, parts[0]) + dot(tri, parts[1]) + dot(tri, parts[2])
        cum_last = cum[c - 1:c, :]
        ref = cum[mid - 1:mid, :]
        rel = cum - ref
        g_in = jnp.exp(rel)
        g_inv = jnp.exp(-rel)
        g_tail = jnp.exp(cum_last - cum)
        a_t = -kk * jnp.exp(rel - logd)
        b_t = kk * a * g_inv
        k_t = k2 * g_inv
        r_t = r * g_in
        b_hat = kk * a * g_tail
        k_hat = k2 * g_tail

        return dict(r=r, k2=k2, v=v, gate=gate, b_hat=b_hat, k_hat=k_hat, ref_scale=jnp.exp(ref),
                    decay=jnp.exp(cum_last), bk=jnp.concatenate([b_t, k_t], axis=0).astype(BF16),
                    ar_all=jnp.concatenate([a_t, r_t], axis=0))

    def chunk_products(g, el):
        v, b_hat, k_hat, ref_scale, decay = el["v"], el["b_hat"], el["k_hat"], el["ref_scale"], el["decay"]
        bk, ar_all = el["bk"], el["ar_all"]
        per_pair = []
        for jp in range(n_pairs):
            st = st_s[g, jp]
            st_ref = (st * lanes(ref_scale, jp)).astype(BF16)
            both = _dot_nt(lanes(ar_all, jp).astype(BF16), st_ref)
            quads = []
            for half in range(2):
                ar = jnp.where(lane_half == half, lanes(ar_all, jp), 0.0)
                m4 = _dot_nt(ar.astype(BF16), lanes(bk, jp))
                quads.append((jnp.where(strict, m4[:c, :c], 0.0),
                              jnp.where(strict, m4[:c, c:], 0.0).astype(BF16),
                              jnp.where(incl, m4[c:, :c], 0.0).astype(BF16),
                              jnp.where(incl, m4[c:, c:], 0.0).astype(BF16)))
            per_pair.append(dict(
                st=st, p0=both[:c], o0=both[c:], v_rows=two_heads(lanes(v, jp)),
                l_ab=[q[0] for q in quads], l_ak=side(quads[0][1], quads[1][1]),
                m_rb=side(quads[0][2], quads[1][2]), m_rk=side(quads[0][3], quads[1][3]),
                v=lanes(v, jp).astype(BF16), b_hat=lanes(b_hat, jp).astype(BF16),
                k_hat=lanes(k_hat, jp).astype(BF16), decay=lanes(decay, jp)))
        return dict(r=el["r"], k2=el["k2"], v=v, gate=el["gate"], pairs=per_pair)

    seqs = {}

    def solve(group):
        heads = [(g, jp, half) for g in group for jp in range(n_pairs) for half in range(2)]
        pair_ids = [(g, jp) for g in group for jp in range(n_pairs)]

        pw = {gh: seqs[gh[0]]["pairs"][gh[1]]["l_ab"][gh[2]].astype(BF16) for gh in heads}
        u_loc = {(g, jp): seqs[g]["pairs"][jp]["p0"]
                 + dot(seqs[g]["pairs"][jp]["l_ak"], seqs[g]["pairs"][jp]["v_rows"]) for g, jp in pair_ids}
        span = 1
        while span < c:
            u_loc = {(g, jp): u_loc[(g, jp)]
                     + dot(side(pw[(g, jp, 0)], pw[(g, jp, 1)]), two_heads(u_loc[(g, jp)]))
                     for g, jp in pair_ids}
            span *= 2
            if span < c:
                pw = {gh: dot(pw[gh], pw[gh]).astype(BF16) for gh in heads}

        for g in group:
            sq = seqs[g]
            outs = []
            for jp in range(n_pairs):
                pr = sq["pairs"][jp]
                u = u_loc[(g, jp)]
                outs.append(pr["o0"] + dot(pr["m_rb"], two_heads(u)) + dot(pr["m_rk"], pr["v_rows"]))
                upd = _dot_tn(u.astype(BF16), pr["b_hat"]) + _dot_tn(pr["v"], pr["k_hat"])
                st_s[g, jp] = pr["st"] * pr["decay"] + hmask * upd
            o = jnp.concatenate(outs, axis=1)

            inv_n = 1.0 / HEAD_DIM
            mean = group_sum(o) * inv_n
            dev = o - mean
            var = group_sum(dev * dev) * inv_n
            y = dev * lax.rsqrt(var + RWKV_GN_EPS) * gng_ref[...] + gnb_ref[...]
            y = y + group_sum(sq["r"] * sq["k2"] * rk_ref[...]) * sq["v"]
            o_ref[g] = y * sq["gate"]

    elementwise = None
    for g in range(n_seq):
        nxt = prepare(g)
        if elementwise is not None:
            seqs[g - 1] = chunk_products(g - 1, elementwise)
        elementwise = nxt
    seqs[n_seq - 1] = chunk_products(n_seq - 1, elementwise)
    solve(list(range(n_seq)))


def _rwkv(p3, mu, w0, w_up, a0, a_up, g_up, k_k, k_a, r_k, gn_g, gn_b):
    b, s, _ = p3.shape
    c = RWKV_CHUNK
    w = RWKV_W
    n_seq = RWKV_SEQS_PER_STEP if b % RWKV_SEQS_PER_STEP == 0 else 1
    base = 3 * MOBA_W
    assert base % w == 0 and (base + 3 * w) % LANES == 0
    lora = DECAY_LORA + AAA_LORA

    def col_spec(width, off_cols):
        assert off_cols % width == 0
        return pl.BlockSpec((n_seq, c, width), lambda bi, ci: (bi, ci, off_cols // width))

    row = lambda x: x.reshape(1, -1).astype(F32)
    const = lambda shape: pl.BlockSpec(shape, lambda bi, ci: (0,) * len(shape))
    wup_pad = jnp.concatenate([w_up, jnp.zeros((AAA_LORA, w), F32)], axis=0).astype(BF16)
    aup_pad = jnp.concatenate([jnp.zeros((DECAY_LORA, w), F32), a_up], axis=0).astype(BF16)
    head = np.arange(LANES) // HEAD_DIM
    hsum = jnp.asarray((head[:, None] == head[None, :]).astype(np.float32)).astype(BF16)
    tri = jnp.asarray(np.tril(np.ones((c, c), np.float32))).astype(BF16)
    vec_args = [mu[:w], mu[w:2 * w], mu[2 * w:3 * w], mu[3 * w:3 * w + lora], mu[3 * w + lora:],
                w0, None, a0, None, None, k_k, k_a, r_k.reshape(-1), gn_g, gn_b]
    args = [p3, p3, p3, p3, p3]
    specs = [col_spec(w, base), col_spec(w, base + w), col_spec(w, base + 2 * w),
             col_spec(lora, base + 3 * w), col_spec(GATE_LORA, base + 3 * w + lora)]
    mats = {6: wup_pad, 8: aup_pad, 9: g_up.astype(BF16)}
    for idx, a in enumerate(vec_args):
        arr = mats[idx] if a is None else row(a)
        args.append(arr)
        specs.append(const(arr.shape))
    for arr in (hsum, tri):
        args.append(arr)
        specs.append(const(arr.shape))
    return pl.pallas_call(
        functools.partial(_rwkv_kernel, n_seq=n_seq),
        grid=(b // n_seq, s // c),
        in_specs=specs,
        out_specs=pl.BlockSpec((n_seq, c, w), lambda bi, ci: (bi, ci, 0)),
        out_shape=jax.ShapeDtypeStruct((b, s, w), F32),
        scratch_shapes=[pltpu.VMEM((n_seq, RWKV_HEADS // 2, LANES, LANES), F32),
                        pltpu.VMEM((n_seq, 1, w), F32),
                        pltpu.VMEM((n_seq, 1, w), F32), pltpu.VMEM((n_seq, 1, w), F32),
                        pltpu.VMEM((n_seq, 1, lora), F32), pltpu.VMEM((n_seq, 1, GATE_LORA), F32)],
        compiler_params=_cparams(("parallel", "arbitrary")),
        name="rwkv",
    )(*args)


def _mem_attn_kernel(q_ref, kv_ref, o_ref):
    scale = HEAD_DIM ** -0.5
    lane_head = lax.broadcasted_iota(I32, (1, MEM_W), 1) // HEAD_DIM
    q = q_ref[0] * scale
    mk = kv_ref[0, :, :MEM_W].astype(BF16)
    mv = kv_ref[0, :, MEM_W:].astype(BF16)
    out = jnp.zeros(q.shape, F32)
    for h in range(MEM_HEADS):
        hm = lane_head == h
        s = _dot_nt(jnp.where(hm, q, 0.0).astype(BF16), mk)
        s = s - jnp.max(s, axis=-1, keepdims=True)
        e = jnp.exp(s)
        p = e / jnp.sum(e, axis=-1, keepdims=True)
        out = jnp.where(hm, jnp.dot(p.astype(BF16), mv, preferred_element_type=F32), out)
    o_ref[0] = out


def _mem_attention(p3, mkv3, tq):
    b, s, _ = p3.shape
    m = mkv3.shape[1]
    off = (3 * MOBA_W + RWKV_COLS) // MEM_W
    assert off * MEM_W == 3 * MOBA_W + RWKV_COLS
    return pl.pallas_call(
        _mem_attn_kernel,
        grid=(b, s // tq),
        in_specs=[pl.BlockSpec((1, tq, MEM_W), lambda bi, i: (bi, i, off)),
                  pl.BlockSpec((1, m, 2 * MEM_W), lambda bi, i: (bi, 0, 0))],
        out_specs=pl.BlockSpec((1, tq, MEM_W), lambda bi, i: (bi, i, 0)),
        out_shape=jax.ShapeDtypeStruct((b, s, MEM_W), F32),
        compiler_params=_cparams(("parallel", "parallel")),
        name="mem_attn",
    )(p3, mkv3)


def _out_router_kernel(x_ref, ym_ref, yr_ref, ye_ref, wo1_ref, wo2_ref, wo3_ref, g_ref,
                       wrh_ref, wrl_ref, br_ref, upper_ref, ones_ref,
                       x1_ref, h_ref, idx_ref, rank_ref, wgt_ref, cnt_ref, run_s):
    @pl.when(pl.program_id(0) == 0)
    def _():
        run_s[...] = jnp.zeros_like(run_s)

    x1 = (x_ref[...] + _bdot(ym_ref[...], wo1_ref[...]) + _bdot(yr_ref[...], wo2_ref[...])
          + _bdot(ye_ref[...], wo3_ref[...]))
    x1_ref[...] = x1
    h = _rms(x1, g_ref[...])
    h_ref[...] = _pack_halves(h)
    tm = h.shape[0]
    h_hi, h_lo = _split_bf16(h, 2)
    dot = functools.partial(jnp.dot, preferred_element_type=F32)
    logits = dot(h_hi, wrh_ref[...]) + (dot(h_hi, wrl_ref[...]) + dot(h_lo, wrh_ref[...])) + br_ref[...]
    lg = logits.T[:N_EXPERTS, :]
    e_iota = lax.broadcasted_iota(I32, (N_EXPERTS, tm), 0)

    vals, idxs = [], []
    for _ in range(TOP_K):
        m = jnp.max(lg, axis=0, keepdims=True)
        idx = jnp.min(jnp.where(lg == m, e_iota, N_EXPERTS), axis=0, keepdims=True)
        vals.append(m)
        idxs.append(idx)
        lg = jnp.where(e_iota == idx, NEG_INF, lg)
    exps = [jnp.exp(vk - vals[0]) for vk in vals]
    denom = exps[0] + exps[1] + exps[2] + exps[3]

    chosen = jnp.zeros((N_EXPERTS, tm), F32)
    for idx in idxs:
        chosen = chosen + jnp.where(e_iota == idx, 1.0, 0.0)
    chosen = chosen.astype(BF16)
    run = run_s[...]
    before = dot(chosen, upper_ref[...]) + jnp.concatenate([run] * (tm // LANES), axis=1)
    run_s[...] = run + dot(chosen, ones_ref[...])
    cnt_ref[...] = run_s[...]

    zero_i = jnp.zeros((SUBLANES - TOP_K, tm), I32)
    ranks = [jnp.sum(jnp.where(e_iota == idx, before, 0.0), axis=0, keepdims=True).astype(I32)
             for idx in idxs]
    idx_ref[0] = jnp.concatenate(idxs + [zero_i], axis=0)
    rank_ref[0] = jnp.concatenate(ranks + [zero_i], axis=0)
    wrows = jnp.concatenate([e / denom for e in exps] + [jnp.zeros((LANES - TOP_K, tm), F32)], axis=0)
    wgt_ref[...] = wrows.T


def _out_router(x2, ym, yr, ye, w_out, g_ffn, w_router, b_router, tm):
    t, d = x2.shape
    wo = w_out.astype(BF16)
    wo1, wo2, wo3 = wo[:MOBA_W], wo[MOBA_W:MOBA_W + RWKV_W], wo[MOBA_W + RWKV_W:]
    wr = jnp.zeros((d, LANES), F32).at[:, :N_EXPERTS].set(w_router)
    wr_hi = wr.astype(BF16)
    wr_lo = (wr - wr_hi.astype(F32)).astype(BF16)
    br = jnp.full((1, LANES), NEG_INF, F32).at[0, :N_EXPERTS].set(b_router)
    upper = jnp.asarray(np.triu(np.ones((tm, tm), np.float32), 1)).astype(BF16)
    ones = jnp.ones((tm, LANES), BF16)
    tile = lambda n: pl.BlockSpec((tm, n), lambda i: (i, 0))
    slots = pl.BlockSpec((1, SUBLANES, tm), lambda i: (i, 0, 0))
    const = lambda a: pl.BlockSpec(a.shape, lambda i: (0,) * a.ndim)
    g2 = g_ffn.reshape(1, d)
    n = t // tm
    return pl.pallas_call(
        _out_router_kernel,
        grid=(n,),
        in_specs=[tile(d), tile(MOBA_W), tile(RWKV_W), tile(MEM_W), const(wo1), const(wo2), const(wo3),
                  const(g2), const(wr_hi), const(wr_lo), const(br), const(upper), const(ones)],
        out_specs=[tile(d), tile(d // 2), slots, slots, tile(LANES),
                   pl.BlockSpec((N_EXPERTS, LANES), lambda i: (0, 0))],
        out_shape=[jax.ShapeDtypeStruct((t, d), F32), jax.ShapeDtypeStruct((t, d // 2), I32),
                   jax.ShapeDtypeStruct((n, SUBLANES, tm), I32), jax.ShapeDtypeStruct((n, SUBLANES, tm), I32),
                   jax.ShapeDtypeStruct((t, LANES), F32), jax.ShapeDtypeStruct((N_EXPERTS, LANES), F32)],
        scratch_shapes=[pltpu.VMEM((N_EXPERTS, LANES), F32)],
        compiler_params=_cparams(("arbitrary",)),
        name="out_router",
    )(x2, ym, yr, ye, wo1, wo2, wo3, g2, wr_hi, wr_lo, br, upper, ones)


def _sc_mesh():
    return plsc.VectorSubcoreMesh(core_axis_name="c", subcore_axis_name="s",
                                  num_cores=SC_CORES, num_subcores=SC_SUBCORES)


def _sc_split(n_rows):
    n_workers = SC_CORES * SC_SUBCORES
    per_worker = n_rows // n_workers
    assert per_worker * n_workers == n_rows and per_worker % (SC_ROWS * SC_INFLIGHT) == 0
    assert SC_ROWS % SC_ALIGN == 0
    return per_worker


def _sc_gather_rows(table, idx):
    n_rows = idx.shape[0]
    width = table.shape[1]
    per_worker = _sc_split(n_rows)

    buf = lambda shape, dtype: [pltpu.VMEM(shape, dtype) for _ in range(SC_INFLIGHT)]

    @functools.partial(
        pl.kernel, mesh=_sc_mesh(),
        out_type=jax.ShapeDtypeStruct((n_rows, width), table.dtype),
        scratch_types=[buf((SC_ROWS,), I32), buf((SC_ROWS, width), table.dtype),
                       [pltpu.SemaphoreType.DMA for _ in range(SC_INFLIGHT)]],
        name="sc_gather",
    )
    def gather(table_hbm, idx_hbm, out_hbm, idx_v, rows_v, sems):
        worker = lax.axis_index("s") * SC_CORES + lax.axis_index("c")
        base = worker * per_worker

        @pl.loop(0, per_worker // (SC_ROWS * SC_INFLIGHT))
        def _(step):
            offs = [pl.multiple_of(base + (step * SC_INFLIGHT + b) * SC_ROWS, SC_ALIGN) for b in range(SC_INFLIGHT)]
            copies = []
            for b in range(SC_INFLIGHT):
                pltpu.sync_copy(idx_hbm.at[pl.ds(offs[b], SC_ROWS)], idx_v[b])
                copies.append(pltpu.async_copy(table_hbm.at[idx_v[b]], rows_v[b], sems[b]))
            for b in range(SC_INFLIGHT):
                copies[b].wait()
                pltpu.sync_copy(rows_v[b], out_hbm.at[pl.ds(offs[b], SC_ROWS)])

    return gather(table, idx)


def _sc_scatter_rows(rows, idx, n_out):
    n_src, width = rows.shape
    n_copies = idx.shape[0] // n_src
    assert n_copies * n_src == idx.shape[0]
    per_worker = _sc_split(n_src)

    @functools.partial(
        pl.kernel, mesh=_sc_mesh(),
        out_type=jax.ShapeDtypeStruct((n_out, width), rows.dtype),
        scratch_types=[[pltpu.VMEM((SC_ROWS,), I32) for _ in range(n_copies)],
                       pltpu.VMEM((SC_ROWS, width), rows.dtype),
                       [pltpu.SemaphoreType.DMA for _ in range(n_copies)]],
        name="sc_scatter",
    )
    def scatter(rows_hbm, idx_hbm, out_hbm, idx_v, rows_v, sems):
        worker = lax.axis_index("s") * SC_CORES + lax.axis_index("c")
        base = worker * per_worker

        @pl.loop(0, per_worker // SC_ROWS)
        def _(step):
            src = pl.multiple_of(base + step * SC_ROWS, SC_ALIGN)
            pltpu.sync_copy(rows_hbm.at[pl.ds(src, SC_ROWS)], rows_v)
            copies = []
            for cpy in range(n_copies):
                off = pl.multiple_of(cpy * n_src + src, SC_ALIGN)
                pltpu.sync_copy(idx_hbm.at[pl.ds(off, SC_ROWS)], idx_v[cpy])
                copies.append(pltpu.async_copy(rows_v, out_hbm.at[idx_v[cpy]], sems[cpy]))
            for cp in copies:
                cp.wait()

    return scatter(rows, idx)


def _experts_kernel(be_ref, bv_ref, nused_ref, next_ref, slot_ref,
                    xs_ref, wgu_hbm, bg_ref, bu_ref, wd_hbm, bd_ref, perm_ref,
                    o_ref, wgu_buf, wd_buf, wg_s, wu_s, wd_s, sem):
    i = pl.program_id(0)
    e = be_ref[i]
    used = i < nused_ref[0]
    changed = ((i == 0) | (e != be_ref[jnp.maximum(i - 1, 0)])) & used
    slot = slot_ref[e]

    def weight_copies(expert, buf_slot):
        return (pltpu.make_async_copy(wgu_hbm.at[expert], wgu_buf.at[buf_slot], sem.at[buf_slot, 0]),
                pltpu.make_async_copy(wd_hbm.at[expert], wd_buf.at[buf_slot], sem.at[buf_slot, 1]))

    @pl.when((i == 0) & used)
    def _():
        for cp in weight_copies(e, slot):
            cp.start()

    @pl.when(changed)
    def _():
        for cp in weight_copies(e, slot):
            cp.wait()
        nxt = next_ref[e]

        @pl.when(nxt >= 0)
        def _():
            for cp in weight_copies(nxt, 1 - slot):
                cp.start()

        half = LANES
        for cblk in range(2 * D_EXPERT // (2 * half)):
            wt = wgu_buf[slot, :, cblk * 2 * half:(cblk + 1) * 2 * half].astype(BF16)
            sep = jnp.dot(wt, perm_ref[...], preferred_element_type=F32).astype(BF16)
            wg_s[:, cblk * half:(cblk + 1) * half] = sep[:, :half]
            wu_s[:, cblk * half:(cblk + 1) * half] = sep[:, half:]
        wd_s[...] = wd_buf[slot].astype(BF16)

    @pl.when(used)
    def _():
        rows = lax.broadcasted_iota(I32, xs_ref.shape, 0)
        xb = _unpack_halves(jnp.where(rows < bv_ref[i], xs_ref[...], 0)).astype(BF16)
        gate = jnp.dot(xb, wg_s[...], preferred_element_type=F32) + bg_ref[0]
        up = jnp.dot(xb, wu_s[...], preferred_element_type=F32) + bu_ref[0]
        gate = jnp.minimum(gate, SWIGLU_LIMIT)
        up = jnp.clip(up, -SWIGLU_LIMIT, SWIGLU_LIMIT)
        glu = gate * _sigmoid(gate * SWIGLU_ALPHA)
        act = ((up + 1.0) * glu).astype(BF16)
        o_ref[...] = _pack_halves(jnp.dot(act, wd_s[...], preferred_element_type=F32) + bd_ref[0])

    @pl.when(jnp.logical_not(used))
    def _():
        o_ref[...] = jnp.zeros_like(o_ref)


def _experts(blk_expert, blk_valid, n_used, next_expert, buf_slot, xs, w_gate_up, b_gate_up, w_down,
             b_down):
    p_rows = xs.shape[0]
    d = 2 * xs.shape[1]
    n_blocks = p_rows // EXPERT_ROWS
    bg = b_gate_up[:, 0::2].reshape(N_EXPERTS, 1, D_EXPERT)
    bu = b_gate_up[:, 1::2].reshape(N_EXPERTS, 1, D_EXPERT)
    bd = b_down.reshape(N_EXPERTS, 1, d)
    perm_np = np.zeros((2 * LANES, 2 * LANES), np.float32)
    perm_np[2 * np.arange(LANES), np.arange(LANES)] = 1.0
    perm_np[2 * np.arange(LANES) + 1, LANES + np.arange(LANES)] = 1.0
    perm = jnp.asarray(perm_np).astype(BF16)
    by_expert = lambda shape: pl.BlockSpec((1,) + shape, lambda i, be, *_: (be[i], 0, 0))
    grid_spec = pltpu.PrefetchScalarGridSpec(
        num_scalar_prefetch=5,
        grid=(n_blocks,),
        in_specs=[pl.BlockSpec((EXPERT_ROWS, d // 2), lambda i, *_: (i, 0)),
                  pl.BlockSpec(memory_space=pl.ANY), by_expert((1, D_EXPERT)), by_expert((1, D_EXPERT)),
                  pl.BlockSpec(memory_space=pl.ANY), by_expert((1, d)),
                  pl.BlockSpec(perm.shape, lambda i, *_: (0, 0))],
        out_specs=pl.BlockSpec((EXPERT_ROWS, d // 2), lambda i, *_: (i, 0)),
        scratch_shapes=[pltpu.VMEM((2, d, 2 * D_EXPERT), F32), pltpu.VMEM((2, D_EXPERT, d), F32),
                        pltpu.VMEM((d, D_EXPERT), BF16), pltpu.VMEM((d, D_EXPERT), BF16),
                        pltpu.VMEM((D_EXPERT, d), BF16), pltpu.SemaphoreType.DMA((2, 2))],
    )
    return pl.pallas_call(
        _experts_kernel,
        grid_spec=grid_spec,
        out_shape=jax.ShapeDtypeStruct((p_rows, d // 2), I32),
        compiler_params=_cparams(("arbitrary",)),
        name="experts",
    )(blk_expert, blk_valid, n_used, next_expert, buf_slot, xs, w_gate_up, bg, bu, w_down, bd, perm)


def _combine_kernel(yg_ref, x1_ref, w_ref, g_ref, o_ref):
    acc = x1_ref[...]
    wts = w_ref[...]
    for kk in range(TOP_K):
        acc = acc + _unpack_halves(yg_ref[kk]) * wts[:, kk:kk + 1]
    o_ref[...] = _rms(acc, g_ref[...])


def _combine(yg, stream, wts, g_final, tb, first_tile):
    t, d = stream.shape
    half = yg.shape[2]
    tile = lambda n: pl.BlockSpec((tb, n), lambda i: (i + first_tile, 0))
    return pl.pallas_call(
        _combine_kernel,
        grid=(yg.shape[1] // tb,),
        in_specs=[pl.BlockSpec((TOP_K, tb, half), lambda i: (0, i, 0)), tile(d), tile(LANES),
                  pl.BlockSpec((1, d), lambda i: (0, 0))],
        out_specs=tile(d),
        out_shape=jax.ShapeDtypeStruct((t, d), F32),
        input_output_aliases={1: 0},
        compiler_params=_cparams(("parallel",)),
        name="combine",
    )(yg, stream, wts, g_final.reshape(1, d))


def _layer(x, mem, w_in, w_out, w_mem_kv, g_mix, g_mem, g_ffn, bias_tables, mu, w0, w_up, a0, a_up,
           g_up, k_k, k_a, r_k, gn_g, gn_b, w_router, b_router, w_gate_up, b_gate_up, w_down, b_down,
           g_last):
    b, s, d = x.shape
    m = mem.shape[1]
    t = b * s
    x2 = x.reshape(t, d)

    p = _norm_matmul(x2, g_mix, w_in.astype(BF16), PROJ_TOKENS, "in_proj")
    p3 = p.reshape(b, s, IN_COLS)
    y_moba = _moba_attention(p3, *bias_tables)
    y_rwkv = _rwkv(p3, mu, w0, w_up, a0, a_up, g_up, k_k, k_a, r_k, gn_g, gn_b)
    mkv = _norm_matmul(mem.reshape(b * m, d), g_mem, w_mem_kv.astype(BF16), PROJ_TOKENS, "mem_kv")
    y_mem = _mem_attention(p3, mkv.reshape(b, m, 2 * MEM_W), MEM_Q_TOKENS)

    tb = ROUTE_TOKENS
    x1, h2, idx_o, rank_o, wgt_p, cnt = _out_router(
        x2, y_moba.reshape(t, MOBA_W), y_rwkv.reshape(t, RWKV_W), y_mem.reshape(t, MEM_W),
        w_out, g_ffn, w_router, b_router, tb)

    counts = cnt[:, 0].astype(I32)
    padded = (counts + EXPERT_ROWS - 1) // EXPERT_ROWS * EXPERT_ROWS
    pad_ends = jnp.cumsum(padded)
    pad_starts = (pad_ends - padded).astype(I32)
    n_blocks = (t * TOP_K) // EXPERT_ROWS + N_EXPERTS
    blk_start = jnp.arange(n_blocks, dtype=I32) * EXPERT_ROWS
    blk_expert = jnp.minimum(jnp.sum(blk_start[:, None] >= pad_ends[None, :], axis=1),
                             N_EXPERTS - 1).astype(I32)
    of_block = blk_expert[:, None] == jnp.arange(N_EXPERTS, dtype=I32)[None, :]
    rows_left = jnp.sum(jnp.where(of_block, (counts + pad_starts)[None, :], 0), axis=1) - blk_start
    blk_valid = jnp.clip(rows_left, 0, EXPERT_ROWS)
    n_used = (pad_ends[-1:] // EXPERT_ROWS).astype(I32)
    has_rows = counts > 0
    first_from = lax.cummin(jnp.where(has_rows, jnp.arange(N_EXPERTS, dtype=I32), N_EXPERTS), reverse=True)
    next_expert = jnp.concatenate([first_from[1:], jnp.full((1,), N_EXPERTS, I32)])
    next_expert = jnp.where(next_expert < N_EXPERTS, next_expert, -1).astype(I32)
    buf_slot = ((jnp.cumsum(has_rows.astype(I32)) - has_rows.astype(I32)) % 2).astype(I32)
    idx_kt = jnp.swapaxes(idx_o[:, :TOP_K, :], 0, 1).reshape(TOP_K, t)
    rank_kt = jnp.swapaxes(rank_o[:, :TOP_K, :], 0, 1).reshape(TOP_K, t)
    experts = jnp.arange(N_EXPERTS, dtype=I32)[:, None, None]
    start_kt = jnp.sum(jnp.where(idx_kt[None] == experts, pad_starts[:, None, None], 0), axis=0)
    dest = (start_kt + rank_kt).reshape(TOP_K * t)

    xs = _sc_scatter_rows(h2, dest, n_blocks * EXPERT_ROWS)
    ys = _experts(blk_expert, blk_valid.astype(I32), n_used, next_expert, buf_slot, xs,
                  w_gate_up, b_gate_up, w_down, b_down)
    dest_kt = dest.reshape(TOP_K, t)
    out = x1
    for part in range(COMBINE_PARTS):
        t_part = t // COMBINE_PARTS
        rows = dest_kt[:, part * t_part:(part + 1) * t_part].reshape(TOP_K * t_part)
        yg = _sc_gather_rows(ys, rows).reshape(TOP_K, t_part, d // 2)
        out = _combine(yg, out, wgt_p, g_last, tb, part * (t_part // tb))
    return out.reshape(b, s, d)


def kernel(x, mem, w_in, w_out, w_mem_kv, g_mix, g_mem, g_ffn, g_final, rel_bias, rwkv_mu, rwkv_w0,
           rwkv_w_up, rwkv_a0, rwkv_a_up, rwkv_g_up, rwkv_k_k, rwkv_k_a, rwkv_r_k, rwkv_gn_g, rwkv_gn_b,
           w_router, b_router, w_gate_up, b_gate_up, w_down, b_down):
    depth = w_in.shape[0]
    assert depth == 1, "the final norm is fused into the last layer's combine kernel"
    bias_tables = _moba_bias_tables(rel_bias)
    l = 0
    return _layer(x, mem, w_in[l], w_out[l], w_mem_kv[l], g_mix[l], g_mem[l], g_ffn[l], bias_tables,
                  rwkv_mu[l], rwkv_w0[l], rwkv_w_up[l], rwkv_a0[l], rwkv_a_up[l], rwkv_g_up[l],
                  rwkv_k_k[l], rwkv_k_a[l], rwkv_r_k[l], rwkv_gn_g[l], rwkv_gn_b[l], w_router[l],
                  b_router[l], w_gate_up[l], b_gate_up[l], w_down[l], b_down[l], g_final)
```

```python
import functools
import math

import numpy as np
import jax
import jax.numpy as jnp
from jax import lax
from jax.experimental import pallas as pl
from jax.experimental.pallas import tpu as pltpu
from jax.experimental.pallas import tpu_sc as plsc

F32 = jnp.float32
BF16 = jnp.bfloat16
I32 = jnp.int32
HI = lax.Precision.HIGHEST

D_MODEL = 1024
HEAD_DIM = 64
MOBA_HEADS = 6
RWKV_HEADS = 6
MEM_HEADS = 4
MOBA_W = MOBA_HEADS * HEAD_DIM
RWKV_W = RWKV_HEADS * HEAD_DIM
MEM_W = MEM_HEADS * HEAD_DIM
MOBA_BLOCK = 256
MOBA_TOPK = 3
N_BUCKETS = 32
MAX_DISTANCE = 128
DECAY_LORA = 64
AAA_LORA = 64
GATE_LORA = 128
RWKV_COLS = 3 * RWKV_W + DECAY_LORA + AAA_LORA + GATE_LORA
RWKV_GN_EPS = 64e-5
IN_COLS = 3 * MOBA_W + RWKV_COLS + MEM_W
N_EXPERTS = 32
TOP_K = 4
D_EXPERT = D_MODEL
SWIGLU_ALPHA = 1.702
SWIGLU_LIMIT = 7.0
RMS_EPS = 1e-5

LANES = 128
SUBLANES = 8
BF16_SUBLANES = 16
PROJ_TOKENS = 512
MEM_Q_TOKENS = 512
RWKV_CHUNK = 128
RWKV_SEQS_PER_STEP = 4
EXPERT_ROWS = 256
ROUTE_TOKENS = 512
SC_CORES = 2
SC_SUBCORES = 16
SC_ROWS = 64
SC_ALIGN = 8
SC_INFLIGHT = 2
COMBINE_PARTS = 4
VMEM_LIMIT = 56 * 1024 * 1024
NEG_INF = float("-inf")
HIGH_HALF = -65536
LOG2_E = 1.4426950408889634
MOBA_VROWS = HEAD_DIM + BF16_SUBLANES


def _cparams(sem):
    return pltpu.CompilerParams(dimension_semantics=sem, vmem_limit_bytes=VMEM_LIMIT)


def _rms(x, g):
    return x * lax.rsqrt(jnp.mean(x * x, axis=-1, keepdims=True) + RMS_EPS) * g


def _bdot(a, b):
    return jnp.dot(a.astype(BF16), b.astype(BF16), preferred_element_type=F32)


def _hdot(a, b):
    return jnp.dot(a, b, preferred_element_type=F32, precision=HI)


def _dot_nt(a, b):
    return lax.dot_general(a, b, (((1,), (1,)), ((), ())), preferred_element_type=F32)


def _dot_tn(a, b):
    return lax.dot_general(a, b, (((0,), (0,)), ((), ())), preferred_element_type=F32)


def _split_bf16(x, terms):
    parts = []
    for _ in range(terms):
        hi = x.astype(BF16)
        parts.append(hi)
        x = x - hi.astype(F32)
    return parts


def _dot_exact_rhs(x, m_bf16, terms):
    acc = None
    for part in _split_bf16(x, terms):
        d = jnp.dot(part, m_bf16, preferred_element_type=F32)
        acc = d if acc is None else acc + d
    return acc


def _pack_halves(x):
    n = x.shape[1] // 2
    lo = pltpu.bitcast(x[:, :n].astype(BF16).astype(F32), I32)
    hi = pltpu.bitcast(x[:, n:].astype(BF16).astype(F32), I32)
    return (hi & HIGH_HALF) | lax.shift_right_logical(lo, 16)


def _unpack_halves(w):
    lo = pltpu.bitcast(w << 16, F32)
    hi = pltpu.bitcast(w & HIGH_HALF, F32)
    return jnp.concatenate([lo, hi], axis=1)


def _norm_matmul_kernel(x_ref, g_ref, w_ref, o_ref):
    h = _rms(x_ref[...], g_ref[...])
    o_ref[...] = jnp.dot(h.astype(BF16), w_ref[...], preferred_element_type=F32)


def _norm_matmul(x, g, w_bf16, tm, name):
    t, d = x.shape
    n = w_bf16.shape[1]
    tm = min(tm, t)
    return pl.pallas_call(
        _norm_matmul_kernel,
        grid=(t // tm,),
        in_specs=[pl.BlockSpec((tm, d), lambda i: (i, 0)),
                  pl.BlockSpec((1, d), lambda i: (0, 0)),
                  pl.BlockSpec((d, n), lambda i: (0, 0))],
        out_specs=pl.BlockSpec((tm, n), lambda i: (i, 0)),
        out_shape=jax.ShapeDtypeStruct((t, n), F32),
        compiler_params=_cparams(("parallel",)),
        name=name,
    )(x, g.reshape(1, d), w_bf16)


def _in_proj_kernel(x_ref, g_ref, w_ref, moba_ref, rwkv_ref, memq_ref):
    h = _rms(x_ref[...], g_ref[...])
    p = jnp.dot(h.astype(BF16), w_ref[...], preferred_element_type=F32)
    for j in range(3 * MOBA_W // LANES):
        moba_ref[j] = p[:, j * LANES:(j + 1) * LANES]
    rwkv_ref[...] = p[:, 3 * MOBA_W:3 * MOBA_W + RWKV_COLS]
    memq_ref[...] = p[:, 3 * MOBA_W + RWKV_COLS:]


def _in_proj(x, g, w_bf16, tm):
    t, d = x.shape
    n_slabs = 3 * MOBA_W // LANES
    return pl.pallas_call(
        _in_proj_kernel,
        grid=(t // tm,),
        in_specs=[pl.BlockSpec((tm, d), lambda i: (i, 0)),
                  pl.BlockSpec((1, d), lambda i: (0, 0)),
                  pl.BlockSpec((d, IN_COLS), lambda i: (0, 0))],
        out_specs=[pl.BlockSpec((n_slabs, tm, LANES), lambda i: (0, i, 0)),
                   pl.BlockSpec((tm, RWKV_COLS), lambda i: (i, 0)),
                   pl.BlockSpec((tm, MEM_W), lambda i: (i, 0))],
        out_shape=[jax.ShapeDtypeStruct((n_slabs, t, LANES), F32),
                   jax.ShapeDtypeStruct((t, RWKV_COLS), F32),
                   jax.ShapeDtypeStruct((t, MEM_W), F32)],
        compiler_params=_cparams(("parallel",)),
        name="in_proj",
    )(x, g.reshape(1, d), w_bf16)


def _t5_bucket_np(dist):
    n = np.maximum(dist, 0)
    max_exact = N_BUCKETS // 2
    nf = np.maximum(n, 1).astype(np.float64)
    large = max_exact + (np.log(nf / max_exact) / math.log(MAX_DISTANCE / max_exact)
                         * (N_BUCKETS - max_exact)).astype(np.int64)
    large = np.minimum(large, N_BUCKETS - 1)
    return np.where(n < max_exact, n, large)


def _moba_bias_tables(rel_bias):
    n = MOBA_BLOCK
    assert np.all(_t5_bucket_np(np.arange(n + 1, 64 * n)) == N_BUCKETS - 1)
    bias_t = rel_bias.astype(F32).T * LOG2_E

    def by_distance(dist):
        bucket = jnp.asarray(_t5_bucket_np(dist), I32)[None]
        tab = jnp.zeros((MOBA_HEADS, dist.shape[0]), F32)
        for b in range(N_BUCKETS):
            tab = jnp.where(bucket == b, bias_t[:, b][:, None], tab)
        return tab

    d = np.arange(2 * n)
    g_own = jnp.where(jnp.asarray(d < n)[None], by_distance(np.where(d < n, d, 0)), NEG_INF)
    g_prev = by_distance(np.where(d < n, n + d, d - n))
    far = bias_t[:, N_BUCKETS - 1]
    return g_own[:, None, :], g_prev[:, None, :], far


def _moba_kernel(far_ref, q_ref, k_ref, v_ref, g0_ref, g1_ref, o_ref,
                 qt_s, kb_s, vt_s, sc_s, t0_ref, t1_ref, *, nb):
    hp = pl.program_id(1)
    blk = MOBA_BLOCK
    scale = HEAD_DIM ** -0.5 * LOG2_E
    row_head = lax.broadcasted_iota(I32, (LANES, 1), 0) // HEAD_DIM
    lane_head = lax.broadcasted_iota(I32, (1, LANES), 1) // HEAD_DIM
    ones_rows = jnp.ones((MOBA_VROWS - HEAD_DIM, blk), BF16)

    for h in range(2):
        for g_ref, t_ref in ((g0_ref, t0_ref), (g1_ref, t1_ref)):
            rows = jnp.broadcast_to(g_ref[h], (blk, 2 * blk))
            t_ref[h] = pltpu.roll(rows, 0, 1, stride=1, stride_axis=0)[:, :blk]

    kmean_rows = []
    for j in range(nb):
        sl = pl.ds(j * blk, blk)
        qt_s[j] = (q_ref[0, sl, :] * scale).T
        vt = v_ref[0, sl, :].T.astype(BF16)
        for h in range(2):
            vt_s[j, h * MOBA_VROWS:h * MOBA_VROWS + HEAD_DIM, :] = vt[h * HEAD_DIM:(h + 1) * HEAD_DIM, :]
            vt_s[j, h * MOBA_VROWS + HEAD_DIM:(h + 1) * MOBA_VROWS, :] = ones_rows
        kj = k_ref[0, sl, :]
        kb_s[j] = kj.astype(BF16)
        kmean_rows.append(jnp.mean(kj, axis=0, keepdims=True))
    kmean = jnp.concatenate(kmean_rows, axis=0)
    km2 = jnp.concatenate([jnp.where(lane_head == 0, kmean, 0.0),
                           jnp.where(lane_head == 1, kmean, 0.0)], axis=0)
    blk_iota = lax.broadcasted_iota(I32, (nb, blk), 0)

    def score_pass(i):
        qt = qt_s[i]
        negs = [None, None]
        if i > MOBA_TOPK:
            gate2 = _hdot(km2, qt)
            for h in range(2):
                g = gate2[h * nb:(h + 1) * nb, :]
                cnt = jnp.zeros((nb, blk), I32)
                for m in range(i):
                    gm = g[m:m + 1, :]
                    beats = (gm > g) | ((gm == g) & (m < blk_iota))
                    cnt = cnt + jnp.where(beats, 1, 0)
                negs[h] = jnp.where(cnt < MOBA_TOPK, 0.0, NEG_INF)
        col_max = []
        for h in range(2):
            qt_h = jnp.where(row_head == h, qt, 0.0).astype(BF16)
            far_bias = far_ref[2 * hp + h]
            m_run = None
            for j in range(i + 1):
                s = jnp.dot(kb_s[j], qt_h, preferred_element_type=F32)
                if j == i:
                    s = s + t0_ref[h]
                else:
                    bias = t1_ref[h] if j == i - 1 else far_bias
                    if negs[h] is not None:
                        bias = bias + negs[h][j:j + 1, :]
                    s = s + bias
                sc_s[i % 2, h, j] = s
                cm = jnp.max(s, axis=0, keepdims=True)
                m_run = cm if m_run is None else jnp.maximum(m_run, cm)
            col_max.append(m_run)
        return col_max

    def prob_pass(i, col_max):
        outs = []
        for h in range(2):
            m_run = col_max[h]
            probs = jnp.concatenate([jnp.exp2(sc_s[i % 2, h, j] - m_run).astype(BF16)
                                     for j in range(i + 1)], axis=0)
            vals = jnp.concatenate([vt_s[j, h * MOBA_VROWS:(h + 1) * MOBA_VROWS, :]
                                    for j in range(i + 1)], axis=1)
            acc = jnp.dot(vals, probs, preferred_element_type=F32)
            outs.append(acc[:HEAD_DIM] / acc[HEAD_DIM:HEAD_DIM + 1])
        o_t = jnp.concatenate(outs, axis=0)
        o_ref[0, pl.ds(i * blk, blk), :] = o_t.T

    pending = None
    for i in range(nb):
        col_max = score_pass(i)
        if pending is not None:
            prob_pass(*pending)
        pending = (i, col_max)
    prob_pass(*pending)


def _moba_attention(p3, g_own, g_prev, far, b):
    s = p3.shape[1]
    nb = s // MOBA_BLOCK
    n_pairs = MOBA_HEADS // 2
    blk_spec = lambda off: pl.BlockSpec((1, s, LANES), lambda bi, hp: ((off + hp) * b + bi, 0, 0))
    tab_spec = pl.BlockSpec((2, 1, 2 * MOBA_BLOCK), lambda bi, hp: (hp, 0, 0))
    return pl.pallas_call(
        functools.partial(_moba_kernel, nb=nb),
        grid=(b, n_pairs),
        in_specs=[pl.BlockSpec(memory_space=pltpu.SMEM),
                  blk_spec(0), blk_spec(n_pairs), blk_spec(2 * n_pairs), tab_spec, tab_spec],
        out_specs=pl.BlockSpec((1, s, LANES), lambda bi, hp: (bi, 0, hp)),
        out_shape=jax.ShapeDtypeStruct((b, s, MOBA_W), F32),
        scratch_shapes=[pltpu.VMEM((nb, LANES, MOBA_BLOCK), F32),
                        pltpu.VMEM((nb, MOBA_BLOCK, LANES), BF16),
                        pltpu.VMEM((nb, 2 * MOBA_VROWS, MOBA_BLOCK), BF16),
                        pltpu.VMEM((2, 2, nb, MOBA_BLOCK, MOBA_BLOCK), F32),
                        pltpu.VMEM((2, MOBA_BLOCK, MOBA_BLOCK), F32),
                        pltpu.VMEM((2, MOBA_BLOCK, MOBA_BLOCK), F32)],
        compiler_params=_cparams(("parallel", "parallel")),
        name="moba",
    )(far, p3, p3, p3, g_own, g_prev)


def _softplus(z):
    return jnp.maximum(z, 0.0) + jnp.log(1.0 + jnp.exp(-jnp.abs(z)))


def _sigmoid(z):
    return 1.0 / (1.0 + jnp.exp(-z))


def _rwkv_kernel(r_ref, k_ref, v_ref, wa_ref, g_ref,
                 mu_r_ref, mu_k_ref, mu_v_ref, mu_wa_ref, mu_g_ref,
                 w0_ref, wup_ref, a0_ref, aup_ref, gup_ref, kk_ref, ka_ref, rk_ref,
                 gng_ref, gnb_ref, hsum_ref, tri_ref,
                 o_ref, st_s, prev_r, prev_k, prev_v, prev_wa, prev_g, *, n_seq):
    c = RWKV_CHUNK
    mid = c // 2

    @pl.when(pl.program_id(1) == 0)
    def _():
        st_s[...] = jnp.zeros_like(st_s)
        prev_r[...] = jnp.zeros_like(prev_r)
        prev_k[...] = jnp.zeros_like(prev_k)
        prev_v[...] = jnp.zeros_like(prev_v)
        prev_wa[...] = jnp.zeros_like(prev_wa)
        prev_g[...] = jnp.zeros_like(prev_g)

    n_pairs = RWKV_HEADS // 2
    hsum = hsum_ref[...]
    tri = tri_ref[...]
    row = lax.broadcasted_iota(I32, (c, c), 0)
    col = lax.broadcasted_iota(I32, (c, c), 1)
    strict = row > col
    incl = row >= col
    lane_half = lax.broadcasted_iota(I32, (1, LANES), 1) // HEAD_DIM
    hmask = hsum.astype(F32)
    dot = functools.partial(jnp.dot, preferred_element_type=F32)
    lanes = lambda x, jp: x[:, jp * LANES:(jp + 1) * LANES]

    def group_sum(x):
        return jnp.concatenate([_dot_exact_rhs(lanes(x, jp), hsum, 2) for jp in range(n_pairs)], axis=1)

    def two_heads(x):
        return jnp.concatenate([jnp.where(lane_half == 0, x, 0.0), jnp.where(lane_half == 1, x, 0.0)],
                               axis=0).astype(BF16)

    def side(a, b):
        return jnp.concatenate([a, b], axis=1)

    def mix(g, x_ref, prev_ref, mu_ref):
        x = x_ref[g]
        rows = lax.broadcasted_iota(I32, x.shape, 0)
        shifted = jnp.where(rows == 0, prev_ref[g], pltpu.roll(x, 1, axis=0))
        prev_ref[g] = x[c - 1:c, :]
        return x + (shifted - x) * mu_ref[...]

    def prepare(g):
        r = mix(g, r_ref, prev_r, mu_r_ref)
        k = mix(g, k_ref, prev_k, mu_k_ref)
        v = mix(g, v_ref, prev_v, mu_v_ref)
        xwa = mix(g, wa_ref, prev_wa, mu_wa_ref)
        xg = mix(g, g_ref, prev_g, mu_g_ref)

        w = -_softplus(-(w0_ref[...] + _bdot(jnp.tanh(xwa), wup_ref[...]))) - 0.5
        logd = -jnp.exp(w)
        a = _sigmoid(a0_ref[...] + _bdot(xwa, aup_ref[...]))
        gate = _bdot(_sigmoid(xg), gup_ref[...])
        kk = k * kk_ref[...]
        kk = kk / jnp.maximum(jnp.sqrt(group_sum(kk * kk)), 1e-12)
        k2 = k * (1.0 + (a - 1.0) * ka_ref[...])

        parts = _split_bf16(logd, 3)
        cum = dot(tri, parts[0]) + dot(tri, parts[1]) + dot(tri, parts[2])
        cum_last = cum[c - 1:c, :]
        ref = cum[mid - 1:mid, :]
        rel = cum - ref
        g_in = jnp.exp(rel)
        g_inv = jnp.exp(-rel)
        g_tail = jnp.exp(cum_last - cum)
        a_t = -kk * jnp.exp(rel - logd)
        b_t = kk * a * g_inv
        k_t = k2 * g_inv
        r_t = r * g_in
        b_hat = kk * a * g_tail
        k_hat = k2 * g_tail

        return dict(r=r, k2=k2, v=v, gate=gate, b_hat=b_hat, k_hat=k_hat, ref_scale=jnp.exp(ref),
                    decay=jnp.exp(cum_last), bk=jnp.concatenate([b_t, k_t], axis=0).astype(BF16),
                    ar_all=jnp.concatenate([a_t, r_t], axis=0))

    def chunk_products(g, el):
        v, b_hat, k_hat, ref_scale, decay = el["v"], el["b_hat"], el["k_hat"], el["ref_scale"], el["decay"]
        bk, ar_all = el["bk"], el["ar_all"]
        per_pair = []
        for jp in range(n_pairs):
            st = st_s[g, jp]
            st_ref = (st * lanes(ref_scale, jp)).astype(BF16)
            both = _dot_nt(lanes(ar_all, jp).astype(BF16), st_ref)
            quads = []
            for half in range(2):
                ar = jnp.where(lane_half == half, lanes(ar_all, jp), 0.0)
                m4 = _dot_nt(ar.astype(BF16), lanes(bk, jp))
                quads.append((jnp.where(strict, m4[:c, :c], 0.0),
                              jnp.where(strict, m4[:c, c:], 0.0).astype(BF16),
                              jnp.where(incl, m4[c:, :c], 0.0).astype(BF16),
                              jnp.where(incl, m4[c:, c:], 0.0).astype(BF16)))
            per_pair.append(dict(
                st=st, p0=both[:c], o0=both[c:], v_rows=two_heads(lanes(v, jp)),
                l_ab=[q[0] for q in quads], l_ak=side(quads[0][1], quads[1][1]),
                m_rb=side(quads[0][2], quads[1][2]), m_rk=side(quads[0][3], quads[1][3]),
                v=lanes(v, jp).astype(BF16), b_hat=lanes(b_hat, jp).astype(BF16),
                k_hat=lanes(k_hat, jp).astype(BF16), decay=lanes(decay, jp)))
        return dict(r=el["r"], k2=el["k2"], v=v, gate=el["gate"], pairs=per_pair)

    seqs = {}

    def solve(group):
        heads = [(g, jp, half) for g in group for jp in range(n_pairs) for half in range(2)]
        pair_ids = [(g, jp) for g in group for jp in range(n_pairs)]

        pw = {gh: seqs[gh[0]]["pairs"][gh[1]]["l_ab"][gh[2]].astype(BF16) for gh in heads}
        u_loc = {(g, jp): seqs[g]["pairs"][jp]["p0"]
                 + dot(seqs[g]["pairs"][jp]["l_ak"], seqs[g]["pairs"][jp]["v_rows"]) for g, jp in pair_ids}
        span = 1
        while span < c:
            u_loc = {(g, jp): u_loc[(g, jp)]
                     + dot(side(pw[(g, jp, 0)], pw[(g, jp, 1)]), two_heads(u_loc[(g, jp)]))
                     for g, jp in pair_ids}
            span *= 2
            if span < c:
                pw = {gh: dot(pw[gh], pw[gh]).astype(BF16) for gh in heads}

        for g in group:
            sq = seqs[g]
            outs = []
            for jp in range(n_pairs):
                pr = sq["pairs"][jp]
                u = u_loc[(g, jp)]
                outs.append(pr["o0"] + dot(pr["m_rb"], two_heads(u)) + dot(pr["m_rk"], pr["v_rows"]))
                upd = _dot_tn(u.astype(BF16), pr["b_hat"]) + _dot_tn(pr["v"], pr["k_hat"])
                st_s[g, jp] = pr["st"] * pr["decay"] + hmask * upd
            o = jnp.concatenate(outs, axis=1)

            inv_n = 1.0 / HEAD_DIM
            mean = group_sum(o) * inv_n
            dev = o - mean
            var = group_sum(dev * dev) * inv_n
            y = dev * lax.rsqrt(var + RWKV_GN_EPS) * gng_ref[...] + gnb_ref[...]
            y = y + group_sum(sq["r"] * sq["k2"] * rk_ref[...]) * sq["v"]
            o_ref[g] = y * sq["gate"]

    elementwise = None
    for g in range(n_seq):
        nxt = prepare(g)
        if elementwise is not None:
            seqs[g - 1] = chunk_products(g - 1, elementwise)
        elementwise = nxt
    seqs[n_seq - 1] = chunk_products(n_seq - 1, elementwise)
    solve(list(range(n_seq)))


def _rwkv(p3, mu, w0, w_up, a0, a_up, g_up, k_k, k_a, r_k, gn_g, gn_b):
    b, s, _ = p3.shape
    c = RWKV_CHUNK
    w = RWKV_W
    n_seq = RWKV_SEQS_PER_STEP if b % RWKV_SEQS_PER_STEP == 0 else 1
    base = 0
    assert (base + 3 * w) % LANES == 0
    lora = DECAY_LORA + AAA_LORA

    def col_spec(width, off_cols):
        assert off_cols % width == 0
        return pl.BlockSpec((n_seq, c, width), lambda bi, ci: (bi, ci, off_cols // width))

    row = lambda x: x.reshape(1, -1).astype(F32)
    const = lambda shape: pl.BlockSpec(shape, lambda bi, ci: (0,) * len(shape))
    wup_pad = jnp.concatenate([w_up, jnp.zeros((AAA_LORA, w), F32)], axis=0).astype(BF16)
    aup_pad = jnp.concatenate([jnp.zeros((DECAY_LORA, w), F32), a_up], axis=0).astype(BF16)
    head = np.arange(LANES) // HEAD_DIM
    hsum = jnp.asarray((head[:, None] == head[None, :]).astype(np.float32)).astype(BF16)
    tri = jnp.asarray(np.tril(np.ones((c, c), np.float32))).astype(BF16)
    vec_args = [mu[:w], mu[w:2 * w], mu[2 * w:3 * w], mu[3 * w:3 * w + lora], mu[3 * w + lora:],
                w0, None, a0, None, None, k_k, k_a, r_k.reshape(-1), gn_g, gn_b]
    args = [p3, p3, p3, p3, p3]
    specs = [col_spec(w, base), col_spec(w, base + w), col_spec(w, base + 2 * w),
             col_spec(lora, base + 3 * w), col_spec(GATE_LORA, base + 3 * w + lora)]
    mats = {6: wup_pad, 8: aup_pad, 9: g_up.astype(BF16)}
    for idx, a in enumerate(vec_args):
        arr = mats[idx] if a is None else row(a)
        args.append(arr)
        specs.append(const(arr.shape))
    for arr in (hsum, tri):
        args.append(arr)
        specs.append(const(arr.shape))
    return pl.pallas_call(
        functools.partial(_rwkv_kernel, n_seq=n_seq),
        grid=(b // n_seq, s // c),
        in_specs=specs,
        out_specs=pl.BlockSpec((n_seq, c, w), lambda bi, ci: (bi, ci, 0)),
        out_shape=jax.ShapeDtypeStruct((b, s, w), F32),
        scratch_shapes=[pltpu.VMEM((n_seq, RWKV_HEADS // 2, LANES, LANES), F32),
                        pltpu.VMEM((n_seq, 1, w), F32),
                        pltpu.VMEM((n_seq, 1, w), F32), pltpu.VMEM((n_seq, 1, w), F32),
                        pltpu.VMEM((n_seq, 1, lora), F32), pltpu.VMEM((n_seq, 1, GATE_LORA), F32)],
        compiler_params=_cparams(("parallel", "arbitrary")),
        name="rwkv",
    )(*args)


def _mem_attn_kernel(q_ref, kv_ref, o_ref):
    scale = HEAD_DIM ** -0.5
    lane_head = lax.broadcasted_iota(I32, (1, MEM_W), 1) // HEAD_DIM
    q = q_ref[0] * scale
    mk = kv_ref[0, :, :MEM_W].astype(BF16)
    mv = kv_ref[0, :, MEM_W:].astype(BF16)
    out = jnp.zeros(q.shape, F32)
    for h in range(MEM_HEADS):
        hm = lane_head == h
        s = _dot_nt(jnp.where(hm, q, 0.0).astype(BF16), mk)
        s = s - jnp.max(s, axis=-1, keepdims=True)
        e = jnp.exp(s)
        p = e / jnp.sum(e, axis=-1, keepdims=True)
        out = jnp.where(hm, jnp.dot(p.astype(BF16), mv, preferred_element_type=F32), out)
    o_ref[0] = out


def _mem_attention(p3, mkv3, tq):
    b, s, _ = p3.shape
    m = mkv3.shape[1]
    off = 0
    return pl.pallas_call(
        _mem_attn_kernel,
        grid=(b, s // tq),
        in_specs=[pl.BlockSpec((1, tq, MEM_W), lambda bi, i: (bi, i, off)),
                  pl.BlockSpec((1, m, 2 * MEM_W), lambda bi, i: (bi, 0, 0))],
        out_specs=pl.BlockSpec((1, tq, MEM_W), lambda bi, i: (bi, i, 0)),
        out_shape=jax.ShapeDtypeStruct((b, s, MEM_W), F32),
        compiler_params=_cparams(("parallel", "parallel")),
        name="mem_attn",
    )(p3, mkv3)


def _out_router_kernel(x_ref, ym_ref, yr_ref, ye_ref, wo1_ref, wo2_ref, wo3_ref, g_ref,
                       wrh_ref, wrl_ref, br_ref, upper_ref, ones_ref,
                       x1_ref, h_ref, idx_ref, rank_ref, wgt_ref, cnt_ref, run_s):
    @pl.when(pl.program_id(0) == 0)
    def _():
        run_s[...] = jnp.zeros_like(run_s)

    x1 = (x_ref[...] + _bdot(ym_ref[...], wo1_ref[...]) + _bdot(yr_ref[...], wo2_ref[...])
          + _bdot(ye_ref[...], wo3_ref[...]))
    x1_ref[...] = x1
    h = _rms(x1, g_ref[...])
    h_ref[...] = _pack_halves(h)
    tm = h.shape[0]
    h_hi, h_lo = _split_bf16(h, 2)
    dot = functools.partial(jnp.dot, preferred_element_type=F32)
    logits = dot(h_hi, wrh_ref[...]) + (dot(h_hi, wrl_ref[...]) + dot(h_lo, wrh_ref[...])) + br_ref[...]
    lg = logits.T[:N_EXPERTS, :]
    e_iota = lax.broadcasted_iota(I32, (N_EXPERTS, tm), 0)

    vals, idxs = [], []
    for _ in range(TOP_K):
        m = jnp.max(lg, axis=0, keepdims=True)
        idx = jnp.min(jnp.where(lg == m, e_iota, N_EXPERTS), axis=0, keepdims=True)
        vals.append(m)
        idxs.append(idx)
        lg = jnp.where(e_iota == idx, NEG_INF, lg)
    exps = [jnp.exp(vk - vals[0]) for vk in vals]
    denom = exps[0] + exps[1] + exps[2] + exps[3]

    chosen = jnp.zeros((N_EXPERTS, tm), F32)
    for idx in idxs:
        chosen = chosen + jnp.where(e_iota == idx, 1.0, 0.0)
    chosen = chosen.astype(BF16)
    run = run_s[...]
    before = dot(chosen, upper_ref[...]) + jnp.concatenate([run] * (tm // LANES), axis=1)
    run_s[...] = run + dot(chosen, ones_ref[...])
    cnt_ref[...] = run_s[...]

    zero_i = jnp.zeros((SUBLANES - TOP_K, tm), I32)
    ranks = [jnp.sum(jnp.where(e_iota == idx, before, 0.0), axis=0, keepdims=True).astype(I32)
             for idx in idxs]
    idx_ref[0] = jnp.concatenate(idxs + [zero_i], axis=0)
    rank_ref[0] = jnp.concatenate(ranks + [zero_i], axis=0)
    wrows = jnp.concatenate([e / denom for e in exps] + [jnp.zeros((LANES - TOP_K, tm), F32)], axis=0)
    wgt_ref[...] = wrows.T


def _out_router(x2, ym, yr, ye, w_out, g_ffn, w_router, b_router, tm):
    t, d = x2.shape
    wo = w_out.astype(BF16)
    wo1, wo2, wo3 = wo[:MOBA_W], wo[MOBA_W:MOBA_W + RWKV_W], wo[MOBA_W + RWKV_W:]
    wr = jnp.zeros((d, LANES), F32).at[:, :N_EXPERTS].set(w_router)
    wr_hi = wr.astype(BF16)
    wr_lo = (wr - wr_hi.astype(F32)).astype(BF16)
    br = jnp.full((1, LANES), NEG_INF, F32).at[0, :N_EXPERTS].set(b_router)
    upper = jnp.asarray(np.triu(np.ones((tm, tm), np.float32), 1)).astype(BF16)
    ones = jnp.ones((tm, LANES), BF16)
    tile = lambda n: pl.BlockSpec((tm, n), lambda i: (i, 0))
    slots = pl.BlockSpec((1, SUBLANES, tm), lambda i: (i, 0, 0))
    const = lambda a: pl.BlockSpec(a.shape, lambda i: (0,) * a.ndim)
    g2 = g_ffn.reshape(1, d)
    n = t // tm
    return pl.pallas_call(
        _out_router_kernel,
        grid=(n,),
        in_specs=[tile(d), tile(MOBA_W), tile(RWKV_W), tile(MEM_W), const(wo1), const(wo2), const(wo3),
                  const(g2), const(wr_hi), const(wr_lo), const(br), const(upper), const(ones)],
        out_specs=[tile(d), tile(d // 2), slots, slots, tile(LANES),
                   pl.BlockSpec((N_EXPERTS, LANES), lambda i: (0, 0))],
        out_shape=[jax.ShapeDtypeStruct((t, d), F32), jax.ShapeDtypeStruct((t, d // 2), I32),
                   jax.ShapeDtypeStruct((n, SUBLANES, tm), I32), jax.ShapeDtypeStruct((n, SUBLANES, tm), I32),
                   jax.ShapeDtypeStruct((t, LANES), F32), jax.ShapeDtypeStruct((N_EXPERTS, LANES), F32)],
        scratch_shapes=[pltpu.VMEM((N_EXPERTS, LANES), F32)],
        compiler_params=_cparams(("arbitrary",)),
        name="out_router",
    )(x2, ym, yr, ye, wo1, wo2, wo3, g2, wr_hi, wr_lo, br, upper, ones)


def _sc_mesh():
    return plsc.VectorSubcoreMesh(core_axis_name="c", subcore_axis_name="s",
                                  num_cores=SC_CORES, num_subcores=SC_SUBCORES)


def _sc_split(n_rows):
    n_workers = SC_CORES * SC_SUBCORES
    per_worker = n_rows // n_workers
    assert per_worker * n_workers == n_rows and per_worker % (SC_ROWS * SC_INFLIGHT) == 0
    assert SC_ROWS % SC_ALIGN == 0
    return per_worker


def _sc_gather_rows(table, idx):
    n_rows = idx.shape[0]
    width = table.shape[1]
    per_worker = _sc_split(n_rows)

    buf = lambda shape, dtype: [pltpu.VMEM(shape, dtype) for _ in range(SC_INFLIGHT)]

    @functools.partial(
        pl.kernel, mesh=_sc_mesh(),
        out_type=jax.ShapeDtypeStruct((n_rows, width), table.dtype),
        scratch_types=[buf((SC_ROWS,), I32), buf((SC_ROWS, width), table.dtype),
                       [pltpu.SemaphoreType.DMA for _ in range(SC_INFLIGHT)]],
        name="sc_gather",
    )
    def gather(table_hbm, idx_hbm, out_hbm, idx_v, rows_v, sems):
        worker = lax.axis_index("s") * SC_CORES + lax.axis_index("c")
        base = worker * per_worker

        @pl.loop(0, per_worker // (SC_ROWS * SC_INFLIGHT))
        def _(step):
            offs = [pl.multiple_of(base + (step * SC_INFLIGHT + b) * SC_ROWS, SC_ALIGN) for b in range(SC_INFLIGHT)]
            copies = []
            for b in range(SC_INFLIGHT):
                pltpu.sync_copy(idx_hbm.at[pl.ds(offs[b], SC_ROWS)], idx_v[b])
                copies.append(pltpu.async_copy(table_hbm.at[idx_v[b]], rows_v[b], sems[b]))
            for b in range(SC_INFLIGHT):
                copies[b].wait()
                pltpu.sync_copy(rows_v[b], out_hbm.at[pl.ds(offs[b], SC_ROWS)])

    return gather(table, idx)


def _sc_scatter_rows(rows, idx, n_out):
    n_src, width = rows.shape
    n_copies = idx.shape[0] // n_src
    assert n_copies * n_src == idx.shape[0]
    per_worker = _sc_split(n_src)

    @functools.partial(
        pl.kernel, mesh=_sc_mesh(),
        out_type=jax.ShapeDtypeStruct((n_out, width), rows.dtype),
        scratch_types=[[pltpu.VMEM((SC_ROWS,), I32) for _ in range(n_copies)],
                       pltpu.VMEM((SC_ROWS, width), rows.dtype),
                       [pltpu.SemaphoreType.DMA for _ in range(n_copies)]],
        name="sc_scatter",
    )
    def scatter(rows_hbm, idx_hbm, out_hbm, idx_v, rows_v, sems):
        worker = lax.axis_index("s") * SC_CORES + lax.axis_index("c")
        base = worker * per_worker

        @pl.loop(0, per_worker // SC_ROWS)
        def _(step):
            src = pl.multiple_of(base + step * SC_ROWS, SC_ALIGN)
            pltpu.sync_copy(rows_hbm.at[pl.ds(src, SC_ROWS)], rows_v)
            copies = []
            for cpy in range(n_copies):
                off = pl.multiple_of(cpy * n_src + src, SC_ALIGN)
                pltpu.sync_copy(idx_hbm.at[pl.ds(off, SC_ROWS)], idx_v[cpy])
                copies.append(pltpu.async_copy(rows_v, out_hbm.at[idx_v[cpy]], sems[cpy]))
            for cp in copies:
                cp.wait()

    return scatter(rows, idx)


def _experts_kernel(be_ref, bv_ref, nused_ref, next_ref, slot_ref,
                    xs_ref, wgu_hbm, bg_ref, bu_ref, wd_hbm, bd_ref, perm_ref,
                    o_ref, wgu_buf, wd_buf, wg_s, wu_s, wd_s, sem):
    i = pl.program_id(0)
    e = be_ref[i]
    used = i < nused_ref[0]
    changed = ((i == 0) | (e != be_ref[jnp.maximum(i - 1, 0)])) & used
    slot = slot_ref[e]

    def weight_copies(expert, buf_slot):
        return (pltpu.make_async_copy(wgu_hbm.at[expert], wgu_buf.at[buf_slot], sem.at[buf_slot, 0]),
                pltpu.make_async_copy(wd_hbm.at[expert], wd_buf.at[buf_slot], sem.at[buf_slot, 1]))

    @pl.when((i == 0) & used)
    def _():
        for cp in weight_copies(e, slot):
            cp.start()

    @pl.when(changed)
    def _():
        for cp in weight_copies(e, slot):
            cp.wait()
        nxt = next_ref[e]

        @pl.when(nxt >= 0)
        def _():
            for cp in weight_copies(nxt, 1 - slot):
                cp.start()

        half = LANES
        for cblk in range(2 * D_EXPERT // (2 * half)):
            wt = wgu_buf[slot, :, cblk * 2 * half:(cblk + 1) * 2 * half].astype(BF16)
            sep = jnp.dot(wt, perm_ref[...], preferred_element_type=F32).astype(BF16)
            wg_s[:, cblk * half:(cblk + 1) * half] = sep[:, :half]
            wu_s[:, cblk * half:(cblk + 1) * half] = sep[:, half:]
        wd_s[...] = wd_buf[slot].astype(BF16)

    @pl.when(used)
    def _():
        rows = lax.broadcasted_iota(I32, xs_ref.shape, 0)
        xb = _unpack_halves(jnp.where(rows < bv_ref[i], xs_ref[...], 0)).astype(BF16)
        gate = jnp.dot(xb, wg_s[...], preferred_element_type=F32) + bg_ref[0]
        up = jnp.dot(xb, wu_s[...], preferred_element_type=F32) + bu_ref[0]
        gate = jnp.minimum(gate, SWIGLU_LIMIT)
        up = jnp.clip(up, -SWIGLU_LIMIT, SWIGLU_LIMIT)
        glu = gate * _sigmoid(gate * SWIGLU_ALPHA)
        act = ((up + 1.0) * glu).astype(BF16)
        o_ref[...] = _pack_halves(jnp.dot(act, wd_s[...], preferred_element_type=F32) + bd_ref[0])

    @pl.when(jnp.logical_not(used))
    def _():
        o_ref[...] = jnp.zeros_like(o_ref)


def _experts(blk_expert, blk_valid, n_used, next_expert, buf_slot, xs, w_gate_up, b_gate_up, w_down,
             b_down):
    p_rows = xs.shape[0]
    d = 2 * xs.shape[1]
    n_blocks = p_rows // EXPERT_ROWS
    bg = b_gate_up[:, 0::2].reshape(N_EXPERTS, 1, D_EXPERT)
    bu = b_gate_up[:, 1::2].reshape(N_EXPERTS, 1, D_EXPERT)
    bd = b_down.reshape(N_EXPERTS, 1, d)
    perm_np = np.zeros((2 * LANES, 2 * LANES), np.float32)
    perm_np[2 * np.arange(LANES), np.arange(LANES)] = 1.0
    perm_np[2 * np.arange(LANES) + 1, LANES + np.arange(LANES)] = 1.0
    perm = jnp.asarray(perm_np).astype(BF16)
    by_expert = lambda shape: pl.BlockSpec((1,) + shape, lambda i, be, *_: (be[i], 0, 0))
    grid_spec = pltpu.PrefetchScalarGridSpec(
        num_scalar_prefetch=5,
        grid=(n_blocks,),
        in_specs=[pl.BlockSpec((EXPERT_ROWS, d // 2), lambda i, *_: (i, 0)),
                  pl.BlockSpec(memory_space=pl.ANY), by_expert((1, D_EXPERT)), by_expert((1, D_EXPERT)),
                  pl.BlockSpec(memory_space=pl.ANY), by_expert((1, d)),
                  pl.BlockSpec(perm.shape, lambda i, *_: (0, 0))],
        out_specs=pl.BlockSpec((EXPERT_ROWS, d // 2), lambda i, *_: (i, 0)),
        scratch_shapes=[pltpu.VMEM((2, d, 2 * D_EXPERT), F32), pltpu.VMEM((2, D_EXPERT, d), F32),
                        pltpu.VMEM((d, D_EXPERT), BF16), pltpu.VMEM((d, D_EXPERT), BF16),
                        pltpu.VMEM((D_EXPERT, d), BF16), pltpu.SemaphoreType.DMA((2, 2))],
    )
    return pl.pallas_call(
        _experts_kernel,
        grid_spec=grid_spec,
        out_shape=jax.ShapeDtypeStruct((p_rows, d // 2), I32),
        compiler_params=_cparams(("arbitrary",)),
        name="experts",
    )(blk_expert, blk_valid, n_used, next_expert, buf_slot, xs, w_gate_up, bg, bu, w_down, bd, perm)


def _combine_kernel(yg_ref, x1_ref, w_ref, g_ref, o_ref):
    acc = x1_ref[...]
    wts = w_ref[...]
    for kk in range(TOP_K):
        acc = acc + _unpack_halves(yg_ref[kk]) * wts[:, kk:kk + 1]
    o_ref[...] = _rms(acc, g_ref[...])


def _combine(yg, stream, wts, g_final, tb, first_tile):
    t, d = stream.shape
    half = yg.shape[2]
    tile = lambda n: pl.BlockSpec((tb, n), lambda i: (i + first_tile, 0))
    return pl.pallas_call(
        _combine_kernel,
        grid=(yg.shape[1] // tb,),
        in_specs=[pl.BlockSpec((TOP_K, tb, half), lambda i: (0, i, 0)), tile(d), tile(LANES),
                  pl.BlockSpec((1, d), lambda i: (0, 0))],
        out_specs=tile(d),
        out_shape=jax.ShapeDtypeStruct((t, d), F32),
        input_output_aliases={1: 0},
        compiler_params=_cparams(("parallel",)),
        name="combine",
    )(yg, stream, wts, g_final.reshape(1, d))


def _layer(x, mem, w_in, w_out, w_mem_kv, g_mix, g_mem, g_ffn, bias_tables, mu, w0, w_up, a0, a_up,
           g_up, k_k, k_a, r_k, gn_g, gn_b, w_router, b_router, w_gate_up, b_gate_up, w_down, b_down,
           g_last):
    b, s, d = x.shape
    m = mem.shape[1]
    t = b * s
    x2 = x.reshape(t, d)

    p_moba, p_rwkv, p_memq = _in_proj(x2, g_mix, w_in.astype(BF16), PROJ_TOKENS)
    y_moba = _moba_attention(p_moba.reshape(-1, s, LANES), *bias_tables, b)
    y_rwkv = _rwkv(p_rwkv.reshape(b, s, RWKV_COLS), mu, w0, w_up, a0, a_up, g_up, k_k, k_a, r_k, gn_g, gn_b)
    mkv = _norm_matmul(mem.reshape(b * m, d), g_mem, w_mem_kv.astype(BF16), PROJ_TOKENS, "mem_kv")
    y_mem = _mem_attention(p_memq.reshape(b, s, MEM_W), mkv.reshape(b, m, 2 * MEM_W), MEM_Q_TOKENS)

    tb = ROUTE_TOKENS
    x1, h2, idx_o, rank_o, wgt_p, cnt = _out_router(
        x2, y_moba.reshape(t, MOBA_W), y_rwkv.reshape(t, RWKV_W), y_mem.reshape(t, MEM_W),
        w_out, g_ffn, w_router, b_router, tb)

    counts = cnt[:, 0].astype(I32)
    padded = (counts + EXPERT_ROWS - 1) // EXPERT_ROWS * EXPERT_ROWS
    pad_ends = jnp.cumsum(padded)
    pad_starts = (pad_ends - padded).astype(I32)
    n_blocks = (t * TOP_K) // EXPERT_ROWS + N_EXPERTS
    blk_start = jnp.arange(n_blocks, dtype=I32) * EXPERT_ROWS
    blk_expert = jnp.minimum(jnp.sum(blk_start[:, None] >= pad_ends[None, :], axis=1),
                             N_EXPERTS - 1).astype(I32)
    of_block = blk_expert[:, None] == jnp.arange(N_EXPERTS, dtype=I32)[None, :]
    rows_left = jnp.sum(jnp.where(of_block, (counts + pad_starts)[None, :], 0), axis=1) - blk_start
    blk_valid = jnp.clip(rows_left, 0, EXPERT_ROWS)
    n_used = (pad_ends[-1:] // EXPERT_ROWS).astype(I32)
    has_rows = counts > 0
    first_from = lax.cummin(jnp.where(has_rows, jnp.arange(N_EXPERTS, dtype=I32), N_EXPERTS), reverse=True)
    next_expert = jnp.concatenate([first_from[1:], jnp.full((1,), N_EXPERTS, I32)])
    next_expert = jnp.where(next_expert < N_EXPERTS, next_expert, -1).astype(I32)
    buf_slot = ((jnp.cumsum(has_rows.astype(I32)) - has_rows.astype(I32)) % 2).astype(I32)
    idx_kt = jnp.swapaxes(idx_o[:, :TOP_K, :], 0, 1).reshape(TOP_K, t)
    rank_kt = jnp.swapaxes(rank_o[:, :TOP_K, :], 0, 1).reshape(TOP_K, t)
    experts = jnp.arange(N_EXPERTS, dtype=I32)[:, None, None]
    start_kt = jnp.sum(jnp.where(idx_kt[None] == experts, pad_starts[:, None, None], 0), axis=0)
    dest = (start_kt + rank_kt).reshape(TOP_K * t)

    xs = _sc_scatter_rows(h2, dest, n_blocks * EXPERT_ROWS)
    ys = _experts(blk_expert, blk_valid.astype(I32), n_used, next_expert, buf_slot, xs,
                  w_gate_up, b_gate_up, w_down, b_down)
    dest_kt = dest.reshape(TOP_K, t)
    out = x1
    for part in range(COMBINE_PARTS):
        t_part = t // COMBINE_PARTS
        rows = dest_kt[:, part * t_part:(part + 1) * t_part].reshape(TOP_K * t_part)
        yg = _sc_gather_rows(ys, rows).reshape(TOP_K, t_part, d // 2)
        out = _combine(yg, out, wgt_p, g_last, tb, part * (t_part // tb))
    return out.reshape(b, s, d)


def kernel(x, mem, w_in, w_out, w_mem_kv, g_mix, g_mem, g_ffn, g_final, rel_bias, rwkv_mu, rwkv_w0,
           rwkv_w_up, rwkv_a0, rwkv_a_up, rwkv_g_up, rwkv_k_k, rwkv_k_a, rwkv_r_k, rwkv_gn_g, rwkv_gn_b,
           w_router, b_router, w_gate_up, b_gate_up, w_down, b_down):
    depth = w_in.shape[0]
    assert depth == 1, "the final norm is fused into the last layer's combine kernel"
    bias_tables = _moba_bias_tables(rel_bias)
    l = 0
    return _layer(x, mem, w_in[l], w_out[l], w_mem_kv[l], g_mix[l], g_mem[l], g_ffn[l], bias_tables,
                  rwkv_mu[l], rwkv_w0[l], rwkv_w_up[l], rwkv_a0[l], rwkv_a_up[l], rwkv_g_up[l],
                  rwkv_k_k[l], rwkv_k_a[l], rwkv_r_k[l], rwkv_gn_g[l], rwkv_gn_b[l], w_router[l],
                  b_router[l], w_gate_up[l], b_gate_up[l], w_down[l], b_down[l], g_final)
```

```python
import functools
import math

import numpy as np
import jax
import jax.numpy as jnp
from jax import lax
from jax.experimental import pallas as pl
from jax.experimental.pallas import tpu as pltpu
from jax.experimental.pallas import tpu_sc as plsc

F32 = jnp.float32
BF16 = jnp.bfloat16
I32 = jnp.int32
HI = lax.Precision.HIGHEST

D_MODEL = 1024
HEAD_DIM = 64
MOBA_HEADS = 6
RWKV_HEADS = 6
MEM_HEADS = 4
MOBA_W = MOBA_HEADS * HEAD_DIM
RWKV_W = RWKV_HEADS * HEAD_DIM
MEM_W = MEM_HEADS * HEAD_DIM
MOBA_BLOCK = 256
MOBA_TOPK = 3
N_BUCKETS = 32
MAX_DISTANCE = 128
DECAY_LORA = 64
AAA_LORA = 64
GATE_LORA = 128
RWKV_COLS = 3 * RWKV_W + DECAY_LORA + AAA_LORA + GATE_LORA
RWKV_GN_EPS = 64e-5
IN_COLS = 3 * MOBA_W + RWKV_COLS + MEM_W
N_EXPERTS = 32
TOP_K = 4
D_EXPERT = D_MODEL
SWIGLU_ALPHA = 1.702
SWIGLU_LIMIT = 7.0
RMS_EPS = 1e-5

LANES = 128
SUBLANES = 8
BF16_SUBLANES = 16
PROJ_TOKENS = 512
MEM_Q_TOKENS = 512
RWKV_CHUNK = 128
RWKV_SEQS_PER_STEP = 4
EXPERT_ROWS = 256
ROUTE_TOKENS = 512
SC_CORES = 2
SC_SUBCORES = 16
SC_ROWS = 64
SC_ALIGN = 8
SC_INFLIGHT = 2
COMBINE_PARTS = 4
VMEM_LIMIT = 56 * 1024 * 1024
NEG_INF = float("-inf")
HIGH_HALF = -65536
LOG2_E = 1.4426950408889634
MOBA_VROWS = HEAD_DIM + BF16_SUBLANES


def _cparams(sem):
    return pltpu.CompilerParams(dimension_semantics=sem, vmem_limit_bytes=VMEM_LIMIT)


def _rms(x, g):
    return x * lax.rsqrt(jnp.mean(x * x, axis=-1, keepdims=True) + RMS_EPS) * g


def _bdot(a, b):
    return jnp.dot(a.astype(BF16), b.astype(BF16), preferred_element_type=F32)


def _hdot(a, b):
    return jnp.dot(a, b, preferred_element_type=F32, precision=HI)


def _dot_nt(a, b):
    return lax.dot_general(a, b, (((1,), (1,)), ((), ())), preferred_element_type=F32)


def _dot_tn(a, b):
    return lax.dot_general(a, b, (((0,), (0,)), ((), ())), preferred_element_type=F32)


def _split_bf16(x, terms):
    parts = []
    for _ in range(terms):
        hi = x.astype(BF16)
        parts.append(hi)
        x = x - hi.astype(F32)
    return parts


def _dot_exact_rhs(x, m_bf16, terms):
    acc = None
    for part in _split_bf16(x, terms):
        d = jnp.dot(part, m_bf16, preferred_element_type=F32)
        acc = d if acc is None else acc + d
    return acc


def _pack_halves(x):
    n = x.shape[1] // 2
    lo = pltpu.bitcast(x[:, :n].astype(BF16).astype(F32), I32)
    hi = pltpu.bitcast(x[:, n:].astype(BF16).astype(F32), I32)
    return (hi & HIGH_HALF) | lax.shift_right_logical(lo, 16)


def _unpack_halves(w):
    lo = pltpu.bitcast(w << 16, F32)
    hi = pltpu.bitcast(w & HIGH_HALF, F32)
    return jnp.concatenate([lo, hi], axis=1)


def _norm_matmul_kernel(x_ref, g_ref, w_ref, o_ref):
    h = _rms(x_ref[...], g_ref[...])
    o_ref[...] = jnp.dot(h.astype(BF16), w_ref[...], preferred_element_type=F32)


def _norm_matmul(x, g, w_bf16, tm, name):
    t, d = x.shape
    n = w_bf16.shape[1]
    tm = min(tm, t)
    return pl.pallas_call(
        _norm_matmul_kernel,
        grid=(t // tm,),
        in_specs=[pl.BlockSpec((tm, d), lambda i: (i, 0)),
                  pl.BlockSpec((1, d), lambda i: (0, 0)),
                  pl.BlockSpec((d, n), lambda i: (0, 0))],
        out_specs=pl.BlockSpec((tm, n), lambda i: (i, 0)),
        out_shape=jax.ShapeDtypeStruct((t, n), F32),
        compiler_params=_cparams(("parallel",)),
        name=name,
    )(x, g.reshape(1, d), w_bf16)


def _in_proj_kernel(x_ref, g_ref, w_ref, moba_ref, rwkv_ref, memq_ref):
    h = _rms(x_ref[...], g_ref[...])
    p = jnp.dot(h.astype(BF16), w_ref[...], preferred_element_type=F32)
    for j in range(3 * MOBA_W // LANES):
        moba_ref[j] = p[:, j * LANES:(j + 1) * LANES]
    rwkv_ref[...] = p[:, 3 * MOBA_W:3 * MOBA_W + RWKV_COLS]
    memq_ref[...] = p[:, 3 * MOBA_W + RWKV_COLS:]


def _in_proj(x, g, w_bf16, tm):
    t, d = x.shape
    n_slabs = 3 * MOBA_W // LANES
    return pl.pallas_call(
        _in_proj_kernel,
        grid=(t // tm,),
        in_specs=[pl.BlockSpec((tm, d), lambda i: (i, 0)),
                  pl.BlockSpec((1, d), lambda i: (0, 0)),
                  pl.BlockSpec((d, IN_COLS), lambda i: (0, 0))],
        out_specs=[pl.BlockSpec((n_slabs, tm, LANES), lambda i: (0, i, 0)),
                   pl.BlockSpec((tm, RWKV_COLS), lambda i: (i, 0)),
                   pl.BlockSpec((tm, MEM_W), lambda i: (i, 0))],
        out_shape=[jax.ShapeDtypeStruct((n_slabs, t, LANES), F32),
                   jax.ShapeDtypeStruct((t, RWKV_COLS), F32),
                   jax.ShapeDtypeStruct((t, MEM_W), F32)],
        compiler_params=_cparams(("parallel",)),
        name="in_proj",
    )(x, g.reshape(1, d), w_bf16)


def _t5_bucket_np(dist):
    n = np.maximum(dist, 0)
    max_exact = N_BUCKETS // 2
    nf = np.maximum(n, 1).astype(np.float64)
    large = max_exact + (np.log(nf / max_exact) / math.log(MAX_DISTANCE / max_exact)
                         * (N_BUCKETS - max_exact)).astype(np.int64)
    large = np.minimum(large, N_BUCKETS - 1)
    return np.where(n < max_exact, n, large)


def _moba_bias_tables(rel_bias):
    n = MOBA_BLOCK
    assert np.all(_t5_bucket_np(np.arange(n + 1, 64 * n)) == N_BUCKETS - 1)
    bias_t = rel_bias.astype(F32).T * LOG2_E

    def by_distance(dist):
        bucket = jnp.asarray(_t5_bucket_np(dist), I32)[None]
        tab = jnp.zeros((MOBA_HEADS, dist.shape[0]), F32)
        for b in range(N_BUCKETS):
            tab = jnp.where(bucket == b, bias_t[:, b][:, None], tab)
        return tab

    d = np.arange(2 * n)
    g_own = jnp.where(jnp.asarray(d < n)[None], by_distance(np.where(d < n, d, 0)), NEG_INF)
    g_prev = by_distance(np.where(d < n, n + d, d - n))
    far = bias_t[:, N_BUCKETS - 1]
    return g_own[:, None, :], g_prev[:, None, :], far


def _moba_kernel(far_ref, q_ref, k_ref, v_ref, g0_ref, g1_ref, o_ref,
                 qt_s, kb_s, vt_s, sc_s, t0_ref, t1_ref, *, nb):
    hp = pl.program_id(1)
    blk = MOBA_BLOCK
    scale = HEAD_DIM ** -0.5 * LOG2_E
    row_head = lax.broadcasted_iota(I32, (LANES, 1), 0) // HEAD_DIM
    lane_head = lax.broadcasted_iota(I32, (1, LANES), 1) // HEAD_DIM
    ones_rows = jnp.ones((MOBA_VROWS - HEAD_DIM, blk), BF16)

    for h in range(2):
        for g_ref, t_ref in ((g0_ref, t0_ref), (g1_ref, t1_ref)):
            rows = jnp.broadcast_to(g_ref[h], (blk, 2 * blk))
            t_ref[h] = pltpu.roll(rows, 0, 1, stride=1, stride_axis=0)[:, :blk]

    kmean_rows = []
    for j in range(nb):
        sl = pl.ds(j * blk, blk)
        qt_s[j] = (q_ref[0, sl, :] * scale).T
        vt = v_ref[0, sl, :].T.astype(BF16)
        for h in range(2):
            vt_s[j, h * MOBA_VROWS:h * MOBA_VROWS + HEAD_DIM, :] = vt[h * HEAD_DIM:(h + 1) * HEAD_DIM, :]
            vt_s[j, h * MOBA_VROWS + HEAD_DIM:(h + 1) * MOBA_VROWS, :] = ones_rows
        kj = k_ref[0, sl, :]
        kb_s[j] = kj.astype(BF16)
        kmean_rows.append(jnp.mean(kj, axis=0, keepdims=True))
    kmean = jnp.concatenate(kmean_rows, axis=0)
    km2 = jnp.concatenate([jnp.where(lane_head == 0, kmean, 0.0),
                           jnp.where(lane_head == 1, kmean, 0.0)], axis=0)
    blk_iota = lax.broadcasted_iota(I32, (nb, blk), 0)

    def score_pass(i):
        qt = qt_s[i]
        negs = [None, None]
        if i > MOBA_TOPK:
            gate2 = _hdot(km2, qt)
            for h in range(2):
                g = gate2[h * nb:(h + 1) * nb, :]
                cnt = jnp.zeros((nb, blk), I32)
                for m in range(i):
                    gm = g[m:m + 1, :]
                    beats = (gm > g) | ((gm == g) & (m < blk_iota))
                    cnt = cnt + jnp.where(beats, 1, 0)
                negs[h] = jnp.where(cnt < MOBA_TOPK, 0.0, NEG_INF)
        col_max = []
        for h in range(2):
            qt_h = jnp.where(row_head == h, qt, 0.0).astype(BF16)
            far_bias = far_ref[2 * hp + h]
            m_run = None
            shifts = []
            for j in range(i + 1):
                s = jnp.dot(kb_s[j], qt_h, preferred_element_type=F32)
                if j >= i - 1:
                    bias = t0_ref[h] if j == i else t1_ref[h]
                    if j < i and negs[h] is not None:
                        bias = bias + negs[h][j:j + 1, :]
                    s = s + bias
                    cm = jnp.max(s, axis=0, keepdims=True)
                    shifts.append(None)
                else:
                    row = jnp.full((1, blk), far_bias, F32)
                    if negs[h] is not None:
                        row = row + negs[h][j:j + 1, :]
                    cm = jnp.max(s, axis=0, keepdims=True) + row
                    shifts.append(row)
                sc_s[i % 2, h, j] = s
                m_run = cm if m_run is None else jnp.maximum(m_run, cm)
            col_max.append([m_run if row is None else m_run - row for row in shifts])
        return col_max

    def prob_pass(i, col_max):
        outs = []
        for h in range(2):
            offsets = col_max[h]
            probs = jnp.concatenate([jnp.exp2(sc_s[i % 2, h, j] - offsets[j]).astype(BF16)
                                     for j in range(i + 1)], axis=0)
            vals = jnp.concatenate([vt_s[j, h * MOBA_VROWS:(h + 1) * MOBA_VROWS, :]
                                    for j in range(i + 1)], axis=1)
            acc = jnp.dot(vals, probs, preferred_element_type=F32)
            outs.append(acc[:HEAD_DIM] / acc[HEAD_DIM:HEAD_DIM + 1])
        o_t = jnp.concatenate(outs, axis=0)
        o_ref[0, pl.ds(i * blk, blk), :] = o_t.T

    pending = None
    for i in range(nb):
        col_max = score_pass(i)
        if pending is not None:
            prob_pass(*pending)
        pending = (i, col_max)
    prob_pass(*pending)


def _moba_attention(p3, g_own, g_prev, far, b):
    s = p3.shape[1]
    nb = s // MOBA_BLOCK
    n_pairs = MOBA_HEADS // 2
    blk_spec = lambda off: pl.BlockSpec((1, s, LANES), lambda bi, hp: ((off + hp) * b + bi, 0, 0))
    tab_spec = pl.BlockSpec((2, 1, 2 * MOBA_BLOCK), lambda bi, hp: (hp, 0, 0))
    return pl.pallas_call(
        functools.partial(_moba_kernel, nb=nb),
        grid=(b, n_pairs),
        in_specs=[pl.BlockSpec(memory_space=pltpu.SMEM),
                  blk_spec(0), blk_spec(n_pairs), blk_spec(2 * n_pairs), tab_spec, tab_spec],
        out_specs=pl.BlockSpec((1, s, LANES), lambda bi, hp: (bi, 0, hp)),
        out_shape=jax.ShapeDtypeStruct((b, s, MOBA_W), F32),
        scratch_shapes=[pltpu.VMEM((nb, LANES, MOBA_BLOCK), F32),
                        pltpu.VMEM((nb, MOBA_BLOCK, LANES), BF16),
                        pltpu.VMEM((nb, 2 * MOBA_VROWS, MOBA_BLOCK), BF16),
                        pltpu.VMEM((2, 2, nb, MOBA_BLOCK, MOBA_BLOCK), F32),
                        pltpu.VMEM((2, MOBA_BLOCK, MOBA_BLOCK), F32),
                        pltpu.VMEM((2, MOBA_BLOCK, MOBA_BLOCK), F32)],
        compiler_params=_cparams(("parallel", "parallel")),
        name="moba",
    )(far, p3, p3, p3, g_own, g_prev)


def _softplus(z):
    return jnp.maximum(z, 0.0) + jnp.log(1.0 + jnp.exp(-jnp.abs(z)))


def _sigmoid(z):
    return 1.0 / (1.0 + jnp.exp(-z))


def _rwkv_kernel(r_ref, k_ref, v_ref, wa_ref, g_ref,
                 mu_r_ref, mu_k_ref, mu_v_ref, mu_wa_ref, mu_g_ref,
                 w0_ref, wup_ref, a0_ref, aup_ref, gup_ref, kk_ref, ka_ref, rk_ref,
                 gng_ref, gnb_ref, hsum_ref, tri_ref,
                 o_ref, st_s, prev_r, prev_k, prev_v, prev_wa, prev_g, *, n_seq):
    c = RWKV_CHUNK
    mid = c // 2

    @pl.when(pl.program_id(1) == 0)
    def _():
        st_s[...] = jnp.zeros_like(st_s)
        prev_r[...] = jnp.zeros_like(prev_r)
        prev_k[...] = jnp.zeros_like(prev_k)
        prev_v[...] = jnp.zeros_like(prev_v)
        prev_wa[...] = jnp.zeros_like(prev_wa)
        prev_g[...] = jnp.zeros_like(prev_g)

    n_pairs = RWKV_HEADS // 2
    hsum = hsum_ref[...]
    tri = tri_ref[...]
    row = lax.broadcasted_iota(I32, (c, c), 0)
    col = lax.broadcasted_iota(I32, (c, c), 1)
    strict = row > col
    incl = row >= col
    lane_half = lax.broadcasted_iota(I32, (1, LANES), 1) // HEAD_DIM
    hmask = hsum.astype(F32)
    dot = functools.partial(jnp.dot, preferred_element_type=F32)
    lanes = lambda x, jp: x[:, jp * LANES:(jp + 1) * LANES]

    def group_sum(x):
        return jnp.concatenate([_dot_exact_rhs(lanes(x, jp), hsum, 2) for jp in range(n_pairs)], axis=1)

    def two_heads(x):
        return jnp.concatenate([jnp.where(lane_half == 0, x, 0.0), jnp.where(lane_half == 1, x, 0.0)],
                               axis=0).astype(BF16)

    def side(a, b):
        return jnp.concatenate([a, b], axis=1)

    def mix(g, x_ref, prev_ref, mu_ref):
        x = x_ref[g]
        rows = lax.broadcasted_iota(I32, x.shape, 0)
        shifted = jnp.where(rows == 0, prev_ref[g], pltpu.roll(x, 1, axis=0))
        prev_ref[g] = x[c - 1:c, :]
        return x + (shifted - x) * mu_ref[...]

    def prepare(g):
        r = mix(g, r_ref, prev_r, mu_r_ref)
        k = mix(g, k_ref, prev_k, mu_k_ref)
        v = mix(g, v_ref, prev_v, mu_v_ref)
        xwa = mix(g, wa_ref, prev_wa, mu_wa_ref)
        xg = mix(g, g_ref, prev_g, mu_g_ref)

        w = -_softplus(-(w0_ref[...] + _bdot(jnp.tanh(xwa), wup_ref[...]))) - 0.5
        logd = -jnp.exp(w)
        a = _sigmoid(a0_ref[...] + _bdot(xwa, aup_ref[...]))
        gate = _bdot(_sigmoid(xg), gup_ref[...])
        kk = k * kk_ref[...]
        kk = kk / jnp.maximum(jnp.sqrt(group_sum(kk * kk)), 1e-12)
        k2 = k * (1.0 + (a - 1.0) * ka_ref[...])

        parts = _split_bf16(logd, 3)
        cum = dot(tri, parts[0]) + dot(tri, parts[1]) + dot(tri, parts[2])
        cum_last = cum[c - 1:c, :]
        ref = cum[mid - 1:mid, :]
        rel = cum - ref
        g_in = jnp.exp(rel)
        g_inv = jnp.exp(-rel)
        g_tail = jnp.exp(cum_last - cum)
        a_t = -kk * jnp.exp(rel - logd)
        b_t = kk * a * g_inv
        k_t = k2 * g_inv
        r_t = r * g_in
        b_hat = kk * a * g_tail
        k_hat = k2 * g_tail

        return dict(r=r, k2=k2, v=v, gate=gate, b_hat=b_hat, k_hat=k_hat, ref_scale=jnp.exp(ref),
                    decay=jnp.exp(cum_last), bk=jnp.concatenate([b_t, k_t], axis=0).astype(BF16),
                    ar_all=jnp.concatenate([a_t, r_t], axis=0))

    def chunk_products(g, el):
        v, b_hat, k_hat, ref_scale, decay = el["v"], el["b_hat"], el["k_hat"], el["ref_scale"], el["decay"]
        bk, ar_all = el["bk"], el["ar_all"]
        per_pair = []
        for jp in range(n_pairs):
            st = st_s[g, jp]
            st_ref = (st * lanes(ref_scale, jp)).astype(BF16)
            both = _dot_nt(lanes(ar_all, jp).astype(BF16), st_ref)
            quads = []
            for half in range(2):
                ar = jnp.where(lane_half == half, lanes(ar_all, jp), 0.0)
                m4 = _dot_nt(ar.astype(BF16), lanes(bk, jp))
                quads.append((jnp.where(strict, m4[:c, :c], 0.0),
                              jnp.where(strict, m4[:c, c:], 0.0).astype(BF16),
                              jnp.where(incl, m4[c:, :c], 0.0).astype(BF16),
                              jnp.where(incl, m4[c:, c:], 0.0).astype(BF16)))
            per_pair.append(dict(
                st=st, p0=both[:c], o0=both[c:], v_rows=two_heads(lanes(v, jp)),
                l_ab=[q[0] for q in quads], l_ak=side(quads[0][1], quads[1][1]),
                m_rb=side(quads[0][2], quads[1][2]), m_rk=side(quads[0][3], quads[1][3]),
                v=lanes(v, jp).astype(BF16), b_hat=lanes(b_hat, jp).astype(BF16),
                k_hat=lanes(k_hat, jp).astype(BF16), decay=lanes(decay, jp)))
        return dict(r=el["r"], k2=el["k2"], v=v, gate=el["gate"], pairs=per_pair)

    seqs = {}

    def solve(group):
        heads = [(g, jp, half) for g in group for jp in range(n_pairs) for half in range(2)]
        pair_ids = [(g, jp) for g in group for jp in range(n_pairs)]

        pw = {gh: seqs[gh[0]]["pairs"][gh[1]]["l_ab"][gh[2]].astype(BF16) for gh in heads}
        u_loc = {(g, jp): seqs[g]["pairs"][jp]["p0"]
                 + dot(seqs[g]["pairs"][jp]["l_ak"], seqs[g]["pairs"][jp]["v_rows"]) for g, jp in pair_ids}
        span = 1
        while span < c:
            u_loc = {(g, jp): u_loc[(g, jp)]
                     + dot(side(pw[(g, jp, 0)], pw[(g, jp, 1)]), two_heads(u_loc[(g, jp)]))
                     for g, jp in pair_ids}
            span *= 2
            if span < c:
                pw = {gh: dot(pw[gh], pw[gh]).astype(BF16) for gh in heads}

        for g in group:
            sq = seqs[g]
            outs = []
            for jp in range(n_pairs):
                pr = sq["pairs"][jp]
                u = u_loc[(g, jp)]
                outs.append(pr["o0"] + dot(pr["m_rb"], two_heads(u)) + dot(pr["m_rk"], pr["v_rows"]))
                upd = _dot_tn(u.astype(BF16), pr["b_hat"]) + _dot_tn(pr["v"], pr["k_hat"])
                st_s[g, jp] = pr["st"] * pr["decay"] + hmask * upd
            o = jnp.concatenate(outs, axis=1)

            inv_n = 1.0 / HEAD_DIM
            mean = group_sum(o) * inv_n
            dev = o - mean
            var = group_sum(dev * dev) * inv_n
            y = dev * lax.rsqrt(var + RWKV_GN_EPS) * gng_ref[...] + gnb_ref[...]
            y = y + group_sum(sq["r"] * sq["k2"] * rk_ref[...]) * sq["v"]
            o_ref[g] = y * sq["gate"]

    elementwise = None
    for g in range(n_seq):
        nxt = prepare(g)
        if elementwise is not None:
            seqs[g - 1] = chunk_products(g - 1, elementwise)
        elementwise = nxt
    seqs[n_seq - 1] = chunk_products(n_seq - 1, elementwise)
    solve(list(range(n_seq)))


def _rwkv(p3, mu, w0, w_up, a0, a_up, g_up, k_k, k_a, r_k, gn_g, gn_b):
    b, s, _ = p3.shape
    c = RWKV_CHUNK
    w = RWKV_W
    n_seq = RWKV_SEQS_PER_STEP if b % RWKV_SEQS_PER_STEP == 0 else 1
    base = 0
    assert (base + 3 * w) % LANES == 0
    lora = DECAY_LORA + AAA_LORA

    def col_spec(width, off_cols):
        assert off_cols % width == 0
        return pl.BlockSpec((n_seq, c, width), lambda bi, ci: (bi, ci, off_cols // width))

    row = lambda x: x.reshape(1, -1).astype(F32)
    const = lambda shape: pl.BlockSpec(shape, lambda bi, ci: (0,) * len(shape))
    wup_pad = jnp.concatenate([w_up, jnp.zeros((AAA_LORA, w), F32)], axis=0).astype(BF16)
    aup_pad = jnp.concatenate([jnp.zeros((DECAY_LORA, w), F32), a_up], axis=0).astype(BF16)
    head = np.arange(LANES) // HEAD_DIM
    hsum = jnp.asarray((head[:, None] == head[None, :]).astype(np.float32)).astype(BF16)
    tri = jnp.asarray(np.tril(np.ones((c, c), np.float32))).astype(BF16)
    vec_args = [mu[:w], mu[w:2 * w], mu[2 * w:3 * w], mu[3 * w:3 * w + lora], mu[3 * w + lora:],
                w0, None, a0, None, None, k_k, k_a, r_k.reshape(-1), gn_g, gn_b]
    args = [p3, p3, p3, p3, p3]
    specs = [col_spec(w, base), col_spec(w, base + w), col_spec(w, base + 2 * w),
             col_spec(lora, base + 3 * w), col_spec(GATE_LORA, base + 3 * w + lora)]
    mats = {6: wup_pad, 8: aup_pad, 9: g_up.astype(BF16)}
    for idx, a in enumerate(vec_args):
        arr = mats[idx] if a is None else row(a)
        args.append(arr)
        specs.append(const(arr.shape))
    for arr in (hsum, tri):
        args.append(arr)
        specs.append(const(arr.shape))
    return pl.pallas_call(
        functools.partial(_rwkv_kernel, n_seq=n_seq),
        grid=(b // n_seq, s // c),
        in_specs=specs,
        out_specs=pl.BlockSpec((n_seq, c, w), lambda bi, ci: (bi, ci, 0)),
        out_shape=jax.ShapeDtypeStruct((b, s, w), F32),
        scratch_shapes=[pltpu.VMEM((n_seq, RWKV_HEADS // 2, LANES, LANES), F32),
                        pltpu.VMEM((n_seq, 1, w), F32),
                        pltpu.VMEM((n_seq, 1, w), F32), pltpu.VMEM((n_seq, 1, w), F32),
                        pltpu.VMEM((n_seq, 1, lora), F32), pltpu.VMEM((n_seq, 1, GATE_LORA), F32)],
        compiler_params=_cparams(("parallel", "arbitrary")),
        name="rwkv",
    )(*args)


def _mem_attn_kernel(q_ref, kv_ref, o_ref):
    scale = HEAD_DIM ** -0.5
    lane_head = lax.broadcasted_iota(I32, (1, MEM_W), 1) // HEAD_DIM
    q = q_ref[0] * scale
    mk = kv_ref[0, :, :MEM_W].astype(BF16)
    mv = kv_ref[0, :, MEM_W:].astype(BF16)
    out = jnp.zeros(q.shape, F32)
    for h in range(MEM_HEADS):
        hm = lane_head == h
        s = _dot_nt(jnp.where(hm, q, 0.0).astype(BF16), mk)
        s = s - jnp.max(s, axis=-1, keepdims=True)
        e = jnp.exp(s)
        p = e / jnp.sum(e, axis=-1, keepdims=True)
        out = jnp.where(hm, jnp.dot(p.astype(BF16), mv, preferred_element_type=F32), out)
    o_ref[0] = out


def _mem_attention(p3, mkv3, tq):
    b, s, _ = p3.shape
    m = mkv3.shape[1]
    off = 0
    return pl.pallas_call(
        _mem_attn_kernel,
        grid=(b, s // tq),
        in_specs=[pl.BlockSpec((1, tq, MEM_W), lambda bi, i: (bi, i, off)),
                  pl.BlockSpec((1, m, 2 * MEM_W), lambda bi, i: (bi, 0, 0))],
        out_specs=pl.BlockSpec((1, tq, MEM_W), lambda bi, i: (bi, i, 0)),
        out_shape=jax.ShapeDtypeStruct((b, s, MEM_W), F32),
        compiler_params=_cparams(("parallel", "parallel")),
        name="mem_attn",
    )(p3, mkv3)


def _out_router_kernel(x_ref, ym_ref, yr_ref, ye_ref, wo1_ref, wo2_ref, wo3_ref, g_ref,
                       wrh_ref, wrl_ref, br_ref, upper_ref, ones_ref,
                       x1_ref, h_ref, idx_ref, rank_ref, wgt_ref, cnt_ref, run_s):
    @pl.when(pl.program_id(0) == 0)
    def _():
        run_s[...] = jnp.zeros_like(run_s)

    x1 = (x_ref[...] + _bdot(ym_ref[...], wo1_ref[...]) + _bdot(yr_ref[...], wo2_ref[...])
          + _bdot(ye_ref[...], wo3_ref[...]))
    x1_ref[...] = x1
    h = _rms(x1, g_ref[...])
    h_ref[...] = _pack_halves(h)
    tm = h.shape[0]
    h_hi, h_lo = _split_bf16(h, 2)
    dot = functools.partial(jnp.dot, preferred_element_type=F32)
    logits = dot(h_hi, wrh_ref[...]) + (dot(h_hi, wrl_ref[...]) + dot(h_lo, wrh_ref[...])) + br_ref[...]
    lg = logits.T[:N_EXPERTS, :]
    e_iota = lax.broadcasted_iota(I32, (N_EXPERTS, tm), 0)

    vals, idxs = [], []
    for _ in range(TOP_K):
        m = jnp.max(lg, axis=0, keepdims=True)
        idx = jnp.min(jnp.where(lg == m, e_iota, N_EXPERTS), axis=0, keepdims=True)
        vals.append(m)
        idxs.append(idx)
        lg = jnp.where(e_iota == idx, NEG_INF, lg)
    exps = [jnp.exp(vk - vals[0]) for vk in vals]
    denom = exps[0] + exps[1] + exps[2] + exps[3]

    chosen = jnp.zeros((N_EXPERTS, tm), F32)
    for idx in idxs:
        chosen = chosen + jnp.where(e_iota == idx, 1.0, 0.0)
    chosen = chosen.astype(BF16)
    run = run_s[...]
    before = dot(chosen, upper_ref[...]) + jnp.concatenate([run] * (tm // LANES), axis=1)
    run_s[...] = run + dot(chosen, ones_ref[...])
    cnt_ref[...] = run_s[...]

    zero_i = jnp.zeros((SUBLANES - TOP_K, tm), I32)
    ranks = [jnp.sum(jnp.where(e_iota == idx, before, 0.0), axis=0, keepdims=True).astype(I32)
             for idx in idxs]
    idx_ref[0] = jnp.concatenate(idxs + [zero_i], axis=0)
    rank_ref[0] = jnp.concatenate(ranks + [zero_i], axis=0)
    wrows = jnp.concatenate([e / denom for e in exps] + [jnp.zeros((LANES - TOP_K, tm), F32)], axis=0)
    wgt_ref[...] = wrows.T


def _out_router(x2, ym, yr, ye, w_out, g_ffn, w_router, b_router, tm):
    t, d = x2.shape
    wo = w_out.astype(BF16)
    wo1, wo2, wo3 = wo[:MOBA_W], wo[MOBA_W:MOBA_W + RWKV_W], wo[MOBA_W + RWKV_W:]
    wr = jnp.zeros((d, LANES), F32).at[:, :N_EXPERTS].set(w_router)
    wr_hi = wr.astype(BF16)
    wr_lo = (wr - wr_hi.astype(F32)).astype(BF16)
    br = jnp.full((1, LANES), NEG_INF, F32).at[0, :N_EXPERTS].set(b_router)
    upper = jnp.asarray(np.triu(np.ones((tm, tm), np.float32), 1)).astype(BF16)
    ones = jnp.ones((tm, LANES), BF16)
    tile = lambda n: pl.BlockSpec((tm, n), lambda i: (i, 0))
    slots = pl.BlockSpec((1, SUBLANES, tm), lambda i: (i, 0, 0))
    const = lambda a: pl.BlockSpec(a.shape, lambda i: (0,) * a.ndim)
    g2 = g_ffn.reshape(1, d)
    n = t // tm
    return pl.pallas_call(
        _out_router_kernel,
        grid=(n,),
        in_specs=[tile(d), tile(MOBA_W), tile(RWKV_W), tile(MEM_W), const(wo1), const(wo2), const(wo3),
                  const(g2), const(wr_hi), const(wr_lo), const(br), const(upper), const(ones)],
        out_specs=[tile(d), tile(d // 2), slots, slots, tile(LANES),
                   pl.BlockSpec((N_EXPERTS, LANES), lambda i: (0, 0))],
        out_shape=[jax.ShapeDtypeStruct((t, d), F32), jax.ShapeDtypeStruct((t, d // 2), I32),
                   jax.ShapeDtypeStruct((n, SUBLANES, tm), I32), jax.ShapeDtypeStruct((n, SUBLANES, tm), I32),
                   jax.ShapeDtypeStruct((t, LANES), F32), jax.ShapeDtypeStruct((N_EXPERTS, LANES), F32)],
        scratch_shapes=[pltpu.VMEM((N_EXPERTS, LANES), F32)],
        compiler_params=_cparams(("arbitrary",)),
        name="out_router",
    )(x2, ym, yr, ye, wo1, wo2, wo3, g2, wr_hi, wr_lo, br, upper, ones)


def _sc_mesh():
    return plsc.VectorSubcoreMesh(core_axis_name="c", subcore_axis_name="s",
                                  num_cores=SC_CORES, num_subcores=SC_SUBCORES)


def _sc_split(n_rows):
    n_workers = SC_CORES * SC_SUBCORES
    per_worker = n_rows // n_workers
    assert per_worker * n_workers == n_rows and per_worker % (SC_ROWS * SC_INFLIGHT) == 0
    assert SC_ROWS % SC_ALIGN == 0
    return per_worker


def _sc_gather_rows(table, idx):
    n_rows = idx.shape[0]
    width = table.shape[1]
    per_worker = _sc_split(n_rows)

    buf = lambda shape, dtype: [pltpu.VMEM(shape, dtype) for _ in range(SC_INFLIGHT)]

    @functools.partial(
        pl.kernel, mesh=_sc_mesh(),
        out_type=jax.ShapeDtypeStruct((n_rows, width), table.dtype),
        scratch_types=[buf((SC_ROWS,), I32), buf((SC_ROWS, width), table.dtype),
                       [pltpu.SemaphoreType.DMA for _ in range(SC_INFLIGHT)]],
        name="sc_gather",
    )
    def gather(table_hbm, idx_hbm, out_hbm, idx_v, rows_v, sems):
        worker = lax.axis_index("s") * SC_CORES + lax.axis_index("c")
        base = worker * per_worker

        @pl.loop(0, per_worker // (SC_ROWS * SC_INFLIGHT))
        def _(step):
            offs = [pl.multiple_of(base + (step * SC_INFLIGHT + b) * SC_ROWS, SC_ALIGN) for b in range(SC_INFLIGHT)]
            copies = []
            for b in range(SC_INFLIGHT):
                pltpu.sync_copy(idx_hbm.at[pl.ds(offs[b], SC_ROWS)], idx_v[b])
                copies.append(pltpu.async_copy(table_hbm.at[idx_v[b]], rows_v[b], sems[b]))
            for b in range(SC_INFLIGHT):
                copies[b].wait()
                pltpu.sync_copy(rows_v[b], out_hbm.at[pl.ds(offs[b], SC_ROWS)])

    return gather(table, idx)


def _sc_scatter_rows(rows, idx, n_out):
    n_src, width = rows.shape
    n_copies = idx.shape[0] // n_src
    assert n_copies * n_src == idx.shape[0]
    per_worker = _sc_split(n_src)

    @functools.partial(
        pl.kernel, mesh=_sc_mesh(),
        out_type=jax.ShapeDtypeStruct((n_out, width), rows.dtype),
        scratch_types=[[pltpu.VMEM((SC_ROWS,), I32) for _ in range(n_copies)],
                       pltpu.VMEM((SC_ROWS, width), rows.dtype),
                       [pltpu.SemaphoreType.DMA for _ in range(n_copies)]],
        name="sc_scatter",
    )
    def scatter(rows_hbm, idx_hbm, out_hbm, idx_v, rows_v, sems):
        worker = lax.axis_index("s") * SC_CORES + lax.axis_index("c")
        base = worker * per_worker

        @pl.loop(0, per_worker // SC_ROWS)
        def _(step):
            src = pl.multiple_of(base + step * SC_ROWS, SC_ALIGN)
            pltpu.sync_copy(rows_hbm.at[pl.ds(src, SC_ROWS)], rows_v)
            copies = []
            for cpy in range(n_copies):
                off = pl.multiple_of(cpy * n_src + src, SC_ALIGN)
                pltpu.sync_copy(idx_hbm.at[pl.ds(off, SC_ROWS)], idx_v[cpy])
                copies.append(pltpu.async_copy(rows_v, out_hbm.at[idx_v[cpy]], sems[cpy]))
            for cp in copies:
                cp.wait()

    return scatter(rows, idx)


def _experts_kernel(be_ref, bv_ref, nused_ref, next_ref, slot_ref,
                    xs_ref, wgu_hbm, bg_ref, bu_ref, wd_hbm, bd_ref, perm_ref,
                    o_ref, wgu_buf, wd_buf, wg_s, wu_s, wd_s, sem):
    i = pl.program_id(0)
    e = be_ref[i]
    used = i < nused_ref[0]
    changed = ((i == 0) | (e != be_ref[jnp.maximum(i - 1, 0)])) & used
    slot = slot_ref[e]

    def weight_copies(expert, buf_slot):
        return (pltpu.make_async_copy(wgu_hbm.at[expert], wgu_buf.at[buf_slot], sem.at[buf_slot, 0]),
                pltpu.make_async_copy(wd_hbm.at[expert], wd_buf.at[buf_slot], sem.at[buf_slot, 1]))

    @pl.when((i == 0) & used)
    def _():
        for cp in weight_copies(e, slot):
            cp.start()

    @pl.when(changed)
    def _():
        for cp in weight_copies(e, slot):
            cp.wait()
        nxt = next_ref[e]

        @pl.when(nxt >= 0)
        def _():
            for cp in weight_copies(nxt, 1 - slot):
                cp.start()

        half = LANES
        for cblk in range(2 * D_EXPERT // (2 * half)):
            wt = wgu_buf[slot, :, cblk * 2 * half:(cblk + 1) * 2 * half].astype(BF16)
            sep = jnp.dot(wt, perm_ref[...], preferred_element_type=F32).astype(BF16)
            wg_s[:, cblk * half:(cblk + 1) * half] = sep[:, :half]
            wu_s[:, cblk * half:(cblk + 1) * half] = sep[:, half:]
        wd_s[...] = wd_buf[slot].astype(BF16)

    @pl.when(used)
    def _():
        rows = lax.broadcasted_iota(I32, xs_ref.shape, 0)
        xb = _unpack_halves(jnp.where(rows < bv_ref[i], xs_ref[...], 0)).astype(BF16)
        gate = jnp.dot(xb, wg_s[...], preferred_element_type=F32) + bg_ref[0]
        up = jnp.dot(xb, wu_s[...], preferred_element_type=F32) + bu_ref[0]
        gate = jnp.minimum(gate, SWIGLU_LIMIT)
        up = jnp.clip(up, -SWIGLU_LIMIT, SWIGLU_LIMIT)
        glu = gate * _sigmoid(gate * SWIGLU_ALPHA)
        act = ((up + 1.0) * glu).astype(BF16)
        o_ref[...] = _pack_halves(jnp.dot(act, wd_s[...], preferred_element_type=F32) + bd_ref[0])

    @pl.when(jnp.logical_not(used))
    def _():
        o_ref[...] = jnp.zeros_like(o_ref)


def _experts(blk_expert, blk_valid, n_used, next_expert, buf_slot, xs, w_gate_up, b_gate_up, w_down,
             b_down):
    p_rows = xs.shape[0]
    d = 2 * xs.shape[1]
    n_blocks = p_rows // EXPERT_ROWS
    bg = b_gate_up[:, 0::2].reshape(N_EXPERTS, 1, D_EXPERT)
    bu = b_gate_up[:, 1::2].reshape(N_EXPERTS, 1, D_EXPERT)
    bd = b_down.reshape(N_EXPERTS, 1, d)
    perm_np = np.zeros((2 * LANES, 2 * LANES), np.float32)
    perm_np[2 * np.arange(LANES), np.arange(LANES)] = 1.0
    perm_np[2 * np.arange(LANES) + 1, LANES + np.arange(LANES)] = 1.0
    perm = jnp.asarray(perm_np).astype(BF16)
    by_expert = lambda shape: pl.BlockSpec((1,) + shape, lambda i, be, *_: (be[i], 0, 0))
    grid_spec = pltpu.PrefetchScalarGridSpec(
        num_scalar_prefetch=5,
        grid=(n_blocks,),
        in_specs=[pl.BlockSpec((EXPERT_ROWS, d // 2), lambda i, *_: (i, 0)),
                  pl.BlockSpec(memory_space=pl.ANY), by_expert((1, D_EXPERT)), by_expert((1, D_EXPERT)),
                  pl.BlockSpec(memory_space=pl.ANY), by_expert((1, d)),
                  pl.BlockSpec(perm.shape, lambda i, *_: (0, 0))],
        out_specs=pl.BlockSpec((EXPERT_ROWS, d // 2), lambda i, *_: (i, 0)),
        scratch_shapes=[pltpu.VMEM((2, d, 2 * D_EXPERT), F32), pltpu.VMEM((2, D_EXPERT, d), F32),
                        pltpu.VMEM((d, D_EXPERT), BF16), pltpu.VMEM((d, D_EXPERT), BF16),
                        pltpu.VMEM((D_EXPERT, d), BF16), pltpu.SemaphoreType.DMA((2, 2))],
    )
    return pl.pallas_call(
        _experts_kernel,
        grid_spec=grid_spec,
        out_shape=jax.ShapeDtypeStruct((p_rows, d // 2), I32),
        compiler_params=_cparams(("arbitrary",)),
        name="experts",
    )(blk_expert, blk_valid, n_used, next_expert, buf_slot, xs, w_gate_up, bg, bu, w_down, bd, perm)


def _combine_kernel(yg_ref, x1_ref, w_ref, g_ref, o_ref):
    acc = x1_ref[...]
    wts = w_ref[...]
    for kk in range(TOP_K):
        acc = acc + _unpack_halves(yg_ref[kk]) * wts[:, kk:kk + 1]
    o_ref[...] = _rms(acc, g_ref[...])


def _combine(yg, stream, wts, g_final, tb, first_tile):
    t, d = stream.shape
    half = yg.shape[2]
    tile = lambda n: pl.BlockSpec((tb, n), lambda i: (i + first_tile, 0))
    return pl.pallas_call(
        _combine_kernel,
        grid=(yg.shape[1] // tb,),
        in_specs=[pl.BlockSpec((TOP_K, tb, half), lambda i: (0, i, 0)), tile(d), tile(LANES),
                  pl.BlockSpec((1, d), lambda i: (0, 0))],
        out_specs=tile(d),
        out_shape=jax.ShapeDtypeStruct((t, d), F32),
        input_output_aliases={1: 0},
        compiler_params=_cparams(("parallel",)),
        name="combine",
    )(yg, stream, wts, g_final.reshape(1, d))


def _layer(x, mem, w_in, w_out, w_mem_kv, g_mix, g_mem, g_ffn, bias_tables, mu, w0, w_up, a0, a_up,
           g_up, k_k, k_a, r_k, gn_g, gn_b, w_router, b_router, w_gate_up, b_gate_up, w_down, b_down,
           g_last):
    b, s, d = x.shape
    m = mem.shape[1]
    t = b * s
    x2 = x.reshape(t, d)

    p_moba, p_rwkv, p_memq = _in_proj(x2, g_mix, w_in.astype(BF16), PROJ_TOKENS)
    y_moba = _moba_attention(p_moba.reshape(-1, s, LANES), *bias_tables, b)
    y_rwkv = _rwkv(p_rwkv.reshape(b, s, RWKV_COLS), mu, w0, w_up, a0, a_up, g_up, k_k, k_a, r_k, gn_g, gn_b)
    mkv = _norm_matmul(mem.reshape(b * m, d), g_mem, w_mem_kv.astype(BF16), PROJ_TOKENS, "mem_kv")
    y_mem = _mem_attention(p_memq.reshape(b, s, MEM_W), mkv.reshape(b, m, 2 * MEM_W), MEM_Q_TOKENS)

    tb = ROUTE_TOKENS
    x1, h2, idx_o, rank_o, wgt_p, cnt = _out_router(
        x2, y_moba.reshape(t, MOBA_W), y_rwkv.reshape(t, RWKV_W), y_mem.reshape(t, MEM_W),
        w_out, g_ffn, w_router, b_router, tb)

    counts = cnt[:, 0].astype(I32)
    padded = (counts + EXPERT_ROWS - 1) // EXPERT_ROWS * EXPERT_ROWS
    pad_ends = jnp.cumsum(padded)
    pad_starts = (pad_ends - padded).astype(I32)
    n_blocks = (t * TOP_K) // EXPERT_ROWS + N_EXPERTS
    blk_start = jnp.arange(n_blocks, dtype=I32) * EXPERT_ROWS
    blk_expert = jnp.minimum(jnp.sum(blk_start[:, None] >= pad_ends[None, :], axis=1),
                             N_EXPERTS - 1).astype(I32)
    of_block = blk_expert[:, None] == jnp.arange(N_EXPERTS, dtype=I32)[None, :]
    rows_left = jnp.sum(jnp.where(of_block, (counts + pad_starts)[None, :], 0), axis=1) - blk_start
    blk_valid = jnp.clip(rows_left, 0, EXPERT_ROWS)
    n_used = (pad_ends[-1:] // EXPERT_ROWS).astype(I32)
    has_rows = counts > 0
    first_from = lax.cummin(jnp.where(has_rows, jnp.arange(N_EXPERTS, dtype=I32), N_EXPERTS), reverse=True)
    next_expert = jnp.concatenate([first_from[1:], jnp.full((1,), N_EXPERTS, I32)])
    next_expert = jnp.where(next_expert < N_EXPERTS, next_expert, -1).astype(I32)
    buf_slot = ((jnp.cumsum(has_rows.astype(I32)) - has_rows.astype(I32)) % 2).astype(I32)
    idx_kt = jnp.swapaxes(idx_o[:, :TOP_K, :], 0, 1).reshape(TOP_K, t)
    rank_kt = jnp.swapaxes(rank_o[:, :TOP_K, :], 0, 1).reshape(TOP_K, t)
    experts = jnp.arange(N_EXPERTS, dtype=I32)[:, None, None]
    start_kt = jnp.sum(jnp.where(idx_kt[None] == experts, pad_starts[:, None, None], 0), axis=0)
    dest = (start_kt + rank_kt).reshape(TOP_K * t)

    xs = _sc_scatter_rows(h2, dest, n_blocks * EXPERT_ROWS)
    ys = _experts(blk_expert, blk_valid.astype(I32), n_used, next_expert, buf_slot, xs,
                  w_gate_up, b_gate_up, w_down, b_down)
    dest_kt = dest.reshape(TOP_K, t)
    out = x1
    for part in range(COMBINE_PARTS):
        t_part = t // COMBINE_PARTS
        rows = dest_kt[:, part * t_part:(part + 1) * t_part].reshape(TOP_K * t_part)
        yg = _sc_gather_rows(ys, rows).reshape(TOP_K, t_part, d // 2)
        out = _combine(yg, out, wgt_p, g_last, tb, part * (t_part // tb))
    return out.reshape(b, s, d)


def kernel(x, mem, w_in, w_out, w_mem_kv, g_mix, g_mem, g_ffn, g_final, rel_bias, rwkv_mu, rwkv_w0,
           rwkv_w_up, rwkv_a0, rwkv_a_up, rwkv_g_up, rwkv_k_k, rwkv_k_a, rwkv_r_k, rwkv_gn_g, rwkv_gn_b,
           w_router, b_router, w_gate_up, b_gate_up, w_down, b_down):
    depth = w_in.shape[0]
    assert depth == 1, "the final norm is fused into the last layer's combine kernel"
    bias_tables = _moba_bias_tables(rel_bias)
    l = 0
    return _layer(x, mem, w_in[l], w_out[l], w_mem_kv[l], g_mix[l], g_mem[l], g_ffn[l], bias_tables,
                  rwkv_mu[l], rwkv_w0[l], rwkv_w_up[l], rwkv_a0[l], rwkv_a_up[l], rwkv_g_up[l],
                  rwkv_k_k[l], rwkv_k_a[l], rwkv_r_k[l], rwkv_gn_g[l], rwkv_gn_b[l], w_router[l],
                  b_router[l], w_gate_up[l], b_gate_up[l], w_down[l], b_down[l], g_final)
```

```python
import functools
import math

import numpy as np
import jax
import jax.numpy as jnp
from jax import lax
from jax.experimental import pallas as pl
from jax.experimental.pallas import tpu as pltpu
from jax.experimental.pallas import tpu_sc as plsc

F32 = jnp.float32
BF16 = jnp.bfloat16
I32 = jnp.int32
HI = lax.Precision.HIGHEST

D_MODEL = 1024
HEAD_DIM = 64
MOBA_HEADS = 6
RWKV_HEADS = 6
MEM_HEADS = 4
MOBA_W = MOBA_HEADS * HEAD_DIM
RWKV_W = RWKV_HEADS * HEAD_DIM
MEM_W = MEM_HEADS * HEAD_DIM
MOBA_BLOCK = 256
MOBA_TOPK = 3
N_BUCKETS = 32
MAX_DISTANCE = 128
DECAY_LORA = 64
AAA_LORA = 64
GATE_LORA = 128
RWKV_COLS = 3 * RWKV_W + DECAY_LORA + AAA_LORA + GATE_LORA
RWKV_GN_EPS = 64e-5
IN_COLS = 3 * MOBA_W + RWKV_COLS + MEM_W
N_EXPERTS = 32
TOP_K = 4
D_EXPERT = D_MODEL
SWIGLU_ALPHA = 1.702
SWIGLU_LIMIT = 7.0
RMS_EPS = 1e-5

LANES = 128
SUBLANES = 8
BF16_SUBLANES = 16
PROJ_TOKENS = 1024
MEM_Q_TOKENS = 512
RWKV_CHUNK = 128
RWKV_SEQS_PER_STEP = 4
EXPERT_ROWS = 256
ROUTE_TOKENS = 1024
SC_CORES = 2
SC_SUBCORES = 16
SC_ROWS = 64
SC_ALIGN = 8
SC_INFLIGHT = 2
COMBINE_PARTS = 4
VMEM_LIMIT = 56 * 1024 * 1024
NEG_INF = float("-inf")
HIGH_HALF = -65536
LOG2_E = 1.4426950408889634
MOBA_VROWS = HEAD_DIM + BF16_SUBLANES


def _cparams(sem):
    return pltpu.CompilerParams(dimension_semantics=sem, vmem_limit_bytes=VMEM_LIMIT)


def _rms(x, g):
    return x * lax.rsqrt(jnp.mean(x * x, axis=-1, keepdims=True) + RMS_EPS) * g


def _bdot(a, b):
    return jnp.dot(a.astype(BF16), b.astype(BF16), preferred_element_type=F32)


def _hdot(a, b):
    return jnp.dot(a, b, preferred_element_type=F32, precision=HI)


def _dot_nt(a, b):
    return lax.dot_general(a, b, (((1,), (1,)), ((), ())), preferred_element_type=F32)


def _dot_tn(a, b):
    return lax.dot_general(a, b, (((0,), (0,)), ((), ())), preferred_element_type=F32)


def _split_bf16(x, terms):
    parts = []
    for _ in range(terms):
        hi = x.astype(BF16)
        parts.append(hi)
        x = x - hi.astype(F32)
    return parts


def _dot_exact_rhs(x, m_bf16, terms):
    acc = None
    for part in _split_bf16(x, terms):
        d = jnp.dot(part, m_bf16, preferred_element_type=F32)
        acc = d if acc is None else acc + d
    return acc


def _pack_halves(x):
    n = x.shape[1] // 2
    lo = pltpu.bitcast(x[:, :n].astype(BF16).astype(F32), I32)
    hi = pltpu.bitcast(x[:, n:].astype(BF16).astype(F32), I32)
    return (hi & HIGH_HALF) | lax.shift_right_logical(lo, 16)


def _unpack_halves(w):
    lo = pltpu.bitcast(w << 16, F32)
    hi = pltpu.bitcast(w & HIGH_HALF, F32)
    return jnp.concatenate([lo, hi], axis=1)


def _norm_matmul_kernel(x_ref, g_ref, w_ref, o_ref):
    h = _rms(x_ref[...], g_ref[...])
    o_ref[...] = jnp.dot(h.astype(BF16), w_ref[...], preferred_element_type=F32)


def _norm_matmul(x, g, w_bf16, tm, name):
    t, d = x.shape
    n = w_bf16.shape[1]
    tm = min(tm, t)
    return pl.pallas_call(
        _norm_matmul_kernel,
        grid=(t // tm,),
        in_specs=[pl.BlockSpec((tm, d), lambda i: (i, 0)),
                  pl.BlockSpec((1, d), lambda i: (0, 0)),
                  pl.BlockSpec((d, n), lambda i: (0, 0))],
        out_specs=pl.BlockSpec((tm, n), lambda i: (i, 0)),
        out_shape=jax.ShapeDtypeStruct((t, n), F32),
        compiler_params=_cparams(("parallel",)),
        name=name,
    )(x, g.reshape(1, d), w_bf16)


def _in_proj_kernel(x_ref, g_ref, w_ref, moba_ref, rwkv_ref, memq_ref):
    h = _rms(x_ref[...], g_ref[...])
    p = jnp.dot(h.astype(BF16), w_ref[...], preferred_element_type=F32)
    for j in range(3 * MOBA_W // LANES):
        moba_ref[j] = p[:, j * LANES:(j + 1) * LANES]
    rwkv_ref[...] = p[:, 3 * MOBA_W:3 * MOBA_W + RWKV_COLS]
    memq_ref[...] = p[:, 3 * MOBA_W + RWKV_COLS:]


def _in_proj(x, g, w_bf16, tm):
    t, d = x.shape
    n_slabs = 3 * MOBA_W // LANES
    return pl.pallas_call(
        _in_proj_kernel,
        grid=(t // tm,),
        in_specs=[pl.BlockSpec((tm, d), lambda i: (i, 0)),
                  pl.BlockSpec((1, d), lambda i: (0, 0)),
                  pl.BlockSpec((d, IN_COLS), lambda i: (0, 0))],
        out_specs=[pl.BlockSpec((n_slabs, tm, LANES), lambda i: (0, i, 0)),
                   pl.BlockSpec((tm, RWKV_COLS), lambda i: (i, 0)),
                   pl.BlockSpec((tm, MEM_W), lambda i: (i, 0))],
        out_shape=[jax.ShapeDtypeStruct((n_slabs, t, LANES), F32),
                   jax.ShapeDtypeStruct((t, RWKV_COLS), F32),
                   jax.ShapeDtypeStruct((t, MEM_W), F32)],
        compiler_params=_cparams(("parallel",)),
        name="in_proj",
    )(x, g.reshape(1, d), w_bf16)


def _t5_bucket_np(dist):
    n = np.maximum(dist, 0)
    max_exact = N_BUCKETS // 2
    nf = np.maximum(n, 1).astype(np.float64)
    large = max_exact + (np.log(nf / max_exact) / math.log(MAX_DISTANCE / max_exact)
                         * (N_BUCKETS - max_exact)).astype(np.int64)
    large = np.minimum(large, N_BUCKETS - 1)
    return np.where(n < max_exact, n, large)


def _moba_bias_tables(rel_bias):
    n = MOBA_BLOCK
    assert np.all(_t5_bucket_np(np.arange(n + 1, 64 * n)) == N_BUCKETS - 1)
    bias_t = rel_bias.astype(F32).T * LOG2_E

    def by_distance(dist):
        bucket = jnp.asarray(_t5_bucket_np(dist), I32)[None]
        tab = jnp.zeros((MOBA_HEADS, dist.shape[0]), F32)
        for b in range(N_BUCKETS):
            tab = jnp.where(bucket == b, bias_t[:, b][:, None], tab)
        return tab

    d = np.arange(2 * n)
    g_own = jnp.where(jnp.asarray(d < n)[None], by_distance(np.where(d < n, d, 0)), NEG_INF)
    g_prev = by_distance(np.where(d < n, n + d, d - n))
    far = bias_t[:, N_BUCKETS - 1]
    return g_own[:, None, :], g_prev[:, None, :], far


def _moba_kernel(far_ref, q_ref, k_ref, v_ref, g0_ref, g1_ref, o_ref,
                 qt_s, kb_s, vt_s, sc_s, t0_ref, t1_ref, *, nb):
    hp = pl.program_id(1)
    blk = MOBA_BLOCK
    scale = HEAD_DIM ** -0.5 * LOG2_E
    row_head = lax.broadcasted_iota(I32, (LANES, 1), 0) // HEAD_DIM
    lane_head = lax.broadcasted_iota(I32, (1, LANES), 1) // HEAD_DIM
    ones_rows = jnp.ones((MOBA_VROWS - HEAD_DIM, blk), BF16)

    for h in range(2):
        for g_ref, t_ref in ((g0_ref, t0_ref), (g1_ref, t1_ref)):
            rows = jnp.broadcast_to(g_ref[h], (blk, 2 * blk))
            t_ref[h] = pltpu.roll(rows, 0, 1, stride=1, stride_axis=0)[:, :blk]

    kmean_rows = []
    for j in range(nb):
        sl = pl.ds(j * blk, blk)
        qt_s[j] = (q_ref[0, sl, :] * scale).T
        vt = v_ref[0, sl, :].T.astype(BF16)
        for h in range(2):
            vt_s[j, h * MOBA_VROWS:h * MOBA_VROWS + HEAD_DIM, :] = vt[h * HEAD_DIM:(h + 1) * HEAD_DIM, :]
            vt_s[j, h * MOBA_VROWS + HEAD_DIM:(h + 1) * MOBA_VROWS, :] = ones_rows
        kj = k_ref[0, sl, :]
        kb_s[j] = kj.astype(BF16)
        kmean_rows.append(jnp.mean(kj, axis=0, keepdims=True))
    kmean = jnp.concatenate(kmean_rows, axis=0)
    km2 = jnp.concatenate([jnp.where(lane_head == 0, kmean, 0.0),
                           jnp.where(lane_head == 1, kmean, 0.0)], axis=0)
    blk_iota = lax.broadcasted_iota(I32, (nb, blk), 0)

    def score_pass(i):
        qt = qt_s[i]
        negs = [None, None]
        if i > MOBA_TOPK:
            gate2 = _hdot(km2, qt)
            for h in range(2):
                g = gate2[h * nb:(h + 1) * nb, :]
                cnt = jnp.zeros((nb, blk), I32)
                for m in range(i):
                    gm = g[m:m + 1, :]
                    beats = (gm > g) | ((gm == g) & (m < blk_iota))
                    cnt = cnt + jnp.where(beats, 1, 0)
                negs[h] = jnp.where(cnt < MOBA_TOPK, 0.0, NEG_INF)
        col_max = []
        for h in range(2):
            qt_h = jnp.where(row_head == h, qt, 0.0).astype(BF16)
            far_bias = far_ref[2 * hp + h]
            m_run = None
            shifts = []
            for j in range(i + 1):
                s = jnp.dot(kb_s[j], qt_h, preferred_element_type=F32)
                if j >= i - 1:
                    bias = t0_ref[h] if j == i else t1_ref[h]
                    if j < i and negs[h] is not None:
                        bias = bias + negs[h][j:j + 1, :]
                    s = s + bias
                    cm = jnp.max(s, axis=0, keepdims=True)
                    shifts.append(None)
                else:
                    row = jnp.full((1, blk), far_bias, F32)
                    if negs[h] is not None:
                        row = row + negs[h][j:j + 1, :]
                    cm = jnp.max(s, axis=0, keepdims=True) + row
                    shifts.append(row)
                sc_s[i % 2, h, j] = s
                m_run = cm if m_run is None else jnp.maximum(m_run, cm)
            col_max.append([m_run if row is None else m_run - row for row in shifts])
        return col_max

    def prob_pass(i, col_max):
        outs = []
        for h in range(2):
            offsets = col_max[h]
            probs = jnp.concatenate([jnp.exp2(sc_s[i % 2, h, j] - offsets[j]).astype(BF16)
                                     for j in range(i + 1)], axis=0)
            vals = jnp.concatenate([vt_s[j, h * MOBA_VROWS:(h + 1) * MOBA_VROWS, :]
                                    for j in range(i + 1)], axis=1)
            acc = jnp.dot(vals, probs, preferred_element_type=F32)
            outs.append(acc[:HEAD_DIM] / acc[HEAD_DIM:HEAD_DIM + 1])
        o_t = jnp.concatenate(outs, axis=0)
        o_ref[0, pl.ds(i * blk, blk), :] = o_t.T

    pending = None
    for i in range(nb):
        col_max = score_pass(i)
        if pending is not None:
            prob_pass(*pending)
        pending = (i, col_max)
    prob_pass(*pending)


def _moba_attention(p3, g_own, g_prev, far, b):
    s = p3.shape[1]
    nb = s // MOBA_BLOCK
    n_pairs = MOBA_HEADS // 2
    blk_spec = lambda off: pl.BlockSpec((1, s, LANES), lambda bi, hp: ((off + hp) * b + bi, 0, 0))
    tab_spec = pl.BlockSpec((2, 1, 2 * MOBA_BLOCK), lambda bi, hp: (hp, 0, 0))
    return pl.pallas_call(
        functools.partial(_moba_kernel, nb=nb),
        grid=(b, n_pairs),
        in_specs=[pl.BlockSpec(memory_space=pltpu.SMEM),
                  blk_spec(0), blk_spec(n_pairs), blk_spec(2 * n_pairs), tab_spec, tab_spec],
        out_specs=pl.BlockSpec((1, s, LANES), lambda bi, hp: (bi, 0, hp)),
        out_shape=jax.ShapeDtypeStruct((b, s, MOBA_W), F32),
        scratch_shapes=[pltpu.VMEM((nb, LANES, MOBA_BLOCK), F32),
                        pltpu.VMEM((nb, MOBA_BLOCK, LANES), BF16),
                        pltpu.VMEM((nb, 2 * MOBA_VROWS, MOBA_BLOCK), BF16),
                        pltpu.VMEM((2, 2, nb, MOBA_BLOCK, MOBA_BLOCK), F32),
                        pltpu.VMEM((2, MOBA_BLOCK, MOBA_BLOCK), F32),
                        pltpu.VMEM((2, MOBA_BLOCK, MOBA_BLOCK), F32)],
        compiler_params=_cparams(("parallel", "parallel")),
        name="moba",
    )(far, p3, p3, p3, g_own, g_prev)


def _softplus(z):
    return jnp.maximum(z, 0.0) + jnp.log(1.0 + jnp.exp(-jnp.abs(z)))


def _sigmoid(z):
    return 1.0 / (1.0 + jnp.exp(-z))


def _rwkv_kernel(r_ref, k_ref, v_ref, wa_ref, g_ref,
                 mu_r_ref, mu_k_ref, mu_v_ref, mu_wa_ref, mu_g_ref,
                 w0_ref, wup_ref, a0_ref, aup_ref, gup_ref, kk_ref, ka_ref, rk_ref,
                 gng_ref, gnb_ref, hsum_ref, tri_ref,
                 o_ref, st_s, prev_r, prev_k, prev_v, prev_wa, prev_g, *, n_seq):
    c = RWKV_CHUNK
    mid = c // 2

    @pl.when(pl.program_id(1) == 0)
    def _():
        st_s[...] = jnp.zeros_like(st_s)
        prev_r[...] = jnp.zeros_like(prev_r)
        prev_k[...] = jnp.zeros_like(prev_k)
        prev_v[...] = jnp.zeros_like(prev_v)
        prev_wa[...] = jnp.zeros_like(prev_wa)
        prev_g[...] = jnp.zeros_like(prev_g)

    n_pairs = RWKV_HEADS // 2
    hsum = hsum_ref[...]
    tri = tri_ref[...]
    row = lax.broadcasted_iota(I32, (c, c), 0)
    col = lax.broadcasted_iota(I32, (c, c), 1)
    strict = row > col
    incl = row >= col
    lane_half = lax.broadcasted_iota(I32, (1, LANES), 1) // HEAD_DIM
    hmask = hsum.astype(F32)
    dot = functools.partial(jnp.dot, preferred_element_type=F32)
    lanes = lambda x, jp: x[:, jp * LANES:(jp + 1) * LANES]

    def group_sum(x):
        return jnp.concatenate([_dot_exact_rhs(lanes(x, jp), hsum, 2) for jp in range(n_pairs)], axis=1)

    def two_heads(x):
        return jnp.concatenate([jnp.where(lane_half == 0, x, 0.0), jnp.where(lane_half == 1, x, 0.0)],
                               axis=0).astype(BF16)

    def side(a, b):
        return jnp.concatenate([a, b], axis=1)

    def mix(g, x_ref, prev_ref, mu_ref):
        x = x_ref[g]
        rows = lax.broadcasted_iota(I32, x.shape, 0)
        shifted = jnp.where(rows == 0, prev_ref[g], pltpu.roll(x, 1, axis=0))
        prev_ref[g] = x[c - 1:c, :]
        return x + (shifted - x) * mu_ref[...]

    def prepare(g):
        r = mix(g, r_ref, prev_r, mu_r_ref)
        k = mix(g, k_ref, prev_k, mu_k_ref)
        v = mix(g, v_ref, prev_v, mu_v_ref)
        xwa = mix(g, wa_ref, prev_wa, mu_wa_ref)
        xg = mix(g, g_ref, prev_g, mu_g_ref)

        w = -_softplus(-(w0_ref[...] + _bdot(jnp.tanh(xwa), wup_ref[...]))) - 0.5
        logd = -jnp.exp(w)
        a = _sigmoid(a0_ref[...] + _bdot(xwa, aup_ref[...]))
        gate = _bdot(_sigmoid(xg), gup_ref[...])
        kk = k * kk_ref[...]
        kk = kk / jnp.maximum(jnp.sqrt(group_sum(kk * kk)), 1e-12)
        k2 = k * (1.0 + (a - 1.0) * ka_ref[...])

        parts = _split_bf16(logd, 3)
        cum = dot(tri, parts[0]) + dot(tri, parts[1]) + dot(tri, parts[2])
        cum_last = cum[c - 1:c, :]
        ref = cum[mid - 1:mid, :]
        rel = cum - ref
        g_in = jnp.exp(rel)
        g_inv = jnp.exp(-rel)
        g_tail = jnp.exp(cum_last - cum)
        a_t = -kk * jnp.exp(rel - logd)
        b_t = kk * a * g_inv
        k_t = k2 * g_inv
        r_t = r * g_in
        b_hat = kk * a * g_tail
        k_hat = k2 * g_tail

        return dict(r=r, k2=k2, v=v, gate=gate, b_hat=b_hat, k_hat=k_hat, ref_scale=jnp.exp(ref),
                    decay=jnp.exp(cum_last), bk=jnp.concatenate([b_t, k_t], axis=0).astype(BF16),
                    ar_all=jnp.concatenate([a_t, r_t], axis=0))

    def chunk_products(g, el):
        v, b_hat, k_hat, ref_scale, decay = el["v"], el["b_hat"], el["k_hat"], el["ref_scale"], el["decay"]
        bk, ar_all = el["bk"], el["ar_all"]
        per_pair = []
        for jp in range(n_pairs):
            st = st_s[g, jp]
            st_ref = (st * lanes(ref_scale, jp)).astype(BF16)
            both = _dot_nt(lanes(ar_all, jp).astype(BF16), st_ref)
            quads = []
            for half in range(2):
                ar = jnp.where(lane_half == half, lanes(ar_all, jp), 0.0)
                m4 = _dot_nt(ar.astype(BF16), lanes(bk, jp))
                quads.append((jnp.where(strict, m4[:c, :c], 0.0),
                              jnp.where(strict, m4[:c, c:], 0.0).astype(BF16),
                              jnp.where(incl, m4[c:, :c], 0.0).astype(BF16),
                              jnp.where(incl, m4[c:, c:], 0.0).astype(BF16)))
            per_pair.append(dict(
                st=st, p0=both[:c], o0=both[c:], v_rows=two_heads(lanes(v, jp)),
                l_ab=[q[0] for q in quads], l_ak=side(quads[0][1], quads[1][1]),
                m_rb=side(quads[0][2], quads[1][2]), m_rk=side(quads[0][3], quads[1][3]),
                v=lanes(v, jp).astype(BF16), b_hat=lanes(b_hat, jp).astype(BF16),
                k_hat=lanes(k_hat, jp).astype(BF16), decay=lanes(decay, jp)))
        return dict(r=el["r"], k2=el["k2"], v=v, gate=el["gate"], pairs=per_pair)

    seqs = {}

    def solve(group):
        heads = [(g, jp, half) for g in group for jp in range(n_pairs) for half in range(2)]
        pair_ids = [(g, jp) for g in group for jp in range(n_pairs)]

        pw = {gh: seqs[gh[0]]["pairs"][gh[1]]["l_ab"][gh[2]].astype(BF16) for gh in heads}
        u_loc = {(g, jp): seqs[g]["pairs"][jp]["p0"]
                 + dot(seqs[g]["pairs"][jp]["l_ak"], seqs[g]["pairs"][jp]["v_rows"]) for g, jp in pair_ids}
        span = 1
        while span < c:
            u_loc = {(g, jp): u_loc[(g, jp)]
                     + dot(side(pw[(g, jp, 0)], pw[(g, jp, 1)]), two_heads(u_loc[(g, jp)]))
                     for g, jp in pair_ids}
            span *= 2
            if span < c:
                pw = {gh: dot(pw[gh], pw[gh]).astype(BF16) for gh in heads}

        for g in group:
            sq = seqs[g]
            outs = []
            for jp in range(n_pairs):
                pr = sq["pairs"][jp]
                u = u_loc[(g, jp)]
                outs.append(pr["o0"] + dot(pr["m_rb"], two_heads(u)) + dot(pr["m_rk"], pr["v_rows"]))
                upd = _dot_tn(u.astype(BF16), pr["b_hat"]) + _dot_tn(pr["v"], pr["k_hat"])
                st_s[g, jp] = pr["st"] * pr["decay"] + hmask * upd
            o = jnp.concatenate(outs, axis=1)

            inv_n = 1.0 / HEAD_DIM
            mean = group_sum(o) * inv_n
            dev = o - mean
            var = group_sum(dev * dev) * inv_n
            y = dev * lax.rsqrt(var + RWKV_GN_EPS) * gng_ref[...] + gnb_ref[...]
            y = y + group_sum(sq["r"] * sq["k2"] * rk_ref[...]) * sq["v"]
            o_ref[g] = y * sq["gate"]

    elementwise = None
    for g in range(n_seq):
        nxt = prepare(g)
        if elementwise is not None:
            seqs[g - 1] = chunk_products(g - 1, elementwise)
        elementwise = nxt
    seqs[n_seq - 1] = chunk_products(n_seq - 1, elementwise)
    solve(list(range(n_seq)))


def _rwkv(p3, mu, w0, w_up, a0, a_up, g_up, k_k, k_a, r_k, gn_g, gn_b):
    b, s, _ = p3.shape
    c = RWKV_CHUNK
    w = RWKV_W
    n_seq = RWKV_SEQS_PER_STEP if b % RWKV_SEQS_PER_STEP == 0 else 1
    base = 0
    assert (base + 3 * w) % LANES == 0
    lora = DECAY_LORA + AAA_LORA

    def col_spec(width, off_cols):
        assert off_cols % width == 0
        return pl.BlockSpec((n_seq, c, width), lambda bi, ci: (bi, ci, off_cols // width))

    row = lambda x: x.reshape(1, -1).astype(F32)
    const = lambda shape: pl.BlockSpec(shape, lambda bi, ci: (0,) * len(shape))
    wup_pad = jnp.concatenate([w_up, jnp.zeros((AAA_LORA, w), F32)], axis=0).astype(BF16)
    aup_pad = jnp.concatenate([jnp.zeros((DECAY_LORA, w), F32), a_up], axis=0).astype(BF16)
    head = np.arange(LANES) // HEAD_DIM
    hsum = jnp.asarray((head[:, None] == head[None, :]).astype(np.float32)).astype(BF16)
    tri = jnp.asarray(np.tril(np.ones((c, c), np.float32))).astype(BF16)
    vec_args = [mu[:w], mu[w:2 * w], mu[2 * w:3 * w], mu[3 * w:3 * w + lora], mu[3 * w + lora:],
                w0, None, a0, None, None, k_k, k_a, r_k.reshape(-1), gn_g, gn_b]
    args = [p3, p3, p3, p3, p3]
    specs = [col_spec(w, base), col_spec(w, base + w), col_spec(w, base + 2 * w),
             col_spec(lora, base + 3 * w), col_spec(GATE_LORA, base + 3 * w + lora)]
    mats = {6: wup_pad, 8: aup_pad, 9: g_up.astype(BF16)}
    for idx, a in enumerate(vec_args):
        arr = mats[idx] if a is None else row(a)
        args.append(arr)
        specs.append(const(arr.shape))
    for arr in (hsum, tri):
        args.append(arr)
        specs.append(const(arr.shape))
    return pl.pallas_call(
        functools.partial(_rwkv_kernel, n_seq=n_seq),
        grid=(b // n_seq, s // c),
        in_specs=specs,
        out_specs=pl.BlockSpec((n_seq, c, w), lambda bi, ci: (bi, ci, 0)),
        out_shape=jax.ShapeDtypeStruct((b, s, w), F32),
        scratch_shapes=[pltpu.VMEM((n_seq, RWKV_HEADS // 2, LANES, LANES), F32),
                        pltpu.VMEM((n_seq, 1, w), F32),
                        pltpu.VMEM((n_seq, 1, w), F32), pltpu.VMEM((n_seq, 1, w), F32),
                        pltpu.VMEM((n_seq, 1, lora), F32), pltpu.VMEM((n_seq, 1, GATE_LORA), F32)],
        compiler_params=_cparams(("parallel", "arbitrary")),
        name="rwkv",
    )(*args)


def _mem_attn_kernel(q_ref, kv_ref, o_ref):
    scale = HEAD_DIM ** -0.5
    lane_head = lax.broadcasted_iota(I32, (1, MEM_W), 1) // HEAD_DIM
    q = q_ref[0] * scale
    mk = kv_ref[0, :, :MEM_W].astype(BF16)
    mv = kv_ref[0, :, MEM_W:].astype(BF16)
    out = jnp.zeros(q.shape, F32)
    for h in range(MEM_HEADS):
        hm = lane_head == h
        s = _dot_nt(jnp.where(hm, q, 0.0).astype(BF16), mk)
        s = s - jnp.max(s, axis=-1, keepdims=True)
        e = jnp.exp(s)
        p = e / jnp.sum(e, axis=-1, keepdims=True)
        out = jnp.where(hm, jnp.dot(p.astype(BF16), mv, preferred_element_type=F32), out)
    o_ref[0] = out


def _mem_attention(p3, mkv3, tq):
    b, s, _ = p3.shape
    m = mkv3.shape[1]
    off = 0
    return pl.pallas_call(
        _mem_attn_kernel,
        grid=(b, s // tq),
        in_specs=[pl.BlockSpec((1, tq, MEM_W), lambda bi, i: (bi, i, off)),
                  pl.BlockSpec((1, m, 2 * MEM_W), lambda bi, i: (bi, 0, 0))],
        out_specs=pl.BlockSpec((1, tq, MEM_W), lambda bi, i: (bi, i, 0)),
        out_shape=jax.ShapeDtypeStruct((b, s, MEM_W), F32),
        compiler_params=_cparams(("parallel", "parallel")),
        name="mem_attn",
    )(p3, mkv3)


def _out_router_kernel(x_ref, ym_ref, yr_ref, ye_ref, wo1_ref, wo2_ref, wo3_ref, g_ref,
                       wrh_ref, wrl_ref, br_ref, upper_ref, ones_ref,
                       x1_ref, h_ref, idx_ref, rank_ref, wgt_ref, cnt_ref, run_s):
    @pl.when(pl.program_id(0) == 0)
    def _():
        run_s[...] = jnp.zeros_like(run_s)

    x1 = (x_ref[...] + _bdot(ym_ref[...], wo1_ref[...]) + _bdot(yr_ref[...], wo2_ref[...])
          + _bdot(ye_ref[...], wo3_ref[...]))
    x1_ref[...] = x1
    h = _rms(x1, g_ref[...])
    h_ref[...] = _pack_halves(h)
    tm = h.shape[0]
    h_hi, h_lo = _split_bf16(h, 2)
    dot = functools.partial(jnp.dot, preferred_element_type=F32)
    logits = dot(h_hi, wrh_ref[...]) + (dot(h_hi, wrl_ref[...]) + dot(h_lo, wrh_ref[...])) + br_ref[...]
    lg = logits.T[:N_EXPERTS, :]
    e_iota = lax.broadcasted_iota(I32, (N_EXPERTS, tm), 0)

    vals, idxs = [], []
    for _ in range(TOP_K):
        m = jnp.max(lg, axis=0, keepdims=True)
        idx = jnp.min(jnp.where(lg == m, e_iota, N_EXPERTS), axis=0, keepdims=True)
        vals.append(m)
        idxs.append(idx)
        lg = jnp.where(e_iota == idx, NEG_INF, lg)
    exps = [jnp.exp(vk - vals[0]) for vk in vals]
    denom = exps[0] + exps[1] + exps[2] + exps[3]

    chosen = jnp.zeros((N_EXPERTS, tm), F32)
    for idx in idxs:
        chosen = chosen + jnp.where(e_iota == idx, 1.0, 0.0)
    chosen = chosen.astype(BF16)
    run = run_s[...]
    before = dot(chosen, upper_ref[...]) + jnp.concatenate([run] * (tm // LANES), axis=1)
    run_s[...] = run + dot(chosen, ones_ref[...])
    cnt_ref[...] = run_s[...]

    zero_i = jnp.zeros((SUBLANES - TOP_K, tm), I32)
    ranks = [jnp.sum(jnp.where(e_iota == idx, before, 0.0), axis=0, keepdims=True).astype(I32)
             for idx in idxs]
    idx_ref[0] = jnp.concatenate(idxs + [zero_i], axis=0)
    rank_ref[0] = jnp.concatenate(ranks + [zero_i], axis=0)
    wrows = jnp.concatenate([e / denom for e in exps] + [jnp.zeros((LANES - TOP_K, tm), F32)], axis=0)
    wgt_ref[...] = wrows.T


def _out_router(x2, ym, yr, ye, w_out, g_ffn, w_router, b_router, tm):
    t, d = x2.shape
    wo = w_out.astype(BF16)
    wo1, wo2, wo3 = wo[:MOBA_W], wo[MOBA_W:MOBA_W + RWKV_W], wo[MOBA_W + RWKV_W:]
    wr = jnp.zeros((d, LANES), F32).at[:, :N_EXPERTS].set(w_router)
    wr_hi = wr.astype(BF16)
    wr_lo = (wr - wr_hi.astype(F32)).astype(BF16)
    br = jnp.full((1, LANES), NEG_INF, F32).at[0, :N_EXPERTS].set(b_router)
    upper = jnp.asarray(np.triu(np.ones((tm, tm), np.float32), 1)).astype(BF16)
    ones = jnp.ones((tm, LANES), BF16)
    tile = lambda n: pl.BlockSpec((tm, n), lambda i: (i, 0))
    slots = pl.BlockSpec((1, SUBLANES, tm), lambda i: (i, 0, 0))
    const = lambda a: pl.BlockSpec(a.shape, lambda i: (0,) * a.ndim)
    g2 = g_ffn.reshape(1, d)
    n = t // tm
    return pl.pallas_call(
        _out_router_kernel,
        grid=(n,),
        in_specs=[tile(d), tile(MOBA_W), tile(RWKV_W), tile(MEM_W), const(wo1), const(wo2), const(wo3),
                  const(g2), const(wr_hi), const(wr_lo), const(br), const(upper), const(ones)],
        out_specs=[tile(d), tile(d // 2), slots, slots, tile(LANES),
                   pl.BlockSpec((N_EXPERTS, LANES), lambda i: (0, 0))],
        out_shape=[jax.ShapeDtypeStruct((t, d), F32), jax.ShapeDtypeStruct((t, d // 2), I32),
                   jax.ShapeDtypeStruct((n, SUBLANES, tm), I32), jax.ShapeDtypeStruct((n, SUBLANES, tm), I32),
                   jax.ShapeDtypeStruct((t, LANES), F32), jax.ShapeDtypeStruct((N_EXPERTS, LANES), F32)],
        scratch_shapes=[pltpu.VMEM((N_EXPERTS, LANES), F32)],
        compiler_params=_cparams(("arbitrary",)),
        name="out_router",
    )(x2, ym, yr, ye, wo1, wo2, wo3, g2, wr_hi, wr_lo, br, upper, ones)


def _sc_mesh():
    return plsc.VectorSubcoreMesh(core_axis_name="c", subcore_axis_name="s",
                                  num_cores=SC_CORES, num_subcores=SC_SUBCORES)


def _sc_split(n_rows):
    n_workers = SC_CORES * SC_SUBCORES
    per_worker = n_rows // n_workers
    assert per_worker * n_workers == n_rows and per_worker % (SC_ROWS * SC_INFLIGHT) == 0
    assert SC_ROWS % SC_ALIGN == 0
    return per_worker


def _sc_gather_rows(table, idx):
    n_rows = idx.shape[0]
    width = table.shape[1]
    per_worker = _sc_split(n_rows)

    buf = lambda shape, dtype: [pltpu.VMEM(shape, dtype) for _ in range(SC_INFLIGHT)]

    @functools.partial(
        pl.kernel, mesh=_sc_mesh(),
        out_type=jax.ShapeDtypeStruct((n_rows, width), table.dtype),
        scratch_types=[buf((SC_ROWS,), I32), buf((SC_ROWS, width), table.dtype),
                       [pltpu.SemaphoreType.DMA for _ in range(SC_INFLIGHT)]],
        name="sc_gather",
    )
    def gather(table_hbm, idx_hbm, out_hbm, idx_v, rows_v, sems):
        worker = lax.axis_index("s") * SC_CORES + lax.axis_index("c")
        base = worker * per_worker

        @pl.loop(0, per_worker // (SC_ROWS * SC_INFLIGHT))
        def _(step):
            offs = [pl.multiple_of(base + (step * SC_INFLIGHT + b) * SC_ROWS, SC_ALIGN) for b in range(SC_INFLIGHT)]
            copies = []
            for b in range(SC_INFLIGHT):
                pltpu.sync_copy(idx_hbm.at[pl.ds(offs[b], SC_ROWS)], idx_v[b])
                copies.append(pltpu.async_copy(table_hbm.at[idx_v[b]], rows_v[b], sems[b]))
            for b in range(SC_INFLIGHT):
                copies[b].wait()
                pltpu.sync_copy(rows_v[b], out_hbm.at[pl.ds(offs[b], SC_ROWS)])

    return gather(table, idx)


def _sc_scatter_rows(rows, idx, n_out):
    n_src, width = rows.shape
    n_copies = idx.shape[0] // n_src
    assert n_copies * n_src == idx.shape[0]
    per_worker = _sc_split(n_src)

    @functools.partial(
        pl.kernel, mesh=_sc_mesh(),
        out_type=jax.ShapeDtypeStruct((n_out, width), rows.dtype),
        scratch_types=[[pltpu.VMEM((SC_ROWS,), I32) for _ in range(n_copies)],
                       pltpu.VMEM((SC_ROWS, width), rows.dtype),
                       [pltpu.SemaphoreType.DMA for _ in range(n_copies)]],
        name="sc_scatter",
    )
    def scatter(rows_hbm, idx_hbm, out_hbm, idx_v, rows_v, sems):
        worker = lax.axis_index("s") * SC_CORES + lax.axis_index("c")
        base = worker * per_worker

        @pl.loop(0, per_worker // SC_ROWS)
        def _(step):
            src = pl.multiple_of(base + step * SC_ROWS, SC_ALIGN)
            pltpu.sync_copy(rows_hbm.at[pl.ds(src, SC_ROWS)], rows_v)
            copies = []
            for cpy in range(n_copies):
                off = pl.multiple_of(cpy * n_src + src, SC_ALIGN)
                pltpu.sync_copy(idx_hbm.at[pl.ds(off, SC_ROWS)], idx_v[cpy])
                copies.append(pltpu.async_copy(rows_v, out_hbm.at[idx_v[cpy]], sems[cpy]))
            for cp in copies:
                cp.wait()

    return scatter(rows, idx)


def _experts_kernel(be_ref, bv_ref, nused_ref, next_ref, slot_ref,
                    xs_ref, wgu_hbm, bg_ref, bu_ref, wd_hbm, bd_ref, perm_ref,
                    o_ref, wgu_buf, wd_buf, wg_s, wu_s, wd_s, sem):
    i = pl.program_id(0)
    e = be_ref[i]
    used = i < nused_ref[0]
    changed = ((i == 0) | (e != be_ref[jnp.maximum(i - 1, 0)])) & used
    slot = slot_ref[e]

    def weight_copies(expert, buf_slot):
        return (pltpu.make_async_copy(wgu_hbm.at[expert], wgu_buf.at[buf_slot], sem.at[buf_slot, 0]),
                pltpu.make_async_copy(wd_hbm.at[expert], wd_buf.at[buf_slot], sem.at[buf_slot, 1]))

    @pl.when((i == 0) & used)
    def _():
        for cp in weight_copies(e, slot):
            cp.start()

    @pl.when(changed)
    def _():
        for cp in weight_copies(e, slot):
            cp.wait()
        nxt = next_ref[e]

        @pl.when(nxt >= 0)
        def _():
            for cp in weight_copies(nxt, 1 - slot):
                cp.start()

        half = LANES
        for cblk in range(2 * D_EXPERT // (2 * half)):
            wt = wgu_buf[slot, :, cblk * 2 * half:(cblk + 1) * 2 * half].astype(BF16)
            sep = jnp.dot(wt, perm_ref[...], preferred_element_type=F32).astype(BF16)
            wg_s[:, cblk * half:(cblk + 1) * half] = sep[:, :half]
            wu_s[:, cblk * half:(cblk + 1) * half] = sep[:, half:]
        wd_s[...] = wd_buf[slot].astype(BF16)

    @pl.when(used)
    def _():
        rows = lax.broadcasted_iota(I32, xs_ref.shape, 0)
        xb = _unpack_halves(jnp.where(rows < bv_ref[i], xs_ref[...], 0)).astype(BF16)
        gate = jnp.dot(xb, wg_s[...], preferred_element_type=F32) + bg_ref[0]
        up = jnp.dot(xb, wu_s[...], preferred_element_type=F32) + bu_ref[0]
        gate = jnp.minimum(gate, SWIGLU_LIMIT)
        up = jnp.clip(up, -SWIGLU_LIMIT, SWIGLU_LIMIT)
        glu = gate * _sigmoid(gate * SWIGLU_ALPHA)
        act = ((up + 1.0) * glu).astype(BF16)
        o_ref[...] = _pack_halves(jnp.dot(act, wd_s[...], preferred_element_type=F32) + bd_ref[0])

    @pl.when(jnp.logical_not(used))
    def _():
        o_ref[...] = jnp.zeros_like(o_ref)


def _experts(blk_expert, blk_valid, n_used, next_expert, buf_slot, xs, w_gate_up, b_gate_up, w_down,
             b_down):
    p_rows = xs.shape[0]
    d = 2 * xs.shape[1]
    n_blocks = p_rows // EXPERT_ROWS
    bg = b_gate_up[:, 0::2].reshape(N_EXPERTS, 1, D_EXPERT)
    bu = b_gate_up[:, 1::2].reshape(N_EXPERTS, 1, D_EXPERT)
    bd = b_down.reshape(N_EXPERTS, 1, d)
    perm_np = np.zeros((2 * LANES, 2 * LANES), np.float32)
    perm_np[2 * np.arange(LANES), np.arange(LANES)] = 1.0
    perm_np[2 * np.arange(LANES) + 1, LANES + np.arange(LANES)] = 1.0
    perm = jnp.asarray(perm_np).astype(BF16)
    by_expert = lambda shape: pl.BlockSpec((1,) + shape, lambda i, be, *_: (be[i], 0, 0))
    grid_spec = pltpu.PrefetchScalarGridSpec(
        num_scalar_prefetch=5,
        grid=(n_blocks,),
        in_specs=[pl.BlockSpec((EXPERT_ROWS, d // 2), lambda i, *_: (i, 0)),
                  pl.BlockSpec(memory_space=pl.ANY), by_expert((1, D_EXPERT)), by_expert((1, D_EXPERT)),
                  pl.BlockSpec(memory_space=pl.ANY), by_expert((1, d)),
                  pl.BlockSpec(perm.shape, lambda i, *_: (0, 0))],
        out_specs=pl.BlockSpec((EXPERT_ROWS, d // 2), lambda i, *_: (i, 0)),
        scratch_shapes=[pltpu.VMEM((2, d, 2 * D_EXPERT), F32), pltpu.VMEM((2, D_EXPERT, d), F32),
                        pltpu.VMEM((d, D_EXPERT), BF16), pltpu.VMEM((d, D_EXPERT), BF16),
                        pltpu.VMEM((D_EXPERT, d), BF16), pltpu.SemaphoreType.DMA((2, 2))],
    )
    return pl.pallas_call(
        _experts_kernel,
        grid_spec=grid_spec,
        out_shape=jax.ShapeDtypeStruct((p_rows, d // 2), I32),
        compiler_params=_cparams(("arbitrary",)),
        name="experts",
    )(blk_expert, blk_valid, n_used, next_expert, buf_slot, xs, w_gate_up, bg, bu, w_down, bd, perm)


def _combine_kernel(yg_ref, x1_ref, w_ref, g_ref, o_ref):
    acc = x1_ref[...]
    wts = w_ref[...]
    for kk in range(TOP_K):
        acc = acc + _unpack_halves(yg_ref[kk]) * wts[:, kk:kk + 1]
    o_ref[...] = _rms(acc, g_ref[...])


def _combine(yg, stream, wts, g_final, tb, first_tile):
    t, d = stream.shape
    half = yg.shape[2]
    tile = lambda n: pl.BlockSpec((tb, n), lambda i: (i + first_tile, 0))
    return pl.pallas_call(
        _combine_kernel,
        grid=(yg.shape[1] // tb,),
        in_specs=[pl.BlockSpec((TOP_K, tb, half), lambda i: (0, i, 0)), tile(d), tile(LANES),
                  pl.BlockSpec((1, d), lambda i: (0, 0))],
        out_specs=tile(d),
        out_shape=jax.ShapeDtypeStruct((t, d), F32),
        input_output_aliases={1: 0},
        compiler_params=_cparams(("parallel",)),
        name="combine",
    )(yg, stream, wts, g_final.reshape(1, d))


def _layer(x, mem, w_in, w_out, w_mem_kv, g_mix, g_mem, g_ffn, bias_tables, mu, w0, w_up, a0, a_up,
           g_up, k_k, k_a, r_k, gn_g, gn_b, w_router, b_router, w_gate_up, b_gate_up, w_down, b_down,
           g_last):
    b, s, d = x.shape
    m = mem.shape[1]
    t = b * s
    x2 = x.reshape(t, d)

    p_moba, p_rwkv, p_memq = _in_proj(x2, g_mix, w_in.astype(BF16), PROJ_TOKENS)
    y_moba = _moba_attention(p_moba.reshape(-1, s, LANES), *bias_tables, b)
    y_rwkv = _rwkv(p_rwkv.reshape(b, s, RWKV_COLS), mu, w0, w_up, a0, a_up, g_up, k_k, k_a, r_k, gn_g, gn_b)
    mkv = _norm_matmul(mem.reshape(b * m, d), g_mem, w_mem_kv.astype(BF16), PROJ_TOKENS, "mem_kv")
    y_mem = _mem_attention(p_memq.reshape(b, s, MEM_W), mkv.reshape(b, m, 2 * MEM_W), MEM_Q_TOKENS)

    tb = ROUTE_TOKENS
    x1, h2, idx_o, rank_o, wgt_p, cnt = _out_router(
        x2, y_moba.reshape(t, MOBA_W), y_rwkv.reshape(t, RWKV_W), y_mem.reshape(t, MEM_W),
        w_out, g_ffn, w_router, b_router, tb)

    counts = cnt[:, 0].astype(I32)
    padded = (counts + EXPERT_ROWS - 1) // EXPERT_ROWS * EXPERT_ROWS
    pad_ends = jnp.cumsum(padded)
    pad_starts = (pad_ends - padded).astype(I32)
    n_blocks = (t * TOP_K) // EXPERT_ROWS + N_EXPERTS
    blk_start = jnp.arange(n_blocks, dtype=I32) * EXPERT_ROWS
    blk_expert = jnp.minimum(jnp.sum(blk_start[:, None] >= pad_ends[None, :], axis=1),
                             N_EXPERTS - 1).astype(I32)
    of_block = blk_expert[:, None] == jnp.arange(N_EXPERTS, dtype=I32)[None, :]
    rows_left = jnp.sum(jnp.where(of_block, (counts + pad_starts)[None, :], 0), axis=1) - blk_start
    blk_valid = jnp.clip(rows_left, 0, EXPERT_ROWS)
    n_used = (pad_ends[-1:] // EXPERT_ROWS).astype(I32)
    has_rows = counts > 0
    first_from = lax.cummin(jnp.where(has_rows, jnp.arange(N_EXPERTS, dtype=I32), N_EXPERTS), reverse=True)
    next_expert = jnp.concatenate([first_from[1:], jnp.full((1,), N_EXPERTS, I32)])
    next_expert = jnp.where(next_expert < N_EXPERTS, next_expert, -1).astype(I32)
    buf_slot = ((jnp.cumsum(has_rows.astype(I32)) - has_rows.astype(I32)) % 2).astype(I32)
    idx_kt = jnp.swapaxes(idx_o[:, :TOP_K, :], 0, 1).reshape(TOP_K, t)
    rank_kt = jnp.swapaxes(rank_o[:, :TOP_K, :], 0, 1).reshape(TOP_K, t)
    experts = jnp.arange(N_EXPERTS, dtype=I32)[:, None, None]
    start_kt = jnp.sum(jnp.where(idx_kt[None] == experts, pad_starts[:, None, None], 0), axis=0)
    dest = (start_kt + rank_kt).reshape(TOP_K * t)

    xs = _sc_scatter_rows(h2, dest, n_blocks * EXPERT_ROWS)
    ys = _experts(blk_expert, blk_valid.astype(I32), n_used, next_expert, buf_slot, xs,
                  w_gate_up, b_gate_up, w_down, b_down)
    dest_kt = dest.reshape(TOP_K, t)
    out = x1
    for part in range(COMBINE_PARTS):
        t_part = t // COMBINE_PARTS
        rows = dest_kt[:, part * t_part:(part + 1) * t_part].reshape(TOP_K * t_part)
        yg = _sc_gather_rows(ys, rows).reshape(TOP_K, t_part, d // 2)
        out = _combine(yg, out, wgt_p, g_last, tb, part * (t_part // tb))
    return out.reshape(b, s, d)


def kernel(x, mem, w_in, w_out, w_mem_kv, g_mix, g_mem, g_ffn, g_final, rel_bias, rwkv_mu, rwkv_w0,
           rwkv_w_up, rwkv_a0, rwkv_a_up, rwkv_g_up, rwkv_k_k, rwkv_k_a, rwkv_r_k, rwkv_gn_g, rwkv_gn_b,
           w_router, b_router, w_gate_up, b_gate_up, w_down, b_down):
    depth = w_in.shape[0]
    assert depth == 1, "the final norm is fused into the last layer's combine kernel"
    bias_tables = _moba_bias_tables(rel_bias)
    l = 0
    return _layer(x, mem, w_in[l], w_out[l], w_mem_kv[l], g_mix[l], g_mem[l], g_ffn[l], bias_tables,
                  rwkv_mu[l], rwkv_w0[l], rwkv_w_up[l], rwkv_a0[l], rwkv_a_up[l], rwkv_g_up[l],
                  rwkv_k_k[l], rwkv_k_a[l], rwkv_r_k[l], rwkv_gn_g[l], rwkv_gn_b[l], w_router[l],
                  b_router[l], w_gate_up[l], b_gate_up[l], w_down[l], b_down[l], g_final)
```
